```python
import math
import jax
import jax.numpy as jnp
from jax import lax
import numpy as np


D_MODEL = 1024
BATCH = 8
SEQ = 16384
DEPTH = 4

D_MIX = D_MODEL
HEAD_DIM = 64
N_Q_HEADS = (D_MIX // 2) // HEAD_DIM
N_KV_HEADS = 2
Q_PER_KV = N_Q_HEADS // N_KV_HEADS
D_ATTN = N_Q_HEADS * HEAD_DIM
D_KV = N_KV_HEADS * HEAD_DIM
WINDOW = 128
ATTN_BLOCK = 128
ROPE_THETA = 10000.0
D_S5 = D_MIX // 4
S5_GROUP = 16
S5_GROUPS = D_S5 // S5_GROUP
S5_STATE = 64
D_LRU = D_MIX - D_ATTN - D_S5
LRU_HEADS = 4
LRU_HEAD_DIM = D_LRU // LRU_HEADS
LRU_CONV = 4
LRU_C = 8.0
D_IN = D_ATTN + 2 * D_KV + D_S5 + 2 * D_LRU
SPLITS = (D_ATTN, D_ATTN + D_KV, D_ATTN + 2 * D_KV, D_ATTN + 2 * D_KV + D_S5, D_ATTN + 2 * D_KV + D_S5 + D_LRU)
D_FF = ((8 * D_MODEL // 3 + 127) // 128) * 128
FFN_CONV = 3
ALPHA = (2 * DEPTH) ** 0.25
BETA = (8 * DEPTH) ** -0.25
LN_EPS = 1e-5
RMS_EPS = 1e-6

kernel_name = 'hymba_swa_s5_rglru_deepnorm_trunk'


def layer_norm(x, g, b):
    xf = x.astype(jnp.float32)
    mu = jnp.mean(xf, axis=-1, keepdims=True)
    xc = xf - mu
    var = jnp.mean(jnp.square(xc), axis=-1, keepdims=True)
    y = xc * lax.rsqrt(var + LN_EPS) * g.astype(jnp.float32) + b.astype(jnp.float32)
    return y.astype(x.dtype)


def group_rmsnorm(parts, g):
    normed = [p.astype(jnp.float32) * lax.rsqrt(jnp.mean(jnp.square(p.astype(jnp.float32)), axis=-1, keepdims=True) + RMS_EPS) for p in parts]
    return (jnp.concatenate(normed, axis=-1) * g.astype(jnp.float32)).astype(parts[0].dtype)


def causal_dwconv(x, w):
    K = w.shape[0]
    L = x.shape[1]
    xp = jnp.pad(x, ((0, 0), (K - 1, 0), (0, 0)))
    y = xp[:, 0:L] * w[0]
    for k in range(1, K):
        y = y + xp[:, k:k + L] * w[k]
    return y


def rope_tables(L, dtype):
    inv_freq = ROPE_THETA ** (-jnp.arange(0, HEAD_DIM, 2, dtype=jnp.float32) / HEAD_DIM)
    ang = jnp.arange(L, dtype=jnp.float32)[:, None] * inv_freq[None, :]
    return jnp.cos(ang).astype(dtype)[None, :, None, :], jnp.sin(ang).astype(dtype)[None, :, None, :]


def apply_rope(t, cos, sin):
    t1, t2 = jnp.split(t, 2, axis=-1)
    return jnp.concatenate([t1 * cos - t2 * sin, t2 * cos + t1 * sin], axis=-1)


def sliding_window_attention(q, k, v, sinks):
    Bsz, L, _, _ = q.shape
    nb = L // ATTN_BLOCK
    qb = q.reshape(Bsz, nb, ATTN_BLOCK, N_KV_HEADS, Q_PER_KV, HEAD_DIM)

    def band(t):
        tp = jnp.pad(t, ((0, 0), (ATTN_BLOCK, 0), (0, 0), (0, 0))).reshape(Bsz, nb + 1, ATTN_BLOCK, N_KV_HEADS, HEAD_DIM)
        return jnp.concatenate([tp[:, :-1], tp[:, 1:]], axis=2)

    kb, vb = band(k), band(v)
    scores = jnp.einsum('bnqkgd,bnskd->bnkgqs', qb, kb).astype(jnp.float32) * (HEAD_DIM ** -0.5)
    qi = jnp.arange(ATTN_BLOCK)[:, None]
    si = jnp.arange(2 * ATTN_BLOCK)[None, :]
    diff = qi + ATTN_BLOCK - si
    blk = jnp.arange(nb)[:, None, None]
    valid = (diff >= 0) & (diff < WINDOW) & (blk * ATTN_BLOCK + si[None] - ATTN_BLOCK >= 0)
    scores = jnp.where(valid[None, :, None, None], scores, -jnp.inf)
    sink = sinks.astype(jnp.float32).reshape(N_KV_HEADS, Q_PER_KV)[None, None, :, :, None, None]
    m = jnp.maximum(jnp.max(scores, axis=-1, keepdims=True), sink)
    p = jnp.exp(scores - m)
    denom = jnp.sum(p, axis=-1, keepdims=True) + jnp.exp(sink - m)
    p = (p / denom).astype(v.dtype)
    out = jnp.einsum('bnkgqs,bnskd->bnqkgd', p, vb)
    return out.reshape(Bsz, L, D_ATTN)


def _complex_affine_combine(e1, e2):
    ar1, ai1, br1, bi1 = e1
    ar2, ai2, br2, bi2 = e2
    ar = ar2 * ar1 - ai2 * ai1
    ai = ar2 * ai1 + ai2 * ar1
    br = ar2 * br1 - ai2 * bi1 + br2
    bi = ar2 * bi1 + ai2 * br1 + bi2
    return (ar, ai, br, bi)


def _real_affine_combine(e1, e2):
    a1, b1 = e1
    a2, b2 = e2
    return (a1 * a2, a2 * b1 + b2)


def s5_mixer(u, a_re, a_im, b_re, b_im, c_re, c_im, d, log_dt, glu_w, glu_b):
    f32 = jnp.float32
    Bsz, L, _ = u.shape
    uf = u.astype(f32).reshape(Bsz, L, S5_GROUPS, S5_GROUP)
    lam_re = jnp.minimum(a_re.astype(f32), -1e-4)
    lam_im = a_im.astype(f32)
    dt = jnp.exp(log_dt.astype(f32))[:, None]
    decay = jnp.exp(dt * lam_re)
    ang = dt * lam_im
    abar_re = decay * jnp.cos(ang)
    abar_im = decay * jnp.sin(ang)
    den = jnp.square(lam_re) + jnp.square(lam_im)
    nr = abar_re - 1.0
    ni = abar_im
    coef_re = (nr * lam_re + ni * lam_im) / den
    coef_im = (ni * lam_re - nr * lam_im) / den
    br = b_re.astype(f32)
    bi = b_im.astype(f32)
    bbar_re = coef_re[..., None] * br - coef_im[..., None] * bi
    bbar_im = coef_re[..., None] * bi + coef_im[..., None] * br
    bu_re = jnp.einsum('blgc,gpc->blgp', uf, bbar_re)
    bu_im = jnp.einsum('blgc,gpc->blgp', uf, bbar_im)
    shape = bu_re.shape
    elems = (jnp.broadcast_to(abar_re, shape), jnp.broadcast_to(abar_im, shape), bu_re, bu_im)
    _, _, h_re, h_im = lax.associative_scan(_complex_affine_combine, elems, axis=1)
    y = (jnp.einsum('blgp,gcp->blgc', h_re, c_re.astype(f32))
         - jnp.einsum('blgp,gcp->blgc', h_im, c_im.astype(f32))
         + d.astype(f32).reshape(S5_GROUPS, S5_GROUP) * uf)
    y = jax.nn.gelu(y.reshape(Bsz, L, D_S5))
    y = y * jax.nn.sigmoid(y @ glu_w.astype(f32) + glu_b.astype(f32))
    return y.astype(u.dtype)


def rg_lru_mixer(xr, gate, conv_w, conv_b, wx, bx, wa, ba, a_param):
    f32 = jnp.float32
    Bsz, L, _ = xr.shape
    xc = causal_dwconv(xr, conv_w) + conv_b
    xh = xc.reshape(Bsz, L, LRU_HEADS, LRU_HEAD_DIM)
    gx = jax.nn.sigmoid(jnp.einsum('blhi,hij->blhj', xh, wx).reshape(Bsz, L, D_LRU) + bx)
    ga = jax.nn.sigmoid(jnp.einsum('blhi,hij->blhj', xh, wa).reshape(Bsz, L, D_LRU) + ba)
    log_a = -LRU_C * ga.astype(f32) * jax.nn.softplus(-a_param.astype(f32))
    a = jnp.exp(log_a)
    mult = jnp.sqrt(-jnp.expm1(2.0 * log_a))
    mult = jnp.where((jnp.arange(L) == 0)[None, :, None], 1.0, mult)
    b = mult * gx.astype(f32) * xc.astype(f32)
    _, h = lax.associative_scan(_real_affine_combine, (a, b), axis=1)
    return h.astype(xr.dtype) * jax.nn.gelu(gate)


def conv_gated_mlp(x, w_gate, w_up, conv_w, conv_b, w_down):
    g = causal_dwconv(x @ w_gate, conv_w) + conv_b
    return (jax.nn.silu(g) * (x @ w_up)) @ w_down


def _fwd_setup_inputs(seed: int = 0) -> dict:
    key = jax.random.key(seed)
    ks = jax.random.split(key, 40)
    f32 = jnp.float32

    def nrm(k, shape, scale):
        return scale * jax.random.normal(k, shape, f32)

    n_idx = jnp.arange(S5_STATE, dtype=f32)
    a0 = jax.random.uniform(ks[20], (DEPTH, D_LRU), f32, 0.9, 0.999)
    return {
        'x': nrm(ks[0], (BATCH, SEQ, D_MODEL), 1.0),
        'w_in': nrm(ks[1], (DEPTH, D_MODEL, D_IN), D_MODEL ** -0.5),
        'b_in': nrm(ks[2], (DEPTH, D_IN), 0.01),
        'attn_sinks': nrm(ks[3], (DEPTH, N_Q_HEADS), 0.5),
        's5_a_re': -0.5 + nrm(ks[4], (DEPTH, S5_GROUPS, S5_STATE), 0.01),
        's5_a_im': jnp.pi * n_idx + nrm(ks[5], (DEPTH, S5_GROUPS, S5_STATE), 0.01),
        's5_b_re': nrm(ks[6], (DEPTH, S5_GROUPS, S5_STATE, S5_GROUP), (2 * S5_GROUP) ** -0.5),
        's5_b_im': nrm(ks[7], (DEPTH, S5_GROUPS, S5_STATE, S5_GROUP), (2 * S5_GROUP) ** -0.5),
        's5_c_re': nrm(ks[8], (DEPTH, S5_GROUPS, S5_GROUP, S5_STATE), (2 * S5_STATE) ** -0.5),
        's5_c_im': nrm(ks[9], (DEPTH, S5_GROUPS, S5_GROUP, S5_STATE), (2 * S5_STATE) ** -0.5),
        's5_d': nrm(ks[10], (DEPTH, D_S5), 1.0),
        's5_log_dt': jax.random.uniform(ks[11], (DEPTH, S5_GROUPS), f32, math.log(1e-3), math.log(1e-1)),
        's5_glu_w': nrm(ks[12], (DEPTH, D_S5, D_S5), D_S5 ** -0.5),
        's5_glu_b': nrm(ks[13], (DEPTH, D_S5), 0.01),
        'lru_conv_w': nrm(ks[14], (DEPTH, LRU_CONV, D_LRU), LRU_CONV ** -0.5),
        'lru_conv_b': nrm(ks[15], (DEPTH, D_LRU), 0.01),
        'lru_wx': nrm(ks[16], (DEPTH, LRU_HEADS, LRU_HEAD_DIM, LRU_HEAD_DIM), LRU_HEAD_DIM ** -0.5),
        'lru_bx': nrm(ks[17], (DEPTH, D_LRU), 0.01),
        'lru_wa': nrm(ks[18], (DEPTH, LRU_HEADS, LRU_HEAD_DIM, LRU_HEAD_DIM), LRU_HEAD_DIM ** -0.5),
        'lru_ba': nrm(ks[19], (DEPTH, D_LRU), 0.01),
        'lru_a_param': jnp.log(a0) - jnp.log1p(-a0),
        'mix_norm_g': 1.0 + nrm(ks[21], (DEPTH, D_MIX), 0.01),
        'w_out': nrm(ks[22], (DEPTH, D_MIX, D_MODEL), BETA * D_MIX ** -0.5),
        'b_out': nrm(ks[23], (DEPTH, D_MODEL), 0.01),
        'ln1_g': 1.0 + nrm(ks[24], (DEPTH, D_MODEL), 0.01),
        'ln1_b': nrm(ks[25], (DEPTH, D_MODEL), 0.01),
        'ffn_w_gate': nrm(ks[26], (DEPTH, D_MODEL, D_FF), D_MODEL ** -0.5),
        'ffn_w_up': nrm(ks[27], (DEPTH, D_MODEL, D_FF), D_MODEL ** -0.5),
        'ffn_conv_w': nrm(ks[28], (DEPTH, FFN_CONV, D_FF), FFN_CONV ** -0.5),
        'ffn_conv_b': nrm(ks[29], (DEPTH, D_FF), 0.01),
        'ffn_w_down': nrm(ks[30], (DEPTH, D_FF, D_MODEL), BETA * D_FF ** -0.5),
        'ln2_g': 1.0 + nrm(ks[31], (DEPTH, D_MODEL), 0.01),
        'ln2_b': nrm(ks[32], (DEPTH, D_MODEL), 0.01),
    }


def _fwd_reference(x, w_in, b_in, attn_sinks, s5_a_re, s5_a_im, s5_b_re, s5_b_im, s5_c_re, s5_c_im,
              s5_d, s5_log_dt, s5_glu_w, s5_glu_b, lru_conv_w, lru_conv_b, lru_wx, lru_bx, lru_wa,
              lru_ba, lru_a_param, mix_norm_g, w_out, b_out, ln1_g, ln1_b, ffn_w_gate, ffn_w_up,
              ffn_conv_w, ffn_conv_b, ffn_w_down, ln2_g, ln2_b):
    Bsz, L, _ = x.shape
    cos, sin = rope_tables(L, x.dtype)
    for l in range(DEPTH):
        proj = x @ w_in[l] + b_in[l]
        q, k, v, u, xr, gate = jnp.split(proj, SPLITS, axis=-1)
        q = apply_rope(q.reshape(Bsz, L, N_Q_HEADS, HEAD_DIM), cos, sin)
        k = apply_rope(k.reshape(Bsz, L, N_KV_HEADS, HEAD_DIM), cos, sin)
        v = v.reshape(Bsz, L, N_KV_HEADS, HEAD_DIM)
        y_attn = sliding_window_attention(q, k, v, attn_sinks[l])
        y_s5 = s5_mixer(u, s5_a_re[l], s5_a_im[l], s5_b_re[l], s5_b_im[l], s5_c_re[l], s5_c_im[l],
                        s5_d[l], s5_log_dt[l], s5_glu_w[l], s5_glu_b[l])
        y_lru = rg_lru_mixer(xr, gate, lru_conv_w[l], lru_conv_b[l], lru_wx[l], lru_bx[l],
                             lru_wa[l], lru_ba[l], lru_a_param[l])
        mix = group_rmsnorm((y_attn, y_s5, y_lru), mix_norm_g[l])
        x = layer_norm(ALPHA * x + mix @ w_out[l] + b_out[l], ln1_g[l], ln1_b[l])
        f = conv_gated_mlp(x, ffn_w_gate[l], ffn_w_up[l], ffn_conv_w[l], ffn_conv_b[l], ffn_w_down[l])
        x = layer_norm(ALPHA * x + f, ln2_g[l], ln2_b[l])
    return x


import jax as _jax
import jax.numpy as _jnp

TWIN_FORMAT = 'train_step'
FWD_PARAMS = ['x', 'w_in', 'b_in', 'attn_sinks', 's5_a_re', 's5_a_im', 's5_b_re', 's5_b_im', 's5_c_re', 's5_c_im', 's5_d', 's5_log_dt', 's5_glu_w', 's5_glu_b', 'lru_conv_w', 'lru_conv_b', 'lru_wx', 'lru_bx', 'lru_wa', 'lru_ba', 'lru_a_param', 'mix_norm_g', 'w_out', 'b_out', 'ln1_g', 'ln1_b', 'ffn_w_gate', 'ffn_w_up', 'ffn_conv_w', 'ffn_conv_b', 'ffn_w_down', 'ln2_g', 'ln2_b']
TWIN_WEIGHTS = ['w_in', 'b_in', 'attn_sinks', 's5_a_re', 's5_a_im', 's5_b_re', 's5_b_im', 's5_c_re', 's5_c_im', 's5_d', 's5_log_dt', 's5_glu_w', 's5_glu_b', 'lru_conv_w', 'lru_conv_b', 'lru_wx', 'lru_bx', 'lru_wa', 'lru_ba', 'lru_a_param', 'mix_norm_g', 'w_out', 'b_out', 'ln1_g', 'ln1_b', 'ffn_w_gate', 'ffn_w_up', 'ffn_conv_w', 'ffn_conv_b', 'ffn_w_down', 'ln2_g', 'ln2_b']
TWIN_DIFF_INPUT = 'x'
TWIN_INPUTS = ['x', 'w_in', 'b_in', 'attn_sinks', 's5_a_re', 's5_a_im', 's5_b_re', 's5_b_im', 's5_c_re', 's5_c_im', 's5_d', 's5_log_dt', 's5_glu_w', 's5_glu_b', 'lru_conv_w', 'lru_conv_b', 'lru_wx', 'lru_bx', 'lru_wa', 'lru_ba', 'lru_a_param', 'mix_norm_g', 'w_out', 'b_out', 'ln1_g', 'ln1_b', 'ffn_w_gate', 'ffn_w_up', 'ffn_conv_w', 'ffn_conv_b', 'ffn_w_down', 'ln2_g', 'ln2_b', 'loss_target', 'm_w_in', 'm_b_in', 'm_attn_sinks', 'm_s5_a_re', 'm_s5_a_im', 'm_s5_b_re', 'm_s5_b_im', 'm_s5_c_re', 'm_s5_c_im', 'm_s5_d', 'm_s5_log_dt', 'm_s5_glu_w', 'm_s5_glu_b', 'm_lru_conv_w', 'm_lru_conv_b', 'm_lru_wx', 'm_lru_bx', 'm_lru_wa', 'm_lru_ba', 'm_lru_a_param', 'm_mix_norm_g', 'm_w_out', 'm_b_out', 'm_ln1_g', 'm_ln1_b', 'm_ffn_w_gate', 'm_ffn_w_up', 'm_ffn_conv_w', 'm_ffn_conv_b', 'm_ffn_w_down', 'm_ln2_g', 'm_ln2_b', 'v_w_in', 'v_b_in', 'v_attn_sinks', 'v_s5_a_re', 'v_s5_a_im', 'v_s5_b_re', 'v_s5_b_im', 'v_s5_c_re', 'v_s5_c_im', 'v_s5_d', 'v_s5_log_dt', 'v_s5_glu_w', 'v_s5_glu_b', 'v_lru_conv_w', 'v_lru_conv_b', 'v_lru_wx', 'v_lru_bx', 'v_lru_wa', 'v_lru_ba', 'v_lru_a_param', 'v_mix_norm_g', 'v_w_out', 'v_b_out', 'v_ln1_g', 'v_ln1_b', 'v_ffn_w_gate', 'v_ffn_w_up', 'v_ffn_conv_w', 'v_ffn_conv_b', 'v_ffn_w_down', 'v_ln2_g', 'v_ln2_b']
TWIN_OUTPUTS = ['loss', 'grad_x', 'grad_w_in', 'grad_b_in', 'grad_attn_sinks', 'grad_s5_a_re', 'grad_s5_a_im', 'grad_s5_b_re', 'grad_s5_b_im', 'grad_s5_c_re', 'grad_s5_c_im', 'grad_s5_d', 'grad_s5_log_dt', 'grad_s5_glu_w', 'grad_s5_glu_b', 'grad_lru_conv_w', 'grad_lru_conv_b', 'grad_lru_wx', 'grad_lru_bx', 'grad_lru_wa', 'grad_lru_ba', 'grad_lru_a_param', 'grad_mix_norm_g', 'grad_w_out', 'grad_b_out', 'grad_ln1_g', 'grad_ln1_b', 'grad_ffn_w_gate', 'grad_ffn_w_up', 'grad_ffn_conv_w', 'grad_ffn_conv_b', 'grad_ffn_w_down', 'grad_ln2_g', 'grad_ln2_b', 'delta_w_in', 'delta_b_in', 'delta_attn_sinks', 'delta_s5_a_re', 'delta_s5_a_im', 'delta_s5_b_re', 'delta_s5_b_im', 'delta_s5_c_re', 'delta_s5_c_im', 'delta_s5_d', 'delta_s5_log_dt', 'delta_s5_glu_w', 'delta_s5_glu_b', 'delta_lru_conv_w', 'delta_lru_conv_b', 'delta_lru_wx', 'delta_lru_bx', 'delta_lru_wa', 'delta_lru_ba', 'delta_lru_a_param', 'delta_mix_norm_g', 'delta_w_out', 'delta_b_out', 'delta_ln1_g', 'delta_ln1_b', 'delta_ffn_w_gate', 'delta_ffn_w_up', 'delta_ffn_conv_w', 'delta_ffn_conv_b', 'delta_ffn_w_down', 'delta_ln2_g', 'delta_ln2_b', 'new_m_w_in', 'new_m_b_in', 'new_m_attn_sinks', 'new_m_s5_a_re', 'new_m_s5_a_im', 'new_m_s5_b_re', 'new_m_s5_b_im', 'new_m_s5_c_re', 'new_m_s5_c_im', 'new_m_s5_d', 'new_m_s5_log_dt', 'new_m_s5_glu_w', 'new_m_s5_glu_b', 'new_m_lru_conv_w', 'new_m_lru_conv_b', 'new_m_lru_wx', 'new_m_lru_bx', 'new_m_lru_wa', 'new_m_lru_ba', 'new_m_lru_a_param', 'new_m_mix_norm_g', 'new_m_w_out', 'new_m_b_out', 'new_m_ln1_g', 'new_m_ln1_b', 'new_m_ffn_w_gate', 'new_m_ffn_w_up', 'new_m_ffn_conv_w', 'new_m_ffn_conv_b', 'new_m_ffn_w_down', 'new_m_ln2_g', 'new_m_ln2_b', 'new_v_w_in', 'new_v_b_in', 'new_v_attn_sinks', 'new_v_s5_a_re', 'new_v_s5_a_im', 'new_v_s5_b_re', 'new_v_s5_b_im', 'new_v_s5_c_re', 'new_v_s5_c_im', 'new_v_s5_d', 'new_v_s5_log_dt', 'new_v_s5_glu_w', 'new_v_s5_glu_b', 'new_v_lru_conv_w', 'new_v_lru_conv_b', 'new_v_lru_wx', 'new_v_lru_bx', 'new_v_lru_wa', 'new_v_lru_ba', 'new_v_lru_a_param', 'new_v_mix_norm_g', 'new_v_w_out', 'new_v_b_out', 'new_v_ln1_g', 'new_v_ln1_b', 'new_v_ffn_w_gate', 'new_v_ffn_w_up', 'new_v_ffn_conv_w', 'new_v_ffn_conv_b', 'new_v_ffn_w_down', 'new_v_ln2_g', 'new_v_ln2_b']
TWIN_LEAF_KINDS = {'loss': 'loss', 'grad_x': 'grad_x', 'grad_w_in': 'grad_w', 'grad_b_in': 'grad_w', 'grad_attn_sinks': 'grad_w', 'grad_s5_a_re': 'grad_w', 'grad_s5_a_im': 'grad_w', 'grad_s5_b_re': 'grad_w', 'grad_s5_b_im': 'grad_w', 'grad_s5_c_re': 'grad_w', 'grad_s5_c_im': 'grad_w', 'grad_s5_d': 'grad_w', 'grad_s5_log_dt': 'grad_w', 'grad_s5_glu_w': 'grad_w', 'grad_s5_glu_b': 'grad_w', 'grad_lru_conv_w': 'grad_w', 'grad_lru_conv_b': 'grad_w', 'grad_lru_wx': 'grad_w', 'grad_lru_bx': 'grad_w', 'grad_lru_wa': 'grad_w', 'grad_lru_ba': 'grad_w', 'grad_lru_a_param': 'grad_w', 'grad_mix_norm_g': 'grad_w', 'grad_w_out': 'grad_w', 'grad_b_out': 'grad_w', 'grad_ln1_g': 'grad_w', 'grad_ln1_b': 'grad_w', 'grad_ffn_w_gate': 'grad_w', 'grad_ffn_w_up': 'grad_w', 'grad_ffn_conv_w': 'grad_w', 'grad_ffn_conv_b': 'grad_w', 'grad_ffn_w_down': 'grad_w', 'grad_ln2_g': 'grad_w', 'grad_ln2_b': 'grad_w', 'delta_w_in': 'delta_w', 'delta_b_in': 'delta_w', 'delta_attn_sinks': 'delta_w', 'delta_s5_a_re': 'delta_w', 'delta_s5_a_im': 'delta_w', 'delta_s5_b_re': 'delta_w', 'delta_s5_b_im': 'delta_w', 'delta_s5_c_re': 'delta_w', 'delta_s5_c_im': 'delta_w', 'delta_s5_d': 'delta_w', 'delta_s5_log_dt': 'delta_w', 'delta_s5_glu_w': 'delta_w', 'delta_s5_glu_b': 'delta_w', 'delta_lru_conv_w': 'delta_w', 'delta_lru_conv_b': 'delta_w', 'delta_lru_wx': 'delta_w', 'delta_lru_bx': 'delta_w', 'delta_lru_wa': 'delta_w', 'delta_lru_ba': 'delta_w', 'delta_lru_a_param': 'delta_w', 'delta_mix_norm_g': 'delta_w', 'delta_w_out': 'delta_w', 'delta_b_out': 'delta_w', 'delta_ln1_g': 'delta_w', 'delta_ln1_b': 'delta_w', 'delta_ffn_w_gate': 'delta_w', 'delta_ffn_w_up': 'delta_w', 'delta_ffn_conv_w': 'delta_w', 'delta_ffn_conv_b': 'delta_w', 'delta_ffn_w_down': 'delta_w', 'delta_ln2_g': 'delta_w', 'delta_ln2_b': 'delta_w', 'new_m_w_in': 'new_m', 'new_m_b_in': 'new_m', 'new_m_attn_sinks': 'new_m', 'new_m_s5_a_re': 'new_m', 'new_m_s5_a_im': 'new_m', 'new_m_s5_b_re': 'new_m', 'new_m_s5_b_im': 'new_m', 'new_m_s5_c_re': 'new_m', 'new_m_s5_c_im': 'new_m', 'new_m_s5_d': 'new_m', 'new_m_s5_log_dt': 'new_m', 'new_m_s5_glu_w': 'new_m', 'new_m_s5_glu_b': 'new_m', 'new_m_lru_conv_w': 'new_m', 'new_m_lru_conv_b': 'new_m', 'new_m_lru_wx': 'new_m', 'new_m_lru_bx': 'new_m', 'new_m_lru_wa': 'new_m', 'new_m_lru_ba': 'new_m', 'new_m_lru_a_param': 'new_m', 'new_m_mix_norm_g': 'new_m', 'new_m_w_out': 'new_m', 'new_m_b_out': 'new_m', 'new_m_ln1_g': 'new_m', 'new_m_ln1_b': 'new_m', 'new_m_ffn_w_gate': 'new_m', 'new_m_ffn_w_up': 'new_m', 'new_m_ffn_conv_w': 'new_m', 'new_m_ffn_conv_b': 'new_m', 'new_m_ffn_w_down': 'new_m', 'new_m_ln2_g': 'new_m', 'new_m_ln2_b': 'new_m', 'new_v_w_in': 'new_v', 'new_v_b_in': 'new_v', 'new_v_attn_sinks': 'new_v', 'new_v_s5_a_re': 'new_v', 'new_v_s5_a_im': 'new_v', 'new_v_s5_b_re': 'new_v', 'new_v_s5_b_im': 'new_v', 'new_v_s5_c_re': 'new_v', 'new_v_s5_c_im': 'new_v', 'new_v_s5_d': 'new_v', 'new_v_s5_log_dt': 'new_v', 'new_v_s5_glu_w': 'new_v', 'new_v_s5_glu_b': 'new_v', 'new_v_lru_conv_w': 'new_v', 'new_v_lru_conv_b': 'new_v', 'new_v_lru_wx': 'new_v', 'new_v_lru_bx': 'new_v', 'new_v_lru_wa': 'new_v', 'new_v_lru_ba': 'new_v', 'new_v_lru_a_param': 'new_v', 'new_v_mix_norm_g': 'new_v', 'new_v_w_out': 'new_v', 'new_v_b_out': 'new_v', 'new_v_ln1_g': 'new_v', 'new_v_ln1_b': 'new_v', 'new_v_ffn_w_gate': 'new_v', 'new_v_ffn_w_up': 'new_v', 'new_v_ffn_conv_w': 'new_v', 'new_v_ffn_conv_b': 'new_v', 'new_v_ffn_w_down': 'new_v', 'new_v_ln2_g': 'new_v', 'new_v_ln2_b': 'new_v'}


def _forward(args):
    return _fwd_reference(*[args[k] for k in FWD_PARAMS])


def _output_shape():
    def fwd():
        inp = _fwd_setup_inputs(0)
        return _fwd_reference(*[inp[k] for k in FWD_PARAMS])
    out = _jax.eval_shape(fwd)
    return out.shape, out.dtype

N_MICROBATCH = 1
ADAM_LR = 0.001
ADAM_B1 = 0.9
ADAM_B2 = 0.999
ADAM_EPS = 1e-08
ADAM_WD = 0.01
ADAM_STEP = 10
PER_EXAMPLE_BATCH_AXIS = {'x': 0, 'loss_target': 0}
SHARED_INPUTS = []
_WEIGHT_DTYPES = {'w_in': _jnp.float32, 'b_in': _jnp.float32, 'attn_sinks': _jnp.float32, 's5_a_re': _jnp.float32, 's5_a_im': _jnp.float32, 's5_b_re': _jnp.float32, 's5_b_im': _jnp.float32, 's5_c_re': _jnp.float32, 's5_c_im': _jnp.float32, 's5_d': _jnp.float32, 's5_log_dt': _jnp.float32, 's5_glu_w': _jnp.float32, 's5_glu_b': _jnp.float32, 'lru_conv_w': _jnp.float32, 'lru_conv_b': _jnp.float32, 'lru_wx': _jnp.float32, 'lru_bx': _jnp.float32, 'lru_wa': _jnp.float32, 'lru_ba': _jnp.float32, 'lru_a_param': _jnp.float32, 'mix_norm_g': _jnp.float32, 'w_out': _jnp.float32, 'b_out': _jnp.float32, 'ln1_g': _jnp.float32, 'ln1_b': _jnp.float32, 'ffn_w_gate': _jnp.float32, 'ffn_w_up': _jnp.float32, 'ffn_conv_w': _jnp.float32, 'ffn_conv_b': _jnp.float32, 'ffn_w_down': _jnp.float32, 'ln2_g': _jnp.float32, 'ln2_b': _jnp.float32}
MOMENT_SCALE = {'w_in': 1.158020e-01, 'b_in': 1.622333e+00, 'attn_sinks': 3.140112e-02, 's5_a_re': 8.773567e-03, 's5_a_im': 6.383116e-03, 's5_b_re': 3.752171e-03, 's5_b_im': 3.628648e-03, 's5_c_re': 6.878007e-03, 's5_c_im': 7.313243e-03, 's5_d': 1.687785e-01, 's5_log_dt': 5.774251e+00, 's5_glu_w': 3.124015e-02, 's5_glu_b': 6.337421e-02, 'lru_conv_w': 1.076372e-01, 'lru_conv_b': 6.593186e-01, 'lru_wx': 4.873171e-02, 'lru_bx': 3.809067e-02, 'lru_wa': 2.703153e-02, 'lru_ba': 2.587120e-02, 'lru_a_param': 5.469559e-02, 'mix_norm_g': 1.309462e-01, 'w_out': 3.040312e-01, 'b_out': 7.631740e-01, 'ln1_g': 1.524442e+00, 'ln1_b': 1.322186e+00, 'ffn_w_gate': 3.479252e-02, 'ffn_w_up': 3.379810e-02, 'ffn_conv_w': 3.468885e-02, 'ffn_conv_b': 3.349548e-02, 'ffn_w_down': 1.333792e-01, 'ln2_g': 6.406948e+01, 'ln2_b': 1.098664e+01}


def _to_microbatches(a, axis):
    t = _jnp.moveaxis(a, axis, 0)
    t = t.reshape((N_MICROBATCH, t.shape[0] // N_MICROBATCH) + t.shape[1:])
    return _jnp.moveaxis(t, 1, axis + 1)


def setup_inputs(seed: int = 0) -> dict:
    inp = _fwd_setup_inputs(seed)
    key = _jax.random.fold_in(_jax.random.key(seed), 7919)
    shape, _ = _output_shape()
    out = dict(inp)
    out["loss_target"] = _jax.random.normal(_jax.random.fold_in(key, 0), shape, _jnp.float32)
    for i, name in enumerate(TWIN_WEIGHTS):
        w = inp[name].astype(_jnp.float32)
        if MOMENT_SCALE is None:
            s = _jnp.sqrt(_jnp.mean(_jnp.square(w)) + 1e-30)
        else:
            s = MOMENT_SCALE[name]
        km, kv = _jax.random.split(_jax.random.fold_in(key, i + 1))
        out[name] = w
        out["m_" + name] = s * _jax.random.normal(km, w.shape, _jnp.float32)
        out["v_" + name] = (s * s) * _jax.random.uniform(kv, w.shape, _jnp.float32, 0.5, 1.5)
    if N_MICROBATCH > 1:
        for name, axis in PER_EXAMPLE_BATCH_AXIS.items():
            out[name] = _to_microbatches(out[name], axis)
    return {'x': out['x'], 'w_in': out['w_in'], 'b_in': out['b_in'], 'attn_sinks': out['attn_sinks'], 's5_a_re': out['s5_a_re'], 's5_a_im': out['s5_a_im'], 's5_b_re': out['s5_b_re'], 's5_b_im': out['s5_b_im'], 's5_c_re': out['s5_c_re'], 's5_c_im': out['s5_c_im'], 's5_d': out['s5_d'], 's5_log_dt': out['s5_log_dt'], 's5_glu_w': out['s5_glu_w'], 's5_glu_b': out['s5_glu_b'], 'lru_conv_w': out['lru_conv_w'], 'lru_conv_b': out['lru_conv_b'], 'lru_wx': out['lru_wx'], 'lru_bx': out['lru_bx'], 'lru_wa': out['lru_wa'], 'lru_ba': out['lru_ba'], 'lru_a_param': out['lru_a_param'], 'mix_norm_g': out['mix_norm_g'], 'w_out': out['w_out'], 'b_out': out['b_out'], 'ln1_g': out['ln1_g'], 'ln1_b': out['ln1_b'], 'ffn_w_gate': out['ffn_w_gate'], 'ffn_w_up': out['ffn_w_up'], 'ffn_conv_w': out['ffn_conv_w'], 'ffn_conv_b': out['ffn_conv_b'], 'ffn_w_down': out['ffn_w_down'], 'ln2_g': out['ln2_g'], 'ln2_b': out['ln2_b'], 'loss_target': out['loss_target'], 'm_w_in': out['m_w_in'], 'm_b_in': out['m_b_in'], 'm_attn_sinks': out['m_attn_sinks'], 'm_s5_a_re': out['m_s5_a_re'], 'm_s5_a_im': out['m_s5_a_im'], 'm_s5_b_re': out['m_s5_b_re'], 'm_s5_b_im': out['m_s5_b_im'], 'm_s5_c_re': out['m_s5_c_re'], 'm_s5_c_im': out['m_s5_c_im'], 'm_s5_d': out['m_s5_d'], 'm_s5_log_dt': out['m_s5_log_dt'], 'm_s5_glu_w': out['m_s5_glu_w'], 'm_s5_glu_b': out['m_s5_glu_b'], 'm_lru_conv_w': out['m_lru_conv_w'], 'm_lru_conv_b': out['m_lru_conv_b'], 'm_lru_wx': out['m_lru_wx'], 'm_lru_bx': out['m_lru_bx'], 'm_lru_wa': out['m_lru_wa'], 'm_lru_ba': out['m_lru_ba'], 'm_lru_a_param': out['m_lru_a_param'], 'm_mix_norm_g': out['m_mix_norm_g'], 'm_w_out': out['m_w_out'], 'm_b_out': out['m_b_out'], 'm_ln1_g': out['m_ln1_g'], 'm_ln1_b': out['m_ln1_b'], 'm_ffn_w_gate': out['m_ffn_w_gate'], 'm_ffn_w_up': out['m_ffn_w_up'], 'm_ffn_conv_w': out['m_ffn_conv_w'], 'm_ffn_conv_b': out['m_ffn_conv_b'], 'm_ffn_w_down': out['m_ffn_w_down'], 'm_ln2_g': out['m_ln2_g'], 'm_ln2_b': out['m_ln2_b'], 'v_w_in': out['v_w_in'], 'v_b_in': out['v_b_in'], 'v_attn_sinks': out['v_attn_sinks'], 'v_s5_a_re': out['v_s5_a_re'], 'v_s5_a_im': out['v_s5_a_im'], 'v_s5_b_re': out['v_s5_b_re'], 'v_s5_b_im': out['v_s5_b_im'], 'v_s5_c_re': out['v_s5_c_re'], 'v_s5_c_im': out['v_s5_c_im'], 'v_s5_d': out['v_s5_d'], 'v_s5_log_dt': out['v_s5_log_dt'], 'v_s5_glu_w': out['v_s5_glu_w'], 'v_s5_glu_b': out['v_s5_glu_b'], 'v_lru_conv_w': out['v_lru_conv_w'], 'v_lru_conv_b': out['v_lru_conv_b'], 'v_lru_wx': out['v_lru_wx'], 'v_lru_bx': out['v_lru_bx'], 'v_lru_wa': out['v_lru_wa'], 'v_lru_ba': out['v_lru_ba'], 'v_lru_a_param': out['v_lru_a_param'], 'v_mix_norm_g': out['v_mix_norm_g'], 'v_w_out': out['v_w_out'], 'v_b_out': out['v_b_out'], 'v_ln1_g': out['v_ln1_g'], 'v_ln1_b': out['v_ln1_b'], 'v_ffn_w_gate': out['v_ffn_w_gate'], 'v_ffn_w_up': out['v_ffn_w_up'], 'v_ffn_conv_w': out['v_ffn_conv_w'], 'v_ffn_conv_b': out['v_ffn_conv_b'], 'v_ffn_w_down': out['v_ffn_w_down'], 'v_ln2_g': out['v_ln2_g'], 'v_ln2_b': out['v_ln2_b']}


def _loss(weights, diff, rest, loss_target):
    with _jax.named_scope("forward"):
        args = {**rest, TWIN_DIFF_INPUT: diff, **{k: w.astype(_WEIGHT_DTYPES[k]) for k, w in weights.items()}}
        y = _forward(args)
    with _jax.named_scope("loss_head"):
        err = _jnp.square(y.astype(_jnp.float32) - loss_target)
        return 0.5 * _jnp.sum(_jnp.mean(err, axis=-1)) if err.ndim else 0.5 * err


def _adamw(w, g, m, v):
    m = ADAM_B1 * m + (1.0 - ADAM_B1) * g
    v = ADAM_B2 * v + (1.0 - ADAM_B2) * _jnp.square(g)
    m_hat = m / (1.0 - ADAM_B1 ** ADAM_STEP)
    v_hat = v / (1.0 - ADAM_B2 ** ADAM_STEP)
    delta = -ADAM_LR * (m_hat / (_jnp.sqrt(v_hat) + ADAM_EPS) + ADAM_WD * w)
    return delta, m, v


def reference(x, w_in, b_in, attn_sinks, s5_a_re, s5_a_im, s5_b_re, s5_b_im, s5_c_re, s5_c_im, s5_d, s5_log_dt, s5_glu_w, s5_glu_b, lru_conv_w, lru_conv_b, lru_wx, lru_bx, lru_wa, lru_ba, lru_a_param, mix_norm_g, w_out, b_out, ln1_g, ln1_b, ffn_w_gate, ffn_w_up, ffn_conv_w, ffn_conv_b, ffn_w_down, ln2_g, ln2_b, loss_target, m_w_in, m_b_in, m_attn_sinks, m_s5_a_re, m_s5_a_im, m_s5_b_re, m_s5_b_im, m_s5_c_re, m_s5_c_im, m_s5_d, m_s5_log_dt, m_s5_glu_w, m_s5_glu_b, m_lru_conv_w, m_lru_conv_b, m_lru_wx, m_lru_bx, m_lru_wa, m_lru_ba, m_lru_a_param, m_mix_norm_g, m_w_out, m_b_out, m_ln1_g, m_ln1_b, m_ffn_w_gate, m_ffn_w_up, m_ffn_conv_w, m_ffn_conv_b, m_ffn_w_down, m_ln2_g, m_ln2_b, v_w_in, v_b_in, v_attn_sinks, v_s5_a_re, v_s5_a_im, v_s5_b_re, v_s5_b_im, v_s5_c_re, v_s5_c_im, v_s5_d, v_s5_log_dt, v_s5_glu_w, v_s5_glu_b, v_lru_conv_w, v_lru_conv_b, v_lru_wx, v_lru_bx, v_lru_wa, v_lru_ba, v_lru_a_param, v_mix_norm_g, v_w_out, v_b_out, v_ln1_g, v_ln1_b, v_ffn_w_gate, v_ffn_w_up, v_ffn_conv_w, v_ffn_conv_b, v_ffn_w_down, v_ln2_g, v_ln2_b):
    given = dict(x=x, w_in=w_in, b_in=b_in, attn_sinks=attn_sinks, s5_a_re=s5_a_re, s5_a_im=s5_a_im, s5_b_re=s5_b_re, s5_b_im=s5_b_im, s5_c_re=s5_c_re, s5_c_im=s5_c_im, s5_d=s5_d, s5_log_dt=s5_log_dt, s5_glu_w=s5_glu_w, s5_glu_b=s5_glu_b, lru_conv_w=lru_conv_w, lru_conv_b=lru_conv_b, lru_wx=lru_wx, lru_bx=lru_bx, lru_wa=lru_wa, lru_ba=lru_ba, lru_a_param=lru_a_param, mix_norm_g=mix_norm_g, w_out=w_out, b_out=b_out, ln1_g=ln1_g, ln1_b=ln1_b, ffn_w_gate=ffn_w_gate, ffn_w_up=ffn_w_up, ffn_conv_w=ffn_conv_w, ffn_conv_b=ffn_conv_b, ffn_w_down=ffn_w_down, ln2_g=ln2_g, ln2_b=ln2_b, loss_target=loss_target, m_w_in=m_w_in, m_b_in=m_b_in, m_attn_sinks=m_attn_sinks, m_s5_a_re=m_s5_a_re, m_s5_a_im=m_s5_a_im, m_s5_b_re=m_s5_b_re, m_s5_b_im=m_s5_b_im, m_s5_c_re=m_s5_c_re, m_s5_c_im=m_s5_c_im, m_s5_d=m_s5_d, m_s5_log_dt=m_s5_log_dt, m_s5_glu_w=m_s5_glu_w, m_s5_glu_b=m_s5_glu_b, m_lru_conv_w=m_lru_conv_w, m_lru_conv_b=m_lru_conv_b, m_lru_wx=m_lru_wx, m_lru_bx=m_lru_bx, m_lru_wa=m_lru_wa, m_lru_ba=m_lru_ba, m_lru_a_param=m_lru_a_param, m_mix_norm_g=m_mix_norm_g, m_w_out=m_w_out, m_b_out=m_b_out, m_ln1_g=m_ln1_g, m_ln1_b=m_ln1_b, m_ffn_w_gate=m_ffn_w_gate, m_ffn_w_up=m_ffn_w_up, m_ffn_conv_w=m_ffn_conv_w, m_ffn_conv_b=m_ffn_conv_b, m_ffn_w_down=m_ffn_w_down, m_ln2_g=m_ln2_g, m_ln2_b=m_ln2_b, v_w_in=v_w_in, v_b_in=v_b_in, v_attn_sinks=v_attn_sinks, v_s5_a_re=v_s5_a_re, v_s5_a_im=v_s5_a_im, v_s5_b_re=v_s5_b_re, v_s5_b_im=v_s5_b_im, v_s5_c_re=v_s5_c_re, v_s5_c_im=v_s5_c_im, v_s5_d=v_s5_d, v_s5_log_dt=v_s5_log_dt, v_s5_glu_w=v_s5_glu_w, v_s5_glu_b=v_s5_glu_b, v_lru_conv_w=v_lru_conv_w, v_lru_conv_b=v_lru_conv_b, v_lru_wx=v_lru_wx, v_lru_bx=v_lru_bx, v_lru_wa=v_lru_wa, v_lru_ba=v_lru_ba, v_lru_a_param=v_lru_a_param, v_mix_norm_g=v_mix_norm_g, v_w_out=v_w_out, v_b_out=v_b_out, v_ln1_g=v_ln1_g, v_ln1_b=v_ln1_b, v_ffn_w_gate=v_ffn_w_gate, v_ffn_w_up=v_ffn_w_up, v_ffn_conv_w=v_ffn_conv_w, v_ffn_conv_b=v_ffn_conv_b, v_ffn_w_down=v_ffn_w_down, v_ln2_g=v_ln2_g, v_ln2_b=v_ln2_b)
    weights = {n: given[n] for n in TWIN_WEIGHTS}
    shared = {n: given[n] for n in SHARED_INPUTS}
    per_example = {n: given[n] for n in ['x']}
    grad_fn = _jax.value_and_grad(_loss, argnums=(0, 1))

    def one_microbatch(ex, loss_target):
        ex = dict(ex)
        diff = ex.pop(TWIN_DIFF_INPUT)
        return grad_fn(weights, diff, {**shared, **ex}, loss_target)

    if N_MICROBATCH == 1:
        loss, (grad_w, grad_x) = one_microbatch(per_example, given["loss_target"])
    else:
        def body(carry, xs):
            loss_sum, grad_sum = carry
            l_k, (gw_k, gx_k) = one_microbatch(xs[0], xs[1])
            with _jax.named_scope("update"):
                return (loss_sum + l_k, _jax.tree.map(_jnp.add, grad_sum, gw_k)), gx_k

        init = (_jnp.zeros((), _jnp.float32), _jax.tree.map(_jnp.zeros_like, weights))
        (loss, grad_w), grad_x = _jax.lax.scan(body, init, (per_example, given["loss_target"]))
    with _jax.named_scope("update"):
        delta_w, new_m, new_v = {}, {}, {}
        for n in TWIN_WEIGHTS:
            delta_w[n], new_m[n], new_v[n] = _adamw(weights[n], grad_w[n], given["m_" + n], given["v_" + n])
    return (loss, grad_x, *[grad_w[n] for n in TWIN_WEIGHTS], *[delta_w[n] for n in TWIN_WEIGHTS],
            *[new_m[n] for n in TWIN_WEIGHTS], *[new_v[n] for n in TWIN_WEIGHTS])
```

```python
import functools
import math

import jax
import jax.numpy as jnp
from jax import lax
from jax.experimental import pallas as pl
from jax.experimental.pallas import tpu as pltpu

F32 = jnp.float32
MXU = jnp.bfloat16

N_DEV = 8
DEPTH = 4
D = 1024
D_ATTN, D_KV, D_S5, D_LRU = 512, 128, 256, 256
D_IN = 1536
D_FF = 2816
FF_CHUNK = 704
N_STATE = 1024
LANES = 1024
ALPHA = (2 * DEPTH) ** 0.25
LN_EPS = 1e-5
RMS_EPS = 1e-6
LRU_C = 8.0
ROPE_THETA = 10000.0
ADAM_LR, ADAM_B1, ADAM_B2, ADAM_EPS, ADAM_WD, ADAM_STEP = 0.001, 0.9, 0.999, 1e-08, 0.01, 10

TILE = 256
TILE_Q = 512
TILE_UP = 128
ATTN_BLOCK = 128
PACK_TILE = 256
VMEM_MB = 56

WEIGHTS = ['w_in', 'b_in', 'attn_sinks', 's5_a_re', 's5_a_im', 's5_b_re', 's5_b_im', 's5_c_re', 's5_c_im', 's5_d', 's5_log_dt',
           's5_glu_w', 's5_glu_b', 'lru_conv_w', 'lru_conv_b', 'lru_wx', 'lru_bx', 'lru_wa', 'lru_ba', 'lru_a_param', 'mix_norm_g',
           'w_out', 'b_out', 'ln1_g', 'ln1_b', 'ffn_w_gate', 'ffn_w_up', 'ffn_conv_w', 'ffn_conv_b', 'ffn_w_down', 'ln2_g', 'ln2_b']
SHARD_AXIS = {'w_in': 2, 's5_glu_w': 1, 'lru_conv_w': 2, 'w_out': 1, 'ffn_w_gate': 2, 'ffn_w_up': 2, 'ffn_conv_w': 2,
              'ffn_w_down': 1}
SHARDED = [n for n in WEIGHTS if n in SHARD_AXIS]
REPLICATED = [n for n in WEIGHTS if n not in SHARD_AXIS]
GATHER_F32 = ('lru_conv_w', 'ffn_conv_w')


def _dot(a, b):
    return jnp.dot(a, b, preferred_element_type=F32)


def _dot_nt(a, b):
    return lax.dot_general(a, b, (((1,), (1,)), ((), ())), preferred_element_type=F32)


def _dot_tn(a, b):
    return lax.dot_general(a, b, (((0,), (0,)), ((), ())), preferred_element_type=F32)


def _mx(a):
    return a.astype(MXU)


_GELU_C = math.sqrt(2.0 / math.pi)


def _gelu(x):
    th = jnp.tanh(_GELU_C * (x + 0.044715 * x * x * x))
    return 0.5 * x * (1.0 + th)


def _gelu_grad(x):
    th = jnp.tanh(_GELU_C * (x + 0.044715 * x * x * x))
    return 0.5 * (1.0 + th) + 0.5 * x * (1.0 - th * th) * _GELU_C * (1.0 + 3.0 * 0.044715 * x * x)


def _sigmoid(x):
    return 1.0 / (1.0 + jnp.exp(-x))


def _ln_stats(r):
    mu = jnp.mean(r, axis=-1, keepdims=True)
    xc = r - mu
    var = jnp.mean(xc * xc, axis=-1, keepdims=True)
    rstd = lax.rsqrt(var + LN_EPS)
    return xc * rstd, rstd


def _ln_bwd(dy, g, xhat, rstd):
    dxh = dy * g
    return rstd * (dxh - jnp.mean(dxh, axis=-1, keepdims=True) - xhat * jnp.mean(dxh * xhat, axis=-1, keepdims=True))


def _rms(y):
    return lax.rsqrt(jnp.mean(y * y, axis=-1, keepdims=True) + RMS_EPS)


def _sum0(a):
    return jnp.sum(a, axis=0, keepdims=True)


def _row_iota(shape):
    return lax.broadcasted_iota(jnp.int32, shape, 0)


def _shift_down(ext, j, rows):
    return pltpu.roll(ext, j, 0)[8:8 + rows]


def _shift_up(ext, j, rows):
    return pltpu.roll(ext, ext.shape[0] - j, 0)[:rows]


def _swap_halves(t):
    w = t.shape[1]
    lane = lax.broadcasted_iota(jnp.int32, t.shape, 1)
    return jnp.where((lane & 32) == 0, pltpu.roll(t, w - 32, 1), pltpu.roll(t, 32, 1))


def _rope(t, cos, sin_signed):
    return t * cos + _swap_halves(t) * sin_signed


def _rope_t(d, cos, sin_signed):
    return d * cos + _swap_halves(d * sin_signed)


def _cscan(br, bi, pr_ref, pi_ref, reverse):
    rows = br.shape[0]
    row = _row_iota(br.shape)
    s, k = 1, 0
    while s < rows:
        ar = pr_ref[k:k + 1, :]
        ai = pi_ref[k:k + 1, :]
        if reverse:
            keep = row < rows - s
            sr = jnp.where(keep, pltpu.roll(br, rows - s, 0), 0.0)
            si = jnp.where(keep, pltpu.roll(bi, rows - s, 0), 0.0)
            br, bi = br + ar * sr + ai * si, bi + ar * si - ai * sr
        else:
            keep = row >= s
            sr = jnp.where(keep, pltpu.roll(br, s, 0), 0.0)
            si = jnp.where(keep, pltpu.roll(bi, s, 0), 0.0)
            br, bi = br + ar * sr - ai * si, bi + ar * si + ai * sr
        s, k = 2 * s, k + 1
    return br, bi


def _rscan(a, b, reverse):
    rows = a.shape[0]
    row = _row_iota(a.shape)
    s = 1
    while s < rows:
        if reverse:
            keep = row < rows - s
            sa = jnp.where(keep, pltpu.roll(a, rows - s, 0), 1.0)
            sb = jnp.where(keep, pltpu.roll(b, rows - s, 0), 0.0)
        else:
            keep = row >= s
            sa = jnp.where(keep, pltpu.roll(a, s, 0), 1.0)
            sb = jnp.where(keep, pltpu.roll(b, s, 0), 0.0)
        b = b + a * sb
        a = a * sa
        s *= 2
    return a, b


def _whole():
    return pl.BlockSpec(memory_space=pltpu.VMEM)


def _rows_spec(rows, cols, n_tiles, reverse=False):
    if reverse:
        return pl.BlockSpec((rows, cols), lambda i: (n_tiles - 1 - i, 0))
    return pl.BlockSpec((rows, cols), lambda i: (i, 0))


def _halo_spec(cols, tile_rows, n_tiles, reverse=False):
    per = tile_rows // 8
    if reverse:
        return pl.BlockSpec((8, cols), lambda i: (jnp.maximum((n_tiles - 1 - i) * per - 1, 0), 0))
    return pl.BlockSpec((8, cols), lambda i: (jnp.maximum(i * per - 1, 0), 0))


def _call(body, name, n_tiles, in_specs, out_specs, out_shape, scratch=()):
    return pl.pallas_call(
        body, name=name, grid=(n_tiles,), in_specs=in_specs, out_specs=out_specs, out_shape=out_shape,
        scratch_shapes=list(scratch),
        compiler_params=pltpu.CompilerParams(dimension_semantics=("arbitrary",), vmem_limit_bytes=VMEM_MB << 20))


def _sds(shape, dtype=F32):
    return jax.ShapeDtypeStruct(shape, dtype)


def _inproj_fwd(x, w, b, cos_t, sin_t):
    n = x.shape[0]
    nt = n // TILE

    def body(x_ref, w_ref, b_ref, c_ref, s_ref, q_ref, k_ref, v_ref, u_ref, xr_ref, g_ref):
        p = _dot(_mx(x_ref[...]), w_ref[...]) + b_ref[...]
        cos, sin = c_ref[...], s_ref[...]
        q_ref[...] = _mx(_rope(p[:, :D_ATTN], jnp.tile(cos, (1, 4)), jnp.tile(sin, (1, 4))))
        k_ref[...] = _mx(_rope(p[:, 512:640], cos, sin))
        v_ref[...] = _mx(p[:, 640:768])
        u_ref[...] = p[:, 768:1024]
        xr_ref[...] = p[:, 1024:1280]
        g_ref[...] = p[:, 1280:1536]

    r = functools.partial(_rows_spec, n_tiles=nt)
    return _call(
        body, "inproj_fwd", nt,
        [r(TILE, D), _whole(), _whole(), r(TILE, 128), r(TILE, 128)],
        [r(TILE, D_ATTN), r(TILE, D_KV), r(TILE, D_KV), r(TILE, D_S5), r(TILE, D_LRU), r(TILE, D_LRU)],
        [_sds((n, D_ATTN), MXU), _sds((n, D_KV), MXU), _sds((n, D_KV), MXU), _sds((n, D_S5)), _sds((n, D_LRU)), _sds((n, D_LRU))],
    )(x, w, b, cos_t, sin_t)


def _inproj_bwd(dq, dk, dv, du, dxr, dgate, cos_t, sin_t, x0, dr1, w):
    n = x0.shape[0]
    nt = n // TILE

    def body(dq_ref, dk_ref, dv_ref, du_ref, dxr_ref, dg_ref, c_ref, s_ref, x_ref, dr_ref, w_ref, dx_ref, dw_ref, db_ref):
        @pl.when(pl.program_id(0) == 0)
        def _():
            dw_ref[...] = jnp.zeros_like(dw_ref)
            db_ref[...] = jnp.zeros_like(db_ref)

        cos, sin = c_ref[...], s_ref[...]
        dtq = _rope_t(dq_ref[...], jnp.tile(cos, (1, 4)), jnp.tile(sin, (1, 4)))
        dtk = _rope_t(dk_ref[...], cos, sin)
        dp = jnp.concatenate([dtq, dtk, dv_ref[...], du_ref[...], dxr_ref[...], dg_ref[...]], axis=1)
        db_ref[...] += _sum0(dp)
        dpb = _mx(dp)
        dw_ref[...] += _dot_tn(_mx(x_ref[...]), dpb)
        dx_ref[...] = ALPHA * dr_ref[...] + _dot_nt(dpb, w_ref[...])

    r = functools.partial(_rows_spec, n_tiles=nt)
    return _call(
        body, "inproj_bwd", nt,
        [r(TILE, D_ATTN), r(TILE, D_KV), r(TILE, D_KV), r(TILE, D_S5), r(TILE, D_LRU), r(TILE, D_LRU), r(TILE, 128), r(TILE, 128),
         r(TILE, D), r(TILE, D), _whole()],
        [r(TILE, D), _whole(), _whole()],
        [_sds((n, D)), _sds((D, D_IN)), _sds((1, D_IN))],
    )(dq, dk, dv, du, dxr, dgate, cos_t, sin_t, x0, dr1, w)


def _kv_variants(t, lo):
    tr = pltpu.roll(t, 64, 1)
    out = []
    for j in range(2):
        first = jnp.where(lo, t if j == 0 else tr, 0.0)
        second = jnp.where(lo, 0.0, tr if j == 0 else t)
        out.append(_mx(jnp.concatenate([first, second], axis=0)))
    return out


def _kv_collect(x0, x1, lo):
    a = x0[:256] + pltpu.roll(x0[256:], 64, 1)
    b = pltpu.roll(x1[:256], 64, 1) + x1[256:]
    return jnp.where(lo, a, b)


def _attn_probs(s, sink_row):
    out = []
    for hp in range(2):
        sh = s[:, hp * 256:(hp + 1) * 256]
        sink = sink_row[:, hp * 256:hp * 256 + 1]
        m = jnp.maximum(jnp.max(sh, axis=1, keepdims=True), sink)
        p = jnp.exp(sh - m)
        es = jnp.exp(sink - m)
        den = jnp.sum(p, axis=1, keepdims=True) + es
        out.append((p / den, es / den))
    return out


def _attn_mask(block, kstart):
    col = lax.broadcasted_iota(jnp.int32, (ATTN_BLOCK, 512), 1)
    row = lax.broadcasted_iota(jnp.int32, (ATTN_BLOCK, 512), 0)
    diff = (block * ATTN_BLOCK + row) - (kstart + (col & 255))
    return (diff >= 0) & (diff < ATTN_BLOCK)


def _attn_fwd(q, k, v, sink_rows):
    n = q.shape[0]
    nt = n // TILE_Q
    nb = TILE_Q // ATTN_BLOCK

    def body(q_ref, k_ref, v_ref, s_ref, o_ref):
        lo = lax.broadcasted_iota(jnp.int32, (256, 128), 1) < 64
        for b in range(nb):
            block = pl.program_id(0) * nb + b
            kstart = pl.multiple_of(jnp.maximum(block - 1, 0) * ATTN_BLOCK, ATTN_BLOCK)
            kcat = _kv_variants(k_ref[pl.ds(kstart, 256), :].astype(F32), lo)
            vcat = _kv_variants(v_ref[pl.ds(kstart, 256), :].astype(F32), lo)
            valid = _attn_mask(block, kstart)
            rows = slice(b * ATTN_BLOCK, (b + 1) * ATTN_BLOCK)
            for i in range(4):
                cols = slice(i * 128, (i + 1) * 128)
                s = _dot_nt(q_ref[rows, cols], kcat[i // 2]) * 0.125
                s = jnp.where(valid, s, -jnp.inf)
                (p0, _), (p1, _) = _attn_probs(s, s_ref[i:i + 1, :])
                o_ref[rows, cols] = _dot(_mx(jnp.concatenate([p0, p1], axis=1)), vcat[i // 2])

    return _call(
        body, "attn_fwd", nt,
        [_rows_spec(TILE_Q, D_ATTN, nt), _whole(), _whole(), _whole()],
        _rows_spec(TILE_Q, D_ATTN, nt), _sds((n, D_ATTN)),
    )(q, k, v, sink_rows)


def _attn_bwd(q, k, v, sink_rows, o, do):
    n = q.shape[0]
    nt = n // TILE_Q
    nb = TILE_Q // ATTN_BLOCK

    def body(q_ref, k_ref, v_ref, s_ref, o_ref, do_ref, dq_ref, dk_ref, dv_ref, ds_ref):
        @pl.when(pl.program_id(0) == 0)
        def _():
            dk_ref[...] = jnp.zeros_like(dk_ref)
            dv_ref[...] = jnp.zeros_like(dv_ref)
            ds_ref[...] = jnp.zeros_like(ds_ref)

        lo = lax.broadcasted_iota(jnp.int32, (256, 128), 1) < 64
        lo_q = lax.broadcasted_iota(jnp.int32, (ATTN_BLOCK, 128), 1) < 64
        for b in range(nb):
            block = pl.program_id(0) * nb + b
            kstart = pl.multiple_of(jnp.maximum(block - 1, 0) * ATTN_BLOCK, ATTN_BLOCK)
            kcat = _kv_variants(k_ref[pl.ds(kstart, 256), :].astype(F32), lo)
            vcat = _kv_variants(v_ref[pl.ds(kstart, 256), :].astype(F32), lo)
            valid = _attn_mask(block, kstart)
            rows = slice(b * ATTN_BLOCK, (b + 1) * ATTN_BLOCK)
            dkc = [jnp.zeros((512, 128), F32), jnp.zeros((512, 128), F32)]
            dvc = [jnp.zeros((512, 128), F32), jnp.zeros((512, 128), F32)]
            for i in range(4):
                j = i // 2
                cols = slice(i * 128, (i + 1) * 128)
                qi = q_ref[rows, cols]
                s = _dot_nt(qi, kcat[j]) * 0.125
                s = jnp.where(valid, s, -jnp.inf)
                probs = _attn_probs(s, s_ref[i:i + 1, :])
                doi = do_ref[rows, cols]
                dob = _mx(doi)
                dp = _dot_nt(dob, vcat[j])
                od = doi * o_ref[rows, cols]
                ds = []
                for hp in range(2):
                    p, p_sink = probs[hp]
                    delta = jnp.sum(jnp.where(lo_q, od, 0.0) if hp == 0 else jnp.where(lo_q, 0.0, od), axis=1, keepdims=True)
                    ds.append(p * (dp[:, hp * 256:(hp + 1) * 256] - delta) * 0.125)
                    dsink = -_sum0(p_sink * delta)
                    ds_ref[i:i + 1, hp * 256:(hp + 1) * 256] += jnp.broadcast_to(dsink, (1, 256))
                dsb = _mx(jnp.concatenate(ds, axis=1))
                pb = _mx(jnp.concatenate([probs[0][0], probs[1][0]], axis=1))
                dq_ref[rows, cols] = _dot(dsb, kcat[j])
                dkc[j] = dkc[j] + _dot_tn(dsb, qi)
                dvc[j] = dvc[j] + _dot_tn(pb, dob)
            dk_ref[pl.ds(kstart, 256), :] += _kv_collect(dkc[0], dkc[1], lo)
            dv_ref[pl.ds(kstart, 256), :] += _kv_collect(dvc[0], dvc[1], lo)

    r = _rows_spec(TILE_Q, D_ATTN, nt)
    return _call(
        body, "attn_bwd", nt,
        [r, _whole(), _whole(), _whole(), r, r],
        [r, _whole(), _whole(), _whole()],
        [_sds((n, D_ATTN)), _sds((n, D_KV)), _sds((n, D_KV)), _sds((4, 512))],
    )(q, k, v, sink_rows, o, do)


def _s5_states(u, carry_r, carry_i, b_re, b_im, pw_re, pw_im, ap_re, ap_im):
    ub = _mx(u)
    hr, hi = _cscan(_dot(ub, b_re[...]), _dot(ub, b_im[...]), pw_re, pw_im, reverse=False)
    apr, api = ap_re[...], ap_im[...]
    return ub, hr + apr * carry_r - api * carry_i, hi + apr * carry_i + api * carry_r


def _s5_fwd(u, prm):
    n = u.shape[0]
    nt = n // TILE

    def body(u_ref, b_re, b_im, pw_re, pw_im, ap_re, ap_im, c_re, c_im, d_ref, gw_ref, gb_ref, y_ref, cr_out, ci_out, cr_s, ci_s):
        @pl.when(pl.program_id(0) == 0)
        def _():
            cr_s[...] = jnp.zeros_like(cr_s)
            ci_s[...] = jnp.zeros_like(ci_s)

        u = u_ref[...]
        cr, ci = cr_s[...], ci_s[...]
        cr_out[...] = jnp.broadcast_to(cr, (8, N_STATE))
        ci_out[...] = jnp.broadcast_to(ci, (8, N_STATE))
        _, hr, hi = _s5_states(u, cr, ci, b_re, b_im, pw_re, pw_im, ap_re, ap_im)
        cr_s[...] = hr[TILE - 1:TILE]
        ci_s[...] = hi[TILE - 1:TILE]
        y = _dot(_mx(hr), c_re[...]) - _dot(_mx(hi), c_im[...]) + d_ref[...] * u
        z = _gelu(y)
        y_ref[...] = z * _sigmoid(_dot(_mx(z), gw_ref[...]) + gb_ref[...])

    r = functools.partial(_rows_spec, n_tiles=nt)
    return _call(
        body, "s5_fwd", nt,
        [r(TILE, D_S5)] + [_whole()] * 11,
        [r(TILE, D_S5), r(8, N_STATE), r(8, N_STATE)],
        [_sds((n, D_S5)), _sds((nt * 8, N_STATE)), _sds((nt * 8, N_STATE))],
        scratch=[pltpu.VMEM((1, N_STATE), F32), pltpu.VMEM((1, N_STATE), F32)],
    )(u, prm['b_re'], prm['b_im'], prm['pw_re'], prm['pw_im'], prm['ap_re'], prm['ap_im'], prm['c_re'], prm['c_im'], prm['d'],
      prm['glu_w'], prm['glu_b'])


def _s5_bwd(u, dys, carry_re, carry_im, prm):
    n = u.shape[0]
    nt = n // TILE

    def body(u_ref, dy_ref, cin_r, cin_i, b_re, b_im, pw_re, pw_im, ap_re, ap_im, rp_re, rp_im, c_re, c_im, d_ref, gw_ref, gb_ref,
             du_ref, dbr_ref, dbi_ref, dcr_ref, dci_ref, dar_ref, dai_ref, dd_ref, dgw_ref, dgb_ref, gr_s, gi_s):
        @pl.when(pl.program_id(0) == 0)
        def _():
            for ref in (dbr_ref, dbi_ref, dcr_ref, dci_ref, dar_ref, dai_ref, dd_ref, dgw_ref, dgb_ref, gr_s, gi_s):
                ref[...] = jnp.zeros_like(ref)

        u = u_ref[...]
        cr, ci = cin_r[0:1, :], cin_i[0:1, :]
        ub, hr, hi = _s5_states(u, cr, ci, b_re, b_im, pw_re, pw_im, ap_re, ap_im)
        hrb, hib = _mx(hr), _mx(hi)
        y = _dot(hrb, c_re[...]) - _dot(hib, c_im[...]) + d_ref[...] * u
        z = _gelu(y)
        zb = _mx(z)
        sg = _sigmoid(_dot(zb, gw_ref[...]) + gb_ref[...])
        dout = dy_ref[...]
        dpre = dout * z * sg * (1.0 - sg)
        dgb_ref[...] += _sum0(dpre)
        dpb = _mx(dpre)
        dgw_ref[...] += _dot_tn(zb, dpb)
        dy = (dout * sg + _dot_nt(dpb, gw_ref[...])) * _gelu_grad(y)
        dd_ref[...] += _sum0(dy * u)
        dyb = _mx(dy)
        dcr_ref[...] += _dot_tn(hrb, dyb)
        dci_ref[...] -= _dot_tn(hib, dyb)
        gr, gi = _cscan(_dot_nt(dyb, c_re[...]), -_dot_nt(dyb, c_im[...]), pw_re, pw_im, reverse=True)
        ncr, nci = gr_s[...], gi_s[...]
        rpr, rpi = rp_re[...], rp_im[...]
        gr, gi = gr + rpr * ncr - rpi * nci, gi + rpr * nci + rpi * ncr
        gr_s[...] = gr[0:1]
        gi_s[...] = gi[0:1]
        first = _row_iota(hr.shape) == 0
        hpr = jnp.where(first, cr, pltpu.roll(hr, 1, 0))
        hpi = jnp.where(first, ci, pltpu.roll(hi, 1, 0))
        dar_ref[...] += _sum0(gr * hpr + gi * hpi)
        dai_ref[...] += _sum0(gi * hpr - gr * hpi)
        grb, gib = _mx(gr), _mx(gi)
        dbr_ref[...] += _dot_tn(ub, grb)
        dbi_ref[...] += _dot_tn(ub, gib)
        du_ref[...] = dy * d_ref[...] + _dot_nt(grb, b_re[...]) + _dot_nt(gib, b_im[...])

    r = functools.partial(_rows_spec, n_tiles=nt, reverse=True)
    return _call(
        body, "s5_bwd", nt,
        [r(TILE, D_S5), r(TILE, D_S5), r(8, N_STATE), r(8, N_STATE)] + [_whole()] * 13,
        [r(TILE, D_S5)] + [_whole()] * 9,
        [_sds((n, D_S5)), _sds((D_S5, N_STATE)), _sds((D_S5, N_STATE)), _sds((N_STATE, D_S5)), _sds((N_STATE, D_S5)),
         _sds((1, N_STATE)), _sds((1, N_STATE)), _sds((1, D_S5)), _sds((D_S5, D_S5)), _sds((1, D_S5))],
        scratch=[pltpu.VMEM((1, N_STATE), F32), pltpu.VMEM((1, N_STATE), F32)],
    )(u, dys, carry_re, carry_im, prm['b_re'], prm['b_im'], prm['pw_re'], prm['pw_im'], prm['ap_re'], prm['ap_im'], prm['rp_re'],
      prm['rp_im'], prm['c_re'], prm['c_im'], prm['d'], prm['glu_w'], prm['glu_b'])


def _lru_gates(xr, halo, tile_index, cw_ref, cb_ref, wx_ref, wa_ref, bx_ref, ba_ref, sp_ref):
    ext = jnp.concatenate([halo, xr], axis=0)
    sh = [xr] + [_shift_down(ext, j, TILE) for j in (1, 2, 3)]
    xc = cb_ref[...] + cw_ref[3:4, :] * sh[0] + cw_ref[2:3, :] * sh[1] + cw_ref[1:2, :] * sh[2] + cw_ref[0:1, :] * sh[3]
    xb = _mx(xc)
    gx = _sigmoid(_dot(xb, wx_ref[...]) + bx_ref[...])
    ga = _sigmoid(_dot(xb, wa_ref[...]) + ba_ref[...])
    la = -LRU_C * ga * sp_ref[...]
    a = jnp.exp(la)
    start = (tile_index * TILE + _row_iota(xr.shape)) == 0
    mult = jnp.where(start, 1.0, jnp.sqrt(-jnp.tanh(la) * (a * a + 1.0)))
    return sh, xc, xb, gx, ga, a, mult, start


def _lru_fwd(xr, gate, prm):
    n = xr.shape[0]
    nt = n // TILE

    def body(x_ref, g_ref, cw_ref, cb_ref, wx_ref, wa_ref, bx_ref, ba_ref, sp_ref, y_ref, c_out, halo_s, c_s):
        first_tile = pl.program_id(0) == 0

        @pl.when(first_tile)
        def _():
            halo_s[...] = jnp.zeros_like(halo_s)
            c_s[...] = jnp.zeros_like(c_s)

        xr = x_ref[...]
        _, xc, _, gx, _, a, mult, _ = _lru_gates(xr, halo_s[...], pl.program_id(0), cw_ref, cb_ref, wx_ref, wa_ref, bx_ref, ba_ref,
                                                 sp_ref)
        halo_s[...] = xr[TILE - 8:]
        acum, h = _rscan(a, mult * gx * xc, reverse=False)
        c = c_s[...]
        c_out[...] = jnp.broadcast_to(c, (8, D_LRU))
        h = h + acum * c
        c_s[...] = h[TILE - 1:TILE]
        y_ref[...] = h * _gelu(g_ref[...])

    r = functools.partial(_rows_spec, n_tiles=nt)
    return _call(
        body, "lru_fwd", nt,
        [r(TILE, D_LRU), r(TILE, D_LRU)] + [_whole()] * 7,
        [r(TILE, D_LRU), r(8, D_LRU)],
        [_sds((n, D_LRU)), _sds((nt * 8, D_LRU))],
        scratch=[pltpu.VMEM((8, D_LRU), F32), pltpu.VMEM((1, D_LRU), F32)],
    )(xr, gate, prm['conv_w'], prm['conv_b'], prm['wx'], prm['wa'], prm['bx'], prm['ba'], prm['sp'])


def _lru_bwd(xr, gate, dyl, carry, prm):
    n = xr.shape[0]
    nt = n // TILE

    def body(x_ref, xh_ref, g_ref, dy_ref, cin_ref, cw_ref, cb_ref, wx_ref, wa_ref, bx_ref, ba_ref, sp_ref,
             dx_ref, dg_ref, dcw0, dcw1, dcw2, dcw3, dcb_ref, dwx_ref, dwa_ref, dbx_ref, dba_ref, dsp_ref, an_s, gn_s, dn_s):
        first_tile = pl.program_id(0) == nt - 1

        @pl.when(pl.program_id(0) == 0)
        def _():
            for ref in (dcw0, dcw1, dcw2, dcw3, dcb_ref, dwx_ref, dwa_ref, dbx_ref, dba_ref, dsp_ref, gn_s, dn_s):
                ref[...] = jnp.zeros_like(ref)
            an_s[...] = jnp.ones_like(an_s)

        xr = x_ref[...]
        halo = jnp.where(first_tile, 0.0, xh_ref[...])
        sh, xc, xb, gx, ga, a, mult, start = _lru_gates(xr, halo, nt - 1 - pl.program_id(0), cw_ref, cb_ref, wx_ref, wa_ref, bx_ref,
                                                        ba_ref, sp_ref)
        acum, h = _rscan(a, mult * gx * xc, reverse=False)
        cin = cin_ref[0:1, :]
        h = h + acum * cin
        gate = g_ref[...]
        dyl = dy_ref[...]
        dg_ref[...] = dyl * h * _gelu_grad(gate)
        row = _row_iota(xr.shape)
        alpha = jnp.where(row < TILE - 1, pltpu.roll(a, TILE - 1, 0), an_s[...])
        racc, g = _rscan(alpha, dyl * _gelu(gate), reverse=True)
        g = g + racc * gn_s[...]
        an_s[...] = a[0:1]
        gn_s[...] = g[0:1]
        hprev = jnp.where(row == 0, cin, pltpu.roll(h, 1, 0))
        da = g * hprev
        dmult = jnp.where(start, 0.0, g * gx * xc)
        dla = da * a - dmult * a * a / mult
        dsp_ref[...] += _sum0(-LRU_C * ga * dla)
        dpa = (-LRU_C * sp_ref[...] * dla) * ga * (1.0 - ga)
        dpx = (g * mult * xc) * gx * (1.0 - gx)
        dba_ref[...] += _sum0(dpa)
        dbx_ref[...] += _sum0(dpx)
        dpab, dpxb = _mx(dpa), _mx(dpx)
        dwa_ref[...] += _dot_tn(xb, dpab)
        dwx_ref[...] += _dot_tn(xb, dpxb)
        dxc = g * mult * gx + _dot_nt(dpab, wa_ref[...]) + _dot_nt(dpxb, wx_ref[...])
        dcb_ref[...] += _sum0(dxc)
        dcw3[...] += _sum0(dxc * sh[0])
        dcw2[...] += _sum0(dxc * sh[1])
        dcw1[...] += _sum0(dxc * sh[2])
        dcw0[...] += _sum0(dxc * sh[3])
        ext = jnp.concatenate([dxc, dn_s[...]], axis=0)
        dx_ref[...] = (cw_ref[3:4, :] * dxc + cw_ref[2:3, :] * _shift_up(ext, 1, TILE) + cw_ref[1:2, :] * _shift_up(ext, 2, TILE)
                       + cw_ref[0:1, :] * _shift_up(ext, 3, TILE))
        dn_s[...] = dxc[:8]

    r = functools.partial(_rows_spec, n_tiles=nt, reverse=True)
    vec = _sds((1, D_LRU))
    return _call(
        body, "lru_bwd", nt,
        [r(TILE, D_LRU), _halo_spec(D_LRU, TILE, nt, reverse=True), r(TILE, D_LRU), r(TILE, D_LRU), r(8, D_LRU)] + [_whole()] * 7,
        [r(TILE, D_LRU), r(TILE, D_LRU)] + [_whole()] * 10,
        [_sds((n, D_LRU)), _sds((n, D_LRU)), vec, vec, vec, vec, vec, _sds((D_LRU, D_LRU)), _sds((D_LRU, D_LRU)), vec, vec, vec],
        scratch=[pltpu.VMEM((1, D_LRU), F32), pltpu.VMEM((1, D_LRU), F32), pltpu.VMEM((8, D_LRU), F32)],
    )(xr, xr, gate, dyl, carry, prm['conv_w'], prm['conv_b'], prm['wx'], prm['wa'], prm['bx'], prm['ba'], prm['sp'])


def _normed_parts(ya, ys, yl):
    return jnp.concatenate([ya * _rms(ya), ys * _rms(ys), yl * _rms(yl)], axis=1)


def _mixout_fwd(ya, ys, yl, x0, g_mix, w_out, b_out, g1, b1):
    n = x0.shape[0]
    nt = n // TILE

    def body(ya_ref, ys_ref, yl_ref, x_ref, gm_ref, w_ref, b_ref, g_ref, be_ref, mix_ref, r_ref, x1_ref):
        mixb = _mx(_normed_parts(ya_ref[...], ys_ref[...], yl_ref[...]) * gm_ref[...])
        mix_ref[...] = mixb
        r1 = ALPHA * x_ref[...] + _dot(mixb, w_ref[...]) + b_ref[...]
        r_ref[...] = r1
        xhat, _ = _ln_stats(r1)
        x1_ref[...] = xhat * g_ref[...] + be_ref[...]

    r = functools.partial(_rows_spec, n_tiles=nt)
    return _call(
        body, "mixout_fwd", nt,
        [r(TILE, D_ATTN), r(TILE, D_S5), r(TILE, D_LRU), r(TILE, D)] + [_whole()] * 5,
        [r(TILE, D), r(TILE, D), r(TILE, D)],
        [_sds((n, D), MXU), _sds((n, D)), _sds((n, D))],
    )(ya, ys, yl, x0, g_mix, w_out, b_out, g1, b1)


def _mixout_bwd(dr1, mix, ya, ys, yl, g_mix, w_out):
    n = dr1.shape[0]
    nt = n // TILE

    def body(dr_ref, mix_ref, ya_ref, ys_ref, yl_ref, gm_ref, w_ref, dya_ref, dys_ref, dyl_ref, dw_ref, db_ref, dgm_ref):
        @pl.when(pl.program_id(0) == 0)
        def _():
            for ref in (dw_ref, db_ref, dgm_ref):
                ref[...] = jnp.zeros_like(ref)

        dr = dr_ref[...]
        db_ref[...] += _sum0(dr)
        drb = _mx(dr)
        dw_ref[...] += _dot_tn(mix_ref[...], drb)
        dmix = _dot_nt(drb, w_ref[...])
        parts = (ya_ref[...], ys_ref[...], yl_ref[...])
        dgm_ref[...] += _sum0(dmix * _normed_parts(*parts))
        dn = dmix * gm_ref[...]
        lo = 0
        for y, out in zip(parts, (dya_ref, dys_ref, dyl_ref)):
            w = y.shape[1]
            rs = _rms(y)
            nrm = y * rs
            dnp = dn[:, lo:lo + w]
            out[...] = rs * (dnp - nrm * jnp.mean(dnp * nrm, axis=-1, keepdims=True))
            lo += w

    r = functools.partial(_rows_spec, n_tiles=nt)
    return _call(
        body, "mixout_bwd", nt,
        [r(TILE, D), r(TILE, D), r(TILE, D_ATTN), r(TILE, D_S5), r(TILE, D_LRU), _whole(), _whole()],
        [r(TILE, D_ATTN), r(TILE, D_S5), r(TILE, D_LRU), _whole(), _whole(), _whole()],
        [_sds((n, D_ATTN)), _sds((n, D_S5)), _sds((n, D_LRU)), _sds((D, D)), _sds((1, D)), _sds((1, D))],
    )(dr1, mix, ya, ys, yl, g_mix, w_out)


def _ffn_conv(gp, halo, cw_ref, cb_ref, cs):
    ext = jnp.concatenate([halo, gp], axis=0)
    s1 = _shift_down(ext, 1, TILE)
    s2 = _shift_down(ext, 2, TILE)
    return s1, s2, cb_ref[:, cs] + cw_ref[2:3, cs] * gp + cw_ref[1:2, cs] * s1 + cw_ref[0:1, cs] * s2


def _ffn_fwd(x1, wg, wu, cw, cb, wd, g2, b2):
    n = x1.shape[0]
    nt = n // TILE

    def body(x_ref, wg_ref, wu_ref, cw_ref, cb_ref, wd_ref, g_ref, be_ref, gp_ref, up_ref, r_ref, x2_ref, halo_s):
        @pl.when(pl.program_id(0) == 0)
        def _():
            halo_s[...] = jnp.zeros_like(halo_s)

        x1 = x_ref[...]
        xb = _mx(x1)
        f = jnp.zeros((TILE, D), F32)
        for c in range(D_FF // FF_CHUNK):
            cs = slice(c * FF_CHUNK, (c + 1) * FF_CHUNK)
            gp = _dot(xb, wg_ref[:, cs])
            up = _dot(xb, wu_ref[:, cs])
            gp_ref[:, cs] = gp
            up_ref[:, cs] = up
            _, _, gc = _ffn_conv(gp, halo_s[:, cs], cw_ref, cb_ref, cs)
            halo_s[:, cs] = gp[TILE - 8:]
            f = f + _dot(_mx(gc * _sigmoid(gc) * up), wd_ref[cs, :])
        r2 = ALPHA * x1 + f
        r_ref[...] = r2
        xhat, _ = _ln_stats(r2)
        x2_ref[...] = xhat * g_ref[...] + be_ref[...]

    r = functools.partial(_rows_spec, n_tiles=nt)
    return _call(
        body, "ffn_fwd", nt,
        [r(TILE, D)] + [_whole()] * 7,
        [r(TILE, D_FF), r(TILE, D_FF), r(TILE, D), r(TILE, D)],
        [_sds((n, D_FF)), _sds((n, D_FF)), _sds((n, D)), _sds((n, D))],
        scratch=[pltpu.VMEM((8, D_FF), F32)],
    )(x1, wg, wu, cw, cb, wd, g2, b2)


def _ffn_bwd_down(dx2, r2, g2, gp, up, cw, cb, wd):
    n = dx2.shape[0]
    nt = n // TILE

    def body(dx_ref, r_ref, g_ref, gp_ref, gh_ref, up_ref, cw_ref, cb_ref, wd_ref,
             dr_ref, dgp_ref, dup_ref, dwd_ref, dcw0, dcw1, dcw2, dcb_ref, dg_ref, db_ref, next_s):
        first_tile = pl.program_id(0) == nt - 1

        @pl.when(pl.program_id(0) == 0)
        def _():
            for ref in (dwd_ref, dcw0, dcw1, dcw2, dcb_ref, dg_ref, db_ref, next_s):
                ref[...] = jnp.zeros_like(ref)

        dx2 = dx_ref[...]
        xhat, rstd = _ln_stats(r_ref[...])
        dg_ref[...] += _sum0(dx2 * xhat)
        db_ref[...] += _sum0(dx2)
        dr2 = _ln_bwd(dx2, g_ref[...], xhat, rstd)
        dr_ref[...] = dr2
        dfb = _mx(dr2)
        for c in range(D_FF // FF_CHUNK):
            cs = slice(c * FF_CHUNK, (c + 1) * FF_CHUNK)
            gp = gp_ref[:, cs]
            up = up_ref[:, cs]
            s1, s2, gc = _ffn_conv(gp, jnp.where(first_tile, 0.0, gh_ref[:, cs]), cw_ref, cb_ref, cs)
            sg = _sigmoid(gc)
            silu = gc * sg
            dact = _dot_nt(dfb, wd_ref[cs, :])
            dwd_ref[cs, :] += _dot_tn(_mx(silu * up), dfb)
            dup_ref[:, cs] = _mx(dact * silu)
            dgc = dact * up * (sg * (1.0 + gc * (1.0 - sg)))
            dcb_ref[:, cs] += _sum0(dgc)
            dcw2[:, cs] += _sum0(dgc * gp)
            dcw1[:, cs] += _sum0(dgc * s1)
            dcw0[:, cs] += _sum0(dgc * s2)
            ext = jnp.concatenate([dgc, next_s[:, cs]], axis=0)
            dgp_ref[:, cs] = _mx(cw_ref[2:3, cs] * dgc + cw_ref[1:2, cs] * _shift_up(ext, 1, TILE)
                                 + cw_ref[0:1, cs] * _shift_up(ext, 2, TILE))
            next_s[:, cs] = dgc[:8]

    r = functools.partial(_rows_spec, n_tiles=nt, reverse=True)
    vff = _sds((1, D_FF))
    return _call(
        body, "ffn_bwd_down", nt,
        [r(TILE, D), r(TILE, D), _whole(), r(TILE, D_FF), _halo_spec(D_FF, TILE, nt, reverse=True), r(TILE, D_FF), _whole(), _whole(),
         _whole()],
        [r(TILE, D), r(TILE, D_FF), r(TILE, D_FF)] + [_whole()] * 7,
        [_sds((n, D)), _sds((n, D_FF), MXU), _sds((n, D_FF), MXU), _sds((D_FF, D)), vff, vff, vff, vff, _sds((1, D)), _sds((1, D))],
        scratch=[pltpu.VMEM((8, D_FF), F32)],
    )(dx2, r2, g2, gp, gp, up, cw, cb, wd)


def _ffn_bwd_up(dr2, dgp, dup, x1, r1, g1, wg, wu):
    n = dr2.shape[0]
    rows = TILE_UP
    nt = n // rows

    def body(dr2_ref, dgp_ref, dup_ref, x_ref, r_ref, g_ref, wg_ref, wu_ref, dr1_ref, dwg_ref, dwu_ref, dg_ref, db_ref):
        @pl.when(pl.program_id(0) == 0)
        def _():
            for ref in (dwg_ref, dwu_ref, dg_ref, db_ref):
                ref[...] = jnp.zeros_like(ref)

        dx1 = ALPHA * dr2_ref[...] + _dot_nt(dgp_ref[...], wg_ref[...]) + _dot_nt(dup_ref[...], wu_ref[...])
        xb = _mx(x_ref[...])
        for c in range(D_FF // FF_CHUNK):
            cs = slice(c * FF_CHUNK, (c + 1) * FF_CHUNK)
            dwg_ref[:, cs] += _dot_tn(xb, dgp_ref[:, cs])
            dwu_ref[:, cs] += _dot_tn(xb, dup_ref[:, cs])
        xhat, rstd = _ln_stats(r_ref[...])
        dg_ref[...] += _sum0(dx1 * xhat)
        db_ref[...] += _sum0(dx1)
        dr1_ref[...] = _ln_bwd(dx1, g_ref[...], xhat, rstd)

    r = functools.partial(_rows_spec, n_tiles=nt)
    return _call(
        body, "ffn_bwd_up", nt,
        [r(rows, D), r(rows, D_FF), r(rows, D_FF), r(rows, D), r(rows, D), _whole(), _whole(), _whole()],
        [r(rows, D), _whole(), _whole(), _whole(), _whole()],
        [_sds((n, D)), _sds((D, D_FF)), _sds((D, D_FF)), _sds((1, D)), _sds((1, D))],
    )(dr2, dgp, dup, x1, r1, g1, wg, wu)


def _loss_head(y, target):
    n = y.shape[0]
    nt = n // TILE

    def body(y_ref, t_ref, loss_ref, dy_ref):
        @pl.when(pl.program_id(0) == 0)
        def _():
            loss_ref[...] = jnp.zeros_like(loss_ref)

        e = y_ref[...] - t_ref[...]
        dy_ref[...] = e * (1.0 / D)
        loss_ref[...] += _sum0(jnp.sum(e * e, axis=1, keepdims=True)) * (0.5 / D)

    r = functools.partial(_rows_spec, n_tiles=nt)
    return _call(body, "loss_head", nt, [r(TILE, D), r(TILE, D)], [_whole(), r(TILE, D)], [_sds((1, 1)), _sds((n, D))])(y, target)


def _place():
    x, y, c = lax.axis_index("x"), lax.axis_index("y"), lax.axis_index("c")
    return x, y, c, 4 * x + 2 * y + c


def _peer(x, y, c, k):
    px, py, pc = x ^ ((k >> 2) & 1), y ^ ((k >> 1) & 1), c ^ (k & 1)
    return (px, py, pc), 4 * px + 2 * py + pc


def _exchange(src, name, scatter):
    shape = src.shape[-2:]

    def body(src_ref, out_ref, send_sems, recv_sems, local_sem):
        x, y, c, me = _place()
        mine = pltpu.make_async_copy(src_ref.at[me] if scatter else src_ref, out_ref.at[me], local_sem)
        mine.start()
        copies = []
        for k in range(1, N_DEV):
            peer, peer_slot = _peer(x, y, c, k)
            copies.append(pltpu.make_async_remote_copy(
                src_ref=src_ref.at[peer_slot] if scatter else src_ref, dst_ref=out_ref.at[me],
                send_sem=send_sems.at[k], recv_sem=recv_sems.at[k], device_id=peer, device_id_type=pl.DeviceIdType.MESH))
        for cp in copies:
            cp.start()
        for k in range(1, N_DEV):
            peer, peer_slot = _peer(x, y, c, k)
            pltpu.make_async_remote_copy(
                src_ref=src_ref.at[me] if scatter else src_ref, dst_ref=out_ref.at[peer_slot],
                send_sem=send_sems.at[k], recv_sem=recv_sems.at[k], device_id=peer, device_id_type=pl.DeviceIdType.MESH).wait_recv()
        for cp in copies:
            cp.wait_send()
        mine.wait()

    return pl.pallas_call(
        body, name=name, out_shape=_sds((N_DEV,) + shape, src.dtype),
        in_specs=[pl.BlockSpec(memory_space=pl.ANY)], out_specs=pl.BlockSpec(memory_space=pl.ANY),
        scratch_shapes=[pltpu.SemaphoreType.DMA((N_DEV,)), pltpu.SemaphoreType.DMA((N_DEV,)), pltpu.SemaphoreType.DMA(())],
    )(src)


def _reduce_adamw(parts, w, m, v):
    rows = w.shape[0]
    nt = rows // PACK_TILE
    c1 = 1.0 - ADAM_B1 ** ADAM_STEP
    c2 = 1.0 - ADAM_B2 ** ADAM_STEP

    def body(p_ref, w_ref, m_ref, v_ref, g_out, d_out, m_out, v_out):
        g = p_ref[0]
        for s in range(1, N_DEV):
            g = g + p_ref[s]
        m_new = ADAM_B1 * m_ref[...] + (1.0 - ADAM_B1) * g
        v_new = ADAM_B2 * v_ref[...] + (1.0 - ADAM_B2) * (g * g)
        g_out[...] = g
        m_out[...] = m_new
        v_out[...] = v_new
        d_out[...] = -ADAM_LR * ((m_new / c1) / (jnp.sqrt(v_new / c2) + ADAM_EPS) + ADAM_WD * w_ref[...])

    r = _rows_spec(PACK_TILE, LANES, nt)
    out = _sds((rows, LANES))
    return _call(
        body, "reduce_adamw", nt,
        [pl.BlockSpec((N_DEV, PACK_TILE, LANES), lambda i: (0, i, 0)), r, r, r], [r, r, r, r], [out, out, out, out],
    )(parts, w, m, v)


def _pack_rows(a, lead=0):
    head = a.shape[:lead]
    flat = a.reshape(head + (-1,))
    size = flat.shape[-1]
    rows = -(-size // (16 * LANES)) * 16
    flat = jnp.pad(flat, [(0, 0)] * lead + [(0, rows * LANES - size)])
    return flat.reshape(head + (rows, LANES))


def _packed_rows(shape):
    return -(-math.prod(shape) // (16 * LANES)) * 16


def _to_blocks(full, axis):
    l, a, b = full.shape
    if axis == 2:
        return full.reshape(l, a, N_DEV, b // N_DEV).transpose(2, 0, 1, 3)
    return full.reshape(l, N_DEV, a // N_DEV, b).transpose(1, 0, 2, 3)


def _from_blocks(blocks, axis):
    _, l, a, b = blocks.shape
    if axis == 2:
        return blocks.transpose(1, 2, 0, 3).reshape(l, a, N_DEV * b)
    return blocks.transpose(1, 0, 2, 3).reshape(l, N_DEV * a, b)


def _gather_weights(local):
    segs, meta = [], []
    for name in SHARDED:
        blk = local[name]
        if name in GATHER_F32:
            bits = lax.bitcast_convert_type(blk, MXU)
        else:
            bits = _mx(blk)
        seg = _pack_rows(bits)
        meta.append((name, blk.shape, bits.shape, seg.shape[0]))
        segs.append(seg)
    gathered = _exchange(jnp.concatenate(segs, axis=0), "gather_weights", scatter=False)
    out, lo = {}, 0
    for name, shape, bits_shape, rows in meta:
        seg = gathered[:, lo:lo + rows].reshape(N_DEV, -1)[:, :math.prod(bits_shape)].reshape((N_DEV,) + bits_shape)
        if name in GATHER_F32:
            seg = lax.bitcast_convert_type(seg, F32)
        out[name] = _from_blocks(seg, SHARD_AXIS[name])
        lo += rows
    return out


def _s5_discretize(a_re, a_im, log_dt, b_re, b_im):
    lam_re = jnp.minimum(a_re, -1e-4)
    lam_im = a_im
    dt = jnp.exp(log_dt)[:, None]
    decay = jnp.exp(dt * lam_re)
    ang = dt * lam_im
    abar_re = decay * jnp.cos(ang)
    abar_im = decay * jnp.sin(ang)
    den = jnp.square(lam_re) + jnp.square(lam_im)
    nr = abar_re - 1.0
    ni = abar_im
    coef_re = (nr * lam_re + ni * lam_im) / den
    coef_im = (ni * lam_re - nr * lam_im) / den
    bbar_re = coef_re[..., None] * b_re - coef_im[..., None] * b_im
    bbar_im = coef_re[..., None] * b_im + coef_im[..., None] * b_re
    return abar_re, abar_im, bbar_re, bbar_im


def _complex_powers(ar, ai, count):
    def combine(e1, e2):
        return e2[0] * e1[0] - e2[1] * e1[1], e2[0] * e1[1] + e2[1] * e1[0]

    shape = (count,) + ar.shape
    return lax.associative_scan(combine, (jnp.broadcast_to(ar, shape), jnp.broadcast_to(ai, shape)), axis=0)


_EYE16 = functools.partial(jnp.eye, 16, dtype=F32)


def _s5_params(p, l):
    disc, disc_vjp = jax.vjp(_s5_discretize, p['s5_a_re'][l], p['s5_a_im'][l], p['s5_log_dt'][l], p['s5_b_re'][l], p['s5_b_im'][l])
    abar_re, abar_im, bbar_re, bbar_im = disc
    ar, ai = abar_re.reshape(N_STATE), abar_im.reshape(N_STATE)
    ap_re, ap_im = _complex_powers(ar, ai, TILE)
    steps = TILE.bit_length() - 1
    prm = {
        'b_re': _mx(jnp.einsum('gpc,gh->gchp', bbar_re, _EYE16()).reshape(D_S5, N_STATE)),
        'b_im': _mx(jnp.einsum('gpc,gh->gchp', bbar_im, _EYE16()).reshape(D_S5, N_STATE)),
        'c_re': _mx(jnp.einsum('gcp,gh->gphc', p['s5_c_re'][l], _EYE16()).reshape(N_STATE, D_S5)),
        'c_im': _mx(jnp.einsum('gcp,gh->gphc', p['s5_c_im'][l], _EYE16()).reshape(N_STATE, D_S5)),
        'pw_re': jnp.stack([ap_re[2 ** k - 1] for k in range(steps)]),
        'pw_im': jnp.stack([ap_im[2 ** k - 1] for k in range(steps)]),
        'ap_re': ap_re, 'ap_im': ap_im, 'rp_re': ap_re[::-1], 'rp_im': -ap_im[::-1],
        'd': p['s5_d'][l][None, :], 'glu_w': p['s5_glu_w'][l], 'glu_b': p['s5_glu_b'][l][None, :],
    }
    return prm, disc_vjp


def _lru_params(p, l):
    eye4 = jnp.eye(4, dtype=F32)
    return {
        'conv_w': p['lru_conv_w'][l], 'conv_b': p['lru_conv_b'][l][None, :],
        'wx': _mx(jnp.einsum('hij,hk->hikj', p['lru_wx'][l], eye4).reshape(D_LRU, D_LRU)),
        'wa': _mx(jnp.einsum('hij,hk->hikj', p['lru_wa'][l], eye4).reshape(D_LRU, D_LRU)),
        'bx': p['lru_bx'][l][None, :], 'ba': p['lru_ba'][l][None, :],
        'sp': jax.nn.softplus(-p['lru_a_param'][l])[None, :],
    }


def _rope_tables(n):
    inv_freq = ROPE_THETA ** (-jnp.arange(0, 64, 2, dtype=F32) / 64)
    ang = jnp.arange(n, dtype=F32)[:, None] * inv_freq[None, :]
    cos, sin = jnp.cos(ang), jnp.sin(ang)
    return jnp.concatenate([cos, cos, cos, cos], axis=1), jnp.concatenate([-sin, sin, -sin, sin], axis=1)


def _sink_rows(sinks):
    return jnp.broadcast_to(sinks.reshape(4, 2, 1), (4, 2, 256)).reshape(4, 512)


def _local_step(x, target, p):
    n = x.shape[0]
    cos_t, sin_t = _rope_tables(n)
    row = lambda a: a[None, :]
    saved = []
    h = x
    for l in range(DEPTH):
        s = {'x0': h}
        s['w_in'] = p['w_in'][l]
        s['q'], s['k'], s['v'], s['u'], s['xr'], s['gate'] = _inproj_fwd(h, s['w_in'], row(p['b_in'][l]), cos_t, sin_t)
        s['sinks'] = _sink_rows(p['attn_sinks'][l])
        s['ya'] = _attn_fwd(s['q'], s['k'], s['v'], s['sinks'])
        s['s5'], s['s5_vjp'] = _s5_params(p, l)
        s['ys'], s['s5_cr'], s['s5_ci'] = _s5_fwd(s['u'], s['s5'])
        s['lru'] = _lru_params(p, l)
        s['yl'], s['lru_c'] = _lru_fwd(s['xr'], s['gate'], s['lru'])
        s['mix'], s['r1'], s['x1'] = _mixout_fwd(s['ya'], s['ys'], s['yl'], h, row(p['mix_norm_g'][l]), p['w_out'][l],
                                                 row(p['b_out'][l]), row(p['ln1_g'][l]), row(p['ln1_b'][l]))
        s['gp'], s['up'], s['r2'], h = _ffn_fwd(s['x1'], p['ffn_w_gate'][l], p['ffn_w_up'][l], p['ffn_conv_w'][l],
                                                row(p['ffn_conv_b'][l]), p['ffn_w_down'][l], row(p['ln2_g'][l]), row(p['ln2_b'][l]))
        saved.append(s)
    loss, dh = _loss_head(h, target)

    grads = {name: [None] * DEPTH for name in WEIGHTS}
    for l in reversed(range(DEPTH)):
        s = saved[l]
        g = {}
        (dr2, dgp, dup, g['ffn_w_down'], cw0, cw1, cw2, dcb, dg2, db2) = _ffn_bwd_down(
            dh, s['r2'], row(p['ln2_g'][l]), s['gp'], s['up'], p['ffn_conv_w'][l], row(p['ffn_conv_b'][l]), p['ffn_w_down'][l])
        g['ffn_conv_w'] = jnp.concatenate([cw0, cw1, cw2], axis=0)
        g['ffn_conv_b'], g['ln2_g'], g['ln2_b'] = dcb[0], dg2[0], db2[0]
        dr1, g['ffn_w_gate'], g['ffn_w_up'], dg1, db1 = _ffn_bwd_up(
            dr2, dgp, dup, s['x1'], s['r1'], row(p['ln1_g'][l]), p['ffn_w_gate'][l], p['ffn_w_up'][l])
        g['ln1_g'], g['ln1_b'] = dg1[0], db1[0]
        dya, dys, dyl, g['w_out'], dbo, dgm = _mixout_bwd(dr1, s['mix'], s['ya'], s['ys'], s['yl'], row(p['mix_norm_g'][l]),
                                                         p['w_out'][l])
        g['b_out'], g['mix_norm_g'] = dbo[0], dgm[0]

        dxr, dgate, lw0, lw1, lw2, lw3, lcb, dwx, dwa, dbx, dba, dsp = _lru_bwd(s['xr'], s['gate'], dyl, s['lru_c'], s['lru'])
        g['lru_conv_w'] = jnp.concatenate([lw0, lw1, lw2, lw3], axis=0)
        g['lru_conv_b'], g['lru_bx'], g['lru_ba'] = lcb[0], dbx[0], dba[0]
        g['lru_wx'] = jnp.einsum('hihj->hij', dwx.reshape(4, 64, 4, 64))
        g['lru_wa'] = jnp.einsum('hihj->hij', dwa.reshape(4, 64, 4, 64))
        g['lru_a_param'] = -dsp[0] * jax.nn.sigmoid(-p['lru_a_param'][l])

        du, dbr, dbi, dcr, dci, dar, dai, dd, g['s5_glu_w'], dgb = _s5_bwd(s['u'], dys, s['s5_cr'], s['s5_ci'], s['s5'])
        g['s5_c_re'] = jnp.einsum('gpgc->gcp', dcr.reshape(16, 64, 16, 16))
        g['s5_c_im'] = jnp.einsum('gpgc->gcp', dci.reshape(16, 64, 16, 16))
        g['s5_d'], g['s5_glu_b'] = dd[0], dgb[0]
        g['s5_a_re'], g['s5_a_im'], g['s5_log_dt'], g['s5_b_re'], g['s5_b_im'] = s['s5_vjp']((
            dar.reshape(16, 64), dai.reshape(16, 64), jnp.einsum('gcgp->gpc', dbr.reshape(16, 16, 16, 64)),
            jnp.einsum('gcgp->gpc', dbi.reshape(16, 16, 16, 64))))

        dq, dk, dv, dsink = _attn_bwd(s['q'], s['k'], s['v'], s['sinks'], s['ya'], dya)
        g['attn_sinks'] = dsink.reshape(4, 2, 256)[:, :, 0].reshape(8)
        dh, g['w_in'], dbin = _inproj_bwd(dq, dk, dv, du, dxr, dgate, cos_t, sin_t, s['x0'], dr1, s['w_in'])
        g['b_in'] = dbin[0]
        for name in WEIGHTS:
            grads[name][l] = g[name]
    return loss, dh, {name: jnp.stack(grads[name]) for name in WEIGHTS}


def kernel(x, w_in, b_in, attn_sinks, s5_a_re, s5_a_im, s5_b_re, s5_b_im, s5_c_re, s5_c_im, s5_d, s5_log_dt, s5_glu_w, s5_glu_b, lru_conv_w, lru_conv_b, lru_wx, lru_bx, lru_wa, lru_ba, lru_a_param, mix_norm_g, w_out, b_out, ln1_g, ln1_b, ffn_w_gate, ffn_w_up, ffn_conv_w, ffn_conv_b, ffn_w_down, ln2_g, ln2_b, loss_target, m_w_in, m_b_in, m_attn_sinks, m_s5_a_re, m_s5_a_im, m_s5_b_re, m_s5_b_im, m_s5_c_re, m_s5_c_im, m_s5_d, m_s5_log_dt, m_s5_glu_w, m_s5_glu_b, m_lru_conv_w, m_lru_conv_b, m_lru_wx, m_lru_bx, m_lru_wa, m_lru_ba, m_lru_a_param, m_mix_norm_g, m_w_out, m_b_out, m_ln1_g, m_ln1_b, m_ffn_w_gate, m_ffn_w_up, m_ffn_conv_w, m_ffn_conv_b, m_ffn_w_down, m_ln2_g, m_ln2_b, v_w_in, v_b_in, v_attn_sinks, v_s5_a_re, v_s5_a_im, v_s5_b_re, v_s5_b_im, v_s5_c_re, v_s5_c_im, v_s5_d, v_s5_log_dt, v_s5_glu_w, v_s5_glu_b, v_lru_conv_w, v_lru_conv_b, v_lru_wx, v_lru_bx, v_lru_wa, v_lru_ba, v_lru_a_param, v_mix_norm_g, v_w_out, v_b_out, v_ln1_g, v_ln1_b, v_ffn_w_gate, v_ffn_w_up, v_ffn_conv_w, v_ffn_conv_b, v_ffn_w_down, v_ln2_g, v_ln2_b):
    given = dict(locals())
    local_w = {name: given[name] for name in WEIGHTS}

    whole = dict(local_w)
    whole.update(_gather_weights({name: local_w[name] for name in SHARDED}))
    loss, grad_x, grads = _local_step(x[0], loss_target[0], whole)

    order = SHARDED + REPLICATED
    send = jnp.concatenate(
        [_pack_rows(_to_blocks(grads[name], SHARD_AXIS[name]), lead=1) for name in SHARDED]
        + [jnp.broadcast_to(_pack_rows(grads[name]), (N_DEV, _packed_rows(grads[name].shape), LANES)) for name in REPLICATED], axis=1)
    rows = send.shape[1]
    pad = -rows % PACK_TILE
    send = jnp.pad(send, ((0, 0), (0, pad), (0, 0)))
    parts = _exchange(send, "exchange_grads", scatter=True)

    def packed(prefix):
        a = jnp.concatenate([_pack_rows(given[prefix + name]) for name in order], axis=0)
        return jnp.pad(a, ((0, pad), (0, 0)))

    outs = _reduce_adamw(parts, packed(''), packed('m_'), packed('v_'))

    def unpack(a):
        res, lo = {}, 0
        for name in order:
            shape = local_w[name].shape
            res[name] = a[lo:lo + _packed_rows(shape)].reshape(-1)[:math.prod(shape)].reshape(shape)
            lo += _packed_rows(shape)
        return [res[name] for name in WEIGHTS]

    total = lax.psum(loss[0, 0], ("x", "y", "c"))
    return (total, grad_x[None], *unpack(outs[0]), *unpack(outs[1]), *unpack(outs[2]), *unpack(outs[3]))
```

```python
import functools
import math

import jax
import jax.numpy as jnp
from jax import lax
from jax.experimental import pallas as pl
from jax.experimental.pallas import tpu as pltpu

F32 = jnp.float32
MXU = jnp.bfloat16

N_DEV = 8
DEPTH = 4
D = 1024
D_ATTN, D_KV, D_S5, D_LRU = 512, 128, 256, 256
D_IN = 1536
D_FF = 2816
FF_CHUNK = 256
N_STATE = 1024
LANES = 1024
ALPHA = (2 * DEPTH) ** 0.25
LN_EPS = 1e-5
RMS_EPS = 1e-6
LRU_C = 8.0
ROPE_THETA = 10000.0
ADAM_LR, ADAM_B1, ADAM_B2, ADAM_EPS, ADAM_WD, ADAM_STEP = 0.001, 0.9, 0.999, 1e-08, 0.01, 10

TILE = 256
N_SEG = 8
SEG = TILE // N_SEG
TILE_Q = 512
TILE_BIG = 512
ATTN_BLOCK = 128
PACK_TILE = 256
VMEM_MB = 56

WEIGHTS = ['w_in', 'b_in', 'attn_sinks', 's5_a_re', 's5_a_im', 's5_b_re', 's5_b_im', 's5_c_re', 's5_c_im', 's5_d', 's5_log_dt',
           's5_glu_w', 's5_glu_b', 'lru_conv_w', 'lru_conv_b', 'lru_wx', 'lru_bx', 'lru_wa', 'lru_ba', 'lru_a_param', 'mix_norm_g',
           'w_out', 'b_out', 'ln1_g', 'ln1_b', 'ffn_w_gate', 'ffn_w_up', 'ffn_conv_w', 'ffn_conv_b', 'ffn_w_down', 'ln2_g', 'ln2_b']
SHARD_AXIS = {'w_in': 2, 's5_glu_w': 1, 'lru_conv_w': 2, 'w_out': 1, 'ffn_w_gate': 2, 'ffn_w_up': 2, 'ffn_conv_w': 2,
              'ffn_w_down': 1}
SHARDED = [n for n in WEIGHTS if n in SHARD_AXIS]
REPLICATED = [n for n in WEIGHTS if n not in SHARD_AXIS]
GATHER_F32 = ('lru_conv_w', 'ffn_conv_w')


def _dot(a, b):
    return jnp.dot(a, b, preferred_element_type=F32)


def _dot_nt(a, b):
    return lax.dot_general(a, b, (((1,), (1,)), ((), ())), preferred_element_type=F32)


def _dot_tn(a, b):
    return lax.dot_general(a, b, (((0,), (0,)), ((), ())), preferred_element_type=F32)


def _mx(a):
    return a.astype(MXU)


_GELU_C = math.sqrt(2.0 / math.pi)


def _gelu(x):
    th = jnp.tanh(_GELU_C * (x + 0.044715 * x * x * x))
    return 0.5 * x * (1.0 + th)


def _gelu_grad(x):
    th = jnp.tanh(_GELU_C * (x + 0.044715 * x * x * x))
    return 0.5 * (1.0 + th) + 0.5 * x * (1.0 - th * th) * _GELU_C * (1.0 + 3.0 * 0.044715 * x * x)


def _sigmoid(x):
    return 1.0 / (1.0 + jnp.exp(-x))


def _ln_stats(r):
    mu = jnp.mean(r, axis=-1, keepdims=True)
    xc = r - mu
    var = jnp.mean(xc * xc, axis=-1, keepdims=True)
    rstd = lax.rsqrt(var + LN_EPS)
    return xc * rstd, rstd


def _ln_bwd(dy, g, xhat, rstd):
    dxh = dy * g
    return rstd * (dxh - jnp.mean(dxh, axis=-1, keepdims=True) - xhat * jnp.mean(dxh * xhat, axis=-1, keepdims=True))


def _rms(y):
    return lax.rsqrt(jnp.mean(y * y, axis=-1, keepdims=True) + RMS_EPS)


def _sum0(a):
    return jnp.sum(a, axis=0, keepdims=True)


def _row_iota(shape):
    return lax.broadcasted_iota(jnp.int32, shape, 0)


def _shift_down(ext, j, rows):
    return pltpu.roll(ext, j, 0)[8:8 + rows]


def _shift_up(ext, j, rows):
    return pltpu.roll(ext, ext.shape[0] - j, 0)[:rows]


def _swap_halves(t):
    w = t.shape[1]
    lane = lax.broadcasted_iota(jnp.int32, t.shape, 1)
    return jnp.where((lane & 32) == 0, pltpu.roll(t, w - 32, 1), pltpu.roll(t, 32, 1))


def _rope(t, cos, sin_signed):
    return t * cos + _swap_halves(t) * sin_signed


def _rope_t(d, cos, sin_signed):
    return d * cos + _swap_halves(d * sin_signed)


def _cmul_add(ar, ai, xr, xi, yr, yi):
    return ar * xr - ai * xi + yr, ar * xi + ai * xr + yi


def _seg_rows(k):
    return slice(N_SEG * k, N_SEG * (k + 1))


def _permute_rows(perm, x):
    hi = _mx(x)
    rest = x - hi.astype(F32)
    mid = _mx(rest)
    lo = _mx(rest - mid.astype(F32))
    return _dot(perm, hi) + _dot(perm, mid) + _dot(perm, lo)


def _cscan(sr, si, tab, cin_r, cin_i, reverse):
    sgn = -1.0 if reverse else 1.0
    pw_re, pw_im, dbl_re, dbl_im = tab['pw_re'], tab['pw_im'], tab['dbl_re'], tab['dbl_im']
    ar, ai = pw_re[0:1, :], sgn * pw_im[0:1, :]
    shape = (N_SEG, sr.shape[1])
    hr = hi = jnp.zeros(shape, F32)
    for k in (range(SEG - 1, -1, -1) if reverse else range(SEG)):
        hr, hi = _cmul_add(ar, ai, hr, hi, sr[_seg_rows(k), :], si[_seg_rows(k), :])
        sr[_seg_rows(k), :] = hr
        si[_seg_rows(k), :] = hi
    sub = _row_iota(shape)

    def shifted(v, d):
        if reverse:
            return jnp.where(sub < N_SEG - d, pltpu.roll(v, N_SEG - d, 0), 0.0)
        return jnp.where(sub >= d, pltpu.roll(v, d, 0), 0.0)

    fr, fi = hr, hi
    for j, d in enumerate((1, 2, 4)):
        fr, fi = _cmul_add(dbl_re[j:j + 1, :], sgn * dbl_im[j:j + 1, :], shifted(fr, d), shifted(fi, d), fr, fi)
    seg_re, seg_im = (tab['segr_re'], tab['segr_im']) if reverse else (tab['seg_re'], tab['seg_im'])
    cr, ci = _cmul_add(seg_re[...], sgn * seg_im[...], cin_r, cin_i, shifted(fr, 1), shifted(fi, 1))
    nr, ni = _cmul_add(dbl_re[0:1, :], sgn * dbl_im[0:1, :], cr, ci, hr, hi)
    for k in range(SEG):
        j = SEG - 1 - k if reverse else k
        xr, xi = _cmul_add(pw_re[j:j + 1, :], sgn * pw_im[j:j + 1, :], cr, ci, sr[_seg_rows(k), :], si[_seg_rows(k), :])
        sr[_seg_rows(k), :] = xr
        si[_seg_rows(k), :] = xi
    edge = slice(0, 1) if reverse else slice(N_SEG - 1, N_SEG)
    return nr[edge], ni[edge]


def _rscan(a, b, reverse):
    rows = a.shape[0]
    row = _row_iota(a.shape)
    s = 1
    while s < rows:
        if reverse:
            keep = row < rows - s
            sa = jnp.where(keep, pltpu.roll(a, rows - s, 0), 1.0)
            sb = jnp.where(keep, pltpu.roll(b, rows - s, 0), 0.0)
        else:
            keep = row >= s
            sa = jnp.where(keep, pltpu.roll(a, s, 0), 1.0)
            sb = jnp.where(keep, pltpu.roll(b, s, 0), 0.0)
        b = b + a * sb
        a = a * sa
        s *= 2
    return a, b


def _whole():
    return pl.BlockSpec(memory_space=pltpu.VMEM)


def _rows_spec(rows, cols, n_tiles, reverse=False):
    if reverse:
        return pl.BlockSpec((rows, cols), lambda i: (n_tiles - 1 - i, 0))
    return pl.BlockSpec((rows, cols), lambda i: (i, 0))


def _halo_spec(cols, tile_rows, n_tiles, reverse=False):
    per = tile_rows // 8
    if reverse:
        return pl.BlockSpec((8, cols), lambda i: (jnp.maximum((n_tiles - 1 - i) * per - 1, 0), 0))
    return pl.BlockSpec((8, cols), lambda i: (jnp.maximum(i * per - 1, 0), 0))


def _call(body, name, n_tiles, in_specs, out_specs, out_shape, scratch=()):
    return pl.pallas_call(
        body, name=name, grid=(n_tiles,), in_specs=in_specs, out_specs=out_specs, out_shape=out_shape,
        scratch_shapes=list(scratch),
        compiler_params=pltpu.CompilerParams(dimension_semantics=("arbitrary",), vmem_limit_bytes=VMEM_MB << 20))


def _sds(shape, dtype=F32):
    return jax.ShapeDtypeStruct(shape, dtype)


def _inproj_fwd(x, w, b, cos_t, sin_t):
    n = x.shape[0]
    nt = n // TILE

    def body(x_ref, w_ref, b_ref, c_ref, s_ref, q_ref, k_ref, v_ref, u_ref, xr_ref, g_ref):
        p = _dot(_mx(x_ref[...]), w_ref[...]) + b_ref[...]
        cos, sin = c_ref[...], s_ref[...]
        q_ref[...] = _mx(_rope(p[:, :D_ATTN], jnp.tile(cos, (1, 4)), jnp.tile(sin, (1, 4))))
        k_ref[...] = _mx(_rope(p[:, 512:640], cos, sin))
        v_ref[...] = _mx(p[:, 640:768])
        u_ref[...] = p[:, 768:1024]
        xr_ref[...] = p[:, 1024:1280]
        g_ref[...] = p[:, 1280:1536]

    r = functools.partial(_rows_spec, n_tiles=nt)
    return _call(
        body, "inproj_fwd", nt,
        [r(TILE, D), _whole(), _whole(), r(TILE, 128), r(TILE, 128)],
        [r(TILE, D_ATTN), r(TILE, D_KV), r(TILE, D_KV), r(TILE, D_S5), r(TILE, D_LRU), r(TILE, D_LRU)],
        [_sds((n, D_ATTN), MXU), _sds((n, D_KV), MXU), _sds((n, D_KV), MXU), _sds((n, D_S5)), _sds((n, D_LRU)), _sds((n, D_LRU))],
    )(x, w, b, cos_t, sin_t)


def _inproj_bwd(dq, dk, dv, du, dxr, dgate, cos_t, sin_t, x0, dr1, w):
    n = x0.shape[0]
    nt = n // TILE

    def body(dq_ref, dk_ref, dv_ref, du_ref, dxr_ref, dg_ref, c_ref, s_ref, x_ref, dr_ref, w_ref, dx_ref, dw_ref, db_ref):
        @pl.when(pl.program_id(0) == 0)
        def _():
            dw_ref[...] = jnp.zeros_like(dw_ref)
            db_ref[...] = jnp.zeros_like(db_ref)

        cos, sin = c_ref[...], s_ref[...]
        dtq = _rope_t(dq_ref[...], jnp.tile(cos, (1, 4)), jnp.tile(sin, (1, 4)))
        dtk = _rope_t(dk_ref[...], cos, sin)
        dp = jnp.concatenate([dtq, dtk, dv_ref[...], du_ref[...], dxr_ref[...], dg_ref[...]], axis=1)
        db_ref[...] += _sum0(dp)
        dpb = _mx(dp)
        dw_ref[...] += _dot_tn(_mx(x_ref[...]), dpb)
        dx_ref[...] = ALPHA * dr_ref[...] + _dot(dpb, w_ref[...])

    r = functools.partial(_rows_spec, n_tiles=nt)
    return _call(
        body, "inproj_bwd", nt,
        [r(TILE, D_ATTN), r(TILE, D_KV), r(TILE, D_KV), r(TILE, D_S5), r(TILE, D_LRU), r(TILE, D_LRU), r(TILE, 128), r(TILE, 128),
         r(TILE, D), r(TILE, D), _whole()],
        [r(TILE, D), _whole(), _whole()],
        [_sds((n, D)), _sds((D, D_IN)), _sds((1, D_IN))],
    )(dq, dk, dv, du, dxr, dgate, cos_t, sin_t, x0, dr1, w)


def _kv_variants(t, lo):
    tr = pltpu.roll(t, 64, 1)
    out = []
    for j in range(2):
        first = jnp.where(lo, t if j == 0 else tr, 0.0)
        second = jnp.where(lo, 0.0, tr if j == 0 else t)
        out.append(_mx(jnp.concatenate([first, second], axis=0)))
    return out


def _kv_collect(x0, x1, lo):
    a = x0[:256] + pltpu.roll(x0[256:], 64, 1)
    b = pltpu.roll(x1[:256], 64, 1) + x1[256:]
    return jnp.where(lo, a, b)


def _attn_probs(s, sink_ref):
    out = []
    for hp in range(2):
        sh = s[:, hp * 256:(hp + 1) * 256]
        sink = sink_ref[hp]
        m = jnp.maximum(jnp.max(sh, axis=1, keepdims=True), sink)
        p = jnp.exp(sh - m)
        es = jnp.exp(sink - m)
        inv = 1.0 / (jnp.sum(p, axis=1, keepdims=True) + es)
        out.append((p * inv, es * inv))
    return out


def _attn_scores(q_ref, k_ref, v_ref, nb):
    lo = lax.broadcasted_iota(jnp.int32, (256, 128), 1) < 64
    kcats, vcats, kstarts, parts = [], [], [], []
    for b in range(nb):
        block = pl.program_id(0) * nb + b
        kstart = pl.multiple_of(jnp.maximum(block - 1, 0) * ATTN_BLOCK, ATTN_BLOCK)
        kcat = _kv_variants(k_ref[pl.ds(kstart, 256), :].astype(F32), lo)
        kcats.append(kcat)
        vcats.append(_kv_variants(v_ref[pl.ds(kstart, 256), :].astype(F32), lo))
        kstarts.append(kstart)
        valid = _attn_mask(block, kstart)
        for i in range(4):
            s = _dot_nt(q_ref[b * ATTN_BLOCK:(b + 1) * ATTN_BLOCK, i * 128:(i + 1) * 128], kcat[i // 2]) * 0.125
            parts.append(jnp.where(valid, s, -jnp.inf))
    return jnp.concatenate(parts, axis=0), kcats, vcats, kstarts


def _attn_mask(block, kstart):
    col = lax.broadcasted_iota(jnp.int32, (ATTN_BLOCK, 512), 1)
    row = lax.broadcasted_iota(jnp.int32, (ATTN_BLOCK, 512), 0)
    diff = (block * ATTN_BLOCK + row) - (kstart + (col & 255))
    return (diff >= 0) & (diff < ATTN_BLOCK)


def _attn_fwd(q, k, v, sink_cols):
    n = q.shape[0]
    nt = n // TILE_Q
    nb = TILE_Q // ATTN_BLOCK

    def body(q_ref, k_ref, v_ref, s_ref, o_ref):
        s, _, vcats, _ = _attn_scores(q_ref, k_ref, v_ref, nb)
        (p0, _), (p1, _) = _attn_probs(s, s_ref)
        pb = _mx(jnp.concatenate([p0, p1], axis=1))
        for b in range(nb):
            for i in range(4):
                unit = (b * 4 + i) * ATTN_BLOCK
                o_ref[b * ATTN_BLOCK:(b + 1) * ATTN_BLOCK, i * 128:(i + 1) * 128] = _dot(pb[unit:unit + ATTN_BLOCK], vcats[b][i // 2])

    return _call(
        body, "attn_fwd", nt,
        [_rows_spec(TILE_Q, D_ATTN, nt), _whole(), _whole(), _whole()],
        _rows_spec(TILE_Q, D_ATTN, nt), _sds((n, D_ATTN)),
    )(q, k, v, sink_cols)


def _attn_bwd(q, k, v, sink_cols, o, do):
    n = q.shape[0]
    nt = n // TILE_Q
    nb = TILE_Q // ATTN_BLOCK

    def body(q_ref, k_ref, v_ref, s_ref, o_ref, do_ref, dq_ref, dk_ref, dv_ref, ds_ref):
        @pl.when(pl.program_id(0) == 0)
        def _():
            dk_ref[...] = jnp.zeros_like(dk_ref)
            dv_ref[...] = jnp.zeros_like(dv_ref)
            ds_ref[...] = jnp.zeros_like(ds_ref)

        lo = lax.broadcasted_iota(jnp.int32, (256, 128), 1) < 64
        s, kcats, vcats, kstarts = _attn_scores(q_ref, k_ref, v_ref, nb)
        probs = _attn_probs(s, s_ref)
        do = do_ref[...]
        dob = _mx(do)
        od = do * o_ref[...]
        lo_q = (lax.broadcasted_iota(jnp.int32, od.shape, 1) & 64) == 0
        od_head = (jnp.where(lo_q, od, 0.0), jnp.where(lo_q, 0.0, od))
        units = [(b, i) for b in range(nb) for i in range(4)]

        def tile_part(a, b, i):
            return a[b * ATTN_BLOCK:(b + 1) * ATTN_BLOCK, i * 128:(i + 1) * 128]

        dp = jnp.concatenate([_dot_nt(tile_part(dob, b, i), vcats[b][i // 2]) for b, i in units], axis=0)
        ds = []
        for hp in range(2):
            p, p_sink = probs[hp]
            delta = jnp.concatenate([jnp.sum(tile_part(od_head[hp], b, i), axis=1, keepdims=True) for b, i in units], axis=0)
            ds.append(p * (dp[:, hp * 256:(hp + 1) * 256] - delta) * 0.125)
            t = p_sink * delta
            for i in range(4):
                dsink = sum(_sum0(t[(b * 4 + i) * ATTN_BLOCK:(b * 4 + i + 1) * ATTN_BLOCK]) for b in range(nb))
                ds_ref[2 * i + hp:2 * i + hp + 1, :] -= jnp.broadcast_to(dsink, (1, 128))
        dsb = _mx(jnp.concatenate(ds, axis=1))
        pb = _mx(jnp.concatenate([probs[0][0], probs[1][0]], axis=1))
        for b in range(nb):
            dkc = [jnp.zeros((512, 128), F32), jnp.zeros((512, 128), F32)]
            dvc = [jnp.zeros((512, 128), F32), jnp.zeros((512, 128), F32)]
            for i in range(4):
                j = i // 2
                unit = slice((b * 4 + i) * ATTN_BLOCK, (b * 4 + i + 1) * ATTN_BLOCK)
                dq_ref[b * ATTN_BLOCK:(b + 1) * ATTN_BLOCK, i * 128:(i + 1) * 128] = _dot(dsb[unit], kcats[b][j])
                dkc[j] = dkc[j] + _dot_tn(dsb[unit], tile_part(q_ref, b, i))
                dvc[j] = dvc[j] + _dot_tn(pb[unit], tile_part(dob, b, i))
            dk_ref[pl.ds(kstarts[b], 256), :] += _kv_collect(dkc[0], dkc[1], lo)
            dv_ref[pl.ds(kstarts[b], 256), :] += _kv_collect(dvc[0], dvc[1], lo)

    r = _rows_spec(TILE_Q, D_ATTN, nt)
    return _call(
        body, "attn_bwd", nt,
        [r, _whole(), _whole(), _whole(), r, r],
        [r, _whole(), _whole(), _whole()],
        [_sds((n, D_ATTN)), _sds((n, D_KV)), _sds((n, D_KV)), _sds((8, 128))],
    )(q, k, v, sink_cols, o, do)


S5_TABLES = ('pw_re', 'pw_im', 'dbl_re', 'dbl_im', 'seg_re', 'seg_im', 'segr_re', 'segr_im')
S5_WEIGHTS = ('b_re', 'b_im', 'c_re', 'c_im', 'd', 'glu_w', 'glu_b', 'perm', 'perm_t')


def _s5_states(u, carry_r, carry_i, b_re, b_im, tab, hr_s, hi_s):
    ub = _mx(u)
    hr_s[...] = _dot(ub, b_re[...])
    hi_s[...] = _dot(ub, b_im[...])
    return ub, _cscan(hr_s, hi_s, tab, carry_r, carry_i, reverse=False)


def _s5_fwd(u, prm):
    n = u.shape[0]
    nt = n // TILE

    def body(u_ref, *refs):
        tab = dict(zip(S5_TABLES, refs[:8]))
        b_re, b_im, c_re, c_im, d_ref, gw_ref, gb_ref, perm, perm_t = refs[8:17]
        y_ref, cr_out, ci_out, cr_s, ci_s, hr_s, hi_s = refs[17:]

        @pl.when(pl.program_id(0) == 0)
        def _():
            cr_s[...] = jnp.zeros_like(cr_s)
            ci_s[...] = jnp.zeros_like(ci_s)

        u = _permute_rows(perm[...], u_ref[...])
        cr, ci = cr_s[...], ci_s[...]
        cr_out[...] = jnp.broadcast_to(cr, (8, N_STATE))
        ci_out[...] = jnp.broadcast_to(ci, (8, N_STATE))
        _, (cr_s[...], ci_s[...]) = _s5_states(u, cr, ci, b_re, b_im, tab, hr_s, hi_s)
        y = _dot(_mx(hr_s[...]), c_re[...]) - _dot(_mx(hi_s[...]), c_im[...]) + d_ref[...] * u
        z = _gelu(y)
        y_ref[...] = _permute_rows(perm_t[...], z * _sigmoid(_dot(_mx(z), gw_ref[...]) + gb_ref[...]))

    r = functools.partial(_rows_spec, n_tiles=nt)
    return _call(
        body, "s5_fwd", nt,
        [r(TILE, D_S5)] + [_whole()] * 17,
        [r(TILE, D_S5), r(8, N_STATE), r(8, N_STATE)],
        [_sds((n, D_S5)), _sds((nt * 8, N_STATE)), _sds((nt * 8, N_STATE))],
        scratch=[pltpu.VMEM((1, N_STATE), F32)] * 2 + [pltpu.VMEM((TILE, N_STATE), F32)] * 2,
    )(u, *[prm[k] for k in S5_TABLES + S5_WEIGHTS])


def _s5_bwd(u, dys, carry_re, carry_im, prm):
    n = u.shape[0]
    nt = n // TILE

    def body(u_ref, dy_ref, cin_r, cin_i, *refs):
        tab = dict(zip(S5_TABLES, refs[:8]))
        b_re, b_im, c_re, c_im, d_ref, gw_ref, gb_ref, perm, perm_t = refs[8:17]
        du_ref, dbr_ref, dbi_ref, dcr_ref, dci_ref, dar_ref, dai_ref, dd_ref, dgw_ref, dgb_ref = refs[17:27]
        gr_s, gi_s, hr_s, hi_s, gr_t, gi_t = refs[27:]

        @pl.when(pl.program_id(0) == 0)
        def _():
            for ref in (dbr_ref, dbi_ref, dcr_ref, dci_ref, dar_ref, dai_ref, dd_ref, dgw_ref, dgb_ref, gr_s, gi_s):
                ref[...] = jnp.zeros_like(ref)

        u = _permute_rows(perm[...], u_ref[...])
        cr, ci = cin_r[0:1, :], cin_i[0:1, :]
        ub, _ = _s5_states(u, cr, ci, b_re, b_im, tab, hr_s, hi_s)
        hrb, hib = _mx(hr_s[...]), _mx(hi_s[...])
        y = _dot(hrb, c_re[...]) - _dot(hib, c_im[...]) + d_ref[...] * u
        z = _gelu(y)
        zb = _mx(z)
        sg = _sigmoid(_dot(zb, gw_ref[...]) + gb_ref[...])
        dout = _permute_rows(perm[...], dy_ref[...])
        dpre = dout * z * sg * (1.0 - sg)
        dgb_ref[...] += _sum0(dpre)
        dpb = _mx(dpre)
        dgw_ref[...] += _dot_tn(zb, dpb)
        dy = (dout * sg + _dot_nt(dpb, gw_ref[...])) * _gelu_grad(y)
        dd_ref[...] += _sum0(dy * u)
        dyb = _mx(dy)
        dcr_ref[...] += _dot_tn(hrb, dyb)
        dci_ref[...] -= _dot_tn(hib, dyb)
        gr_t[...] = _dot_nt(dyb, c_re[...])
        gi_t[...] = -_dot_nt(dyb, c_im[...])
        gr_s[...], gi_s[...] = _cscan(gr_t, gi_t, tab, gr_s[...], gi_s[...], reverse=True)
        sub = _row_iota((N_SEG, N_STATE))
        acc_r = acc_i = jnp.zeros((N_SEG, N_STATE), F32)
        for k in range(SEG):
            if k == 0:
                hpr = jnp.where(sub >= 1, pltpu.roll(hr_s[_seg_rows(SEG - 1), :], 1, 0), cr)
                hpi = jnp.where(sub >= 1, pltpu.roll(hi_s[_seg_rows(SEG - 1), :], 1, 0), ci)
            else:
                hpr, hpi = hr_s[_seg_rows(k - 1), :], hi_s[_seg_rows(k - 1), :]
            gr, gi = gr_t[_seg_rows(k), :], gi_t[_seg_rows(k), :]
            acc_r = acc_r + gr * hpr + gi * hpi
            acc_i = acc_i + gi * hpr - gr * hpi
        dar_ref[...] += _sum0(acc_r)
        dai_ref[...] += _sum0(acc_i)
        grb, gib = _mx(gr_t[...]), _mx(gi_t[...])
        dbr_ref[...] += _dot_tn(ub, grb)
        dbi_ref[...] += _dot_tn(ub, gib)
        du_ref[...] = _permute_rows(perm_t[...], dy * d_ref[...] + _dot_nt(grb, b_re[...]) + _dot_nt(gib, b_im[...]))

    r = functools.partial(_rows_spec, n_tiles=nt, reverse=True)
    return _call(
        body, "s5_bwd", nt,
        [r(TILE, D_S5), r(TILE, D_S5), r(8, N_STATE), r(8, N_STATE)] + [_whole()] * 17,
        [r(TILE, D_S5)] + [_whole()] * 9,
        [_sds((n, D_S5)), _sds((D_S5, N_STATE)), _sds((D_S5, N_STATE)), _sds((N_STATE, D_S5)), _sds((N_STATE, D_S5)),
         _sds((1, N_STATE)), _sds((1, N_STATE)), _sds((1, D_S5)), _sds((D_S5, D_S5)), _sds((1, D_S5))],
        scratch=[pltpu.VMEM((1, N_STATE), F32)] * 2 + [pltpu.VMEM((TILE, N_STATE), F32)] * 4,
    )(u, dys, carry_re, carry_im, *[prm[k] for k in S5_TABLES + S5_WEIGHTS])


def _lru_gates(xr, halo, tile_index, cw_ref, cb_ref, wx_ref, wa_ref, bx_ref, ba_ref, sp_ref):
    ext = jnp.concatenate([halo, xr], axis=0)
    sh = [xr] + [_shift_down(ext, j, TILE) for j in (1, 2, 3)]
    xc = cb_ref[...] + cw_ref[3:4, :] * sh[0] + cw_ref[2:3, :] * sh[1] + cw_ref[1:2, :] * sh[2] + cw_ref[0:1, :] * sh[3]
    xb = _mx(xc)
    gx = _sigmoid(_dot(xb, wx_ref[...]) + bx_ref[...])
    ga = _sigmoid(_dot(xb, wa_ref[...]) + ba_ref[...])
    la = -LRU_C * ga * sp_ref[...]
    a = jnp.exp(la)
    start = (tile_index * TILE + _row_iota(xr.shape)) == 0
    mult = jnp.where(start, 1.0, jnp.sqrt(-jnp.tanh(la) * (a * a + 1.0)))
    return sh, xc, xb, gx, ga, a, mult, start


def _lru_fwd(xr, gate, prm):
    n = xr.shape[0]
    nt = n // TILE

    def body(x_ref, g_ref, cw_ref, cb_ref, wx_ref, wa_ref, bx_ref, ba_ref, sp_ref, y_ref, c_out, halo_s, c_s):
        first_tile = pl.program_id(0) == 0

        @pl.when(first_tile)
        def _():
            halo_s[...] = jnp.zeros_like(halo_s)
            c_s[...] = jnp.zeros_like(c_s)

        xr = x_ref[...]
        _, xc, _, gx, _, a, mult, _ = _lru_gates(xr, halo_s[...], pl.program_id(0), cw_ref, cb_ref, wx_ref, wa_ref, bx_ref, ba_ref,
                                                 sp_ref)
        halo_s[...] = xr[TILE - 8:]
        acum, h = _rscan(a, mult * gx * xc, reverse=False)
        c = c_s[...]
        c_out[...] = jnp.broadcast_to(c, (8, D_LRU))
        h = h + acum * c
        c_s[...] = h[TILE - 1:TILE]
        y_ref[...] = h * _gelu(g_ref[...])

    r = functools.partial(_rows_spec, n_tiles=nt)
    return _call(
        body, "lru_fwd", nt,
        [r(TILE, D_LRU), r(TILE, D_LRU)] + [_whole()] * 7,
        [r(TILE, D_LRU), r(8, D_LRU)],
        [_sds((n, D_LRU)), _sds((nt * 8, D_LRU))],
        scratch=[pltpu.VMEM((8, D_LRU), F32), pltpu.VMEM((1, D_LRU), F32)],
    )(xr, gate, prm['conv_w'], prm['conv_b'], prm['wx'], prm['wa'], prm['bx'], prm['ba'], prm['sp'])


def _lru_bwd(xr, gate, dyl, carry, prm):
    n = xr.shape[0]
    nt = n // TILE

    def body(x_ref, xh_ref, g_ref, dy_ref, cin_ref, cw_ref, cb_ref, wx_ref, wa_ref, bx_ref, ba_ref, sp_ref,
             dx_ref, dg_ref, dcw0, dcw1, dcw2, dcw3, dcb_ref, dwx_ref, dwa_ref, dbx_ref, dba_ref, dsp_ref, an_s, gn_s, dn_s):
        first_tile = pl.program_id(0) == nt - 1

        @pl.when(pl.program_id(0) == 0)
        def _():
            for ref in (dcw0, dcw1, dcw2, dcw3, dcb_ref, dwx_ref, dwa_ref, dbx_ref, dba_ref, dsp_ref, gn_s, dn_s):
                ref[...] = jnp.zeros_like(ref)
            an_s[...] = jnp.ones_like(an_s)

        xr = x_ref[...]
        halo = jnp.where(first_tile, 0.0, xh_ref[...])
        sh, xc, xb, gx, ga, a, mult, start = _lru_gates(xr, halo, nt - 1 - pl.program_id(0), cw_ref, cb_ref, wx_ref, wa_ref, bx_ref,
                                                        ba_ref, sp_ref)
        acum, h = _rscan(a, mult * gx * xc, reverse=False)
        cin = cin_ref[0:1, :]
        h = h + acum * cin
        gate = g_ref[...]
        dyl = dy_ref[...]
        dg_ref[...] = dyl * h * _gelu_grad(gate)
        row = _row_iota(xr.shape)
        alpha = jnp.where(row < TILE - 1, pltpu.roll(a, TILE - 1, 0), an_s[...])
        racc, g = _rscan(alpha, dyl * _gelu(gate), reverse=True)
        g = g + racc * gn_s[...]
        an_s[...] = a[0:1]
        gn_s[...] = g[0:1]
        hprev = jnp.where(row == 0, cin, pltpu.roll(h, 1, 0))
        da = g * hprev
        dmult = jnp.where(start, 0.0, g * gx * xc)
        dla = da * a - dmult * a * a / mult
        dsp_ref[...] += _sum0(-LRU_C * ga * dla)
        dpa = (-LRU_C * sp_ref[...] * dla) * ga * (1.0 - ga)
        dpx = (g * mult * xc) * gx * (1.0 - gx)
        dba_ref[...] += _sum0(dpa)
        dbx_ref[...] += _sum0(dpx)
        dpab, dpxb = _mx(dpa), _mx(dpx)
        dwa_ref[...] += _dot_tn(xb, dpab)
        dwx_ref[...] += _dot_tn(xb, dpxb)
        dxc = g * mult * gx + _dot_nt(dpab, wa_ref[...]) + _dot_nt(dpxb, wx_ref[...])
        dcb_ref[...] += _sum0(dxc)
        dcw3[...] += _sum0(dxc * sh[0])
        dcw2[...] += _sum0(dxc * sh[1])
        dcw1[...] += _sum0(dxc * sh[2])
        dcw0[...] += _sum0(dxc * sh[3])
        ext = jnp.concatenate([dxc, dn_s[...]], axis=0)
        dx_ref[...] = (cw_ref[3:4, :] * dxc + cw_ref[2:3, :] * _shift_up(ext, 1, TILE) + cw_ref[1:2, :] * _shift_up(ext, 2, TILE)
                       + cw_ref[0:1, :] * _shift_up(ext, 3, TILE))
        dn_s[...] = dxc[:8]

    r = functools.partial(_rows_spec, n_tiles=nt, reverse=True)
    vec = _sds((1, D_LRU))
    return _call(
        body, "lru_bwd", nt,
        [r(TILE, D_LRU), _halo_spec(D_LRU, TILE, nt, reverse=True), r(TILE, D_LRU), r(TILE, D_LRU), r(8, D_LRU)] + [_whole()] * 7,
        [r(TILE, D_LRU), r(TILE, D_LRU)] + [_whole()] * 10,
        [_sds((n, D_LRU)), _sds((n, D_LRU)), vec, vec, vec, vec, vec, _sds((D_LRU, D_LRU)), _sds((D_LRU, D_LRU)), vec, vec, vec],
        scratch=[pltpu.VMEM((1, D_LRU), F32), pltpu.VMEM((1, D_LRU), F32), pltpu.VMEM((8, D_LRU), F32)],
    )(xr, xr, gate, dyl, carry, prm['conv_w'], prm['conv_b'], prm['wx'], prm['wa'], prm['bx'], prm['ba'], prm['sp'])


def _normed_parts(ya, ys, yl):
    return jnp.concatenate([ya * _rms(ya), ys * _rms(ys), yl * _rms(yl)], axis=1)


def _mixout_fwd(ya, ys, yl, x0, g_mix, w_out, b_out, g1, b1):
    n = x0.shape[0]
    nt = n // TILE

    def body(ya_ref, ys_ref, yl_ref, x_ref, gm_ref, w_ref, b_ref, g_ref, be_ref, mix_ref, r_ref, x1_ref):
        mixb = _mx(_normed_parts(ya_ref[...], ys_ref[...], yl_ref[...]) * gm_ref[...])
        mix_ref[...] = mixb
        r1 = ALPHA * x_ref[...] + _dot(mixb, w_ref[...]) + b_ref[...]
        r_ref[...] = r1
        xhat, _ = _ln_stats(r1)
        x1_ref[...] = xhat * g_ref[...] + be_ref[...]

    r = functools.partial(_rows_spec, n_tiles=nt)
    return _call(
        body, "mixout_fwd", nt,
        [r(TILE, D_ATTN), r(TILE, D_S5), r(TILE, D_LRU), r(TILE, D)] + [_whole()] * 5,
        [r(TILE, D), r(TILE, D), r(TILE, D)],
        [_sds((n, D), MXU), _sds((n, D)), _sds((n, D))],
    )(ya, ys, yl, x0, g_mix, w_out, b_out, g1, b1)


def _mixout_bwd(dr1, mix, ya, ys, yl, g_mix, w_out):
    n = dr1.shape[0]
    nt = n // TILE

    def body(dr_ref, mix_ref, ya_ref, ys_ref, yl_ref, gm_ref, w_ref, dya_ref, dys_ref, dyl_ref, dw_ref, db_ref, dgm_ref):
        @pl.when(pl.program_id(0) == 0)
        def _():
            for ref in (dw_ref, db_ref, dgm_ref):
                ref[...] = jnp.zeros_like(ref)

        dr = dr_ref[...]
        db_ref[...] += _sum0(dr)
        drb = _mx(dr)
        dw_ref[...] += _dot_tn(mix_ref[...], drb)
        dmix = _dot(drb, w_ref[...])
        parts = (ya_ref[...], ys_ref[...], yl_ref[...])
        dgm_ref[...] += _sum0(dmix * _normed_parts(*parts))
        dn = dmix * gm_ref[...]
        lo = 0
        for y, out in zip(parts, (dya_ref, dys_ref, dyl_ref)):
            w = y.shape[1]
            rs = _rms(y)
            nrm = y * rs
            dnp = dn[:, lo:lo + w]
            out[...] = rs * (dnp - nrm * jnp.mean(dnp * nrm, axis=-1, keepdims=True))
            lo += w

    r = functools.partial(_rows_spec, n_tiles=nt)
    return _call(
        body, "mixout_bwd", nt,
        [r(TILE, D), r(TILE, D), r(TILE, D_ATTN), r(TILE, D_S5), r(TILE, D_LRU), _whole(), _whole()],
        [r(TILE, D_ATTN), r(TILE, D_S5), r(TILE, D_LRU), _whole(), _whole(), _whole()],
        [_sds((n, D_ATTN)), _sds((n, D_S5)), _sds((n, D_LRU)), _sds((D, D)), _sds((1, D)), _sds((1, D))],
    )(dr1, mix, ya, ys, yl, g_mix, w_out)


def _ffn_conv(gp, halo, cw_ref, cb_ref, cs):
    ext = jnp.concatenate([halo, gp], axis=0)
    s1 = _shift_down(ext, 1, TILE)
    s2 = _shift_down(ext, 2, TILE)
    return s1, s2, cb_ref[:, cs] + cw_ref[2:3, cs] * gp + cw_ref[1:2, cs] * s1 + cw_ref[0:1, cs] * s2


def _ffn_fwd(x1, wg, wu, cw, cb, wd, g2, b2):
    n = x1.shape[0]
    nt = n // TILE

    def body(x_ref, wg_ref, wu_ref, cw_ref, cb_ref, wd_ref, g_ref, be_ref, gp_ref, up_ref, r_ref, x2_ref, halo_s):
        @pl.when(pl.program_id(0) == 0)
        def _():
            halo_s[...] = jnp.zeros_like(halo_s)

        x1 = x_ref[...]
        xb = _mx(x1)
        f = jnp.zeros((TILE, D), F32)
        for c in range(D_FF // FF_CHUNK):
            cs = slice(c * FF_CHUNK, (c + 1) * FF_CHUNK)
            gp = _dot(xb, wg_ref[:, cs])
            up = _dot(xb, wu_ref[:, cs])
            gp_ref[:, cs] = gp
            up_ref[:, cs] = up
            _, _, gc = _ffn_conv(gp, halo_s[:, cs], cw_ref, cb_ref, cs)
            halo_s[:, cs] = gp[TILE - 8:]
            f = f + _dot(_mx(gc * _sigmoid(gc) * up), wd_ref[cs, :])
        r2 = ALPHA * x1 + f
        r_ref[...] = r2
        xhat, _ = _ln_stats(r2)
        x2_ref[...] = xhat * g_ref[...] + be_ref[...]

    r = functools.partial(_rows_spec, n_tiles=nt)
    return _call(
        body, "ffn_fwd", nt,
        [r(TILE, D)] + [_whole()] * 7,
        [r(TILE, D_FF), r(TILE, D_FF), r(TILE, D), r(TILE, D)],
        [_sds((n, D_FF)), _sds((n, D_FF)), _sds((n, D)), _sds((n, D))],
        scratch=[pltpu.VMEM((8, D_FF), F32)],
    )(x1, wg, wu, cw, cb, wd, g2, b2)


def _ffn_bwd_down(dx2, r2, g2, gp, up, cw, cb, wd_t):
    n = dx2.shape[0]
    nt = n // TILE

    def body(dx_ref, r_ref, g_ref, gp_ref, gh_ref, up_ref, cw_ref, cb_ref, wd_ref,
             dr_ref, dgp_ref, dup_ref, dwd_ref, dcw0, dcw1, dcw2, dcb_ref, dg_ref, db_ref, next_s):
        first_tile = pl.program_id(0) == nt - 1

        @pl.when(pl.program_id(0) == 0)
        def _():
            for ref in (dwd_ref, dcw0, dcw1, dcw2, dcb_ref, dg_ref, db_ref, next_s):
                ref[...] = jnp.zeros_like(ref)

        dx2 = dx_ref[...]
        xhat, rstd = _ln_stats(r_ref[...])
        dg_ref[...] += _sum0(dx2 * xhat)
        db_ref[...] += _sum0(dx2)
        dr2 = _ln_bwd(dx2, g_ref[...], xhat, rstd)
        dr_ref[...] = dr2
        dfb = _mx(dr2)
        for c in range(D_FF // FF_CHUNK):
            cs = slice(c * FF_CHUNK, (c + 1) * FF_CHUNK)
            gp = gp_ref[:, cs]
            up = up_ref[:, cs]
            s1, s2, gc = _ffn_conv(gp, jnp.where(first_tile, 0.0, gh_ref[:, cs]), cw_ref, cb_ref, cs)
            sg = _sigmoid(gc)
            silu = gc * sg
            dact = _dot(dfb, wd_ref[:, cs])
            dwd_ref[cs, :] += _dot_tn(_mx(silu * up), dfb)
            dup_ref[:, cs] = _mx(dact * silu)
            dgc = dact * up * (sg * (1.0 + gc * (1.0 - sg)))
            dcb_ref[:, cs] += _sum0(dgc)
            dcw2[:, cs] += _sum0(dgc * gp)
            dcw1[:, cs] += _sum0(dgc * s1)
            dcw0[:, cs] += _sum0(dgc * s2)
            ext = jnp.concatenate([dgc, next_s[:, cs]], axis=0)
            dgp_ref[:, cs] = _mx(cw_ref[2:3, cs] * dgc + cw_ref[1:2, cs] * _shift_up(ext, 1, TILE)
                                 + cw_ref[0:1, cs] * _shift_up(ext, 2, TILE))
            next_s[:, cs] = dgc[:8]

    r = functools.partial(_rows_spec, n_tiles=nt, reverse=True)
    vff = _sds((1, D_FF))
    return _call(
        body, "ffn_bwd_down", nt,
        [r(TILE, D), r(TILE, D), _whole(), r(TILE, D_FF), _halo_spec(D_FF, TILE, nt, reverse=True), r(TILE, D_FF), _whole(), _whole(),
         _whole()],
        [r(TILE, D), r(TILE, D_FF), r(TILE, D_FF)] + [_whole()] * 7,
        [_sds((n, D)), _sds((n, D_FF), MXU), _sds((n, D_FF), MXU), _sds((D_FF, D)), vff, vff, vff, vff, _sds((1, D)), _sds((1, D))],
        scratch=[pltpu.VMEM((8, D_FF), F32)],
    )(dx2, r2, g2, gp, gp, up, cw, cb, wd_t)


def _ffn_bwd_dx(dr2, dgp, dup, r1, g1, wg_t, wu_t):
    n = dr2.shape[0]
    rows = TILE_BIG
    nt = n // rows

    def body(dr2_ref, dgp_ref, dup_ref, r_ref, g_ref, wg_ref, wu_ref, dr1_ref, dg_ref, db_ref):
        @pl.when(pl.program_id(0) == 0)
        def _():
            for ref in (dg_ref, db_ref):
                ref[...] = jnp.zeros_like(ref)

        dx1 = ALPHA * dr2_ref[...] + _dot(dgp_ref[...], wg_ref[...]) + _dot(dup_ref[...], wu_ref[...])
        xhat, rstd = _ln_stats(r_ref[...])
        dg_ref[...] += _sum0(dx1 * xhat)
        db_ref[...] += _sum0(dx1)
        dr1_ref[...] = _ln_bwd(dx1, g_ref[...], xhat, rstd)

    r = functools.partial(_rows_spec, n_tiles=nt)
    return _call(
        body, "ffn_bwd_dx", nt,
        [r(rows, D), r(rows, D_FF), r(rows, D_FF), r(rows, D), _whole(), _whole(), _whole()],
        [r(rows, D), _whole(), _whole()],
        [_sds((n, D)), _sds((1, D)), _sds((1, D))],
    )(dr2, dgp, dup, r1, g1, wg_t, wu_t)


def _ffn_bwd_dw(x1, dgp, dup):
    n = x1.shape[0]
    rows = TILE_BIG
    nt = n // rows

    def body(x_ref, dgp_ref, dup_ref, dwg_ref, dwu_ref):
        @pl.when(pl.program_id(0) == 0)
        def _():
            for ref in (dwg_ref, dwu_ref):
                ref[...] = jnp.zeros_like(ref)

        xb = _mx(x_ref[...])
        for c in range(D_FF // FF_CHUNK):
            cs = slice(c * FF_CHUNK, (c + 1) * FF_CHUNK)
            dwg_ref[:, cs] += _dot_tn(xb, dgp_ref[:, cs])
            dwu_ref[:, cs] += _dot_tn(xb, dup_ref[:, cs])

    r = functools.partial(_rows_spec, n_tiles=nt)
    return _call(
        body, "ffn_bwd_dw", nt,
        [r(rows, D), r(rows, D_FF), r(rows, D_FF)], [_whole(), _whole()], [_sds((D, D_FF)), _sds((D, D_FF))],
    )(x1, dgp, dup)


def _loss_head(y, target):
    n = y.shape[0]
    nt = n // TILE

    def body(y_ref, t_ref, loss_ref, dy_ref):
        @pl.when(pl.program_id(0) == 0)
        def _():
            loss_ref[...] = jnp.zeros_like(loss_ref)

        e = y_ref[...] - t_ref[...]
        dy_ref[...] = e * (1.0 / D)
        loss_ref[...] += _sum0(jnp.sum(e * e, axis=1, keepdims=True)) * (0.5 / D)

    r = functools.partial(_rows_spec, n_tiles=nt)
    return _call(body, "loss_head", nt, [r(TILE, D), r(TILE, D)], [_whole(), r(TILE, D)], [_sds((1, 1)), _sds((n, D))])(y, target)


def _place():
    x, y, c = lax.axis_index("x"), lax.axis_index("y"), lax.axis_index("c")
    return x, y, c, 4 * x + 2 * y + c


def _peer(x, y, c, k):
    px, py, pc = x ^ ((k >> 2) & 1), y ^ ((k >> 1) & 1), c ^ (k & 1)
    return (px, py, pc), 4 * px + 2 * py + pc


def _exchange(src, name, scatter):
    shape = src.shape[-2:]

    def body(src_ref, out_ref, send_sems, recv_sems, local_sem):
        x, y, c, me = _place()
        mine = pltpu.make_async_copy(src_ref.at[me] if scatter else src_ref, out_ref.at[me], local_sem)
        mine.start()
        copies = []
        for k in range(1, N_DEV):
            peer, peer_slot = _peer(x, y, c, k)
            copies.append(pltpu.make_async_remote_copy(
                src_ref=src_ref.at[peer_slot] if scatter else src_ref, dst_ref=out_ref.at[me],
                send_sem=send_sems.at[k], recv_sem=recv_sems.at[k], device_id=peer, device_id_type=pl.DeviceIdType.MESH))
        for cp in copies:
            cp.start()
        for k in range(1, N_DEV):
            peer, peer_slot = _peer(x, y, c, k)
            pltpu.make_async_remote_copy(
                src_ref=src_ref.at[me] if scatter else src_ref, dst_ref=out_ref.at[peer_slot],
                send_sem=send_sems.at[k], recv_sem=recv_sems.at[k], device_id=peer, device_id_type=pl.DeviceIdType.MESH).wait_recv()
        for cp in copies:
            cp.wait_send()
        mine.wait()

    return pl.pallas_call(
        body, name=name, out_shape=_sds((N_DEV,) + shape, src.dtype),
        in_specs=[pl.BlockSpec(memory_space=pl.ANY)], out_specs=pl.BlockSpec(memory_space=pl.ANY),
        scratch_shapes=[pltpu.SemaphoreType.DMA((N_DEV,)), pltpu.SemaphoreType.DMA((N_DEV,)), pltpu.SemaphoreType.DMA(())],
    )(src)


def _reduce_adamw(parts, w, m, v):
    rows = w.shape[0]
    nt = rows // PACK_TILE
    c1 = 1.0 - ADAM_B1 ** ADAM_STEP
    c2 = 1.0 - ADAM_B2 ** ADAM_STEP

    def body(p_ref, w_ref, m_ref, v_ref, g_out, d_out, m_out, v_out):
        g = p_ref[0]
        for s in range(1, N_DEV):
            g = g + p_ref[s]
        m_new = ADAM_B1 * m_ref[...] + (1.0 - ADAM_B1) * g
        v_new = ADAM_B2 * v_ref[...] + (1.0 - ADAM_B2) * (g * g)
        g_out[...] = g
        m_out[...] = m_new
        v_out[...] = v_new
        d_out[...] = -ADAM_LR * ((m_new / c1) / (jnp.sqrt(v_new / c2) + ADAM_EPS) + ADAM_WD * w_ref[...])

    r = _rows_spec(PACK_TILE, LANES, nt)
    out = _sds((rows, LANES))
    return _call(
        body, "reduce_adamw", nt,
        [pl.BlockSpec((N_DEV, PACK_TILE, LANES), lambda i: (0, i, 0)), r, r, r], [r, r, r, r], [out, out, out, out],
    )(parts, w, m, v)


def _pack_rows(a, lead=0):
    head = a.shape[:lead]
    flat = a.reshape(head + (-1,))
    size = flat.shape[-1]
    rows = -(-size // (16 * LANES)) * 16
    flat = jnp.pad(flat, [(0, 0)] * lead + [(0, rows * LANES - size)])
    return flat.reshape(head + (rows, LANES))


def _packed_rows(shape):
    return -(-math.prod(shape) // (16 * LANES)) * 16


def _to_blocks(full, axis):
    l, a, b = full.shape
    if axis == 2:
        return full.reshape(l, a, N_DEV, b // N_DEV).transpose(2, 0, 1, 3)
    return full.reshape(l, N_DEV, a // N_DEV, b).transpose(1, 0, 2, 3)


def _from_blocks(blocks, axis):
    _, l, a, b = blocks.shape
    if axis == 2:
        return blocks.transpose(1, 2, 0, 3).reshape(l, a, N_DEV * b)
    return blocks.transpose(1, 0, 2, 3).reshape(l, N_DEV * a, b)


def _gather_weights(local):
    segs, meta = [], []
    for name in SHARDED:
        blk = local[name]
        if name in GATHER_F32:
            bits = lax.bitcast_convert_type(blk, MXU)
        else:
            bits = _mx(blk)
        seg = _pack_rows(bits)
        meta.append((name, blk.shape, bits.shape, seg.shape[0]))
        segs.append(seg)
    gathered = _exchange(jnp.concatenate(segs, axis=0), "gather_weights", scatter=False)
    out, lo = {}, 0
    for name, shape, bits_shape, rows in meta:
        seg = gathered[:, lo:lo + rows].reshape(N_DEV, -1)[:, :math.prod(bits_shape)].reshape((N_DEV,) + bits_shape)
        if name in GATHER_F32:
            seg = lax.bitcast_convert_type(seg, F32)
        out[name] = _from_blocks(seg, SHARD_AXIS[name])
        lo += rows
    return out


def _s5_discretize(a_re, a_im, log_dt, b_re, b_im):
    lam_re = jnp.minimum(a_re, -1e-4)
    lam_im = a_im
    dt = jnp.exp(log_dt)[:, None]
    decay = jnp.exp(dt * lam_re)
    ang = dt * lam_im
    abar_re = decay * jnp.cos(ang)
    abar_im = decay * jnp.sin(ang)
    den = jnp.square(lam_re) + jnp.square(lam_im)
    nr = abar_re - 1.0
    ni = abar_im
    coef_re = (nr * lam_re + ni * lam_im) / den
    coef_im = (ni * lam_re - nr * lam_im) / den
    bbar_re = coef_re[..., None] * b_re - coef_im[..., None] * b_im
    bbar_im = coef_re[..., None] * b_im + coef_im[..., None] * b_re
    return abar_re, abar_im, bbar_re, bbar_im


def _complex_powers(ar, ai, count):
    def combine(e1, e2):
        return e2[0] * e1[0] - e2[1] * e1[1], e2[0] * e1[1] + e2[1] * e1[0]

    shape = (count,) + ar.shape
    return lax.associative_scan(combine, (jnp.broadcast_to(ar, shape), jnp.broadcast_to(ai, shape)), axis=0)


_EYE16 = functools.partial(jnp.eye, 16, dtype=F32)


def _s5_params(p, l):
    disc, disc_vjp = jax.vjp(_s5_discretize, p['s5_a_re'][l], p['s5_a_im'][l], p['s5_log_dt'][l], p['s5_b_re'][l], p['s5_b_im'][l])
    abar_re, abar_im, bbar_re, bbar_im = disc
    ar, ai = abar_re.reshape(N_STATE), abar_im.reshape(N_STATE)
    ap_re, ap_im = _complex_powers(ar, ai, TILE)
    one, zero = jnp.ones((1, N_STATE), F32), jnp.zeros((1, N_STATE), F32)
    seg_re = jnp.concatenate([one, ap_re[SEG - 1:TILE - SEG:SEG]], axis=0)
    seg_im = jnp.concatenate([zero, ap_im[SEG - 1:TILE - SEG:SEG]], axis=0)
    doubling = [SEG - 1, 2 * SEG - 1, 4 * SEG - 1]

    src = (jnp.arange(TILE) % N_SEG) * SEG + jnp.arange(TILE) // N_SEG
    perm = (src[:, None] == jnp.arange(TILE)[None, :]).astype(MXU)
    prm = {
        'perm': perm, 'perm_t': perm.T,
        'pw_re': ap_re[:SEG], 'pw_im': ap_im[:SEG],
        'dbl_re': jnp.concatenate([jnp.stack([ap_re[k] for k in doubling]), jnp.zeros((5, N_STATE), F32)], axis=0),
        'dbl_im': jnp.concatenate([jnp.stack([ap_im[k] for k in doubling]), jnp.zeros((5, N_STATE), F32)], axis=0),
        'seg_re': seg_re, 'seg_im': seg_im, 'segr_re': seg_re[::-1], 'segr_im': seg_im[::-1],
        'b_re': _mx(jnp.einsum('gpc,gh->gchp', bbar_re, _EYE16()).reshape(D_S5, N_STATE)),
        'b_im': _mx(jnp.einsum('gpc,gh->gchp', bbar_im, _EYE16()).reshape(D_S5, N_STATE)),
        'c_re': _mx(jnp.einsum('gcp,gh->gphc', p['s5_c_re'][l], _EYE16()).reshape(N_STATE, D_S5)),
        'c_im': _mx(jnp.einsum('gcp,gh->gphc', p['s5_c_im'][l], _EYE16()).reshape(N_STATE, D_S5)),
        'd': p['s5_d'][l][None, :], 'glu_w': p['s5_glu_w'][l], 'glu_b': p['s5_glu_b'][l][None, :],
    }
    return prm, disc_vjp


def _lru_params(p, l):
    eye4 = jnp.eye(4, dtype=F32)
    return {
        'conv_w': p['lru_conv_w'][l], 'conv_b': p['lru_conv_b'][l][None, :],
        'wx': _mx(jnp.einsum('hij,hk->hikj', p['lru_wx'][l], eye4).reshape(D_LRU, D_LRU)),
        'wa': _mx(jnp.einsum('hij,hk->hikj', p['lru_wa'][l], eye4).reshape(D_LRU, D_LRU)),
        'bx': p['lru_bx'][l][None, :], 'ba': p['lru_ba'][l][None, :],
        'sp': jax.nn.softplus(-p['lru_a_param'][l])[None, :],
    }


def _rope_tables(n):
    inv_freq = ROPE_THETA ** (-jnp.arange(0, 64, 2, dtype=F32) / 64)
    ang = jnp.arange(n, dtype=F32)[:, None] * inv_freq[None, :]
    cos, sin = jnp.cos(ang), jnp.sin(ang)
    return jnp.concatenate([cos, cos, cos, cos], axis=1), jnp.concatenate([-sin, sin, -sin, sin], axis=1)


def _sink_cols(sinks):
    nb = TILE_Q // ATTN_BLOCK
    per_unit = sinks.reshape(4, 2).T
    return jnp.broadcast_to(per_unit[:, None, :, None], (2, nb, 4, ATTN_BLOCK)).reshape(2, nb * 4 * ATTN_BLOCK, 1)


def _local_step(x, target, p):
    n = x.shape[0]
    cos_t, sin_t = _rope_tables(n)
    row = lambda a: a[None, :]
    saved = []
    h = x
    for l in range(DEPTH):
        s = {'x0': h}
        s['w_in'] = p['w_in'][l]
        s['q'], s['k'], s['v'], s['u'], s['xr'], s['gate'] = _inproj_fwd(h, s['w_in'], row(p['b_in'][l]), cos_t, sin_t)
        s['sinks'] = _sink_cols(p['attn_sinks'][l])
        s['ya'] = _attn_fwd(s['q'], s['k'], s['v'], s['sinks'])
        s['s5'], s['s5_vjp'] = _s5_params(p, l)
        s['ys'], s['s5_cr'], s['s5_ci'] = _s5_fwd(s['u'], s['s5'])
        s['lru'] = _lru_params(p, l)
        s['yl'], s['lru_c'] = _lru_fwd(s['xr'], s['gate'], s['lru'])
        s['mix'], s['r1'], s['x1'] = _mixout_fwd(s['ya'], s['ys'], s['yl'], h, row(p['mix_norm_g'][l]), p['w_out'][l],
                                                 row(p['b_out'][l]), row(p['ln1_g'][l]), row(p['ln1_b'][l]))
        s['gp'], s['up'], s['r2'], h = _ffn_fwd(s['x1'], p['ffn_w_gate'][l], p['ffn_w_up'][l], p['ffn_conv_w'][l],
                                                row(p['ffn_conv_b'][l]), p['ffn_w_down'][l], row(p['ln2_g'][l]), row(p['ln2_b'][l]))
        saved.append(s)
    loss, dh = _loss_head(h, target)

    grads = {name: [None] * DEPTH for name in WEIGHTS}
    for l in reversed(range(DEPTH)):
        s = saved[l]
        g = {}
        (dr2, dgp, dup, g['ffn_w_down'], cw0, cw1, cw2, dcb, dg2, db2) = _ffn_bwd_down(
            dh, s['r2'], row(p['ln2_g'][l]), s['gp'], s['up'], p['ffn_conv_w'][l], row(p['ffn_conv_b'][l]), p['ffn_w_down'][l].T)
        g['ffn_conv_w'] = jnp.concatenate([cw0, cw1, cw2], axis=0)
        g['ffn_conv_b'], g['ln2_g'], g['ln2_b'] = dcb[0], dg2[0], db2[0]
        dr1, dg1, db1 = _ffn_bwd_dx(dr2, dgp, dup, s['r1'], row(p['ln1_g'][l]), p['ffn_w_gate'][l].T, p['ffn_w_up'][l].T)
        g['ffn_w_gate'], g['ffn_w_up'] = _ffn_bwd_dw(s['x1'], dgp, dup)
        g['ln1_g'], g['ln1_b'] = dg1[0], db1[0]
        dya, dys, dyl, g['w_out'], dbo, dgm = _mixout_bwd(dr1, s['mix'], s['ya'], s['ys'], s['yl'], row(p['mix_norm_g'][l]),
                                                         p['w_out'][l].T)
        g['b_out'], g['mix_norm_g'] = dbo[0], dgm[0]

        dxr, dgate, lw0, lw1, lw2, lw3, lcb, dwx, dwa, dbx, dba, dsp = _lru_bwd(s['xr'], s['gate'], dyl, s['lru_c'], s['lru'])
        g['lru_conv_w'] = jnp.concatenate([lw0, lw1, lw2, lw3], axis=0)
        g['lru_conv_b'], g['lru_bx'], g['lru_ba'] = lcb[0], dbx[0], dba[0]
        g['lru_wx'] = jnp.einsum('hihj->hij', dwx.reshape(4, 64, 4, 64))
        g['lru_wa'] = jnp.einsum('hihj->hij', dwa.reshape(4, 64, 4, 64))
        g['lru_a_param'] = -dsp[0] * jax.nn.sigmoid(-p['lru_a_param'][l])

        du, dbr, dbi, dcr, dci, dar, dai, dd, g['s5_glu_w'], dgb = _s5_bwd(s['u'], dys, s['s5_cr'], s['s5_ci'], s['s5'])
        g['s5_c_re'] = jnp.einsum('gpgc->gcp', dcr.reshape(16, 64, 16, 16))
        g['s5_c_im'] = jnp.einsum('gpgc->gcp', dci.reshape(16, 64, 16, 16))
        g['s5_d'], g['s5_glu_b'] = dd[0], dgb[0]
        g['s5_a_re'], g['s5_a_im'], g['s5_log_dt'], g['s5_b_re'], g['s5_b_im'] = s['s5_vjp']((
            dar.reshape(16, 64), dai.reshape(16, 64), jnp.einsum('gcgp->gpc', dbr.reshape(16, 16, 16, 64)),
            jnp.einsum('gcgp->gpc', dbi.reshape(16, 16, 16, 64))))

        dq, dk, dv, dsink = _attn_bwd(s['q'], s['k'], s['v'], s['sinks'], s['ya'], dya)
        g['attn_sinks'] = dsink[:, 0]
        dh, g['w_in'], dbin = _inproj_bwd(dq, dk, dv, du, dxr, dgate, cos_t, sin_t, s['x0'], dr1, s['w_in'].T)
        g['b_in'] = dbin[0]
        for name in WEIGHTS:
            grads[name][l] = g[name]
    return loss, dh, {name: jnp.stack(grads[name]) for name in WEIGHTS}


def kernel(x, w_in, b_in, attn_sinks, s5_a_re, s5_a_im, s5_b_re, s5_b_im, s5_c_re, s5_c_im, s5_d, s5_log_dt, s5_glu_w, s5_glu_b, lru_conv_w, lru_conv_b, lru_wx, lru_bx, lru_wa, lru_ba, lru_a_param, mix_norm_g, w_out, b_out, ln1_g, ln1_b, ffn_w_gate, ffn_w_up, ffn_conv_w, ffn_conv_b, ffn_w_down, ln2_g, ln2_b, loss_target, m_w_in, m_b_in, m_attn_sinks, m_s5_a_re, m_s5_a_im, m_s5_b_re, m_s5_b_im, m_s5_c_re, m_s5_c_im, m_s5_d, m_s5_log_dt, m_s5_glu_w, m_s5_glu_b, m_lru_conv_w, m_lru_conv_b, m_lru_wx, m_lru_bx, m_lru_wa, m_lru_ba, m_lru_a_param, m_mix_norm_g, m_w_out, m_b_out, m_ln1_g, m_ln1_b, m_ffn_w_gate, m_ffn_w_up, m_ffn_conv_w, m_ffn_conv_b, m_ffn_w_down, m_ln2_g, m_ln2_b, v_w_in, v_b_in, v_attn_sinks, v_s5_a_re, v_s5_a_im, v_s5_b_re, v_s5_b_im, v_s5_c_re, v_s5_c_im, v_s5_d, v_s5_log_dt, v_s5_glu_w, v_s5_glu_b, v_lru_conv_w, v_lru_conv_b, v_lru_wx, v_lru_bx, v_lru_wa, v_lru_ba, v_lru_a_param, v_mix_norm_g, v_w_out, v_b_out, v_ln1_g, v_ln1_b, v_ffn_w_gate, v_ffn_w_up, v_ffn_conv_w, v_ffn_conv_b, v_ffn_w_down, v_ln2_g, v_ln2_b):
    given = dict(locals())
    local_w = {name: given[name] for name in WEIGHTS}

    whole = dict(local_w)
    whole.update(_gather_weights({name: local_w[name] for name in SHARDED}))
    loss, grad_x, grads = _local_step(x[0], loss_target[0], whole)

    order = SHARDED + REPLICATED
    send = jnp.concatenate(
        [_pack_rows(_to_blocks(grads[name], SHARD_AXIS[name]), lead=1) for name in SHARDED]
        + [jnp.broadcast_to(_pack_rows(grads[name]), (N_DEV, _packed_rows(grads[name].shape), LANES)) for name in REPLICATED], axis=1)
    rows = send.shape[1]
    pad = -rows % PACK_TILE
    send = jnp.pad(send, ((0, 0), (0, pad), (0, 0)))
    parts = _exchange(send, "exchange_grads", scatter=True)

    def packed(prefix):
        a = jnp.concatenate([_pack_rows(given[prefix + name]) for name in order], axis=0)
        return jnp.pad(a, ((0, pad), (0, 0)))

    outs = _reduce_adamw(parts, packed(''), packed('m_'), packed('v_'))

    def unpack(a):
        res, lo = {}, 0
        for name in order:
            shape = local_w[name].shape
            res[name] = a[lo:lo + _packed_rows(shape)].reshape(-1)[:math.prod(shape)].reshape(shape)
            lo += _packed_rows(shape)
        return [res[name] for name in WEIGHTS]

    total = lax.psum(loss[0, 0], ("x", "y", "c"))
    return (total, grad_x[None], *unpack(outs[0]), *unpack(outs[1]), *unpack(outs[2]), *unpack(outs[3]))
```

```python
import functools
import math

import jax
import jax.numpy as jnp
from jax import lax
from jax.experimental import pallas as pl
from jax.experimental.pallas import tpu as pltpu

F32 = jnp.float32
MXU = jnp.bfloat16

N_DEV = 8
DEPTH = 4
D = 1024
D_ATTN, D_KV, D_S5, D_LRU = 512, 128, 256, 256
D_IN = 1536
D_FF = 2816
FF_CHUNK = 256
N_STATE = 1024
LANES = 1024
ALPHA = (2 * DEPTH) ** 0.25
LN_EPS = 1e-5
RMS_EPS = 1e-6
LRU_C = 8.0
ROPE_THETA = 10000.0
ADAM_LR, ADAM_B1, ADAM_B2, ADAM_EPS, ADAM_WD, ADAM_STEP = 0.001, 0.9, 0.999, 1e-08, 0.01, 10

TILE = 256
N_SEG = 8
SEG = TILE // N_SEG
TILE_Q = 512
TILE_BIG = 512
ATTN_BLOCK = 128
PACK_TILE = 256
VMEM_MB = 56

WEIGHTS = ['w_in', 'b_in', 'attn_sinks', 's5_a_re', 's5_a_im', 's5_b_re', 's5_b_im', 's5_c_re', 's5_c_im', 's5_d', 's5_log_dt',
           's5_glu_w', 's5_glu_b', 'lru_conv_w', 'lru_conv_b', 'lru_wx', 'lru_bx', 'lru_wa', 'lru_ba', 'lru_a_param', 'mix_norm_g',
           'w_out', 'b_out', 'ln1_g', 'ln1_b', 'ffn_w_gate', 'ffn_w_up', 'ffn_conv_w', 'ffn_conv_b', 'ffn_w_down', 'ln2_g', 'ln2_b']
SHARD_AXIS = {'w_in': 2, 's5_glu_w': 1, 'lru_conv_w': 2, 'w_out': 1, 'ffn_w_gate': 2, 'ffn_w_up': 2, 'ffn_conv_w': 2,
              'ffn_w_down': 1}
SHARDED = [n for n in WEIGHTS if n in SHARD_AXIS]
REPLICATED = [n for n in WEIGHTS if n not in SHARD_AXIS]
BIG = [(n, SHARD_AXIS[n] == 2) for n in ('w_in', 'w_out', 'ffn_w_gate', 'ffn_w_up', 'ffn_w_down')]
SMALL_SHARDED = [n for n in SHARDED if n not in dict(BIG)]
GATHER_F32 = ('lru_conv_w', 'ffn_conv_w')


def _dot(a, b):
    return jnp.dot(a, b, preferred_element_type=F32)


def _dot_nt(a, b):
    return lax.dot_general(a, b, (((1,), (1,)), ((), ())), preferred_element_type=F32)


def _dot_tn(a, b):
    return lax.dot_general(a, b, (((0,), (0,)), ((), ())), preferred_element_type=F32)


def _mx(a):
    return a.astype(MXU)


_GELU_C = math.sqrt(2.0 / math.pi)


def _gelu(x):
    th = jnp.tanh(_GELU_C * (x + 0.044715 * x * x * x))
    return 0.5 * x * (1.0 + th)


def _gelu_grad(x):
    th = jnp.tanh(_GELU_C * (x + 0.044715 * x * x * x))
    return 0.5 * (1.0 + th) + 0.5 * x * (1.0 - th * th) * _GELU_C * (1.0 + 3.0 * 0.044715 * x * x)


def _sigmoid(x):
    return 1.0 / (1.0 + jnp.exp(-x))


def _ln_stats(r):
    mu = jnp.mean(r, axis=-1, keepdims=True)
    xc = r - mu
    var = jnp.mean(xc * xc, axis=-1, keepdims=True)
    rstd = lax.rsqrt(var + LN_EPS)
    return xc * rstd, rstd


def _ln_bwd(dy, g, xhat, rstd):
    dxh = dy * g
    return rstd * (dxh - jnp.mean(dxh, axis=-1, keepdims=True) - xhat * jnp.mean(dxh * xhat, axis=-1, keepdims=True))


def _rms(y):
    return lax.rsqrt(jnp.mean(y * y, axis=-1, keepdims=True) + RMS_EPS)


def _sum0(a):
    return jnp.sum(a, axis=0, keepdims=True)


def _row_iota(shape):
    return lax.broadcasted_iota(jnp.int32, shape, 0)


def _shift_down(ext, j, rows):
    return pltpu.roll(ext, j, 0)[8:8 + rows]


def _shift_up(ext, j, rows):
    return pltpu.roll(ext, ext.shape[0] - j, 0)[:rows]


def _swap_halves(t):
    w = t.shape[1]
    lane = lax.broadcasted_iota(jnp.int32, t.shape, 1)
    return jnp.where((lane & 32) == 0, pltpu.roll(t, w - 32, 1), pltpu.roll(t, 32, 1))


def _rope(t, cos, sin_signed):
    return t * cos + _swap_halves(t) * sin_signed


def _rope_t(d, cos, sin_signed):
    return d * cos + _swap_halves(d * sin_signed)


def _cmul_add(ar, ai, xr, xi, yr, yi):
    return ar * xr - ai * xi + yr, ar * xi + ai * xr + yi


def _seg_rows(k):
    return slice(N_SEG * k, N_SEG * (k + 1))


def _permute_rows(perm, x):
    hi = _mx(x)
    rest = x - hi.astype(F32)
    mid = _mx(rest)
    lo = _mx(rest - mid.astype(F32))
    return _dot(perm, hi) + _dot(perm, mid) + _dot(perm, lo)


def _cscan(sr, si, tab, cin_r, cin_i, reverse):
    sgn = -1.0 if reverse else 1.0
    pw_re, pw_im, dbl_re, dbl_im = tab['pw_re'], tab['pw_im'], tab['dbl_re'], tab['dbl_im']
    ar, ai = pw_re[0:1, :], sgn * pw_im[0:1, :]
    shape = (N_SEG, sr.shape[1])
    hr = hi = jnp.zeros(shape, F32)
    for k in (range(SEG - 1, -1, -1) if reverse else range(SEG)):
        hr, hi = _cmul_add(ar, ai, hr, hi, sr[_seg_rows(k), :], si[_seg_rows(k), :])
        sr[_seg_rows(k), :] = hr
        si[_seg_rows(k), :] = hi
    sub = _row_iota(shape)

    def shifted(v, d):
        if reverse:
            return jnp.where(sub < N_SEG - d, pltpu.roll(v, N_SEG - d, 0), 0.0)
        return jnp.where(sub >= d, pltpu.roll(v, d, 0), 0.0)

    fr, fi = hr, hi
    for j, d in enumerate((1, 2, 4)):
        fr, fi = _cmul_add(dbl_re[j:j + 1, :], sgn * dbl_im[j:j + 1, :], shifted(fr, d), shifted(fi, d), fr, fi)
    seg_re, seg_im = (tab['segr_re'], tab['segr_im']) if reverse else (tab['seg_re'], tab['seg_im'])
    cr, ci = _cmul_add(seg_re[...], sgn * seg_im[...], cin_r, cin_i, shifted(fr, 1), shifted(fi, 1))
    nr, ni = _cmul_add(dbl_re[0:1, :], sgn * dbl_im[0:1, :], cr, ci, hr, hi)
    for k in range(SEG):
        j = SEG - 1 - k if reverse else k
        xr, xi = _cmul_add(pw_re[j:j + 1, :], sgn * pw_im[j:j + 1, :], cr, ci, sr[_seg_rows(k), :], si[_seg_rows(k), :])
        sr[_seg_rows(k), :] = xr
        si[_seg_rows(k), :] = xi
    edge = slice(0, 1) if reverse else slice(N_SEG - 1, N_SEG)
    return nr[edge], ni[edge]


def _rscan(a, b, reverse):
    rows = a.shape[0]
    row = _row_iota(a.shape)
    s = 1
    while s < rows:
        if reverse:
            keep = row < rows - s
            sa = jnp.where(keep, pltpu.roll(a, rows - s, 0), 1.0)
            sb = jnp.where(keep, pltpu.roll(b, rows - s, 0), 0.0)
        else:
            keep = row >= s
            sa = jnp.where(keep, pltpu.roll(a, s, 0), 1.0)
            sb = jnp.where(keep, pltpu.roll(b, s, 0), 0.0)
        b = b + a * sb
        a = a * sa
        s *= 2
    return a, b


def _whole():
    return pl.BlockSpec(memory_space=pltpu.VMEM)


def _rows_spec(rows, cols, n_tiles, reverse=False):
    if reverse:
        return pl.BlockSpec((rows, cols), lambda i: (n_tiles - 1 - i, 0))
    return pl.BlockSpec((rows, cols), lambda i: (i, 0))


def _halo_spec(cols, tile_rows, n_tiles, reverse=False):
    per = tile_rows // 8
    if reverse:
        return pl.BlockSpec((8, cols), lambda i: (jnp.maximum((n_tiles - 1 - i) * per - 1, 0), 0))
    return pl.BlockSpec((8, cols), lambda i: (jnp.maximum(i * per - 1, 0), 0))


def _call(body, name, n_tiles, in_specs, out_specs, out_shape, scratch=()):
    return pl.pallas_call(
        body, name=name, grid=(n_tiles,), in_specs=in_specs, out_specs=out_specs, out_shape=out_shape,
        scratch_shapes=list(scratch),
        compiler_params=pltpu.CompilerParams(dimension_semantics=("arbitrary",), vmem_limit_bytes=VMEM_MB << 20))


def _sds(shape, dtype=F32):
    return jax.ShapeDtypeStruct(shape, dtype)


def _inproj_fwd(x, w, b, cos_t, sin_t):
    n = x.shape[0]
    nt = n // TILE

    def body(x_ref, w_ref, b_ref, c_ref, s_ref, q_ref, k_ref, v_ref, u_ref, xr_ref, g_ref):
        p = _dot(_mx(x_ref[...]), w_ref[...]) + b_ref[...]
        cos, sin = c_ref[...], s_ref[...]
        q_ref[...] = _mx(_rope(p[:, :D_ATTN], jnp.tile(cos, (1, 4)), jnp.tile(sin, (1, 4))))
        k_ref[...] = _mx(_rope(p[:, 512:640], cos, sin))
        v_ref[...] = _mx(p[:, 640:768])
        u_ref[...] = p[:, 768:1024]
        xr_ref[...] = p[:, 1024:1280]
        g_ref[...] = p[:, 1280:1536]

    r = functools.partial(_rows_spec, n_tiles=nt)
    return _call(
        body, "inproj_fwd", nt,
        [r(TILE, D), _whole(), _whole(), r(TILE, 128), r(TILE, 128)],
        [r(TILE, D_ATTN), r(TILE, D_KV), r(TILE, D_KV), r(TILE, D_S5), r(TILE, D_LRU), r(TILE, D_LRU)],
        [_sds((n, D_ATTN), MXU), _sds((n, D_KV), MXU), _sds((n, D_KV), MXU), _sds((n, D_S5)), _sds((n, D_LRU)), _sds((n, D_LRU))],
    )(x, w, b, cos_t, sin_t)


def _inproj_bwd(dq, dk, dv, du, dxr, dgate, cos_t, sin_t, x0, dr1, w_t):
    n = x0.shape[0]
    nt = n // TILE

    def body(dq_ref, dk_ref, dv_ref, du_ref, dxr_ref, dg_ref, c_ref, s_ref, x_ref, dr_ref, w_ref, dx_ref, dw_ref, db_ref):
        @pl.when(pl.program_id(0) == 0)
        def _():
            dw_ref[...] = jnp.zeros_like(dw_ref)
            db_ref[...] = jnp.zeros_like(db_ref)

        cos, sin = c_ref[...], s_ref[...]
        dtq = _rope_t(dq_ref[...], jnp.tile(cos, (1, 4)), jnp.tile(sin, (1, 4)))
        dtk = _rope_t(dk_ref[...], cos, sin)
        dp = jnp.concatenate([dtq, dtk, dv_ref[...], du_ref[...], dxr_ref[...], dg_ref[...]], axis=1)
        db_ref[...] += _sum0(dp)
        dpb = _mx(dp)
        dw_ref[...] += _dot_tn(dpb, _mx(x_ref[...]))
        dx_ref[...] = ALPHA * dr_ref[...] + _dot(dpb, w_ref[...])

    r = functools.partial(_rows_spec, n_tiles=nt)
    return _call(
        body, "inproj_bwd", nt,
        [r(TILE, D_ATTN), r(TILE, D_KV), r(TILE, D_KV), r(TILE, D_S5), r(TILE, D_LRU), r(TILE, D_LRU), r(TILE, 128), r(TILE, 128),
         r(TILE, D), r(TILE, D), _whole()],
        [r(TILE, D), _whole(), _whole()],
        [_sds((n, D)), _sds((D_IN, D)), _sds((1, D_IN))],
    )(dq, dk, dv, du, dxr, dgate, cos_t, sin_t, x0, dr1, w_t)


def _kv_variants(t, lo):
    tr = pltpu.roll(t, 64, 1)
    out = []
    for j in range(2):
        first = jnp.where(lo, t if j == 0 else tr, 0.0)
        second = jnp.where(lo, 0.0, tr if j == 0 else t)
        out.append(_mx(jnp.concatenate([first, second], axis=0)))
    return out


def _kv_collect(x0, x1, lo):
    a = x0[:256] + pltpu.roll(x0[256:], 64, 1)
    b = pltpu.roll(x1[:256], 64, 1) + x1[256:]
    return jnp.where(lo, a, b)


def _attn_probs(s, sink_ref):
    out = []
    for hp in range(2):
        sh = s[:, hp * 256:(hp + 1) * 256]
        sink = sink_ref[hp]
        m = jnp.maximum(jnp.max(sh, axis=1, keepdims=True), sink)
        p = jnp.exp(sh - m)
        es = jnp.exp(sink - m)
        inv = 1.0 / (jnp.sum(p, axis=1, keepdims=True) + es)
        out.append((p * inv, es * inv))
    return out


def _attn_scores(q_ref, k_ref, v_ref, nb):
    lo = lax.broadcasted_iota(jnp.int32, (256, 128), 1) < 64
    kcats, vcats, kstarts, parts = [], [], [], []
    for b in range(nb):
        block = pl.program_id(0) * nb + b
        kstart = pl.multiple_of(jnp.maximum(block - 1, 0) * ATTN_BLOCK, ATTN_BLOCK)
        kcat = _kv_variants(k_ref[pl.ds(kstart, 256), :].astype(F32), lo)
        kcats.append(kcat)
        vcats.append(_kv_variants(v_ref[pl.ds(kstart, 256), :].astype(F32), lo))
        kstarts.append(kstart)
        valid = _attn_mask(block, kstart)
        for i in range(4):
            s = _dot_nt(q_ref[b * ATTN_BLOCK:(b + 1) * ATTN_BLOCK, i * 128:(i + 1) * 128], kcat[i // 2]) * 0.125
            parts.append(jnp.where(valid, s, -jnp.inf))
    return jnp.concatenate(parts, axis=0), kcats, vcats, kstarts


def _attn_mask(block, kstart):
    col = lax.broadcasted_iota(jnp.int32, (ATTN_BLOCK, 512), 1)
    row = lax.broadcasted_iota(jnp.int32, (ATTN_BLOCK, 512), 0)
    diff = (block * ATTN_BLOCK + row) - (kstart + (col & 255))
    return (diff >= 0) & (diff < ATTN_BLOCK)


def _attn_fwd(q, k, v, sink_cols):
    n = q.shape[0]
    nt = n // TILE_Q
    nb = TILE_Q // ATTN_BLOCK

    def body(q_ref, k_ref, v_ref, s_ref, o_ref):
        s, _, vcats, _ = _attn_scores(q_ref, k_ref, v_ref, nb)
        (p0, _), (p1, _) = _attn_probs(s, s_ref)
        pb = _mx(jnp.concatenate([p0, p1], axis=1))
        for b in range(nb):
            for i in range(4):
                unit = (b * 4 + i) * ATTN_BLOCK
                o_ref[b * ATTN_BLOCK:(b + 1) * ATTN_BLOCK, i * 128:(i + 1) * 128] = _dot(pb[unit:unit + ATTN_BLOCK], vcats[b][i // 2])

    return _call(
        body, "attn_fwd", nt,
        [_rows_spec(TILE_Q, D_ATTN, nt), _whole(), _whole(), _whole()],
        _rows_spec(TILE_Q, D_ATTN, nt), _sds((n, D_ATTN)),
    )(q, k, v, sink_cols)


def _attn_bwd(q, k, v, sink_cols, o, do):
    n = q.shape[0]
    nt = n // TILE_Q
    nb = TILE_Q // ATTN_BLOCK

    def body(q_ref, k_ref, v_ref, s_ref, o_ref, do_ref, dq_ref, dk_ref, dv_ref, ds_ref):
        @pl.when(pl.program_id(0) == 0)
        def _():
            dk_ref[...] = jnp.zeros_like(dk_ref)
            dv_ref[...] = jnp.zeros_like(dv_ref)
            ds_ref[...] = jnp.zeros_like(ds_ref)

        lo = lax.broadcasted_iota(jnp.int32, (256, 128), 1) < 64
        s, kcats, vcats, kstarts = _attn_scores(q_ref, k_ref, v_ref, nb)
        probs = _attn_probs(s, s_ref)
        do = do_ref[...]
        dob = _mx(do)
        od = do * o_ref[...]
        lo_q = (lax.broadcasted_iota(jnp.int32, od.shape, 1) & 64) == 0
        od_head = (jnp.where(lo_q, od, 0.0), jnp.where(lo_q, 0.0, od))
        units = [(b, i) for b in range(nb) for i in range(4)]

        def tile_part(a, b, i):
            return a[b * ATTN_BLOCK:(b + 1) * ATTN_BLOCK, i * 128:(i + 1) * 128]

        dp = jnp.concatenate([_dot_nt(tile_part(dob, b, i), vcats[b][i // 2]) for b, i in units], axis=0)
        ds = []
        for hp in range(2):
            p, p_sink = probs[hp]
            delta = jnp.concatenate([jnp.sum(tile_part(od_head[hp], b, i), axis=1, keepdims=True) for b, i in units], axis=0)
            ds.append(p * (dp[:, hp * 256:(hp + 1) * 256] - delta) * 0.125)
            t = p_sink * delta
            for i in range(4):
                dsink = sum(_sum0(t[(b * 4 + i) * ATTN_BLOCK:(b * 4 + i + 1) * ATTN_BLOCK]) for b in range(nb))
                ds_ref[2 * i + hp:2 * i + hp + 1, :] -= jnp.broadcast_to(dsink, (1, 128))
        dsb = _mx(jnp.concatenate(ds, axis=1))
        pb = _mx(jnp.concatenate([probs[0][0], probs[1][0]], axis=1))
        for b in range(nb):
            dkc = [jnp.zeros((512, 128), F32), jnp.zeros((512, 128), F32)]
            dvc = [jnp.zeros((512, 128), F32), jnp.zeros((512, 128), F32)]
            for i in range(4):
                j = i // 2
                unit = slice((b * 4 + i) * ATTN_BLOCK, (b * 4 + i + 1) * ATTN_BLOCK)
                dq_ref[b * ATTN_BLOCK:(b + 1) * ATTN_BLOCK, i * 128:(i + 1) * 128] = _dot(dsb[unit], kcats[b][j])
                dkc[j] = dkc[j] + _dot_tn(dsb[unit], tile_part(q_ref, b, i))
                dvc[j] = dvc[j] + _dot_tn(pb[unit], tile_part(dob, b, i))
            dk_ref[pl.ds(kstarts[b], 256), :] += _kv_collect(dkc[0], dkc[1], lo)
            dv_ref[pl.ds(kstarts[b], 256), :] += _kv_collect(dvc[0], dvc[1], lo)

    r = _rows_spec(TILE_Q, D_ATTN, nt)
    return _call(
        body, "attn_bwd", nt,
        [r, _whole(), _whole(), _whole(), r, r],
        [r, _whole(), _whole(), _whole()],
        [_sds((n, D_ATTN)), _sds((n, D_KV)), _sds((n, D_KV)), _sds((8, 128))],
    )(q, k, v, sink_cols, o, do)


S5_TABLES = ('pw_re', 'pw_im', 'dbl_re', 'dbl_im', 'seg_re', 'seg_im', 'segr_re', 'segr_im')
S5_WEIGHTS = ('b_re', 'b_im', 'c_re', 'c_im', 'd', 'glu_w', 'glu_b', 'perm', 'perm_t')


def _s5_states(u, carry_r, carry_i, b_re, b_im, tab, hr_s, hi_s):
    ub = _mx(u)
    hr_s[...] = _dot(ub, b_re[...])
    hi_s[...] = _dot(ub, b_im[...])
    return ub, _cscan(hr_s, hi_s, tab, carry_r, carry_i, reverse=False)


def _s5_fwd(u, prm):
    n = u.shape[0]
    nt = n // TILE

    def body(u_ref, *refs):
        tab = dict(zip(S5_TABLES, refs[:8]))
        b_re, b_im, c_re, c_im, d_ref, gw_ref, gb_ref, perm, perm_t = refs[8:17]
        y_ref, cr_out, ci_out, cr_s, ci_s, hr_s, hi_s = refs[17:]

        @pl.when(pl.program_id(0) == 0)
        def _():
            cr_s[...] = jnp.zeros_like(cr_s)
            ci_s[...] = jnp.zeros_like(ci_s)

        u = _permute_rows(perm[...], u_ref[...])
        cr, ci = cr_s[...], ci_s[...]
        cr_out[...] = jnp.broadcast_to(cr, (8, N_STATE))
        ci_out[...] = jnp.broadcast_to(ci, (8, N_STATE))
        _, (cr_s[...], ci_s[...]) = _s5_states(u, cr, ci, b_re, b_im, tab, hr_s, hi_s)
        y = _dot(_mx(hr_s[...]), c_re[...]) - _dot(_mx(hi_s[...]), c_im[...]) + d_ref[...] * u
        z = _gelu(y)
        y_ref[...] = _permute_rows(perm_t[...], z * _sigmoid(_dot(_mx(z), gw_ref[...]) + gb_ref[...]))

    r = functools.partial(_rows_spec, n_tiles=nt)
    return _call(
        body, "s5_fwd", nt,
        [r(TILE, D_S5)] + [_whole()] * 17,
        [r(TILE, D_S5), r(8, N_STATE), r(8, N_STATE)],
        [_sds((n, D_S5)), _sds((nt * 8, N_STATE)), _sds((nt * 8, N_STATE))],
        scratch=[pltpu.VMEM((1, N_STATE), F32)] * 2 + [pltpu.VMEM((TILE, N_STATE), F32)] * 2,
    )(u, *[prm[k] for k in S5_TABLES + S5_WEIGHTS])


def _s5_bwd(u, dys, carry_re, carry_im, prm):
    n = u.shape[0]
    nt = n // TILE

    def body(u_ref, dy_ref, cin_r, cin_i, *refs):
        tab = dict(zip(S5_TABLES, refs[:8]))
        b_re, b_im, c_re, c_im, d_ref, gw_ref, gb_ref, perm, perm_t = refs[8:17]
        du_ref, dbr_ref, dbi_ref, dcr_ref, dci_ref, dar_ref, dai_ref, dd_ref, dgw_ref, dgb_ref = refs[17:27]
        gr_s, gi_s, hr_s, hi_s, gr_t, gi_t = refs[27:]

        @pl.when(pl.program_id(0) == 0)
        def _():
            for ref in (dbr_ref, dbi_ref, dcr_ref, dci_ref, dar_ref, dai_ref, dd_ref, dgw_ref, dgb_ref, gr_s, gi_s):
                ref[...] = jnp.zeros_like(ref)

        u = _permute_rows(perm[...], u_ref[...])
        cr, ci = cin_r[0:1, :], cin_i[0:1, :]
        ub, _ = _s5_states(u, cr, ci, b_re, b_im, tab, hr_s, hi_s)
        hrb, hib = _mx(hr_s[...]), _mx(hi_s[...])
        y = _dot(hrb, c_re[...]) - _dot(hib, c_im[...]) + d_ref[...] * u
        z = _gelu(y)
        zb = _mx(z)
        sg = _sigmoid(_dot(zb, gw_ref[...]) + gb_ref[...])
        dout = _permute_rows(perm[...], dy_ref[...])
        dpre = dout * z * sg * (1.0 - sg)
        dgb_ref[...] += _sum0(dpre)
        dpb = _mx(dpre)
        dgw_ref[...] += _dot_tn(zb, dpb)
        dy = (dout * sg + _dot_nt(dpb, gw_ref[...])) * _gelu_grad(y)
        dd_ref[...] += _sum0(dy * u)
        dyb = _mx(dy)
        dcr_ref[...] += _dot_tn(hrb, dyb)
        dci_ref[...] -= _dot_tn(hib, dyb)
        gr_t[...] = _dot_nt(dyb, c_re[...])
        gi_t[...] = -_dot_nt(dyb, c_im[...])
        gr_s[...], gi_s[...] = _cscan(gr_t, gi_t, tab, gr_s[...], gi_s[...], reverse=True)
        sub = _row_iota((N_SEG, N_STATE))
        acc_r = acc_i = jnp.zeros((N_SEG, N_STATE), F32)
        for k in range(SEG):
            if k == 0:
                hpr = jnp.where(sub >= 1, pltpu.roll(hr_s[_seg_rows(SEG - 1), :], 1, 0), cr)
                hpi = jnp.where(sub >= 1, pltpu.roll(hi_s[_seg_rows(SEG - 1), :], 1, 0), ci)
            else:
                hpr, hpi = hr_s[_seg_rows(k - 1), :], hi_s[_seg_rows(k - 1), :]
            gr, gi = gr_t[_seg_rows(k), :], gi_t[_seg_rows(k), :]
            acc_r = acc_r + gr * hpr + gi * hpi
            acc_i = acc_i + gi * hpr - gr * hpi
        dar_ref[...] += _sum0(acc_r)
        dai_ref[...] += _sum0(acc_i)
        grb, gib = _mx(gr_t[...]), _mx(gi_t[...])
        dbr_ref[...] += _dot_tn(ub, grb)
        dbi_ref[...] += _dot_tn(ub, gib)
        du_ref[...] = _permute_rows(perm_t[...], dy * d_ref[...] + _dot_nt(grb, b_re[...]) + _dot_nt(gib, b_im[...]))

    r = functools.partial(_rows_spec, n_tiles=nt, reverse=True)
    return _call(
        body, "s5_bwd", nt,
        [r(TILE, D_S5), r(TILE, D_S5), r(8, N_STATE), r(8, N_STATE)] + [_whole()] * 17,
        [r(TILE, D_S5)] + [_whole()] * 9,
        [_sds((n, D_S5)), _sds((D_S5, N_STATE)), _sds((D_S5, N_STATE)), _sds((N_STATE, D_S5)), _sds((N_STATE, D_S5)),
         _sds((1, N_STATE)), _sds((1, N_STATE)), _sds((1, D_S5)), _sds((D_S5, D_S5)), _sds((1, D_S5))],
        scratch=[pltpu.VMEM((1, N_STATE), F32)] * 2 + [pltpu.VMEM((TILE, N_STATE), F32)] * 4,
    )(u, dys, carry_re, carry_im, *[prm[k] for k in S5_TABLES + S5_WEIGHTS])


def _lru_gates(xr, halo, tile_index, cw_ref, cb_ref, wx_ref, wa_ref, bx_ref, ba_ref, sp_ref):
    ext = jnp.concatenate([halo, xr], axis=0)
    sh = [xr] + [_shift_down(ext, j, TILE) for j in (1, 2, 3)]
    xc = cb_ref[...] + cw_ref[3:4, :] * sh[0] + cw_ref[2:3, :] * sh[1] + cw_ref[1:2, :] * sh[2] + cw_ref[0:1, :] * sh[3]
    xb = _mx(xc)
    gx = _sigmoid(_dot(xb, wx_ref[...]) + bx_ref[...])
    ga = _sigmoid(_dot(xb, wa_ref[...]) + ba_ref[...])
    la = -LRU_C * ga * sp_ref[...]
    a = jnp.exp(la)
    start = (tile_index * TILE + _row_iota(xr.shape)) == 0
    mult = jnp.where(start, 1.0, jnp.sqrt(-jnp.tanh(la) * (a * a + 1.0)))
    return sh, xc, xb, gx, ga, a, mult, start


def _lru_fwd(xr, gate, prm):
    n = xr.shape[0]
    nt = n // TILE

    def body(x_ref, g_ref, cw_ref, cb_ref, wx_ref, wa_ref, bx_ref, ba_ref, sp_ref, y_ref, c_out, halo_s, c_s):
        first_tile = pl.program_id(0) == 0

        @pl.when(first_tile)
        def _():
            halo_s[...] = jnp.zeros_like(halo_s)
            c_s[...] = jnp.zeros_like(c_s)

        xr = x_ref[...]
        _, xc, _, gx, _, a, mult, _ = _lru_gates(xr, halo_s[...], pl.program_id(0), cw_ref, cb_ref, wx_ref, wa_ref, bx_ref, ba_ref,
                                                 sp_ref)
        halo_s[...] = xr[TILE - 8:]
        acum, h = _rscan(a, mult * gx * xc, reverse=False)
        c = c_s[...]
        c_out[...] = jnp.broadcast_to(c, (8, D_LRU))
        h = h + acum * c
        c_s[...] = h[TILE - 1:TILE]
        y_ref[...] = h * _gelu(g_ref[...])

    r = functools.partial(_rows_spec, n_tiles=nt)
    return _call(
        body, "lru_fwd", nt,
        [r(TILE, D_LRU), r(TILE, D_LRU)] + [_whole()] * 7,
        [r(TILE, D_LRU), r(8, D_LRU)],
        [_sds((n, D_LRU)), _sds((nt * 8, D_LRU))],
        scratch=[pltpu.VMEM((8, D_LRU), F32), pltpu.VMEM((1, D_LRU), F32)],
    )(xr, gate, prm['conv_w'], prm['conv_b'], prm['wx'], prm['wa'], prm['bx'], prm['ba'], prm['sp'])


def _lru_bwd(xr, gate, dyl, carry, prm):
    n = xr.shape[0]
    nt = n // TILE

    def body(x_ref, xh_ref, g_ref, dy_ref, cin_ref, cw_ref, cb_ref, wx_ref, wa_ref, bx_ref, ba_ref, sp_ref,
             dx_ref, dg_ref, dcw0, dcw1, dcw2, dcw3, dcb_ref, dwx_ref, dwa_ref, dbx_ref, dba_ref, dsp_ref, an_s, gn_s, dn_s):
        first_tile = pl.program_id(0) == nt - 1

        @pl.when(pl.program_id(0) == 0)
        def _():
            for ref in (dcw0, dcw1, dcw2, dcw3, dcb_ref, dwx_ref, dwa_ref, dbx_ref, dba_ref, dsp_ref, gn_s, dn_s):
                ref[...] = jnp.zeros_like(ref)
            an_s[...] = jnp.ones_like(an_s)

        xr = x_ref[...]
        halo = jnp.where(first_tile, 0.0, xh_ref[...])
        sh, xc, xb, gx, ga, a, mult, start = _lru_gates(xr, halo, nt - 1 - pl.program_id(0), cw_ref, cb_ref, wx_ref, wa_ref, bx_ref,
                                                        ba_ref, sp_ref)
        acum, h = _rscan(a, mult * gx * xc, reverse=False)
        cin = cin_ref[0:1, :]
        h = h + acum * cin
        gate = g_ref[...]
        dyl = dy_ref[...]
        dg_ref[...] = dyl * h * _gelu_grad(gate)
        row = _row_iota(xr.shape)
        alpha = jnp.where(row < TILE - 1, pltpu.roll(a, TILE - 1, 0), an_s[...])
        racc, g = _rscan(alpha, dyl * _gelu(gate), reverse=True)
        g = g + racc * gn_s[...]
        an_s[...] = a[0:1]
        gn_s[...] = g[0:1]
        hprev = jnp.where(row == 0, cin, pltpu.roll(h, 1, 0))
        da = g * hprev
        dmult = jnp.where(start, 0.0, g * gx * xc)
        dla = da * a - dmult * a * a / mult
        dsp_ref[...] += _sum0(-LRU_C * ga * dla)
        dpa = (-LRU_C * sp_ref[...] * dla) * ga * (1.0 - ga)
        dpx = (g * mult * xc) * gx * (1.0 - gx)
        dba_ref[...] += _sum0(dpa)
        dbx_ref[...] += _sum0(dpx)
        dpab, dpxb = _mx(dpa), _mx(dpx)
        dwa_ref[...] += _dot_tn(xb, dpab)
        dwx_ref[...] += _dot_tn(xb, dpxb)
        dxc = g * mult * gx + _dot_nt(dpab, wa_ref[...]) + _dot_nt(dpxb, wx_ref[...])
        dcb_ref[...] += _sum0(dxc)
        dcw3[...] += _sum0(dxc * sh[0])
        dcw2[...] += _sum0(dxc * sh[1])
        dcw1[...] += _sum0(dxc * sh[2])
        dcw0[...] += _sum0(dxc * sh[3])
        ext = jnp.concatenate([dxc, dn_s[...]], axis=0)
        dx_ref[...] = (cw_ref[3:4, :] * dxc + cw_ref[2:3, :] * _shift_up(ext, 1, TILE) + cw_ref[1:2, :] * _shift_up(ext, 2, TILE)
                       + cw_ref[0:1, :] * _shift_up(ext, 3, TILE))
        dn_s[...] = dxc[:8]

    r = functools.partial(_rows_spec, n_tiles=nt, reverse=True)
    vec = _sds((1, D_LRU))
    return _call(
        body, "lru_bwd", nt,
        [r(TILE, D_LRU), _halo_spec(D_LRU, TILE, nt, reverse=True), r(TILE, D_LRU), r(TILE, D_LRU), r(8, D_LRU)] + [_whole()] * 7,
        [r(TILE, D_LRU), r(TILE, D_LRU)] + [_whole()] * 10,
        [_sds((n, D_LRU)), _sds((n, D_LRU)), vec, vec, vec, vec, vec, _sds((D_LRU, D_LRU)), _sds((D_LRU, D_LRU)), vec, vec, vec],
        scratch=[pltpu.VMEM((1, D_LRU), F32), pltpu.VMEM((1, D_LRU), F32), pltpu.VMEM((8, D_LRU), F32)],
    )(xr, xr, gate, dyl, carry, prm['conv_w'], prm['conv_b'], prm['wx'], prm['wa'], prm['bx'], prm['ba'], prm['sp'])


def _normed_parts(ya, ys, yl):
    return jnp.concatenate([ya * _rms(ya), ys * _rms(ys), yl * _rms(yl)], axis=1)


def _mixout_fwd(ya, ys, yl, x0, g_mix, w_out, b_out, g1, b1):
    n = x0.shape[0]
    nt = n // TILE

    def body(ya_ref, ys_ref, yl_ref, x_ref, gm_ref, w_ref, b_ref, g_ref, be_ref, mix_ref, r_ref, x1_ref):
        mixb = _mx(_normed_parts(ya_ref[...], ys_ref[...], yl_ref[...]) * gm_ref[...])
        mix_ref[...] = mixb
        r1 = ALPHA * x_ref[...] + _dot(mixb, w_ref[...]) + b_ref[...]
        r_ref[...] = r1
        xhat, _ = _ln_stats(r1)
        x1_ref[...] = xhat * g_ref[...] + be_ref[...]

    r = functools.partial(_rows_spec, n_tiles=nt)
    return _call(
        body, "mixout_fwd", nt,
        [r(TILE, D_ATTN), r(TILE, D_S5), r(TILE, D_LRU), r(TILE, D)] + [_whole()] * 5,
        [r(TILE, D), r(TILE, D), r(TILE, D)],
        [_sds((n, D), MXU), _sds((n, D)), _sds((n, D))],
    )(ya, ys, yl, x0, g_mix, w_out, b_out, g1, b1)


def _mixout_bwd(dr1, mix, ya, ys, yl, g_mix, w_out):
    n = dr1.shape[0]
    nt = n // TILE

    def body(dr_ref, mix_ref, ya_ref, ys_ref, yl_ref, gm_ref, w_ref, dya_ref, dys_ref, dyl_ref, dw_ref, db_ref, dgm_ref):
        @pl.when(pl.program_id(0) == 0)
        def _():
            for ref in (dw_ref, db_ref, dgm_ref):
                ref[...] = jnp.zeros_like(ref)

        dr = dr_ref[...]
        db_ref[...] += _sum0(dr)
        drb = _mx(dr)
        dw_ref[...] += _dot_tn(mix_ref[...], drb)
        dmix = _dot(drb, w_ref[...])
        parts = (ya_ref[...], ys_ref[...], yl_ref[...])
        dgm_ref[...] += _sum0(dmix * _normed_parts(*parts))
        dn = dmix * gm_ref[...]
        lo = 0
        for y, out in zip(parts, (dya_ref, dys_ref, dyl_ref)):
            w = y.shape[1]
            rs = _rms(y)
            nrm = y * rs
            dnp = dn[:, lo:lo + w]
            out[...] = rs * (dnp - nrm * jnp.mean(dnp * nrm, axis=-1, keepdims=True))
            lo += w

    r = functools.partial(_rows_spec, n_tiles=nt)
    return _call(
        body, "mixout_bwd", nt,
        [r(TILE, D), r(TILE, D), r(TILE, D_ATTN), r(TILE, D_S5), r(TILE, D_LRU), _whole(), _whole()],
        [r(TILE, D_ATTN), r(TILE, D_S5), r(TILE, D_LRU), _whole(), _whole(), _whole()],
        [_sds((n, D_ATTN)), _sds((n, D_S5)), _sds((n, D_LRU)), _sds((D, D)), _sds((1, D)), _sds((1, D))],
    )(dr1, mix, ya, ys, yl, g_mix, w_out)


def _ffn_conv(gp, halo, cw_ref, cb_ref, cs):
    ext = jnp.concatenate([halo, gp], axis=0)
    s1 = _shift_down(ext, 1, TILE)
    s2 = _shift_down(ext, 2, TILE)
    return s1, s2, cb_ref[:, cs] + cw_ref[2:3, cs] * gp + cw_ref[1:2, cs] * s1 + cw_ref[0:1, cs] * s2


def _ffn_fwd(x1, wg, wu, cw, cb, wd, g2, b2):
    n = x1.shape[0]
    nt = n // TILE

    def body(x_ref, wg_ref, wu_ref, cw_ref, cb_ref, wd_ref, g_ref, be_ref, gp_ref, up_ref, r_ref, x2_ref, halo_s):
        @pl.when(pl.program_id(0) == 0)
        def _():
            halo_s[...] = jnp.zeros_like(halo_s)

        x1 = x_ref[...]
        xb = _mx(x1)
        f = jnp.zeros((TILE, D), F32)
        for c in range(D_FF // FF_CHUNK):
            cs = slice(c * FF_CHUNK, (c + 1) * FF_CHUNK)
            gp = _dot(xb, wg_ref[:, cs])
            up = _dot(xb, wu_ref[:, cs])
            gp_ref[:, cs] = gp
            up_ref[:, cs] = up
            _, _, gc = _ffn_conv(gp, halo_s[:, cs], cw_ref, cb_ref, cs)
            halo_s[:, cs] = gp[TILE - 8:]
            f = f + _dot(_mx(gc * _sigmoid(gc) * up), wd_ref[cs, :])
        r2 = ALPHA * x1 + f
        r_ref[...] = r2
        xhat, _ = _ln_stats(r2)
        x2_ref[...] = xhat * g_ref[...] + be_ref[...]

    r = functools.partial(_rows_spec, n_tiles=nt)
    return _call(
        body, "ffn_fwd", nt,
        [r(TILE, D)] + [_whole()] * 7,
        [r(TILE, D_FF), r(TILE, D_FF), r(TILE, D), r(TILE, D)],
        [_sds((n, D_FF)), _sds((n, D_FF)), _sds((n, D)), _sds((n, D))],
        scratch=[pltpu.VMEM((8, D_FF), F32)],
    )(x1, wg, wu, cw, cb, wd, g2, b2)


def _ffn_bwd_down(dx2, r2, g2, gp, up, cw, cb, wd_t):
    n = dx2.shape[0]
    nt = n // TILE

    def body(dx_ref, r_ref, g_ref, gp_ref, gh_ref, up_ref, cw_ref, cb_ref, wd_ref,
             dr_ref, dgp_ref, dup_ref, dwd_ref, dcw0, dcw1, dcw2, dcb_ref, dg_ref, db_ref, next_s):
        first_tile = pl.program_id(0) == nt - 1

        @pl.when(pl.program_id(0) == 0)
        def _():
            for ref in (dwd_ref, dcw0, dcw1, dcw2, dcb_ref, dg_ref, db_ref, next_s):
                ref[...] = jnp.zeros_like(ref)

        dx2 = dx_ref[...]
        xhat, rstd = _ln_stats(r_ref[...])
        dg_ref[...] += _sum0(dx2 * xhat)
        db_ref[...] += _sum0(dx2)
        dr2 = _ln_bwd(dx2, g_ref[...], xhat, rstd)
        dr_ref[...] = dr2
        dfb = _mx(dr2)
        for c in range(D_FF // FF_CHUNK):
            cs = slice(c * FF_CHUNK, (c + 1) * FF_CHUNK)
            gp = gp_ref[:, cs]
            up = up_ref[:, cs]
            s1, s2, gc = _ffn_conv(gp, jnp.where(first_tile, 0.0, gh_ref[:, cs]), cw_ref, cb_ref, cs)
            sg = _sigmoid(gc)
            silu = gc * sg
            dact = _dot(dfb, wd_ref[:, cs])
            dwd_ref[cs, :] += _dot_tn(_mx(silu * up), dfb)
            dup_ref[:, cs] = _mx(dact * silu)
            dgc = dact * up * (sg * (1.0 + gc * (1.0 - sg)))
            dcb_ref[:, cs] += _sum0(dgc)
            dcw2[:, cs] += _sum0(dgc * gp)
            dcw1[:, cs] += _sum0(dgc * s1)
            dcw0[:, cs] += _sum0(dgc * s2)
            ext = jnp.concatenate([dgc, next_s[:, cs]], axis=0)
            dgp_ref[:, cs] = _mx(cw_ref[2:3, cs] * dgc + cw_ref[1:2, cs] * _shift_up(ext, 1, TILE)
                                 + cw_ref[0:1, cs] * _shift_up(ext, 2, TILE))
            next_s[:, cs] = dgc[:8]

    r = functools.partial(_rows_spec, n_tiles=nt, reverse=True)
    vff = _sds((1, D_FF))
    return _call(
        body, "ffn_bwd_down", nt,
        [r(TILE, D), r(TILE, D), _whole(), r(TILE, D_FF), _halo_spec(D_FF, TILE, nt, reverse=True), r(TILE, D_FF), _whole(), _whole(),
         _whole()],
        [r(TILE, D), r(TILE, D_FF), r(TILE, D_FF)] + [_whole()] * 7,
        [_sds((n, D)), _sds((n, D_FF), MXU), _sds((n, D_FF), MXU), _sds((D_FF, D)), vff, vff, vff, vff, _sds((1, D)), _sds((1, D))],
        scratch=[pltpu.VMEM((8, D_FF), F32)],
    )(dx2, r2, g2, gp, gp, up, cw, cb, wd_t)


def _ffn_bwd_dx(dr2, dgp, dup, r1, g1, wg_t, wu_t):
    n = dr2.shape[0]
    rows = TILE_BIG
    nt = n // rows

    def body(dr2_ref, dgp_ref, dup_ref, r_ref, g_ref, wg_ref, wu_ref, dr1_ref, dg_ref, db_ref):
        @pl.when(pl.program_id(0) == 0)
        def _():
            for ref in (dg_ref, db_ref):
                ref[...] = jnp.zeros_like(ref)

        dx1 = ALPHA * dr2_ref[...] + _dot(dgp_ref[...], wg_ref[...]) + _dot(dup_ref[...], wu_ref[...])
        xhat, rstd = _ln_stats(r_ref[...])
        dg_ref[...] += _sum0(dx1 * xhat)
        db_ref[...] += _sum0(dx1)
        dr1_ref[...] = _ln_bwd(dx1, g_ref[...], xhat, rstd)

    r = functools.partial(_rows_spec, n_tiles=nt)
    return _call(
        body, "ffn_bwd_dx", nt,
        [r(rows, D), r(rows, D_FF), r(rows, D_FF), r(rows, D), _whole(), _whole(), _whole()],
        [r(rows, D), _whole(), _whole()],
        [_sds((n, D)), _sds((1, D)), _sds((1, D))],
    )(dr2, dgp, dup, r1, g1, wg_t, wu_t)


def _ffn_bwd_dw(x1, dgp, dup):
    n = x1.shape[0]
    rows = TILE_BIG
    nt = n // rows

    def body(x_ref, dgp_ref, dup_ref, dwg_ref, dwu_ref):
        @pl.when(pl.program_id(0) == 0)
        def _():
            for ref in (dwg_ref, dwu_ref):
                ref[...] = jnp.zeros_like(ref)

        xb = _mx(x_ref[...])
        for c in range(D_FF // FF_CHUNK):
            cs = slice(c * FF_CHUNK, (c + 1) * FF_CHUNK)
            dwg_ref[cs, :] += _dot_tn(dgp_ref[:, cs], xb)
            dwu_ref[cs, :] += _dot_tn(dup_ref[:, cs], xb)

    r = functools.partial(_rows_spec, n_tiles=nt)
    return _call(
        body, "ffn_bwd_dw", nt,
        [r(rows, D), r(rows, D_FF), r(rows, D_FF)], [_whole(), _whole()], [_sds((D_FF, D)), _sds((D_FF, D))],
    )(x1, dgp, dup)


def _loss_head(y, target):
    n = y.shape[0]
    nt = n // TILE

    def body(y_ref, t_ref, loss_ref, dy_ref):
        @pl.when(pl.program_id(0) == 0)
        def _():
            loss_ref[...] = jnp.zeros_like(loss_ref)

        e = y_ref[...] - t_ref[...]
        dy_ref[...] = e * (1.0 / D)
        loss_ref[...] += _sum0(jnp.sum(e * e, axis=1, keepdims=True)) * (0.5 / D)

    r = functools.partial(_rows_spec, n_tiles=nt)
    return _call(body, "loss_head", nt, [r(TILE, D), r(TILE, D)], [_whole(), r(TILE, D)], [_sds((1, 1)), _sds((n, D))])(y, target)


def _place():
    x, y, c = lax.axis_index("x"), lax.axis_index("y"), lax.axis_index("c")
    return x, y, c, 4 * x + 2 * y + c


def _peer(x, y, c, k):
    px, py, pc = x ^ ((k >> 2) & 1), y ^ ((k >> 1) & 1), c ^ (k & 1)
    return (px, py, pc), 4 * px + 2 * py + pc


def _all_gather(blocks, small):
    srcs = list(blocks) + [small]
    n = len(srcs)
    out_shapes = [_sds((a.shape[0], N_DEV * a.shape[1], LANES), a.dtype) for a in blocks] + [_sds((N_DEV,) + small.shape, small.dtype)]

    def body(*refs):
        src_refs, out_refs = refs[:n], refs[n:2 * n]
        send_sems, recv_sems, local_sems = refs[2 * n:]
        x, y, c, me = _place()

        def landing(a, slot):
            if a == n - 1:
                return out_refs[a].at[slot]
            r = src_refs[a].shape[1]
            return out_refs[a].at[:, pl.ds(slot * r, r), :]

        def remote(a, k, slot):
            peer, _ = _peer(x, y, c, k)
            return pltpu.make_async_remote_copy(
                src_ref=src_refs[a], dst_ref=landing(a, slot), send_sem=send_sems.at[a * N_DEV + k],
                recv_sem=recv_sems.at[a * N_DEV + k], device_id=peer, device_id_type=pl.DeviceIdType.MESH)

        mine = [pltpu.make_async_copy(src_refs[a], landing(a, me), local_sems.at[a]) for a in range(n)]
        sends = [remote(a, k, me) for a in range(n) for k in range(1, N_DEV)]
        for cp in mine + sends:
            cp.start()
        for a in range(n):
            for k in range(1, N_DEV):
                remote(a, k, _peer(x, y, c, k)[1]).wait_recv()
        for cp in sends:
            cp.wait_send()
        for cp in mine:
            cp.wait()

    any_space = pl.BlockSpec(memory_space=pl.ANY)
    return pl.pallas_call(
        body, name="gather_weights", out_shape=out_shapes, in_specs=[any_space] * n, out_specs=[any_space] * n,
        scratch_shapes=[pltpu.SemaphoreType.DMA((n * N_DEV,)), pltpu.SemaphoreType.DMA((n * N_DEV,)), pltpu.SemaphoreType.DMA((n,))],
    )(*srcs)


def _exchange_grads(big, small):
    nb = len(big)
    rows_a = [a.shape[0] // N_DEV for a in big] + [small.shape[1]]
    offs = [sum(rows_a[:a]) for a in range(nb + 1)]
    total = sum(rows_a)

    def body(*refs):
        src_refs, out_ref = refs[:nb + 1], refs[nb + 1]
        send_sems, recv_sems, local_sem = refs[nb + 2:]
        x, y, c, me = _place()

        def pieces(slot):
            out = []
            for a in range(nb + 1):
                src = src_refs[a].at[slot] if a == nb else src_refs[a].at[pl.ds(slot * rows_a[a], rows_a[a]), :]
                out.append((src, out_ref.at[me, pl.ds(offs[a], rows_a[a]), :]))
            return out

        for src, dst in pieces(me):
            pltpu.make_async_copy(src, dst, local_sem).start()
        for k in range(1, N_DEV):
            peer, peer_slot = _peer(x, y, c, k)
            for src, dst in pieces(peer_slot):
                pltpu.make_async_remote_copy(src_ref=src, dst_ref=dst, send_sem=send_sems.at[k], recv_sem=recv_sems.at[k],
                                             device_id=peer, device_id_type=pl.DeviceIdType.MESH).start()
        slots = []
        for k in range(1, N_DEV):
            peer, peer_slot = _peer(x, y, c, k)
            slots.append(pltpu.make_async_remote_copy(
                src_ref=out_ref.at[me], dst_ref=out_ref.at[peer_slot], send_sem=send_sems.at[k], recv_sem=recv_sems.at[k],
                device_id=peer, device_id_type=pl.DeviceIdType.MESH))
        for cp in slots:
            cp.wait_recv()
        for cp in slots:
            cp.wait_send()
        pltpu.make_async_copy(out_ref.at[me], out_ref.at[me], local_sem).wait()

    any_space = pl.BlockSpec(memory_space=pl.ANY)
    return pl.pallas_call(
        body, name="exchange_grads", out_shape=_sds((N_DEV, total, LANES)), in_specs=[any_space] * (nb + 1), out_specs=any_space,
        scratch_shapes=[pltpu.SemaphoreType.DMA((N_DEV,)), pltpu.SemaphoreType.DMA((N_DEV,)), pltpu.SemaphoreType.DMA(())],
    )(*big, small)


def _reduce_adamw(parts, w, m, v):
    rows = w.shape[0]
    nt = rows // PACK_TILE
    c1 = 1.0 - ADAM_B1 ** ADAM_STEP
    c2 = 1.0 - ADAM_B2 ** ADAM_STEP

    def body(p_ref, w_ref, m_ref, v_ref, g_out, d_out, m_out, v_out):
        g = p_ref[0]
        for s in range(1, N_DEV):
            g = g + p_ref[s]
        m_new = ADAM_B1 * m_ref[...] + (1.0 - ADAM_B1) * g
        v_new = ADAM_B2 * v_ref[...] + (1.0 - ADAM_B2) * (g * g)
        g_out[...] = g
        m_out[...] = m_new
        v_out[...] = v_new
        d_out[...] = -ADAM_LR * ((m_new / c1) / (jnp.sqrt(v_new / c2) + ADAM_EPS) + ADAM_WD * w_ref[...])

    r = _rows_spec(PACK_TILE, LANES, nt)
    out = _sds((rows, LANES))
    return _call(
        body, "reduce_adamw", nt,
        [pl.BlockSpec((N_DEV, PACK_TILE, LANES), lambda i: (0, i, 0)), r, r, r], [r, r, r, r], [out, out, out, out],
    )(parts, w, m, v)


def _pack_rows(a, lead=0):
    head = a.shape[:lead]
    flat = a.reshape(head + (-1,))
    size = flat.shape[-1]
    rows = -(-size // (16 * LANES)) * 16
    flat = jnp.pad(flat, [(0, 0)] * lead + [(0, rows * LANES - size)])
    return flat.reshape(head + (rows, LANES))


def _packed_rows(shape):
    return -(-math.prod(shape) // (16 * LANES)) * 16


def _to_blocks(full, axis):
    l, a, b = full.shape
    if axis == 2:
        return full.reshape(l, a, N_DEV, b // N_DEV).transpose(2, 0, 1, 3)
    return full.reshape(l, N_DEV, a // N_DEV, b).transpose(1, 0, 2, 3)


def _from_blocks(blocks, axis):
    _, l, a, b = blocks.shape
    if axis == 2:
        return blocks.transpose(1, 2, 0, 3).reshape(l, a, N_DEV * b)
    return blocks.transpose(1, 0, 2, 3).reshape(l, N_DEV * a, b)


def _row_form(shard, transposed):
    return shard.transpose(0, 2, 1) if transposed else shard


def _gather_weights(local):
    segs, meta = [], []
    for name in SMALL_SHARDED:
        blk = local[name]
        if name in GATHER_F32:
            bits = lax.bitcast_convert_type(blk, MXU)
        else:
            bits = _mx(blk)
        seg = _pack_rows(bits)
        meta.append((name, bits.shape, seg.shape[0]))
        segs.append(seg)
    *wholes, gathered = _all_gather([_mx(_row_form(local[name], t)) for name, t in BIG], jnp.concatenate(segs, axis=0))
    out = {}
    for (name, t), w in zip(BIG, wholes):
        out[name + '_t' if t else name] = w
        out[name if t else name + '_t'] = w.transpose(0, 2, 1)
    lo = 0
    for name, bits_shape, rows in meta:
        seg = gathered[:, lo:lo + rows].reshape(N_DEV, -1)[:, :math.prod(bits_shape)].reshape((N_DEV,) + bits_shape)
        if name in GATHER_F32:
            seg = lax.bitcast_convert_type(seg, F32)
        out[name] = _from_blocks(seg, SHARD_AXIS[name])
        lo += rows
    return out


def _s5_discretize(a_re, a_im, log_dt, b_re, b_im):
    lam_re = jnp.minimum(a_re, -1e-4)
    lam_im = a_im
    dt = jnp.exp(log_dt)[:, None]
    decay = jnp.exp(dt * lam_re)
    ang = dt * lam_im
    abar_re = decay * jnp.cos(ang)
    abar_im = decay * jnp.sin(ang)
    den = jnp.square(lam_re) + jnp.square(lam_im)
    nr = abar_re - 1.0
    ni = abar_im
    coef_re = (nr * lam_re + ni * lam_im) / den
    coef_im = (ni * lam_re - nr * lam_im) / den
    bbar_re = coef_re[..., None] * b_re - coef_im[..., None] * b_im
    bbar_im = coef_re[..., None] * b_im + coef_im[..., None] * b_re
    return abar_re, abar_im, bbar_re, bbar_im


def _complex_powers(ar, ai, count):
    def combine(e1, e2):
        return e2[0] * e1[0] - e2[1] * e1[1], e2[0] * e1[1] + e2[1] * e1[0]

    shape = (count,) + ar.shape
    return lax.associative_scan(combine, (jnp.broadcast_to(ar, shape), jnp.broadcast_to(ai, shape)), axis=0)


_EYE16 = functools.partial(jnp.eye, 16, dtype=F32)


def _s5_params(p, l):
    disc, disc_vjp = jax.vjp(_s5_discretize, p['s5_a_re'][l], p['s5_a_im'][l], p['s5_log_dt'][l], p['s5_b_re'][l], p['s5_b_im'][l])
    abar_re, abar_im, bbar_re, bbar_im = disc
    ar, ai = abar_re.reshape(N_STATE), abar_im.reshape(N_STATE)
    ap_re, ap_im = _complex_powers(ar, ai, TILE)
    one, zero = jnp.ones((1, N_STATE), F32), jnp.zeros((1, N_STATE), F32)
    seg_re = jnp.concatenate([one, ap_re[SEG - 1:TILE - SEG:SEG]], axis=0)
    seg_im = jnp.concatenate([zero, ap_im[SEG - 1:TILE - SEG:SEG]], axis=0)
    doubling = [SEG - 1, 2 * SEG - 1, 4 * SEG - 1]

    src = (jnp.arange(TILE) % N_SEG) * SEG + jnp.arange(TILE) // N_SEG
    perm = (src[:, None] == jnp.arange(TILE)[None, :]).astype(MXU)
    prm = {
        'perm': perm, 'perm_t': perm.T,
        'pw_re': ap_re[:SEG], 'pw_im': ap_im[:SEG],
        'dbl_re': jnp.concatenate([jnp.stack([ap_re[k] for k in doubling]), jnp.zeros((5, N_STATE), F32)], axis=0),
        'dbl_im': jnp.concatenate([jnp.stack([ap_im[k] for k in doubling]), jnp.zeros((5, N_STATE), F32)], axis=0),
        'seg_re': seg_re, 'seg_im': seg_im, 'segr_re': seg_re[::-1], 'segr_im': seg_im[::-1],
        'b_re': _mx(jnp.einsum('gpc,gh->gchp', bbar_re, _EYE16()).reshape(D_S5, N_STATE)),
        'b_im': _mx(jnp.einsum('gpc,gh->gchp', bbar_im, _EYE16()).reshape(D_S5, N_STATE)),
        'c_re': _mx(jnp.einsum('gcp,gh->gphc', p['s5_c_re'][l], _EYE16()).reshape(N_STATE, D_S5)),
        'c_im': _mx(jnp.einsum('gcp,gh->gphc', p['s5_c_im'][l], _EYE16()).reshape(N_STATE, D_S5)),
        'd': p['s5_d'][l][None, :], 'glu_w': p['s5_glu_w'][l], 'glu_b': p['s5_glu_b'][l][None, :],
    }
    return prm, disc_vjp


def _lru_params(p, l):
    eye4 = jnp.eye(4, dtype=F32)
    return {
        'conv_w': p['lru_conv_w'][l], 'conv_b': p['lru_conv_b'][l][None, :],
        'wx': _mx(jnp.einsum('hij,hk->hikj', p['lru_wx'][l], eye4).reshape(D_LRU, D_LRU)),
        'wa': _mx(jnp.einsum('hij,hk->hikj', p['lru_wa'][l], eye4).reshape(D_LRU, D_LRU)),
        'bx': p['lru_bx'][l][None, :], 'ba': p['lru_ba'][l][None, :],
        'sp': jax.nn.softplus(-p['lru_a_param'][l])[None, :],
    }


def _rope_tables(n):
    inv_freq = ROPE_THETA ** (-jnp.arange(0, 64, 2, dtype=F32) / 64)
    ang = jnp.arange(n, dtype=F32)[:, None] * inv_freq[None, :]
    cos, sin = jnp.cos(ang), jnp.sin(ang)
    return jnp.concatenate([cos, cos, cos, cos], axis=1), jnp.concatenate([-sin, sin, -sin, sin], axis=1)


def _sink_cols(sinks):
    nb = TILE_Q // ATTN_BLOCK
    per_unit = sinks.reshape(4, 2).T
    return jnp.broadcast_to(per_unit[:, None, :, None], (2, nb, 4, ATTN_BLOCK)).reshape(2, nb * 4 * ATTN_BLOCK, 1)


def _local_step(x, target, p):
    n = x.shape[0]
    cos_t, sin_t = _rope_tables(n)
    row = lambda a: a[None, :]
    saved = []
    h = x
    for l in range(DEPTH):
        s = {'x0': h}
        s['w_in'] = p['w_in'][l]
        s['q'], s['k'], s['v'], s['u'], s['xr'], s['gate'] = _inproj_fwd(h, s['w_in'], row(p['b_in'][l]), cos_t, sin_t)
        s['sinks'] = _sink_cols(p['attn_sinks'][l])
        s['ya'] = _attn_fwd(s['q'], s['k'], s['v'], s['sinks'])
        s['s5'], s['s5_vjp'] = _s5_params(p, l)
        s['ys'], s['s5_cr'], s['s5_ci'] = _s5_fwd(s['u'], s['s5'])
        s['lru'] = _lru_params(p, l)
        s['yl'], s['lru_c'] = _lru_fwd(s['xr'], s['gate'], s['lru'])
        s['mix'], s['r1'], s['x1'] = _mixout_fwd(s['ya'], s['ys'], s['yl'], h, row(p['mix_norm_g'][l]), p['w_out'][l],
                                                 row(p['b_out'][l]), row(p['ln1_g'][l]), row(p['ln1_b'][l]))
        s['gp'], s['up'], s['r2'], h = _ffn_fwd(s['x1'], p['ffn_w_gate'][l], p['ffn_w_up'][l], p['ffn_conv_w'][l],
                                                row(p['ffn_conv_b'][l]), p['ffn_w_down'][l], row(p['ln2_g'][l]), row(p['ln2_b'][l]))
        saved.append(s)
    loss, dh = _loss_head(h, target)

    grads = {name: [None] * DEPTH for name in WEIGHTS}
    for l in reversed(range(DEPTH)):
        s = saved[l]
        g = {}
        (dr2, dgp, dup, g['ffn_w_down'], cw0, cw1, cw2, dcb, dg2, db2) = _ffn_bwd_down(
            dh, s['r2'], row(p['ln2_g'][l]), s['gp'], s['up'], p['ffn_conv_w'][l], row(p['ffn_conv_b'][l]), p['ffn_w_down_t'][l])
        g['ffn_conv_w'] = jnp.concatenate([cw0, cw1, cw2], axis=0)
        g['ffn_conv_b'], g['ln2_g'], g['ln2_b'] = dcb[0], dg2[0], db2[0]
        dr1, dg1, db1 = _ffn_bwd_dx(dr2, dgp, dup, s['r1'], row(p['ln1_g'][l]), p['ffn_w_gate_t'][l], p['ffn_w_up_t'][l])
        g['ffn_w_gate'], g['ffn_w_up'] = _ffn_bwd_dw(s['x1'], dgp, dup)
        g['ln1_g'], g['ln1_b'] = dg1[0], db1[0]
        dya, dys, dyl, g['w_out'], dbo, dgm = _mixout_bwd(dr1, s['mix'], s['ya'], s['ys'], s['yl'], row(p['mix_norm_g'][l]),
                                                         p['w_out_t'][l])
        g['b_out'], g['mix_norm_g'] = dbo[0], dgm[0]

        dxr, dgate, lw0, lw1, lw2, lw3, lcb, dwx, dwa, dbx, dba, dsp = _lru_bwd(s['xr'], s['gate'], dyl, s['lru_c'], s['lru'])
        g['lru_conv_w'] = jnp.concatenate([lw0, lw1, lw2, lw3], axis=0)
        g['lru_conv_b'], g['lru_bx'], g['lru_ba'] = lcb[0], dbx[0], dba[0]
        g['lru_wx'] = jnp.einsum('hihj->hij', dwx.reshape(4, 64, 4, 64))
        g['lru_wa'] = jnp.einsum('hihj->hij', dwa.reshape(4, 64, 4, 64))
        g['lru_a_param'] = -dsp[0] * jax.nn.sigmoid(-p['lru_a_param'][l])

        du, dbr, dbi, dcr, dci, dar, dai, dd, g['s5_glu_w'], dgb = _s5_bwd(s['u'], dys, s['s5_cr'], s['s5_ci'], s['s5'])
        g['s5_c_re'] = jnp.einsum('gpgc->gcp', dcr.reshape(16, 64, 16, 16))
        g['s5_c_im'] = jnp.einsum('gpgc->gcp', dci.reshape(16, 64, 16, 16))
        g['s5_d'], g['s5_glu_b'] = dd[0], dgb[0]
        g['s5_a_re'], g['s5_a_im'], g['s5_log_dt'], g['s5_b_re'], g['s5_b_im'] = s['s5_vjp']((
            dar.reshape(16, 64), dai.reshape(16, 64), jnp.einsum('gcgp->gpc', dbr.reshape(16, 16, 16, 64)),
            jnp.einsum('gcgp->gpc', dbi.reshape(16, 16, 16, 64))))

        dq, dk, dv, dsink = _attn_bwd(s['q'], s['k'], s['v'], s['sinks'], s['ya'], dya)
        g['attn_sinks'] = dsink[:, 0]
        dh, g['w_in'], dbin = _inproj_bwd(dq, dk, dv, du, dxr, dgate, cos_t, sin_t, s['x0'], dr1, p['w_in_t'][l])
        g['b_in'] = dbin[0]
        for name in WEIGHTS:
            grads[name][l] = g[name]
    big = dict(BIG)
    return loss, dh, {name: grads[name] if name in big else jnp.stack(grads[name]) for name in WEIGHTS}


def kernel(x, w_in, b_in, attn_sinks, s5_a_re, s5_a_im, s5_b_re, s5_b_im, s5_c_re, s5_c_im, s5_d, s5_log_dt, s5_glu_w, s5_glu_b, lru_conv_w, lru_conv_b, lru_wx, lru_bx, lru_wa, lru_ba, lru_a_param, mix_norm_g, w_out, b_out, ln1_g, ln1_b, ffn_w_gate, ffn_w_up, ffn_conv_w, ffn_conv_b, ffn_w_down, ln2_g, ln2_b, loss_target, m_w_in, m_b_in, m_attn_sinks, m_s5_a_re, m_s5_a_im, m_s5_b_re, m_s5_b_im, m_s5_c_re, m_s5_c_im, m_s5_d, m_s5_log_dt, m_s5_glu_w, m_s5_glu_b, m_lru_conv_w, m_lru_conv_b, m_lru_wx, m_lru_bx, m_lru_wa, m_lru_ba, m_lru_a_param, m_mix_norm_g, m_w_out, m_b_out, m_ln1_g, m_ln1_b, m_ffn_w_gate, m_ffn_w_up, m_ffn_conv_w, m_ffn_conv_b, m_ffn_w_down, m_ln2_g, m_ln2_b, v_w_in, v_b_in, v_attn_sinks, v_s5_a_re, v_s5_a_im, v_s5_b_re, v_s5_b_im, v_s5_c_re, v_s5_c_im, v_s5_d, v_s5_log_dt, v_s5_glu_w, v_s5_glu_b, v_lru_conv_w, v_lru_conv_b, v_lru_wx, v_lru_bx, v_lru_wa, v_lru_ba, v_lru_a_param, v_mix_norm_g, v_w_out, v_b_out, v_ln1_g, v_ln1_b, v_ffn_w_gate, v_ffn_w_up, v_ffn_conv_w, v_ffn_conv_b, v_ffn_w_down, v_ln2_g, v_ln2_b):
    given = dict(locals())
    whole = {name: given[name] for name in WEIGHTS}
    whole.update(_gather_weights({name: given[name] for name in SHARDED}))
    loss, grad_x, grads = _local_step(x[0], loss_target[0], whole)
    total = lax.psum(loss[0, 0], ("x", "y", "c"))
    return (total, grad_x[None], *_update(given, grads))


def _update(given, grads):
    local_w = {name: given[name] for name in WEIGHTS}

    rest = SMALL_SHARDED + REPLICATED
    big_rows = sum(DEPTH * _row_form(local_w[name], t).shape[1] for name, t in BIG)
    rest_rows = sum(_packed_rows(local_w[name].shape) for name in rest)
    pad = -(big_rows + rest_rows) % PACK_TILE
    small = jnp.concatenate(
        [_pack_rows(_to_blocks(grads[name], SHARD_AXIS[name]), lead=1) for name in SMALL_SHARDED]
        + [jnp.broadcast_to(_pack_rows(grads[name]), (N_DEV, _packed_rows(grads[name].shape), LANES)) for name in REPLICATED]
        + [jnp.zeros((N_DEV, pad, LANES), F32)], axis=1)
    parts = _exchange_grads([g for name, _ in BIG for g in grads[name]], small)

    def packed(prefix):
        return jnp.concatenate(
            [_row_form(given[prefix + name], t).reshape(-1, LANES) for name, t in BIG]
            + [_pack_rows(given[prefix + name]) for name in rest] + [jnp.zeros((pad, LANES), F32)], axis=0)

    outs = _reduce_adamw(parts, packed(''), packed('m_'), packed('v_'))

    def unpack(a):
        res, lo = {}, 0
        for name, t in BIG:
            form = _row_form(local_w[name], t).shape
            res[name] = _row_form(a[lo:lo + form[0] * form[1]].reshape(form), t)
            lo += form[0] * form[1]
        for name in rest:
            shape = local_w[name].shape
            res[name] = a[lo:lo + _packed_rows(shape)].reshape(-1)[:math.prod(shape)].reshape(shape)
            lo += _packed_rows(shape)
        return [res[name] for name in WEIGHTS]

    return (*unpack(outs[0]), *unpack(outs[1]), *unpack(outs[2]), *unpack(outs[3]))
```

```python
import functools
import math

import jax
import jax.numpy as jnp
from jax import lax
from jax.experimental import pallas as pl
from jax.experimental.pallas import tpu as pltpu

F32 = jnp.float32
MXU = jnp.bfloat16

N_DEV = 8
DEPTH = 4
D = 1024
D_ATTN, D_KV, D_S5, D_LRU = 512, 128, 256, 256
D_IN = 1536
D_FF = 2816
FF_CHUNK = 256
N_STATE = 1024
LANES = 1024
ALPHA = (2 * DEPTH) ** 0.25
LN_EPS = 1e-5
RMS_EPS = 1e-6
LRU_C = 8.0
ROPE_THETA = 10000.0
ADAM_LR, ADAM_B1, ADAM_B2, ADAM_EPS, ADAM_WD, ADAM_STEP = 0.001, 0.9, 0.999, 1e-08, 0.01, 10

TILE = 256
N_SEG = 8
SEG = TILE // N_SEG
TILE_Q = 512
TILE_BIG = 512
ATTN_BLOCK = 128
PACK_TILE = 256
VMEM_MB = 56

WEIGHTS = ['w_in', 'b_in', 'attn_sinks', 's5_a_re', 's5_a_im', 's5_b_re', 's5_b_im', 's5_c_re', 's5_c_im', 's5_d', 's5_log_dt',
           's5_glu_w', 's5_glu_b', 'lru_conv_w', 'lru_conv_b', 'lru_wx', 'lru_bx', 'lru_wa', 'lru_ba', 'lru_a_param', 'mix_norm_g',
           'w_out', 'b_out', 'ln1_g', 'ln1_b', 'ffn_w_gate', 'ffn_w_up', 'ffn_conv_w', 'ffn_conv_b', 'ffn_w_down', 'ln2_g', 'ln2_b']
SHARD_AXIS = {'w_in': 2, 's5_glu_w': 1, 'lru_conv_w': 2, 'w_out': 1, 'ffn_w_gate': 2, 'ffn_w_up': 2, 'ffn_conv_w': 2,
              'ffn_w_down': 1}
SHARDED = [n for n in WEIGHTS if n in SHARD_AXIS]
REPLICATED = [n for n in WEIGHTS if n not in SHARD_AXIS]
BIG = [(n, SHARD_AXIS[n] == 2) for n in ('w_in', 'w_out', 'ffn_w_gate', 'ffn_w_up', 'ffn_w_down')]
SMALL_SHARDED = [n for n in SHARDED if n not in dict(BIG)]
GATHER_F32 = ('lru_conv_w', 'ffn_conv_w')


def _dot(a, b):
    return jnp.dot(a, b, preferred_element_type=F32)


def _dot_nt(a, b):
    return lax.dot_general(a, b, (((1,), (1,)), ((), ())), preferred_element_type=F32)


def _dot_tn(a, b):
    return lax.dot_general(a, b, (((0,), (0,)), ((), ())), preferred_element_type=F32)


def _mx(a):
    return a.astype(MXU)


_GELU_C = math.sqrt(2.0 / math.pi)


def _gelu(x):
    th = jnp.tanh(_GELU_C * (x + 0.044715 * x * x * x))
    return 0.5 * x * (1.0 + th)


def _gelu_grad(x):
    th = jnp.tanh(_GELU_C * (x + 0.044715 * x * x * x))
    return 0.5 * (1.0 + th) + 0.5 * x * (1.0 - th * th) * _GELU_C * (1.0 + 3.0 * 0.044715 * x * x)


def _sigmoid(x):
    return 1.0 / (1.0 + jnp.exp(-x))


def _ln_stats(r):
    mu = jnp.mean(r, axis=-1, keepdims=True)
    xc = r - mu
    var = jnp.mean(xc * xc, axis=-1, keepdims=True)
    rstd = lax.rsqrt(var + LN_EPS)
    return xc * rstd, rstd


def _ln_bwd(dy, g, xhat, rstd):
    dxh = dy * g
    return rstd * (dxh - jnp.mean(dxh, axis=-1, keepdims=True) - xhat * jnp.mean(dxh * xhat, axis=-1, keepdims=True))


def _rms(y):
    return lax.rsqrt(jnp.mean(y * y, axis=-1, keepdims=True) + RMS_EPS)


def _sum0(a):
    return jnp.sum(a, axis=0, keepdims=True)


def _row_iota(shape):
    return lax.broadcasted_iota(jnp.int32, shape, 0)


def _shift_down(ext, j, rows):
    return pltpu.roll(ext, j, 0)[8:8 + rows]


def _shift_up(ext, j, rows):
    return pltpu.roll(ext, ext.shape[0] - j, 0)[:rows]


def _swap_halves(t):
    w = t.shape[1]
    lane = lax.broadcasted_iota(jnp.int32, t.shape, 1)
    return jnp.where((lane & 32) == 0, pltpu.roll(t, w - 32, 1), pltpu.roll(t, 32, 1))


def _rope(t, cos, sin_signed):
    return t * cos + _swap_halves(t) * sin_signed


def _rope_t(d, cos, sin_signed):
    return d * cos + _swap_halves(d * sin_signed)


def _cmul_add(ar, ai, xr, xi, yr, yi):
    return ar * xr - ai * xi + yr, ar * xi + ai * xr + yi


def _seg_rows(k):
    return slice(N_SEG * k, N_SEG * (k + 1))


def _permute_rows(perm, x):
    hi = _mx(x)
    rest = x - hi.astype(F32)
    mid = _mx(rest)
    lo = _mx(rest - mid.astype(F32))
    return _dot(perm, hi) + _dot(perm, mid) + _dot(perm, lo)


def _cscan(sr, si, tab, cin_r, cin_i, reverse):
    sgn = -1.0 if reverse else 1.0
    pw_re, pw_im, dbl_re, dbl_im = tab['pw_re'], tab['pw_im'], tab['dbl_re'], tab['dbl_im']
    ar, ai = pw_re[0:1, :], sgn * pw_im[0:1, :]
    shape = (N_SEG, sr.shape[1])
    hr = hi = jnp.zeros(shape, F32)
    for k in (range(SEG - 1, -1, -1) if reverse else range(SEG)):
        hr, hi = _cmul_add(ar, ai, hr, hi, sr[_seg_rows(k), :], si[_seg_rows(k), :])
        sr[_seg_rows(k), :] = hr
        si[_seg_rows(k), :] = hi
    sub = _row_iota(shape)

    def shifted(v, d):
        if reverse:
            return jnp.where(sub < N_SEG - d, pltpu.roll(v, N_SEG - d, 0), 0.0)
        return jnp.where(sub >= d, pltpu.roll(v, d, 0), 0.0)

    fr, fi = hr, hi
    for j, d in enumerate((1, 2, 4)):
        fr, fi = _cmul_add(dbl_re[j:j + 1, :], sgn * dbl_im[j:j + 1, :], shifted(fr, d), shifted(fi, d), fr, fi)
    seg_re, seg_im = (tab['segr_re'], tab['segr_im']) if reverse else (tab['seg_re'], tab['seg_im'])
    cr, ci = _cmul_add(seg_re[...], sgn * seg_im[...], cin_r, cin_i, shifted(fr, 1), shifted(fi, 1))
    nr, ni = _cmul_add(dbl_re[0:1, :], sgn * dbl_im[0:1, :], cr, ci, hr, hi)
    for k in range(SEG):
        j = SEG - 1 - k if reverse else k
        xr, xi = _cmul_add(pw_re[j:j + 1, :], sgn * pw_im[j:j + 1, :], cr, ci, sr[_seg_rows(k), :], si[_seg_rows(k), :])
        sr[_seg_rows(k), :] = xr
        si[_seg_rows(k), :] = xi
    edge = slice(0, 1) if reverse else slice(N_SEG - 1, N_SEG)
    return nr[edge], ni[edge]


def _rscan(a, b, reverse):
    rows = a.shape[0]
    row = _row_iota(a.shape)
    s = 1
    while s < rows:
        if reverse:
            keep = row < rows - s
            sa = jnp.where(keep, pltpu.roll(a, rows - s, 0), 1.0)
            sb = jnp.where(keep, pltpu.roll(b, rows - s, 0), 0.0)
        else:
            keep = row >= s
            sa = jnp.where(keep, pltpu.roll(a, s, 0), 1.0)
            sb = jnp.where(keep, pltpu.roll(b, s, 0), 0.0)
        b = b + a * sb
        a = a * sa
        s *= 2
    return a, b


def _whole():
    return pl.BlockSpec(memory_space=pltpu.VMEM)


def _rows_spec(rows, cols, n_tiles, reverse=False):
    if reverse:
        return pl.BlockSpec((rows, cols), lambda i: (n_tiles - 1 - i, 0))
    return pl.BlockSpec((rows, cols), lambda i: (i, 0))


def _halo_spec(cols, tile_rows, n_tiles, reverse=False):
    per = tile_rows // 8
    if reverse:
        return pl.BlockSpec((8, cols), lambda i: (jnp.maximum((n_tiles - 1 - i) * per - 1, 0), 0))
    return pl.BlockSpec((8, cols), lambda i: (jnp.maximum(i * per - 1, 0), 0))


def _call(body, name, n_tiles, in_specs, out_specs, out_shape, scratch=()):
    return pl.pallas_call(
        body, name=name, grid=(n_tiles,), in_specs=in_specs, out_specs=out_specs, out_shape=out_shape,
        scratch_shapes=list(scratch),
        compiler_params=pltpu.CompilerParams(dimension_semantics=("arbitrary",), vmem_limit_bytes=VMEM_MB << 20))


def _sds(shape, dtype=F32):
    return jax.ShapeDtypeStruct(shape, dtype)


def _inproj_fwd(x, w, b, cos_t, sin_t):
    n = x.shape[0]
    nt = n // TILE

    def body(x_ref, w_ref, b_ref, c_ref, s_ref, q_ref, k_ref, v_ref, u_ref, xr_ref, g_ref):
        p = _dot(_mx(x_ref[...]), w_ref[...]) + b_ref[...]
        cos, sin = c_ref[...], s_ref[...]
        q_ref[...] = _mx(_rope(p[:, :D_ATTN], jnp.tile(cos, (1, 4)), jnp.tile(sin, (1, 4))))
        k_ref[...] = _mx(_rope(p[:, 512:640], cos, sin))
        v_ref[...] = _mx(p[:, 640:768])
        u_ref[...] = p[:, 768:1024]
        xr_ref[...] = p[:, 1024:1280]
        g_ref[...] = p[:, 1280:1536]

    r = functools.partial(_rows_spec, n_tiles=nt)
    return _call(
        body, "inproj_fwd", nt,
        [r(TILE, D), _whole(), _whole(), r(TILE, 128), r(TILE, 128)],
        [r(TILE, D_ATTN), r(TILE, D_KV), r(TILE, D_KV), r(TILE, D_S5), r(TILE, D_LRU), r(TILE, D_LRU)],
        [_sds((n, D_ATTN), MXU), _sds((n, D_KV), MXU), _sds((n, D_KV), MXU), _sds((n, D_S5)), _sds((n, D_LRU)), _sds((n, D_LRU))],
    )(x, w, b, cos_t, sin_t)


def _inproj_bwd(dq, dk, dv, du, dxr, dgate, cos_t, sin_t, x0, dr1, w_t):
    n = x0.shape[0]
    nt = n // TILE

    def body(dq_ref, dk_ref, dv_ref, du_ref, dxr_ref, dg_ref, c_ref, s_ref, x_ref, dr_ref, w_ref, dx_ref, dw_ref, db_ref):
        @pl.when(pl.program_id(0) == 0)
        def _():
            dw_ref[...] = jnp.zeros_like(dw_ref)
            db_ref[...] = jnp.zeros_like(db_ref)

        cos, sin = c_ref[...], s_ref[...]
        dtq = _rope_t(dq_ref[...], jnp.tile(cos, (1, 4)), jnp.tile(sin, (1, 4)))
        dtk = _rope_t(dk_ref[...], cos, sin)
        dp = jnp.concatenate([dtq, dtk, dv_ref[...], du_ref[...], dxr_ref[...], dg_ref[...]], axis=1)
        db_ref[...] += _sum0(dp)
        dpb = _mx(dp)
        dw_ref[...] += _dot_tn(dpb, _mx(x_ref[...]))
        dx_ref[...] = ALPHA * dr_ref[...] + _dot(dpb, w_ref[...])

    r = functools.partial(_rows_spec, n_tiles=nt)
    return _call(
        body, "inproj_bwd", nt,
        [r(TILE, D_ATTN), r(TILE, D_KV), r(TILE, D_KV), r(TILE, D_S5), r(TILE, D_LRU), r(TILE, D_LRU), r(TILE, 128), r(TILE, 128),
         r(TILE, D), r(TILE, D), _whole()],
        [r(TILE, D), _whole(), _whole()],
        [_sds((n, D)), _sds((D_IN, D)), _sds((1, D_IN))],
    )(dq, dk, dv, du, dxr, dgate, cos_t, sin_t, x0, dr1, w_t)


def _kv_variants(t, lo):
    tr = pltpu.roll(t, 64, 1)
    out = []
    for j in range(2):
        first = jnp.where(lo, t if j == 0 else tr, 0.0)
        second = jnp.where(lo, 0.0, tr if j == 0 else t)
        out.append(_mx(jnp.concatenate([first, second], axis=0)))
    return out


def _kv_collect(x0, x1, lo):
    a = x0[:256] + pltpu.roll(x0[256:], 64, 1)
    b = pltpu.roll(x1[:256], 64, 1) + x1[256:]
    return jnp.where(lo, a, b)


def _attn_probs(s, sink_ref):
    out = []
    for hp in range(2):
        sh = s[:, hp * 256:(hp + 1) * 256]
        sink = sink_ref[hp]
        m = jnp.maximum(jnp.max(sh, axis=1, keepdims=True), sink)
        p = jnp.exp(sh - m)
        es = jnp.exp(sink - m)
        inv = 1.0 / (jnp.sum(p, axis=1, keepdims=True) + es)
        out.append((p * inv, es * inv))
    return out


def _attn_scores(q_ref, k_ref, v_ref, nb):
    lo = lax.broadcasted_iota(jnp.int32, (256, 128), 1) < 64
    kcats, vcats, kstarts, parts = [], [], [], []
    for b in range(nb):
        block = pl.program_id(0) * nb + b
        kstart = pl.multiple_of(jnp.maximum(block - 1, 0) * ATTN_BLOCK, ATTN_BLOCK)
        kcat = _kv_variants(k_ref[pl.ds(kstart, 256), :].astype(F32), lo)
        kcats.append(kcat)
        vcats.append(_kv_variants(v_ref[pl.ds(kstart, 256), :].astype(F32), lo))
        kstarts.append(kstart)
        valid = _attn_mask(block, kstart)
        for i in range(4):
            s = _dot_nt(q_ref[b * ATTN_BLOCK:(b + 1) * ATTN_BLOCK, i * 128:(i + 1) * 128], kcat[i // 2]) * 0.125
            parts.append(jnp.where(valid, s, -jnp.inf))
    return jnp.concatenate(parts, axis=0), kcats, vcats, kstarts


def _attn_mask(block, kstart):
    col = lax.broadcasted_iota(jnp.int32, (ATTN_BLOCK, 512), 1)
    row = lax.broadcasted_iota(jnp.int32, (ATTN_BLOCK, 512), 0)
    diff = (block * ATTN_BLOCK + row) - (kstart + (col & 255))
    return (diff >= 0) & (diff < ATTN_BLOCK)


def _attn_fwd(q, k, v, sink_cols):
    n = q.shape[0]
    nt = n // TILE_Q
    nb = TILE_Q // ATTN_BLOCK

    def body(q_ref, k_ref, v_ref, s_ref, o_ref):
        s, _, vcats, _ = _attn_scores(q_ref, k_ref, v_ref, nb)
        (p0, _), (p1, _) = _attn_probs(s, s_ref)
        pb = _mx(jnp.concatenate([p0, p1], axis=1))
        for b in range(nb):
            for i in range(4):
                unit = (b * 4 + i) * ATTN_BLOCK
                o_ref[b * ATTN_BLOCK:(b + 1) * ATTN_BLOCK, i * 128:(i + 1) * 128] = _dot(pb[unit:unit + ATTN_BLOCK], vcats[b][i // 2])

    return _call(
        body, "attn_fwd", nt,
        [_rows_spec(TILE_Q, D_ATTN, nt), _whole(), _whole(), _whole()],
        _rows_spec(TILE_Q, D_ATTN, nt), _sds((n, D_ATTN)),
    )(q, k, v, sink_cols)


def _attn_bwd(q, k, v, sink_cols, o, do):
    n = q.shape[0]
    nt = n // TILE_Q
    nb = TILE_Q // ATTN_BLOCK

    def body(q_ref, k_ref, v_ref, s_ref, o_ref, do_ref, dq_ref, dk_ref, dv_ref, ds_ref):
        @pl.when(pl.program_id(0) == 0)
        def _():
            dk_ref[...] = jnp.zeros_like(dk_ref)
            dv_ref[...] = jnp.zeros_like(dv_ref)
            ds_ref[...] = jnp.zeros_like(ds_ref)

        lo = lax.broadcasted_iota(jnp.int32, (256, 128), 1) < 64
        s, kcats, vcats, kstarts = _attn_scores(q_ref, k_ref, v_ref, nb)
        probs = _attn_probs(s, s_ref)
        do = do_ref[...]
        dob = _mx(do)
        od = do * o_ref[...]
        lo_q = (lax.broadcasted_iota(jnp.int32, od.shape, 1) & 64) == 0
        od_head = (jnp.where(lo_q, od, 0.0), jnp.where(lo_q, 0.0, od))
        units = [(b, i) for b in range(nb) for i in range(4)]

        def tile_part(a, b, i):
            return a[b * ATTN_BLOCK:(b + 1) * ATTN_BLOCK, i * 128:(i + 1) * 128]

        dp = jnp.concatenate([_dot_nt(tile_part(dob, b, i), vcats[b][i // 2]) for b, i in units], axis=0)
        ds = []
        for hp in range(2):
            p, p_sink = probs[hp]
            delta = jnp.concatenate([jnp.sum(tile_part(od_head[hp], b, i), axis=1, keepdims=True) for b, i in units], axis=0)
            ds.append(p * (dp[:, hp * 256:(hp + 1) * 256] - delta) * 0.125)
            t = p_sink * delta
            for i in range(4):
                dsink = sum(_sum0(t[(b * 4 + i) * ATTN_BLOCK:(b * 4 + i + 1) * ATTN_BLOCK]) for b in range(nb))
                ds_ref[2 * i + hp:2 * i + hp + 1, :] -= jnp.broadcast_to(dsink, (1, 128))
        dsb = _mx(jnp.concatenate(ds, axis=1))
        pb = _mx(jnp.concatenate([probs[0][0], probs[1][0]], axis=1))
        for b in range(nb):
            dkc = [jnp.zeros((512, 128), F32), jnp.zeros((512, 128), F32)]
            dvc = [jnp.zeros((512, 128), F32), jnp.zeros((512, 128), F32)]
            for i in range(4):
                j = i // 2
                unit = slice((b * 4 + i) * ATTN_BLOCK, (b * 4 + i + 1) * ATTN_BLOCK)
                dq_ref[b * ATTN_BLOCK:(b + 1) * ATTN_BLOCK, i * 128:(i + 1) * 128] = _dot(dsb[unit], kcats[b][j])
                dkc[j] = dkc[j] + _dot_tn(dsb[unit], tile_part(q_ref, b, i))
                dvc[j] = dvc[j] + _dot_tn(pb[unit], tile_part(dob, b, i))
            dk_ref[pl.ds(kstarts[b], 256), :] += _kv_collect(dkc[0], dkc[1], lo)
            dv_ref[pl.ds(kstarts[b], 256), :] += _kv_collect(dvc[0], dvc[1], lo)

    r = _rows_spec(TILE_Q, D_ATTN, nt)
    return _call(
        body, "attn_bwd", nt,
        [r, _whole(), _whole(), _whole(), r, r],
        [r, _whole(), _whole(), _whole()],
        [_sds((n, D_ATTN)), _sds((n, D_KV)), _sds((n, D_KV)), _sds((8, 128))],
    )(q, k, v, sink_cols, o, do)


S5_TABLES = ('pw_re', 'pw_im', 'dbl_re', 'dbl_im', 'seg_re', 'seg_im', 'segr_re', 'segr_im')
S5_WEIGHTS = ('b_re', 'b_im', 'c_re', 'c_im', 'd', 'glu_w', 'glu_b', 'perm', 'perm_t')


def _s5_states(u, carry_r, carry_i, b_re, b_im, tab, hr_s, hi_s):
    ub = _mx(u)
    hr_s[...] = _dot(ub, b_re[...])
    hi_s[...] = _dot(ub, b_im[...])
    return ub, _cscan(hr_s, hi_s, tab, carry_r, carry_i, reverse=False)


def _s5_fwd(u, prm):
    n = u.shape[0]
    nt = n // TILE

    def body(u_ref, *refs):
        tab = dict(zip(S5_TABLES, refs[:8]))
        b_re, b_im, c_re, c_im, d_ref, gw_ref, gb_ref, perm, perm_t = refs[8:17]
        y_ref, cr_out, ci_out, cr_s, ci_s, hr_s, hi_s = refs[17:]

        @pl.when(pl.program_id(0) == 0)
        def _():
            cr_s[...] = jnp.zeros_like(cr_s)
            ci_s[...] = jnp.zeros_like(ci_s)

        u = _permute_rows(perm[...], u_ref[...])
        cr, ci = cr_s[...], ci_s[...]
        cr_out[...] = jnp.broadcast_to(cr, (8, N_STATE))
        ci_out[...] = jnp.broadcast_to(ci, (8, N_STATE))
        _, (cr_s[...], ci_s[...]) = _s5_states(u, cr, ci, b_re, b_im, tab, hr_s, hi_s)
        y = _dot(_mx(hr_s[...]), c_re[...]) - _dot(_mx(hi_s[...]), c_im[...]) + d_ref[...] * u
        z = _gelu(y)
        y_ref[...] = _permute_rows(perm_t[...], z * _sigmoid(_dot(_mx(z), gw_ref[...]) + gb_ref[...]))

    r = functools.partial(_rows_spec, n_tiles=nt)
    return _call(
        body, "s5_fwd", nt,
        [r(TILE, D_S5)] + [_whole()] * 17,
        [r(TILE, D_S5), r(8, N_STATE), r(8, N_STATE)],
        [_sds((n, D_S5)), _sds((nt * 8, N_STATE)), _sds((nt * 8, N_STATE))],
        scratch=[pltpu.VMEM((1, N_STATE), F32)] * 2 + [pltpu.VMEM((TILE, N_STATE), F32)] * 2,
    )(u, *[prm[k] for k in S5_TABLES + S5_WEIGHTS])


def _s5_bwd(u, dys, carry_re, carry_im, prm):
    n = u.shape[0]
    nt = n // TILE

    def body(u_ref, dy_ref, cin_r, cin_i, *refs):
        tab = dict(zip(S5_TABLES, refs[:8]))
        b_re, b_im, c_re, c_im, d_ref, gw_ref, gb_ref, perm, perm_t = refs[8:17]
        du_ref, dbr_ref, dbi_ref, dcr_ref, dci_ref, dar_ref, dai_ref, dd_ref, dgw_ref, dgb_ref = refs[17:27]
        gr_s, gi_s, hr_s, hi_s, gr_t, gi_t = refs[27:]

        @pl.when(pl.program_id(0) == 0)
        def _():
            for ref in (dbr_ref, dbi_ref, dcr_ref, dci_ref, dar_ref, dai_ref, dd_ref, dgw_ref, dgb_ref, gr_s, gi_s):
                ref[...] = jnp.zeros_like(ref)

        u = _permute_rows(perm[...], u_ref[...])
        cr, ci = cin_r[0:1, :], cin_i[0:1, :]
        ub, _ = _s5_states(u, cr, ci, b_re, b_im, tab, hr_s, hi_s)
        hrb, hib = _mx(hr_s[...]), _mx(hi_s[...])
        y = _dot(hrb, c_re[...]) - _dot(hib, c_im[...]) + d_ref[...] * u
        z = _gelu(y)
        zb = _mx(z)
        sg = _sigmoid(_dot(zb, gw_ref[...]) + gb_ref[...])
        dout = _permute_rows(perm[...], dy_ref[...])
        dpre = dout * z * sg * (1.0 - sg)
        dgb_ref[...] += _sum0(dpre)
        dpb = _mx(dpre)
        dgw_ref[...] += _dot_tn(zb, dpb)
        dy = (dout * sg + _dot_nt(dpb, gw_ref[...])) * _gelu_grad(y)
        dd_ref[...] += _sum0(dy * u)
        dyb = _mx(dy)
        dcr_ref[...] += _dot_tn(hrb, dyb)
        dci_ref[...] -= _dot_tn(hib, dyb)
        gr_t[...] = _dot_nt(dyb, c_re[...])
        gi_t[...] = -_dot_nt(dyb, c_im[...])
        gr_s[...], gi_s[...] = _cscan(gr_t, gi_t, tab, gr_s[...], gi_s[...], reverse=True)
        sub = _row_iota((N_SEG, N_STATE))
        acc_r = acc_i = jnp.zeros((N_SEG, N_STATE), F32)
        for k in range(SEG):
            if k == 0:
                hpr = jnp.where(sub >= 1, pltpu.roll(hr_s[_seg_rows(SEG - 1), :], 1, 0), cr)
                hpi = jnp.where(sub >= 1, pltpu.roll(hi_s[_seg_rows(SEG - 1), :], 1, 0), ci)
            else:
                hpr, hpi = hr_s[_seg_rows(k - 1), :], hi_s[_seg_rows(k - 1), :]
            gr, gi = gr_t[_seg_rows(k), :], gi_t[_seg_rows(k), :]
            acc_r = acc_r + gr * hpr + gi * hpi
            acc_i = acc_i + gi * hpr - gr * hpi
        dar_ref[...] += _sum0(acc_r)
        dai_ref[...] += _sum0(acc_i)
        grb, gib = _mx(gr_t[...]), _mx(gi_t[...])
        dbr_ref[...] += _dot_tn(ub, grb)
        dbi_ref[...] += _dot_tn(ub, gib)
        du_ref[...] = _permute_rows(perm_t[...], dy * d_ref[...] + _dot_nt(grb, b_re[...]) + _dot_nt(gib, b_im[...]))

    r = functools.partial(_rows_spec, n_tiles=nt, reverse=True)
    return _call(
        body, "s5_bwd", nt,
        [r(TILE, D_S5), r(TILE, D_S5), r(8, N_STATE), r(8, N_STATE)] + [_whole()] * 17,
        [r(TILE, D_S5)] + [_whole()] * 9,
        [_sds((n, D_S5)), _sds((D_S5, N_STATE)), _sds((D_S5, N_STATE)), _sds((N_STATE, D_S5)), _sds((N_STATE, D_S5)),
         _sds((1, N_STATE)), _sds((1, N_STATE)), _sds((1, D_S5)), _sds((D_S5, D_S5)), _sds((1, D_S5))],
        scratch=[pltpu.VMEM((1, N_STATE), F32)] * 2 + [pltpu.VMEM((TILE, N_STATE), F32)] * 4,
    )(u, dys, carry_re, carry_im, *[prm[k] for k in S5_TABLES + S5_WEIGHTS])


def _lru_gates(xr, halo, tile_index, cw_ref, cb_ref, wx_ref, wa_ref, bx_ref, ba_ref, sp_ref):
    ext = jnp.concatenate([halo, xr], axis=0)
    sh = [xr] + [_shift_down(ext, j, TILE) for j in (1, 2, 3)]
    xc = cb_ref[...] + cw_ref[3:4, :] * sh[0] + cw_ref[2:3, :] * sh[1] + cw_ref[1:2, :] * sh[2] + cw_ref[0:1, :] * sh[3]
    xb = _mx(xc)
    gx = _sigmoid(_dot(xb, wx_ref[...]) + bx_ref[...])
    ga = _sigmoid(_dot(xb, wa_ref[...]) + ba_ref[...])
    la = -LRU_C * ga * sp_ref[...]
    a = jnp.exp(la)
    start = (tile_index * TILE + _row_iota(xr.shape)) == 0
    mult = jnp.where(start, 1.0, jnp.sqrt(-jnp.tanh(la) * (a * a + 1.0)))
    return sh, xc, xb, gx, ga, a, mult, start


def _lru_fwd(xr, gate, prm):
    n = xr.shape[0]
    nt = n // TILE

    def body(x_ref, g_ref, cw_ref, cb_ref, wx_ref, wa_ref, bx_ref, ba_ref, sp_ref, y_ref, c_out, halo_s, c_s):
        first_tile = pl.program_id(0) == 0

        @pl.when(first_tile)
        def _():
            halo_s[...] = jnp.zeros_like(halo_s)
            c_s[...] = jnp.zeros_like(c_s)

        xr = x_ref[...]
        _, xc, _, gx, _, a, mult, _ = _lru_gates(xr, halo_s[...], pl.program_id(0), cw_ref, cb_ref, wx_ref, wa_ref, bx_ref, ba_ref,
                                                 sp_ref)
        halo_s[...] = xr[TILE - 8:]
        acum, h = _rscan(a, mult * gx * xc, reverse=False)
        c = c_s[...]
        c_out[...] = jnp.broadcast_to(c, (8, D_LRU))
        h = h + acum * c
        c_s[...] = h[TILE - 1:TILE]
        y_ref[...] = h * _gelu(g_ref[...])

    r = functools.partial(_rows_spec, n_tiles=nt)
    return _call(
        body, "lru_fwd", nt,
        [r(TILE, D_LRU), r(TILE, D_LRU)] + [_whole()] * 7,
        [r(TILE, D_LRU), r(8, D_LRU)],
        [_sds((n, D_LRU)), _sds((nt * 8, D_LRU))],
        scratch=[pltpu.VMEM((8, D_LRU), F32), pltpu.VMEM((1, D_LRU), F32)],
    )(xr, gate, prm['conv_w'], prm['conv_b'], prm['wx'], prm['wa'], prm['bx'], prm['ba'], prm['sp'])


def _lru_bwd(xr, gate, dyl, carry, prm):
    n = xr.shape[0]
    nt = n // TILE

    def body(x_ref, xh_ref, g_ref, dy_ref, cin_ref, cw_ref, cb_ref, wx_ref, wa_ref, bx_ref, ba_ref, sp_ref,
             dx_ref, dg_ref, dcw0, dcw1, dcw2, dcw3, dcb_ref, dwx_ref, dwa_ref, dbx_ref, dba_ref, dsp_ref, an_s, gn_s, dn_s):
        first_tile = pl.program_id(0) == nt - 1

        @pl.when(pl.program_id(0) == 0)
        def _():
            for ref in (dcw0, dcw1, dcw2, dcw3, dcb_ref, dwx_ref, dwa_ref, dbx_ref, dba_ref, dsp_ref, gn_s, dn_s):
                ref[...] = jnp.zeros_like(ref)
            an_s[...] = jnp.ones_like(an_s)

        xr = x_ref[...]
        halo = jnp.where(first_tile, 0.0, xh_ref[...])
        sh, xc, xb, gx, ga, a, mult, start = _lru_gates(xr, halo, nt - 1 - pl.program_id(0), cw_ref, cb_ref, wx_ref, wa_ref, bx_ref,
                                                        ba_ref, sp_ref)
        acum, h = _rscan(a, mult * gx * xc, reverse=False)
        cin = cin_ref[0:1, :]
        h = h + acum * cin
        gate = g_ref[...]
        dyl = dy_ref[...]
        dg_ref[...] = dyl * h * _gelu_grad(gate)
        row = _row_iota(xr.shape)
        alpha = jnp.where(row < TILE - 1, pltpu.roll(a, TILE - 1, 0), an_s[...])
        racc, g = _rscan(alpha, dyl * _gelu(gate), reverse=True)
        g = g + racc * gn_s[...]
        an_s[...] = a[0:1]
        gn_s[...] = g[0:1]
        hprev = jnp.where(row == 0, cin, pltpu.roll(h, 1, 0))
        da = g * hprev
        dmult = jnp.where(start, 0.0, g * gx * xc)
        dla = da * a - dmult * a * a / mult
        dsp_ref[...] += _sum0(-LRU_C * ga * dla)
        dpa = (-LRU_C * sp_ref[...] * dla) * ga * (1.0 - ga)
        dpx = (g * mult * xc) * gx * (1.0 - gx)
        dba_ref[...] += _sum0(dpa)
        dbx_ref[...] += _sum0(dpx)
        dpab, dpxb = _mx(dpa), _mx(dpx)
        dwa_ref[...] += _dot_tn(xb, dpab)
        dwx_ref[...] += _dot_tn(xb, dpxb)
        dxc = g * mult * gx + _dot_nt(dpab, wa_ref[...]) + _dot_nt(dpxb, wx_ref[...])
        dcb_ref[...] += _sum0(dxc)
        dcw3[...] += _sum0(dxc * sh[0])
        dcw2[...] += _sum0(dxc * sh[1])
        dcw1[...] += _sum0(dxc * sh[2])
        dcw0[...] += _sum0(dxc * sh[3])
        ext = jnp.concatenate([dxc, dn_s[...]], axis=0)
        dx_ref[...] = (cw_ref[3:4, :] * dxc + cw_ref[2:3, :] * _shift_up(ext, 1, TILE) + cw_ref[1:2, :] * _shift_up(ext, 2, TILE)
                       + cw_ref[0:1, :] * _shift_up(ext, 3, TILE))
        dn_s[...] = dxc[:8]

    r = functools.partial(_rows_spec, n_tiles=nt, reverse=True)
    vec = _sds((1, D_LRU))
    return _call(
        body, "lru_bwd", nt,
        [r(TILE, D_LRU), _halo_spec(D_LRU, TILE, nt, reverse=True), r(TILE, D_LRU), r(TILE, D_LRU), r(8, D_LRU)] + [_whole()] * 7,
        [r(TILE, D_LRU), r(TILE, D_LRU)] + [_whole()] * 10,
        [_sds((n, D_LRU)), _sds((n, D_LRU)), vec, vec, vec, vec, vec, _sds((D_LRU, D_LRU)), _sds((D_LRU, D_LRU)), vec, vec, vec],
        scratch=[pltpu.VMEM((1, D_LRU), F32), pltpu.VMEM((1, D_LRU), F32), pltpu.VMEM((8, D_LRU), F32)],
    )(xr, xr, gate, dyl, carry, prm['conv_w'], prm['conv_b'], prm['wx'], prm['wa'], prm['bx'], prm['ba'], prm['sp'])


def _normed_parts(ya, ys, yl):
    return jnp.concatenate([ya * _rms(ya), ys * _rms(ys), yl * _rms(yl)], axis=1)


def _mixout_fwd(ya, ys, yl, x0, g_mix, w_out, b_out, g1, b1):
    n = x0.shape[0]
    nt = n // TILE

    def body(ya_ref, ys_ref, yl_ref, x_ref, gm_ref, w_ref, b_ref, g_ref, be_ref, mix_ref, r_ref, x1_ref):
        mixb = _mx(_normed_parts(ya_ref[...], ys_ref[...], yl_ref[...]) * gm_ref[...])
        mix_ref[...] = mixb
        r1 = ALPHA * x_ref[...] + _dot(mixb, w_ref[...]) + b_ref[...]
        r_ref[...] = r1
        xhat, _ = _ln_stats(r1)
        x1_ref[...] = xhat * g_ref[...] + be_ref[...]

    r = functools.partial(_rows_spec, n_tiles=nt)
    return _call(
        body, "mixout_fwd", nt,
        [r(TILE, D_ATTN), r(TILE, D_S5), r(TILE, D_LRU), r(TILE, D)] + [_whole()] * 5,
        [r(TILE, D), r(TILE, D), r(TILE, D)],
        [_sds((n, D), MXU), _sds((n, D)), _sds((n, D))],
    )(ya, ys, yl, x0, g_mix, w_out, b_out, g1, b1)


def _mixout_bwd(dr1, mix, ya, ys, yl, g_mix, w_out):
    n = dr1.shape[0]
    nt = n // TILE

    def body(dr_ref, mix_ref, ya_ref, ys_ref, yl_ref, gm_ref, w_ref, dya_ref, dys_ref, dyl_ref, dw_ref, db_ref, dgm_ref):
        @pl.when(pl.program_id(0) == 0)
        def _():
            for ref in (dw_ref, db_ref, dgm_ref):
                ref[...] = jnp.zeros_like(ref)

        dr = dr_ref[...]
        db_ref[...] += _sum0(dr)
        drb = _mx(dr)
        dw_ref[...] += _dot_tn(mix_ref[...], drb)
        dmix = _dot(drb, w_ref[...])
        parts = (ya_ref[...], ys_ref[...], yl_ref[...])
        dgm_ref[...] += _sum0(dmix * _normed_parts(*parts))
        dn = dmix * gm_ref[...]
        lo = 0
        for y, out in zip(parts, (dya_ref, dys_ref, dyl_ref)):
            w = y.shape[1]
            rs = _rms(y)
            nrm = y * rs
            dnp = dn[:, lo:lo + w]
            out[...] = rs * (dnp - nrm * jnp.mean(dnp * nrm, axis=-1, keepdims=True))
            lo += w

    r = functools.partial(_rows_spec, n_tiles=nt)
    return _call(
        body, "mixout_bwd", nt,
        [r(TILE, D), r(TILE, D), r(TILE, D_ATTN), r(TILE, D_S5), r(TILE, D_LRU), _whole(), _whole()],
        [r(TILE, D_ATTN), r(TILE, D_S5), r(TILE, D_LRU), _whole(), _whole(), _whole()],
        [_sds((n, D_ATTN)), _sds((n, D_S5)), _sds((n, D_LRU)), _sds((D, D)), _sds((1, D)), _sds((1, D))],
    )(dr1, mix, ya, ys, yl, g_mix, w_out)


def _ffn_conv(gp, halo, cw_ref, cb_ref, cs):
    ext = jnp.concatenate([halo, gp], axis=0)
    s1 = _shift_down(ext, 1, TILE)
    s2 = _shift_down(ext, 2, TILE)
    return s1, s2, cb_ref[:, cs] + cw_ref[2:3, cs] * gp + cw_ref[1:2, cs] * s1 + cw_ref[0:1, cs] * s2


def _ffn_fwd(x1, wg, wu, cw, cb, wd, g2, b2):
    n = x1.shape[0]
    nt = n // TILE

    def body(x_ref, wg_ref, wu_ref, cw_ref, cb_ref, wd_ref, g_ref, be_ref, gp_ref, up_ref, r_ref, x2_ref, halo_s):
        @pl.when(pl.program_id(0) == 0)
        def _():
            halo_s[...] = jnp.zeros_like(halo_s)

        x1 = x_ref[...]
        xb = _mx(x1)
        f = jnp.zeros((TILE, D), F32)
        for c in range(D_FF // FF_CHUNK):
            cs = slice(c * FF_CHUNK, (c + 1) * FF_CHUNK)
            gp = _dot(xb, wg_ref[:, cs])
            up = _dot(xb, wu_ref[:, cs])
            gp_ref[:, cs] = gp
            up_ref[:, cs] = up
            _, _, gc = _ffn_conv(gp, halo_s[:, cs], cw_ref, cb_ref, cs)
            halo_s[:, cs] = gp[TILE - 8:]
            f = f + _dot(_mx(gc * _sigmoid(gc) * up), wd_ref[cs, :])
        r2 = ALPHA * x1 + f
        r_ref[...] = r2
        xhat, _ = _ln_stats(r2)
        x2_ref[...] = xhat * g_ref[...] + be_ref[...]

    r = functools.partial(_rows_spec, n_tiles=nt)
    return _call(
        body, "ffn_fwd", nt,
        [r(TILE, D)] + [_whole()] * 7,
        [r(TILE, D_FF), r(TILE, D_FF), r(TILE, D), r(TILE, D)],
        [_sds((n, D_FF)), _sds((n, D_FF)), _sds((n, D)), _sds((n, D))],
        scratch=[pltpu.VMEM((8, D_FF), F32)],
    )(x1, wg, wu, cw, cb, wd, g2, b2)


def _ffn_bwd_down(dx2, r2, g2, gp, up, cw, cb, wd_t):
    n = dx2.shape[0]
    nt = n // TILE

    def body(dx_ref, r_ref, g_ref, gp_ref, gh_ref, up_ref, cw_ref, cb_ref, wd_ref,
             dr_ref, dgp_ref, dup_ref, dwd_ref, dcw0, dcw1, dcw2, dcb_ref, dg_ref, db_ref, next_s):
        first_tile = pl.program_id(0) == nt - 1

        @pl.when(pl.program_id(0) == 0)
        def _():
            for ref in (dwd_ref, dcw0, dcw1, dcw2, dcb_ref, dg_ref, db_ref, next_s):
                ref[...] = jnp.zeros_like(ref)

        dx2 = dx_ref[...]
        xhat, rstd = _ln_stats(r_ref[...])
        dg_ref[...] += _sum0(dx2 * xhat)
        db_ref[...] += _sum0(dx2)
        dr2 = _ln_bwd(dx2, g_ref[...], xhat, rstd)
        dr_ref[...] = dr2
        dfb = _mx(dr2)
        for c in range(D_FF // FF_CHUNK):
            cs = slice(c * FF_CHUNK, (c + 1) * FF_CHUNK)
            gp = gp_ref[:, cs]
            up = up_ref[:, cs]
            s1, s2, gc = _ffn_conv(gp, jnp.where(first_tile, 0.0, gh_ref[:, cs]), cw_ref, cb_ref, cs)
            sg = _sigmoid(gc)
            silu = gc * sg
            dact = _dot(dfb, wd_ref[:, cs])
            dwd_ref[cs, :] += _dot_tn(_mx(silu * up), dfb)
            dup_ref[:, cs] = _mx(dact * silu)
            dgc = dact * up * (sg * (1.0 + gc * (1.0 - sg)))
            dcb_ref[:, cs] += _sum0(dgc)
            dcw2[:, cs] += _sum0(dgc * gp)
            dcw1[:, cs] += _sum0(dgc * s1)
            dcw0[:, cs] += _sum0(dgc * s2)
            ext = jnp.concatenate([dgc, next_s[:, cs]], axis=0)
            dgp_ref[:, cs] = _mx(cw_ref[2:3, cs] * dgc + cw_ref[1:2, cs] * _shift_up(ext, 1, TILE)
                                 + cw_ref[0:1, cs] * _shift_up(ext, 2, TILE))
            next_s[:, cs] = dgc[:8]

    r = functools.partial(_rows_spec, n_tiles=nt, reverse=True)
    vff = _sds((1, D_FF))
    return _call(
        body, "ffn_bwd_down", nt,
        [r(TILE, D), r(TILE, D), _whole(), r(TILE, D_FF), _halo_spec(D_FF, TILE, nt, reverse=True), r(TILE, D_FF), _whole(), _whole(),
         _whole()],
        [r(TILE, D), r(TILE, D_FF), r(TILE, D_FF)] + [_whole()] * 7,
        [_sds((n, D)), _sds((n, D_FF), MXU), _sds((n, D_FF), MXU), _sds((D_FF, D)), vff, vff, vff, vff, _sds((1, D)), _sds((1, D))],
        scratch=[pltpu.VMEM((8, D_FF), F32)],
    )(dx2, r2, g2, gp, gp, up, cw, cb, wd_t)


def _ffn_bwd_dx(dr2, dgp, dup, r1, g1, wg_t, wu_t):
    n = dr2.shape[0]
    rows = TILE_BIG
    nt = n // rows

    def body(dr2_ref, dgp_ref, dup_ref, r_ref, g_ref, wg_ref, wu_ref, dr1_ref, dg_ref, db_ref):
        @pl.when(pl.program_id(0) == 0)
        def _():
            for ref in (dg_ref, db_ref):
                ref[...] = jnp.zeros_like(ref)

        dx1 = ALPHA * dr2_ref[...] + _dot(dgp_ref[...], wg_ref[...]) + _dot(dup_ref[...], wu_ref[...])
        xhat, rstd = _ln_stats(r_ref[...])
        dg_ref[...] += _sum0(dx1 * xhat)
        db_ref[...] += _sum0(dx1)
        dr1_ref[...] = _ln_bwd(dx1, g_ref[...], xhat, rstd)

    r = functools.partial(_rows_spec, n_tiles=nt)
    return _call(
        body, "ffn_bwd_dx", nt,
        [r(rows, D), r(rows, D_FF), r(rows, D_FF), r(rows, D), _whole(), _whole(), _whole()],
        [r(rows, D), _whole(), _whole()],
        [_sds((n, D)), _sds((1, D)), _sds((1, D))],
    )(dr2, dgp, dup, r1, g1, wg_t, wu_t)


def _ffn_bwd_dw(x1, dgp, dup):
    n = x1.shape[0]
    rows = TILE_BIG
    nt = n // rows

    def body(x_ref, dgp_ref, dup_ref, dwg_ref, dwu_ref):
        @pl.when(pl.program_id(0) == 0)
        def _():
            for ref in (dwg_ref, dwu_ref):
                ref[...] = jnp.zeros_like(ref)

        xb = _mx(x_ref[...])
        for c in range(D_FF // FF_CHUNK):
            cs = slice(c * FF_CHUNK, (c + 1) * FF_CHUNK)
            dwg_ref[cs, :] += _dot_tn(dgp_ref[:, cs], xb)
            dwu_ref[cs, :] += _dot_tn(dup_ref[:, cs], xb)

    r = functools.partial(_rows_spec, n_tiles=nt)
    return _call(
        body, "ffn_bwd_dw", nt,
        [r(rows, D), r(rows, D_FF), r(rows, D_FF)], [_whole(), _whole()], [_sds((D_FF, D)), _sds((D_FF, D))],
    )(x1, dgp, dup)


def _loss_head(y, target):
    n = y.shape[0]
    nt = n // TILE

    def body(y_ref, t_ref, loss_ref, dy_ref):
        @pl.when(pl.program_id(0) == 0)
        def _():
            loss_ref[...] = jnp.zeros_like(loss_ref)

        e = y_ref[...] - t_ref[...]
        dy_ref[...] = e * (1.0 / D)
        loss_ref[...] += _sum0(jnp.sum(e * e, axis=1, keepdims=True)) * (0.5 / D)

    r = functools.partial(_rows_spec, n_tiles=nt)
    return _call(body, "loss_head", nt, [r(TILE, D), r(TILE, D)], [_whole(), r(TILE, D)], [_sds((1, 1)), _sds((n, D))])(y, target)


def _place():
    x, y, c = lax.axis_index("x"), lax.axis_index("y"), lax.axis_index("c")
    return x, y, c, 4 * x + 2 * y + c


def _peer(x, y, c, k):
    px, py, pc = x ^ ((k >> 2) & 1), y ^ ((k >> 1) & 1), c ^ (k & 1)
    return (px, py, pc), 4 * px + 2 * py + pc


def _all_gather(blocks, small):
    srcs = list(blocks) + [small]
    n = len(srcs)
    out_shapes = [_sds((a.shape[0], N_DEV * a.shape[1], LANES), a.dtype) for a in blocks] + [_sds((N_DEV,) + small.shape, small.dtype)]

    def body(*refs):
        src_refs, out_refs = refs[:n], refs[n:2 * n]
        send_sems, recv_sems, local_sems = refs[2 * n:]
        x, y, c, me = _place()

        def landing(a, slot):
            if a == n - 1:
                return out_refs[a].at[slot]
            r = src_refs[a].shape[1]
            return out_refs[a].at[:, pl.ds(slot * r, r), :]

        def remote(a, k, slot):
            peer, _ = _peer(x, y, c, k)
            return pltpu.make_async_remote_copy(
                src_ref=src_refs[a], dst_ref=landing(a, slot), send_sem=send_sems.at[a * N_DEV + k],
                recv_sem=recv_sems.at[a * N_DEV + k], device_id=peer, device_id_type=pl.DeviceIdType.MESH)

        mine = [pltpu.make_async_copy(src_refs[a], landing(a, me), local_sems.at[a]) for a in range(n)]
        sends = [remote(a, k, me) for a in range(n) for k in range(1, N_DEV)]
        for cp in mine + sends:
            cp.start()
        for a in range(n):
            for k in range(1, N_DEV):
                remote(a, k, _peer(x, y, c, k)[1]).wait_recv()
        for cp in sends:
            cp.wait_send()
        for cp in mine:
            cp.wait()

    any_space = pl.BlockSpec(memory_space=pl.ANY)
    return pl.pallas_call(
        body, name="gather_weights", out_shape=out_shapes, in_specs=[any_space] * n, out_specs=[any_space] * n,
        scratch_shapes=[pltpu.SemaphoreType.DMA((n * N_DEV,)), pltpu.SemaphoreType.DMA((n * N_DEV,)), pltpu.SemaphoreType.DMA((n,))],
    )(*srcs)


def _exchange_grads(big, small):
    nb = len(big)
    rows_a = [a.shape[0] // N_DEV for a in big] + [small.shape[1]]
    offs = [sum(rows_a[:a]) for a in range(nb + 1)]
    total = sum(rows_a)

    def body(*refs):
        src_refs, out_ref = refs[:nb + 1], refs[nb + 1]
        send_sems, recv_sems, local_sem = refs[nb + 2:]
        x, y, c, me = _place()

        def pieces(slot):
            out = []
            for a in range(nb + 1):
                src = src_refs[a].at[slot] if a == nb else src_refs[a].at[pl.ds(slot * rows_a[a], rows_a[a]), :]
                out.append((src, out_ref.at[me, pl.ds(offs[a], rows_a[a]), :]))
            return out

        for src, dst in pieces(me):
            pltpu.make_async_copy(src, dst, local_sem).start()
        for k in range(1, N_DEV):
            peer, peer_slot = _peer(x, y, c, k)
            for src, dst in pieces(peer_slot):
                pltpu.make_async_remote_copy(src_ref=src, dst_ref=dst, send_sem=send_sems.at[k], recv_sem=recv_sems.at[k],
                                             device_id=peer, device_id_type=pl.DeviceIdType.MESH).start()
        slots = []
        for k in range(1, N_DEV):
            peer, peer_slot = _peer(x, y, c, k)
            slots.append(pltpu.make_async_remote_copy(
                src_ref=out_ref.at[me], dst_ref=out_ref.at[peer_slot], send_sem=send_sems.at[k], recv_sem=recv_sems.at[k],
                device_id=peer, device_id_type=pl.DeviceIdType.MESH))
        for cp in slots:
            cp.wait_recv()
        for cp in slots:
            cp.wait_send()
        pltpu.make_async_copy(out_ref.at[me], out_ref.at[me], local_sem).wait()

    any_space = pl.BlockSpec(memory_space=pl.ANY)
    return pl.pallas_call(
        body, name="exchange_grads", out_shape=_sds((N_DEV, total, LANES)), in_specs=[any_space] * (nb + 1), out_specs=any_space,
        scratch_shapes=[pltpu.SemaphoreType.DMA((N_DEV,)), pltpu.SemaphoreType.DMA((N_DEV,)), pltpu.SemaphoreType.DMA(())],
    )(*big, small)


_HBM = pl.BlockSpec(memory_space=pltpu.HBM)
_SEM = pl.BlockSpec(memory_space=pltpu.SEMAPHORE)
_EFFECT = pltpu.SideEffectType.DATAFLOW_SIDE_EFFECTING


def _in_hbm(a):
    return pltpu.with_memory_space_constraint(a, pltpu.HBM)


def _scatter_start(name, srcs):
    ns = len(srcs)
    rows_a = [a.shape[0] // N_DEV for a in srcs]
    offs = [sum(rows_a[:a]) for a in range(ns)]
    total = sum(rows_a)

    def body(*refs):
        src_refs, land_ref, send_sems, recv_sems, token = refs[:ns], refs[ns], refs[ns + 1], refs[ns + 2], refs[-1]
        x, y, c, me = _place()
        for k in range(1, N_DEV):
            peer, peer_slot = _peer(x, y, c, k)
            for a in range(ns):
                pltpu.make_async_remote_copy(
                    src_ref=src_refs[a].at[pl.ds(peer_slot * rows_a[a], rows_a[a]), :],
                    dst_ref=land_ref.at[me, pl.ds(offs[a], rows_a[a]), :], send_sem=send_sems.at[k], recv_sem=recv_sems.at[k],
                    device_id=peer, device_id_type=pl.DeviceIdType.MESH).start()
        token[...] = jnp.zeros_like(token)

    landing = lax.empty((N_DEV, total, LANES), F32)
    out = pl.pallas_call(
        body, name=name,
        out_shape=(pltpu.SemaphoreType.DMA((N_DEV,)), pltpu.SemaphoreType.DMA((N_DEV,)), *[pltpu.HBM(a.shape, a.dtype) for a in srcs],
                   pltpu.HBM(landing.shape, F32), _sds((8, 128))),
        in_specs=[_HBM] * (ns + 1), out_specs=(_SEM, _SEM, *[_HBM] * (ns + 1), pl.BlockSpec(memory_space=pltpu.VMEM)),
        input_output_aliases={a: 2 + a for a in range(ns + 1)},
        compiler_params=pltpu.CompilerParams(has_side_effects=_EFFECT),
    )(*[_in_hbm(a) for a in srcs], _in_hbm(landing))
    return out[0], out[1], out[2:2 + ns], out[2 + ns], out[-1]


def _scatter_wait(name, send_sems, recv_sems, srcs, landing, after):
    ns = len(srcs)

    def body(*refs):
        land_ref, send_ref, recv_ref = refs[ns], refs[ns + 1], refs[ns + 2]
        x, y, c, me = _place()
        for k in range(1, N_DEV):
            peer, peer_slot = _peer(x, y, c, k)
            slot = pltpu.make_async_remote_copy(
                src_ref=land_ref.at[me], dst_ref=land_ref.at[peer_slot], send_sem=send_ref.at[k], recv_sem=recv_ref.at[k],
                device_id=peer, device_id_type=pl.DeviceIdType.MESH)
            slot.wait_send()
            slot.wait_recv()

    out = pl.pallas_call(
        body, name=name, out_shape=(*[pltpu.HBM(a.shape, a.dtype) for a in srcs], pltpu.HBM(landing.shape, landing.dtype)),
        in_specs=[_HBM] * (ns + 1) + [_SEM, _SEM, pl.BlockSpec(memory_space=pl.ANY)], out_specs=[_HBM] * (ns + 1),
        input_output_aliases={a: a for a in range(ns + 1)},
        compiler_params=pltpu.CompilerParams(has_side_effects=_EFFECT),
    )(*srcs, landing, send_sems, recv_sems, after)
    return out[:ns], out[ns]


def _gather_start(name, blocks):
    n = len(blocks)

    def body(*refs):
        src_refs, land_refs, send_sems, recv_sems, token = refs[:n], refs[n:2 * n], refs[2 * n], refs[2 * n + 1], refs[-1]
        x, y, c, me = _place()
        for a in range(n):
            r = src_refs[a].shape[1]
            for k in range(1, N_DEV):
                pltpu.make_async_remote_copy(
                    src_ref=src_refs[a], dst_ref=land_refs[a].at[:, pl.ds(me * r, r), :], send_sem=send_sems.at[a * N_DEV + k],
                    recv_sem=recv_sems.at[a * N_DEV + k], device_id=_peer(x, y, c, k)[0], device_id_type=pl.DeviceIdType.MESH).start()
        token[...] = jnp.zeros_like(token)

    wholes = [lax.empty((a.shape[0], N_DEV * a.shape[1], LANES), a.dtype) for a in blocks]
    out = pl.pallas_call(
        body, name=name,
        out_shape=(pltpu.SemaphoreType.DMA((n * N_DEV,)), pltpu.SemaphoreType.DMA((n * N_DEV,)),
                   *[pltpu.HBM(a.shape, a.dtype) for a in blocks + wholes], _sds((8, 128))),
        in_specs=[_HBM] * (2 * n), out_specs=(_SEM, _SEM, *[_HBM] * (2 * n), pl.BlockSpec(memory_space=pltpu.VMEM)),
        input_output_aliases={a: 2 + a for a in range(2 * n)},
        compiler_params=pltpu.CompilerParams(has_side_effects=_EFFECT),
    )(*[_in_hbm(a) for a in blocks + wholes])
    return out[0], out[1], out[2:2 + n], out[2 + n:2 + 2 * n], out[-1]


def _gather_wait(name, send_sems, recv_sems, blocks, wholes, after):
    n = len(blocks)

    def body(*refs):
        src_refs, land_refs, send_ref, recv_ref = refs[:n], refs[n:2 * n], refs[2 * n], refs[2 * n + 1]
        x, y, c, me = _place()
        for a in range(n):
            r = src_refs[a].shape[1]
            for k in range(1, N_DEV):
                peer, peer_slot = _peer(x, y, c, k)
                cp = pltpu.make_async_remote_copy(
                    src_ref=src_refs[a], dst_ref=land_refs[a].at[:, pl.ds(peer_slot * r, r), :], send_sem=send_ref.at[a * N_DEV + k],
                    recv_sem=recv_ref.at[a * N_DEV + k], device_id=peer, device_id_type=pl.DeviceIdType.MESH)
                cp.wait_send()
                cp.wait_recv()

    return pl.pallas_call(
        body, name=name, out_shape=tuple(pltpu.HBM(a.shape, a.dtype) for a in list(blocks) + list(wholes)),
        in_specs=[_HBM] * (2 * n) + [_SEM, _SEM, pl.BlockSpec(memory_space=pl.ANY)], out_specs=[_HBM] * (2 * n),
        input_output_aliases={a: a for a in range(2 * n)},
        compiler_params=pltpu.CompilerParams(has_side_effects=_EFFECT),
    )(*blocks, *wholes, send_sems, recv_sems, after)


def _reduce_adamw(parts, w, m, v, tile_rows=PACK_TILE):
    rows = w.shape[0]
    nt = rows // tile_rows
    c1 = 1.0 - ADAM_B1 ** ADAM_STEP
    c2 = 1.0 - ADAM_B2 ** ADAM_STEP

    def body(p_ref, w_ref, m_ref, v_ref, g_out, d_out, m_out, v_out):
        g = p_ref[0]
        for s in range(1, N_DEV):
            g = g + p_ref[s]
        m_new = ADAM_B1 * m_ref[...] + (1.0 - ADAM_B1) * g
        v_new = ADAM_B2 * v_ref[...] + (1.0 - ADAM_B2) * (g * g)
        g_out[...] = g
        m_out[...] = m_new
        v_out[...] = v_new
        d_out[...] = -ADAM_LR * ((m_new / c1) / (jnp.sqrt(v_new / c2) + ADAM_EPS) + ADAM_WD * w_ref[...])

    r = _rows_spec(tile_rows, LANES, nt)
    out = _sds((rows, LANES))
    return _call(
        body, "reduce_adamw", nt,
        [pl.BlockSpec((N_DEV, tile_rows, LANES), lambda i: (0, i, 0)), r, r, r], [r, r, r, r], [out, out, out, out],
    )(parts, w, m, v)


def _pack_rows(a, lead=0):
    head = a.shape[:lead]
    flat = a.reshape(head + (-1,))
    size = flat.shape[-1]
    rows = -(-size // (16 * LANES)) * 16
    flat = jnp.pad(flat, [(0, 0)] * lead + [(0, rows * LANES - size)])
    return flat.reshape(head + (rows, LANES))


def _packed_rows(shape):
    return -(-math.prod(shape) // (16 * LANES)) * 16


def _to_blocks(full, axis):
    l, a, b = full.shape
    if axis == 2:
        return full.reshape(l, a, N_DEV, b // N_DEV).transpose(2, 0, 1, 3)
    return full.reshape(l, N_DEV, a // N_DEV, b).transpose(1, 0, 2, 3)


def _from_blocks(blocks, axis):
    _, l, a, b = blocks.shape
    if axis == 2:
        return blocks.transpose(1, 2, 0, 3).reshape(l, a, N_DEV * b)
    return blocks.transpose(1, 0, 2, 3).reshape(l, N_DEV * a, b)


def _row_form(shard, transposed):
    return shard.transpose(0, 2, 1) if transposed else shard


def _me():
    return 4 * lax.axis_index("x") + 2 * lax.axis_index("y") + lax.axis_index("c")


def _both_forms(wholes):
    out = {}
    for (name, t), w in zip(BIG, wholes):
        out[name + '_t' if t else name] = w
        out[name if t else name + '_t'] = w.transpose(0, 2, 1)
    return out


def _gather_weights(local):
    segs, meta = [], []
    for name in SMALL_SHARDED:
        blk = local[name]
        if name in GATHER_F32:
            bits = lax.bitcast_convert_type(blk, MXU)
        else:
            bits = _mx(blk)
        seg = _pack_rows(bits)
        meta.append((name, bits.shape, seg.shape[0]))
        segs.append(seg)
    blocks = [_mx(_row_form(local[name], t)) for name, t in BIG]
    *first, gathered = _all_gather([b[:1] for b in blocks], jnp.concatenate(segs, axis=0))
    later = [b[1:] for b in blocks]
    send_sems, recv_sems, later, landing, token = _gather_start("gather_later_start", later)
    out, lo = {}, 0
    for name, bits_shape, rows in meta:
        seg = gathered[:, lo:lo + rows].reshape(N_DEV, -1)[:, :math.prod(bits_shape)].reshape((N_DEV,) + bits_shape)
        if name in GATHER_F32:
            seg = lax.bitcast_convert_type(seg, F32)
        out[name] = _from_blocks(seg, SHARD_AXIS[name])
        lo += rows
    layers = {0: _both_forms(first)}

    def big_weights(l, h):
        if l not in layers:
            done = _gather_wait("gather_later_wait", send_sems, recv_sems, later, landing, h)
            mine, wholes = done[:len(BIG)], done[len(BIG):]
            own = [lax.dynamic_update_slice(w, b, (0, _me() * b.shape[1], 0)) for w, b in zip(wholes, mine)]
            for k, v in _both_forms(own).items():
                for j in range(1, DEPTH):
                    layers.setdefault(j, {})[k] = v[j - 1]
            return layers[l]
        return {k: v[0] for k, v in layers[0].items()} if l == 0 else layers[l]

    return out, big_weights, token[0, 0]


def _s5_discretize(a_re, a_im, log_dt, b_re, b_im):
    lam_re = jnp.minimum(a_re, -1e-4)
    lam_im = a_im
    dt = jnp.exp(log_dt)[:, None]
    decay = jnp.exp(dt * lam_re)
    ang = dt * lam_im
    abar_re = decay * jnp.cos(ang)
    abar_im = decay * jnp.sin(ang)
    den = jnp.square(lam_re) + jnp.square(lam_im)
    nr = abar_re - 1.0
    ni = abar_im
    coef_re = (nr * lam_re + ni * lam_im) / den
    coef_im = (ni * lam_re - nr * lam_im) / den
    bbar_re = coef_re[..., None] * b_re - coef_im[..., None] * b_im
    bbar_im = coef_re[..., None] * b_im + coef_im[..., None] * b_re
    return abar_re, abar_im, bbar_re, bbar_im


def _complex_powers(ar, ai, count):
    def combine(e1, e2):
        return e2[0] * e1[0] - e2[1] * e1[1], e2[0] * e1[1] + e2[1] * e1[0]

    shape = (count,) + ar.shape
    return lax.associative_scan(combine, (jnp.broadcast_to(ar, shape), jnp.broadcast_to(ai, shape)), axis=0)


_EYE16 = functools.partial(jnp.eye, 16, dtype=F32)


def _s5_params(p, l):
    disc, disc_vjp = jax.vjp(_s5_discretize, p['s5_a_re'][l], p['s5_a_im'][l], p['s5_log_dt'][l], p['s5_b_re'][l], p['s5_b_im'][l])
    abar_re, abar_im, bbar_re, bbar_im = disc
    ar, ai = abar_re.reshape(N_STATE), abar_im.reshape(N_STATE)
    ap_re, ap_im = _complex_powers(ar, ai, TILE)
    one, zero = jnp.ones((1, N_STATE), F32), jnp.zeros((1, N_STATE), F32)
    seg_re = jnp.concatenate([one, ap_re[SEG - 1:TILE - SEG:SEG]], axis=0)
    seg_im = jnp.concatenate([zero, ap_im[SEG - 1:TILE - SEG:SEG]], axis=0)
    doubling = [SEG - 1, 2 * SEG - 1, 4 * SEG - 1]

    src = (jnp.arange(TILE) % N_SEG) * SEG + jnp.arange(TILE) // N_SEG
    perm = (src[:, None] == jnp.arange(TILE)[None, :]).astype(MXU)
    prm = {
        'perm': perm, 'perm_t': perm.T,
        'pw_re': ap_re[:SEG], 'pw_im': ap_im[:SEG],
        'dbl_re': jnp.concatenate([jnp.stack([ap_re[k] for k in doubling]), jnp.zeros((5, N_STATE), F32)], axis=0),
        'dbl_im': jnp.concatenate([jnp.stack([ap_im[k] for k in doubling]), jnp.zeros((5, N_STATE), F32)], axis=0),
        'seg_re': seg_re, 'seg_im': seg_im, 'segr_re': seg_re[::-1], 'segr_im': seg_im[::-1],
        'b_re': _mx(jnp.einsum('gpc,gh->gchp', bbar_re, _EYE16()).reshape(D_S5, N_STATE)),
        'b_im': _mx(jnp.einsum('gpc,gh->gchp', bbar_im, _EYE16()).reshape(D_S5, N_STATE)),
        'c_re': _mx(jnp.einsum('gcp,gh->gphc', p['s5_c_re'][l], _EYE16()).reshape(N_STATE, D_S5)),
        'c_im': _mx(jnp.einsum('gcp,gh->gphc', p['s5_c_im'][l], _EYE16()).reshape(N_STATE, D_S5)),
        'd': p['s5_d'][l][None, :], 'glu_w': p['s5_glu_w'][l], 'glu_b': p['s5_glu_b'][l][None, :],
    }
    return prm, disc_vjp


def _lru_params(p, l):
    eye4 = jnp.eye(4, dtype=F32)
    return {
        'conv_w': p['lru_conv_w'][l], 'conv_b': p['lru_conv_b'][l][None, :],
        'wx': _mx(jnp.einsum('hij,hk->hikj', p['lru_wx'][l], eye4).reshape(D_LRU, D_LRU)),
        'wa': _mx(jnp.einsum('hij,hk->hikj', p['lru_wa'][l], eye4).reshape(D_LRU, D_LRU)),
        'bx': p['lru_bx'][l][None, :], 'ba': p['lru_ba'][l][None, :],
        'sp': jax.nn.softplus(-p['lru_a_param'][l])[None, :],
    }


def _rope_tables(n):
    inv_freq = ROPE_THETA ** (-jnp.arange(0, 64, 2, dtype=F32) / 64)
    ang = jnp.arange(n, dtype=F32)[:, None] * inv_freq[None, :]
    cos, sin = jnp.cos(ang), jnp.sin(ang)
    return jnp.concatenate([cos, cos, cos, cos], axis=1), jnp.concatenate([-sin, sin, -sin, sin], axis=1)


def _sink_cols(sinks):
    nb = TILE_Q // ATTN_BLOCK
    per_unit = sinks.reshape(4, 2).T
    return jnp.broadcast_to(per_unit[:, None, :, None], (2, nb, 4, ATTN_BLOCK)).reshape(2, nb * 4 * ATTN_BLOCK, 1)


def _local_step(x, target, p, big_weights=None, emit_grads=None):
    if big_weights is None:
        big_weights = lambda l, h: {k: p[k][l] for name, _ in BIG for k in (name, name + '_t')}
    n = x.shape[0]
    cos_t, sin_t = _rope_tables(n)
    row = lambda a: a[None, :]
    saved = []
    h = x
    for l in range(DEPTH):
        s = {'x0': h}
        bw = s['bw'] = big_weights(l, h)
        s['q'], s['k'], s['v'], s['u'], s['xr'], s['gate'] = _inproj_fwd(h, bw['w_in'], row(p['b_in'][l]), cos_t, sin_t)
        s['sinks'] = _sink_cols(p['attn_sinks'][l])
        s['ya'] = _attn_fwd(s['q'], s['k'], s['v'], s['sinks'])
        s['s5'], s['s5_vjp'] = _s5_params(p, l)
        s['ys'], s['s5_cr'], s['s5_ci'] = _s5_fwd(s['u'], s['s5'])
        s['lru'] = _lru_params(p, l)
        s['yl'], s['lru_c'] = _lru_fwd(s['xr'], s['gate'], s['lru'])
        s['mix'], s['r1'], s['x1'] = _mixout_fwd(s['ya'], s['ys'], s['yl'], h, row(p['mix_norm_g'][l]), bw['w_out'],
                                                 row(p['b_out'][l]), row(p['ln1_g'][l]), row(p['ln1_b'][l]))
        s['gp'], s['up'], s['r2'], h = _ffn_fwd(s['x1'], bw['ffn_w_gate'], bw['ffn_w_up'], p['ffn_conv_w'][l],
                                                row(p['ffn_conv_b'][l]), bw['ffn_w_down'], row(p['ln2_g'][l]), row(p['ln2_b'][l]))
        saved.append(s)
    loss, dh = _loss_head(h, target)
    placed = 0.0

    grads = {name: [None] * DEPTH for name in WEIGHTS}
    for l in reversed(range(DEPTH)):
        s = saved[l]
        g = {}
        (dr2, dgp, dup, g['ffn_w_down'], cw0, cw1, cw2, dcb, dg2, db2) = _ffn_bwd_down(
            dh, s['r2'], row(p['ln2_g'][l]) + placed, s['gp'], s['up'], p['ffn_conv_w'][l], row(p['ffn_conv_b'][l]),
            s['bw']['ffn_w_down_t'])
        g['ffn_conv_w'] = jnp.concatenate([cw0, cw1, cw2], axis=0)
        g['ffn_conv_b'], g['ln2_g'], g['ln2_b'] = dcb[0], dg2[0], db2[0]
        dr1, dg1, db1 = _ffn_bwd_dx(dr2, dgp, dup, s['r1'], row(p['ln1_g'][l]), s['bw']['ffn_w_gate_t'], s['bw']['ffn_w_up_t'])
        g['ffn_w_gate'], g['ffn_w_up'] = _ffn_bwd_dw(s['x1'], dgp, dup)
        g['ln1_g'], g['ln1_b'] = dg1[0], db1[0]
        dya, dys, dyl, g['w_out'], dbo, dgm = _mixout_bwd(dr1, s['mix'], s['ya'], s['ys'], s['yl'], row(p['mix_norm_g'][l]),
                                                         s['bw']['w_out_t'])
        g['b_out'], g['mix_norm_g'] = dbo[0], dgm[0]

        dxr, dgate, lw0, lw1, lw2, lw3, lcb, dwx, dwa, dbx, dba, dsp = _lru_bwd(s['xr'], s['gate'], dyl, s['lru_c'], s['lru'])
        g['lru_conv_w'] = jnp.concatenate([lw0, lw1, lw2, lw3], axis=0)
        g['lru_conv_b'], g['lru_bx'], g['lru_ba'] = lcb[0], dbx[0], dba[0]
        g['lru_wx'] = jnp.einsum('hihj->hij', dwx.reshape(4, 64, 4, 64))
        g['lru_wa'] = jnp.einsum('hihj->hij', dwa.reshape(4, 64, 4, 64))
        g['lru_a_param'] = -dsp[0] * jax.nn.sigmoid(-p['lru_a_param'][l])

        du, dbr, dbi, dcr, dci, dar, dai, dd, g['s5_glu_w'], dgb = _s5_bwd(s['u'], dys, s['s5_cr'], s['s5_ci'], s['s5'])
        g['s5_c_re'] = jnp.einsum('gpgc->gcp', dcr.reshape(16, 64, 16, 16))
        g['s5_c_im'] = jnp.einsum('gpgc->gcp', dci.reshape(16, 64, 16, 16))
        g['s5_d'], g['s5_glu_b'] = dd[0], dgb[0]
        g['s5_a_re'], g['s5_a_im'], g['s5_log_dt'], g['s5_b_re'], g['s5_b_im'] = s['s5_vjp']((
            dar.reshape(16, 64), dai.reshape(16, 64), jnp.einsum('gcgp->gpc', dbr.reshape(16, 16, 16, 64)),
            jnp.einsum('gcgp->gpc', dbi.reshape(16, 16, 16, 64))))

        dq, dk, dv, dsink = _attn_bwd(s['q'], s['k'], s['v'], s['sinks'], s['ya'], dya)
        g['attn_sinks'] = dsink[:, 0]
        dh, g['w_in'], dbin = _inproj_bwd(dq, dk, dv, du, dxr, dgate, cos_t, sin_t, s['x0'], dr1, s['bw']['w_in_t'])
        g['b_in'] = dbin[0]
        if emit_grads is not None:
            placed = emit_grads(l, [g[name] for name, _ in BIG])
        for name in WEIGHTS:
            grads[name][l] = g[name]
    big = dict(BIG)
    return loss, dh, {name: grads[name] if name in big else jnp.stack(grads[name]) for name in WEIGHTS}


def kernel(x, w_in, b_in, attn_sinks, s5_a_re, s5_a_im, s5_b_re, s5_b_im, s5_c_re, s5_c_im, s5_d, s5_log_dt, s5_glu_w, s5_glu_b, lru_conv_w, lru_conv_b, lru_wx, lru_bx, lru_wa, lru_ba, lru_a_param, mix_norm_g, w_out, b_out, ln1_g, ln1_b, ffn_w_gate, ffn_w_up, ffn_conv_w, ffn_conv_b, ffn_w_down, ln2_g, ln2_b, loss_target, m_w_in, m_b_in, m_attn_sinks, m_s5_a_re, m_s5_a_im, m_s5_b_re, m_s5_b_im, m_s5_c_re, m_s5_c_im, m_s5_d, m_s5_log_dt, m_s5_glu_w, m_s5_glu_b, m_lru_conv_w, m_lru_conv_b, m_lru_wx, m_lru_bx, m_lru_wa, m_lru_ba, m_lru_a_param, m_mix_norm_g, m_w_out, m_b_out, m_ln1_g, m_ln1_b, m_ffn_w_gate, m_ffn_w_up, m_ffn_conv_w, m_ffn_conv_b, m_ffn_w_down, m_ln2_g, m_ln2_b, v_w_in, v_b_in, v_attn_sinks, v_s5_a_re, v_s5_a_im, v_s5_b_re, v_s5_b_im, v_s5_c_re, v_s5_c_im, v_s5_d, v_s5_log_dt, v_s5_glu_w, v_s5_glu_b, v_lru_conv_w, v_lru_conv_b, v_lru_wx, v_lru_bx, v_lru_wa, v_lru_ba, v_lru_a_param, v_mix_norm_g, v_w_out, v_b_out, v_ln1_g, v_ln1_b, v_ffn_w_gate, v_ffn_w_up, v_ffn_conv_w, v_ffn_conv_b, v_ffn_w_down, v_ln2_g, v_ln2_b):
    given = dict(locals())
    whole = {name: given[name] for name in WEIGHTS if name not in dict(BIG)}
    small_whole, big_weights, placed = _gather_weights({name: given[name] for name in SHARDED})
    whole.update(small_whole)
    whole['b_in'] = whole['b_in'] + placed
    in_flight = {}

    def emit_grads(l, grads):
        in_flight[l] = _scatter_start(f"grads_start_{l}", grads)
        return in_flight[l][-1][0, 0]

    loss, grad_x, grads = _local_step(x[0], loss_target[0], whole, big_weights, emit_grads)
    total = lax.psum(loss[0, 0], ("x", "y", "c"))
    return (total, grad_x[None], *_update(given, grads, in_flight, grad_x))


def _update(given, grads, in_flight, after):
    local_w = {name: given[name] for name in WEIGHTS}
    me = _me()
    outs = {}

    forms = [(name, t, _row_form(local_w[name], t).shape[1]) for name, t in BIG]
    layer_rows = sum(r for _, _, r in forms)
    for l in range(DEPTH):
        send_sems, recv_sems, srcs, landing, _ = in_flight[l]
        srcs, landing = _scatter_wait(f"grads_wait_{l}", send_sems, recv_sems, srcs, landing, after)
        own = jnp.concatenate([lax.dynamic_slice_in_dim(g, me * r, r, axis=0) for g, (_, _, r) in zip(srcs, forms)], axis=0)
        parts = lax.dynamic_update_slice(landing, own[None], (me, 0, 0))
        packed = [jnp.concatenate([_row_form(given[prefix + name], t)[l] for name, t, _ in forms], axis=0) for prefix in ('', 'm_', 'v_')]
        outs[l] = _reduce_adamw(parts, *packed, tile_rows=layer_rows // 4)

    rest = SMALL_SHARDED + REPLICATED
    pad = -sum(_packed_rows(local_w[name].shape) for name in rest) % PACK_TILE
    small = jnp.concatenate(
        [_pack_rows(_to_blocks(grads[name], SHARD_AXIS[name]), lead=1) for name in SMALL_SHARDED]
        + [jnp.broadcast_to(_pack_rows(grads[name]), (N_DEV, _packed_rows(grads[name].shape), LANES)) for name in REPLICATED]
        + [jnp.zeros((N_DEV, pad, LANES), F32)], axis=1)

    def packed_rest(prefix):
        return jnp.concatenate([_pack_rows(given[prefix + name]) for name in rest] + [jnp.zeros((pad, LANES), F32)], axis=0)

    rest_outs = _reduce_adamw(_exchange_grads([], small), packed_rest(''), packed_rest('m_'), packed_rest('v_'))

    def unpack(i):
        res, lo = {}, 0
        for name, t, r in forms:
            res[name] = _row_form(jnp.stack([outs[l][i][lo:lo + r] for l in range(DEPTH)]), t)
            lo += r
        lo = 0
        for name in rest:
            shape = local_w[name].shape
            res[name] = rest_outs[i][lo:lo + _packed_rows(shape)].reshape(-1)[:math.prod(shape)].reshape(shape)
            lo += _packed_rows(shape)
        return [res[name] for name in WEIGHTS]

    return (*unpack(0), *unpack(1), *unpack(2), *unpack(3))
```

```python
import functools
import math

import jax
import jax.numpy as jnp
from jax import lax
from jax.experimental import pallas as pl
from jax.experimental.pallas import tpu as pltpu

F32 = jnp.float32
MXU = jnp.bfloat16

N_DEV = 8
DEPTH = 4
D = 1024
D_ATTN, D_KV, D_S5, D_LRU = 512, 128, 256, 256
D_IN = 1536
D_FF = 2816
FF_CHUNK = 256
N_STATE = 1024
LANES = 1024
ALPHA = (2 * DEPTH) ** 0.25
LN_EPS = 1e-5
RMS_EPS = 1e-6
LRU_C = 8.0
ROPE_THETA = 10000.0
ADAM_LR, ADAM_B1, ADAM_B2, ADAM_EPS, ADAM_WD, ADAM_STEP = 0.001, 0.9, 0.999, 1e-08, 0.01, 10

TILE = 256
N_SEG = 8
SEG = TILE // N_SEG
TILE_Q = 512
TILE_BIG = 512
TILE_WIDE = 512
ATTN_BLOCK = 128
PACK_TILE = 256
VMEM_MB = 56

WEIGHTS = ['w_in', 'b_in', 'attn_sinks', 's5_a_re', 's5_a_im', 's5_b_re', 's5_b_im', 's5_c_re', 's5_c_im', 's5_d', 's5_log_dt',
           's5_glu_w', 's5_glu_b', 'lru_conv_w', 'lru_conv_b', 'lru_wx', 'lru_bx', 'lru_wa', 'lru_ba', 'lru_a_param', 'mix_norm_g',
           'w_out', 'b_out', 'ln1_g', 'ln1_b', 'ffn_w_gate', 'ffn_w_up', 'ffn_conv_w', 'ffn_conv_b', 'ffn_w_down', 'ln2_g', 'ln2_b']
SHARD_AXIS = {'w_in': 2, 's5_glu_w': 1, 'lru_conv_w': 2, 'w_out': 1, 'ffn_w_gate': 2, 'ffn_w_up': 2, 'ffn_conv_w': 2,
              'ffn_w_down': 1}
SHARDED = [n for n in WEIGHTS if n in SHARD_AXIS]
REPLICATED = [n for n in WEIGHTS if n not in SHARD_AXIS]
BIG = [(n, SHARD_AXIS[n] == 2) for n in ('w_in', 'w_out', 'ffn_w_gate', 'ffn_w_up', 'ffn_w_down')]
SMALL_SHARDED = [n for n in SHARDED if n not in dict(BIG)]
PARTS = {'mix': ['w_in', 'w_out'], 'ffn': ['ffn_w_gate', 'ffn_w_up', 'ffn_w_down']}
GATHER_F32 = ('lru_conv_w', 'ffn_conv_w')


def _dot(a, b):
    return jnp.dot(a, b, preferred_element_type=F32)


def _dot_nt(a, b):
    return lax.dot_general(a, b, (((1,), (1,)), ((), ())), preferred_element_type=F32)


def _dot_tn(a, b):
    return lax.dot_general(a, b, (((0,), (0,)), ((), ())), preferred_element_type=F32)


def _mx(a):
    return a.astype(MXU)


_GELU_C = math.sqrt(2.0 / math.pi)


def _gelu(x):
    th = jnp.tanh(_GELU_C * (x + 0.044715 * x * x * x))
    return 0.5 * x * (1.0 + th)


def _gelu_grad(x):
    th = jnp.tanh(_GELU_C * (x + 0.044715 * x * x * x))
    return 0.5 * (1.0 + th) + 0.5 * x * (1.0 - th * th) * _GELU_C * (1.0 + 3.0 * 0.044715 * x * x)


def _sigmoid(x):
    return 1.0 / (1.0 + jnp.exp(-x))


def _ln_stats(r):
    mu = jnp.mean(r, axis=-1, keepdims=True)
    xc = r - mu
    var = jnp.mean(xc * xc, axis=-1, keepdims=True)
    rstd = lax.rsqrt(var + LN_EPS)
    return xc * rstd, rstd


def _ln_bwd(dy, g, xhat, rstd):
    dxh = dy * g
    return rstd * (dxh - jnp.mean(dxh, axis=-1, keepdims=True) - xhat * jnp.mean(dxh * xhat, axis=-1, keepdims=True))


def _rms(y):
    return lax.rsqrt(jnp.mean(y * y, axis=-1, keepdims=True) + RMS_EPS)


def _sum0(a):
    return jnp.sum(a, axis=0, keepdims=True)


def _row_iota(shape):
    return lax.broadcasted_iota(jnp.int32, shape, 0)


def _shift_down(ext, j, rows):
    return pltpu.roll(ext, j, 0)[8:8 + rows]


def _shift_up(ext, j, rows):
    return pltpu.roll(ext, ext.shape[0] - j, 0)[:rows]


def _swap_halves(t):
    w = t.shape[1]
    lane = lax.broadcasted_iota(jnp.int32, t.shape, 1)
    return jnp.where((lane & 32) == 0, pltpu.roll(t, w - 32, 1), pltpu.roll(t, 32, 1))


def _rope(t, cos, sin_signed):
    return t * cos + _swap_halves(t) * sin_signed


def _rope_t(d, cos, sin_signed):
    return d * cos + _swap_halves(d * sin_signed)


def _cmul_add(ar, ai, xr, xi, yr, yi):
    return ar * xr - ai * xi + yr, ar * xi + ai * xr + yi


def _seg_rows(k):
    return slice(N_SEG * k, N_SEG * (k + 1))


def _permute_rows(perm, x):
    hi = _mx(x)
    rest = x - hi.astype(F32)
    mid = _mx(rest)
    lo = _mx(rest - mid.astype(F32))
    return _dot(perm, hi) + _dot(perm, mid) + _dot(perm, lo)


def _cscan(sr, si, tab, cin_r, cin_i, reverse):
    sgn = -1.0 if reverse else 1.0
    pw_re, pw_im, dbl_re, dbl_im = tab['pw_re'], tab['pw_im'], tab['dbl_re'], tab['dbl_im']
    ar, ai = pw_re[0:1, :], sgn * pw_im[0:1, :]
    shape = (N_SEG, sr.shape[1])
    hr = hi = jnp.zeros(shape, F32)
    for k in (range(SEG - 1, -1, -1) if reverse else range(SEG)):
        hr, hi = _cmul_add(ar, ai, hr, hi, sr[_seg_rows(k), :], si[_seg_rows(k), :])
        sr[_seg_rows(k), :] = hr
        si[_seg_rows(k), :] = hi
    sub = _row_iota(shape)

    def shifted(v, d):
        if reverse:
            return jnp.where(sub < N_SEG - d, pltpu.roll(v, N_SEG - d, 0), 0.0)
        return jnp.where(sub >= d, pltpu.roll(v, d, 0), 0.0)

    fr, fi = hr, hi
    for j, d in enumerate((1, 2, 4)):
        fr, fi = _cmul_add(dbl_re[j:j + 1, :], sgn * dbl_im[j:j + 1, :], shifted(fr, d), shifted(fi, d), fr, fi)
    seg_re, seg_im = (tab['segr_re'], tab['segr_im']) if reverse else (tab['seg_re'], tab['seg_im'])
    cr, ci = _cmul_add(seg_re[...], sgn * seg_im[...], cin_r, cin_i, shifted(fr, 1), shifted(fi, 1))
    nr, ni = _cmul_add(dbl_re[0:1, :], sgn * dbl_im[0:1, :], cr, ci, hr, hi)
    for k in range(SEG):
        j = SEG - 1 - k if reverse else k
        xr, xi = _cmul_add(pw_re[j:j + 1, :], sgn * pw_im[j:j + 1, :], cr, ci, sr[_seg_rows(k), :], si[_seg_rows(k), :])
        sr[_seg_rows(k), :] = xr
        si[_seg_rows(k), :] = xi
    edge = slice(0, 1) if reverse else slice(N_SEG - 1, N_SEG)
    return nr[edge], ni[edge]


def _rscan(a, b, reverse):
    rows = a.shape[0]
    row = _row_iota(a.shape)
    s = 1
    while s < rows:
        if reverse:
            keep = row < rows - s
            sa = jnp.where(keep, pltpu.roll(a, rows - s, 0), 1.0)
            sb = jnp.where(keep, pltpu.roll(b, rows - s, 0), 0.0)
        else:
            keep = row >= s
            sa = jnp.where(keep, pltpu.roll(a, s, 0), 1.0)
            sb = jnp.where(keep, pltpu.roll(b, s, 0), 0.0)
        b = b + a * sb
        a = a * sa
        s *= 2
    return a, b


def _whole():
    return pl.BlockSpec(memory_space=pltpu.VMEM)


def _rows_spec(rows, cols, n_tiles, reverse=False):
    if reverse:
        return pl.BlockSpec((rows, cols), lambda i: (n_tiles - 1 - i, 0))
    return pl.BlockSpec((rows, cols), lambda i: (i, 0))


def _halo_spec(cols, tile_rows, n_tiles, reverse=False):
    per = tile_rows // 8
    if reverse:
        return pl.BlockSpec((8, cols), lambda i: (jnp.maximum((n_tiles - 1 - i) * per - 1, 0), 0))
    return pl.BlockSpec((8, cols), lambda i: (jnp.maximum(i * per - 1, 0), 0))


def _call(body, name, n_tiles, in_specs, out_specs, out_shape, scratch=()):
    return pl.pallas_call(
        body, name=name, grid=(n_tiles,), in_specs=in_specs, out_specs=out_specs, out_shape=out_shape,
        scratch_shapes=list(scratch),
        compiler_params=pltpu.CompilerParams(dimension_semantics=("arbitrary",), vmem_limit_bytes=VMEM_MB << 20))


def _sds(shape, dtype=F32):
    return jax.ShapeDtypeStruct(shape, dtype)


def _inproj_fwd(x, w, b, cos_t, sin_t):
    n = x.shape[0]
    nt = n // TILE_WIDE

    def body(x_ref, w_ref, b_ref, c_ref, s_ref, q_ref, k_ref, v_ref, u_ref, xr_ref, g_ref):
        p = _dot(_mx(x_ref[...]), w_ref[...]) + b_ref[...]
        cos, sin = c_ref[...], s_ref[...]
        q_ref[...] = _mx(_rope(p[:, :D_ATTN], jnp.tile(cos, (1, 4)), jnp.tile(sin, (1, 4))))
        k_ref[...] = _mx(_rope(p[:, 512:640], cos, sin))
        v_ref[...] = _mx(p[:, 640:768])
        u_ref[...] = p[:, 768:1024]
        xr_ref[...] = p[:, 1024:1280]
        g_ref[...] = p[:, 1280:1536]

    r = functools.partial(_rows_spec, n_tiles=nt)
    return _call(
        body, "inproj_fwd", nt,
        [r(TILE_WIDE, D), _whole(), _whole(), r(TILE_WIDE, 128), r(TILE_WIDE, 128)],
        [r(TILE_WIDE, D_ATTN), r(TILE_WIDE, D_KV), r(TILE_WIDE, D_KV), r(TILE_WIDE, D_S5), r(TILE_WIDE, D_LRU), r(TILE_WIDE, D_LRU)],
        [_sds((n, D_ATTN), MXU), _sds((n, D_KV), MXU), _sds((n, D_KV), MXU), _sds((n, D_S5)), _sds((n, D_LRU)), _sds((n, D_LRU))],
    )(x, w, b, cos_t, sin_t)


def _inproj_bwd(dq, dk, dv, du, dxr, dgate, cos_t, sin_t, x0, dr1, w_t):
    n = x0.shape[0]
    nt = n // TILE_WIDE

    def body(dq_ref, dk_ref, dv_ref, du_ref, dxr_ref, dg_ref, c_ref, s_ref, x_ref, dr_ref, w_ref, dx_ref, dw_ref, db_ref):
        @pl.when(pl.program_id(0) == 0)
        def _():
            dw_ref[...] = jnp.zeros_like(dw_ref)
            db_ref[...] = jnp.zeros_like(db_ref)

        cos, sin = c_ref[...], s_ref[...]
        dtq = _rope_t(dq_ref[...], jnp.tile(cos, (1, 4)), jnp.tile(sin, (1, 4)))
        dtk = _rope_t(dk_ref[...], cos, sin)
        dp = jnp.concatenate([dtq, dtk, dv_ref[...], du_ref[...], dxr_ref[...], dg_ref[...]], axis=1)
        db_ref[...] += _sum0(dp)
        dpb = _mx(dp)
        dw_ref[...] += _dot_tn(dpb, _mx(x_ref[...]))
        dx_ref[...] = ALPHA * dr_ref[...] + _dot(dpb, w_ref[...])

    r = functools.partial(_rows_spec, n_tiles=nt)
    return _call(
        body, "inproj_bwd", nt,
        [r(TILE_WIDE, D_ATTN), r(TILE_WIDE, D_KV), r(TILE_WIDE, D_KV), r(TILE_WIDE, D_S5), r(TILE_WIDE, D_LRU), r(TILE_WIDE, D_LRU),
         r(TILE_WIDE, 128), r(TILE_WIDE, 128), r(TILE_WIDE, D), r(TILE_WIDE, D), _whole()],
        [r(TILE_WIDE, D), _whole(), _whole()],
        [_sds((n, D)), _sds((D_IN, D)), _sds((1, D_IN))],
    )(dq, dk, dv, du, dxr, dgate, cos_t, sin_t, x0, dr1, w_t)


def _kv_variants(t, lo):
    tr = pltpu.roll(t, 64, 1)
    out = []
    for j in range(2):
        first = jnp.where(lo, t if j == 0 else tr, 0.0)
        second = jnp.where(lo, 0.0, tr if j == 0 else t)
        out.append(_mx(jnp.concatenate([first, second], axis=0)))
    return out


def _kv_collect(x0, x1, lo):
    a = x0[:256] + pltpu.roll(x0[256:], 64, 1)
    b = pltpu.roll(x1[:256], 64, 1) + x1[256:]
    return jnp.where(lo, a, b)


def _attn_probs(s, sink_ref):
    out = []
    for hp in range(2):
        sh = s[:, hp * 256:(hp + 1) * 256]
        sink = sink_ref[hp]
        m = jnp.maximum(jnp.max(sh, axis=1, keepdims=True), sink)
        p = jnp.exp(sh - m)
        es = jnp.exp(sink - m)
        inv = 1.0 / (jnp.sum(p, axis=1, keepdims=True) + es)
        out.append((p * inv, es * inv))
    return out


def _attn_scores(q_ref, k_ref, v_ref, nb):
    lo = lax.broadcasted_iota(jnp.int32, (256, 128), 1) < 64
    kcats, vcats, kstarts, parts = [], [], [], []
    for b in range(nb):
        block = pl.program_id(0) * nb + b
        kstart = pl.multiple_of(jnp.maximum(block - 1, 0) * ATTN_BLOCK, ATTN_BLOCK)
        kcat = _kv_variants(k_ref[pl.ds(kstart, 256), :].astype(F32), lo)
        kcats.append(kcat)
        vcats.append(_kv_variants(v_ref[pl.ds(kstart, 256), :].astype(F32), lo))
        kstarts.append(kstart)
        valid = _attn_mask(block, kstart)
        for i in range(4):
            s = _dot_nt(q_ref[b * ATTN_BLOCK:(b + 1) * ATTN_BLOCK, i * 128:(i + 1) * 128], kcat[i // 2]) * 0.125
            parts.append(jnp.where(valid, s, -jnp.inf))
    return jnp.concatenate(parts, axis=0), kcats, vcats, kstarts


def _attn_mask(block, kstart):
    col = lax.broadcasted_iota(jnp.int32, (ATTN_BLOCK, 512), 1)
    row = lax.broadcasted_iota(jnp.int32, (ATTN_BLOCK, 512), 0)
    diff = (block * ATTN_BLOCK + row) - (kstart + (col & 255))
    return (diff >= 0) & (diff < ATTN_BLOCK)


def _attn_fwd(q, k, v, sink_cols):
    n = q.shape[0]
    nt = n // TILE_Q
    nb = TILE_Q // ATTN_BLOCK

    def body(q_ref, k_ref, v_ref, s_ref, o_ref):
        s, _, vcats, _ = _attn_scores(q_ref, k_ref, v_ref, nb)
        (p0, _), (p1, _) = _attn_probs(s, s_ref)
        pb = _mx(jnp.concatenate([p0, p1], axis=1))
        for b in range(nb):
            for i in range(4):
                unit = (b * 4 + i) * ATTN_BLOCK
                o_ref[b * ATTN_BLOCK:(b + 1) * ATTN_BLOCK, i * 128:(i + 1) * 128] = _dot(pb[unit:unit + ATTN_BLOCK], vcats[b][i // 2])

    return _call(
        body, "attn_fwd", nt,
        [_rows_spec(TILE_Q, D_ATTN, nt), _whole(), _whole(), _whole()],
        _rows_spec(TILE_Q, D_ATTN, nt), _sds((n, D_ATTN)),
    )(q, k, v, sink_cols)


def _attn_bwd(q, k, v, sink_cols, o, do):
    n = q.shape[0]
    nt = n // TILE_Q
    nb = TILE_Q // ATTN_BLOCK

    def body(q_ref, k_ref, v_ref, s_ref, o_ref, do_ref, dq_ref, dk_ref, dv_ref, ds_ref):
        @pl.when(pl.program_id(0) == 0)
        def _():
            dk_ref[...] = jnp.zeros_like(dk_ref)
            dv_ref[...] = jnp.zeros_like(dv_ref)
            ds_ref[...] = jnp.zeros_like(ds_ref)

        lo = lax.broadcasted_iota(jnp.int32, (256, 128), 1) < 64
        s, kcats, vcats, kstarts = _attn_scores(q_ref, k_ref, v_ref, nb)
        probs = _attn_probs(s, s_ref)
        do = do_ref[...]
        dob = _mx(do)
        od = do * o_ref[...]
        lo_q = (lax.broadcasted_iota(jnp.int32, od.shape, 1) & 64) == 0
        od_head = (jnp.where(lo_q, od, 0.0), jnp.where(lo_q, 0.0, od))
        units = [(b, i) for b in range(nb) for i in range(4)]

        def tile_part(a, b, i):
            return a[b * ATTN_BLOCK:(b + 1) * ATTN_BLOCK, i * 128:(i + 1) * 128]

        dp = jnp.concatenate([_dot_nt(tile_part(dob, b, i), vcats[b][i // 2]) for b, i in units], axis=0)
        ds = []
        for hp in range(2):
            p, p_sink = probs[hp]
            delta = jnp.concatenate([jnp.sum(tile_part(od_head[hp], b, i), axis=1, keepdims=True) for b, i in units], axis=0)
            ds.append(p * (dp[:, hp * 256:(hp + 1) * 256] - delta) * 0.125)
            t = p_sink * delta
            for i in range(4):
                dsink = sum(_sum0(t[(b * 4 + i) * ATTN_BLOCK:(b * 4 + i + 1) * ATTN_BLOCK]) for b in range(nb))
                ds_ref[2 * i + hp:2 * i + hp + 1, :] -= jnp.broadcast_to(dsink, (1, 128))
        dsb = _mx(jnp.concatenate(ds, axis=1))
        pb = _mx(jnp.concatenate([probs[0][0], probs[1][0]], axis=1))
        for b in range(nb):
            dkc = [jnp.zeros((512, 128), F32), jnp.zeros((512, 128), F32)]
            dvc = [jnp.zeros((512, 128), F32), jnp.zeros((512, 128), F32)]
            for i in range(4):
                j = i // 2
                unit = slice((b * 4 + i) * ATTN_BLOCK, (b * 4 + i + 1) * ATTN_BLOCK)
                dq_ref[b * ATTN_BLOCK:(b + 1) * ATTN_BLOCK, i * 128:(i + 1) * 128] = _dot(dsb[unit], kcats[b][j])
                dkc[j] = dkc[j] + _dot_tn(dsb[unit], tile_part(q_ref, b, i))
                dvc[j] = dvc[j] + _dot_tn(pb[unit], tile_part(dob, b, i))
            dk_ref[pl.ds(kstarts[b], 256), :] += _kv_collect(dkc[0], dkc[1], lo)
            dv_ref[pl.ds(kstarts[b], 256), :] += _kv_collect(dvc[0], dvc[1], lo)

    r = _rows_spec(TILE_Q, D_ATTN, nt)
    return _call(
        body, "attn_bwd", nt,
        [r, _whole(), _whole(), _whole(), r, r],
        [r, _whole(), _whole(), _whole()],
        [_sds((n, D_ATTN)), _sds((n, D_KV)), _sds((n, D_KV)), _sds((8, 128))],
    )(q, k, v, sink_cols, o, do)


S5_TABLES = ('pw_re', 'pw_im', 'dbl_re', 'dbl_im', 'seg_re', 'seg_im', 'segr_re', 'segr_im')
S5_WEIGHTS = ('b_re', 'b_im', 'c_re', 'c_im', 'd', 'glu_w', 'glu_b', 'perm', 'perm_t')


def _s5_states(u, carry_r, carry_i, b_re, b_im, tab, hr_s, hi_s):
    ub = _mx(u)
    hr_s[...] = _dot(ub, b_re[...])
    hi_s[...] = _dot(ub, b_im[...])
    return ub, _cscan(hr_s, hi_s, tab, carry_r, carry_i, reverse=False)


def _s5_fwd(u, prm):
    n = u.shape[0]
    nt = n // TILE

    def body(u_ref, *refs):
        tab = dict(zip(S5_TABLES, refs[:8]))
        b_re, b_im, c_re, c_im, d_ref, gw_ref, gb_ref, perm, perm_t = refs[8:17]
        y_ref, cr_out, ci_out, cr_s, ci_s, hr_s, hi_s = refs[17:]

        @pl.when(pl.program_id(0) == 0)
        def _():
            cr_s[...] = jnp.zeros_like(cr_s)
            ci_s[...] = jnp.zeros_like(ci_s)

        u = _permute_rows(perm[...], u_ref[...])
        cr, ci = cr_s[...], ci_s[...]
        cr_out[...] = jnp.broadcast_to(cr, (8, N_STATE))
        ci_out[...] = jnp.broadcast_to(ci, (8, N_STATE))
        _, (cr_s[...], ci_s[...]) = _s5_states(u, cr, ci, b_re, b_im, tab, hr_s, hi_s)
        y = _dot(_mx(hr_s[...]), c_re[...]) - _dot(_mx(hi_s[...]), c_im[...]) + d_ref[...] * u
        z = _gelu(y)
        y_ref[...] = _permute_rows(perm_t[...], z * _sigmoid(_dot(_mx(z), gw_ref[...]) + gb_ref[...]))

    r = functools.partial(_rows_spec, n_tiles=nt)
    return _call(
        body, "s5_fwd", nt,
        [r(TILE, D_S5)] + [_whole()] * 17,
        [r(TILE, D_S5), r(8, N_STATE), r(8, N_STATE)],
        [_sds((n, D_S5)), _sds((nt * 8, N_STATE)), _sds((nt * 8, N_STATE))],
        scratch=[pltpu.VMEM((1, N_STATE), F32)] * 2 + [pltpu.VMEM((TILE, N_STATE), F32)] * 2,
    )(u, *[prm[k] for k in S5_TABLES + S5_WEIGHTS])


def _s5_bwd(u, dys, carry_re, carry_im, prm):
    n = u.shape[0]
    nt = n // TILE

    def body(u_ref, dy_ref, cin_r, cin_i, *refs):
        tab = dict(zip(S5_TABLES, refs[:8]))
        b_re, b_im, c_re, c_im, d_ref, gw_ref, gb_ref, perm, perm_t = refs[8:17]
        du_ref, dbr_ref, dbi_ref, dcr_ref, dci_ref, dar_ref, dai_ref, dd_ref, dgw_ref, dgb_ref = refs[17:27]
        gr_s, gi_s, hr_s, hi_s, gr_t, gi_t = refs[27:]

        @pl.when(pl.program_id(0) == 0)
        def _():
            for ref in (dbr_ref, dbi_ref, dcr_ref, dci_ref, dar_ref, dai_ref, dd_ref, dgw_ref, dgb_ref, gr_s, gi_s):
                ref[...] = jnp.zeros_like(ref)

        u = _permute_rows(perm[...], u_ref[...])
        cr, ci = cin_r[0:1, :], cin_i[0:1, :]
        ub, _ = _s5_states(u, cr, ci, b_re, b_im, tab, hr_s, hi_s)
        hrb, hib = _mx(hr_s[...]), _mx(hi_s[...])
        y = _dot(hrb, c_re[...]) - _dot(hib, c_im[...]) + d_ref[...] * u
        z = _gelu(y)
        zb = _mx(z)
        sg = _sigmoid(_dot(zb, gw_ref[...]) + gb_ref[...])
        dout = _permute_rows(perm[...], dy_ref[...])
        dpre = dout * z * sg * (1.0 - sg)
        dgb_ref[...] += _sum0(dpre)
        dpb = _mx(dpre)
        dgw_ref[...] += _dot_tn(zb, dpb)
        dy = (dout * sg + _dot_nt(dpb, gw_ref[...])) * _gelu_grad(y)
        dd_ref[...] += _sum0(dy * u)
        dyb = _mx(dy)
        dcr_ref[...] += _dot_tn(hrb, dyb)
        dci_ref[...] -= _dot_tn(hib, dyb)
        gr_t[...] = _dot_nt(dyb, c_re[...])
        gi_t[...] = -_dot_nt(dyb, c_im[...])
        gr_s[...], gi_s[...] = _cscan(gr_t, gi_t, tab, gr_s[...], gi_s[...], reverse=True)
        sub = _row_iota((N_SEG, N_STATE))
        acc_r = acc_i = jnp.zeros((N_SEG, N_STATE), F32)
        for k in range(SEG):
            if k == 0:
                hpr = jnp.where(sub >= 1, pltpu.roll(hr_s[_seg_rows(SEG - 1), :], 1, 0), cr)
                hpi = jnp.where(sub >= 1, pltpu.roll(hi_s[_seg_rows(SEG - 1), :], 1, 0), ci)
            else:
                hpr, hpi = hr_s[_seg_rows(k - 1), :], hi_s[_seg_rows(k - 1), :]
            gr, gi = gr_t[_seg_rows(k), :], gi_t[_seg_rows(k), :]
            acc_r = acc_r + gr * hpr + gi * hpi
            acc_i = acc_i + gi * hpr - gr * hpi
        dar_ref[...] += _sum0(acc_r)
        dai_ref[...] += _sum0(acc_i)
        grb, gib = _mx(gr_t[...]), _mx(gi_t[...])
        dbr_ref[...] += _dot_tn(ub, grb)
        dbi_ref[...] += _dot_tn(ub, gib)
        du_ref[...] = _permute_rows(perm_t[...], dy * d_ref[...] + _dot_nt(grb, b_re[...]) + _dot_nt(gib, b_im[...]))

    r = functools.partial(_rows_spec, n_tiles=nt, reverse=True)
    return _call(
        body, "s5_bwd", nt,
        [r(TILE, D_S5), r(TILE, D_S5), r(8, N_STATE), r(8, N_STATE)] + [_whole()] * 17,
        [r(TILE, D_S5)] + [_whole()] * 9,
        [_sds((n, D_S5)), _sds((D_S5, N_STATE)), _sds((D_S5, N_STATE)), _sds((N_STATE, D_S5)), _sds((N_STATE, D_S5)),
         _sds((1, N_STATE)), _sds((1, N_STATE)), _sds((1, D_S5)), _sds((D_S5, D_S5)), _sds((1, D_S5))],
        scratch=[pltpu.VMEM((1, N_STATE), F32)] * 2 + [pltpu.VMEM((TILE, N_STATE), F32)] * 4,
    )(u, dys, carry_re, carry_im, *[prm[k] for k in S5_TABLES + S5_WEIGHTS])


def _lru_gates(xr, halo, tile_index, cw_ref, cb_ref, wx_ref, wa_ref, bx_ref, ba_ref, sp_ref):
    ext = jnp.concatenate([halo, xr], axis=0)
    sh = [xr] + [_shift_down(ext, j, TILE) for j in (1, 2, 3)]
    xc = cb_ref[...] + cw_ref[3:4, :] * sh[0] + cw_ref[2:3, :] * sh[1] + cw_ref[1:2, :] * sh[2] + cw_ref[0:1, :] * sh[3]
    xb = _mx(xc)
    gx = _sigmoid(_dot(xb, wx_ref[...]) + bx_ref[...])
    ga = _sigmoid(_dot(xb, wa_ref[...]) + ba_ref[...])
    la = -LRU_C * ga * sp_ref[...]
    a = jnp.exp(la)
    start = (tile_index * TILE + _row_iota(xr.shape)) == 0
    mult = jnp.where(start, 1.0, jnp.sqrt(-jnp.tanh(la) * (a * a + 1.0)))
    return sh, xc, xb, gx, ga, a, mult, start


def _lru_fwd(xr, gate, prm):
    n = xr.shape[0]
    nt = n // TILE

    def body(x_ref, g_ref, cw_ref, cb_ref, wx_ref, wa_ref, bx_ref, ba_ref, sp_ref, y_ref, c_out, halo_s, c_s):
        first_tile = pl.program_id(0) == 0

        @pl.when(first_tile)
        def _():
            halo_s[...] = jnp.zeros_like(halo_s)
            c_s[...] = jnp.zeros_like(c_s)

        xr = x_ref[...]
        _, xc, _, gx, _, a, mult, _ = _lru_gates(xr, halo_s[...], pl.program_id(0), cw_ref, cb_ref, wx_ref, wa_ref, bx_ref, ba_ref,
                                                 sp_ref)
        halo_s[...] = xr[TILE - 8:]
        acum, h = _rscan(a, mult * gx * xc, reverse=False)
        c = c_s[...]
        c_out[...] = jnp.broadcast_to(c, (8, D_LRU))
        h = h + acum * c
        c_s[...] = h[TILE - 1:TILE]
        y_ref[...] = h * _gelu(g_ref[...])

    r = functools.partial(_rows_spec, n_tiles=nt)
    return _call(
        body, "lru_fwd", nt,
        [r(TILE, D_LRU), r(TILE, D_LRU)] + [_whole()] * 7,
        [r(TILE, D_LRU), r(8, D_LRU)],
        [_sds((n, D_LRU)), _sds((nt * 8, D_LRU))],
        scratch=[pltpu.VMEM((8, D_LRU), F32), pltpu.VMEM((1, D_LRU), F32)],
    )(xr, gate, prm['conv_w'], prm['conv_b'], prm['wx'], prm['wa'], prm['bx'], prm['ba'], prm['sp'])


def _lru_bwd(xr, gate, dyl, carry, prm):
    n = xr.shape[0]
    nt = n // TILE

    def body(x_ref, xh_ref, g_ref, dy_ref, cin_ref, cw_ref, cb_ref, wx_ref, wa_ref, bx_ref, ba_ref, sp_ref,
             dx_ref, dg_ref, dcw0, dcw1, dcw2, dcw3, dcb_ref, dwx_ref, dwa_ref, dbx_ref, dba_ref, dsp_ref, an_s, gn_s, dn_s):
        first_tile = pl.program_id(0) == nt - 1

        @pl.when(pl.program_id(0) == 0)
        def _():
            for ref in (dcw0, dcw1, dcw2, dcw3, dcb_ref, dwx_ref, dwa_ref, dbx_ref, dba_ref, dsp_ref, gn_s, dn_s):
                ref[...] = jnp.zeros_like(ref)
            an_s[...] = jnp.ones_like(an_s)

        xr = x_ref[...]
        halo = jnp.where(first_tile, 0.0, xh_ref[...])
        sh, xc, xb, gx, ga, a, mult, start = _lru_gates(xr, halo, nt - 1 - pl.program_id(0), cw_ref, cb_ref, wx_ref, wa_ref, bx_ref,
                                                        ba_ref, sp_ref)
        acum, h = _rscan(a, mult * gx * xc, reverse=False)
        cin = cin_ref[0:1, :]
        h = h + acum * cin
        gate = g_ref[...]
        dyl = dy_ref[...]
        dg_ref[...] = dyl * h * _gelu_grad(gate)
        row = _row_iota(xr.shape)
        alpha = jnp.where(row < TILE - 1, pltpu.roll(a, TILE - 1, 0), an_s[...])
        racc, g = _rscan(alpha, dyl * _gelu(gate), reverse=True)
        g = g + racc * gn_s[...]
        an_s[...] = a[0:1]
        gn_s[...] = g[0:1]
        hprev = jnp.where(row == 0, cin, pltpu.roll(h, 1, 0))
        da = g * hprev
        dmult = jnp.where(start, 0.0, g * gx * xc)
        dla = da * a - dmult * a * a / mult
        dsp_ref[...] += _sum0(-LRU_C * ga * dla)
        dpa = (-LRU_C * sp_ref[...] * dla) * ga * (1.0 - ga)
        dpx = (g * mult * xc) * gx * (1.0 - gx)
        dba_ref[...] += _sum0(dpa)
        dbx_ref[...] += _sum0(dpx)
        dpab, dpxb = _mx(dpa), _mx(dpx)
        dwa_ref[...] += _dot_tn(xb, dpab)
        dwx_ref[...] += _dot_tn(xb, dpxb)
        dxc = g * mult * gx + _dot_nt(dpab, wa_ref[...]) + _dot_nt(dpxb, wx_ref[...])
        dcb_ref[...] += _sum0(dxc)
        dcw3[...] += _sum0(dxc * sh[0])
        dcw2[...] += _sum0(dxc * sh[1])
        dcw1[...] += _sum0(dxc * sh[2])
        dcw0[...] += _sum0(dxc * sh[3])
        ext = jnp.concatenate([dxc, dn_s[...]], axis=0)
        dx_ref[...] = (cw_ref[3:4, :] * dxc + cw_ref[2:3, :] * _shift_up(ext, 1, TILE) + cw_ref[1:2, :] * _shift_up(ext, 2, TILE)
                       + cw_ref[0:1, :] * _shift_up(ext, 3, TILE))
        dn_s[...] = dxc[:8]

    r = functools.partial(_rows_spec, n_tiles=nt, reverse=True)
    vec = _sds((1, D_LRU))
    return _call(
        body, "lru_bwd", nt,
        [r(TILE, D_LRU), _halo_spec(D_LRU, TILE, nt, reverse=True), r(TILE, D_LRU), r(TILE, D_LRU), r(8, D_LRU)] + [_whole()] * 7,
        [r(TILE, D_LRU), r(TILE, D_LRU)] + [_whole()] * 10,
        [_sds((n, D_LRU)), _sds((n, D_LRU)), vec, vec, vec, vec, vec, _sds((D_LRU, D_LRU)), _sds((D_LRU, D_LRU)), vec, vec, vec],
        scratch=[pltpu.VMEM((1, D_LRU), F32), pltpu.VMEM((1, D_LRU), F32), pltpu.VMEM((8, D_LRU), F32)],
    )(xr, xr, gate, dyl, carry, prm['conv_w'], prm['conv_b'], prm['wx'], prm['wa'], prm['bx'], prm['ba'], prm['sp'])


def _normed_parts(ya, ys, yl):
    return jnp.concatenate([ya * _rms(ya), ys * _rms(ys), yl * _rms(yl)], axis=1)


def _mixout_fwd(ya, ys, yl, x0, g_mix, w_out, b_out, g1, b1):
    n = x0.shape[0]
    nt = n // TILE_WIDE

    def body(ya_ref, ys_ref, yl_ref, x_ref, gm_ref, w_ref, b_ref, g_ref, be_ref, mix_ref, r_ref, x1_ref):
        mixb = _mx(_normed_parts(ya_ref[...], ys_ref[...], yl_ref[...]) * gm_ref[...])
        mix_ref[...] = mixb
        r1 = ALPHA * x_ref[...] + _dot(mixb, w_ref[...]) + b_ref[...]
        r_ref[...] = r1
        xhat, _ = _ln_stats(r1)
        x1_ref[...] = xhat * g_ref[...] + be_ref[...]

    r = functools.partial(_rows_spec, n_tiles=nt)
    return _call(
        body, "mixout_fwd", nt,
        [r(TILE_WIDE, D_ATTN), r(TILE_WIDE, D_S5), r(TILE_WIDE, D_LRU), r(TILE_WIDE, D)] + [_whole()] * 5,
        [r(TILE_WIDE, D), r(TILE_WIDE, D), r(TILE_WIDE, D)],
        [_sds((n, D), MXU), _sds((n, D)), _sds((n, D))],
    )(ya, ys, yl, x0, g_mix, w_out, b_out, g1, b1)


def _mixout_bwd(dr1, mix, ya, ys, yl, g_mix, w_out):
    n = dr1.shape[0]
    nt = n // TILE_WIDE

    def body(dr_ref, mix_ref, ya_ref, ys_ref, yl_ref, gm_ref, w_ref, dya_ref, dys_ref, dyl_ref, dw_ref, db_ref, dgm_ref):
        @pl.when(pl.program_id(0) == 0)
        def _():
            for ref in (dw_ref, db_ref, dgm_ref):
                ref[...] = jnp.zeros_like(ref)

        dr = dr_ref[...]
        db_ref[...] += _sum0(dr)
        drb = _mx(dr)
        dw_ref[...] += _dot_tn(mix_ref[...], drb)
        dmix = _dot(drb, w_ref[...])
        parts = (ya_ref[...], ys_ref[...], yl_ref[...])
        dgm_ref[...] += _sum0(dmix * _normed_parts(*parts))
        dn = dmix * gm_ref[...]
        lo = 0
        for y, out in zip(parts, (dya_ref, dys_ref, dyl_ref)):
            w = y.shape[1]
            rs = _rms(y)
            nrm = y * rs
            dnp = dn[:, lo:lo + w]
            out[...] = rs * (dnp - nrm * jnp.mean(dnp * nrm, axis=-1, keepdims=True))
            lo += w

    r = functools.partial(_rows_spec, n_tiles=nt)
    return _call(
        body, "mixout_bwd", nt,
        [r(TILE_WIDE, D), r(TILE_WIDE, D), r(TILE_WIDE, D_ATTN), r(TILE_WIDE, D_S5), r(TILE_WIDE, D_LRU), _whole(), _whole()],
        [r(TILE_WIDE, D_ATTN), r(TILE_WIDE, D_S5), r(TILE_WIDE, D_LRU), _whole(), _whole(), _whole()],
        [_sds((n, D_ATTN)), _sds((n, D_S5)), _sds((n, D_LRU)), _sds((D, D)), _sds((1, D)), _sds((1, D))],
    )(dr1, mix, ya, ys, yl, g_mix, w_out)


def _ffn_conv(gp, halo, cw_ref, cb_ref, cs):
    ext = jnp.concatenate([halo, gp], axis=0)
    s1 = _shift_down(ext, 1, TILE)
    s2 = _shift_down(ext, 2, TILE)
    return s1, s2, cb_ref[:, cs] + cw_ref[2:3, cs] * gp + cw_ref[1:2, cs] * s1 + cw_ref[0:1, cs] * s2


def _ffn_fwd(x1, wg, wu, cw, cb, wd, g2, b2):
    n = x1.shape[0]
    nt = n // TILE

    def body(x_ref, wg_ref, wu_ref, cw_ref, cb_ref, wd_ref, g_ref, be_ref, gp_ref, up_ref, r_ref, x2_ref, halo_s, act_s):
        @pl.when(pl.program_id(0) == 0)
        def _():
            halo_s[...] = jnp.zeros_like(halo_s)

        x1 = x_ref[...]
        xb = _mx(x1)
        for c in range(D_FF // FF_CHUNK):
            cs = slice(c * FF_CHUNK, (c + 1) * FF_CHUNK)
            gp = _dot(xb, wg_ref[:, cs])
            up = _dot(xb, wu_ref[:, cs])
            gp_ref[:, cs] = gp
            up_ref[:, cs] = up
            _, _, gc = _ffn_conv(gp, halo_s[:, cs], cw_ref, cb_ref, cs)
            halo_s[:, cs] = gp[TILE - 8:]
            act_s[:, cs] = _mx(gc * _sigmoid(gc) * up)
        r2 = ALPHA * x1 + _dot(act_s[...], wd_ref[...])
        r_ref[...] = r2
        xhat, _ = _ln_stats(r2)
        x2_ref[...] = xhat * g_ref[...] + be_ref[...]

    r = functools.partial(_rows_spec, n_tiles=nt)
    return _call(
        body, "ffn_fwd", nt,
        [r(TILE, D)] + [_whole()] * 7,
        [r(TILE, D_FF), r(TILE, D_FF), r(TILE, D), r(TILE, D)],
        [_sds((n, D_FF)), _sds((n, D_FF)), _sds((n, D)), _sds((n, D))],
        scratch=[pltpu.VMEM((8, D_FF), F32), pltpu.VMEM((TILE, D_FF), MXU)],
    )(x1, wg, wu, cw, cb, wd, g2, b2)


def _ffn_bwd_down(dx2, r2, g2, gp, up, cw, cb, wd_t):
    n = dx2.shape[0]
    nt = n // TILE

    def body(dx_ref, r_ref, g_ref, gp_ref, gh_ref, up_ref, cw_ref, cb_ref, wd_ref,
             dr_ref, dgp_ref, dup_ref, dwd_ref, dcw0, dcw1, dcw2, dcb_ref, dg_ref, db_ref, next_s):
        first_tile = pl.program_id(0) == nt - 1

        @pl.when(pl.program_id(0) == 0)
        def _():
            for ref in (dwd_ref, dcw0, dcw1, dcw2, dcb_ref, dg_ref, db_ref, next_s):
                ref[...] = jnp.zeros_like(ref)

        dx2 = dx_ref[...]
        xhat, rstd = _ln_stats(r_ref[...])
        dg_ref[...] += _sum0(dx2 * xhat)
        db_ref[...] += _sum0(dx2)
        dr2 = _ln_bwd(dx2, g_ref[...], xhat, rstd)
        dr_ref[...] = dr2
        dfb = _mx(dr2)
        for c in range(D_FF // FF_CHUNK):
            cs = slice(c * FF_CHUNK, (c + 1) * FF_CHUNK)
            gp = gp_ref[:, cs]
            up = up_ref[:, cs]
            s1, s2, gc = _ffn_conv(gp, jnp.where(first_tile, 0.0, gh_ref[:, cs]), cw_ref, cb_ref, cs)
            sg = _sigmoid(gc)
            silu = gc * sg
            dact = _dot(dfb, wd_ref[:, cs])
            dwd_ref[cs, :] += _dot_tn(_mx(silu * up), dfb)
            dup_ref[:, cs] = _mx(dact * silu)
            dgc = dact * up * (sg * (1.0 + gc * (1.0 - sg)))
            dcb_ref[:, cs] += _sum0(dgc)
            dcw2[:, cs] += _sum0(dgc * gp)
            dcw1[:, cs] += _sum0(dgc * s1)
            dcw0[:, cs] += _sum0(dgc * s2)
            ext = jnp.concatenate([dgc, next_s[:, cs]], axis=0)
            dgp_ref[:, cs] = _mx(cw_ref[2:3, cs] * dgc + cw_ref[1:2, cs] * _shift_up(ext, 1, TILE)
                                 + cw_ref[0:1, cs] * _shift_up(ext, 2, TILE))
            next_s[:, cs] = dgc[:8]

    r = functools.partial(_rows_spec, n_tiles=nt, reverse=True)
    vff = _sds((1, D_FF))
    return _call(
        body, "ffn_bwd_down", nt,
        [r(TILE, D), r(TILE, D), _whole(), r(TILE, D_FF), _halo_spec(D_FF, TILE, nt, reverse=True), r(TILE, D_FF), _whole(), _whole(),
         _whole()],
        [r(TILE, D), r(TILE, D_FF), r(TILE, D_FF)] + [_whole()] * 7,
        [_sds((n, D)), _sds((n, D_FF), MXU), _sds((n, D_FF), MXU), _sds((D_FF, D)), vff, vff, vff, vff, _sds((1, D)), _sds((1, D))],
        scratch=[pltpu.VMEM((8, D_FF), F32)],
    )(dx2, r2, g2, gp, gp, up, cw, cb, wd_t)


def _ffn_bwd_dx(dr2, dgp, dup, r1, g1, wg_t, wu_t):
    n = dr2.shape[0]
    rows = TILE_BIG
    nt = n // rows

    def body(dr2_ref, dgp_ref, dup_ref, r_ref, g_ref, wg_ref, wu_ref, dr1_ref, dg_ref, db_ref):
        @pl.when(pl.program_id(0) == 0)
        def _():
            for ref in (dg_ref, db_ref):
                ref[...] = jnp.zeros_like(ref)

        dx1 = ALPHA * dr2_ref[...] + _dot(dgp_ref[...], wg_ref[...]) + _dot(dup_ref[...], wu_ref[...])
        xhat, rstd = _ln_stats(r_ref[...])
        dg_ref[...] += _sum0(dx1 * xhat)
        db_ref[...] += _sum0(dx1)
        dr1_ref[...] = _ln_bwd(dx1, g_ref[...], xhat, rstd)

    r = functools.partial(_rows_spec, n_tiles=nt)
    return _call(
        body, "ffn_bwd_dx", nt,
        [r(rows, D), r(rows, D_FF), r(rows, D_FF), r(rows, D), _whole(), _whole(), _whole()],
        [r(rows, D), _whole(), _whole()],
        [_sds((n, D)), _sds((1, D)), _sds((1, D))],
    )(dr2, dgp, dup, r1, g1, wg_t, wu_t)


def _ffn_bwd_dw(x1, dgp, dup):
    n = x1.shape[0]
    rows = TILE_BIG
    nt = n // rows

    def body(x_ref, dgp_ref, dup_ref, dwg_ref, dwu_ref):
        @pl.when(pl.program_id(0) == 0)
        def _():
            for ref in (dwg_ref, dwu_ref):
                ref[...] = jnp.zeros_like(ref)

        xb = _mx(x_ref[...])
        for c in range(D_FF // FF_CHUNK):
            cs = slice(c * FF_CHUNK, (c + 1) * FF_CHUNK)
            dwg_ref[cs, :] += _dot_tn(dgp_ref[:, cs], xb)
            dwu_ref[cs, :] += _dot_tn(dup_ref[:, cs], xb)

    r = functools.partial(_rows_spec, n_tiles=nt)
    return _call(
        body, "ffn_bwd_dw", nt,
        [r(rows, D), r(rows, D_FF), r(rows, D_FF)], [_whole(), _whole()], [_sds((D_FF, D)), _sds((D_FF, D))],
    )(x1, dgp, dup)


def _loss_head(y, target):
    n = y.shape[0]
    nt = n // TILE_WIDE

    def body(y_ref, t_ref, loss_ref, dy_ref):
        @pl.when(pl.program_id(0) == 0)
        def _():
            loss_ref[...] = jnp.zeros_like(loss_ref)

        e = y_ref[...] - t_ref[...]
        dy_ref[...] = e * (1.0 / D)
        loss_ref[...] += _sum0(jnp.sum(e * e, axis=1, keepdims=True)) * (0.5 / D)

    r = functools.partial(_rows_spec, n_tiles=nt)
    return _call(body, "loss_head", nt, [r(TILE_WIDE, D), r(TILE_WIDE, D)], [_whole(), r(TILE_WIDE, D)],
                 [_sds((1, 1)), _sds((n, D))])(y, target)


def _place():
    x, y, c = lax.axis_index("x"), lax.axis_index("y"), lax.axis_index("c")
    return x, y, c, 4 * x + 2 * y + c


def _peer(x, y, c, k):
    px, py, pc = x ^ ((k >> 2) & 1), y ^ ((k >> 1) & 1), c ^ (k & 1)
    return (px, py, pc), 4 * px + 2 * py + pc


def _all_gather(blocks, small):
    srcs = list(blocks) + [small]
    n = len(srcs)
    out_shapes = [_sds((a.shape[0], N_DEV * a.shape[1], LANES), a.dtype) for a in blocks] + [_sds((N_DEV,) + small.shape, small.dtype)]

    def body(*refs):
        src_refs, out_refs = refs[:n], refs[n:2 * n]
        send_sems, recv_sems, local_sems = refs[2 * n:]
        x, y, c, me = _place()

        def landing(a, slot):
            if a == n - 1:
                return out_refs[a].at[slot]
            r = src_refs[a].shape[1]
            return out_refs[a].at[:, pl.ds(slot * r, r), :]

        def remote(a, k, slot):
            peer, _ = _peer(x, y, c, k)
            return pltpu.make_async_remote_copy(
                src_ref=src_refs[a], dst_ref=landing(a, slot), send_sem=send_sems.at[a * N_DEV + k],
                recv_sem=recv_sems.at[a * N_DEV + k], device_id=peer, device_id_type=pl.DeviceIdType.MESH)

        mine = [pltpu.make_async_copy(src_refs[a], landing(a, me), local_sems.at[a]) for a in range(n)]
        sends = [remote(a, k, me) for a in range(n) for k in range(1, N_DEV)]
        for cp in mine + sends:
            cp.start()
        for a in range(n):
            for k in range(1, N_DEV):
                remote(a, k, _peer(x, y, c, k)[1]).wait_recv()
        for cp in sends:
            cp.wait_send()
        for cp in mine:
            cp.wait()

    any_space = pl.BlockSpec(memory_space=pl.ANY)
    return pl.pallas_call(
        body, name="gather_weights", out_shape=out_shapes, in_specs=[any_space] * n, out_specs=[any_space] * n,
        scratch_shapes=[pltpu.SemaphoreType.DMA((n * N_DEV,)), pltpu.SemaphoreType.DMA((n * N_DEV,)), pltpu.SemaphoreType.DMA((n,))],
    )(*srcs)


def _exchange_grads(big, small):
    nb = len(big)
    rows_a = [a.shape[0] // N_DEV for a in big] + [small.shape[1]]
    offs = [sum(rows_a[:a]) for a in range(nb + 1)]
    total = sum(rows_a)

    def body(*refs):
        src_refs, out_ref = refs[:nb + 1], refs[nb + 1]
        send_sems, recv_sems, local_sem = refs[nb + 2:]
        x, y, c, me = _place()

        def pieces(slot):
            out = []
            for a in range(nb + 1):
                src = src_refs[a].at[slot] if a == nb else src_refs[a].at[pl.ds(slot * rows_a[a], rows_a[a]), :]
                out.append((src, out_ref.at[me, pl.ds(offs[a], rows_a[a]), :]))
            return out

        for src, dst in pieces(me):
            pltpu.make_async_copy(src, dst, local_sem).start()
        for k in range(1, N_DEV):
            peer, peer_slot = _peer(x, y, c, k)
            for src, dst in pieces(peer_slot):
                pltpu.make_async_remote_copy(src_ref=src, dst_ref=dst, send_sem=send_sems.at[k], recv_sem=recv_sems.at[k],
                                             device_id=peer, device_id_type=pl.DeviceIdType.MESH).start()
        slots = []
        for k in range(1, N_DEV):
            peer, peer_slot = _peer(x, y, c, k)
            slots.append(pltpu.make_async_remote_copy(
                src_ref=out_ref.at[me], dst_ref=out_ref.at[peer_slot], send_sem=send_sems.at[k], recv_sem=recv_sems.at[k],
                device_id=peer, device_id_type=pl.DeviceIdType.MESH))
        for cp in slots:
            cp.wait_recv()
        for cp in slots:
            cp.wait_send()
        pltpu.make_async_copy(out_ref.at[me], out_ref.at[me], local_sem).wait()

    any_space = pl.BlockSpec(memory_space=pl.ANY)
    return pl.pallas_call(
        body, name="exchange_grads", out_shape=_sds((N_DEV, total, LANES)), in_specs=[any_space] * (nb + 1), out_specs=any_space,
        scratch_shapes=[pltpu.SemaphoreType.DMA((N_DEV,)), pltpu.SemaphoreType.DMA((N_DEV,)), pltpu.SemaphoreType.DMA(())],
    )(*big, small)


_HBM = pl.BlockSpec(memory_space=pltpu.HBM)
_SEM = pl.BlockSpec(memory_space=pltpu.SEMAPHORE)
_EFFECT = pltpu.SideEffectType.DATAFLOW_SIDE_EFFECTING


def _in_hbm(a):
    return pltpu.with_memory_space_constraint(a, pltpu.HBM)


def _scatter_start(name, srcs):
    ns = len(srcs)
    rows_a = [a.shape[0] // N_DEV for a in srcs]
    offs = [sum(rows_a[:a]) for a in range(ns)]
    total = sum(rows_a)

    def body(*refs):
        src_refs, land_ref, send_sems, recv_sems, token = refs[:ns], refs[ns], refs[ns + 1], refs[ns + 2], refs[-1]
        x, y, c, me = _place()
        for k in range(1, N_DEV):
            peer, peer_slot = _peer(x, y, c, k)
            for a in range(ns):
                pltpu.make_async_remote_copy(
                    src_ref=src_refs[a].at[pl.ds(peer_slot * rows_a[a], rows_a[a]), :],
                    dst_ref=land_ref.at[me, pl.ds(offs[a], rows_a[a]), :], send_sem=send_sems.at[k], recv_sem=recv_sems.at[k],
                    device_id=peer, device_id_type=pl.DeviceIdType.MESH).start()
        token[...] = jnp.zeros_like(token)

    landing = lax.empty((N_DEV, total, LANES), F32)
    out = pl.pallas_call(
        body, name=name,
        out_shape=(pltpu.SemaphoreType.DMA((N_DEV,)), pltpu.SemaphoreType.DMA((N_DEV,)), *[pltpu.HBM(a.shape, a.dtype) for a in srcs],
                   pltpu.HBM(landing.shape, F32), _sds((8, 128))),
        in_specs=[_HBM] * (ns + 1), out_specs=(_SEM, _SEM, *[_HBM] * (ns + 1), pl.BlockSpec(memory_space=pltpu.VMEM)),
        input_output_aliases={a: 2 + a for a in range(ns + 1)},
        compiler_params=pltpu.CompilerParams(has_side_effects=_EFFECT),
    )(*[_in_hbm(a) for a in srcs], _in_hbm(landing))
    return out[0], out[1], out[2:2 + ns], out[2 + ns], out[-1]


def _scatter_wait(name, send_sems, recv_sems, srcs, landing, after):
    ns = len(srcs)

    def body(*refs):
        land_ref, send_ref, recv_ref = refs[ns], refs[ns + 1], refs[ns + 2]
        x, y, c, me = _place()
        for k in range(1, N_DEV):
            peer, peer_slot = _peer(x, y, c, k)
            slot = pltpu.make_async_remote_copy(
                src_ref=land_ref.at[me], dst_ref=land_ref.at[peer_slot], send_sem=send_ref.at[k], recv_sem=recv_ref.at[k],
                device_id=peer, device_id_type=pl.DeviceIdType.MESH)
            slot.wait_send()
            slot.wait_recv()

    out = pl.pallas_call(
        body, name=name, out_shape=(*[pltpu.HBM(a.shape, a.dtype) for a in srcs], pltpu.HBM(landing.shape, landing.dtype)),
        in_specs=[_HBM] * (ns + 1) + [_SEM, _SEM, pl.BlockSpec(memory_space=pl.ANY)], out_specs=[_HBM] * (ns + 1),
        input_output_aliases={a: a for a in range(ns + 1)},
        compiler_params=pltpu.CompilerParams(has_side_effects=_EFFECT),
    )(*srcs, landing, send_sems, recv_sems, after)
    return out[:ns], out[ns]


def _gather_start(name, blocks):
    n = len(blocks)

    def body(*refs):
        src_refs, land_refs, send_sems, recv_sems, token = refs[:n], refs[n:2 * n], refs[2 * n], refs[2 * n + 1], refs[-1]
        x, y, c, me = _place()
        for a in range(n):
            r = src_refs[a].shape[1]
            for k in range(1, N_DEV):
                pltpu.make_async_remote_copy(
                    src_ref=src_refs[a], dst_ref=land_refs[a].at[:, pl.ds(me * r, r), :], send_sem=send_sems.at[a * N_DEV + k],
                    recv_sem=recv_sems.at[a * N_DEV + k], device_id=_peer(x, y, c, k)[0], device_id_type=pl.DeviceIdType.MESH).start()
        token[...] = jnp.zeros_like(token)

    wholes = [lax.empty((a.shape[0], N_DEV * a.shape[1], LANES), a.dtype) for a in blocks]
    out = pl.pallas_call(
        body, name=name,
        out_shape=(pltpu.SemaphoreType.DMA((n * N_DEV,)), pltpu.SemaphoreType.DMA((n * N_DEV,)),
                   *[pltpu.HBM(a.shape, a.dtype) for a in blocks + wholes], _sds((8, 128))),
        in_specs=[_HBM] * (2 * n), out_specs=(_SEM, _SEM, *[_HBM] * (2 * n), pl.BlockSpec(memory_space=pltpu.VMEM)),
        input_output_aliases={a: 2 + a for a in range(2 * n)},
        compiler_params=pltpu.CompilerParams(has_side_effects=_EFFECT),
    )(*[_in_hbm(a) for a in blocks + wholes])
    return out[0], out[1], out[2:2 + n], out[2 + n:2 + 2 * n], out[-1]


def _gather_wait(name, send_sems, recv_sems, blocks, wholes, after):
    n = len(blocks)

    def body(*refs):
        src_refs, land_refs, send_ref, recv_ref = refs[:n], refs[n:2 * n], refs[2 * n], refs[2 * n + 1]
        x, y, c, me = _place()
        for a in range(n):
            r = src_refs[a].shape[1]
            for k in range(1, N_DEV):
                peer, peer_slot = _peer(x, y, c, k)
                cp = pltpu.make_async_remote_copy(
                    src_ref=src_refs[a], dst_ref=land_refs[a].at[:, pl.ds(peer_slot * r, r), :], send_sem=send_ref.at[a * N_DEV + k],
                    recv_sem=recv_ref.at[a * N_DEV + k], device_id=peer, device_id_type=pl.DeviceIdType.MESH)
                cp.wait_send()
                cp.wait_recv()

    return pl.pallas_call(
        body, name=name, out_shape=tuple(pltpu.HBM(a.shape, a.dtype) for a in list(blocks) + list(wholes)),
        in_specs=[_HBM] * (2 * n) + [_SEM, _SEM, pl.BlockSpec(memory_space=pl.ANY)], out_specs=[_HBM] * (2 * n),
        input_output_aliases={a: a for a in range(2 * n)},
        compiler_params=pltpu.CompilerParams(has_side_effects=_EFFECT),
    )(*blocks, *wholes, send_sems, recv_sems, after)


def _reduce_adamw(parts, w, m, v, tile_rows=PACK_TILE):
    rows = w.shape[0]
    nt = rows // tile_rows
    c1 = 1.0 - ADAM_B1 ** ADAM_STEP
    c2 = 1.0 - ADAM_B2 ** ADAM_STEP

    def body(p_ref, w_ref, m_ref, v_ref, g_out, d_out, m_out, v_out):
        g = p_ref[0]
        for s in range(1, N_DEV):
            g = g + p_ref[s]
        m_new = ADAM_B1 * m_ref[...] + (1.0 - ADAM_B1) * g
        v_new = ADAM_B2 * v_ref[...] + (1.0 - ADAM_B2) * (g * g)
        g_out[...] = g
        m_out[...] = m_new
        v_out[...] = v_new
        d_out[...] = -ADAM_LR * ((m_new / c1) / (jnp.sqrt(v_new / c2) + ADAM_EPS) + ADAM_WD * w_ref[...])

    r = _rows_spec(tile_rows, LANES, nt)
    out = _sds((rows, LANES))
    return _call(
        body, "reduce_adamw", nt,
        [pl.BlockSpec((N_DEV, tile_rows, LANES), lambda i: (0, i, 0)), r, r, r], [r, r, r, r], [out, out, out, out],
    )(parts, w, m, v)


def _pack_rows(a, lead=0):
    head = a.shape[:lead]
    flat = a.reshape(head + (-1,))
    size = flat.shape[-1]
    rows = -(-size // (16 * LANES)) * 16
    flat = jnp.pad(flat, [(0, 0)] * lead + [(0, rows * LANES - size)])
    return flat.reshape(head + (rows, LANES))


def _packed_rows(shape):
    return -(-math.prod(shape) // (16 * LANES)) * 16


def _to_blocks(full, axis):
    l, a, b = full.shape
    if axis == 2:
        return full.reshape(l, a, N_DEV, b // N_DEV).transpose(2, 0, 1, 3)
    return full.reshape(l, N_DEV, a // N_DEV, b).transpose(1, 0, 2, 3)


def _from_blocks(blocks, axis):
    _, l, a, b = blocks.shape
    if axis == 2:
        return blocks.transpose(1, 2, 0, 3).reshape(l, a, N_DEV * b)
    return blocks.transpose(1, 0, 2, 3).reshape(l, N_DEV * a, b)


def _row_form(shard, transposed):
    return shard.transpose(0, 2, 1) if transposed else shard


def _me():
    return 4 * lax.axis_index("x") + 2 * lax.axis_index("y") + lax.axis_index("c")


def _both_forms(names, wholes):
    out = {}
    for name, w in zip(names, wholes):
        t = dict(BIG)[name]
        out[name + '_t' if t else name] = w
        out[name if t else name + '_t'] = w.transpose(0, 2, 1)
    return out


def _gather_weights(local):
    segs, meta = [], []
    for name in SMALL_SHARDED:
        blk = local[name]
        if name in GATHER_F32:
            bits = lax.bitcast_convert_type(blk, MXU)
        else:
            bits = _mx(blk)
        seg = _pack_rows(bits)
        meta.append((name, bits.shape, seg.shape[0]))
        segs.append(seg)
    blocks = {name: _mx(_row_form(local[name], t)) for name, t in BIG}
    mix, ffn = PARTS['mix'], PARTS['ffn']
    *first, gathered = _all_gather([blocks[n][:1] for n in mix], jnp.concatenate(segs, axis=0))
    flights = {'ffn0': _gather_start("gather_ffn0_start", [blocks[n][:1] for n in ffn]),
               'later': _gather_start("gather_later_start", [blocks[n][1:] for n in mix + ffn])}
    out, lo = {}, 0
    for name, bits_shape, rows in meta:
        seg = gathered[:, lo:lo + rows].reshape(N_DEV, -1)[:, :math.prod(bits_shape)].reshape((N_DEV,) + bits_shape)
        if name in GATHER_F32:
            seg = lax.bitcast_convert_type(seg, F32)
        out[name] = _from_blocks(seg, SHARD_AXIS[name])
        lo += rows
    ready = {(0, 'mix'): {k: v[0] for k, v in _both_forms(mix, first).items()}}

    def landed(flight, names, after):
        send_sems, recv_sems, mine, wholes, _ = flights[flight]
        done = _gather_wait(f"gather_{flight}_wait", send_sems, recv_sems, mine, wholes, after)
        own = [lax.dynamic_update_slice(w, b, (0, _me() * b.shape[1], 0)) for b, w in zip(done[:len(names)], done[len(names):])]
        return _both_forms(names, own)

    def big_weights(l, part, after):
        if (l, part) not in ready and l == 0:
            ready[(0, 'ffn')] = {k: v[0] for k, v in landed('ffn0', ffn, after).items()}
        elif (l, part) not in ready:
            forms = landed('later', mix + ffn, after)
            for j in range(1, DEPTH):
                for p, names in PARTS.items():
                    ready[(j, p)] = {k: forms[k][j - 1] for n in names for k in (n, n + '_t')}
        return ready[(l, part)]

    return out, big_weights, flights['ffn0'][-1][0, 0] + flights['later'][-1][0, 0]


def _s5_discretize(a_re, a_im, log_dt, b_re, b_im):
    lam_re = jnp.minimum(a_re, -1e-4)
    lam_im = a_im
    dt = jnp.exp(log_dt)[:, None]
    decay = jnp.exp(dt * lam_re)
    ang = dt * lam_im
    abar_re = decay * jnp.cos(ang)
    abar_im = decay * jnp.sin(ang)
    den = jnp.square(lam_re) + jnp.square(lam_im)
    nr = abar_re - 1.0
    ni = abar_im
    coef_re = (nr * lam_re + ni * lam_im) / den
    coef_im = (ni * lam_re - nr * lam_im) / den
    bbar_re = coef_re[..., None] * b_re - coef_im[..., None] * b_im
    bbar_im = coef_re[..., None] * b_im + coef_im[..., None] * b_re
    return abar_re, abar_im, bbar_re, bbar_im


def _complex_powers(ar, ai, count):
    def combine(e1, e2):
        return e2[0] * e1[0] - e2[1] * e1[1], e2[0] * e1[1] + e2[1] * e1[0]

    shape = (count,) + ar.shape
    return lax.associative_scan(combine, (jnp.broadcast_to(ar, shape), jnp.broadcast_to(ai, shape)), axis=0)


_EYE16 = functools.partial(jnp.eye, 16, dtype=F32)


def _s5_params(p, l):
    disc, disc_vjp = jax.vjp(_s5_discretize, p['s5_a_re'][l], p['s5_a_im'][l], p['s5_log_dt'][l], p['s5_b_re'][l], p['s5_b_im'][l])
    abar_re, abar_im, bbar_re, bbar_im = disc
    ar, ai = abar_re.reshape(N_STATE), abar_im.reshape(N_STATE)
    ap_re, ap_im = _complex_powers(ar, ai, TILE)
    one, zero = jnp.ones((1, N_STATE), F32), jnp.zeros((1, N_STATE), F32)
    seg_re = jnp.concatenate([one, ap_re[SEG - 1:TILE - SEG:SEG]], axis=0)
    seg_im = jnp.concatenate([zero, ap_im[SEG - 1:TILE - SEG:SEG]], axis=0)
    doubling = [SEG - 1, 2 * SEG - 1, 4 * SEG - 1]

    src = (jnp.arange(TILE) % N_SEG) * SEG + jnp.arange(TILE) // N_SEG
    perm = (src[:, None] == jnp.arange(TILE)[None, :]).astype(MXU)
    prm = {
        'perm': perm, 'perm_t': perm.T,
        'pw_re': ap_re[:SEG], 'pw_im': ap_im[:SEG],
        'dbl_re': jnp.concatenate([jnp.stack([ap_re[k] for k in doubling]), jnp.zeros((5, N_STATE), F32)], axis=0),
        'dbl_im': jnp.concatenate([jnp.stack([ap_im[k] for k in doubling]), jnp.zeros((5, N_STATE), F32)], axis=0),
        'seg_re': seg_re, 'seg_im': seg_im, 'segr_re': seg_re[::-1], 'segr_im': seg_im[::-1],
        'b_re': _mx(jnp.einsum('gpc,gh->gchp', bbar_re, _EYE16()).reshape(D_S5, N_STATE)),
        'b_im': _mx(jnp.einsum('gpc,gh->gchp', bbar_im, _EYE16()).reshape(D_S5, N_STATE)),
        'c_re': _mx(jnp.einsum('gcp,gh->gphc', p['s5_c_re'][l], _EYE16()).reshape(N_STATE, D_S5)),
        'c_im': _mx(jnp.einsum('gcp,gh->gphc', p['s5_c_im'][l], _EYE16()).reshape(N_STATE, D_S5)),
        'd': p['s5_d'][l][None, :], 'glu_w': p['s5_glu_w'][l], 'glu_b': p['s5_glu_b'][l][None, :],
    }
    return prm, disc_vjp


def _lru_params(p, l):
    eye4 = jnp.eye(4, dtype=F32)
    return {
        'conv_w': p['lru_conv_w'][l], 'conv_b': p['lru_conv_b'][l][None, :],
        'wx': _mx(jnp.einsum('hij,hk->hikj', p['lru_wx'][l], eye4).reshape(D_LRU, D_LRU)),
        'wa': _mx(jnp.einsum('hij,hk->hikj', p['lru_wa'][l], eye4).reshape(D_LRU, D_LRU)),
        'bx': p['lru_bx'][l][None, :], 'ba': p['lru_ba'][l][None, :],
        'sp': jax.nn.softplus(-p['lru_a_param'][l])[None, :],
    }


def _rope_tables(n):
    inv_freq = ROPE_THETA ** (-jnp.arange(0, 64, 2, dtype=F32) / 64)
    ang = jnp.arange(n, dtype=F32)[:, None] * inv_freq[None, :]
    cos, sin = jnp.cos(ang), jnp.sin(ang)
    return jnp.concatenate([cos, cos, cos, cos], axis=1), jnp.concatenate([-sin, sin, -sin, sin], axis=1)


def _sink_cols(sinks):
    nb = TILE_Q // ATTN_BLOCK
    per_unit = sinks.reshape(4, 2).T
    return jnp.broadcast_to(per_unit[:, None, :, None], (2, nb, 4, ATTN_BLOCK)).reshape(2, nb * 4 * ATTN_BLOCK, 1)


def _local_step(x, target, p, big_weights=None, emit_grads=None):
    if big_weights is None:
        big_weights = lambda l, part, after: {k: p[k][l] for name in PARTS[part] for k in (name, name + '_t')}
    if emit_grads is None:
        emit_grads = lambda l, part, grads: 0.0
    n = x.shape[0]
    cos_t, sin_t = _rope_tables(n)
    row = lambda a: a[None, :]
    saved = []
    h = x
    for l in range(DEPTH):
        s = {'x0': h}
        bw = s['bw'] = dict(big_weights(l, 'mix', h))
        s['q'], s['k'], s['v'], s['u'], s['xr'], s['gate'] = _inproj_fwd(h, bw['w_in'], row(p['b_in'][l]), cos_t, sin_t)
        s['sinks'] = _sink_cols(p['attn_sinks'][l])
        s['ya'] = _attn_fwd(s['q'], s['k'], s['v'], s['sinks'])
        s['s5'], s['s5_vjp'] = _s5_params(p, l)
        s['ys'], s['s5_cr'], s['s5_ci'] = _s5_fwd(s['u'], s['s5'])
        s['lru'] = _lru_params(p, l)
        s['yl'], s['lru_c'] = _lru_fwd(s['xr'], s['gate'], s['lru'])
        s['mix'], s['r1'], s['x1'] = _mixout_fwd(s['ya'], s['ys'], s['yl'], h, row(p['mix_norm_g'][l]), bw['w_out'],
                                                 row(p['b_out'][l]), row(p['ln1_g'][l]), row(p['ln1_b'][l]))
        bw.update(big_weights(l, 'ffn', s['x1']))
        s['gp'], s['up'], s['r2'], h = _ffn_fwd(s['x1'], bw['ffn_w_gate'], bw['ffn_w_up'], p['ffn_conv_w'][l],
                                                row(p['ffn_conv_b'][l]), bw['ffn_w_down'], row(p['ln2_g'][l]), row(p['ln2_b'][l]))
        saved.append(s)
    loss, dh = _loss_head(h, target)
    placed = 0.0

    grads = {name: [None] * DEPTH for name in WEIGHTS}
    for l in reversed(range(DEPTH)):
        s = saved[l]
        g = {}
        (dr2, dgp, dup, g['ffn_w_down'], cw0, cw1, cw2, dcb, dg2, db2) = _ffn_bwd_down(
            dh, s['r2'], row(p['ln2_g'][l]) + placed, s['gp'], s['up'], p['ffn_conv_w'][l], row(p['ffn_conv_b'][l]),
            s['bw']['ffn_w_down_t'])
        g['ffn_conv_w'] = jnp.concatenate([cw0, cw1, cw2], axis=0)
        g['ffn_conv_b'], g['ln2_g'], g['ln2_b'] = dcb[0], dg2[0], db2[0]
        dr1, dg1, db1 = _ffn_bwd_dx(dr2, dgp, dup, s['r1'], row(p['ln1_g'][l]), s['bw']['ffn_w_gate_t'], s['bw']['ffn_w_up_t'])
        g['ffn_w_gate'], g['ffn_w_up'] = _ffn_bwd_dw(s['x1'], dgp, dup)
        g['ln1_g'], g['ln1_b'] = dg1[0], db1[0]
        placed = emit_grads(l, 'ffn', [g[name] for name in PARTS['ffn']])
        dya, dys, dyl, g['w_out'], dbo, dgm = _mixout_bwd(dr1, s['mix'], s['ya'], s['ys'], s['yl'],
                                                         row(p['mix_norm_g'][l]) + placed, s['bw']['w_out_t'])
        g['b_out'], g['mix_norm_g'] = dbo[0], dgm[0]

        dxr, dgate, lw0, lw1, lw2, lw3, lcb, dwx, dwa, dbx, dba, dsp = _lru_bwd(s['xr'], s['gate'], dyl, s['lru_c'], s['lru'])
        g['lru_conv_w'] = jnp.concatenate([lw0, lw1, lw2, lw3], axis=0)
        g['lru_conv_b'], g['lru_bx'], g['lru_ba'] = lcb[0], dbx[0], dba[0]
        g['lru_wx'] = jnp.einsum('hihj->hij', dwx.reshape(4, 64, 4, 64))
        g['lru_wa'] = jnp.einsum('hihj->hij', dwa.reshape(4, 64, 4, 64))
        g['lru_a_param'] = -dsp[0] * jax.nn.sigmoid(-p['lru_a_param'][l])

        du, dbr, dbi, dcr, dci, dar, dai, dd, g['s5_glu_w'], dgb = _s5_bwd(s['u'], dys, s['s5_cr'], s['s5_ci'], s['s5'])
        g['s5_c_re'] = jnp.einsum('gpgc->gcp', dcr.reshape(16, 64, 16, 16))
        g['s5_c_im'] = jnp.einsum('gpgc->gcp', dci.reshape(16, 64, 16, 16))
        g['s5_d'], g['s5_glu_b'] = dd[0], dgb[0]
        g['s5_a_re'], g['s5_a_im'], g['s5_log_dt'], g['s5_b_re'], g['s5_b_im'] = s['s5_vjp']((
            dar.reshape(16, 64), dai.reshape(16, 64), jnp.einsum('gcgp->gpc', dbr.reshape(16, 16, 16, 64)),
            jnp.einsum('gcgp->gpc', dbi.reshape(16, 16, 16, 64))))

        dq, dk, dv, dsink = _attn_bwd(s['q'], s['k'], s['v'], s['sinks'], s['ya'], dya)
        g['attn_sinks'] = dsink[:, 0]
        dh, g['w_in'], dbin = _inproj_bwd(dq, dk, dv, du, dxr, dgate, cos_t, sin_t, s['x0'], dr1, s['bw']['w_in_t'])
        g['b_in'] = dbin[0]
        placed = emit_grads(l, 'mix', [g[name] for name in PARTS['mix']])
        for name in WEIGHTS:
            grads[name][l] = g[name]
    big = dict(BIG)
    return loss, dh, {name: grads[name] if name in big else jnp.stack(grads[name]) for name in WEIGHTS}


def kernel(x, w_in, b_in, attn_sinks, s5_a_re, s5_a_im, s5_b_re, s5_b_im, s5_c_re, s5_c_im, s5_d, s5_log_dt, s5_glu_w, s5_glu_b, lru_conv_w, lru_conv_b, lru_wx, lru_bx, lru_wa, lru_ba, lru_a_param, mix_norm_g, w_out, b_out, ln1_g, ln1_b, ffn_w_gate, ffn_w_up, ffn_conv_w, ffn_conv_b, ffn_w_down, ln2_g, ln2_b, loss_target, m_w_in, m_b_in, m_attn_sinks, m_s5_a_re, m_s5_a_im, m_s5_b_re, m_s5_b_im, m_s5_c_re, m_s5_c_im, m_s5_d, m_s5_log_dt, m_s5_glu_w, m_s5_glu_b, m_lru_conv_w, m_lru_conv_b, m_lru_wx, m_lru_bx, m_lru_wa, m_lru_ba, m_lru_a_param, m_mix_norm_g, m_w_out, m_b_out, m_ln1_g, m_ln1_b, m_ffn_w_gate, m_ffn_w_up, m_ffn_conv_w, m_ffn_conv_b, m_ffn_w_down, m_ln2_g, m_ln2_b, v_w_in, v_b_in, v_attn_sinks, v_s5_a_re, v_s5_a_im, v_s5_b_re, v_s5_b_im, v_s5_c_re, v_s5_c_im, v_s5_d, v_s5_log_dt, v_s5_glu_w, v_s5_glu_b, v_lru_conv_w, v_lru_conv_b, v_lru_wx, v_lru_bx, v_lru_wa, v_lru_ba, v_lru_a_param, v_mix_norm_g, v_w_out, v_b_out, v_ln1_g, v_ln1_b, v_ffn_w_gate, v_ffn_w_up, v_ffn_conv_w, v_ffn_conv_b, v_ffn_w_down, v_ln2_g, v_ln2_b):
    given = dict(locals())
    whole = {name: given[name] for name in WEIGHTS if name not in dict(BIG)}
    small_whole, big_weights, placed = _gather_weights({name: given[name] for name in SHARDED})
    whole.update(small_whole)
    whole['b_in'] = whole['b_in'] + placed
    in_flight = {}

    def emit_grads(l, part, grads):
        in_flight[(l, part)] = _scatter_start(f"grads_start_{part}{l}", grads)
        return in_flight[(l, part)][-1][0, 0]

    loss, grad_x, grads = _local_step(x[0], loss_target[0], whole, big_weights, emit_grads)
    total = lax.psum(loss[0, 0], ("x", "y", "c"))
    return (total, grad_x[None], *_update(given, grads, in_flight, grad_x))


def _update(given, grads, in_flight, after):
    local_w = {name: given[name] for name in WEIGHTS}
    me = _me()
    outs = {}

    forms = {part: [(name, dict(BIG)[name], _row_form(local_w[name], dict(BIG)[name]).shape[1]) for name in names]
             for part, names in PARTS.items()}
    for (l, part), (send_sems, recv_sems, srcs, landing, _) in in_flight.items():
        srcs, landing = _scatter_wait(f"grads_wait_{part}{l}", send_sems, recv_sems, srcs, landing, after)
        own = jnp.concatenate([lax.dynamic_slice_in_dim(g, me * r, r, axis=0) for g, (_, _, r) in zip(srcs, forms[part])], axis=0)
        parts = lax.dynamic_update_slice(landing, own[None], (me, 0, 0))
        packed = [jnp.concatenate([_row_form(given[prefix + name], t)[l] for name, t, _ in forms[part]], axis=0)
                  for prefix in ('', 'm_', 'v_')]
        outs[(l, part)] = _reduce_adamw(parts, *packed, tile_rows=own.shape[0] // 4)

    rest = SMALL_SHARDED + REPLICATED
    pad = -sum(_packed_rows(local_w[name].shape) for name in rest) % PACK_TILE
    small = jnp.concatenate(
        [_pack_rows(_to_blocks(grads[name], SHARD_AXIS[name]), lead=1) for name in SMALL_SHARDED]
        + [jnp.broadcast_to(_pack_rows(grads[name]), (N_DEV, _packed_rows(grads[name].shape), LANES)) for name in REPLICATED]
        + [jnp.zeros((N_DEV, pad, LANES), F32)], axis=1)

    def packed_rest(prefix):
        return jnp.concatenate([_pack_rows(given[prefix + name]) for name in rest] + [jnp.zeros((pad, LANES), F32)], axis=0)

    rest_outs = _reduce_adamw(_exchange_grads([], small), packed_rest(''), packed_rest('m_'), packed_rest('v_'))

    def unpack(i):
        res = {}
        for part in PARTS:
            lo = 0
            for name, t, r in forms[part]:
                res[name] = _row_form(jnp.stack([outs[(l, part)][i][lo:lo + r] for l in range(DEPTH)]), t)
                lo += r
        lo = 0
        for name in rest:
            shape = local_w[name].shape
            res[name] = rest_outs[i][lo:lo + _packed_rows(shape)].reshape(-1)[:math.prod(shape)].reshape(shape)
            lo += _packed_rows(shape)
        return [res[name] for name in WEIGHTS]

    return (*unpack(0), *unpack(1), *unpack(2), *unpack(3))
```

```python
import functools
import math

import jax
import jax.numpy as jnp
from jax import lax
from jax.experimental import pallas as pl
from jax.experimental.pallas import tpu as pltpu

F32 = jnp.float32
MXU = jnp.bfloat16

N_DEV = 8
DEPTH = 4
D = 1024
D_ATTN, D_KV, D_S5, D_LRU = 512, 128, 256, 256
D_IN = 1536
D_FF = 2816
FF_CHUNK = 256
N_STATE = 1024
LANES = 1024
ALPHA = (2 * DEPTH) ** 0.25
LN_EPS = 1e-5
RMS_EPS = 1e-6
LRU_C = 8.0
ROPE_THETA = 10000.0
ADAM_LR, ADAM_B1, ADAM_B2, ADAM_EPS, ADAM_WD, ADAM_STEP = 0.001, 0.9, 0.999, 1e-08, 0.01, 10

TILE = 256
S5_PAIR = 2
N_SEG = 8
SEG = TILE // N_SEG
TILE_Q = 512
TILE_BIG = 512
TILE_WIDE = 512
ATTN_BLOCK = 128
PACK_TILE = 256
VMEM_MB = 56

WEIGHTS = ['w_in', 'b_in', 'attn_sinks', 's5_a_re', 's5_a_im', 's5_b_re', 's5_b_im', 's5_c_re', 's5_c_im', 's5_d', 's5_log_dt',
           's5_glu_w', 's5_glu_b', 'lru_conv_w', 'lru_conv_b', 'lru_wx', 'lru_bx', 'lru_wa', 'lru_ba', 'lru_a_param', 'mix_norm_g',
           'w_out', 'b_out', 'ln1_g', 'ln1_b', 'ffn_w_gate', 'ffn_w_up', 'ffn_conv_w', 'ffn_conv_b', 'ffn_w_down', 'ln2_g', 'ln2_b']
SHARD_AXIS = {'w_in': 2, 's5_glu_w': 1, 'lru_conv_w': 2, 'w_out': 1, 'ffn_w_gate': 2, 'ffn_w_up': 2, 'ffn_conv_w': 2,
              'ffn_w_down': 1}
SHARDED = [n for n in WEIGHTS if n in SHARD_AXIS]
REPLICATED = [n for n in WEIGHTS if n not in SHARD_AXIS]
BIG = [(n, SHARD_AXIS[n] == 2) for n in ('w_in', 'w_out', 'ffn_w_gate', 'ffn_w_up', 'ffn_w_down')]
SMALL_SHARDED = [n for n in SHARDED if n not in dict(BIG)]
PARTS = {'mix': ['w_in', 'w_out'], 'ffn': ['ffn_w_gate', 'ffn_w_up', 'ffn_w_down']}
GATHER_F32 = ('lru_conv_w', 'ffn_conv_w')


def _dot(a, b):
    return jnp.dot(a, b, preferred_element_type=F32)


def _dot_nt(a, b):
    return lax.dot_general(a, b, (((1,), (1,)), ((), ())), preferred_element_type=F32)


def _dot_tn(a, b):
    return lax.dot_general(a, b, (((0,), (0,)), ((), ())), preferred_element_type=F32)


def _mx(a):
    return a.astype(MXU)


_GELU_C = math.sqrt(2.0 / math.pi)


def _gelu(x):
    th = jnp.tanh(_GELU_C * (x + 0.044715 * x * x * x))
    return 0.5 * x * (1.0 + th)


def _gelu_grad(x):
    th = jnp.tanh(_GELU_C * (x + 0.044715 * x * x * x))
    return 0.5 * (1.0 + th) + 0.5 * x * (1.0 - th * th) * _GELU_C * (1.0 + 3.0 * 0.044715 * x * x)


def _sigmoid(x):
    return 0.5 * jnp.tanh(0.5 * x) + 0.5


def _ln_stats(r):
    mu = jnp.mean(r, axis=-1, keepdims=True)
    xc = r - mu
    var = jnp.mean(xc * xc, axis=-1, keepdims=True)
    rstd = lax.rsqrt(var + LN_EPS)
    return xc * rstd, rstd


def _ln_bwd(dy, g, xhat, rstd):
    dxh = dy * g
    return rstd * (dxh - jnp.mean(dxh, axis=-1, keepdims=True) - xhat * jnp.mean(dxh * xhat, axis=-1, keepdims=True))


def _rms(y):
    return lax.rsqrt(jnp.mean(y * y, axis=-1, keepdims=True) + RMS_EPS)


def _sum0(a):
    return jnp.sum(a, axis=0, keepdims=True)


def _row_iota(shape):
    return lax.broadcasted_iota(jnp.int32, shape, 0)


def _shift_down(ext, j, rows):
    return pltpu.roll(ext, j, 0)[8:8 + rows]


def _shift_up(ext, j, rows):
    return pltpu.roll(ext, ext.shape[0] - j, 0)[:rows]


def _swap_halves(t):
    w = t.shape[1]
    lane = lax.broadcasted_iota(jnp.int32, t.shape, 1)
    return jnp.where((lane & 32) == 0, pltpu.roll(t, w - 32, 1), pltpu.roll(t, 32, 1))


def _rope(t, cos, sin_signed):
    return t * cos + _swap_halves(t) * sin_signed


def _rope_t(d, cos, sin_signed):
    return d * cos + _swap_halves(d * sin_signed)


def _cmul_add(ar, ai, xr, xi, yr, yi):
    return ar * xr - ai * xi + yr, ar * xi + ai * xr + yi


def _seg_rows(k):
    return slice(N_SEG * k, N_SEG * (k + 1))


def _permute_rows(perm, x):
    hi = _mx(x)
    rest = x - hi.astype(F32)
    mid = _mx(rest)
    lo = _mx(rest - mid.astype(F32))
    return _dot(perm, hi) + _dot(perm, mid) + _dot(perm, lo)


def _cscan(sr, si, tab, cin_r, cin_i, reverse):
    sgn = -1.0 if reverse else 1.0
    pw_re, pw_im, dbl_re, dbl_im = tab['pw_re'], tab['pw_im'], tab['dbl_re'], tab['dbl_im']
    ar, ai = pw_re[0:1, :], sgn * pw_im[0:1, :]
    shape = (N_SEG, sr.shape[1])
    hr = hi = jnp.zeros(shape, F32)
    for k in (range(SEG - 1, -1, -1) if reverse else range(SEG)):
        hr, hi = _cmul_add(ar, ai, hr, hi, sr[_seg_rows(k), :], si[_seg_rows(k), :])
        sr[_seg_rows(k), :] = hr
        si[_seg_rows(k), :] = hi
    sub = _row_iota(shape)

    def shifted(v, d):
        if reverse:
            return jnp.where(sub < N_SEG - d, pltpu.roll(v, N_SEG - d, 0), 0.0)
        return jnp.where(sub >= d, pltpu.roll(v, d, 0), 0.0)

    fr, fi = hr, hi
    for j, d in enumerate((1, 2, 4)):
        fr, fi = _cmul_add(dbl_re[j:j + 1, :], sgn * dbl_im[j:j + 1, :], shifted(fr, d), shifted(fi, d), fr, fi)
    seg_re, seg_im = (tab['segr_re'], tab['segr_im']) if reverse else (tab['seg_re'], tab['seg_im'])
    cr, ci = _cmul_add(seg_re[...], sgn * seg_im[...], cin_r, cin_i, shifted(fr, 1), shifted(fi, 1))
    nr, ni = _cmul_add(dbl_re[0:1, :], sgn * dbl_im[0:1, :], cr, ci, hr, hi)
    for k in range(SEG):
        j = SEG - 1 - k if reverse else k
        xr, xi = _cmul_add(pw_re[j:j + 1, :], sgn * pw_im[j:j + 1, :], cr, ci, sr[_seg_rows(k), :], si[_seg_rows(k), :])
        sr[_seg_rows(k), :] = xr
        si[_seg_rows(k), :] = xi
    edge = slice(0, 1) if reverse else slice(N_SEG - 1, N_SEG)
    return nr[edge], ni[edge]


def _rscan(a, b, reverse):
    rows = a.shape[0]
    row = _row_iota(a.shape)
    s = 1
    while s < rows:
        if reverse:
            keep = row < rows - s
            sa = jnp.where(keep, pltpu.roll(a, rows - s, 0), 1.0)
            sb = jnp.where(keep, pltpu.roll(b, rows - s, 0), 0.0)
        else:
            keep = row >= s
            sa = jnp.where(keep, pltpu.roll(a, s, 0), 1.0)
            sb = jnp.where(keep, pltpu.roll(b, s, 0), 0.0)
        b = b + a * sb
        a = a * sa
        s *= 2
    return a, b


def _whole():
    return pl.BlockSpec(memory_space=pltpu.VMEM)


def _rows_spec(rows, cols, n_tiles, reverse=False):
    if reverse:
        return pl.BlockSpec((rows, cols), lambda i: (n_tiles - 1 - i, 0))
    return pl.BlockSpec((rows, cols), lambda i: (i, 0))


def _halo_spec(cols, tile_rows, n_tiles, reverse=False):
    per = tile_rows // 8
    if reverse:
        return pl.BlockSpec((8, cols), lambda i: (jnp.maximum((n_tiles - 1 - i) * per - 1, 0), 0))
    return pl.BlockSpec((8, cols), lambda i: (jnp.maximum(i * per - 1, 0), 0))


def _call(body, name, n_tiles, in_specs, out_specs, out_shape, scratch=()):
    return pl.pallas_call(
        body, name=name, grid=(n_tiles,), in_specs=in_specs, out_specs=out_specs, out_shape=out_shape,
        scratch_shapes=list(scratch),
        compiler_params=pltpu.CompilerParams(dimension_semantics=("arbitrary",), vmem_limit_bytes=VMEM_MB << 20))


def _sds(shape, dtype=F32):
    return jax.ShapeDtypeStruct(shape, dtype)


def _inproj_fwd(x, w, b, cos_t, sin_t):
    n = x.shape[0]
    nt = n // TILE_WIDE

    def body(x_ref, w_ref, b_ref, c_ref, s_ref, q_ref, k_ref, v_ref, u_ref, xr_ref, g_ref):
        p = _dot(_mx(x_ref[...]), w_ref[...]) + b_ref[...]
        cos, sin = c_ref[...], s_ref[...]
        q_ref[...] = _mx(_rope(p[:, :D_ATTN], jnp.tile(cos, (1, 4)), jnp.tile(sin, (1, 4))))
        k_ref[...] = _mx(_rope(p[:, 512:640], cos, sin))
        v_ref[...] = _mx(p[:, 640:768])
        u_ref[...] = p[:, 768:1024]
        xr_ref[...] = p[:, 1024:1280]
        g_ref[...] = p[:, 1280:1536]

    r = functools.partial(_rows_spec, n_tiles=nt)
    return _call(
        body, "inproj_fwd", nt,
        [r(TILE_WIDE, D), _whole(), _whole(), r(TILE_WIDE, 128), r(TILE_WIDE, 128)],
        [r(TILE_WIDE, D_ATTN), r(TILE_WIDE, D_KV), r(TILE_WIDE, D_KV), r(TILE_WIDE, D_S5), r(TILE_WIDE, D_LRU), r(TILE_WIDE, D_LRU)],
        [_sds((n, D_ATTN), MXU), _sds((n, D_KV), MXU), _sds((n, D_KV), MXU), _sds((n, D_S5)), _sds((n, D_LRU)), _sds((n, D_LRU))],
    )(x, w, b, cos_t, sin_t)


def _inproj_bwd(dq, dk, dv, du, dxr, dgate, cos_t, sin_t, x0, dr1, w_t):
    n = x0.shape[0]
    nt = n // TILE_WIDE

    def body(dq_ref, dk_ref, dv_ref, du_ref, dxr_ref, dg_ref, c_ref, s_ref, x_ref, dr_ref, w_ref, dx_ref, dw_ref, db_ref):
        @pl.when(pl.program_id(0) == 0)
        def _():
            dw_ref[...] = jnp.zeros_like(dw_ref)
            db_ref[...] = jnp.zeros_like(db_ref)

        cos, sin = c_ref[...], s_ref[...]
        dtq = _rope_t(dq_ref[...], jnp.tile(cos, (1, 4)), jnp.tile(sin, (1, 4)))
        dtk = _rope_t(dk_ref[...], cos, sin)
        dp = jnp.concatenate([dtq, dtk, dv_ref[...], du_ref[...], dxr_ref[...], dg_ref[...]], axis=1)
        db_ref[...] += _sum0(dp)
        dpb = _mx(dp)
        dw_ref[...] += _dot_tn(dpb, _mx(x_ref[...]))
        dx_ref[...] = ALPHA * dr_ref[...] + _dot(dpb, w_ref[...])

    r = functools.partial(_rows_spec, n_tiles=nt)
    return _call(
        body, "inproj_bwd", nt,
        [r(TILE_WIDE, D_ATTN), r(TILE_WIDE, D_KV), r(TILE_WIDE, D_KV), r(TILE_WIDE, D_S5), r(TILE_WIDE, D_LRU), r(TILE_WIDE, D_LRU),
         r(TILE_WIDE, 128), r(TILE_WIDE, 128), r(TILE_WIDE, D), r(TILE_WIDE, D), _whole()],
        [r(TILE_WIDE, D), _whole(), _whole()],
        [_sds((n, D)), _sds((D_IN, D)), _sds((1, D_IN))],
    )(dq, dk, dv, du, dxr, dgate, cos_t, sin_t, x0, dr1, w_t)


def _kv_variants(t, lo):
    tr = pltpu.roll(t, 64, 1)
    out = []
    for j in range(2):
        first = jnp.where(lo, t if j == 0 else tr, 0.0)
        second = jnp.where(lo, 0.0, tr if j == 0 else t)
        out.append(_mx(jnp.concatenate([first, second], axis=0)))
    return out


def _kv_collect(x0, x1, lo):
    a = x0[:256] + pltpu.roll(x0[256:], 64, 1)
    b = pltpu.roll(x1[:256], 64, 1) + x1[256:]
    return jnp.where(lo, a, b)


def _attn_probs(s, sink_ref):
    out = []
    for hp in range(2):
        sh = s[:, hp * 256:(hp + 1) * 256]
        sink = sink_ref[hp]
        m = jnp.maximum(jnp.max(sh, axis=1, keepdims=True), sink)
        p = jnp.exp(sh - m)
        es = jnp.exp(sink - m)
        inv = 1.0 / (jnp.sum(p, axis=1, keepdims=True) + es)
        out.append((p * inv, es * inv))
    return out


def _attn_scores(q_ref, k_ref, v_ref, nb):
    lo = lax.broadcasted_iota(jnp.int32, (256, 128), 1) < 64
    kcats, vcats, kstarts, parts = [], [], [], []
    for b in range(nb):
        block = pl.program_id(0) * nb + b
        kstart = pl.multiple_of(jnp.maximum(block - 1, 0) * ATTN_BLOCK, ATTN_BLOCK)
        kcat = _kv_variants(k_ref[pl.ds(kstart, 256), :].astype(F32), lo)
        kcats.append(kcat)
        vcats.append(_kv_variants(v_ref[pl.ds(kstart, 256), :].astype(F32), lo))
        kstarts.append(kstart)
        valid = _attn_mask(block, kstart)
        for i in range(4):
            s = _dot_nt(q_ref[b * ATTN_BLOCK:(b + 1) * ATTN_BLOCK, i * 128:(i + 1) * 128], kcat[i // 2]) * 0.125
            parts.append(jnp.where(valid, s, -jnp.inf))
    return jnp.concatenate(parts, axis=0), kcats, vcats, kstarts


def _attn_mask(block, kstart):
    col = lax.broadcasted_iota(jnp.int32, (ATTN_BLOCK, 512), 1)
    row = lax.broadcasted_iota(jnp.int32, (ATTN_BLOCK, 512), 0)
    diff = (block * ATTN_BLOCK + row) - (kstart + (col & 255))
    return (diff >= 0) & (diff < ATTN_BLOCK)


def _attn_fwd(q, k, v, sink_cols):
    n = q.shape[0]
    nt = n // TILE_Q
    nb = TILE_Q // ATTN_BLOCK

    def body(q_ref, k_ref, v_ref, s_ref, o_ref):
        s, _, vcats, _ = _attn_scores(q_ref, k_ref, v_ref, nb)
        (p0, _), (p1, _) = _attn_probs(s, s_ref)
        pb = _mx(jnp.concatenate([p0, p1], axis=1))
        for b in range(nb):
            for i in range(4):
                unit = (b * 4 + i) * ATTN_BLOCK
                o_ref[b * ATTN_BLOCK:(b + 1) * ATTN_BLOCK, i * 128:(i + 1) * 128] = _dot(pb[unit:unit + ATTN_BLOCK], vcats[b][i // 2])

    return _call(
        body, "attn_fwd", nt,
        [_rows_spec(TILE_Q, D_ATTN, nt), _whole(), _whole(), _whole()],
        _rows_spec(TILE_Q, D_ATTN, nt), _sds((n, D_ATTN)),
    )(q, k, v, sink_cols)


def _attn_bwd(q, k, v, sink_cols, o, do):
    n = q.shape[0]
    nt = n // TILE_Q
    nb = TILE_Q // ATTN_BLOCK

    def body(q_ref, k_ref, v_ref, s_ref, o_ref, do_ref, dq_ref, dk_ref, dv_ref, ds_ref):
        @pl.when(pl.program_id(0) == 0)
        def _():
            dk_ref[...] = jnp.zeros_like(dk_ref)
            dv_ref[...] = jnp.zeros_like(dv_ref)
            ds_ref[...] = jnp.zeros_like(ds_ref)

        lo = lax.broadcasted_iota(jnp.int32, (256, 128), 1) < 64
        s, kcats, vcats, kstarts = _attn_scores(q_ref, k_ref, v_ref, nb)
        probs = _attn_probs(s, s_ref)
        do = do_ref[...]
        dob = _mx(do)
        od = do * o_ref[...]
        lo_q = (lax.broadcasted_iota(jnp.int32, od.shape, 1) & 64) == 0
        od_head = (jnp.where(lo_q, od, 0.0), jnp.where(lo_q, 0.0, od))
        units = [(b, i) for b in range(nb) for i in range(4)]

        def tile_part(a, b, i):
            return a[b * ATTN_BLOCK:(b + 1) * ATTN_BLOCK, i * 128:(i + 1) * 128]

        dp = jnp.concatenate([_dot_nt(tile_part(dob, b, i), vcats[b][i // 2]) for b, i in units], axis=0)
        ds = []
        for hp in range(2):
            p, p_sink = probs[hp]
            delta = jnp.concatenate([jnp.sum(tile_part(od_head[hp], b, i), axis=1, keepdims=True) for b, i in units], axis=0)
            ds.append(p * (dp[:, hp * 256:(hp + 1) * 256] - delta) * 0.125)
            t = p_sink * delta
            for i in range(4):
                dsink = sum(_sum0(t[(b * 4 + i) * ATTN_BLOCK:(b * 4 + i + 1) * ATTN_BLOCK]) for b in range(nb))
                ds_ref[2 * i + hp:2 * i + hp + 1, :] -= jnp.broadcast_to(dsink, (1, 128))
        dsb = _mx(jnp.concatenate(ds, axis=1))
        pb = _mx(jnp.concatenate([probs[0][0], probs[1][0]], axis=1))
        for b in range(nb):
            dkc = [jnp.zeros((512, 128), F32), jnp.zeros((512, 128), F32)]
            dvc = [jnp.zeros((512, 128), F32), jnp.zeros((512, 128), F32)]
            for i in range(4):
                j = i // 2
                unit = slice((b * 4 + i) * ATTN_BLOCK, (b * 4 + i + 1) * ATTN_BLOCK)
                dq_ref[b * ATTN_BLOCK:(b + 1) * ATTN_BLOCK, i * 128:(i + 1) * 128] = _dot(dsb[unit], kcats[b][j])
                dkc[j] = dkc[j] + _dot_tn(dsb[unit], tile_part(q_ref, b, i))
                dvc[j] = dvc[j] + _dot_tn(pb[unit], tile_part(dob, b, i))
            dk_ref[pl.ds(kstarts[b], 256), :] += _kv_collect(dkc[0], dkc[1], lo)
            dv_ref[pl.ds(kstarts[b], 256), :] += _kv_collect(dvc[0], dvc[1], lo)

    r = _rows_spec(TILE_Q, D_ATTN, nt)
    return _call(
        body, "attn_bwd", nt,
        [r, _whole(), _whole(), _whole(), r, r],
        [r, _whole(), _whole(), _whole()],
        [_sds((n, D_ATTN)), _sds((n, D_KV)), _sds((n, D_KV)), _sds((8, 128))],
    )(q, k, v, sink_cols, o, do)


S5_TABLES = ('pw_re', 'pw_im', 'dbl_re', 'dbl_im', 'seg_re', 'seg_im', 'segr_re', 'segr_im')
S5_WEIGHTS = ('b_re', 'b_im', 'c_re', 'c_im', 'd', 'glu_w', 'glu_b', 'perm', 'perm_t')


def _s5_states(u, carry_r, carry_i, b_re, b_im, tab, hr_s, hi_s):
    ub = _mx(u)
    hr_s[...] = _dot(ub, b_re[...])
    hi_s[...] = _dot(ub, b_im[...])
    return ub, _cscan(hr_s, hi_s, tab, carry_r, carry_i, reverse=False)


def _s5_fwd(u, prm):
    n = u.shape[0]
    nt = n // (S5_PAIR * TILE)

    def body(u_ref, *refs):
        tab = dict(zip(S5_TABLES, refs[:8]))
        b_re, b_im, c_re, c_im, d_ref, gw_ref, gb_ref, perm, perm_t = refs[8:17]
        y_ref, cr_out, ci_out, cr_s, ci_s = refs[17:22]
        states = [refs[22 + 2 * j:24 + 2 * j] for j in range(S5_PAIR)]
        rows = [slice(j * TILE, (j + 1) * TILE) for j in range(S5_PAIR)]

        @pl.when(pl.program_id(0) == 0)
        def _():
            cr_s[...] = jnp.zeros_like(cr_s)
            ci_s[...] = jnp.zeros_like(ci_s)

        us = []
        for j in range(S5_PAIR):
            us.append(_permute_rows(perm[...], u_ref[rows[j], :]))
            ub = _mx(us[j])
            states[j][0][...] = _dot(ub, b_re[...])
            states[j][1][...] = _dot(ub, b_im[...])
        for j in range(S5_PAIR):
            cr, ci = cr_s[...], ci_s[...]
            cr_out[8 * j:8 * j + 8, :] = jnp.broadcast_to(cr, (8, N_STATE))
            ci_out[8 * j:8 * j + 8, :] = jnp.broadcast_to(ci, (8, N_STATE))
            cr_s[...], ci_s[...] = _cscan(*states[j], tab, cr, ci, reverse=False)
        for j in range(S5_PAIR):
            hr_s, hi_s = states[j]
            y = _dot(_mx(hr_s[...]), c_re[...]) - _dot(_mx(hi_s[...]), c_im[...]) + d_ref[...] * us[j]
            z = _gelu(y)
            y_ref[rows[j], :] = _permute_rows(perm_t[...], z * _sigmoid(_dot(_mx(z), gw_ref[...]) + gb_ref[...]))

    r = functools.partial(_rows_spec, n_tiles=nt)
    return _call(
        body, "s5_fwd", nt,
        [r(S5_PAIR * TILE, D_S5)] + [_whole()] * 17,
        [r(S5_PAIR * TILE, D_S5), r(S5_PAIR * 8, N_STATE), r(S5_PAIR * 8, N_STATE)],
        [_sds((n, D_S5)), _sds((n // TILE * 8, N_STATE)), _sds((n // TILE * 8, N_STATE))],
        scratch=[pltpu.VMEM((1, N_STATE), F32)] * 2 + [pltpu.VMEM((TILE, N_STATE), F32)] * (2 * S5_PAIR),
    )(u, *[prm[k] for k in S5_TABLES + S5_WEIGHTS])


def _s5_bwd(u, dys, carry_re, carry_im, prm):
    n = u.shape[0]
    nt = n // (S5_PAIR * TILE)

    def body(u_ref, dy_ref, cin_r, cin_i, *refs):
        tab = dict(zip(S5_TABLES, refs[:8]))
        b_re, b_im, c_re, c_im, d_ref, gw_ref, gb_ref, perm, perm_t = refs[8:17]
        du_ref, dbr_ref, dbi_ref, dcr_ref, dci_ref, dar_ref, dai_ref, dd_ref, dgw_ref, dgb_ref = refs[17:27]
        gr_s, gi_s = refs[27:29]
        scratch = [refs[29 + 4 * j:33 + 4 * j] for j in range(S5_PAIR)]
        later_first = list(reversed(range(S5_PAIR)))
        rows = [slice(j * TILE, (j + 1) * TILE) for j in range(S5_PAIR)]

        @pl.when(pl.program_id(0) == 0)
        def _():
            for ref in (dbr_ref, dbi_ref, dcr_ref, dci_ref, dar_ref, dai_ref, dd_ref, dgw_ref, dgb_ref, gr_s, gi_s):
                ref[...] = jnp.zeros_like(ref)

        us, ubs, carries, dy_of = {}, {}, {}, {}
        for j in later_first:
            us[j] = _permute_rows(perm[...], u_ref[rows[j], :])
            carries[j] = (cin_r[8 * j:8 * j + 1, :], cin_i[8 * j:8 * j + 1, :])
            ubs[j], _ = _s5_states(us[j], *carries[j], b_re, b_im, tab, *scratch[j][:2])
        for j in later_first:
            u = us[j]
            hr_s, hi_s, gr_t, gi_t = scratch[j]
            hrb, hib = _mx(hr_s[...]), _mx(hi_s[...])
            y = _dot(hrb, c_re[...]) - _dot(hib, c_im[...]) + d_ref[...] * u
            z = _gelu(y)
            zb = _mx(z)
            sg = _sigmoid(_dot(zb, gw_ref[...]) + gb_ref[...])
            dout = _permute_rows(perm[...], dy_ref[rows[j], :])
            dpre = dout * z * sg * (1.0 - sg)
            dgb_ref[...] += _sum0(dpre)
            dpb = _mx(dpre)
            dgw_ref[...] += _dot_tn(zb, dpb)
            dy = (dout * sg + _dot_nt(dpb, gw_ref[...])) * _gelu_grad(y)
            dd_ref[...] += _sum0(dy * u)
            dyb = _mx(dy)
            dcr_ref[...] += _dot_tn(hrb, dyb)
            dci_ref[...] -= _dot_tn(hib, dyb)
            gr_t[...] = _dot_nt(dyb, c_re[...])
            gi_t[...] = -_dot_nt(dyb, c_im[...])
            dy_of[j] = dy
        for j in later_first:
            gr_s[...], gi_s[...] = _cscan(*scratch[j][2:], tab, gr_s[...], gi_s[...], reverse=True)
        for j in later_first:
            hr_s, hi_s, gr_t, gi_t = scratch[j]
            cr, ci = carries[j]
            sub = _row_iota((N_SEG, N_STATE))
            acc_r = acc_i = jnp.zeros((N_SEG, N_STATE), F32)
            for k in range(SEG):
                if k == 0:
                    hpr = jnp.where(sub >= 1, pltpu.roll(hr_s[_seg_rows(SEG - 1), :], 1, 0), cr)
                    hpi = jnp.where(sub >= 1, pltpu.roll(hi_s[_seg_rows(SEG - 1), :], 1, 0), ci)
                else:
                    hpr, hpi = hr_s[_seg_rows(k - 1), :], hi_s[_seg_rows(k - 1), :]
                gr, gi = gr_t[_seg_rows(k), :], gi_t[_seg_rows(k), :]
                acc_r = acc_r + gr * hpr + gi * hpi
                acc_i = acc_i + gi * hpr - gr * hpi
            dar_ref[...] += _sum0(acc_r)
            dai_ref[...] += _sum0(acc_i)
            grb, gib = _mx(gr_t[...]), _mx(gi_t[...])
            dbr_ref[...] += _dot_tn(ubs[j], grb)
            dbi_ref[...] += _dot_tn(ubs[j], gib)
            du_ref[rows[j], :] = _permute_rows(perm_t[...], dy_of[j] * d_ref[...] + _dot_nt(grb, b_re[...]) + _dot_nt(gib, b_im[...]))

    r = functools.partial(_rows_spec, n_tiles=nt, reverse=True)
    return _call(
        body, "s5_bwd", nt,
        [r(S5_PAIR * TILE, D_S5), r(S5_PAIR * TILE, D_S5), r(S5_PAIR * 8, N_STATE), r(S5_PAIR * 8, N_STATE)] + [_whole()] * 17,
        [r(S5_PAIR * TILE, D_S5)] + [_whole()] * 9,
        [_sds((n, D_S5)), _sds((D_S5, N_STATE)), _sds((D_S5, N_STATE)), _sds((N_STATE, D_S5)), _sds((N_STATE, D_S5)),
         _sds((1, N_STATE)), _sds((1, N_STATE)), _sds((1, D_S5)), _sds((D_S5, D_S5)), _sds((1, D_S5))],
        scratch=[pltpu.VMEM((1, N_STATE), F32)] * 2 + [pltpu.VMEM((TILE, N_STATE), F32)] * (4 * S5_PAIR),
    )(u, dys, carry_re, carry_im, *[prm[k] for k in S5_TABLES + S5_WEIGHTS])


def _lru_gates(xr, halo, tile_index, cw_ref, cb_ref, wx_ref, wa_ref, bx_ref, ba_ref, sp_ref):
    ext = jnp.concatenate([halo, xr], axis=0)
    sh = [xr] + [_shift_down(ext, j, TILE) for j in (1, 2, 3)]
    xc = cb_ref[...] + cw_ref[3:4, :] * sh[0] + cw_ref[2:3, :] * sh[1] + cw_ref[1:2, :] * sh[2] + cw_ref[0:1, :] * sh[3]
    xb = _mx(xc)
    gx = _sigmoid(_dot(xb, wx_ref[...]) + bx_ref[...])
    ga = _sigmoid(_dot(xb, wa_ref[...]) + ba_ref[...])
    la = -LRU_C * ga * sp_ref[...]
    a = jnp.exp(la)
    start = (tile_index * TILE + _row_iota(xr.shape)) == 0
    mult = jnp.where(start, 1.0, jnp.sqrt(-jnp.tanh(la) * (a * a + 1.0)))
    return sh, xc, xb, gx, ga, a, mult, start


def _lru_fwd(xr, gate, prm):
    n = xr.shape[0]
    nt = n // TILE

    def body(x_ref, g_ref, cw_ref, cb_ref, wx_ref, wa_ref, bx_ref, ba_ref, sp_ref, y_ref, c_out, halo_s, c_s):
        first_tile = pl.program_id(0) == 0

        @pl.when(first_tile)
        def _():
            halo_s[...] = jnp.zeros_like(halo_s)
            c_s[...] = jnp.zeros_like(c_s)

        xr = x_ref[...]
        _, xc, _, gx, _, a, mult, _ = _lru_gates(xr, halo_s[...], pl.program_id(0), cw_ref, cb_ref, wx_ref, wa_ref, bx_ref, ba_ref,
                                                 sp_ref)
        halo_s[...] = xr[TILE - 8:]
        acum, h = _rscan(a, mult * gx * xc, reverse=False)
        c = c_s[...]
        c_out[...] = jnp.broadcast_to(c, (8, D_LRU))
        h = h + acum * c
        c_s[...] = h[TILE - 1:TILE]
        y_ref[...] = h * _gelu(g_ref[...])

    r = functools.partial(_rows_spec, n_tiles=nt)
    return _call(
        body, "lru_fwd", nt,
        [r(TILE, D_LRU), r(TILE, D_LRU)] + [_whole()] * 7,
        [r(TILE, D_LRU), r(8, D_LRU)],
        [_sds((n, D_LRU)), _sds((nt * 8, D_LRU))],
        scratch=[pltpu.VMEM((8, D_LRU), F32), pltpu.VMEM((1, D_LRU), F32)],
    )(xr, gate, prm['conv_w'], prm['conv_b'], prm['wx'], prm['wa'], prm['bx'], prm['ba'], prm['sp'])


def _lru_bwd(xr, gate, dyl, carry, prm):
    n = xr.shape[0]
    nt = n // TILE

    def body(x_ref, xh_ref, g_ref, dy_ref, cin_ref, cw_ref, cb_ref, wx_ref, wa_ref, bx_ref, ba_ref, sp_ref,
             dx_ref, dg_ref, dcw0, dcw1, dcw2, dcw3, dcb_ref, dwx_ref, dwa_ref, dbx_ref, dba_ref, dsp_ref, an_s, gn_s, dn_s):
        first_tile = pl.program_id(0) == nt - 1

        @pl.when(pl.program_id(0) == 0)
        def _():
            for ref in (dcw0, dcw1, dcw2, dcw3, dcb_ref, dwx_ref, dwa_ref, dbx_ref, dba_ref, dsp_ref, gn_s, dn_s):
                ref[...] = jnp.zeros_like(ref)
            an_s[...] = jnp.ones_like(an_s)

        xr = x_ref[...]
        halo = jnp.where(first_tile, 0.0, xh_ref[...])
        sh, xc, xb, gx, ga, a, mult, start = _lru_gates(xr, halo, nt - 1 - pl.program_id(0), cw_ref, cb_ref, wx_ref, wa_ref, bx_ref,
                                                        ba_ref, sp_ref)
        acum, h = _rscan(a, mult * gx * xc, reverse=False)
        cin = cin_ref[0:1, :]
        h = h + acum * cin
        gate = g_ref[...]
        dyl = dy_ref[...]
        dg_ref[...] = dyl * h * _gelu_grad(gate)
        row = _row_iota(xr.shape)
        alpha = jnp.where(row < TILE - 1, pltpu.roll(a, TILE - 1, 0), an_s[...])
        racc, g = _rscan(alpha, dyl * _gelu(gate), reverse=True)
        g = g + racc * gn_s[...]
        an_s[...] = a[0:1]
        gn_s[...] = g[0:1]
        hprev = jnp.where(row == 0, cin, pltpu.roll(h, 1, 0))
        da = g * hprev
        dmult = jnp.where(start, 0.0, g * gx * xc)
        dla = da * a - dmult * a * a / mult
        dsp_ref[...] += _sum0(-LRU_C * ga * dla)
        dpa = (-LRU_C * sp_ref[...] * dla) * ga * (1.0 - ga)
        dpx = (g * mult * xc) * gx * (1.0 - gx)
        dba_ref[...] += _sum0(dpa)
        dbx_ref[...] += _sum0(dpx)
        dpab, dpxb = _mx(dpa), _mx(dpx)
        dwa_ref[...] += _dot_tn(xb, dpab)
        dwx_ref[...] += _dot_tn(xb, dpxb)
        dxc = g * mult * gx + _dot_nt(dpab, wa_ref[...]) + _dot_nt(dpxb, wx_ref[...])
        dcb_ref[...] += _sum0(dxc)
        dcw3[...] += _sum0(dxc * sh[0])
        dcw2[...] += _sum0(dxc * sh[1])
        dcw1[...] += _sum0(dxc * sh[2])
        dcw0[...] += _sum0(dxc * sh[3])
        ext = jnp.concatenate([dxc, dn_s[...]], axis=0)
        dx_ref[...] = (cw_ref[3:4, :] * dxc + cw_ref[2:3, :] * _shift_up(ext, 1, TILE) + cw_ref[1:2, :] * _shift_up(ext, 2, TILE)
                       + cw_ref[0:1, :] * _shift_up(ext, 3, TILE))
        dn_s[...] = dxc[:8]

    r = functools.partial(_rows_spec, n_tiles=nt, reverse=True)
    vec = _sds((1, D_LRU))
    return _call(
        body, "lru_bwd", nt,
        [r(TILE, D_LRU), _halo_spec(D_LRU, TILE, nt, reverse=True), r(TILE, D_LRU), r(TILE, D_LRU), r(8, D_LRU)] + [_whole()] * 7,
        [r(TILE, D_LRU), r(TILE, D_LRU)] + [_whole()] * 10,
        [_sds((n, D_LRU)), _sds((n, D_LRU)), vec, vec, vec, vec, vec, _sds((D_LRU, D_LRU)), _sds((D_LRU, D_LRU)), vec, vec, vec],
        scratch=[pltpu.VMEM((1, D_LRU), F32), pltpu.VMEM((1, D_LRU), F32), pltpu.VMEM((8, D_LRU), F32)],
    )(xr, xr, gate, dyl, carry, prm['conv_w'], prm['conv_b'], prm['wx'], prm['wa'], prm['bx'], prm['ba'], prm['sp'])


def _normed_parts(ya, ys, yl):
    return jnp.concatenate([ya * _rms(ya), ys * _rms(ys), yl * _rms(yl)], axis=1)


def _mixout_fwd(ya, ys, yl, x0, g_mix, w_out, b_out, g1, b1):
    n = x0.shape[0]
    nt = n // TILE_WIDE

    def body(ya_ref, ys_ref, yl_ref, x_ref, gm_ref, w_ref, b_ref, g_ref, be_ref, mix_ref, r_ref, x1_ref):
        mixb = _mx(_normed_parts(ya_ref[...], ys_ref[...], yl_ref[...]) * gm_ref[...])
        mix_ref[...] = mixb
        r1 = ALPHA * x_ref[...] + _dot(mixb, w_ref[...]) + b_ref[...]
        r_ref[...] = r1
        xhat, _ = _ln_stats(r1)
        x1_ref[...] = xhat * g_ref[...] + be_ref[...]

    r = functools.partial(_rows_spec, n_tiles=nt)
    return _call(
        body, "mixout_fwd", nt,
        [r(TILE_WIDE, D_ATTN), r(TILE_WIDE, D_S5), r(TILE_WIDE, D_LRU), r(TILE_WIDE, D)] + [_whole()] * 5,
        [r(TILE_WIDE, D), r(TILE_WIDE, D), r(TILE_WIDE, D)],
        [_sds((n, D), MXU), _sds((n, D)), _sds((n, D))],
    )(ya, ys, yl, x0, g_mix, w_out, b_out, g1, b1)


def _mixout_bwd(dr1, mix, ya, ys, yl, g_mix, w_out):
    n = dr1.shape[0]
    nt = n // TILE_WIDE

    def body(dr_ref, mix_ref, ya_ref, ys_ref, yl_ref, gm_ref, w_ref, dya_ref, dys_ref, dyl_ref, dw_ref, db_ref, dgm_ref):
        @pl.when(pl.program_id(0) == 0)
        def _():
            for ref in (dw_ref, db_ref, dgm_ref):
                ref[...] = jnp.zeros_like(ref)

        dr = dr_ref[...]
        db_ref[...] += _sum0(dr)
        drb = _mx(dr)
        dw_ref[...] += _dot_tn(mix_ref[...], drb)
        dmix = _dot(drb, w_ref[...])
        parts = (ya_ref[...], ys_ref[...], yl_ref[...])
        dgm_ref[...] += _sum0(dmix * _normed_parts(*parts))
        dn = dmix * gm_ref[...]
        lo = 0
        for y, out in zip(parts, (dya_ref, dys_ref, dyl_ref)):
            w = y.shape[1]
            rs = _rms(y)
            nrm = y * rs
            dnp = dn[:, lo:lo + w]
            out[...] = rs * (dnp - nrm * jnp.mean(dnp * nrm, axis=-1, keepdims=True))
            lo += w

    r = functools.partial(_rows_spec, n_tiles=nt)
    return _call(
        body, "mixout_bwd", nt,
        [r(TILE_WIDE, D), r(TILE_WIDE, D), r(TILE_WIDE, D_ATTN), r(TILE_WIDE, D_S5), r(TILE_WIDE, D_LRU), _whole(), _whole()],
        [r(TILE_WIDE, D_ATTN), r(TILE_WIDE, D_S5), r(TILE_WIDE, D_LRU), _whole(), _whole(), _whole()],
        [_sds((n, D_ATTN)), _sds((n, D_S5)), _sds((n, D_LRU)), _sds((D, D)), _sds((1, D)), _sds((1, D))],
    )(dr1, mix, ya, ys, yl, g_mix, w_out)


def _ffn_conv(gp, halo, cw_ref, cb_ref, cs):
    ext = jnp.concatenate([halo, gp], axis=0)
    s1 = _shift_down(ext, 1, TILE)
    s2 = _shift_down(ext, 2, TILE)
    return s1, s2, cb_ref[:, cs] + cw_ref[2:3, cs] * gp + cw_ref[1:2, cs] * s1 + cw_ref[0:1, cs] * s2


def _ffn_fwd(x1, wg, wu, cw, cb, wd, g2, b2):
    n = x1.shape[0]
    nt = n // TILE

    def body(x_ref, wg_ref, wu_ref, cw_ref, cb_ref, wd_ref, g_ref, be_ref, gp_ref, up_ref, r_ref, x2_ref, halo_s, act_s):
        @pl.when(pl.program_id(0) == 0)
        def _():
            halo_s[...] = jnp.zeros_like(halo_s)

        x1 = x_ref[...]
        xb = _mx(x1)
        for c in range(D_FF // FF_CHUNK):
            cs = slice(c * FF_CHUNK, (c + 1) * FF_CHUNK)
            gp = _dot(xb, wg_ref[:, cs])
            up = _dot(xb, wu_ref[:, cs])
            gp_ref[:, cs] = gp
            up_ref[:, cs] = up
            _, _, gc = _ffn_conv(gp, halo_s[:, cs], cw_ref, cb_ref, cs)
            halo_s[:, cs] = gp[TILE - 8:]
            act_s[:, cs] = _mx(gc * _sigmoid(gc) * up)
        r2 = ALPHA * x1 + _dot(act_s[...], wd_ref[...])
        r_ref[...] = r2
        xhat, _ = _ln_stats(r2)
        x2_ref[...] = xhat * g_ref[...] + be_ref[...]

    r = functools.partial(_rows_spec, n_tiles=nt)
    return _call(
        body, "ffn_fwd", nt,
        [r(TILE, D)] + [_whole()] * 7,
        [r(TILE, D_FF), r(TILE, D_FF), r(TILE, D), r(TILE, D)],
        [_sds((n, D_FF)), _sds((n, D_FF)), _sds((n, D)), _sds((n, D))],
        scratch=[pltpu.VMEM((8, D_FF), F32), pltpu.VMEM((TILE, D_FF), MXU)],
    )(x1, wg, wu, cw, cb, wd, g2, b2)


def _ffn_bwd_down(dx2, r2, g2, gp, up, cw, cb, wd_t):
    n = dx2.shape[0]
    nt = n // TILE

    def body(dx_ref, r_ref, g_ref, gp_ref, gh_ref, up_ref, cw_ref, cb_ref, wd_ref,
             dr_ref, dgp_ref, dup_ref, dwd_ref, dcw0, dcw1, dcw2, dcb_ref, dg_ref, db_ref, next_s):
        first_tile = pl.program_id(0) == nt - 1

        @pl.when(pl.program_id(0) == 0)
        def _():
            for ref in (dwd_ref, dcw0, dcw1, dcw2, dcb_ref, dg_ref, db_ref, next_s):
                ref[...] = jnp.zeros_like(ref)

        dx2 = dx_ref[...]
        xhat, rstd = _ln_stats(r_ref[...])
        dg_ref[...] += _sum0(dx2 * xhat)
        db_ref[...] += _sum0(dx2)
        dr2 = _ln_bwd(dx2, g_ref[...], xhat, rstd)
        dr_ref[...] = dr2
        dfb = _mx(dr2)
        for c in range(D_FF // FF_CHUNK):
            cs = slice(c * FF_CHUNK, (c + 1) * FF_CHUNK)
            gp = gp_ref[:, cs]
            up = up_ref[:, cs]
            s1, s2, gc = _ffn_conv(gp, jnp.where(first_tile, 0.0, gh_ref[:, cs]), cw_ref, cb_ref, cs)
            sg = _sigmoid(gc)
            silu = gc * sg
            dact = _dot(dfb, wd_ref[:, cs])
            dwd_ref[cs, :] += _dot_tn(_mx(silu * up), dfb)
            dup_ref[:, cs] = _mx(dact * silu)
            dgc = dact * up * (sg + silu * (1.0 - sg))
            dcb_ref[:, cs] += _sum0(dgc)
            dcw2[:, cs] += _sum0(dgc * gp)
            dcw1[:, cs] += _sum0(dgc * s1)
            dcw0[:, cs] += _sum0(dgc * s2)
            ext = jnp.concatenate([dgc, next_s[:, cs]], axis=0)
            dgp_ref[:, cs] = _mx(cw_ref[2:3, cs] * dgc + cw_ref[1:2, cs] * _shift_up(ext, 1, TILE)
                                 + cw_ref[0:1, cs] * _shift_up(ext, 2, TILE))
            next_s[:, cs] = dgc[:8]

    r = functools.partial(_rows_spec, n_tiles=nt, reverse=True)
    vff = _sds((1, D_FF))
    return _call(
        body, "ffn_bwd_down", nt,
        [r(TILE, D), r(TILE, D), _whole(), r(TILE, D_FF), _halo_spec(D_FF, TILE, nt, reverse=True), r(TILE, D_FF), _whole(), _whole(),
         _whole()],
        [r(TILE, D), r(TILE, D_FF), r(TILE, D_FF)] + [_whole()] * 7,
        [_sds((n, D)), _sds((n, D_FF), MXU), _sds((n, D_FF), MXU), _sds((D_FF, D)), vff, vff, vff, vff, _sds((1, D)), _sds((1, D))],
        scratch=[pltpu.VMEM((8, D_FF), F32)],
    )(dx2, r2, g2, gp, gp, up, cw, cb, wd_t)


def _ffn_bwd_dx(dr2, dgp, dup, r1, g1, wg_t, wu_t):
    n = dr2.shape[0]
    rows = TILE_BIG
    nt = n // rows

    def body(dr2_ref, dgp_ref, dup_ref, r_ref, g_ref, wg_ref, wu_ref, dr1_ref, dg_ref, db_ref):
        @pl.when(pl.program_id(0) == 0)
        def _():
            for ref in (dg_ref, db_ref):
                ref[...] = jnp.zeros_like(ref)

        dx1 = ALPHA * dr2_ref[...] + _dot(dgp_ref[...], wg_ref[...]) + _dot(dup_ref[...], wu_ref[...])
        xhat, rstd = _ln_stats(r_ref[...])
        dg_ref[...] += _sum0(dx1 * xhat)
        db_ref[...] += _sum0(dx1)
        dr1_ref[...] = _ln_bwd(dx1, g_ref[...], xhat, rstd)

    r = functools.partial(_rows_spec, n_tiles=nt)
    return _call(
        body, "ffn_bwd_dx", nt,
        [r(rows, D), r(rows, D_FF), r(rows, D_FF), r(rows, D), _whole(), _whole(), _whole()],
        [r(rows, D), _whole(), _whole()],
        [_sds((n, D)), _sds((1, D)), _sds((1, D))],
    )(dr2, dgp, dup, r1, g1, wg_t, wu_t)


def _ffn_bwd_dw(x1, dgp, dup):
    n = x1.shape[0]
    rows = TILE_BIG
    nt = n // rows

    def body(x_ref, dgp_ref, dup_ref, dwg_ref, dwu_ref):
        @pl.when(pl.program_id(0) == 0)
        def _():
            for ref in (dwg_ref, dwu_ref):
                ref[...] = jnp.zeros_like(ref)

        xb = _mx(x_ref[...])
        for c in range(D_FF // FF_CHUNK):
            cs = slice(c * FF_CHUNK, (c + 1) * FF_CHUNK)
            dwg_ref[cs, :] += _dot_tn(dgp_ref[:, cs], xb)
            dwu_ref[cs, :] += _dot_tn(dup_ref[:, cs], xb)

    r = functools.partial(_rows_spec, n_tiles=nt)
    return _call(
        body, "ffn_bwd_dw", nt,
        [r(rows, D), r(rows, D_FF), r(rows, D_FF)], [_whole(), _whole()], [_sds((D_FF, D)), _sds((D_FF, D))],
    )(x1, dgp, dup)


def _loss_head(y, target):
    n = y.shape[0]
    nt = n // TILE_WIDE

    def body(y_ref, t_ref, loss_ref, dy_ref):
        @pl.when(pl.program_id(0) == 0)
        def _():
            loss_ref[...] = jnp.zeros_like(loss_ref)

        e = y_ref[...] - t_ref[...]
        dy_ref[...] = e * (1.0 / D)
        loss_ref[...] += _sum0(jnp.sum(e * e, axis=1, keepdims=True)) * (0.5 / D)

    r = functools.partial(_rows_spec, n_tiles=nt)
    return _call(body, "loss_head", nt, [r(TILE_WIDE, D), r(TILE_WIDE, D)], [_whole(), r(TILE_WIDE, D)],
                 [_sds((1, 1)), _sds((n, D))])(y, target)


def _place():
    x, y, c = lax.axis_index("x"), lax.axis_index("y"), lax.axis_index("c")
    return x, y, c, 4 * x + 2 * y + c


def _peer(x, y, c, k):
    px, py, pc = x ^ ((k >> 2) & 1), y ^ ((k >> 1) & 1), c ^ (k & 1)
    return (px, py, pc), 4 * px + 2 * py + pc


def _all_gather(blocks, small):
    srcs = list(blocks) + [small]
    n = len(srcs)
    out_shapes = [_sds((a.shape[0], N_DEV * a.shape[1], LANES), a.dtype) for a in blocks] + [_sds((N_DEV,) + small.shape, small.dtype)]

    def body(*refs):
        src_refs, out_refs = refs[:n], refs[n:2 * n]
        send_sems, recv_sems, local_sems = refs[2 * n:]
        x, y, c, me = _place()

        def landing(a, slot):
            if a == n - 1:
                return out_refs[a].at[slot]
            r = src_refs[a].shape[1]
            return out_refs[a].at[:, pl.ds(slot * r, r), :]

        def remote(a, k, slot):
            peer, _ = _peer(x, y, c, k)
            return pltpu.make_async_remote_copy(
                src_ref=src_refs[a], dst_ref=landing(a, slot), send_sem=send_sems.at[a * N_DEV + k],
                recv_sem=recv_sems.at[a * N_DEV + k], device_id=peer, device_id_type=pl.DeviceIdType.MESH)

        mine = [pltpu.make_async_copy(src_refs[a], landing(a, me), local_sems.at[a]) for a in range(n)]
        sends = [remote(a, k, me) for a in range(n) for k in range(1, N_DEV)]
        for cp in mine + sends:
            cp.start()
        for a in range(n):
            for k in range(1, N_DEV):
                remote(a, k, _peer(x, y, c, k)[1]).wait_recv()
        for cp in sends:
            cp.wait_send()
        for cp in mine:
            cp.wait()

    any_space = pl.BlockSpec(memory_space=pl.ANY)
    return pl.pallas_call(
        body, name="gather_weights", out_shape=out_shapes, in_specs=[any_space] * n, out_specs=[any_space] * n,
        scratch_shapes=[pltpu.SemaphoreType.DMA((n * N_DEV,)), pltpu.SemaphoreType.DMA((n * N_DEV,)), pltpu.SemaphoreType.DMA((n,))],
    )(*srcs)


def _exchange_grads(big, small):
    nb = len(big)
    rows_a = [a.shape[0] // N_DEV for a in big] + [small.shape[1]]
    offs = [sum(rows_a[:a]) for a in range(nb + 1)]
    total = sum(rows_a)

    def body(*refs):
        src_refs, out_ref = refs[:nb + 1], refs[nb + 1]
        send_sems, recv_sems, local_sem = refs[nb + 2:]
        x, y, c, me = _place()

        def pieces(slot):
            out = []
            for a in range(nb + 1):
                src = src_refs[a].at[slot] if a == nb else src_refs[a].at[pl.ds(slot * rows_a[a], rows_a[a]), :]
                out.append((src, out_ref.at[me, pl.ds(offs[a], rows_a[a]), :]))
            return out

        for src, dst in pieces(me):
            pltpu.make_async_copy(src, dst, local_sem).start()
        for k in range(1, N_DEV):
            peer, peer_slot = _peer(x, y, c, k)
            for src, dst in pieces(peer_slot):
                pltpu.make_async_remote_copy(src_ref=src, dst_ref=dst, send_sem=send_sems.at[k], recv_sem=recv_sems.at[k],
                                             device_id=peer, device_id_type=pl.DeviceIdType.MESH).start()
        slots = []
        for k in range(1, N_DEV):
            peer, peer_slot = _peer(x, y, c, k)
            slots.append(pltpu.make_async_remote_copy(
                src_ref=out_ref.at[me], dst_ref=out_ref.at[peer_slot], send_sem=send_sems.at[k], recv_sem=recv_sems.at[k],
                device_id=peer, device_id_type=pl.DeviceIdType.MESH))
        for cp in slots:
            cp.wait_recv()
        for cp in slots:
            cp.wait_send()
        pltpu.make_async_copy(out_ref.at[me], out_ref.at[me], local_sem).wait()

    any_space = pl.BlockSpec(memory_space=pl.ANY)
    return pl.pallas_call(
        body, name="exchange_grads", out_shape=_sds((N_DEV, total, LANES)), in_specs=[any_space] * (nb + 1), out_specs=any_space,
        scratch_shapes=[pltpu.SemaphoreType.DMA((N_DEV,)), pltpu.SemaphoreType.DMA((N_DEV,)), pltpu.SemaphoreType.DMA(())],
    )(*big, small)


_HBM = pl.BlockSpec(memory_space=pltpu.HBM)
_SEM = pl.BlockSpec(memory_space=pltpu.SEMAPHORE)
_EFFECT = pltpu.SideEffectType.DATAFLOW_SIDE_EFFECTING


def _in_hbm(a):
    return pltpu.with_memory_space_constraint(a, pltpu.HBM)


def _scatter_start(name, srcs):
    ns = len(srcs)
    rows_a = [a.shape[0] // N_DEV for a in srcs]
    offs = [sum(rows_a[:a]) for a in range(ns)]
    total = sum(rows_a)

    def body(*refs):
        src_refs, land_ref, send_sems, recv_sems, token = refs[:ns], refs[ns], refs[ns + 1], refs[ns + 2], refs[-1]
        x, y, c, me = _place()
        for k in range(1, N_DEV):
            peer, peer_slot = _peer(x, y, c, k)
            for a in range(ns):
                pltpu.make_async_remote_copy(
                    src_ref=src_refs[a].at[pl.ds(peer_slot * rows_a[a], rows_a[a]), :],
                    dst_ref=land_ref.at[me, pl.ds(offs[a], rows_a[a]), :], send_sem=send_sems.at[k], recv_sem=recv_sems.at[k],
                    device_id=peer, device_id_type=pl.DeviceIdType.MESH).start()
        token[...] = jnp.zeros_like(token)

    landing = lax.empty((N_DEV, total, LANES), F32)
    out = pl.pallas_call(
        body, name=name,
        out_shape=(pltpu.SemaphoreType.DMA((N_DEV,)), pltpu.SemaphoreType.DMA((N_DEV,)), *[pltpu.HBM(a.shape, a.dtype) for a in srcs],
                   pltpu.HBM(landing.shape, F32), _sds((8, 128))),
        in_specs=[_HBM] * (ns + 1), out_specs=(_SEM, _SEM, *[_HBM] * (ns + 1), pl.BlockSpec(memory_space=pltpu.VMEM)),
        input_output_aliases={a: 2 + a for a in range(ns + 1)},
        compiler_params=pltpu.CompilerParams(has_side_effects=_EFFECT),
    )(*[_in_hbm(a) for a in srcs], _in_hbm(landing))
    return out[0], out[1], out[2:2 + ns], out[2 + ns], out[-1]


def _scatter_wait(name, send_sems, recv_sems, srcs, landing, after):
    ns = len(srcs)

    def body(*refs):
        land_ref, send_ref, recv_ref = refs[ns], refs[ns + 1], refs[ns + 2]
        x, y, c, me = _place()
        for k in range(1, N_DEV):
            peer, peer_slot = _peer(x, y, c, k)
            slot = pltpu.make_async_remote_copy(
                src_ref=land_ref.at[me], dst_ref=land_ref.at[peer_slot], send_sem=send_ref.at[k], recv_sem=recv_ref.at[k],
                device_id=peer, device_id_type=pl.DeviceIdType.MESH)
            slot.wait_send()
            slot.wait_recv()

    out = pl.pallas_call(
        body, name=name, out_shape=(*[pltpu.HBM(a.shape, a.dtype) for a in srcs], pltpu.HBM(landing.shape, landing.dtype)),
        in_specs=[_HBM] * (ns + 1) + [_SEM, _SEM, pl.BlockSpec(memory_space=pl.ANY)], out_specs=[_HBM] * (ns + 1),
        input_output_aliases={a: a for a in range(ns + 1)},
        compiler_params=pltpu.CompilerParams(has_side_effects=_EFFECT),
    )(*srcs, landing, send_sems, recv_sems, after)
    return out[:ns], out[ns]


def _gather_start(name, blocks):
    n = len(blocks)

    def body(*refs):
        src_refs, land_refs, send_sems, recv_sems, token = refs[:n], refs[n:2 * n], refs[2 * n], refs[2 * n + 1], refs[-1]
        x, y, c, me = _place()
        for a in range(n):
            r = src_refs[a].shape[1]
            for k in range(1, N_DEV):
                pltpu.make_async_remote_copy(
                    src_ref=src_refs[a], dst_ref=land_refs[a].at[:, pl.ds(me * r, r), :], send_sem=send_sems.at[a * N_DEV + k],
                    recv_sem=recv_sems.at[a * N_DEV + k], device_id=_peer(x, y, c, k)[0], device_id_type=pl.DeviceIdType.MESH).start()
        token[...] = jnp.zeros_like(token)

    wholes = [lax.empty((a.shape[0], N_DEV * a.shape[1], LANES), a.dtype) for a in blocks]
    out = pl.pallas_call(
        body, name=name,
        out_shape=(pltpu.SemaphoreType.DMA((n * N_DEV,)), pltpu.SemaphoreType.DMA((n * N_DEV,)),
                   *[pltpu.HBM(a.shape, a.dtype) for a in blocks + wholes], _sds((8, 128))),
        in_specs=[_HBM] * (2 * n), out_specs=(_SEM, _SEM, *[_HBM] * (2 * n), pl.BlockSpec(memory_space=pltpu.VMEM)),
        input_output_aliases={a: 2 + a for a in range(2 * n)},
        compiler_params=pltpu.CompilerParams(has_side_effects=_EFFECT),
    )(*[_in_hbm(a) for a in blocks + wholes])
    return out[0], out[1], out[2:2 + n], out[2 + n:2 + 2 * n], out[-1]


def _gather_wait(name, send_sems, recv_sems, blocks, wholes, after):
    n = len(blocks)

    def body(*refs):
        src_refs, land_refs, send_ref, recv_ref = refs[:n], refs[n:2 * n], refs[2 * n], refs[2 * n + 1]
        x, y, c, me = _place()
        for a in range(n):
            r = src_refs[a].shape[1]
            for k in range(1, N_DEV):
                peer, peer_slot = _peer(x, y, c, k)
                cp = pltpu.make_async_remote_copy(
                    src_ref=src_refs[a], dst_ref=land_refs[a].at[:, pl.ds(peer_slot * r, r), :], send_sem=send_ref.at[a * N_DEV + k],
                    recv_sem=recv_ref.at[a * N_DEV + k], device_id=peer, device_id_type=pl.DeviceIdType.MESH)
                cp.wait_send()
                cp.wait_recv()

    return pl.pallas_call(
        body, name=name, out_shape=tuple(pltpu.HBM(a.shape, a.dtype) for a in list(blocks) + list(wholes)),
        in_specs=[_HBM] * (2 * n) + [_SEM, _SEM, pl.BlockSpec(memory_space=pl.ANY)], out_specs=[_HBM] * (2 * n),
        input_output_aliases={a: a for a in range(2 * n)},
        compiler_params=pltpu.CompilerParams(has_side_effects=_EFFECT),
    )(*blocks, *wholes, send_sems, recv_sems, after)


def _reduce_adamw(parts, w, m, v, tile_rows=PACK_TILE):
    rows = w.shape[0]
    nt = rows // tile_rows
    c1 = 1.0 - ADAM_B1 ** ADAM_STEP
    c2 = 1.0 - ADAM_B2 ** ADAM_STEP

    def body(p_ref, w_ref, m_ref, v_ref, g_out, d_out, m_out, v_out):
        g = p_ref[0]
        for s in range(1, N_DEV):
            g = g + p_ref[s]
        m_new = ADAM_B1 * m_ref[...] + (1.0 - ADAM_B1) * g
        v_new = ADAM_B2 * v_ref[...] + (1.0 - ADAM_B2) * (g * g)
        g_out[...] = g
        m_out[...] = m_new
        v_out[...] = v_new
        d_out[...] = -ADAM_LR * ((m_new / c1) / (jnp.sqrt(v_new / c2) + ADAM_EPS) + ADAM_WD * w_ref[...])

    r = _rows_spec(tile_rows, LANES, nt)
    out = _sds((rows, LANES))
    return _call(
        body, "reduce_adamw", nt,
        [pl.BlockSpec((N_DEV, tile_rows, LANES), lambda i: (0, i, 0)), r, r, r], [r, r, r, r], [out, out, out, out],
    )(parts, w, m, v)


def _pack_rows(a, lead=0):
    head = a.shape[:lead]
    flat = a.reshape(head + (-1,))
    size = flat.shape[-1]
    rows = -(-size // (16 * LANES)) * 16
    flat = jnp.pad(flat, [(0, 0)] * lead + [(0, rows * LANES - size)])
    return flat.reshape(head + (rows, LANES))


def _packed_rows(shape):
    return -(-math.prod(shape) // (16 * LANES)) * 16


def _to_blocks(full, axis):
    l, a, b = full.shape
    if axis == 2:
        return full.reshape(l, a, N_DEV, b // N_DEV).transpose(2, 0, 1, 3)
    return full.reshape(l, N_DEV, a // N_DEV, b).transpose(1, 0, 2, 3)


def _from_blocks(blocks, axis):
    _, l, a, b = blocks.shape
    if axis == 2:
        return blocks.transpose(1, 2, 0, 3).reshape(l, a, N_DEV * b)
    return blocks.transpose(1, 0, 2, 3).reshape(l, N_DEV * a, b)


def _row_form(shard, transposed):
    return shard.transpose(0, 2, 1) if transposed else shard


def _me():
    return 4 * lax.axis_index("x") + 2 * lax.axis_index("y") + lax.axis_index("c")


def _both_forms(names, wholes):
    out = {}
    for name, w in zip(names, wholes):
        t = dict(BIG)[name]
        out[name + '_t' if t else name] = w
        out[name if t else name + '_t'] = w.transpose(0, 2, 1)
    return out


def _gather_weights(local):
    segs, meta = [], []
    for name in SMALL_SHARDED:
        blk = local[name]
        if name in GATHER_F32:
            bits = lax.bitcast_convert_type(blk, MXU)
        else:
            bits = _mx(blk)
        seg = _pack_rows(bits)
        meta.append((name, bits.shape, seg.shape[0]))
        segs.append(seg)
    blocks = {name: _mx(_row_form(local[name], t)) for name, t in BIG}
    mix, ffn = PARTS['mix'], PARTS['ffn']
    *first, gathered = _all_gather([blocks[n][:1] for n in mix], jnp.concatenate(segs, axis=0))
    flights = {'ffn0': _gather_start("gather_ffn0_start", [blocks[n][:1] for n in ffn]),
               'later': _gather_start("gather_later_start", [blocks[n][1:] for n in mix + ffn])}
    out, lo = {}, 0
    for name, bits_shape, rows in meta:
        seg = gathered[:, lo:lo + rows].reshape(N_DEV, -1)[:, :math.prod(bits_shape)].reshape((N_DEV,) + bits_shape)
        if name in GATHER_F32:
            seg = lax.bitcast_convert_type(seg, F32)
        out[name] = _from_blocks(seg, SHARD_AXIS[name])
        lo += rows
    ready = {(0, 'mix'): {k: v[0] for k, v in _both_forms(mix, first).items()}}

    def landed(flight, names, after):
        send_sems, recv_sems, mine, wholes, _ = flights[flight]
        done = _gather_wait(f"gather_{flight}_wait", send_sems, recv_sems, mine, wholes, after)
        own = [lax.dynamic_update_slice(w, b, (0, _me() * b.shape[1], 0)) for b, w in zip(done[:len(names)], done[len(names):])]
        return _both_forms(names, own)

    def big_weights(l, part, after):
        if (l, part) not in ready and l == 0:
            ready[(0, 'ffn')] = {k: v[0] for k, v in landed('ffn0', ffn, after).items()}
        elif (l, part) not in ready:
            forms = landed('later', mix + ffn, after)
            for j in range(1, DEPTH):
                for p, names in PARTS.items():
                    ready[(j, p)] = {k: forms[k][j - 1] for n in names for k in (n, n + '_t')}
        return ready[(l, part)]

    return out, big_weights, flights['ffn0'][-1][0, 0] + flights['later'][-1][0, 0]


def _s5_discretize(a_re, a_im, log_dt, b_re, b_im):
    lam_re = jnp.minimum(a_re, -1e-4)
    lam_im = a_im
    dt = jnp.exp(log_dt)[:, None]
    decay = jnp.exp(dt * lam_re)
    ang = dt * lam_im
    abar_re = decay * jnp.cos(ang)
    abar_im = decay * jnp.sin(ang)
    den = jnp.square(lam_re) + jnp.square(lam_im)
    nr = abar_re - 1.0
    ni = abar_im
    coef_re = (nr * lam_re + ni * lam_im) / den
    coef_im = (ni * lam_re - nr * lam_im) / den
    bbar_re = coef_re[..., None] * b_re - coef_im[..., None] * b_im
    bbar_im = coef_re[..., None] * b_im + coef_im[..., None] * b_re
    return abar_re, abar_im, bbar_re, bbar_im


def _complex_powers(ar, ai, count):
    def combine(e1, e2):
        return e2[0] * e1[0] - e2[1] * e1[1], e2[0] * e1[1] + e2[1] * e1[0]

    shape = (count,) + ar.shape
    return lax.associative_scan(combine, (jnp.broadcast_to(ar, shape), jnp.broadcast_to(ai, shape)), axis=0)


_EYE16 = functools.partial(jnp.eye, 16, dtype=F32)


def _s5_params(p, l):
    disc, disc_vjp = jax.vjp(_s5_discretize, p['s5_a_re'][l], p['s5_a_im'][l], p['s5_log_dt'][l], p['s5_b_re'][l], p['s5_b_im'][l])
    abar_re, abar_im, bbar_re, bbar_im = disc
    ar, ai = abar_re.reshape(N_STATE), abar_im.reshape(N_STATE)
    ap_re, ap_im = _complex_powers(ar, ai, TILE)
    one, zero = jnp.ones((1, N_STATE), F32), jnp.zeros((1, N_STATE), F32)
    seg_re = jnp.concatenate([one, ap_re[SEG - 1:TILE - SEG:SEG]], axis=0)
    seg_im = jnp.concatenate([zero, ap_im[SEG - 1:TILE - SEG:SEG]], axis=0)
    doubling = [SEG - 1, 2 * SEG - 1, 4 * SEG - 1]

    src = (jnp.arange(TILE) % N_SEG) * SEG + jnp.arange(TILE) // N_SEG
    perm = (src[:, None] == jnp.arange(TILE)[None, :]).astype(MXU)
    prm = {
        'perm': perm, 'perm_t': perm.T,
        'pw_re': ap_re[:SEG], 'pw_im': ap_im[:SEG],
        'dbl_re': jnp.concatenate([jnp.stack([ap_re[k] for k in doubling]), jnp.zeros((5, N_STATE), F32)], axis=0),
        'dbl_im': jnp.concatenate([jnp.stack([ap_im[k] for k in doubling]), jnp.zeros((5, N_STATE), F32)], axis=0),
        'seg_re': seg_re, 'seg_im': seg_im, 'segr_re': seg_re[::-1], 'segr_im': seg_im[::-1],
        'b_re': _mx(jnp.einsum('gpc,gh->gchp', bbar_re, _EYE16()).reshape(D_S5, N_STATE)),
        'b_im': _mx(jnp.einsum('gpc,gh->gchp', bbar_im, _EYE16()).reshape(D_S5, N_STATE)),
        'c_re': _mx(jnp.einsum('gcp,gh->gphc', p['s5_c_re'][l], _EYE16()).reshape(N_STATE, D_S5)),
        'c_im': _mx(jnp.einsum('gcp,gh->gphc', p['s5_c_im'][l], _EYE16()).reshape(N_STATE, D_S5)),
        'd': p['s5_d'][l][None, :], 'glu_w': p['s5_glu_w'][l], 'glu_b': p['s5_glu_b'][l][None, :],
    }
    return prm, disc_vjp


def _lru_params(p, l):
    eye4 = jnp.eye(4, dtype=F32)
    return {
        'conv_w': p['lru_conv_w'][l], 'conv_b': p['lru_conv_b'][l][None, :],
        'wx': _mx(jnp.einsum('hij,hk->hikj', p['lru_wx'][l], eye4).reshape(D_LRU, D_LRU)),
        'wa': _mx(jnp.einsum('hij,hk->hikj', p['lru_wa'][l], eye4).reshape(D_LRU, D_LRU)),
        'bx': p['lru_bx'][l][None, :], 'ba': p['lru_ba'][l][None, :],
        'sp': jax.nn.softplus(-p['lru_a_param'][l])[None, :],
    }


def _rope_tables(n):
    inv_freq = ROPE_THETA ** (-jnp.arange(0, 64, 2, dtype=F32) / 64)
    ang = jnp.arange(n, dtype=F32)[:, None] * inv_freq[None, :]
    cos, sin = jnp.cos(ang), jnp.sin(ang)
    return jnp.concatenate([cos, cos, cos, cos], axis=1), jnp.concatenate([-sin, sin, -sin, sin], axis=1)


def _sink_cols(sinks):
    nb = TILE_Q // ATTN_BLOCK
    per_unit = sinks.reshape(4, 2).T
    return jnp.broadcast_to(per_unit[:, None, :, None], (2, nb, 4, ATTN_BLOCK)).reshape(2, nb * 4 * ATTN_BLOCK, 1)


def _local_step(x, target, p, big_weights=None, emit_grads=None):
    if big_weights is None:
        big_weights = lambda l, part, after: {k: p[k][l] for name in PARTS[part] for k in (name, name + '_t')}
    if emit_grads is None:
        emit_grads = lambda l, part, grads: 0.0
    n = x.shape[0]
    cos_t, sin_t = _rope_tables(n)
    row = lambda a: a[None, :]
    saved = []
    h = x
    for l in range(DEPTH):
        s = {'x0': h}
        bw = s['bw'] = dict(big_weights(l, 'mix', h))
        s['q'], s['k'], s['v'], s['u'], s['xr'], s['gate'] = _inproj_fwd(h, bw['w_in'], row(p['b_in'][l]), cos_t, sin_t)
        s['sinks'] = _sink_cols(p['attn_sinks'][l])
        s['ya'] = _attn_fwd(s['q'], s['k'], s['v'], s['sinks'])
        s['s5'], s['s5_vjp'] = _s5_params(p, l)
        s['lru'] = _lru_params(p, l)
        s['ys'], s['s5_cr'], s['s5_ci'] = _s5_fwd(s['u'], s['s5'])
        s['yl'], s['lru_c'] = _lru_fwd(s['xr'], s['gate'], s['lru'])
        s['mix'], s['r1'], s['x1'] = _mixout_fwd(s['ya'], s['ys'], s['yl'], h, row(p['mix_norm_g'][l]), bw['w_out'],
                                                 row(p['b_out'][l]), row(p['ln1_g'][l]), row(p['ln1_b'][l]))
        bw.update(big_weights(l, 'ffn', s['x1']))
        s['gp'], s['up'], s['r2'], h = _ffn_fwd(s['x1'], bw['ffn_w_gate'], bw['ffn_w_up'], p['ffn_conv_w'][l],
                                                row(p['ffn_conv_b'][l]), bw['ffn_w_down'], row(p['ln2_g'][l]), row(p['ln2_b'][l]))
        saved.append(s)
    loss, dh = _loss_head(h, target)
    placed = 0.0

    grads = {name: [None] * DEPTH for name in WEIGHTS}
    for l in reversed(range(DEPTH)):
        s = saved[l]
        g = {}
        (dr2, dgp, dup, g['ffn_w_down'], cw0, cw1, cw2, dcb, dg2, db2) = _ffn_bwd_down(
            dh, s['r2'], row(p['ln2_g'][l]) + placed, s['gp'], s['up'], p['ffn_conv_w'][l], row(p['ffn_conv_b'][l]),
            s['bw']['ffn_w_down_t'])
        g['ffn_conv_w'] = jnp.concatenate([cw0, cw1, cw2], axis=0)
        g['ffn_conv_b'], g['ln2_g'], g['ln2_b'] = dcb[0], dg2[0], db2[0]
        dr1, dg1, db1 = _ffn_bwd_dx(dr2, dgp, dup, s['r1'], row(p['ln1_g'][l]), s['bw']['ffn_w_gate_t'], s['bw']['ffn_w_up_t'])
        g['ffn_w_gate'], g['ffn_w_up'] = _ffn_bwd_dw(s['x1'], dgp, dup)
        g['ln1_g'], g['ln1_b'] = dg1[0], db1[0]
        placed = emit_grads(l, 'ffn', [g[name] for name in PARTS['ffn']])
        dya, dys, dyl, g['w_out'], dbo, dgm = _mixout_bwd(dr1, s['mix'], s['ya'], s['ys'], s['yl'],
                                                         row(p['mix_norm_g'][l]) + placed, s['bw']['w_out_t'])
        g['b_out'], g['mix_norm_g'] = dbo[0], dgm[0]

        du, dbr, dbi, dcr, dci, dar, dai, dd, g['s5_glu_w'], dgb = _s5_bwd(s['u'], dys, s['s5_cr'], s['s5_ci'], s['s5'])
        dxr, dgate, lw0, lw1, lw2, lw3, lcb, dwx, dwa, dbx, dba, dsp = _lru_bwd(s['xr'], s['gate'], dyl, s['lru_c'], s['lru'])
        g['lru_conv_w'] = jnp.concatenate([lw0, lw1, lw2, lw3], axis=0)
        g['lru_conv_b'], g['lru_bx'], g['lru_ba'] = lcb[0], dbx[0], dba[0]
        g['lru_wx'] = jnp.einsum('hihj->hij', dwx.reshape(4, 64, 4, 64))
        g['lru_wa'] = jnp.einsum('hihj->hij', dwa.reshape(4, 64, 4, 64))
        g['lru_a_param'] = -dsp[0] * jax.nn.sigmoid(-p['lru_a_param'][l])

        g['s5_c_re'] = jnp.einsum('gpgc->gcp', dcr.reshape(16, 64, 16, 16))
        g['s5_c_im'] = jnp.einsum('gpgc->gcp', dci.reshape(16, 64, 16, 16))
        g['s5_d'], g['s5_glu_b'] = dd[0], dgb[0]
        g['s5_a_re'], g['s5_a_im'], g['s5_log_dt'], g['s5_b_re'], g['s5_b_im'] = s['s5_vjp']((
            dar.reshape(16, 64), dai.reshape(16, 64), jnp.einsum('gcgp->gpc', dbr.reshape(16, 16, 16, 64)),
            jnp.einsum('gcgp->gpc', dbi.reshape(16, 16, 16, 64))))

        dq, dk, dv, dsink = _attn_bwd(s['q'], s['k'], s['v'], s['sinks'], s['ya'], dya)
        g['attn_sinks'] = dsink[:, 0]
        dh, g['w_in'], dbin = _inproj_bwd(dq, dk, dv, du, dxr, dgate, cos_t, sin_t, s['x0'], dr1, s['bw']['w_in_t'])
        g['b_in'] = dbin[0]
        placed = emit_grads(l, 'mix', [g[name] for name in PARTS['mix']])
        for name in WEIGHTS:
            grads[name][l] = g[name]
    big = dict(BIG)
    return loss, dh, {name: grads[name] if name in big else jnp.stack(grads[name]) for name in WEIGHTS}


def kernel(x, w_in, b_in, attn_sinks, s5_a_re, s5_a_im, s5_b_re, s5_b_im, s5_c_re, s5_c_im, s5_d, s5_log_dt, s5_glu_w, s5_glu_b, lru_conv_w, lru_conv_b, lru_wx, lru_bx, lru_wa, lru_ba, lru_a_param, mix_norm_g, w_out, b_out, ln1_g, ln1_b, ffn_w_gate, ffn_w_up, ffn_conv_w, ffn_conv_b, ffn_w_down, ln2_g, ln2_b, loss_target, m_w_in, m_b_in, m_attn_sinks, m_s5_a_re, m_s5_a_im, m_s5_b_re, m_s5_b_im, m_s5_c_re, m_s5_c_im, m_s5_d, m_s5_log_dt, m_s5_glu_w, m_s5_glu_b, m_lru_conv_w, m_lru_conv_b, m_lru_wx, m_lru_bx, m_lru_wa, m_lru_ba, m_lru_a_param, m_mix_norm_g, m_w_out, m_b_out, m_ln1_g, m_ln1_b, m_ffn_w_gate, m_ffn_w_up, m_ffn_conv_w, m_ffn_conv_b, m_ffn_w_down, m_ln2_g, m_ln2_b, v_w_in, v_b_in, v_attn_sinks, v_s5_a_re, v_s5_a_im, v_s5_b_re, v_s5_b_im, v_s5_c_re, v_s5_c_im, v_s5_d, v_s5_log_dt, v_s5_glu_w, v_s5_glu_b, v_lru_conv_w, v_lru_conv_b, v_lru_wx, v_lru_bx, v_lru_wa, v_lru_ba, v_lru_a_param, v_mix_norm_g, v_w_out, v_b_out, v_ln1_g, v_ln1_b, v_ffn_w_gate, v_ffn_w_up, v_ffn_conv_w, v_ffn_conv_b, v_ffn_w_down, v_ln2_g, v_ln2_b):
    given = dict(locals())
    whole = {name: given[name] for name in WEIGHTS if name not in dict(BIG)}
    small_whole, big_weights, placed = _gather_weights({name: given[name] for name in SHARDED})
    whole.update(small_whole)
    whole['b_in'] = whole['b_in'] + placed
    in_flight = {}

    def emit_grads(l, part, grads):
        in_flight[(l, part)] = _scatter_start(f"grads_start_{part}{l}", grads)
        return in_flight[(l, part)][-1][0, 0]

    loss, grad_x, grads = _local_step(x[0], loss_target[0], whole, big_weights, emit_grads)
    total = lax.psum(loss[0, 0], ("x", "y", "c"))
    return (total, grad_x[None], *_update(given, grads, in_flight, grad_x))


def _update(given, grads, in_flight, after):
    local_w = {name: given[name] for name in WEIGHTS}
    me = _me()
    outs = {}

    forms = {part: [(name, dict(BIG)[name], _row_form(local_w[name], dict(BIG)[name]).shape[1]) for name in names]
             for part, names in PARTS.items()}
    for (l, part), (send_sems, recv_sems, srcs, landing, _) in in_flight.items():
        srcs, landing = _scatter_wait(f"grads_wait_{part}{l}", send_sems, recv_sems, srcs, landing, after)
        own = jnp.concatenate([lax.dynamic_slice_in_dim(g, me * r, r, axis=0) for g, (_, _, r) in zip(srcs, forms[part])], axis=0)
        parts = lax.dynamic_update_slice(landing, own[None], (me, 0, 0))
        packed = [jnp.concatenate([_row_form(given[prefix + name], t)[l] for name, t, _ in forms[part]], axis=0)
                  for prefix in ('', 'm_', 'v_')]
        outs[(l, part)] = _reduce_adamw(parts, *packed, tile_rows=own.shape[0] // 4)

    rest = SMALL_SHARDED + REPLICATED
    pad = -sum(_packed_rows(local_w[name].shape) for name in rest) % PACK_TILE
    small = jnp.concatenate(
        [_pack_rows(_to_blocks(grads[name], SHARD_AXIS[name]), lead=1) for name in SMALL_SHARDED]
        + [jnp.broadcast_to(_pack_rows(grads[name]), (N_DEV, _packed_rows(grads[name].shape), LANES)) for name in REPLICATED]
        + [jnp.zeros((N_DEV, pad, LANES), F32)], axis=1)

    def packed_rest(prefix):
        return jnp.concatenate([_pack_rows(given[prefix + name]) for name in rest] + [jnp.zeros((pad, LANES), F32)], axis=0)

    rest_outs = _reduce_adamw(_exchange_grads([], small), packed_rest(''), packed_rest('m_'), packed_rest('v_'))

    def unpack(i):
        res = {}
        for part in PARTS:
            lo = 0
            for name, t, r in forms[part]:
                res[name] = _row_form(jnp.stack([outs[(l, part)][i][lo:lo + r] for l in range(DEPTH)]), t)
                lo += r
        lo = 0
        for name in rest:
            shape = local_w[name].shape
            res[name] = rest_outs[i][lo:lo + _packed_rows(shape)].reshape(-1)[:math.prod(shape)].reshape(shape)
            lo += _packed_rows(shape)
        return [res[name] for name in WEIGHTS]

    return (*unpack(0), *unpack(1), *unpack(2), *unpack(3))
```

```python
import functools
import math

import jax
import jax.numpy as jnp
from jax import lax
from jax.experimental import pallas as pl
from jax.experimental.pallas import tpu as pltpu

F32 = jnp.float32
MXU = jnp.bfloat16

N_DEV = 8
DEPTH = 4
D = 1024
D_ATTN, D_KV, D_S5, D_LRU = 512, 128, 256, 256
D_IN = 1536
D_FF = 2816
FF_CHUNK = 256
N_STATE = 1024
LANES = 1024
ALPHA = (2 * DEPTH) ** 0.25
LN_EPS = 1e-5
RMS_EPS = 1e-6
LRU_C = 8.0
ROPE_THETA = 10000.0
ADAM_LR, ADAM_B1, ADAM_B2, ADAM_EPS, ADAM_WD, ADAM_STEP = 0.001, 0.9, 0.999, 1e-08, 0.01, 10

TILE = 256
S5_PAIR = 2
N_SEG = 8
SEG = TILE // N_SEG
TILE_Q = 512
TILE_BIG = 512
TILE_WIDE = 512
ATTN_BLOCK = 128
PACK_TILE = 256
VMEM_MB = 56

WEIGHTS = ['w_in', 'b_in', 'attn_sinks', 's5_a_re', 's5_a_im', 's5_b_re', 's5_b_im', 's5_c_re', 's5_c_im', 's5_d', 's5_log_dt',
           's5_glu_w', 's5_glu_b', 'lru_conv_w', 'lru_conv_b', 'lru_wx', 'lru_bx', 'lru_wa', 'lru_ba', 'lru_a_param', 'mix_norm_g',
           'w_out', 'b_out', 'ln1_g', 'ln1_b', 'ffn_w_gate', 'ffn_w_up', 'ffn_conv_w', 'ffn_conv_b', 'ffn_w_down', 'ln2_g', 'ln2_b']
SHARD_AXIS = {'w_in': 2, 's5_glu_w': 1, 'lru_conv_w': 2, 'w_out': 1, 'ffn_w_gate': 2, 'ffn_w_up': 2, 'ffn_conv_w': 2,
              'ffn_w_down': 1}
SHARDED = [n for n in WEIGHTS if n in SHARD_AXIS]
REPLICATED = [n for n in WEIGHTS if n not in SHARD_AXIS]
BIG = [(n, SHARD_AXIS[n] == 2) for n in ('w_in', 'w_out', 'ffn_w_gate', 'ffn_w_up', 'ffn_w_down')]
SMALL_SHARDED = [n for n in SHARDED if n not in dict(BIG)]
PARTS = {'mix': ['w_in', 'w_out'], 'ffn': ['ffn_w_gate', 'ffn_w_up', 'ffn_w_down']}
GATHER_F32 = ('lru_conv_w', 'ffn_conv_w')


def _dot(a, b):
    return jnp.dot(a, b, preferred_element_type=F32)


def _dot_nt(a, b):
    return lax.dot_general(a, b, (((1,), (1,)), ((), ())), preferred_element_type=F32)


def _dot_tn(a, b):
    return lax.dot_general(a, b, (((0,), (0,)), ((), ())), preferred_element_type=F32)


def _mx(a):
    return a.astype(MXU)


_GELU_C = math.sqrt(2.0 / math.pi)


def _gelu(x):
    th = jnp.tanh(_GELU_C * (x + 0.044715 * x * x * x))
    return 0.5 * x * (1.0 + th)


def _gelu_grad(x):
    th = jnp.tanh(_GELU_C * (x + 0.044715 * x * x * x))
    return 0.5 * (1.0 + th) + 0.5 * x * (1.0 - th * th) * _GELU_C * (1.0 + 3.0 * 0.044715 * x * x)


def _sigmoid(x):
    return 0.5 * jnp.tanh(0.5 * x) + 0.5


def _ln_stats(r):
    mu = jnp.mean(r, axis=-1, keepdims=True)
    xc = r - mu
    var = jnp.mean(xc * xc, axis=-1, keepdims=True)
    rstd = lax.rsqrt(var + LN_EPS)
    return xc * rstd, rstd


def _ln_bwd(dy, g, xhat, rstd):
    dxh = dy * g
    return rstd * (dxh - jnp.mean(dxh, axis=-1, keepdims=True) - xhat * jnp.mean(dxh * xhat, axis=-1, keepdims=True))


def _rms(y):
    return lax.rsqrt(jnp.mean(y * y, axis=-1, keepdims=True) + RMS_EPS)


def _sum0(a):
    return jnp.sum(a, axis=0, keepdims=True)


def _row_iota(shape):
    return lax.broadcasted_iota(jnp.int32, shape, 0)


def _shift_down(ext, j, rows):
    return pltpu.roll(ext, j, 0)[8:8 + rows]


def _shift_up(ext, j, rows):
    return pltpu.roll(ext, ext.shape[0] - j, 0)[:rows]


def _swap_halves(t):
    w = t.shape[1]
    lane = lax.broadcasted_iota(jnp.int32, t.shape, 1)
    return jnp.where((lane & 32) == 0, pltpu.roll(t, w - 32, 1), pltpu.roll(t, 32, 1))


def _rope(t, cos, sin_signed):
    return t * cos + _swap_halves(t) * sin_signed


def _rope_t(d, cos, sin_signed):
    return d * cos + _swap_halves(d * sin_signed)


def _cmul_add(ar, ai, xr, xi, yr, yi):
    return ar * xr - ai * xi + yr, ar * xi + ai * xr + yi


def _seg_rows(k):
    return slice(N_SEG * k, N_SEG * (k + 1))


def _permute_rows(perm, x):
    hi = _mx(x)
    rest = x - hi.astype(F32)
    mid = _mx(rest)
    lo = _mx(rest - mid.astype(F32))
    return _dot(perm, hi) + _dot(perm, mid) + _dot(perm, lo)


def _cscan(sr, si, tab, cin_r, cin_i, reverse):
    sgn = -1.0 if reverse else 1.0
    pw_re, pw_im, dbl_re, dbl_im = tab['pw_re'], tab['pw_im'], tab['dbl_re'], tab['dbl_im']
    ar, ai = pw_re[0:1, :], sgn * pw_im[0:1, :]
    shape = (N_SEG, sr.shape[1])
    hr = hi = jnp.zeros(shape, F32)
    for k in (range(SEG - 1, -1, -1) if reverse else range(SEG)):
        hr, hi = _cmul_add(ar, ai, hr, hi, sr[_seg_rows(k), :], si[_seg_rows(k), :])
        sr[_seg_rows(k), :] = hr
        si[_seg_rows(k), :] = hi
    sub = _row_iota(shape)

    def shifted(v, d):
        if reverse:
            return jnp.where(sub < N_SEG - d, pltpu.roll(v, N_SEG - d, 0), 0.0)
        return jnp.where(sub >= d, pltpu.roll(v, d, 0), 0.0)

    fr, fi = hr, hi
    for j, d in enumerate((1, 2, 4)):
        fr, fi = _cmul_add(dbl_re[j:j + 1, :], sgn * dbl_im[j:j + 1, :], shifted(fr, d), shifted(fi, d), fr, fi)
    seg_re, seg_im = (tab['segr_re'], tab['segr_im']) if reverse else (tab['seg_re'], tab['seg_im'])
    cr, ci = _cmul_add(seg_re[...], sgn * seg_im[...], cin_r, cin_i, shifted(fr, 1), shifted(fi, 1))
    nr, ni = _cmul_add(dbl_re[0:1, :], sgn * dbl_im[0:1, :], cr, ci, hr, hi)
    for k in range(SEG):
        j = SEG - 1 - k if reverse else k
        xr, xi = _cmul_add(pw_re[j:j + 1, :], sgn * pw_im[j:j + 1, :], cr, ci, sr[_seg_rows(k), :], si[_seg_rows(k), :])
        sr[_seg_rows(k), :] = xr
        si[_seg_rows(k), :] = xi
    edge = slice(0, 1) if reverse else slice(N_SEG - 1, N_SEG)
    return nr[edge], ni[edge]


def _rscan(a, b, reverse):
    rows = a.shape[0]
    row = _row_iota(a.shape)
    s = 1
    while s < rows:
        if reverse:
            keep = row < rows - s
            sa = jnp.where(keep, pltpu.roll(a, rows - s, 0), 1.0)
            sb = jnp.where(keep, pltpu.roll(b, rows - s, 0), 0.0)
        else:
            keep = row >= s
            sa = jnp.where(keep, pltpu.roll(a, s, 0), 1.0)
            sb = jnp.where(keep, pltpu.roll(b, s, 0), 0.0)
        b = b + a * sb
        a = a * sa
        s *= 2
    return a, b


def _whole():
    return pl.BlockSpec(memory_space=pltpu.VMEM)


def _rows_spec(rows, cols, n_tiles, reverse=False):
    if reverse:
        return pl.BlockSpec((rows, cols), lambda i: (n_tiles - 1 - i, 0))
    return pl.BlockSpec((rows, cols), lambda i: (i, 0))


def _halo_spec(cols, tile_rows, n_tiles, reverse=False):
    per = tile_rows // 8
    if reverse:
        return pl.BlockSpec((8, cols), lambda i: (jnp.maximum((n_tiles - 1 - i) * per - 1, 0), 0))
    return pl.BlockSpec((8, cols), lambda i: (jnp.maximum(i * per - 1, 0), 0))


def _call(body, name, n_tiles, in_specs, out_specs, out_shape, scratch=()):
    return pl.pallas_call(
        body, name=name, grid=(n_tiles,), in_specs=in_specs, out_specs=out_specs, out_shape=out_shape,
        scratch_shapes=list(scratch),
        compiler_params=pltpu.CompilerParams(dimension_semantics=("arbitrary",), vmem_limit_bytes=VMEM_MB << 20))


def _sds(shape, dtype=F32):
    return jax.ShapeDtypeStruct(shape, dtype)


def _inproj_fwd(x, w, b, cos_t, sin_t):
    n = x.shape[0]
    nt = n // TILE_WIDE

    def body(x_ref, w_ref, b_ref, c_ref, s_ref, q_ref, k_ref, v_ref, u_ref, xr_ref, g_ref):
        p = _dot(_mx(x_ref[...]), w_ref[...]) + b_ref[...]
        cos, sin = c_ref[...], s_ref[...]
        q_ref[...] = _mx(_rope(p[:, :D_ATTN], jnp.tile(cos, (1, 4)), jnp.tile(sin, (1, 4))))
        k_ref[...] = _mx(_rope(p[:, 512:640], cos, sin))
        v_ref[...] = _mx(p[:, 640:768])
        u_ref[...] = p[:, 768:1024]
        xr_ref[...] = p[:, 1024:1280]
        g_ref[...] = p[:, 1280:1536]

    r = functools.partial(_rows_spec, n_tiles=nt)
    return _call(
        body, "inproj_fwd", nt,
        [r(TILE_WIDE, D), _whole(), _whole(), r(TILE_WIDE, 128), r(TILE_WIDE, 128)],
        [r(TILE_WIDE, D_ATTN), r(TILE_WIDE, D_KV), r(TILE_WIDE, D_KV), r(TILE_WIDE, D_S5), r(TILE_WIDE, D_LRU), r(TILE_WIDE, D_LRU)],
        [_sds((n, D_ATTN), MXU), _sds((n, D_KV), MXU), _sds((n, D_KV), MXU), _sds((n, D_S5)), _sds((n, D_LRU)), _sds((n, D_LRU))],
    )(x, w, b, cos_t, sin_t)


def _inproj_bwd(dq, dk, dv, du, dxr, dgate, cos_t, sin_t, x0, dr1, w_t):
    n = x0.shape[0]
    nt = n // TILE_WIDE

    def body(dq_ref, dk_ref, dv_ref, du_ref, dxr_ref, dg_ref, c_ref, s_ref, x_ref, dr_ref, w_ref, dx_ref, dw_ref, db_ref):
        @pl.when(pl.program_id(0) == 0)
        def _():
            dw_ref[...] = jnp.zeros_like(dw_ref)
            db_ref[...] = jnp.zeros_like(db_ref)

        cos, sin = c_ref[...], s_ref[...]
        dtq = _rope_t(dq_ref[...], jnp.tile(cos, (1, 4)), jnp.tile(sin, (1, 4)))
        dtk = _rope_t(dk_ref[...], cos, sin)
        dp = jnp.concatenate([dtq, dtk, dv_ref[...], du_ref[...], dxr_ref[...], dg_ref[...]], axis=1)
        db_ref[...] += _sum0(dp)
        dpb = _mx(dp)
        dw_ref[...] += _dot_tn(dpb, _mx(x_ref[...]))
        dx_ref[...] = ALPHA * dr_ref[...] + _dot(dpb, w_ref[...])

    r = functools.partial(_rows_spec, n_tiles=nt)
    return _call(
        body, "inproj_bwd", nt,
        [r(TILE_WIDE, D_ATTN), r(TILE_WIDE, D_KV), r(TILE_WIDE, D_KV), r(TILE_WIDE, D_S5), r(TILE_WIDE, D_LRU), r(TILE_WIDE, D_LRU),
         r(TILE_WIDE, 128), r(TILE_WIDE, 128), r(TILE_WIDE, D), r(TILE_WIDE, D), _whole()],
        [r(TILE_WIDE, D), _whole(), _whole()],
        [_sds((n, D)), _sds((D_IN, D)), _sds((1, D_IN))],
    )(dq, dk, dv, du, dxr, dgate, cos_t, sin_t, x0, dr1, w_t)


def _kv_variants(t, lo):
    tr = pltpu.roll(t, 64, 1)
    out = []
    for j in range(2):
        first = jnp.where(lo, t if j == 0 else tr, 0.0)
        second = jnp.where(lo, 0.0, tr if j == 0 else t)
        out.append(_mx(jnp.concatenate([first, second], axis=0)))
    return out


def _kv_collect(x0, x1, lo):
    a = x0[:256] + pltpu.roll(x0[256:], 64, 1)
    b = pltpu.roll(x1[:256], 64, 1) + x1[256:]
    return jnp.where(lo, a, b)


def _attn_probs(s, sink_ref):
    out = []
    for hp in range(2):
        sh = s[:, hp * 128:(hp + 1) * 128]
        sink = sink_ref[hp]
        m = jnp.maximum(jnp.max(sh, axis=1, keepdims=True), sink)
        p = jnp.exp(sh - m)
        es = jnp.exp(sink - m)
        inv = 1.0 / (jnp.sum(p, axis=1, keepdims=True) + es)
        out.append((p * inv, es * inv))
    return out


def _band_merge(x, tri, no_previous=None):
    bands = []
    for hp in range(2):
        prev, own = x[:, hp * 256:hp * 256 + 128], x[:, hp * 256 + 128:hp * 256 + 256]
        if no_previous is not None:
            prev = jnp.where(no_previous, -jnp.inf, prev)
        bands.append(jnp.where(tri, own, prev))
    return jnp.concatenate(bands, axis=1)


def _band_split(y, tri):
    parts = []
    for hp in range(2):
        band = y[:, hp * 128:(hp + 1) * 128]
        parts += [jnp.where(tri, 0.0, band), jnp.where(tri, band, 0.0)]
    return jnp.concatenate(parts, axis=1)


def _tri():
    shape = (ATTN_BLOCK, ATTN_BLOCK)
    return lax.broadcasted_iota(jnp.int32, shape, 0) >= lax.broadcasted_iota(jnp.int32, shape, 1)


def _attn_scores(q_ref, k_ref, v_ref, nb, tri):
    lo = lax.broadcasted_iota(jnp.int32, (256, 128), 1) < 64
    kcats, vcats, kstarts, parts = [], [], [], []
    for b in range(nb):
        block = pl.program_id(0) * nb + b
        kstart = pl.multiple_of(block * ATTN_BLOCK, ATTN_BLOCK)
        kcat = _kv_variants(k_ref[pl.ds(kstart, 256), :].astype(F32), lo)
        kcats.append(kcat)
        vcats.append(_kv_variants(v_ref[pl.ds(kstart, 256), :].astype(F32), lo))
        kstarts.append(kstart)
        for i in range(4):
            s = _dot_nt(q_ref[b * ATTN_BLOCK:(b + 1) * ATTN_BLOCK, i * 128:(i + 1) * 128], kcat[i // 2]) * 0.125
            parts.append(_band_merge(s, tri, block == 0))
    return jnp.concatenate(parts, axis=0), kcats, vcats, kstarts


def _attn_fwd(q, k, v, sink_cols):
    n = q.shape[0]
    nt = n // TILE_Q
    nb = TILE_Q // ATTN_BLOCK

    def body(q_ref, k_ref, v_ref, s_ref, o_ref):
        tri = _tri()
        s, _, vcats, _ = _attn_scores(q_ref, k_ref, v_ref, nb, tri)
        (p0, _), (p1, _) = _attn_probs(s, s_ref)
        p = jnp.concatenate([p0, p1], axis=1)
        for b in range(nb):
            for i in range(4):
                unit = (b * 4 + i) * ATTN_BLOCK
                o_ref[b * ATTN_BLOCK:(b + 1) * ATTN_BLOCK, i * 128:(i + 1) * 128] = _dot(
                    _mx(_band_split(p[unit:unit + ATTN_BLOCK], tri)), vcats[b][i // 2])

    return _call(
        body, "attn_fwd", nt,
        [_rows_spec(TILE_Q, D_ATTN, nt), _whole(), _whole(), _whole()],
        _rows_spec(TILE_Q, D_ATTN, nt), _sds((n, D_ATTN)),
    )(q, k, v, sink_cols)


def _attn_bwd(q, k, v, sink_cols, o, do):
    n = q.shape[0]
    nt = n // TILE_Q
    nb = TILE_Q // ATTN_BLOCK

    def body(q_ref, k_ref, v_ref, s_ref, o_ref, do_ref, dq_ref, dk_ref, dv_ref, ds_ref):
        @pl.when(pl.program_id(0) == 0)
        def _():
            dk_ref[...] = jnp.zeros_like(dk_ref)
            dv_ref[...] = jnp.zeros_like(dv_ref)
            ds_ref[...] = jnp.zeros_like(ds_ref)

        lo = lax.broadcasted_iota(jnp.int32, (256, 128), 1) < 64
        tri = _tri()
        s, kcats, vcats, kstarts = _attn_scores(q_ref, k_ref, v_ref, nb, tri)
        probs = _attn_probs(s, s_ref)
        do = do_ref[...]
        dob = _mx(do)
        od = do * o_ref[...]
        lo_q = (lax.broadcasted_iota(jnp.int32, od.shape, 1) & 64) == 0
        od_head = (jnp.where(lo_q, od, 0.0), jnp.where(lo_q, 0.0, od))
        units = [(b, i) for b in range(nb) for i in range(4)]

        def tile_part(a, b, i):
            return a[b * ATTN_BLOCK:(b + 1) * ATTN_BLOCK, i * 128:(i + 1) * 128]

        dp = jnp.concatenate([_band_merge(_dot_nt(tile_part(dob, b, i), vcats[b][i // 2]), tri) for b, i in units], axis=0)
        ds = []
        for hp in range(2):
            p, p_sink = probs[hp]
            delta = jnp.concatenate([jnp.sum(tile_part(od_head[hp], b, i), axis=1, keepdims=True) for b, i in units], axis=0)
            ds.append(p * (dp[:, hp * 128:(hp + 1) * 128] - delta) * 0.125)
            t = p_sink * delta
            for i in range(4):
                dsink = sum(_sum0(t[(b * 4 + i) * ATTN_BLOCK:(b * 4 + i + 1) * ATTN_BLOCK]) for b in range(nb))
                ds_ref[2 * i + hp:2 * i + hp + 1, :] -= jnp.broadcast_to(dsink, (1, 128))
        ds = jnp.concatenate(ds, axis=1)
        p = jnp.concatenate([probs[0][0], probs[1][0]], axis=1)
        for b in range(nb):
            dkc = [jnp.zeros((512, 128), F32), jnp.zeros((512, 128), F32)]
            dvc = [jnp.zeros((512, 128), F32), jnp.zeros((512, 128), F32)]
            for i in range(4):
                j = i // 2
                unit = slice((b * 4 + i) * ATTN_BLOCK, (b * 4 + i + 1) * ATTN_BLOCK)
                dsb = _mx(_band_split(ds[unit], tri))
                dq_ref[b * ATTN_BLOCK:(b + 1) * ATTN_BLOCK, i * 128:(i + 1) * 128] = _dot(dsb, kcats[b][j])
                dkc[j] = dkc[j] + _dot_tn(dsb, tile_part(q_ref, b, i))
                dvc[j] = dvc[j] + _dot_tn(_mx(_band_split(p[unit], tri)), tile_part(dob, b, i))
            dk_ref[pl.ds(kstarts[b], 256), :] += _kv_collect(dkc[0], dkc[1], lo)
            dv_ref[pl.ds(kstarts[b], 256), :] += _kv_collect(dvc[0], dvc[1], lo)

    r = _rows_spec(TILE_Q, D_ATTN, nt)
    return _call(
        body, "attn_bwd", nt,
        [r, _whole(), _whole(), _whole(), r, r],
        [r, _whole(), _whole(), _whole()],
        [_sds((n, D_ATTN)), _sds((n + ATTN_BLOCK, D_KV)), _sds((n + ATTN_BLOCK, D_KV)), _sds((8, 128))],
    )(q, k, v, sink_cols, o, do)


S5_TABLES = ('pw_re', 'pw_im', 'dbl_re', 'dbl_im', 'seg_re', 'seg_im', 'segr_re', 'segr_im')
S5_WEIGHTS = ('b_re', 'b_im', 'c_re', 'c_im', 'd', 'glu_w', 'glu_b', 'perm', 'perm_t')


def _s5_states(u, carry_r, carry_i, b_re, b_im, tab, hr_s, hi_s):
    ub = _mx(u)
    hr_s[...] = _dot(ub, b_re[...])
    hi_s[...] = _dot(ub, b_im[...])
    return ub, _cscan(hr_s, hi_s, tab, carry_r, carry_i, reverse=False)


def _s5_fwd(u, prm):
    n = u.shape[0]
    nt = n // (S5_PAIR * TILE)

    def body(u_ref, *refs):
        tab = dict(zip(S5_TABLES, refs[:8]))
        b_re, b_im, c_re, c_im, d_ref, gw_ref, gb_ref, perm, perm_t = refs[8:17]
        y_ref, cr_out, ci_out, cr_s, ci_s = refs[17:22]
        states = [refs[22 + 2 * j:24 + 2 * j] for j in range(S5_PAIR)]
        rows = [slice(j * TILE, (j + 1) * TILE) for j in range(S5_PAIR)]

        @pl.when(pl.program_id(0) == 0)
        def _():
            cr_s[...] = jnp.zeros_like(cr_s)
            ci_s[...] = jnp.zeros_like(ci_s)

        us = []
        for j in range(S5_PAIR):
            us.append(_permute_rows(perm[...], u_ref[rows[j], :]))
            ub = _mx(us[j])
            states[j][0][...] = _dot(ub, b_re[...])
            states[j][1][...] = _dot(ub, b_im[...])
        for j in range(S5_PAIR):
            cr, ci = cr_s[...], ci_s[...]
            cr_out[8 * j:8 * j + 8, :] = jnp.broadcast_to(cr, (8, N_STATE))
            ci_out[8 * j:8 * j + 8, :] = jnp.broadcast_to(ci, (8, N_STATE))
            cr_s[...], ci_s[...] = _cscan(*states[j], tab, cr, ci, reverse=False)
        for j in range(S5_PAIR):
            hr_s, hi_s = states[j]
            y = _dot(_mx(hr_s[...]), c_re[...]) - _dot(_mx(hi_s[...]), c_im[...]) + d_ref[...] * us[j]
            z = _gelu(y)
            y_ref[rows[j], :] = _permute_rows(perm_t[...], z * _sigmoid(_dot(_mx(z), gw_ref[...]) + gb_ref[...]))

    r = functools.partial(_rows_spec, n_tiles=nt)
    return _call(
        body, "s5_fwd", nt,
        [r(S5_PAIR * TILE, D_S5)] + [_whole()] * 17,
        [r(S5_PAIR * TILE, D_S5), r(S5_PAIR * 8, N_STATE), r(S5_PAIR * 8, N_STATE)],
        [_sds((n, D_S5)), _sds((n // TILE * 8, N_STATE)), _sds((n // TILE * 8, N_STATE))],
        scratch=[pltpu.VMEM((1, N_STATE), F32)] * 2 + [pltpu.VMEM((TILE, N_STATE), F32)] * (2 * S5_PAIR),
    )(u, *[prm[k] for k in S5_TABLES + S5_WEIGHTS])


def _s5_bwd(u, dys, carry_re, carry_im, prm):
    n = u.shape[0]
    nt = n // (S5_PAIR * TILE)

    def body(u_ref, dy_ref, cin_r, cin_i, *refs):
        tab = dict(zip(S5_TABLES, refs[:8]))
        b_re, b_im, c_re, c_im, d_ref, gw_ref, gb_ref, perm, perm_t = refs[8:17]
        du_ref, dbr_ref, dbi_ref, dcr_ref, dci_ref, dar_ref, dai_ref, dd_ref, dgw_ref, dgb_ref = refs[17:27]
        gr_s, gi_s = refs[27:29]
        scratch = [refs[29 + 4 * j:33 + 4 * j] for j in range(S5_PAIR)]
        later_first = list(reversed(range(S5_PAIR)))
        rows = [slice(j * TILE, (j + 1) * TILE) for j in range(S5_PAIR)]

        @pl.when(pl.program_id(0) == 0)
        def _():
            for ref in (dbr_ref, dbi_ref, dcr_ref, dci_ref, dar_ref, dai_ref, dd_ref, dgw_ref, dgb_ref, gr_s, gi_s):
                ref[...] = jnp.zeros_like(ref)

        us, ubs, carries, dy_of = {}, {}, {}, {}
        for j in later_first:
            us[j] = _permute_rows(perm[...], u_ref[rows[j], :])
            carries[j] = (cin_r[8 * j:8 * j + 1, :], cin_i[8 * j:8 * j + 1, :])
            ubs[j], _ = _s5_states(us[j], *carries[j], b_re, b_im, tab, *scratch[j][:2])
        for j in later_first:
            u = us[j]
            hr_s, hi_s, gr_t, gi_t = scratch[j]
            hrb, hib = _mx(hr_s[...]), _mx(hi_s[...])
            y = _dot(hrb, c_re[...]) - _dot(hib, c_im[...]) + d_ref[...] * u
            z = _gelu(y)
            zb = _mx(z)
            sg = _sigmoid(_dot(zb, gw_ref[...]) + gb_ref[...])
            dout = _permute_rows(perm[...], dy_ref[rows[j], :])
            dpre = dout * z * sg * (1.0 - sg)
            dgb_ref[...] += _sum0(dpre)
            dpb = _mx(dpre)
            dgw_ref[...] += _dot_tn(zb, dpb)
            dy = (dout * sg + _dot_nt(dpb, gw_ref[...])) * _gelu_grad(y)
            dd_ref[...] += _sum0(dy * u)
            dyb = _mx(dy)
            dcr_ref[...] += _dot_tn(hrb, dyb)
            dci_ref[...] -= _dot_tn(hib, dyb)
            gr_t[...] = _dot_nt(dyb, c_re[...])
            gi_t[...] = -_dot_nt(dyb, c_im[...])
            dy_of[j] = dy
        for j in later_first:
            gr_s[...], gi_s[...] = _cscan(*scratch[j][2:], tab, gr_s[...], gi_s[...], reverse=True)
        for j in later_first:
            hr_s, hi_s, gr_t, gi_t = scratch[j]
            cr, ci = carries[j]
            sub = _row_iota((N_SEG, N_STATE))
            acc_r = acc_i = jnp.zeros((N_SEG, N_STATE), F32)
            for k in range(SEG):
                if k == 0:
                    hpr = jnp.where(sub >= 1, pltpu.roll(hr_s[_seg_rows(SEG - 1), :], 1, 0), cr)
                    hpi = jnp.where(sub >= 1, pltpu.roll(hi_s[_seg_rows(SEG - 1), :], 1, 0), ci)
                else:
                    hpr, hpi = hr_s[_seg_rows(k - 1), :], hi_s[_seg_rows(k - 1), :]
                gr, gi = gr_t[_seg_rows(k), :], gi_t[_seg_rows(k), :]
                acc_r = acc_r + gr * hpr + gi * hpi
                acc_i = acc_i + gi * hpr - gr * hpi
            dar_ref[...] += _sum0(acc_r)
            dai_ref[...] += _sum0(acc_i)
            grb, gib = _mx(gr_t[...]), _mx(gi_t[...])
            dbr_ref[...] += _dot_tn(ubs[j], grb)
            dbi_ref[...] += _dot_tn(ubs[j], gib)
            du_ref[rows[j], :] = _permute_rows(perm_t[...], dy_of[j] * d_ref[...] + _dot_nt(grb, b_re[...]) + _dot_nt(gib, b_im[...]))

    r = functools.partial(_rows_spec, n_tiles=nt, reverse=True)
    return _call(
        body, "s5_bwd", nt,
        [r(S5_PAIR * TILE, D_S5), r(S5_PAIR * TILE, D_S5), r(S5_PAIR * 8, N_STATE), r(S5_PAIR * 8, N_STATE)] + [_whole()] * 17,
        [r(S5_PAIR * TILE, D_S5)] + [_whole()] * 9,
        [_sds((n, D_S5)), _sds((D_S5, N_STATE)), _sds((D_S5, N_STATE)), _sds((N_STATE, D_S5)), _sds((N_STATE, D_S5)),
         _sds((1, N_STATE)), _sds((1, N_STATE)), _sds((1, D_S5)), _sds((D_S5, D_S5)), _sds((1, D_S5))],
        scratch=[pltpu.VMEM((1, N_STATE), F32)] * 2 + [pltpu.VMEM((TILE, N_STATE), F32)] * (4 * S5_PAIR),
    )(u, dys, carry_re, carry_im, *[prm[k] for k in S5_TABLES + S5_WEIGHTS])


def _lru_gates(xr, halo, tile_index, cw_ref, cb_ref, wx_ref, wa_ref, bx_ref, ba_ref, sp_ref):
    ext = jnp.concatenate([halo, xr], axis=0)
    sh = [xr] + [_shift_down(ext, j, TILE) for j in (1, 2, 3)]
    xc = cb_ref[...] + cw_ref[3:4, :] * sh[0] + cw_ref[2:3, :] * sh[1] + cw_ref[1:2, :] * sh[2] + cw_ref[0:1, :] * sh[3]
    xb = _mx(xc)
    gx = _sigmoid(_dot(xb, wx_ref[...]) + bx_ref[...])
    ga = _sigmoid(_dot(xb, wa_ref[...]) + ba_ref[...])
    la = -LRU_C * ga * sp_ref[...]
    a = jnp.exp(la)
    start = (tile_index * TILE + _row_iota(xr.shape)) == 0
    mult = jnp.where(start, 1.0, jnp.sqrt(-jnp.tanh(la) * (a * a + 1.0)))
    return sh, xc, xb, gx, ga, a, mult, start


def _lru_fwd(xr, gate, prm):
    n = xr.shape[0]
    nt = n // TILE

    def body(x_ref, g_ref, cw_ref, cb_ref, wx_ref, wa_ref, bx_ref, ba_ref, sp_ref, y_ref, c_out, halo_s, c_s):
        first_tile = pl.program_id(0) == 0

        @pl.when(first_tile)
        def _():
            halo_s[...] = jnp.zeros_like(halo_s)
            c_s[...] = jnp.zeros_like(c_s)

        xr = x_ref[...]
        _, xc, _, gx, _, a, mult, _ = _lru_gates(xr, halo_s[...], pl.program_id(0), cw_ref, cb_ref, wx_ref, wa_ref, bx_ref, ba_ref,
                                                 sp_ref)
        halo_s[...] = xr[TILE - 8:]
        acum, h = _rscan(a, mult * gx * xc, reverse=False)
        c = c_s[...]
        c_out[...] = jnp.broadcast_to(c, (8, D_LRU))
        h = h + acum * c
        c_s[...] = h[TILE - 1:TILE]
        y_ref[...] = h * _gelu(g_ref[...])

    r = functools.partial(_rows_spec, n_tiles=nt)
    return _call(
        body, "lru_fwd", nt,
        [r(TILE, D_LRU), r(TILE, D_LRU)] + [_whole()] * 7,
        [r(TILE, D_LRU), r(8, D_LRU)],
        [_sds((n, D_LRU)), _sds((nt * 8, D_LRU))],
        scratch=[pltpu.VMEM((8, D_LRU), F32), pltpu.VMEM((1, D_LRU), F32)],
    )(xr, gate, prm['conv_w'], prm['conv_b'], prm['wx'], prm['wa'], prm['bx'], prm['ba'], prm['sp'])


def _lru_bwd(xr, gate, dyl, carry, prm):
    n = xr.shape[0]
    nt = n // TILE

    def body(x_ref, xh_ref, g_ref, dy_ref, cin_ref, cw_ref, cb_ref, wx_ref, wa_ref, bx_ref, ba_ref, sp_ref,
             dx_ref, dg_ref, dcw0, dcw1, dcw2, dcw3, dcb_ref, dwx_ref, dwa_ref, dbx_ref, dba_ref, dsp_ref, an_s, gn_s, dn_s):
        first_tile = pl.program_id(0) == nt - 1

        @pl.when(pl.program_id(0) == 0)
        def _():
            for ref in (dcw0, dcw1, dcw2, dcw3, dcb_ref, dwx_ref, dwa_ref, dbx_ref, dba_ref, dsp_ref, gn_s, dn_s):
                ref[...] = jnp.zeros_like(ref)
            an_s[...] = jnp.ones_like(an_s)

        xr = x_ref[...]
        halo = jnp.where(first_tile, 0.0, xh_ref[...])
        sh, xc, xb, gx, ga, a, mult, start = _lru_gates(xr, halo, nt - 1 - pl.program_id(0), cw_ref, cb_ref, wx_ref, wa_ref, bx_ref,
                                                        ba_ref, sp_ref)
        acum, h = _rscan(a, mult * gx * xc, reverse=False)
        cin = cin_ref[0:1, :]
        h = h + acum * cin
        gate = g_ref[...]
        dyl = dy_ref[...]
        dg_ref[...] = dyl * h * _gelu_grad(gate)
        row = _row_iota(xr.shape)
        alpha = jnp.where(row < TILE - 1, pltpu.roll(a, TILE - 1, 0), an_s[...])
        racc, g = _rscan(alpha, dyl * _gelu(gate), reverse=True)
        g = g + racc * gn_s[...]
        an_s[...] = a[0:1]
        gn_s[...] = g[0:1]
        hprev = jnp.where(row == 0, cin, pltpu.roll(h, 1, 0))
        da = g * hprev
        dmult = jnp.where(start, 0.0, g * gx * xc)
        dla = da * a - dmult * a * a / mult
        dsp_ref[...] += _sum0(-LRU_C * ga * dla)
        dpa = (-LRU_C * sp_ref[...] * dla) * ga * (1.0 - ga)
        dpx = (g * mult * xc) * gx * (1.0 - gx)
        dba_ref[...] += _sum0(dpa)
        dbx_ref[...] += _sum0(dpx)
        dpab, dpxb = _mx(dpa), _mx(dpx)
        dwa_ref[...] += _dot_tn(xb, dpab)
        dwx_ref[...] += _dot_tn(xb, dpxb)
        dxc = g * mult * gx + _dot_nt(dpab, wa_ref[...]) + _dot_nt(dpxb, wx_ref[...])
        dcb_ref[...] += _sum0(dxc)
        dcw3[...] += _sum0(dxc * sh[0])
        dcw2[...] += _sum0(dxc * sh[1])
        dcw1[...] += _sum0(dxc * sh[2])
        dcw0[...] += _sum0(dxc * sh[3])
        ext = jnp.concatenate([dxc, dn_s[...]], axis=0)
        dx_ref[...] = (cw_ref[3:4, :] * dxc + cw_ref[2:3, :] * _shift_up(ext, 1, TILE) + cw_ref[1:2, :] * _shift_up(ext, 2, TILE)
                       + cw_ref[0:1, :] * _shift_up(ext, 3, TILE))
        dn_s[...] = dxc[:8]

    r = functools.partial(_rows_spec, n_tiles=nt, reverse=True)
    vec = _sds((1, D_LRU))
    return _call(
        body, "lru_bwd", nt,
        [r(TILE, D_LRU), _halo_spec(D_LRU, TILE, nt, reverse=True), r(TILE, D_LRU), r(TILE, D_LRU), r(8, D_LRU)] + [_whole()] * 7,
        [r(TILE, D_LRU), r(TILE, D_LRU)] + [_whole()] * 10,
        [_sds((n, D_LRU)), _sds((n, D_LRU)), vec, vec, vec, vec, vec, _sds((D_LRU, D_LRU)), _sds((D_LRU, D_LRU)), vec, vec, vec],
        scratch=[pltpu.VMEM((1, D_LRU), F32), pltpu.VMEM((1, D_LRU), F32), pltpu.VMEM((8, D_LRU), F32)],
    )(xr, xr, gate, dyl, carry, prm['conv_w'], prm['conv_b'], prm['wx'], prm['wa'], prm['bx'], prm['ba'], prm['sp'])


def _normed_parts(ya, ys, yl):
    return jnp.concatenate([ya * _rms(ya), ys * _rms(ys), yl * _rms(yl)], axis=1)


def _mixout_fwd(ya, ys, yl, x0, g_mix, w_out, b_out, g1, b1):
    n = x0.shape[0]
    nt = n // TILE_WIDE

    def body(ya_ref, ys_ref, yl_ref, x_ref, gm_ref, w_ref, b_ref, g_ref, be_ref, mix_ref, r_ref, x1_ref):
        mixb = _mx(_normed_parts(ya_ref[...], ys_ref[...], yl_ref[...]) * gm_ref[...])
        mix_ref[...] = mixb
        r1 = ALPHA * x_ref[...] + _dot(mixb, w_ref[...]) + b_ref[...]
        r_ref[...] = r1
        xhat, _ = _ln_stats(r1)
        x1_ref[...] = xhat * g_ref[...] + be_ref[...]

    r = functools.partial(_rows_spec, n_tiles=nt)
    return _call(
        body, "mixout_fwd", nt,
        [r(TILE_WIDE, D_ATTN), r(TILE_WIDE, D_S5), r(TILE_WIDE, D_LRU), r(TILE_WIDE, D)] + [_whole()] * 5,
        [r(TILE_WIDE, D), r(TILE_WIDE, D), r(TILE_WIDE, D)],
        [_sds((n, D), MXU), _sds((n, D)), _sds((n, D))],
    )(ya, ys, yl, x0, g_mix, w_out, b_out, g1, b1)


def _mixout_bwd(dr1, mix, ya, ys, yl, g_mix, w_out):
    n = dr1.shape[0]
    nt = n // TILE_WIDE

    def body(dr_ref, mix_ref, ya_ref, ys_ref, yl_ref, gm_ref, w_ref, dya_ref, dys_ref, dyl_ref, dw_ref, db_ref, dgm_ref):
        @pl.when(pl.program_id(0) == 0)
        def _():
            for ref in (dw_ref, db_ref, dgm_ref):
                ref[...] = jnp.zeros_like(ref)

        dr = dr_ref[...]
        db_ref[...] += _sum0(dr)
        drb = _mx(dr)
        dw_ref[...] += _dot_tn(mix_ref[...], drb)
        dmix = _dot(drb, w_ref[...])
        parts = (ya_ref[...], ys_ref[...], yl_ref[...])
        dgm_ref[...] += _sum0(dmix * _normed_parts(*parts))
        dn = dmix * gm_ref[...]
        lo = 0
        for y, out in zip(parts, (dya_ref, dys_ref, dyl_ref)):
            w = y.shape[1]
            rs = _rms(y)
            nrm = y * rs
            dnp = dn[:, lo:lo + w]
            out[...] = rs * (dnp - nrm * jnp.mean(dnp * nrm, axis=-1, keepdims=True))
            lo += w

    r = functools.partial(_rows_spec, n_tiles=nt)
    return _call(
        body, "mixout_bwd", nt,
        [r(TILE_WIDE, D), r(TILE_WIDE, D), r(TILE_WIDE, D_ATTN), r(TILE_WIDE, D_S5), r(TILE_WIDE, D_LRU), _whole(), _whole()],
        [r(TILE_WIDE, D_ATTN), r(TILE_WIDE, D_S5), r(TILE_WIDE, D_LRU), _whole(), _whole(), _whole()],
        [_sds((n, D_ATTN)), _sds((n, D_S5)), _sds((n, D_LRU)), _sds((D, D)), _sds((1, D)), _sds((1, D))],
    )(dr1, mix, ya, ys, yl, g_mix, w_out)


def _ffn_conv(gp, halo, cw_ref, cb_ref, cs):
    ext = jnp.concatenate([halo, gp], axis=0)
    s1 = _shift_down(ext, 1, TILE)
    s2 = _shift_down(ext, 2, TILE)
    return s1, s2, cb_ref[:, cs] + cw_ref[2:3, cs] * gp + cw_ref[1:2, cs] * s1 + cw_ref[0:1, cs] * s2


def _ffn_fwd(x1, wg, wu, cw, cb, wd, g2, b2):
    n = x1.shape[0]
    nt = n // TILE

    def body(x_ref, wg_ref, wu_ref, cw_ref, cb_ref, wd_ref, g_ref, be_ref, gp_ref, up_ref, r_ref, x2_ref, halo_s, act_s):
        @pl.when(pl.program_id(0) == 0)
        def _():
            halo_s[...] = jnp.zeros_like(halo_s)

        x1 = x_ref[...]
        xb = _mx(x1)
        for c in range(D_FF // FF_CHUNK):
            cs = slice(c * FF_CHUNK, (c + 1) * FF_CHUNK)
            gp = _dot(xb, wg_ref[:, cs])
            up = _dot(xb, wu_ref[:, cs])
            gp_ref[:, cs] = gp
            up_ref[:, cs] = up
            _, _, gc = _ffn_conv(gp, halo_s[:, cs], cw_ref, cb_ref, cs)
            halo_s[:, cs] = gp[TILE - 8:]
            act_s[:, cs] = _mx(gc * _sigmoid(gc) * up)
        r2 = ALPHA * x1 + _dot(act_s[...], wd_ref[...])
        r_ref[...] = r2
        xhat, _ = _ln_stats(r2)
        x2_ref[...] = xhat * g_ref[...] + be_ref[...]

    r = functools.partial(_rows_spec, n_tiles=nt)
    return _call(
        body, "ffn_fwd", nt,
        [r(TILE, D)] + [_whole()] * 7,
        [r(TILE, D_FF), r(TILE, D_FF), r(TILE, D), r(TILE, D)],
        [_sds((n, D_FF)), _sds((n, D_FF)), _sds((n, D)), _sds((n, D))],
        scratch=[pltpu.VMEM((8, D_FF), F32), pltpu.VMEM((TILE, D_FF), MXU)],
    )(x1, wg, wu, cw, cb, wd, g2, b2)


def _ffn_bwd_down(dx2, r2, g2, gp, up, cw, cb, wd_t):
    n = dx2.shape[0]
    nt = n // TILE

    def body(dx_ref, r_ref, g_ref, gp_ref, gh_ref, up_ref, cw_ref, cb_ref, wd_ref,
             dr_ref, dgp_ref, dup_ref, dwd_ref, dcw0, dcw1, dcw2, dcb_ref, dg_ref, db_ref, next_s):
        first_tile = pl.program_id(0) == nt - 1

        @pl.when(pl.program_id(0) == 0)
        def _():
            for ref in (dwd_ref, dcw0, dcw1, dcw2, dcb_ref, dg_ref, db_ref, next_s):
                ref[...] = jnp.zeros_like(ref)

        dx2 = dx_ref[...]
        xhat, rstd = _ln_stats(r_ref[...])
        dg_ref[...] += _sum0(dx2 * xhat)
        db_ref[...] += _sum0(dx2)
        dr2 = _ln_bwd(dx2, g_ref[...], xhat, rstd)
        dr_ref[...] = dr2
        dfb = _mx(dr2)
        for c in range(D_FF // FF_CHUNK):
            cs = slice(c * FF_CHUNK, (c + 1) * FF_CHUNK)
            gp = gp_ref[:, cs]
            up = up_ref[:, cs]
            s1, s2, gc = _ffn_conv(gp, jnp.where(first_tile, 0.0, gh_ref[:, cs]), cw_ref, cb_ref, cs)
            sg = _sigmoid(gc)
            silu = gc * sg
            dact = _dot(dfb, wd_ref[:, cs])
            dwd_ref[cs, :] += _dot_tn(_mx(silu * up), dfb)
            dup_ref[:, cs] = _mx(dact * silu)
            dgc = dact * up * (sg + silu * (1.0 - sg))
            dcb_ref[:, cs] += _sum0(dgc)
            dcw2[:, cs] += _sum0(dgc * gp)
            dcw1[:, cs] += _sum0(dgc * s1)
            dcw0[:, cs] += _sum0(dgc * s2)
            ext = jnp.concatenate([dgc, next_s[:, cs]], axis=0)
            dgp_ref[:, cs] = _mx(cw_ref[2:3, cs] * dgc + cw_ref[1:2, cs] * _shift_up(ext, 1, TILE)
                                 + cw_ref[0:1, cs] * _shift_up(ext, 2, TILE))
            next_s[:, cs] = dgc[:8]

    r = functools.partial(_rows_spec, n_tiles=nt, reverse=True)
    vff = _sds((1, D_FF))
    return _call(
        body, "ffn_bwd_down", nt,
        [r(TILE, D), r(TILE, D), _whole(), r(TILE, D_FF), _halo_spec(D_FF, TILE, nt, reverse=True), r(TILE, D_FF), _whole(), _whole(),
         _whole()],
        [r(TILE, D), r(TILE, D_FF), r(TILE, D_FF)] + [_whole()] * 7,
        [_sds((n, D)), _sds((n, D_FF), MXU), _sds((n, D_FF), MXU), _sds((D_FF, D)), vff, vff, vff, vff, _sds((1, D)), _sds((1, D))],
        scratch=[pltpu.VMEM((8, D_FF), F32)],
    )(dx2, r2, g2, gp, gp, up, cw, cb, wd_t)


def _ffn_bwd_dx(dr2, dgp, dup, r1, g1, wg_t, wu_t):
    n = dr2.shape[0]
    rows = TILE_BIG
    nt = n // rows

    def body(dr2_ref, dgp_ref, dup_ref, r_ref, g_ref, wg_ref, wu_ref, dr1_ref, dg_ref, db_ref):
        @pl.when(pl.program_id(0) == 0)
        def _():
            for ref in (dg_ref, db_ref):
                ref[...] = jnp.zeros_like(ref)

        dx1 = ALPHA * dr2_ref[...] + _dot(dgp_ref[...], wg_ref[...]) + _dot(dup_ref[...], wu_ref[...])
        xhat, rstd = _ln_stats(r_ref[...])
        dg_ref[...] += _sum0(dx1 * xhat)
        db_ref[...] += _sum0(dx1)
        dr1_ref[...] = _ln_bwd(dx1, g_ref[...], xhat, rstd)

    r = functools.partial(_rows_spec, n_tiles=nt)
    return _call(
        body, "ffn_bwd_dx", nt,
        [r(rows, D), r(rows, D_FF), r(rows, D_FF), r(rows, D), _whole(), _whole(), _whole()],
        [r(rows, D), _whole(), _whole()],
        [_sds((n, D)), _sds((1, D)), _sds((1, D))],
    )(dr2, dgp, dup, r1, g1, wg_t, wu_t)


def _ffn_bwd_dw(x1, dgp, dup):
    n = x1.shape[0]
    rows = TILE_BIG
    nt = n // rows

    def body(x_ref, dgp_ref, dup_ref, dwg_ref, dwu_ref):
        @pl.when(pl.program_id(0) == 0)
        def _():
            for ref in (dwg_ref, dwu_ref):
                ref[...] = jnp.zeros_like(ref)

        xb = _mx(x_ref[...])
        for c in range(D_FF // FF_CHUNK):
            cs = slice(c * FF_CHUNK, (c + 1) * FF_CHUNK)
            dwg_ref[cs, :] += _dot_tn(dgp_ref[:, cs], xb)
            dwu_ref[cs, :] += _dot_tn(dup_ref[:, cs], xb)

    r = functools.partial(_rows_spec, n_tiles=nt)
    return _call(
        body, "ffn_bwd_dw", nt,
        [r(rows, D), r(rows, D_FF), r(rows, D_FF)], [_whole(), _whole()], [_sds((D_FF, D)), _sds((D_FF, D))],
    )(x1, dgp, dup)


def _loss_head(y, target):
    n = y.shape[0]
    nt = n // TILE_WIDE

    def body(y_ref, t_ref, loss_ref, dy_ref):
        @pl.when(pl.program_id(0) == 0)
        def _():
            loss_ref[...] = jnp.zeros_like(loss_ref)

        e = y_ref[...] - t_ref[...]
        dy_ref[...] = e * (1.0 / D)
        loss_ref[...] += _sum0(jnp.sum(e * e, axis=1, keepdims=True)) * (0.5 / D)

    r = functools.partial(_rows_spec, n_tiles=nt)
    return _call(body, "loss_head", nt, [r(TILE_WIDE, D), r(TILE_WIDE, D)], [_whole(), r(TILE_WIDE, D)],
                 [_sds((1, 1)), _sds((n, D))])(y, target)


def _place():
    x, y, c = lax.axis_index("x"), lax.axis_index("y"), lax.axis_index("c")
    return x, y, c, 4 * x + 2 * y + c


def _peer(x, y, c, k):
    px, py, pc = x ^ ((k >> 2) & 1), y ^ ((k >> 1) & 1), c ^ (k & 1)
    return (px, py, pc), 4 * px + 2 * py + pc


def _all_gather(blocks, small):
    srcs = list(blocks) + [small]
    n = len(srcs)
    out_shapes = [_sds((a.shape[0], N_DEV * a.shape[1], LANES), a.dtype) for a in blocks] + [_sds((N_DEV,) + small.shape, small.dtype)]

    def body(*refs):
        src_refs, out_refs = refs[:n], refs[n:2 * n]
        send_sems, recv_sems, local_sems = refs[2 * n:]
        x, y, c, me = _place()

        def landing(a, slot):
            if a == n - 1:
                return out_refs[a].at[slot]
            r = src_refs[a].shape[1]
            return out_refs[a].at[:, pl.ds(slot * r, r), :]

        def remote(a, k, slot):
            peer, _ = _peer(x, y, c, k)
            return pltpu.make_async_remote_copy(
                src_ref=src_refs[a], dst_ref=landing(a, slot), send_sem=send_sems.at[a * N_DEV + k],
                recv_sem=recv_sems.at[a * N_DEV + k], device_id=peer, device_id_type=pl.DeviceIdType.MESH)

        mine = [pltpu.make_async_copy(src_refs[a], landing(a, me), local_sems.at[a]) for a in range(n)]
        sends = [remote(a, k, me) for a in range(n) for k in range(1, N_DEV)]
        for cp in mine + sends:
            cp.start()
        for a in range(n):
            for k in range(1, N_DEV):
                remote(a, k, _peer(x, y, c, k)[1]).wait_recv()
        for cp in sends:
            cp.wait_send()
        for cp in mine:
            cp.wait()

    any_space = pl.BlockSpec(memory_space=pl.ANY)
    return pl.pallas_call(
        body, name="gather_weights", out_shape=out_shapes, in_specs=[any_space] * n, out_specs=[any_space] * n,
        scratch_shapes=[pltpu.SemaphoreType.DMA((n * N_DEV,)), pltpu.SemaphoreType.DMA((n * N_DEV,)), pltpu.SemaphoreType.DMA((n,))],
    )(*srcs)


_HBM = pl.BlockSpec(memory_space=pltpu.HBM)
_SEM = pl.BlockSpec(memory_space=pltpu.SEMAPHORE)
_EFFECT = pltpu.SideEffectType.DATAFLOW_SIDE_EFFECTING


def _in_hbm(a):
    return pltpu.with_memory_space_constraint(a, pltpu.HBM)


def _scatter_start(name, srcs):
    ns = len(srcs)
    rows_a = [a.shape[0] // N_DEV for a in srcs]
    offs = [sum(rows_a[:a]) for a in range(ns)]
    total = sum(rows_a)

    def body(*refs):
        src_refs, land_ref, send_sems, recv_sems, token = refs[:ns], refs[ns], refs[ns + 1], refs[ns + 2], refs[-1]
        x, y, c, me = _place()
        for k in range(1, N_DEV):
            peer, peer_slot = _peer(x, y, c, k)
            for a in range(ns):
                pltpu.make_async_remote_copy(
                    src_ref=src_refs[a].at[pl.ds(peer_slot * rows_a[a], rows_a[a]), :],
                    dst_ref=land_ref.at[me, pl.ds(offs[a], rows_a[a]), :], send_sem=send_sems.at[k], recv_sem=recv_sems.at[k],
                    device_id=peer, device_id_type=pl.DeviceIdType.MESH).start()
        token[...] = jnp.zeros_like(token)

    landing = lax.empty((N_DEV, total, LANES), F32)
    out = pl.pallas_call(
        body, name=name,
        out_shape=(pltpu.SemaphoreType.DMA((N_DEV,)), pltpu.SemaphoreType.DMA((N_DEV,)), *[pltpu.HBM(a.shape, a.dtype) for a in srcs],
                   pltpu.HBM(landing.shape, F32), _sds((8, 128))),
        in_specs=[_HBM] * (ns + 1), out_specs=(_SEM, _SEM, *[_HBM] * (ns + 1), pl.BlockSpec(memory_space=pltpu.VMEM)),
        input_output_aliases={a: 2 + a for a in range(ns + 1)},
        compiler_params=pltpu.CompilerParams(has_side_effects=_EFFECT),
    )(*[_in_hbm(a) for a in srcs], _in_hbm(landing))
    return out[0], out[1], out[2:2 + ns], out[2 + ns], out[-1]


def _scatter_wait(name, send_sems, recv_sems, srcs, landing, after):
    ns = len(srcs)

    def body(*refs):
        land_ref, send_ref, recv_ref = refs[ns], refs[ns + 1], refs[ns + 2]
        x, y, c, me = _place()
        for k in range(1, N_DEV):
            peer, peer_slot = _peer(x, y, c, k)
            slot = pltpu.make_async_remote_copy(
                src_ref=land_ref.at[me], dst_ref=land_ref.at[peer_slot], send_sem=send_ref.at[k], recv_sem=recv_ref.at[k],
                device_id=peer, device_id_type=pl.DeviceIdType.MESH)
            slot.wait_send()
            slot.wait_recv()

    out = pl.pallas_call(
        body, name=name, out_shape=(*[pltpu.HBM(a.shape, a.dtype) for a in srcs], pltpu.HBM(landing.shape, landing.dtype)),
        in_specs=[_HBM] * (ns + 1) + [_SEM, _SEM, pl.BlockSpec(memory_space=pl.ANY)], out_specs=[_HBM] * (ns + 1),
        input_output_aliases={a: a for a in range(ns + 1)},
        compiler_params=pltpu.CompilerParams(has_side_effects=_EFFECT),
    )(*srcs, landing, send_sems, recv_sems, after)
    return out[:ns], out[ns]


def _gather_start(name, blocks):
    n = len(blocks)

    def body(*refs):
        src_refs, land_refs, send_sems, recv_sems, token = refs[:n], refs[n:2 * n], refs[2 * n], refs[2 * n + 1], refs[-1]
        x, y, c, me = _place()
        for a in range(n):
            r = src_refs[a].shape[1]
            for k in range(1, N_DEV):
                pltpu.make_async_remote_copy(
                    src_ref=src_refs[a], dst_ref=land_refs[a].at[:, pl.ds(me * r, r), :], send_sem=send_sems.at[a * N_DEV + k],
                    recv_sem=recv_sems.at[a * N_DEV + k], device_id=_peer(x, y, c, k)[0], device_id_type=pl.DeviceIdType.MESH).start()
        token[...] = jnp.zeros_like(token)

    wholes = [lax.empty((a.shape[0], N_DEV * a.shape[1], LANES), a.dtype) for a in blocks]
    out = pl.pallas_call(
        body, name=name,
        out_shape=(pltpu.SemaphoreType.DMA((n * N_DEV,)), pltpu.SemaphoreType.DMA((n * N_DEV,)),
                   *[pltpu.HBM(a.shape, a.dtype) for a in blocks + wholes], _sds((8, 128))),
        in_specs=[_HBM] * (2 * n), out_specs=(_SEM, _SEM, *[_HBM] * (2 * n), pl.BlockSpec(memory_space=pltpu.VMEM)),
        input_output_aliases={a: 2 + a for a in range(2 * n)},
        compiler_params=pltpu.CompilerParams(has_side_effects=_EFFECT),
    )(*[_in_hbm(a) for a in blocks + wholes])
    return out[0], out[1], out[2:2 + n], out[2 + n:2 + 2 * n], out[-1]


def _gather_wait(name, send_sems, recv_sems, blocks, wholes, after):
    n = len(blocks)

    def body(*refs):
        src_refs, land_refs, send_ref, recv_ref = refs[:n], refs[n:2 * n], refs[2 * n], refs[2 * n + 1]
        x, y, c, me = _place()
        for a in range(n):
            r = src_refs[a].shape[1]
            for k in range(1, N_DEV):
                peer, peer_slot = _peer(x, y, c, k)
                cp = pltpu.make_async_remote_copy(
                    src_ref=src_refs[a], dst_ref=land_refs[a].at[:, pl.ds(peer_slot * r, r), :], send_sem=send_ref.at[a * N_DEV + k],
                    recv_sem=recv_ref.at[a * N_DEV + k], device_id=peer, device_id_type=pl.DeviceIdType.MESH)
                cp.wait_send()
                cp.wait_recv()

    return pl.pallas_call(
        body, name=name, out_shape=tuple(pltpu.HBM(a.shape, a.dtype) for a in list(blocks) + list(wholes)),
        in_specs=[_HBM] * (2 * n) + [_SEM, _SEM, pl.BlockSpec(memory_space=pl.ANY)], out_specs=[_HBM] * (2 * n),
        input_output_aliases={a: a for a in range(2 * n)},
        compiler_params=pltpu.CompilerParams(has_side_effects=_EFFECT),
    )(*blocks, *wholes, send_sems, recv_sems, after)


def _reduce_adamw(parts, w, m, v, tile_rows=PACK_TILE):
    rows = w.shape[0]
    nt = rows // tile_rows
    c1 = 1.0 - ADAM_B1 ** ADAM_STEP
    c2 = 1.0 - ADAM_B2 ** ADAM_STEP

    def body(p_ref, w_ref, m_ref, v_ref, g_out, d_out, m_out, v_out):
        g = p_ref[0]
        for s in range(1, N_DEV):
            g = g + p_ref[s]
        m_new = ADAM_B1 * m_ref[...] + (1.0 - ADAM_B1) * g
        v_new = ADAM_B2 * v_ref[...] + (1.0 - ADAM_B2) * (g * g)
        g_out[...] = g
        m_out[...] = m_new
        v_out[...] = v_new
        d_out[...] = -ADAM_LR * ((m_new / c1) / (jnp.sqrt(v_new / c2) + ADAM_EPS) + ADAM_WD * w_ref[...])

    r = _rows_spec(tile_rows, LANES, nt)
    out = _sds((rows, LANES))
    return _call(
        body, "reduce_adamw", nt,
        [pl.BlockSpec((N_DEV, tile_rows, LANES), lambda i: (0, i, 0)), r, r, r], [r, r, r, r], [out, out, out, out],
    )(parts, w, m, v)


def _pack_rows(a, lead=0):
    head = a.shape[:lead]
    flat = a.reshape(head + (-1,))
    size = flat.shape[-1]
    rows = -(-size // (16 * LANES)) * 16
    flat = jnp.pad(flat, [(0, 0)] * lead + [(0, rows * LANES - size)])
    return flat.reshape(head + (rows, LANES))


def _packed_rows(shape):
    return -(-math.prod(shape) // (16 * LANES)) * 16


def _to_blocks(full, axis):
    l, a, b = full.shape
    if axis == 2:
        return full.reshape(l, a, N_DEV, b // N_DEV).transpose(2, 0, 1, 3)
    return full.reshape(l, N_DEV, a // N_DEV, b).transpose(1, 0, 2, 3)


def _from_blocks(blocks, axis):
    _, l, a, b = blocks.shape
    if axis == 2:
        return blocks.transpose(1, 2, 0, 3).reshape(l, a, N_DEV * b)
    return blocks.transpose(1, 0, 2, 3).reshape(l, N_DEV * a, b)


def _row_form(shard, transposed):
    return shard.transpose(0, 2, 1) if transposed else shard


def _me():
    return 4 * lax.axis_index("x") + 2 * lax.axis_index("y") + lax.axis_index("c")


def _both_forms(names, wholes):
    out = {}
    for name, w in zip(names, wholes):
        t = dict(BIG)[name]
        out[name + '_t' if t else name] = w
        out[name if t else name + '_t'] = w.transpose(0, 2, 1)
    return out


def _gather_weights(local):
    segs, meta = [], []
    for name in SMALL_SHARDED:
        blk = local[name]
        if name in GATHER_F32:
            bits = lax.bitcast_convert_type(blk, MXU)
        else:
            bits = _mx(blk)
        seg = _pack_rows(bits)
        meta.append((name, bits.shape, seg.shape[0]))
        segs.append(seg)
    blocks = {name: _mx(_row_form(local[name], t)) for name, t in BIG}
    mix, ffn = PARTS['mix'], PARTS['ffn']
    *first, gathered = _all_gather([blocks[n][:1] for n in mix], jnp.concatenate(segs, axis=0))
    flights = {'ffn0': _gather_start("gather_ffn0_start", [blocks[n][:1] for n in ffn]),
               'later': _gather_start("gather_later_start", [blocks[n][1:] for n in mix + ffn])}
    out, lo = {}, 0
    for name, bits_shape, rows in meta:
        seg = gathered[:, lo:lo + rows].reshape(N_DEV, -1)[:, :math.prod(bits_shape)].reshape((N_DEV,) + bits_shape)
        if name in GATHER_F32:
            seg = lax.bitcast_convert_type(seg, F32)
        out[name] = _from_blocks(seg, SHARD_AXIS[name])
        lo += rows
    ready = {(0, 'mix'): {k: v[0] for k, v in _both_forms(mix, first).items()}}

    def landed(flight, names, after):
        send_sems, recv_sems, mine, wholes, _ = flights[flight]
        done = _gather_wait(f"gather_{flight}_wait", send_sems, recv_sems, mine, wholes, after)
        own = [lax.dynamic_update_slice(w, b, (0, _me() * b.shape[1], 0)) for b, w in zip(done[:len(names)], done[len(names):])]
        return _both_forms(names, own)

    def big_weights(l, part, after):
        if (l, part) not in ready and l == 0:
            ready[(0, 'ffn')] = {k: v[0] for k, v in landed('ffn0', ffn, after).items()}
        elif (l, part) not in ready:
            forms = landed('later', mix + ffn, after)
            for j in range(1, DEPTH):
                for p, names in PARTS.items():
                    ready[(j, p)] = {k: forms[k][j - 1] for n in names for k in (n, n + '_t')}
        return ready[(l, part)]

    return out, big_weights, flights['ffn0'][-1][0, 0] + flights['later'][-1][0, 0]


def _s5_discretize(a_re, a_im, log_dt, b_re, b_im):
    lam_re = jnp.minimum(a_re, -1e-4)
    lam_im = a_im
    dt = jnp.exp(log_dt)[:, None]
    decay = jnp.exp(dt * lam_re)
    ang = dt * lam_im
    abar_re = decay * jnp.cos(ang)
    abar_im = decay * jnp.sin(ang)
    den = jnp.square(lam_re) + jnp.square(lam_im)
    nr = abar_re - 1.0
    ni = abar_im
    coef_re = (nr * lam_re + ni * lam_im) / den
    coef_im = (ni * lam_re - nr * lam_im) / den
    bbar_re = coef_re[..., None] * b_re - coef_im[..., None] * b_im
    bbar_im = coef_re[..., None] * b_im + coef_im[..., None] * b_re
    return abar_re, abar_im, bbar_re, bbar_im


def _complex_powers(ar, ai, count):
    def combine(e1, e2):
        return e2[0] * e1[0] - e2[1] * e1[1], e2[0] * e1[1] + e2[1] * e1[0]

    shape = (count,) + ar.shape
    return lax.associative_scan(combine, (jnp.broadcast_to(ar, shape), jnp.broadcast_to(ai, shape)), axis=0)


_EYE16 = functools.partial(jnp.eye, 16, dtype=F32)


def _s5_params(p, l):
    disc, disc_vjp = jax.vjp(_s5_discretize, p['s5_a_re'][l], p['s5_a_im'][l], p['s5_log_dt'][l], p['s5_b_re'][l], p['s5_b_im'][l])
    abar_re, abar_im, bbar_re, bbar_im = disc
    ar, ai = abar_re.reshape(N_STATE), abar_im.reshape(N_STATE)
    ap_re, ap_im = _complex_powers(ar, ai, TILE)
    one, zero = jnp.ones((1, N_STATE), F32), jnp.zeros((1, N_STATE), F32)
    seg_re = jnp.concatenate([one, ap_re[SEG - 1:TILE - SEG:SEG]], axis=0)
    seg_im = jnp.concatenate([zero, ap_im[SEG - 1:TILE - SEG:SEG]], axis=0)
    doubling = [SEG - 1, 2 * SEG - 1, 4 * SEG - 1]

    src = (jnp.arange(TILE) % N_SEG) * SEG + jnp.arange(TILE) // N_SEG
    perm = (src[:, None] == jnp.arange(TILE)[None, :]).astype(MXU)
    prm = {
        'perm': perm, 'perm_t': perm.T,
        'pw_re': ap_re[:SEG], 'pw_im': ap_im[:SEG],
        'dbl_re': jnp.concatenate([jnp.stack([ap_re[k] for k in doubling]), jnp.zeros((5, N_STATE), F32)], axis=0),
        'dbl_im': jnp.concatenate([jnp.stack([ap_im[k] for k in doubling]), jnp.zeros((5, N_STATE), F32)], axis=0),
        'seg_re': seg_re, 'seg_im': seg_im, 'segr_re': seg_re[::-1], 'segr_im': seg_im[::-1],
        'b_re': _mx(jnp.einsum('gpc,gh->gchp', bbar_re, _EYE16()).reshape(D_S5, N_STATE)),
        'b_im': _mx(jnp.einsum('gpc,gh->gchp', bbar_im, _EYE16()).reshape(D_S5, N_STATE)),
        'c_re': _mx(jnp.einsum('gcp,gh->gphc', p['s5_c_re'][l], _EYE16()).reshape(N_STATE, D_S5)),
        'c_im': _mx(jnp.einsum('gcp,gh->gphc', p['s5_c_im'][l], _EYE16()).reshape(N_STATE, D_S5)),
        'd': p['s5_d'][l][None, :], 'glu_w': p['s5_glu_w'][l], 'glu_b': p['s5_glu_b'][l][None, :],
    }
    return prm, disc_vjp


def _lru_params(p, l):
    eye4 = jnp.eye(4, dtype=F32)
    return {
        'conv_w': p['lru_conv_w'][l], 'conv_b': p['lru_conv_b'][l][None, :],
        'wx': _mx(jnp.einsum('hij,hk->hikj', p['lru_wx'][l], eye4).reshape(D_LRU, D_LRU)),
        'wa': _mx(jnp.einsum('hij,hk->hikj', p['lru_wa'][l], eye4).reshape(D_LRU, D_LRU)),
        'bx': p['lru_bx'][l][None, :], 'ba': p['lru_ba'][l][None, :],
        'sp': jax.nn.softplus(-p['lru_a_param'][l])[None, :],
    }


def _rope_tables(n):
    inv_freq = ROPE_THETA ** (-jnp.arange(0, 64, 2, dtype=F32) / 64)
    ang = jnp.arange(n, dtype=F32)[:, None] * inv_freq[None, :]
    cos, sin = jnp.cos(ang), jnp.sin(ang)
    return jnp.concatenate([cos, cos, cos, cos], axis=1), jnp.concatenate([-sin, sin, -sin, sin], axis=1)


def _sink_cols(sinks):
    nb = TILE_Q // ATTN_BLOCK
    per_unit = sinks.reshape(4, 2).T
    return jnp.broadcast_to(per_unit[:, None, :, None], (2, nb, 4, ATTN_BLOCK)).reshape(2, nb * 4 * ATTN_BLOCK, 1)


def _local_step(x, target, p, big_weights=None, emit_grads=None):
    if big_weights is None:
        big_weights = lambda l, part, after: {k: p[k][l] for name in PARTS[part] for k in (name, name + '_t')}
    if emit_grads is None:
        emit_grads = lambda l, part, grads: 0.0
    n = x.shape[0]
    cos_t, sin_t = _rope_tables(n)
    row = lambda a: a[None, :]
    saved = []
    h = x
    for l in range(DEPTH):
        s = {'x0': h}
        bw = s['bw'] = dict(big_weights(l, 'mix', h))
        s['q'], k, v, s['u'], s['xr'], s['gate'] = _inproj_fwd(h, bw['w_in'], row(p['b_in'][l]), cos_t, sin_t)
        no_keys = jnp.zeros((ATTN_BLOCK, D_KV), MXU)
        s['k'], s['v'] = jnp.concatenate([no_keys, k], axis=0), jnp.concatenate([no_keys, v], axis=0)
        s['sinks'] = _sink_cols(p['attn_sinks'][l])
        s['ya'] = _attn_fwd(s['q'], s['k'], s['v'], s['sinks'])
        s['s5'], s['s5_vjp'] = _s5_params(p, l)
        s['lru'] = _lru_params(p, l)
        s['ys'], s['s5_cr'], s['s5_ci'] = _s5_fwd(s['u'], s['s5'])
        s['yl'], s['lru_c'] = _lru_fwd(s['xr'], s['gate'], s['lru'])
        s['mix'], s['r1'], s['x1'] = _mixout_fwd(s['ya'], s['ys'], s['yl'], h, row(p['mix_norm_g'][l]), bw['w_out'],
                                                 row(p['b_out'][l]), row(p['ln1_g'][l]), row(p['ln1_b'][l]))
        bw.update(big_weights(l, 'ffn', s['x1']))
        s['gp'], s['up'], s['r2'], h = _ffn_fwd(s['x1'], bw['ffn_w_gate'], bw['ffn_w_up'], p['ffn_conv_w'][l],
                                                row(p['ffn_conv_b'][l]), bw['ffn_w_down'], row(p['ln2_g'][l]), row(p['ln2_b'][l]))
        saved.append(s)
    loss, dh = _loss_head(h, target)
    placed = 0.0

    grads = {name: [None] * DEPTH for name in WEIGHTS}
    for l in reversed(range(DEPTH)):
        s = saved[l]
        g = {}
        (dr2, dgp, dup, g['ffn_w_down'], cw0, cw1, cw2, dcb, dg2, db2) = _ffn_bwd_down(
            dh, s['r2'], row(p['ln2_g'][l]) + placed, s['gp'], s['up'], p['ffn_conv_w'][l], row(p['ffn_conv_b'][l]),
            s['bw']['ffn_w_down_t'])
        g['ffn_conv_w'] = jnp.concatenate([cw0, cw1, cw2], axis=0)
        g['ffn_conv_b'], g['ln2_g'], g['ln2_b'] = dcb[0], dg2[0], db2[0]
        dr1, dg1, db1 = _ffn_bwd_dx(dr2, dgp, dup, s['r1'], row(p['ln1_g'][l]), s['bw']['ffn_w_gate_t'], s['bw']['ffn_w_up_t'])
        g['ffn_w_gate'], g['ffn_w_up'] = _ffn_bwd_dw(s['x1'], dgp, dup)
        g['ln1_g'], g['ln1_b'] = dg1[0], db1[0]
        placed = emit_grads(l, 'ffn', [g[name] for name in PARTS['ffn']])
        dya, dys, dyl, g['w_out'], dbo, dgm = _mixout_bwd(dr1, s['mix'], s['ya'], s['ys'], s['yl'],
                                                         row(p['mix_norm_g'][l]) + placed, s['bw']['w_out_t'])
        g['b_out'], g['mix_norm_g'] = dbo[0], dgm[0]

        du, dbr, dbi, dcr, dci, dar, dai, dd, g['s5_glu_w'], dgb = _s5_bwd(s['u'], dys, s['s5_cr'], s['s5_ci'], s['s5'])
        dxr, dgate, lw0, lw1, lw2, lw3, lcb, dwx, dwa, dbx, dba, dsp = _lru_bwd(s['xr'], s['gate'], dyl, s['lru_c'], s['lru'])
        g['lru_conv_w'] = jnp.concatenate([lw0, lw1, lw2, lw3], axis=0)
        g['lru_conv_b'], g['lru_bx'], g['lru_ba'] = lcb[0], dbx[0], dba[0]
        g['lru_wx'] = jnp.einsum('hihj->hij', dwx.reshape(4, 64, 4, 64))
        g['lru_wa'] = jnp.einsum('hihj->hij', dwa.reshape(4, 64, 4, 64))
        g['lru_a_param'] = -dsp[0] * jax.nn.sigmoid(-p['lru_a_param'][l])

        g['s5_c_re'] = jnp.einsum('gpgc->gcp', dcr.reshape(16, 64, 16, 16))
        g['s5_c_im'] = jnp.einsum('gpgc->gcp', dci.reshape(16, 64, 16, 16))
        g['s5_d'], g['s5_glu_b'] = dd[0], dgb[0]
        g['s5_a_re'], g['s5_a_im'], g['s5_log_dt'], g['s5_b_re'], g['s5_b_im'] = s['s5_vjp']((
            dar.reshape(16, 64), dai.reshape(16, 64), jnp.einsum('gcgp->gpc', dbr.reshape(16, 16, 16, 64)),
            jnp.einsum('gcgp->gpc', dbi.reshape(16, 16, 16, 64))))

        dq, dk, dv, dsink = _attn_bwd(s['q'], s['k'], s['v'], s['sinks'], s['ya'], dya)
        g['attn_sinks'] = dsink[:, 0]
        dh, g['w_in'], dbin = _inproj_bwd(dq, dk[ATTN_BLOCK:], dv[ATTN_BLOCK:], du, dxr, dgate, cos_t, sin_t, s['x0'], dr1,
                                          s['bw']['w_in_t'])
        g['b_in'] = dbin[0]
        placed = emit_grads(l, 'mix', [g[name] for name in PARTS['mix']])
        for name in WEIGHTS:
            grads[name][l] = g[name]
    big = dict(BIG)
    return loss, dh, {name: grads[name] if name in big else jnp.stack(grads[name]) for name in WEIGHTS}


def kernel(x, w_in, b_in, attn_sinks, s5_a_re, s5_a_im, s5_b_re, s5_b_im, s5_c_re, s5_c_im, s5_d, s5_log_dt, s5_glu_w, s5_glu_b, lru_conv_w, lru_conv_b, lru_wx, lru_bx, lru_wa, lru_ba, lru_a_param, mix_norm_g, w_out, b_out, ln1_g, ln1_b, ffn_w_gate, ffn_w_up, ffn_conv_w, ffn_conv_b, ffn_w_down, ln2_g, ln2_b, loss_target, m_w_in, m_b_in, m_attn_sinks, m_s5_a_re, m_s5_a_im, m_s5_b_re, m_s5_b_im, m_s5_c_re, m_s5_c_im, m_s5_d, m_s5_log_dt, m_s5_glu_w, m_s5_glu_b, m_lru_conv_w, m_lru_conv_b, m_lru_wx, m_lru_bx, m_lru_wa, m_lru_ba, m_lru_a_param, m_mix_norm_g, m_w_out, m_b_out, m_ln1_g, m_ln1_b, m_ffn_w_gate, m_ffn_w_up, m_ffn_conv_w, m_ffn_conv_b, m_ffn_w_down, m_ln2_g, m_ln2_b, v_w_in, v_b_in, v_attn_sinks, v_s5_a_re, v_s5_a_im, v_s5_b_re, v_s5_b_im, v_s5_c_re, v_s5_c_im, v_s5_d, v_s5_log_dt, v_s5_glu_w, v_s5_glu_b, v_lru_conv_w, v_lru_conv_b, v_lru_wx, v_lru_bx, v_lru_wa, v_lru_ba, v_lru_a_param, v_mix_norm_g, v_w_out, v_b_out, v_ln1_g, v_ln1_b, v_ffn_w_gate, v_ffn_w_up, v_ffn_conv_w, v_ffn_conv_b, v_ffn_w_down, v_ln2_g, v_ln2_b):
    given = dict(locals())
    whole = {name: given[name] for name in WEIGHTS if name not in dict(BIG)}
    small_whole, big_weights, placed = _gather_weights({name: given[name] for name in SHARDED})
    whole.update(small_whole)
    whole['b_in'] = whole['b_in'] + placed
    in_flight = {}

    def emit_grads(l, part, grads):
        in_flight[(l, part)] = _scatter_start(f"grads_start_{part}{l}", grads)
        return in_flight[(l, part)][-1][0, 0]

    loss, grad_x, grads = _local_step(x[0], loss_target[0], whole, big_weights, emit_grads)
    total = lax.psum(loss[0, 0], ("x", "y", "c"))
    return (total, grad_x[None], *_update(given, grads, in_flight, grad_x))


def _update(given, grads, in_flight, after):
    local_w = {name: given[name] for name in WEIGHTS}
    me = _me()
    outs = {}

    rest = SMALL_SHARDED + REPLICATED
    pad = -sum(_packed_rows(local_w[name].shape) for name in rest) % PACK_TILE
    small = jnp.concatenate(
        [_pack_rows(_to_blocks(grads[name], SHARD_AXIS[name]), lead=1) for name in SMALL_SHARDED]
        + [jnp.broadcast_to(_pack_rows(grads[name]), (N_DEV, _packed_rows(grads[name].shape), LANES)) for name in REPLICATED]
        + [jnp.zeros((N_DEV, pad, LANES), F32)], axis=1)
    small_rows = small.shape[1]
    small_flight = _scatter_start("grads_start_small", [small.reshape(N_DEV * small_rows, LANES)])

    def summed(name, flight, rows_of, packed, after):
        send_sems, recv_sems, srcs, landing, _ = flight
        srcs, landing = _scatter_wait(name, send_sems, recv_sems, srcs, landing, after)
        own = jnp.concatenate([lax.dynamic_slice_in_dim(g, me * r, r, axis=0) for g, r in zip(srcs, rows_of)], axis=0)
        parts = lax.dynamic_update_slice(landing, own[None], (me, 0, 0))
        return _reduce_adamw(parts, *packed, tile_rows=own.shape[0] // 4)

    forms = {part: [(name, dict(BIG)[name], _row_form(local_w[name], dict(BIG)[name]).shape[1]) for name in names]
             for part, names in PARTS.items()}
    for (l, part), flight in in_flight.items():
        packed = [jnp.concatenate([_row_form(given[prefix + name], t)[l] for name, t, _ in forms[part]], axis=0)
                  for prefix in ('', 'm_', 'v_')]
        outs[(l, part)] = summed(f"grads_wait_{part}{l}", flight, [r for _, _, r in forms[part]], packed, after)
        after = outs[(l, part)][0]

    def packed_rest(prefix):
        return jnp.concatenate([_pack_rows(given[prefix + name]) for name in rest] + [jnp.zeros((pad, LANES), F32)], axis=0)

    rest_outs = summed("grads_wait_small", small_flight, [small_rows], [packed_rest(p) for p in ('', 'm_', 'v_')], after)

    def unpack(i):
        res = {}
        for part in PARTS:
            lo = 0
            for name, t, r in forms[part]:
                res[name] = _row_form(jnp.stack([outs[(l, part)][i][lo:lo + r] for l in range(DEPTH)]), t)
                lo += r
        lo = 0
        for name in rest:
            shape = local_w[name].shape
            res[name] = rest_outs[i][lo:lo + _packed_rows(shape)].reshape(-1)[:math.prod(shape)].reshape(shape)
            lo += _packed_rows(shape)
        return [res[name] for name in WEIGHTS]

    return (*unpack(0), *unpack(1), *unpack(2), *unpack(3))
```

```python
import functools
import math

import jax
import jax.numpy as jnp
from jax import lax
from jax.experimental import pallas as pl
from jax.experimental.pallas import tpu as pltpu

F32 = jnp.float32
MXU = jnp.bfloat16

N_DEV = 8
DEPTH = 4
D = 1024
D_ATTN, D_KV, D_S5, D_LRU = 512, 128, 256, 256
D_IN = 1536
D_FF = 2816
FF_CHUNK = 256
N_STATE = 1024
LANES = 1024
ALPHA = (2 * DEPTH) ** 0.25
LN_EPS = 1e-5
RMS_EPS = 1e-6
LRU_C = 8.0
ROPE_THETA = 10000.0
ADAM_LR, ADAM_B1, ADAM_B2, ADAM_EPS, ADAM_WD, ADAM_STEP = 0.001, 0.9, 0.999, 1e-08, 0.01, 10

TILE = 256
S5_PAIR = 2
N_SEG = 8
SEG = TILE // N_SEG
TILE_Q = 512
TILE_BIG = 512
TILE_WIDE = 512
ATTN_BLOCK = 128
PACK_TILE = 256
VMEM_MB = 56

WEIGHTS = ['w_in', 'b_in', 'attn_sinks', 's5_a_re', 's5_a_im', 's5_b_re', 's5_b_im', 's5_c_re', 's5_c_im', 's5_d', 's5_log_dt',
           's5_glu_w', 's5_glu_b', 'lru_conv_w', 'lru_conv_b', 'lru_wx', 'lru_bx', 'lru_wa', 'lru_ba', 'lru_a_param', 'mix_norm_g',
           'w_out', 'b_out', 'ln1_g', 'ln1_b', 'ffn_w_gate', 'ffn_w_up', 'ffn_conv_w', 'ffn_conv_b', 'ffn_w_down', 'ln2_g', 'ln2_b']
SHARD_AXIS = {'w_in': 2, 's5_glu_w': 1, 'lru_conv_w': 2, 'w_out': 1, 'ffn_w_gate': 2, 'ffn_w_up': 2, 'ffn_conv_w': 2,
              'ffn_w_down': 1}
SHARDED = [n for n in WEIGHTS if n in SHARD_AXIS]
REPLICATED = [n for n in WEIGHTS if n not in SHARD_AXIS]
BIG = [(n, SHARD_AXIS[n] == 2) for n in ('w_in', 'w_out', 'ffn_w_gate', 'ffn_w_up', 'ffn_w_down')]
SMALL_SHARDED = [n for n in SHARDED if n not in dict(BIG)]
PARTS = {'mix': ['w_in', 'w_out'], 'ffn': ['ffn_w_gate', 'ffn_w_up', 'ffn_w_down']}
GATHER_F32 = ('lru_conv_w', 'ffn_conv_w')


def _dot(a, b):
    return jnp.dot(a, b, preferred_element_type=F32)


def _dot_nt(a, b):
    return lax.dot_general(a, b, (((1,), (1,)), ((), ())), preferred_element_type=F32)


def _dot_tn(a, b):
    return lax.dot_general(a, b, (((0,), (0,)), ((), ())), preferred_element_type=F32)


def _mx(a):
    return a.astype(MXU)


_GELU_C = math.sqrt(2.0 / math.pi)


def _gelu(x):
    th = jnp.tanh(_GELU_C * (x + 0.044715 * x * x * x))
    return 0.5 * x * (1.0 + th)


def _gelu_grad(x):
    th = jnp.tanh(_GELU_C * (x + 0.044715 * x * x * x))
    return 0.5 * (1.0 + th) + 0.5 * x * (1.0 - th * th) * _GELU_C * (1.0 + 3.0 * 0.044715 * x * x)


def _sigmoid(x):
    return 0.5 * jnp.tanh(0.5 * x) + 0.5


def _ln_stats(r):
    mu = jnp.mean(r, axis=-1, keepdims=True)
    xc = r - mu
    var = jnp.mean(xc * xc, axis=-1, keepdims=True)
    rstd = lax.rsqrt(var + LN_EPS)
    return xc * rstd, rstd


def _ln_bwd(dy, g, xhat, rstd):
    dxh = dy * g
    return rstd * (dxh - jnp.mean(dxh, axis=-1, keepdims=True) - xhat * jnp.mean(dxh * xhat, axis=-1, keepdims=True))


def _rms(y):
    return lax.rsqrt(jnp.mean(y * y, axis=-1, keepdims=True) + RMS_EPS)


def _sum0(a):
    return jnp.sum(a, axis=0, keepdims=True)


def _row_iota(shape):
    return lax.broadcasted_iota(jnp.int32, shape, 0)


def _shift_down(ext, j, rows):
    return pltpu.roll(ext, j, 0)[8:8 + rows]


def _shift_up(ext, j, rows):
    return pltpu.roll(ext, ext.shape[0] - j, 0)[:rows]


def _swap_halves(t):
    w = t.shape[1]
    lane = lax.broadcasted_iota(jnp.int32, t.shape, 1)
    return jnp.where((lane & 32) == 0, pltpu.roll(t, w - 32, 1), pltpu.roll(t, 32, 1))


def _rope(t, cos, sin_signed):
    return t * cos + _swap_halves(t) * sin_signed


def _rope_t(d, cos, sin_signed):
    return d * cos + _swap_halves(d * sin_signed)


def _cmul_add(ar, ai, xr, xi, yr, yi):
    return ar * xr - ai * xi + yr, ar * xi + ai * xr + yi


def _seg_rows(k):
    return slice(N_SEG * k, N_SEG * (k + 1))


def _permute_rows(perm, x):
    hi = _mx(x)
    rest = x - hi.astype(F32)
    mid = _mx(rest)
    lo = _mx(rest - mid.astype(F32))
    return _dot(perm, hi) + _dot(perm, mid) + _dot(perm, lo)


def _cscan(sr, si, tab, cin_r, cin_i, reverse):
    sgn = -1.0 if reverse else 1.0
    pw_re, pw_im, dbl_re, dbl_im = tab['pw_re'], tab['pw_im'], tab['dbl_re'], tab['dbl_im']
    ar, ai = pw_re[0:1, :], sgn * pw_im[0:1, :]
    shape = (N_SEG, sr.shape[1])
    hr = hi = jnp.zeros(shape, F32)
    for k in (range(SEG - 1, -1, -1) if reverse else range(SEG)):
        hr, hi = _cmul_add(ar, ai, hr, hi, sr[_seg_rows(k), :], si[_seg_rows(k), :])
        sr[_seg_rows(k), :] = hr
        si[_seg_rows(k), :] = hi
    sub = _row_iota(shape)

    def shifted(v, d):
        if reverse:
            return jnp.where(sub < N_SEG - d, pltpu.roll(v, N_SEG - d, 0), 0.0)
        return jnp.where(sub >= d, pltpu.roll(v, d, 0), 0.0)

    fr, fi = hr, hi
    for j, d in enumerate((1, 2, 4)):
        fr, fi = _cmul_add(dbl_re[j:j + 1, :], sgn * dbl_im[j:j + 1, :], shifted(fr, d), shifted(fi, d), fr, fi)
    seg_re, seg_im = (tab['segr_re'], tab['segr_im']) if reverse else (tab['seg_re'], tab['seg_im'])
    cr, ci = _cmul_add(seg_re[...], sgn * seg_im[...], cin_r, cin_i, shifted(fr, 1), shifted(fi, 1))
    nr, ni = _cmul_add(dbl_re[0:1, :], sgn * dbl_im[0:1, :], cr, ci, hr, hi)
    for k in range(SEG):
        j = SEG - 1 - k if reverse else k
        xr, xi = _cmul_add(pw_re[j:j + 1, :], sgn * pw_im[j:j + 1, :], cr, ci, sr[_seg_rows(k), :], si[_seg_rows(k), :])
        sr[_seg_rows(k), :] = xr
        si[_seg_rows(k), :] = xi
    edge = slice(0, 1) if reverse else slice(N_SEG - 1, N_SEG)
    return nr[edge], ni[edge]


def _rscan(a, b, reverse):
    rows = a.shape[0]
    row = _row_iota(a.shape)
    s = 1
    while s < rows:
        if reverse:
            keep = row < rows - s
            sa = jnp.where(keep, pltpu.roll(a, rows - s, 0), 1.0)
            sb = jnp.where(keep, pltpu.roll(b, rows - s, 0), 0.0)
        else:
            keep = row >= s
            sa = jnp.where(keep, pltpu.roll(a, s, 0), 1.0)
            sb = jnp.where(keep, pltpu.roll(b, s, 0), 0.0)
        b = b + a * sb
        a = a * sa
        s *= 2
    return a, b


def _whole():
    return pl.BlockSpec(memory_space=pltpu.VMEM)


def _rows_spec(rows, cols, n_tiles, reverse=False):
    if reverse:
        return pl.BlockSpec((rows, cols), lambda i: (n_tiles - 1 - i, 0))
    return pl.BlockSpec((rows, cols), lambda i: (i, 0))


def _halo_spec(cols, tile_rows, n_tiles, reverse=False):
    per = tile_rows // 8
    if reverse:
        return pl.BlockSpec((8, cols), lambda i: (jnp.maximum((n_tiles - 1 - i) * per - 1, 0), 0))
    return pl.BlockSpec((8, cols), lambda i: (jnp.maximum(i * per - 1, 0), 0))


def _call(body, name, n_tiles, in_specs, out_specs, out_shape, scratch=()):
    return pl.pallas_call(
        body, name=name, grid=(n_tiles,), in_specs=in_specs, out_specs=out_specs, out_shape=out_shape,
        scratch_shapes=list(scratch),
        compiler_params=pltpu.CompilerParams(dimension_semantics=("arbitrary",), vmem_limit_bytes=VMEM_MB << 20))


def _sds(shape, dtype=F32):
    return jax.ShapeDtypeStruct(shape, dtype)


def _inproj_fwd(x, w, b, cos_t, sin_t):
    n = x.shape[0]
    nt = n // TILE_WIDE

    def body(x_ref, w_ref, b_ref, c_ref, s_ref, q_ref, k_ref, v_ref, u_ref, xr_ref, g_ref):
        p = _dot(_mx(x_ref[...]), w_ref[...]) + b_ref[...]
        cos, sin = c_ref[...], s_ref[...]
        q_ref[...] = _mx(_rope(p[:, :D_ATTN], jnp.tile(cos, (1, 4)), jnp.tile(sin, (1, 4))))
        k_ref[...] = _mx(_rope(p[:, 512:640], cos, sin))
        v_ref[...] = _mx(p[:, 640:768])
        u_ref[...] = p[:, 768:1024]
        xr_ref[...] = p[:, 1024:1280]
        g_ref[...] = p[:, 1280:1536]

    r = functools.partial(_rows_spec, n_tiles=nt)
    return _call(
        body, "inproj_fwd", nt,
        [r(TILE_WIDE, D), _whole(), _whole(), r(TILE_WIDE, 128), r(TILE_WIDE, 128)],
        [r(TILE_WIDE, D_ATTN), r(TILE_WIDE, D_KV), r(TILE_WIDE, D_KV), r(TILE_WIDE, D_S5), r(TILE_WIDE, D_LRU), r(TILE_WIDE, D_LRU)],
        [_sds((n, D_ATTN), MXU), _sds((n, D_KV), MXU), _sds((n, D_KV), MXU), _sds((n, D_S5)), _sds((n, D_LRU)), _sds((n, D_LRU))],
    )(x, w, b, cos_t, sin_t)


def _inproj_bwd(dq, dk, dv, du, dxr, dgate, cos_t, sin_t, x0, dr1, w_t):
    n = x0.shape[0]
    nt = n // TILE_WIDE

    def body(dq_ref, dk_ref, dv_ref, du_ref, dxr_ref, dg_ref, c_ref, s_ref, x_ref, dr_ref, w_ref, dx_ref, dw_ref, db_ref):
        @pl.when(pl.program_id(0) == 0)
        def _():
            dw_ref[...] = jnp.zeros_like(dw_ref)
            db_ref[...] = jnp.zeros_like(db_ref)

        cos, sin = c_ref[...], s_ref[...]
        dtq = _rope_t(dq_ref[...], jnp.tile(cos, (1, 4)), jnp.tile(sin, (1, 4)))
        dtk = _rope_t(dk_ref[...], cos, sin)
        dp = jnp.concatenate([dtq, dtk, dv_ref[...], du_ref[...], dxr_ref[...], dg_ref[...]], axis=1)
        db_ref[...] += _sum0(dp)
        dpb = _mx(dp)
        dw_ref[...] += _dot_tn(dpb, _mx(x_ref[...]))
        dx_ref[...] = ALPHA * dr_ref[...] + _dot(dpb, w_ref[...])

    r = functools.partial(_rows_spec, n_tiles=nt)
    return _call(
        body, "inproj_bwd", nt,
        [r(TILE_WIDE, D_ATTN), r(TILE_WIDE, D_KV), r(TILE_WIDE, D_KV), r(TILE_WIDE, D_S5), r(TILE_WIDE, D_LRU), r(TILE_WIDE, D_LRU),
         r(TILE_WIDE, 128), r(TILE_WIDE, 128), r(TILE_WIDE, D), r(TILE_WIDE, D), _whole()],
        [r(TILE_WIDE, D), _whole(), _whole()],
        [_sds((n, D)), _sds((D_IN, D)), _sds((1, D_IN))],
    )(dq, dk, dv, du, dxr, dgate, cos_t, sin_t, x0, dr1, w_t)


def _kv_variants(t, lo):
    tr = pltpu.roll(t, 64, 1)
    out = []
    for j in range(2):
        first = jnp.where(lo, t if j == 0 else tr, 0.0)
        second = jnp.where(lo, 0.0, tr if j == 0 else t)
        out.append(_mx(jnp.concatenate([first, second], axis=0)))
    return out


def _kv_collect(x0, x1, lo):
    a = x0[:256] + pltpu.roll(x0[256:], 64, 1)
    b = pltpu.roll(x1[:256], 64, 1) + x1[256:]
    return jnp.where(lo, a, b)


def _row_sums(x):
    ones = jnp.ones((128, 128), MXU)
    hi = _mx(x)
    lo = _mx(x - hi.astype(F32))
    return _dot(hi, ones) + _dot(lo, ones)


def _attn_probs(s, sink_ref):
    out = []
    for hp in range(2):
        sh = s[:, hp * 128:(hp + 1) * 128]
        sink = sink_ref[hp]
        m = jnp.maximum(jnp.broadcast_to(jnp.max(sh, axis=1, keepdims=True), sh.shape), sink)
        p = jnp.exp(sh - m)
        es = jnp.exp(sink - m)
        inv = 1.0 / (_row_sums(p) + es)
        out.append((p * inv, es * inv))
    return out


def _band_merge(x, tri, no_previous=None):
    bands = []
    for hp in range(2):
        prev, own = x[:, hp * 256:hp * 256 + 128], x[:, hp * 256 + 128:hp * 256 + 256]
        if no_previous is not None:
            prev = jnp.where(no_previous, -jnp.inf, prev)
        bands.append(jnp.where(tri, own, prev))
    return jnp.concatenate(bands, axis=1)


def _band_split(y, tri):
    parts = []
    for hp in range(2):
        band = y[:, hp * 128:(hp + 1) * 128]
        parts += [jnp.where(tri, 0.0, band), jnp.where(tri, band, 0.0)]
    return jnp.concatenate(parts, axis=1)


def _tri():
    shape = (ATTN_BLOCK, ATTN_BLOCK)
    return lax.broadcasted_iota(jnp.int32, shape, 0) >= lax.broadcasted_iota(jnp.int32, shape, 1)


def _attn_scores(q_ref, k_ref, v_ref, nb, tri):
    lo = lax.broadcasted_iota(jnp.int32, (256, 128), 1) < 64
    kcats, vcats, kstarts, parts = [], [], [], []
    for b in range(nb):
        block = pl.program_id(0) * nb + b
        kstart = pl.multiple_of(block * ATTN_BLOCK, ATTN_BLOCK)
        kcat = _kv_variants(k_ref[pl.ds(kstart, 256), :].astype(F32), lo)
        kcats.append(kcat)
        vcats.append(_kv_variants(v_ref[pl.ds(kstart, 256), :].astype(F32), lo))
        kstarts.append(kstart)
        for j in range(2):
            s = _dot_nt(_kv_group(q_ref, b, j), kcat[j]) * 0.125
            parts += [_band_merge(s[:ATTN_BLOCK], tri, block == 0), _band_merge(s[ATTN_BLOCK:], tri, block == 0)]
    return jnp.concatenate(parts, axis=0), kcats, vcats, kstarts


def _kv_group(a, b, j):
    rows = slice(b * ATTN_BLOCK, (b + 1) * ATTN_BLOCK)
    return jnp.concatenate([a[rows, 2 * j * 128:(2 * j + 1) * 128], a[rows, (2 * j + 1) * 128:(2 * j + 2) * 128]], axis=0)


def _put_kv_group(ref, b, j, x):
    rows = slice(b * ATTN_BLOCK, (b + 1) * ATTN_BLOCK)
    ref[rows, 2 * j * 128:(2 * j + 1) * 128] = x[:ATTN_BLOCK]
    ref[rows, (2 * j + 1) * 128:(2 * j + 2) * 128] = x[ATTN_BLOCK:]


def _band_split_group(y, unit, tri):
    return _mx(jnp.concatenate([_band_split(y[unit:unit + ATTN_BLOCK], tri),
                                _band_split(y[unit + ATTN_BLOCK:unit + 2 * ATTN_BLOCK], tri)], axis=0))


def _attn_fwd(q, k, v, sink_cols):
    n = q.shape[0]
    nt = n // TILE_Q
    nb = TILE_Q // ATTN_BLOCK

    def body(q_ref, k_ref, v_ref, s_ref, o_ref):
        tri = _tri()
        s, _, vcats, _ = _attn_scores(q_ref, k_ref, v_ref, nb, tri)
        (p0, _), (p1, _) = _attn_probs(s, s_ref)
        p = jnp.concatenate([p0, p1], axis=1)
        for b in range(nb):
            for j in range(2):
                _put_kv_group(o_ref, b, j, _dot(_band_split_group(p, (b * 4 + 2 * j) * ATTN_BLOCK, tri), vcats[b][j]))

    return _call(
        body, "attn_fwd", nt,
        [_rows_spec(TILE_Q, D_ATTN, nt), _whole(), _whole(), _whole()],
        _rows_spec(TILE_Q, D_ATTN, nt), _sds((n, D_ATTN)),
    )(q, k, v, sink_cols)


def _attn_bwd(q, k, v, sink_cols, o, do):
    n = q.shape[0]
    nt = n // TILE_Q
    nb = TILE_Q // ATTN_BLOCK

    def body(q_ref, k_ref, v_ref, s_ref, o_ref, do_ref, dq_ref, dk_ref, dv_ref, ds_ref):
        @pl.when(pl.program_id(0) == 0)
        def _():
            dk_ref[...] = jnp.zeros_like(dk_ref)
            dv_ref[...] = jnp.zeros_like(dv_ref)
            ds_ref[...] = jnp.zeros_like(ds_ref)

        lo = lax.broadcasted_iota(jnp.int32, (256, 128), 1) < 64
        tri = _tri()
        s, kcats, vcats, kstarts = _attn_scores(q_ref, k_ref, v_ref, nb, tri)
        probs = _attn_probs(s, s_ref)
        do = do_ref[...]
        dob = _mx(do)
        od = do * o_ref[...]
        lo_q = (lax.broadcasted_iota(jnp.int32, od.shape, 1) & 64) == 0
        od_head = (jnp.where(lo_q, od, 0.0), jnp.where(lo_q, 0.0, od))
        units = [(b, i) for b in range(nb) for i in range(4)]

        def tile_part(a, b, i):
            return a[b * ATTN_BLOCK:(b + 1) * ATTN_BLOCK, i * 128:(i + 1) * 128]

        dp = []
        for b in range(nb):
            for j in range(2):
                x = _dot_nt(_kv_group(dob, b, j), vcats[b][j])
                dp += [_band_merge(x[:ATTN_BLOCK], tri), _band_merge(x[ATTN_BLOCK:], tri)]
        dp = jnp.concatenate(dp, axis=0)
        ds = []
        for hp in range(2):
            p, p_sink = probs[hp]
            delta = _row_sums(jnp.concatenate([tile_part(od_head[hp], b, i) for b, i in units], axis=0))
            ds.append(p * (dp[:, hp * 128:(hp + 1) * 128] - delta) * 0.125)
            t = p_sink * delta
            for i in range(4):
                ds_ref[2 * i + hp:2 * i + hp + 1, :] -= sum(
                    _sum0(t[(b * 4 + i) * ATTN_BLOCK:(b * 4 + i + 1) * ATTN_BLOCK]) for b in range(nb))
        ds = jnp.concatenate(ds, axis=1)
        p = jnp.concatenate([probs[0][0], probs[1][0]], axis=1)
        for b in range(nb):
            dkc, dvc = [], []
            for j in range(2):
                unit = (b * 4 + 2 * j) * ATTN_BLOCK
                dsb = _band_split_group(ds, unit, tri)
                _put_kv_group(dq_ref, b, j, _dot(dsb, kcats[b][j]))
                dkc.append(_dot_tn(dsb, _kv_group(q_ref, b, j)))
                dvc.append(_dot_tn(_band_split_group(p, unit, tri), _kv_group(dob, b, j)))
            dk_ref[pl.ds(kstarts[b], 256), :] += _kv_collect(dkc[0], dkc[1], lo)
            dv_ref[pl.ds(kstarts[b], 256), :] += _kv_collect(dvc[0], dvc[1], lo)

    r = _rows_spec(TILE_Q, D_ATTN, nt)
    return _call(
        body, "attn_bwd", nt,
        [r, _whole(), _whole(), _whole(), r, r],
        [r, _whole(), _whole(), _whole()],
        [_sds((n, D_ATTN)), _sds((n + ATTN_BLOCK, D_KV)), _sds((n + ATTN_BLOCK, D_KV)), _sds((8, 128))],
    )(q, k, v, sink_cols, o, do)


S5_TABLES = ('pw_re', 'pw_im', 'dbl_re', 'dbl_im', 'seg_re', 'seg_im', 'segr_re', 'segr_im')
S5_WEIGHTS = ('b_re', 'b_im', 'c_re', 'c_im', 'd', 'glu_w', 'glu_b', 'perm', 'perm_t')


def _s5_states(u, carry_r, carry_i, b_re, b_im, tab, hr_s, hi_s):
    ub = _mx(u)
    hr_s[...] = _dot(ub, b_re[...])
    hi_s[...] = _dot(ub, b_im[...])
    return ub, _cscan(hr_s, hi_s, tab, carry_r, carry_i, reverse=False)


def _s5_fwd(u, prm):
    n = u.shape[0]
    nt = n // (S5_PAIR * TILE)

    def body(u_ref, *refs):
        tab = dict(zip(S5_TABLES, refs[:8]))
        b_re, b_im, c_re, c_im, d_ref, gw_ref, gb_ref, perm, perm_t = refs[8:17]
        y_ref, cr_out, ci_out, cr_s, ci_s = refs[17:22]
        states = [refs[22 + 2 * j:24 + 2 * j] for j in range(S5_PAIR)]
        rows = [slice(j * TILE, (j + 1) * TILE) for j in range(S5_PAIR)]

        @pl.when(pl.program_id(0) == 0)
        def _():
            cr_s[...] = jnp.zeros_like(cr_s)
            ci_s[...] = jnp.zeros_like(ci_s)

        us = []
        for j in range(S5_PAIR):
            us.append(_permute_rows(perm[...], u_ref[rows[j], :]))
            ub = _mx(us[j])
            states[j][0][...] = _dot(ub, b_re[...])
            states[j][1][...] = _dot(ub, b_im[...])
        for j in range(S5_PAIR):
            cr, ci = cr_s[...], ci_s[...]
            cr_out[8 * j:8 * j + 8, :] = jnp.broadcast_to(cr, (8, N_STATE))
            ci_out[8 * j:8 * j + 8, :] = jnp.broadcast_to(ci, (8, N_STATE))
            cr_s[...], ci_s[...] = _cscan(*states[j], tab, cr, ci, reverse=False)
        for j in range(S5_PAIR):
            hr_s, hi_s = states[j]
            y = _dot(_mx(hr_s[...]), c_re[...]) - _dot(_mx(hi_s[...]), c_im[...]) + d_ref[...] * us[j]
            z = _gelu(y)
            y_ref[rows[j], :] = _permute_rows(perm_t[...], z * _sigmoid(_dot(_mx(z), gw_ref[...]) + gb_ref[...]))

    r = functools.partial(_rows_spec, n_tiles=nt)
    return _call(
        body, "s5_fwd", nt,
        [r(S5_PAIR * TILE, D_S5)] + [_whole()] * 17,
        [r(S5_PAIR * TILE, D_S5), r(S5_PAIR * 8, N_STATE), r(S5_PAIR * 8, N_STATE)],
        [_sds((n, D_S5)), _sds((n // TILE * 8, N_STATE)), _sds((n // TILE * 8, N_STATE))],
        scratch=[pltpu.VMEM((1, N_STATE), F32)] * 2 + [pltpu.VMEM((TILE, N_STATE), F32)] * (2 * S5_PAIR),
    )(u, *[prm[k] for k in S5_TABLES + S5_WEIGHTS])


def _s5_bwd(u, dys, carry_re, carry_im, prm):
    n = u.shape[0]
    nt = n // (S5_PAIR * TILE)

    def body(u_ref, dy_ref, cin_r, cin_i, *refs):
        tab = dict(zip(S5_TABLES, refs[:8]))
        b_re, b_im, c_re, c_im, d_ref, gw_ref, gb_ref, perm, perm_t = refs[8:17]
        du_ref, dbr_ref, dbi_ref, dcr_ref, dci_ref, dar_ref, dai_ref, dd_ref, dgw_ref, dgb_ref = refs[17:27]
        gr_s, gi_s = refs[27:29]
        scratch = [refs[29 + 4 * j:33 + 4 * j] for j in range(S5_PAIR)]
        later_first = list(reversed(range(S5_PAIR)))
        rows = [slice(j * TILE, (j + 1) * TILE) for j in range(S5_PAIR)]

        @pl.when(pl.program_id(0) == 0)
        def _():
            for ref in (dbr_ref, dbi_ref, dcr_ref, dci_ref, dar_ref, dai_ref, dd_ref, dgw_ref, dgb_ref, gr_s, gi_s):
                ref[...] = jnp.zeros_like(ref)

        us, ubs, carries, dy_of = {}, {}, {}, {}
        for j in later_first:
            us[j] = _permute_rows(perm[...], u_ref[rows[j], :])
            carries[j] = (cin_r[8 * j:8 * j + 1, :], cin_i[8 * j:8 * j + 1, :])
            ubs[j], _ = _s5_states(us[j], *carries[j], b_re, b_im, tab, *scratch[j][:2])
        for j in later_first:
            u = us[j]
            hr_s, hi_s, gr_t, gi_t = scratch[j]
            hrb, hib = _mx(hr_s[...]), _mx(hi_s[...])
            y = _dot(hrb, c_re[...]) - _dot(hib, c_im[...]) + d_ref[...] * u
            z = _gelu(y)
            zb = _mx(z)
            sg = _sigmoid(_dot(zb, gw_ref[...]) + gb_ref[...])
            dout = _permute_rows(perm[...], dy_ref[rows[j], :])
            dpre = dout * z * sg * (1.0 - sg)
            dgb_ref[...] += _sum0(dpre)
            dpb = _mx(dpre)
            dgw_ref[...] += _dot_tn(zb, dpb)
            dy = (dout * sg + _dot_nt(dpb, gw_ref[...])) * _gelu_grad(y)
            dd_ref[...] += _sum0(dy * u)
            dyb = _mx(dy)
            dcr_ref[...] += _dot_tn(hrb, dyb)
            dci_ref[...] -= _dot_tn(hib, dyb)
            gr_t[...] = _dot_nt(dyb, c_re[...])
            gi_t[...] = -_dot_nt(dyb, c_im[...])
            dy_of[j] = dy
        for j in later_first:
            gr_s[...], gi_s[...] = _cscan(*scratch[j][2:], tab, gr_s[...], gi_s[...], reverse=True)
        for j in later_first:
            hr_s, hi_s, gr_t, gi_t = scratch[j]
            cr, ci = carries[j]
            sub = _row_iota((N_SEG, N_STATE))
            acc_r = acc_i = jnp.zeros((N_SEG, N_STATE), F32)
            for k in range(SEG):
                if k == 0:
                    hpr = jnp.where(sub >= 1, pltpu.roll(hr_s[_seg_rows(SEG - 1), :], 1, 0), cr)
                    hpi = jnp.where(sub >= 1, pltpu.roll(hi_s[_seg_rows(SEG - 1), :], 1, 0), ci)
                else:
                    hpr, hpi = hr_s[_seg_rows(k - 1), :], hi_s[_seg_rows(k - 1), :]
                gr, gi = gr_t[_seg_rows(k), :], gi_t[_seg_rows(k), :]
                acc_r = acc_r + gr * hpr + gi * hpi
                acc_i = acc_i + gi * hpr - gr * hpi
            dar_ref[...] += _sum0(acc_r)
            dai_ref[...] += _sum0(acc_i)
            grb, gib = _mx(gr_t[...]), _mx(gi_t[...])
            dbr_ref[...] += _dot_tn(ubs[j], grb)
            dbi_ref[...] += _dot_tn(ubs[j], gib)
            du_ref[rows[j], :] = _permute_rows(perm_t[...], dy_of[j] * d_ref[...] + _dot_nt(grb, b_re[...]) + _dot_nt(gib, b_im[...]))

    r = functools.partial(_rows_spec, n_tiles=nt, reverse=True)
    return _call(
        body, "s5_bwd", nt,
        [r(S5_PAIR * TILE, D_S5), r(S5_PAIR * TILE, D_S5), r(S5_PAIR * 8, N_STATE), r(S5_PAIR * 8, N_STATE)] + [_whole()] * 17,
        [r(S5_PAIR * TILE, D_S5)] + [_whole()] * 9,
        [_sds((n, D_S5)), _sds((D_S5, N_STATE)), _sds((D_S5, N_STATE)), _sds((N_STATE, D_S5)), _sds((N_STATE, D_S5)),
         _sds((1, N_STATE)), _sds((1, N_STATE)), _sds((1, D_S5)), _sds((D_S5, D_S5)), _sds((1, D_S5))],
        scratch=[pltpu.VMEM((1, N_STATE), F32)] * 2 + [pltpu.VMEM((TILE, N_STATE), F32)] * (4 * S5_PAIR),
    )(u, dys, carry_re, carry_im, *[prm[k] for k in S5_TABLES + S5_WEIGHTS])


def _lru_gates(xr, halo, tile_index, cw_ref, cb_ref, wx_ref, wa_ref, bx_ref, ba_ref, sp_ref):
    ext = jnp.concatenate([halo, xr], axis=0)
    sh = [xr] + [_shift_down(ext, j, TILE) for j in (1, 2, 3)]
    xc = cb_ref[...] + cw_ref[3:4, :] * sh[0] + cw_ref[2:3, :] * sh[1] + cw_ref[1:2, :] * sh[2] + cw_ref[0:1, :] * sh[3]
    xb = _mx(xc)
    gx = _sigmoid(_dot(xb, wx_ref[...]) + bx_ref[...])
    ga = _sigmoid(_dot(xb, wa_ref[...]) + ba_ref[...])
    la = -LRU_C * ga * sp_ref[...]
    a = jnp.exp(la)
    start = (tile_index * TILE + _row_iota(xr.shape)) == 0
    mult = jnp.where(start, 1.0, jnp.sqrt(-jnp.tanh(la) * (a * a + 1.0)))
    return sh, xc, xb, gx, ga, a, mult, start


def _lru_fwd(xr, gate, prm):
    n = xr.shape[0]
    nt = n // TILE

    def body(x_ref, g_ref, cw_ref, cb_ref, wx_ref, wa_ref, bx_ref, ba_ref, sp_ref, y_ref, c_out, halo_s, c_s):
        first_tile = pl.program_id(0) == 0

        @pl.when(first_tile)
        def _():
            halo_s[...] = jnp.zeros_like(halo_s)
            c_s[...] = jnp.zeros_like(c_s)

        xr = x_ref[...]
        _, xc, _, gx, _, a, mult, _ = _lru_gates(xr, halo_s[...], pl.program_id(0), cw_ref, cb_ref, wx_ref, wa_ref, bx_ref, ba_ref,
                                                 sp_ref)
        halo_s[...] = xr[TILE - 8:]
        acum, h = _rscan(a, mult * gx * xc, reverse=False)
        c = c_s[...]
        c_out[...] = jnp.broadcast_to(c, (8, D_LRU))
        h = h + acum * c
        c_s[...] = h[TILE - 1:TILE]
        y_ref[...] = h * _gelu(g_ref[...])

    r = functools.partial(_rows_spec, n_tiles=nt)
    return _call(
        body, "lru_fwd", nt,
        [r(TILE, D_LRU), r(TILE, D_LRU)] + [_whole()] * 7,
        [r(TILE, D_LRU), r(8, D_LRU)],
        [_sds((n, D_LRU)), _sds((nt * 8, D_LRU))],
        scratch=[pltpu.VMEM((8, D_LRU), F32), pltpu.VMEM((1, D_LRU), F32)],
    )(xr, gate, prm['conv_w'], prm['conv_b'], prm['wx'], prm['wa'], prm['bx'], prm['ba'], prm['sp'])


def _lru_bwd(xr, gate, dyl, carry, prm):
    n = xr.shape[0]
    nt = n // TILE

    def body(x_ref, xh_ref, g_ref, dy_ref, cin_ref, cw_ref, cb_ref, wx_ref, wa_ref, bx_ref, ba_ref, sp_ref,
             dx_ref, dg_ref, dcw0, dcw1, dcw2, dcw3, dcb_ref, dwx_ref, dwa_ref, dbx_ref, dba_ref, dsp_ref, an_s, gn_s, dn_s):
        first_tile = pl.program_id(0) == nt - 1

        @pl.when(pl.program_id(0) == 0)
        def _():
            for ref in (dcw0, dcw1, dcw2, dcw3, dcb_ref, dwx_ref, dwa_ref, dbx_ref, dba_ref, dsp_ref, gn_s, dn_s):
                ref[...] = jnp.zeros_like(ref)
            an_s[...] = jnp.ones_like(an_s)

        xr = x_ref[...]
        halo = jnp.where(first_tile, 0.0, xh_ref[...])
        sh, xc, xb, gx, ga, a, mult, start = _lru_gates(xr, halo, nt - 1 - pl.program_id(0), cw_ref, cb_ref, wx_ref, wa_ref, bx_ref,
                                                        ba_ref, sp_ref)
        acum, h = _rscan(a, mult * gx * xc, reverse=False)
        cin = cin_ref[0:1, :]
        h = h + acum * cin
        gate = g_ref[...]
        dyl = dy_ref[...]
        dg_ref[...] = dyl * h * _gelu_grad(gate)
        row = _row_iota(xr.shape)
        alpha = jnp.where(row < TILE - 1, pltpu.roll(a, TILE - 1, 0), an_s[...])
        racc, g = _rscan(alpha, dyl * _gelu(gate), reverse=True)
        g = g + racc * gn_s[...]
        an_s[...] = a[0:1]
        gn_s[...] = g[0:1]
        hprev = jnp.where(row == 0, cin, pltpu.roll(h, 1, 0))
        da = g * hprev
        dmult = jnp.where(start, 0.0, g * gx * xc)
        dla = da * a - dmult * a * a / mult
        dsp_ref[...] += _sum0(-LRU_C * ga * dla)
        dpa = (-LRU_C * sp_ref[...] * dla) * ga * (1.0 - ga)
        dpx = (g * mult * xc) * gx * (1.0 - gx)
        dba_ref[...] += _sum0(dpa)
        dbx_ref[...] += _sum0(dpx)
        dpab, dpxb = _mx(dpa), _mx(dpx)
        dwa_ref[...] += _dot_tn(xb, dpab)
        dwx_ref[...] += _dot_tn(xb, dpxb)
        dxc = g * mult * gx + _dot_nt(dpab, wa_ref[...]) + _dot_nt(dpxb, wx_ref[...])
        dcb_ref[...] += _sum0(dxc)
        dcw3[...] += _sum0(dxc * sh[0])
        dcw2[...] += _sum0(dxc * sh[1])
        dcw1[...] += _sum0(dxc * sh[2])
        dcw0[...] += _sum0(dxc * sh[3])
        ext = jnp.concatenate([dxc, dn_s[...]], axis=0)
        dx_ref[...] = (cw_ref[3:4, :] * dxc + cw_ref[2:3, :] * _shift_up(ext, 1, TILE) + cw_ref[1:2, :] * _shift_up(ext, 2, TILE)
                       + cw_ref[0:1, :] * _shift_up(ext, 3, TILE))
        dn_s[...] = dxc[:8]

    r = functools.partial(_rows_spec, n_tiles=nt, reverse=True)
    vec = _sds((1, D_LRU))
    return _call(
        body, "lru_bwd", nt,
        [r(TILE, D_LRU), _halo_spec(D_LRU, TILE, nt, reverse=True), r(TILE, D_LRU), r(TILE, D_LRU), r(8, D_LRU)] + [_whole()] * 7,
        [r(TILE, D_LRU), r(TILE, D_LRU)] + [_whole()] * 10,
        [_sds((n, D_LRU)), _sds((n, D_LRU)), vec, vec, vec, vec, vec, _sds((D_LRU, D_LRU)), _sds((D_LRU, D_LRU)), vec, vec, vec],
        scratch=[pltpu.VMEM((1, D_LRU), F32), pltpu.VMEM((1, D_LRU), F32), pltpu.VMEM((8, D_LRU), F32)],
    )(xr, xr, gate, dyl, carry, prm['conv_w'], prm['conv_b'], prm['wx'], prm['wa'], prm['bx'], prm['ba'], prm['sp'])


def _normed_parts(ya, ys, yl):
    return jnp.concatenate([ya * _rms(ya), ys * _rms(ys), yl * _rms(yl)], axis=1)


def _mixout_fwd(ya, ys, yl, x0, g_mix, w_out, b_out, g1, b1):
    n = x0.shape[0]
    nt = n // TILE_WIDE

    def body(ya_ref, ys_ref, yl_ref, x_ref, gm_ref, w_ref, b_ref, g_ref, be_ref, mix_ref, r_ref, x1_ref):
        mixb = _mx(_normed_parts(ya_ref[...], ys_ref[...], yl_ref[...]) * gm_ref[...])
        mix_ref[...] = mixb
        r1 = ALPHA * x_ref[...] + _dot(mixb, w_ref[...]) + b_ref[...]
        r_ref[...] = r1
        xhat, _ = _ln_stats(r1)
        x1_ref[...] = xhat * g_ref[...] + be_ref[...]

    r = functools.partial(_rows_spec, n_tiles=nt)
    return _call(
        body, "mixout_fwd", nt,
        [r(TILE_WIDE, D_ATTN), r(TILE_WIDE, D_S5), r(TILE_WIDE, D_LRU), r(TILE_WIDE, D)] + [_whole()] * 5,
        [r(TILE_WIDE, D), r(TILE_WIDE, D), r(TILE_WIDE, D)],
        [_sds((n, D), MXU), _sds((n, D)), _sds((n, D))],
    )(ya, ys, yl, x0, g_mix, w_out, b_out, g1, b1)


def _mixout_bwd(dr1, mix, ya, ys, yl, g_mix, w_out):
    n = dr1.shape[0]
    nt = n // TILE_WIDE

    def body(dr_ref, mix_ref, ya_ref, ys_ref, yl_ref, gm_ref, w_ref, dya_ref, dys_ref, dyl_ref, dw_ref, db_ref, dgm_ref):
        @pl.when(pl.program_id(0) == 0)
        def _():
            for ref in (dw_ref, db_ref, dgm_ref):
                ref[...] = jnp.zeros_like(ref)

        dr = dr_ref[...]
        db_ref[...] += _sum0(dr)
        drb = _mx(dr)
        dw_ref[...] += _dot_tn(mix_ref[...], drb)
        dmix = _dot(drb, w_ref[...])
        parts = (ya_ref[...], ys_ref[...], yl_ref[...])
        dgm_ref[...] += _sum0(dmix * _normed_parts(*parts))
        dn = dmix * gm_ref[...]
        lo = 0
        for y, out in zip(parts, (dya_ref, dys_ref, dyl_ref)):
            w = y.shape[1]
            rs = _rms(y)
            nrm = y * rs
            dnp = dn[:, lo:lo + w]
            out[...] = rs * (dnp - nrm * jnp.mean(dnp * nrm, axis=-1, keepdims=True))
            lo += w

    r = functools.partial(_rows_spec, n_tiles=nt)
    return _call(
        body, "mixout_bwd", nt,
        [r(TILE_WIDE, D), r(TILE_WIDE, D), r(TILE_WIDE, D_ATTN), r(TILE_WIDE, D_S5), r(TILE_WIDE, D_LRU), _whole(), _whole()],
        [r(TILE_WIDE, D_ATTN), r(TILE_WIDE, D_S5), r(TILE_WIDE, D_LRU), _whole(), _whole(), _whole()],
        [_sds((n, D_ATTN)), _sds((n, D_S5)), _sds((n, D_LRU)), _sds((D, D)), _sds((1, D)), _sds((1, D))],
    )(dr1, mix, ya, ys, yl, g_mix, w_out)


def _ffn_conv(gp, halo, cw_ref, cb_ref, cs):
    ext = jnp.concatenate([halo, gp], axis=0)
    s1 = _shift_down(ext, 1, TILE)
    s2 = _shift_down(ext, 2, TILE)
    return s1, s2, cb_ref[:, cs] + cw_ref[2:3, cs] * gp + cw_ref[1:2, cs] * s1 + cw_ref[0:1, cs] * s2


def _ffn_fwd(x1, wg, wu, cw, cb, wd, g2, b2):
    n = x1.shape[0]
    nt = n // TILE

    def body(x_ref, wg_ref, wu_ref, cw_ref, cb_ref, wd_ref, g_ref, be_ref, gp_ref, up_ref, r_ref, x2_ref, halo_s, act_s):
        @pl.when(pl.program_id(0) == 0)
        def _():
            halo_s[...] = jnp.zeros_like(halo_s)

        x1 = x_ref[...]
        xb = _mx(x1)
        for c in range(D_FF // FF_CHUNK):
            cs = slice(c * FF_CHUNK, (c + 1) * FF_CHUNK)
            gp = _dot(xb, wg_ref[:, cs])
            up = _dot(xb, wu_ref[:, cs])
            gp_ref[:, cs] = gp
            up_ref[:, cs] = up
            _, _, gc = _ffn_conv(gp, halo_s[:, cs], cw_ref, cb_ref, cs)
            halo_s[:, cs] = gp[TILE - 8:]
            act_s[:, cs] = _mx(gc * _sigmoid(gc) * up)
        r2 = ALPHA * x1 + _dot(act_s[...], wd_ref[...])
        r_ref[...] = r2
        xhat, _ = _ln_stats(r2)
        x2_ref[...] = xhat * g_ref[...] + be_ref[...]

    r = functools.partial(_rows_spec, n_tiles=nt)
    return _call(
        body, "ffn_fwd", nt,
        [r(TILE, D)] + [_whole()] * 7,
        [r(TILE, D_FF), r(TILE, D_FF), r(TILE, D), r(TILE, D)],
        [_sds((n, D_FF)), _sds((n, D_FF)), _sds((n, D)), _sds((n, D))],
        scratch=[pltpu.VMEM((8, D_FF), F32), pltpu.VMEM((TILE, D_FF), MXU)],
    )(x1, wg, wu, cw, cb, wd, g2, b2)


def _ffn_bwd_down(dx2, r2, g2, gp, up, cw, cb, wd_t):
    n = dx2.shape[0]
    nt = n // TILE

    def body(dx_ref, r_ref, g_ref, gp_ref, gh_ref, up_ref, cw_ref, cb_ref, wd_ref,
             dr_ref, dgp_ref, dup_ref, dwd_ref, dcw0, dcw1, dcw2, dcb_ref, dg_ref, db_ref, next_s):
        first_tile = pl.program_id(0) == nt - 1

        @pl.when(pl.program_id(0) == 0)
        def _():
            for ref in (dwd_ref, dcw0, dcw1, dcw2, dcb_ref, dg_ref, db_ref, next_s):
                ref[...] = jnp.zeros_like(ref)

        dx2 = dx_ref[...]
        xhat, rstd = _ln_stats(r_ref[...])
        dg_ref[...] += _sum0(dx2 * xhat)
        db_ref[...] += _sum0(dx2)
        dr2 = _ln_bwd(dx2, g_ref[...], xhat, rstd)
        dr_ref[...] = dr2
        dfb = _mx(dr2)
        for c in range(D_FF // FF_CHUNK):
            cs = slice(c * FF_CHUNK, (c + 1) * FF_CHUNK)
            gp = gp_ref[:, cs]
            up = up_ref[:, cs]
            s1, s2, gc = _ffn_conv(gp, jnp.where(first_tile, 0.0, gh_ref[:, cs]), cw_ref, cb_ref, cs)
            sg = _sigmoid(gc)
            silu = gc * sg
            dact = _dot(dfb, wd_ref[:, cs])
            dwd_ref[cs, :] += _dot_tn(_mx(silu * up), dfb)
            dup_ref[:, cs] = _mx(dact * silu)
            dgc = dact * up * (sg + silu * (1.0 - sg))
            dcb_ref[:, cs] += _sum0(dgc)
            dcw2[:, cs] += _sum0(dgc * gp)
            dcw1[:, cs] += _sum0(dgc * s1)
            dcw0[:, cs] += _sum0(dgc * s2)
            ext = jnp.concatenate([dgc, next_s[:, cs]], axis=0)
            dgp_ref[:, cs] = _mx(cw_ref[2:3, cs] * dgc + cw_ref[1:2, cs] * _shift_up(ext, 1, TILE)
                                 + cw_ref[0:1, cs] * _shift_up(ext, 2, TILE))
            next_s[:, cs] = dgc[:8]

    r = functools.partial(_rows_spec, n_tiles=nt, reverse=True)
    vff = _sds((1, D_FF))
    return _call(
        body, "ffn_bwd_down", nt,
        [r(TILE, D), r(TILE, D), _whole(), r(TILE, D_FF), _halo_spec(D_FF, TILE, nt, reverse=True), r(TILE, D_FF), _whole(), _whole(),
         _whole()],
        [r(TILE, D), r(TILE, D_FF), r(TILE, D_FF)] + [_whole()] * 7,
        [_sds((n, D)), _sds((n, D_FF), MXU), _sds((n, D_FF), MXU), _sds((D_FF, D)), vff, vff, vff, vff, _sds((1, D)), _sds((1, D))],
        scratch=[pltpu.VMEM((8, D_FF), F32)],
    )(dx2, r2, g2, gp, gp, up, cw, cb, wd_t)


def _ffn_bwd_dx(dr2, dgp, dup, r1, g1, wg_t, wu_t):
    n = dr2.shape[0]
    rows = TILE_BIG
    nt = n // rows

    def body(dr2_ref, dgp_ref, dup_ref, r_ref, g_ref, wg_ref, wu_ref, dr1_ref, dg_ref, db_ref):
        @pl.when(pl.program_id(0) == 0)
        def _():
            for ref in (dg_ref, db_ref):
                ref[...] = jnp.zeros_like(ref)

        dx1 = ALPHA * dr2_ref[...] + _dot(dgp_ref[...], wg_ref[...]) + _dot(dup_ref[...], wu_ref[...])
        xhat, rstd = _ln_stats(r_ref[...])
        dg_ref[...] += _sum0(dx1 * xhat)
        db_ref[...] += _sum0(dx1)
        dr1_ref[...] = _ln_bwd(dx1, g_ref[...], xhat, rstd)

    r = functools.partial(_rows_spec, n_tiles=nt)
    return _call(
        body, "ffn_bwd_dx", nt,
        [r(rows, D), r(rows, D_FF), r(rows, D_FF), r(rows, D), _whole(), _whole(), _whole()],
        [r(rows, D), _whole(), _whole()],
        [_sds((n, D)), _sds((1, D)), _sds((1, D))],
    )(dr2, dgp, dup, r1, g1, wg_t, wu_t)


def _ffn_bwd_dw(x1, dgp, dup):
    n = x1.shape[0]
    rows = TILE_BIG
    nt = n // rows

    def body(x_ref, dgp_ref, dup_ref, dwg_ref, dwu_ref):
        @pl.when(pl.program_id(0) == 0)
        def _():
            for ref in (dwg_ref, dwu_ref):
                ref[...] = jnp.zeros_like(ref)

        xb = _mx(x_ref[...])
        for c in range(D_FF // FF_CHUNK):
            cs = slice(c * FF_CHUNK, (c + 1) * FF_CHUNK)
            dwg_ref[cs, :] += _dot_tn(dgp_ref[:, cs], xb)
            dwu_ref[cs, :] += _dot_tn(dup_ref[:, cs], xb)

    r = functools.partial(_rows_spec, n_tiles=nt)
    return _call(
        body, "ffn_bwd_dw", nt,
        [r(rows, D), r(rows, D_FF), r(rows, D_FF)], [_whole(), _whole()], [_sds((D_FF, D)), _sds((D_FF, D))],
    )(x1, dgp, dup)


def _loss_head(y, target):
    n = y.shape[0]
    nt = n // TILE_WIDE

    def body(y_ref, t_ref, loss_ref, dy_ref):
        @pl.when(pl.program_id(0) == 0)
        def _():
            loss_ref[...] = jnp.zeros_like(loss_ref)

        e = y_ref[...] - t_ref[...]
        dy_ref[...] = e * (1.0 / D)
        loss_ref[...] += _sum0(jnp.sum(e * e, axis=1, keepdims=True)) * (0.5 / D)

    r = functools.partial(_rows_spec, n_tiles=nt)
    return _call(body, "loss_head", nt, [r(TILE_WIDE, D), r(TILE_WIDE, D)], [_whole(), r(TILE_WIDE, D)],
                 [_sds((1, 1)), _sds((n, D))])(y, target)


def _place():
    x, y, c = lax.axis_index("x"), lax.axis_index("y"), lax.axis_index("c")
    return x, y, c, 4 * x + 2 * y + c


def _peer(x, y, c, k):
    px, py, pc = x ^ ((k >> 2) & 1), y ^ ((k >> 1) & 1), c ^ (k & 1)
    return (px, py, pc), 4 * px + 2 * py + pc


def _all_gather(blocks, small):
    srcs = list(blocks) + [small]
    n = len(srcs)
    out_shapes = [_sds((a.shape[0], N_DEV * a.shape[1], LANES), a.dtype) for a in blocks] + [_sds((N_DEV,) + small.shape, small.dtype)]

    def body(*refs):
        src_refs, out_refs = refs[:n], refs[n:2 * n]
        send_sems, recv_sems, local_sems = refs[2 * n:]
        x, y, c, me = _place()

        def landing(a, slot):
            if a == n - 1:
                return out_refs[a].at[slot]
            r = src_refs[a].shape[1]
            return out_refs[a].at[:, pl.ds(slot * r, r), :]

        def remote(a, k, slot):
            peer, _ = _peer(x, y, c, k)
            return pltpu.make_async_remote_copy(
                src_ref=src_refs[a], dst_ref=landing(a, slot), send_sem=send_sems.at[a * N_DEV + k],
                recv_sem=recv_sems.at[a * N_DEV + k], device_id=peer, device_id_type=pl.DeviceIdType.MESH)

        mine = [pltpu.make_async_copy(src_refs[a], landing(a, me), local_sems.at[a]) for a in range(n)]
        sends = [remote(a, k, me) for a in range(n) for k in range(1, N_DEV)]
        for cp in mine + sends:
            cp.start()
        for a in range(n):
            for k in range(1, N_DEV):
                remote(a, k, _peer(x, y, c, k)[1]).wait_recv()
        for cp in sends:
            cp.wait_send()
        for cp in mine:
            cp.wait()

    any_space = pl.BlockSpec(memory_space=pl.ANY)
    return pl.pallas_call(
        body, name="gather_weights", out_shape=out_shapes, in_specs=[any_space] * n, out_specs=[any_space] * n,
        scratch_shapes=[pltpu.SemaphoreType.DMA((n * N_DEV,)), pltpu.SemaphoreType.DMA((n * N_DEV,)), pltpu.SemaphoreType.DMA((n,))],
    )(*srcs)


_HBM = pl.BlockSpec(memory_space=pltpu.HBM)
_SEM = pl.BlockSpec(memory_space=pltpu.SEMAPHORE)
_EFFECT = pltpu.SideEffectType.DATAFLOW_SIDE_EFFECTING


def _in_hbm(a):
    return pltpu.with_memory_space_constraint(a, pltpu.HBM)


def _scatter_start(name, srcs):
    ns = len(srcs)
    rows_a = [a.shape[0] // N_DEV for a in srcs]
    offs = [sum(rows_a[:a]) for a in range(ns)]
    total = sum(rows_a)

    def body(*refs):
        src_refs, land_ref, send_sems, recv_sems, token = refs[:ns], refs[ns], refs[ns + 1], refs[ns + 2], refs[-1]
        x, y, c, me = _place()
        for k in range(1, N_DEV):
            peer, peer_slot = _peer(x, y, c, k)
            for a in range(ns):
                pltpu.make_async_remote_copy(
                    src_ref=src_refs[a].at[pl.ds(peer_slot * rows_a[a], rows_a[a]), :],
                    dst_ref=land_ref.at[me, pl.ds(offs[a], rows_a[a]), :], send_sem=send_sems.at[k], recv_sem=recv_sems.at[k],
                    device_id=peer, device_id_type=pl.DeviceIdType.MESH).start()
        token[...] = jnp.zeros_like(token)

    landing = lax.empty((N_DEV, total, LANES), F32)
    out = pl.pallas_call(
        body, name=name,
        out_shape=(pltpu.SemaphoreType.DMA((N_DEV,)), pltpu.SemaphoreType.DMA((N_DEV,)), *[pltpu.HBM(a.shape, a.dtype) for a in srcs],
                   pltpu.HBM(landing.shape, F32), _sds((8, 128))),
        in_specs=[_HBM] * (ns + 1), out_specs=(_SEM, _SEM, *[_HBM] * (ns + 1), pl.BlockSpec(memory_space=pltpu.VMEM)),
        input_output_aliases={a: 2 + a for a in range(ns + 1)},
        compiler_params=pltpu.CompilerParams(has_side_effects=_EFFECT),
    )(*[_in_hbm(a) for a in srcs], _in_hbm(landing))
    return out[0], out[1], out[2:2 + ns], out[2 + ns], out[-1]


def _scatter_wait(name, send_sems, recv_sems, srcs, landing, after):
    ns = len(srcs)

    def body(*refs):
        land_ref, send_ref, recv_ref = refs[ns], refs[ns + 1], refs[ns + 2]
        x, y, c, me = _place()
        for k in range(1, N_DEV):
            peer, peer_slot = _peer(x, y, c, k)
            slot = pltpu.make_async_remote_copy(
                src_ref=land_ref.at[me], dst_ref=land_ref.at[peer_slot], send_sem=send_ref.at[k], recv_sem=recv_ref.at[k],
                device_id=peer, device_id_type=pl.DeviceIdType.MESH)
            slot.wait_send()
            slot.wait_recv()

    out = pl.pallas_call(
        body, name=name, out_shape=(*[pltpu.HBM(a.shape, a.dtype) for a in srcs], pltpu.HBM(landing.shape, landing.dtype)),
        in_specs=[_HBM] * (ns + 1) + [_SEM, _SEM, pl.BlockSpec(memory_space=pl.ANY)], out_specs=[_HBM] * (ns + 1),
        input_output_aliases={a: a for a in range(ns + 1)},
        compiler_params=pltpu.CompilerParams(has_side_effects=_EFFECT),
    )(*srcs, landing, send_sems, recv_sems, after)
    return out[:ns], out[ns]


def _gather_start(name, blocks):
    n = len(blocks)

    def body(*refs):
        src_refs, land_refs, send_sems, recv_sems, token = refs[:n], refs[n:2 * n], refs[2 * n], refs[2 * n + 1], refs[-1]
        x, y, c, me = _place()
        for a in range(n):
            r = src_refs[a].shape[1]
            for k in range(1, N_DEV):
                pltpu.make_async_remote_copy(
                    src_ref=src_refs[a], dst_ref=land_refs[a].at[:, pl.ds(me * r, r), :], send_sem=send_sems.at[a * N_DEV + k],
                    recv_sem=recv_sems.at[a * N_DEV + k], device_id=_peer(x, y, c, k)[0], device_id_type=pl.DeviceIdType.MESH).start()
        token[...] = jnp.zeros_like(token)

    wholes = [lax.empty((a.shape[0], N_DEV * a.shape[1], LANES), a.dtype) for a in blocks]
    out = pl.pallas_call(
        body, name=name,
        out_shape=(pltpu.SemaphoreType.DMA((n * N_DEV,)), pltpu.SemaphoreType.DMA((n * N_DEV,)),
                   *[pltpu.HBM(a.shape, a.dtype) for a in blocks + wholes], _sds((8, 128))),
        in_specs=[_HBM] * (2 * n), out_specs=(_SEM, _SEM, *[_HBM] * (2 * n), pl.BlockSpec(memory_space=pltpu.VMEM)),
        input_output_aliases={a: 2 + a for a in range(2 * n)},
        compiler_params=pltpu.CompilerParams(has_side_effects=_EFFECT),
    )(*[_in_hbm(a) for a in blocks + wholes])
    return out[0], out[1], out[2:2 + n], out[2 + n:2 + 2 * n], out[-1]


def _gather_wait(name, send_sems, recv_sems, blocks, wholes, after):
    n = len(blocks)

    def body(*refs):
        src_refs, land_refs, send_ref, recv_ref = refs[:n], refs[n:2 * n], refs[2 * n], refs[2 * n + 1]
        x, y, c, me = _place()
        for a in range(n):
            r = src_refs[a].shape[1]
            for k in range(1, N_DEV):
                peer, peer_slot = _peer(x, y, c, k)
                cp = pltpu.make_async_remote_copy(
                    src_ref=src_refs[a], dst_ref=land_refs[a].at[:, pl.ds(peer_slot * r, r), :], send_sem=send_ref.at[a * N_DEV + k],
                    recv_sem=recv_ref.at[a * N_DEV + k], device_id=peer, device_id_type=pl.DeviceIdType.MESH)
                cp.wait_send()
                cp.wait_recv()

    return pl.pallas_call(
        body, name=name, out_shape=tuple(pltpu.HBM(a.shape, a.dtype) for a in list(blocks) + list(wholes)),
        in_specs=[_HBM] * (2 * n) + [_SEM, _SEM, pl.BlockSpec(memory_space=pl.ANY)], out_specs=[_HBM] * (2 * n),
        input_output_aliases={a: a for a in range(2 * n)},
        compiler_params=pltpu.CompilerParams(has_side_effects=_EFFECT),
    )(*blocks, *wholes, send_sems, recv_sems, after)


def _reduce_adamw(parts, w, m, v, tile_rows=PACK_TILE):
    rows = w.shape[0]
    nt = rows // tile_rows
    c1 = 1.0 - ADAM_B1 ** ADAM_STEP
    c2 = 1.0 - ADAM_B2 ** ADAM_STEP

    def body(p_ref, w_ref, m_ref, v_ref, g_out, d_out, m_out, v_out):
        g = p_ref[0]
        for s in range(1, N_DEV):
            g = g + p_ref[s]
        m_new = ADAM_B1 * m_ref[...] + (1.0 - ADAM_B1) * g
        v_new = ADAM_B2 * v_ref[...] + (1.0 - ADAM_B2) * (g * g)
        g_out[...] = g
        m_out[...] = m_new
        v_out[...] = v_new
        d_out[...] = -ADAM_LR * ((m_new / c1) / (jnp.sqrt(v_new / c2) + ADAM_EPS) + ADAM_WD * w_ref[...])

    r = _rows_spec(tile_rows, LANES, nt)
    out = _sds((rows, LANES))
    return _call(
        body, "reduce_adamw", nt,
        [pl.BlockSpec((N_DEV, tile_rows, LANES), lambda i: (0, i, 0)), r, r, r], [r, r, r, r], [out, out, out, out],
    )(parts, w, m, v)


def _pack_rows(a, lead=0):
    head = a.shape[:lead]
    flat = a.reshape(head + (-1,))
    size = flat.shape[-1]
    rows = -(-size // (16 * LANES)) * 16
    flat = jnp.pad(flat, [(0, 0)] * lead + [(0, rows * LANES - size)])
    return flat.reshape(head + (rows, LANES))


def _packed_rows(shape):
    return -(-math.prod(shape) // (16 * LANES)) * 16


def _to_blocks(full, axis):
    l, a, b = full.shape
    if axis == 2:
        return full.reshape(l, a, N_DEV, b // N_DEV).transpose(2, 0, 1, 3)
    return full.reshape(l, N_DEV, a // N_DEV, b).transpose(1, 0, 2, 3)


def _from_blocks(blocks, axis):
    _, l, a, b = blocks.shape
    if axis == 2:
        return blocks.transpose(1, 2, 0, 3).reshape(l, a, N_DEV * b)
    return blocks.transpose(1, 0, 2, 3).reshape(l, N_DEV * a, b)


def _row_form(shard, transposed):
    return shard.transpose(0, 2, 1) if transposed else shard


def _me():
    return 4 * lax.axis_index("x") + 2 * lax.axis_index("y") + lax.axis_index("c")


def _both_forms(names, wholes):
    out = {}
    for name, w in zip(names, wholes):
        t = dict(BIG)[name]
        out[name + '_t' if t else name] = w
        out[name if t else name + '_t'] = w.transpose(0, 2, 1)
    return out


def _gather_weights(local):
    segs, meta = [], []
    for name in SMALL_SHARDED:
        blk = local[name]
        if name in GATHER_F32:
            bits = lax.bitcast_convert_type(blk, MXU)
        else:
            bits = _mx(blk)
        seg = _pack_rows(bits)
        meta.append((name, bits.shape, seg.shape[0]))
        segs.append(seg)
    blocks = {name: _mx(_row_form(local[name], t)) for name, t in BIG}
    mix, ffn = PARTS['mix'], PARTS['ffn']
    *first, gathered = _all_gather([blocks[n][:1] for n in mix], jnp.concatenate(segs, axis=0))
    flights = {'ffn0': _gather_start("gather_ffn0_start", [blocks[n][:1] for n in ffn]),
               'later': _gather_start("gather_later_start", [blocks[n][1:] for n in mix + ffn])}
    out, lo = {}, 0
    for name, bits_shape, rows in meta:
        seg = gathered[:, lo:lo + rows].reshape(N_DEV, -1)[:, :math.prod(bits_shape)].reshape((N_DEV,) + bits_shape)
        if name in GATHER_F32:
            seg = lax.bitcast_convert_type(seg, F32)
        out[name] = _from_blocks(seg, SHARD_AXIS[name])
        lo += rows
    ready = {(0, 'mix'): {k: v[0] for k, v in _both_forms(mix, first).items()}}

    def landed(flight, names, after):
        send_sems, recv_sems, mine, wholes, _ = flights[flight]
        done = _gather_wait(f"gather_{flight}_wait", send_sems, recv_sems, mine, wholes, after)
        own = [lax.dynamic_update_slice(w, b, (0, _me() * b.shape[1], 0)) for b, w in zip(done[:len(names)], done[len(names):])]
        return _both_forms(names, own)

    def big_weights(l, part, after):
        if (l, part) not in ready and l == 0:
            ready[(0, 'ffn')] = {k: v[0] for k, v in landed('ffn0', ffn, after).items()}
        elif (l, part) not in ready:
            forms = landed('later', mix + ffn, after)
            for j in range(1, DEPTH):
                for p, names in PARTS.items():
                    ready[(j, p)] = {k: forms[k][j - 1] for n in names for k in (n, n + '_t')}
        return ready[(l, part)]

    return out, big_weights, flights['ffn0'][-1][0, 0] + flights['later'][-1][0, 0]


def _s5_discretize(a_re, a_im, log_dt, b_re, b_im):
    lam_re = jnp.minimum(a_re, -1e-4)
    lam_im = a_im
    dt = jnp.exp(log_dt)[:, None]
    decay = jnp.exp(dt * lam_re)
    ang = dt * lam_im
    abar_re = decay * jnp.cos(ang)
    abar_im = decay * jnp.sin(ang)
    den = jnp.square(lam_re) + jnp.square(lam_im)
    nr = abar_re - 1.0
    ni = abar_im
    coef_re = (nr * lam_re + ni * lam_im) / den
    coef_im = (ni * lam_re - nr * lam_im) / den
    bbar_re = coef_re[..., None] * b_re - coef_im[..., None] * b_im
    bbar_im = coef_re[..., None] * b_im + coef_im[..., None] * b_re
    return abar_re, abar_im, bbar_re, bbar_im


def _complex_powers(ar, ai, count):
    def combine(e1, e2):
        return e2[0] * e1[0] - e2[1] * e1[1], e2[0] * e1[1] + e2[1] * e1[0]

    shape = (count,) + ar.shape
    return lax.associative_scan(combine, (jnp.broadcast_to(ar, shape), jnp.broadcast_to(ai, shape)), axis=0)


_EYE16 = functools.partial(jnp.eye, 16, dtype=F32)


def _s5_params(p, l):
    disc, disc_vjp = jax.vjp(_s5_discretize, p['s5_a_re'][l], p['s5_a_im'][l], p['s5_log_dt'][l], p['s5_b_re'][l], p['s5_b_im'][l])
    abar_re, abar_im, bbar_re, bbar_im = disc
    ar, ai = abar_re.reshape(N_STATE), abar_im.reshape(N_STATE)
    ap_re, ap_im = _complex_powers(ar, ai, TILE)
    one, zero = jnp.ones((1, N_STATE), F32), jnp.zeros((1, N_STATE), F32)
    seg_re = jnp.concatenate([one, ap_re[SEG - 1:TILE - SEG:SEG]], axis=0)
    seg_im = jnp.concatenate([zero, ap_im[SEG - 1:TILE - SEG:SEG]], axis=0)
    doubling = [SEG - 1, 2 * SEG - 1, 4 * SEG - 1]

    src = (jnp.arange(TILE) % N_SEG) * SEG + jnp.arange(TILE) // N_SEG
    perm = (src[:, None] == jnp.arange(TILE)[None, :]).astype(MXU)
    prm = {
        'perm': perm, 'perm_t': perm.T,
        'pw_re': ap_re[:SEG], 'pw_im': ap_im[:SEG],
        'dbl_re': jnp.concatenate([jnp.stack([ap_re[k] for k in doubling]), jnp.zeros((5, N_STATE), F32)], axis=0),
        'dbl_im': jnp.concatenate([jnp.stack([ap_im[k] for k in doubling]), jnp.zeros((5, N_STATE), F32)], axis=0),
        'seg_re': seg_re, 'seg_im': seg_im, 'segr_re': seg_re[::-1], 'segr_im': seg_im[::-1],
        'b_re': _mx(jnp.einsum('gpc,gh->gchp', bbar_re, _EYE16()).reshape(D_S5, N_STATE)),
        'b_im': _mx(jnp.einsum('gpc,gh->gchp', bbar_im, _EYE16()).reshape(D_S5, N_STATE)),
        'c_re': _mx(jnp.einsum('gcp,gh->gphc', p['s5_c_re'][l], _EYE16()).reshape(N_STATE, D_S5)),
        'c_im': _mx(jnp.einsum('gcp,gh->gphc', p['s5_c_im'][l], _EYE16()).reshape(N_STATE, D_S5)),
        'd': p['s5_d'][l][None, :], 'glu_w': p['s5_glu_w'][l], 'glu_b': p['s5_glu_b'][l][None, :],
    }
    return prm, disc_vjp


def _lru_params(p, l):
    eye4 = jnp.eye(4, dtype=F32)
    return {
        'conv_w': p['lru_conv_w'][l], 'conv_b': p['lru_conv_b'][l][None, :],
        'wx': _mx(jnp.einsum('hij,hk->hikj', p['lru_wx'][l], eye4).reshape(D_LRU, D_LRU)),
        'wa': _mx(jnp.einsum('hij,hk->hikj', p['lru_wa'][l], eye4).reshape(D_LRU, D_LRU)),
        'bx': p['lru_bx'][l][None, :], 'ba': p['lru_ba'][l][None, :],
        'sp': jax.nn.softplus(-p['lru_a_param'][l])[None, :],
    }


def _rope_tables(n):
    inv_freq = ROPE_THETA ** (-jnp.arange(0, 64, 2, dtype=F32) / 64)
    ang = jnp.arange(n, dtype=F32)[:, None] * inv_freq[None, :]
    cos, sin = jnp.cos(ang), jnp.sin(ang)
    return jnp.concatenate([cos, cos, cos, cos], axis=1), jnp.concatenate([-sin, sin, -sin, sin], axis=1)


def _sink_cols(sinks):
    nb = TILE_Q // ATTN_BLOCK
    per_unit = sinks.reshape(4, 2).T
    return jnp.broadcast_to(per_unit[:, None, :, None, None], (2, nb, 4, ATTN_BLOCK, 128)).reshape(2, nb * 4 * ATTN_BLOCK, 128)


def _local_step(x, target, p, big_weights=None, emit_grads=None):
    if big_weights is None:
        big_weights = lambda l, part, after: {k: p[k][l] for name in PARTS[part] for k in (name, name + '_t')}
    if emit_grads is None:
        emit_grads = lambda l, part, grads: 0.0
    n = x.shape[0]
    cos_t, sin_t = _rope_tables(n)
    row = lambda a: a[None, :]
    saved = []
    h = x
    for l in range(DEPTH):
        s = {'x0': h}
        bw = s['bw'] = dict(big_weights(l, 'mix', h))
        s['q'], k, v, s['u'], s['xr'], s['gate'] = _inproj_fwd(h, bw['w_in'], row(p['b_in'][l]), cos_t, sin_t)
        no_keys = jnp.zeros((ATTN_BLOCK, D_KV), MXU)
        s['k'], s['v'] = jnp.concatenate([no_keys, k], axis=0), jnp.concatenate([no_keys, v], axis=0)
        s['sinks'] = _sink_cols(p['attn_sinks'][l])
        s['ya'] = _attn_fwd(s['q'], s['k'], s['v'], s['sinks'])
        s['s5'], s['s5_vjp'] = _s5_params(p, l)
        s['lru'] = _lru_params(p, l)
        s['ys'], s['s5_cr'], s['s5_ci'] = _s5_fwd(s['u'], s['s5'])
        s['yl'], s['lru_c'] = _lru_fwd(s['xr'], s['gate'], s['lru'])
        s['mix'], s['r1'], s['x1'] = _mixout_fwd(s['ya'], s['ys'], s['yl'], h, row(p['mix_norm_g'][l]), bw['w_out'],
                                                 row(p['b_out'][l]), row(p['ln1_g'][l]), row(p['ln1_b'][l]))
        bw.update(big_weights(l, 'ffn', s['x1']))
        s['gp'], s['up'], s['r2'], h = _ffn_fwd(s['x1'], bw['ffn_w_gate'], bw['ffn_w_up'], p['ffn_conv_w'][l],
                                                row(p['ffn_conv_b'][l]), bw['ffn_w_down'], row(p['ln2_g'][l]), row(p['ln2_b'][l]))
        saved.append(s)
    loss, dh = _loss_head(h, target)
    placed = 0.0

    grads = {name: [None] * DEPTH for name in WEIGHTS}
    for l in reversed(range(DEPTH)):
        s = saved[l]
        g = {}
        (dr2, dgp, dup, g['ffn_w_down'], cw0, cw1, cw2, dcb, dg2, db2) = _ffn_bwd_down(
            dh, s['r2'], row(p['ln2_g'][l]) + placed, s['gp'], s['up'], p['ffn_conv_w'][l], row(p['ffn_conv_b'][l]),
            s['bw']['ffn_w_down_t'])
        g['ffn_conv_w'] = jnp.concatenate([cw0, cw1, cw2], axis=0)
        g['ffn_conv_b'], g['ln2_g'], g['ln2_b'] = dcb[0], dg2[0], db2[0]
        dr1, dg1, db1 = _ffn_bwd_dx(dr2, dgp, dup, s['r1'], row(p['ln1_g'][l]), s['bw']['ffn_w_gate_t'], s['bw']['ffn_w_up_t'])
        g['ffn_w_gate'], g['ffn_w_up'] = _ffn_bwd_dw(s['x1'], dgp, dup)
        g['ln1_g'], g['ln1_b'] = dg1[0], db1[0]
        placed = emit_grads(l, 'ffn', [g[name] for name in PARTS['ffn']])
        dya, dys, dyl, g['w_out'], dbo, dgm = _mixout_bwd(dr1, s['mix'], s['ya'], s['ys'], s['yl'],
                                                         row(p['mix_norm_g'][l]) + placed, s['bw']['w_out_t'])
        g['b_out'], g['mix_norm_g'] = dbo[0], dgm[0]

        du, dbr, dbi, dcr, dci, dar, dai, dd, g['s5_glu_w'], dgb = _s5_bwd(s['u'], dys, s['s5_cr'], s['s5_ci'], s['s5'])
        dxr, dgate, lw0, lw1, lw2, lw3, lcb, dwx, dwa, dbx, dba, dsp = _lru_bwd(s['xr'], s['gate'], dyl, s['lru_c'], s['lru'])
        g['lru_conv_w'] = jnp.concatenate([lw0, lw1, lw2, lw3], axis=0)
        g['lru_conv_b'], g['lru_bx'], g['lru_ba'] = lcb[0], dbx[0], dba[0]
        g['lru_wx'] = jnp.einsum('hihj->hij', dwx.reshape(4, 64, 4, 64))
        g['lru_wa'] = jnp.einsum('hihj->hij', dwa.reshape(4, 64, 4, 64))
        g['lru_a_param'] = -dsp[0] * jax.nn.sigmoid(-p['lru_a_param'][l])

        g['s5_c_re'] = jnp.einsum('gpgc->gcp', dcr.reshape(16, 64, 16, 16))
        g['s5_c_im'] = jnp.einsum('gpgc->gcp', dci.reshape(16, 64, 16, 16))
        g['s5_d'], g['s5_glu_b'] = dd[0], dgb[0]
        g['s5_a_re'], g['s5_a_im'], g['s5_log_dt'], g['s5_b_re'], g['s5_b_im'] = s['s5_vjp']((
            dar.reshape(16, 64), dai.reshape(16, 64), jnp.einsum('gcgp->gpc', dbr.reshape(16, 16, 16, 64)),
            jnp.einsum('gcgp->gpc', dbi.reshape(16, 16, 16, 64))))

        dq, dk, dv, dsink = _attn_bwd(s['q'], s['k'], s['v'], s['sinks'], s['ya'], dya)
        g['attn_sinks'] = dsink[:, 0]
        dh, g['w_in'], dbin = _inproj_bwd(dq, dk[ATTN_BLOCK:], dv[ATTN_BLOCK:], du, dxr, dgate, cos_t, sin_t, s['x0'], dr1,
                                          s['bw']['w_in_t'])
        g['b_in'] = dbin[0]
        placed = emit_grads(l, 'mix', [g[name] for name in PARTS['mix']])
        for name in WEIGHTS:
            grads[name][l] = g[name]
    big = dict(BIG)
    return loss, dh, {name: grads[name] if name in big else jnp.stack(grads[name]) for name in WEIGHTS}


def kernel(x, w_in, b_in, attn_sinks, s5_a_re, s5_a_im, s5_b_re, s5_b_im, s5_c_re, s5_c_im, s5_d, s5_log_dt, s5_glu_w, s5_glu_b, lru_conv_w, lru_conv_b, lru_wx, lru_bx, lru_wa, lru_ba, lru_a_param, mix_norm_g, w_out, b_out, ln1_g, ln1_b, ffn_w_gate, ffn_w_up, ffn_conv_w, ffn_conv_b, ffn_w_down, ln2_g, ln2_b, loss_target, m_w_in, m_b_in, m_attn_sinks, m_s5_a_re, m_s5_a_im, m_s5_b_re, m_s5_b_im, m_s5_c_re, m_s5_c_im, m_s5_d, m_s5_log_dt, m_s5_glu_w, m_s5_glu_b, m_lru_conv_w, m_lru_conv_b, m_lru_wx, m_lru_bx, m_lru_wa, m_lru_ba, m_lru_a_param, m_mix_norm_g, m_w_out, m_b_out, m_ln1_g, m_ln1_b, m_ffn_w_gate, m_ffn_w_up, m_ffn_conv_w, m_ffn_conv_b, m_ffn_w_down, m_ln2_g, m_ln2_b, v_w_in, v_b_in, v_attn_sinks, v_s5_a_re, v_s5_a_im, v_s5_b_re, v_s5_b_im, v_s5_c_re, v_s5_c_im, v_s5_d, v_s5_log_dt, v_s5_glu_w, v_s5_glu_b, v_lru_conv_w, v_lru_conv_b, v_lru_wx, v_lru_bx, v_lru_wa, v_lru_ba, v_lru_a_param, v_mix_norm_g, v_w_out, v_b_out, v_ln1_g, v_ln1_b, v_ffn_w_gate, v_ffn_w_up, v_ffn_conv_w, v_ffn_conv_b, v_ffn_w_down, v_ln2_g, v_ln2_b):
    given = dict(locals())
    whole = {name: given[name] for name in WEIGHTS if name not in dict(BIG)}
    small_whole, big_weights, placed = _gather_weights({name: given[name] for name in SHARDED})
    whole.update(small_whole)
    whole['b_in'] = whole['b_in'] + placed
    in_flight = {}

    def emit_grads(l, part, grads):
        in_flight[(l, part)] = _scatter_start(f"grads_start_{part}{l}", grads)
        return in_flight[(l, part)][-1][0, 0]

    loss, grad_x, grads = _local_step(x[0], loss_target[0], whole, big_weights, emit_grads)
    total = lax.psum(loss[0, 0], ("x", "y", "c"))
    return (total, grad_x[None], *_update(given, grads, in_flight, grad_x))


def _update(given, grads, in_flight, after):
    local_w = {name: given[name] for name in WEIGHTS}
    me = _me()
    outs = {}

    rest = SMALL_SHARDED + REPLICATED
    pad = -sum(_packed_rows(local_w[name].shape) for name in rest) % PACK_TILE
    small = jnp.concatenate(
        [_pack_rows(_to_blocks(grads[name], SHARD_AXIS[name]), lead=1) for name in SMALL_SHARDED]
        + [jnp.broadcast_to(_pack_rows(grads[name]), (N_DEV, _packed_rows(grads[name].shape), LANES)) for name in REPLICATED]
        + [jnp.zeros((N_DEV, pad, LANES), F32)], axis=1)
    small_rows = small.shape[1]
    small_flight = _scatter_start("grads_start_small", [small.reshape(N_DEV * small_rows, LANES)])

    def summed(name, flight, rows_of, packed, after):
        send_sems, recv_sems, srcs, landing, _ = flight
        srcs, landing = _scatter_wait(name, send_sems, recv_sems, srcs, landing, after)
        own = jnp.concatenate([lax.dynamic_slice_in_dim(g, me * r, r, axis=0) for g, r in zip(srcs, rows_of)], axis=0)
        parts = lax.dynamic_update_slice(landing, own[None], (me, 0, 0))
        return _reduce_adamw(parts, *packed, tile_rows=own.shape[0] // 4)

    forms = {part: [(name, dict(BIG)[name], _row_form(local_w[name], dict(BIG)[name]).shape[1]) for name in names]
             for part, names in PARTS.items()}
    for (l, part), flight in in_flight.items():
        packed = [jnp.concatenate([_row_form(given[prefix + name], t)[l] for name, t, _ in forms[part]], axis=0)
                  for prefix in ('', 'm_', 'v_')]
        outs[(l, part)] = summed(f"grads_wait_{part}{l}", flight, [r for _, _, r in forms[part]], packed, after)
        after = outs[(l, part)][0]

    def packed_rest(prefix):
        return jnp.concatenate([_pack_rows(given[prefix + name]) for name in rest] + [jnp.zeros((pad, LANES), F32)], axis=0)

    rest_outs = summed("grads_wait_small", small_flight, [small_rows], [packed_rest(p) for p in ('', 'm_', 'v_')], after)

    def unpack(i):
        res = {}
        for part in PARTS:
            lo = 0
            for name, t, r in forms[part]:
                res[name] = _row_form(jnp.stack([outs[(l, part)][i][lo:lo + r] for l in range(DEPTH)]), t)
                lo += r
        lo = 0
        for name in rest:
            shape = local_w[name].shape
            res[name] = rest_outs[i][lo:lo + _packed_rows(shape)].reshape(-1)[:math.prod(shape)].reshape(shape)
            lo += _packed_rows(shape)
        return [res[name] for name in WEIGHTS]

    return (*unpack(0), *unpack(1), *unpack(2), *unpack(3))
```

```python
import functools
import math

import jax
import jax.numpy as jnp
from jax import lax
from jax.experimental import pallas as pl
from jax.experimental.pallas import tpu as pltpu

F32 = jnp.float32
MXU = jnp.bfloat16

N_DEV = 8
DEPTH = 4
D = 1024
D_ATTN, D_KV, D_S5, D_LRU = 512, 128, 256, 256
D_IN = 1536
D_FF = 2816
FF_CHUNK = 256
N_STATE = 1024
LANES = 1024
ALPHA = (2 * DEPTH) ** 0.25
LN_EPS = 1e-5
RMS_EPS = 1e-6
LRU_C = 8.0
ROPE_THETA = 10000.0
ADAM_LR, ADAM_B1, ADAM_B2, ADAM_EPS, ADAM_WD, ADAM_STEP = 0.001, 0.9, 0.999, 1e-08, 0.01, 10

TILE = 256
S5_PAIR = 2
N_SEG = 8
SEG = TILE // N_SEG
TILE_Q = 512
TILE_BIG = 512
TILE_WIDE = 512
ATTN_BLOCK = 128
PACK_TILE = 256
VMEM_MB = 56

WEIGHTS = ['w_in', 'b_in', 'attn_sinks', 's5_a_re', 's5_a_im', 's5_b_re', 's5_b_im', 's5_c_re', 's5_c_im', 's5_d', 's5_log_dt',
           's5_glu_w', 's5_glu_b', 'lru_conv_w', 'lru_conv_b', 'lru_wx', 'lru_bx', 'lru_wa', 'lru_ba', 'lru_a_param', 'mix_norm_g',
           'w_out', 'b_out', 'ln1_g', 'ln1_b', 'ffn_w_gate', 'ffn_w_up', 'ffn_conv_w', 'ffn_conv_b', 'ffn_w_down', 'ln2_g', 'ln2_b']
SHARD_AXIS = {'w_in': 2, 's5_glu_w': 1, 'lru_conv_w': 2, 'w_out': 1, 'ffn_w_gate': 2, 'ffn_w_up': 2, 'ffn_conv_w': 2,
              'ffn_w_down': 1}
SHARDED = [n for n in WEIGHTS if n in SHARD_AXIS]
REPLICATED = [n for n in WEIGHTS if n not in SHARD_AXIS]
BIG = [(n, SHARD_AXIS[n] == 2) for n in ('w_in', 'w_out', 'ffn_w_gate', 'ffn_w_up', 'ffn_w_down')]
SMALL_SHARDED = [n for n in SHARDED if n not in dict(BIG)]
PARTS = {'mix': ['w_in', 'w_out'], 'ffn': ['ffn_w_gate', 'ffn_w_up', 'ffn_w_down']}
GATHER_F32 = ('lru_conv_w', 'ffn_conv_w')


def _dot(a, b):
    return jnp.dot(a, b, preferred_element_type=F32)


def _dot_nt(a, b):
    return lax.dot_general(a, b, (((1,), (1,)), ((), ())), preferred_element_type=F32)


def _dot_tn(a, b):
    return lax.dot_general(a, b, (((0,), (0,)), ((), ())), preferred_element_type=F32)


def _mx(a):
    return a.astype(MXU)


_GELU_C = math.sqrt(2.0 / math.pi)


def _gelu(x):
    th = jnp.tanh(_GELU_C * (x + 0.044715 * x * x * x))
    return 0.5 * x * (1.0 + th)


def _gelu_grad(x):
    th = jnp.tanh(_GELU_C * (x + 0.044715 * x * x * x))
    return 0.5 * (1.0 + th) + 0.5 * x * (1.0 - th * th) * _GELU_C * (1.0 + 3.0 * 0.044715 * x * x)


def _sigmoid(x):
    return 0.5 * jnp.tanh(0.5 * x) + 0.5


def _ln_stats(r):
    mu = jnp.mean(r, axis=-1, keepdims=True)
    xc = r - mu
    var = jnp.mean(xc * xc, axis=-1, keepdims=True)
    rstd = lax.rsqrt(var + LN_EPS)
    return xc * rstd, rstd


def _ln_bwd(dy, g, xhat, rstd):
    dxh = dy * g
    return rstd * (dxh - jnp.mean(dxh, axis=-1, keepdims=True) - xhat * jnp.mean(dxh * xhat, axis=-1, keepdims=True))


def _rms(y):
    return lax.rsqrt(jnp.mean(y * y, axis=-1, keepdims=True) + RMS_EPS)


def _sum0(a):
    return jnp.sum(a, axis=0, keepdims=True)


def _row_iota(shape):
    return lax.broadcasted_iota(jnp.int32, shape, 0)


def _shift_down(ext, j, rows):
    return pltpu.roll(ext, j, 0)[8:8 + rows]


def _shift_up(ext, j, rows):
    return pltpu.roll(ext, ext.shape[0] - j, 0)[:rows]


def _swap_halves(t):
    w = t.shape[1]
    lane = lax.broadcasted_iota(jnp.int32, t.shape, 1)
    return jnp.where((lane & 32) == 0, pltpu.roll(t, w - 32, 1), pltpu.roll(t, 32, 1))


def _rope(t, cos, sin_signed):
    return t * cos + _swap_halves(t) * sin_signed


def _rope_t(d, cos, sin_signed):
    return d * cos + _swap_halves(d * sin_signed)


def _cmul_add(ar, ai, xr, xi, yr, yi):
    return ar * xr - ai * xi + yr, ar * xi + ai * xr + yi


def _seg_rows(k):
    return slice(N_SEG * k, N_SEG * (k + 1))


def _permute_rows(perm, x):
    hi = _mx(x)
    rest = x - hi.astype(F32)
    mid = _mx(rest)
    lo = _mx(rest - mid.astype(F32))
    return _dot(perm, hi) + _dot(perm, mid) + _dot(perm, lo)


def _cscan(sr, si, tab, cin_r, cin_i, reverse):
    sgn = -1.0 if reverse else 1.0
    pw_re, pw_im, dbl_re, dbl_im = tab['pw_re'], tab['pw_im'], tab['dbl_re'], tab['dbl_im']
    ar, ai = pw_re[0:1, :], sgn * pw_im[0:1, :]
    shape = (N_SEG, sr.shape[1])
    hr = hi = jnp.zeros(shape, F32)
    for k in (range(SEG - 1, -1, -1) if reverse else range(SEG)):
        hr, hi = _cmul_add(ar, ai, hr, hi, sr[_seg_rows(k), :], si[_seg_rows(k), :])
        sr[_seg_rows(k), :] = hr
        si[_seg_rows(k), :] = hi
    sub = _row_iota(shape)

    def shifted(v, d):
        if reverse:
            return jnp.where(sub < N_SEG - d, pltpu.roll(v, N_SEG - d, 0), 0.0)
        return jnp.where(sub >= d, pltpu.roll(v, d, 0), 0.0)

    fr, fi = hr, hi
    for j, d in enumerate((1, 2, 4)):
        fr, fi = _cmul_add(dbl_re[j:j + 1, :], sgn * dbl_im[j:j + 1, :], shifted(fr, d), shifted(fi, d), fr, fi)
    seg_re, seg_im = (tab['segr_re'], tab['segr_im']) if reverse else (tab['seg_re'], tab['seg_im'])
    cr, ci = _cmul_add(seg_re[...], sgn * seg_im[...], cin_r, cin_i, shifted(fr, 1), shifted(fi, 1))
    nr, ni = _cmul_add(dbl_re[0:1, :], sgn * dbl_im[0:1, :], cr, ci, hr, hi)
    for k in range(SEG):
        j = SEG - 1 - k if reverse else k
        xr, xi = _cmul_add(pw_re[j:j + 1, :], sgn * pw_im[j:j + 1, :], cr, ci, sr[_seg_rows(k), :], si[_seg_rows(k), :])
        sr[_seg_rows(k), :] = xr
        si[_seg_rows(k), :] = xi
    edge = slice(0, 1) if reverse else slice(N_SEG - 1, N_SEG)
    return nr[edge], ni[edge]


def _rscan(a, b, reverse):
    rows = a.shape[0]
    row = _row_iota(a.shape)
    s = 1
    while s < rows:
        if reverse:
            keep = row < rows - s
            sa = jnp.where(keep, pltpu.roll(a, rows - s, 0), 1.0)
            sb = jnp.where(keep, pltpu.roll(b, rows - s, 0), 0.0)
        else:
            keep = row >= s
            sa = jnp.where(keep, pltpu.roll(a, s, 0), 1.0)
            sb = jnp.where(keep, pltpu.roll(b, s, 0), 0.0)
        b = b + a * sb
        a = a * sa
        s *= 2
    return a, b


def _whole():
    return pl.BlockSpec(memory_space=pltpu.VMEM)


def _rows_spec(rows, cols, n_tiles, reverse=False):
    if reverse:
        return pl.BlockSpec((rows, cols), lambda i: (n_tiles - 1 - i, 0))
    return pl.BlockSpec((rows, cols), lambda i: (i, 0))


def _halo_spec(cols, tile_rows, n_tiles, reverse=False):
    per = tile_rows // 8
    if reverse:
        return pl.BlockSpec((8, cols), lambda i: (jnp.maximum((n_tiles - 1 - i) * per - 1, 0), 0))
    return pl.BlockSpec((8, cols), lambda i: (jnp.maximum(i * per - 1, 0), 0))


def _call(body, name, n_tiles, in_specs, out_specs, out_shape, scratch=()):
    return pl.pallas_call(
        body, name=name, grid=(n_tiles,), in_specs=in_specs, out_specs=out_specs, out_shape=out_shape,
        scratch_shapes=list(scratch),
        compiler_params=pltpu.CompilerParams(dimension_semantics=("arbitrary",), vmem_limit_bytes=VMEM_MB << 20))


def _sds(shape, dtype=F32):
    return jax.ShapeDtypeStruct(shape, dtype)


def _inproj_fwd(x, w, b, cos_t, sin_t):
    n = x.shape[0]
    nt = n // TILE_WIDE

    def body(x_ref, w_ref, b_ref, c_ref, s_ref, q_ref, k_ref, v_ref, u_ref, xr_ref, g_ref):
        p = _dot(_mx(x_ref[...]), w_ref[...]) + b_ref[...]
        cos, sin = c_ref[...], s_ref[...]
        q_ref[...] = _mx(_rope(p[:, :D_ATTN], jnp.tile(cos, (1, 4)), jnp.tile(sin, (1, 4))))
        k_ref[...] = _mx(_rope(p[:, 512:640], cos, sin))
        v_ref[...] = _mx(p[:, 640:768])
        u_ref[...] = p[:, 768:1024]
        xr_ref[...] = p[:, 1024:1280]
        g_ref[...] = p[:, 1280:1536]

    r = functools.partial(_rows_spec, n_tiles=nt)
    return _call(
        body, "inproj_fwd", nt,
        [r(TILE_WIDE, D), _whole(), _whole(), r(TILE_WIDE, 128), r(TILE_WIDE, 128)],
        [r(TILE_WIDE, D_ATTN), r(TILE_WIDE, D_KV), r(TILE_WIDE, D_KV), r(TILE_WIDE, D_S5), r(TILE_WIDE, D_LRU), r(TILE_WIDE, D_LRU)],
        [_sds((n, D_ATTN), MXU), _sds((n, D_KV), MXU), _sds((n, D_KV), MXU), _sds((n, D_S5)), _sds((n, D_LRU)), _sds((n, D_LRU))],
    )(x, w, b, cos_t, sin_t)


def _inproj_bwd(dq, dk, dv, du, dxr, dgate, cos_t, sin_t, x0, dr1, w_t):
    n = x0.shape[0]
    nt = n // TILE_WIDE

    def body(dq_ref, dk_ref, dv_ref, du_ref, dxr_ref, dg_ref, c_ref, s_ref, x_ref, dr_ref, w_ref, dx_ref, dw_ref, db_ref):
        @pl.when(pl.program_id(0) == 0)
        def _():
            dw_ref[...] = jnp.zeros_like(dw_ref)
            db_ref[...] = jnp.zeros_like(db_ref)

        cos, sin = c_ref[...], s_ref[...]
        dtq = _rope_t(dq_ref[...], jnp.tile(cos, (1, 4)), jnp.tile(sin, (1, 4)))
        dtk = _rope_t(dk_ref[...], cos, sin)
        dp = jnp.concatenate([dtq, dtk, dv_ref[...], du_ref[...], dxr_ref[...], dg_ref[...]], axis=1)
        db_ref[...] += _sum0(dp)
        dpb = _mx(dp)
        dw_ref[...] += _dot_tn(dpb, _mx(x_ref[...]))
        dx_ref[...] = ALPHA * dr_ref[...] + _dot(dpb, w_ref[...])

    r = functools.partial(_rows_spec, n_tiles=nt)
    return _call(
        body, "inproj_bwd", nt,
        [r(TILE_WIDE, D_ATTN), r(TILE_WIDE, D_KV), r(TILE_WIDE, D_KV), r(TILE_WIDE, D_S5), r(TILE_WIDE, D_LRU), r(TILE_WIDE, D_LRU),
         r(TILE_WIDE, 128), r(TILE_WIDE, 128), r(TILE_WIDE, D), r(TILE_WIDE, D), _whole()],
        [r(TILE_WIDE, D), _whole(), _whole()],
        [_sds((n, D)), _sds((D_IN, D)), _sds((1, D_IN))],
    )(dq, dk, dv, du, dxr, dgate, cos_t, sin_t, x0, dr1, w_t)


def _kv_variants(t, lo):
    tr = pltpu.roll(t, 64, 1)
    out = []
    for j in range(2):
        first = jnp.where(lo, t if j == 0 else tr, 0.0)
        second = jnp.where(lo, 0.0, tr if j == 0 else t)
        out.append(_mx(jnp.concatenate([first, second], axis=0)))
    return out


def _kv_collect(x0, x1, lo):
    a = x0[:256] + pltpu.roll(x0[256:], 64, 1)
    b = pltpu.roll(x1[:256], 64, 1) + x1[256:]
    return jnp.where(lo, a, b)


def _row_sums(x):
    ones = jnp.ones((128, 128), MXU)
    hi = _mx(x)
    lo = _mx(x - hi.astype(F32))
    return _dot(hi, ones) + _dot(lo, ones)


def _attn_probs(s, sink_ref):
    out = []
    for hp in range(2):
        sh = s[:, hp * 128:(hp + 1) * 128]
        sink = sink_ref[hp]
        m = jnp.maximum(jnp.broadcast_to(jnp.max(sh, axis=1, keepdims=True), sh.shape), sink)
        p = jnp.exp(sh - m)
        es = jnp.exp(sink - m)
        inv = 1.0 / (_row_sums(p) + es)
        out.append((p * inv, es * inv))
    return out


def _band_merge(x, tri, no_previous=None):
    bands = []
    for hp in range(2):
        prev, own = x[:, hp * 256:hp * 256 + 128], x[:, hp * 256 + 128:hp * 256 + 256]
        if no_previous is not None:
            prev = jnp.where(no_previous, -jnp.inf, prev)
        bands.append(jnp.where(tri, own, prev))
    return jnp.concatenate(bands, axis=1)


def _band_split(y, tri):
    parts = []
    for hp in range(2):
        band = y[:, hp * 128:(hp + 1) * 128]
        parts += [jnp.where(tri, 0.0, band), jnp.where(tri, band, 0.0)]
    return jnp.concatenate(parts, axis=1)


def _tri():
    shape = (ATTN_BLOCK, ATTN_BLOCK)
    return lax.broadcasted_iota(jnp.int32, shape, 0) >= lax.broadcasted_iota(jnp.int32, shape, 1)


def _attn_scores(q_ref, k_ref, v_ref, nb, tri):
    lo = lax.broadcasted_iota(jnp.int32, (256, 128), 1) < 64
    kcats, vcats, kstarts, parts = [], [], [], []
    for b in range(nb):
        block = pl.program_id(0) * nb + b
        kstart = pl.multiple_of(block * ATTN_BLOCK, ATTN_BLOCK)
        kcat = _kv_variants(k_ref[pl.ds(kstart, 256), :].astype(F32), lo)
        kcats.append(kcat)
        vcats.append(_kv_variants(v_ref[pl.ds(kstart, 256), :].astype(F32), lo))
        kstarts.append(kstart)
        for j in range(2):
            s = _dot_nt(_kv_group(q_ref, b, j), kcat[j]) * 0.125
            parts += [_band_merge(s[:ATTN_BLOCK], tri, block == 0), _band_merge(s[ATTN_BLOCK:], tri, block == 0)]
    return jnp.concatenate(parts, axis=0), kcats, vcats, kstarts


def _kv_group(a, b, j):
    rows = slice(b * ATTN_BLOCK, (b + 1) * ATTN_BLOCK)
    return jnp.concatenate([a[rows, 2 * j * 128:(2 * j + 1) * 128], a[rows, (2 * j + 1) * 128:(2 * j + 2) * 128]], axis=0)


def _put_kv_group(ref, b, j, x):
    rows = slice(b * ATTN_BLOCK, (b + 1) * ATTN_BLOCK)
    ref[rows, 2 * j * 128:(2 * j + 1) * 128] = x[:ATTN_BLOCK]
    ref[rows, (2 * j + 1) * 128:(2 * j + 2) * 128] = x[ATTN_BLOCK:]


def _band_split_group(y, unit, tri):
    return _mx(jnp.concatenate([_band_split(y[unit:unit + ATTN_BLOCK], tri),
                                _band_split(y[unit + ATTN_BLOCK:unit + 2 * ATTN_BLOCK], tri)], axis=0))


def _attn_fwd(q, k, v, sink_cols):
    n = q.shape[0]
    nt = n // TILE_Q
    nb = TILE_Q // ATTN_BLOCK

    def body(q_ref, k_ref, v_ref, s_ref, o_ref):
        tri = _tri()
        s, _, vcats, _ = _attn_scores(q_ref, k_ref, v_ref, nb, tri)
        (p0, _), (p1, _) = _attn_probs(s, s_ref)
        p = jnp.concatenate([p0, p1], axis=1)
        for b in range(nb):
            for j in range(2):
                _put_kv_group(o_ref, b, j, _dot(_band_split_group(p, (b * 4 + 2 * j) * ATTN_BLOCK, tri), vcats[b][j]))

    return _call(
        body, "attn_fwd", nt,
        [_rows_spec(TILE_Q, D_ATTN, nt), _whole(), _whole(), _whole()],
        _rows_spec(TILE_Q, D_ATTN, nt), _sds((n, D_ATTN)),
    )(q, k, v, sink_cols)


def _attn_bwd(q, k, v, sink_cols, o, do):
    n = q.shape[0]
    nt = n // TILE_Q
    nb = TILE_Q // ATTN_BLOCK

    def body(q_ref, k_ref, v_ref, s_ref, o_ref, do_ref, dq_ref, dk_ref, dv_ref, ds_ref):
        @pl.when(pl.program_id(0) == 0)
        def _():
            dk_ref[...] = jnp.zeros_like(dk_ref)
            dv_ref[...] = jnp.zeros_like(dv_ref)
            ds_ref[...] = jnp.zeros_like(ds_ref)

        lo = lax.broadcasted_iota(jnp.int32, (256, 128), 1) < 64
        tri = _tri()
        s, kcats, vcats, kstarts = _attn_scores(q_ref, k_ref, v_ref, nb, tri)
        probs = _attn_probs(s, s_ref)
        do = do_ref[...]
        dob = _mx(do)
        od = do * o_ref[...]
        lo_q = (lax.broadcasted_iota(jnp.int32, od.shape, 1) & 64) == 0
        od_head = (jnp.where(lo_q, od, 0.0), jnp.where(lo_q, 0.0, od))
        units = [(b, i) for b in range(nb) for i in range(4)]

        def tile_part(a, b, i):
            return a[b * ATTN_BLOCK:(b + 1) * ATTN_BLOCK, i * 128:(i + 1) * 128]

        dp = []
        for b in range(nb):
            for j in range(2):
                x = _dot_nt(_kv_group(dob, b, j), vcats[b][j])
                dp += [_band_merge(x[:ATTN_BLOCK], tri), _band_merge(x[ATTN_BLOCK:], tri)]
        dp = jnp.concatenate(dp, axis=0)
        ds = []
        for hp in range(2):
            p, p_sink = probs[hp]
            delta = _row_sums(jnp.concatenate([tile_part(od_head[hp], b, i) for b, i in units], axis=0))
            ds.append(p * (dp[:, hp * 128:(hp + 1) * 128] - delta) * 0.125)
            t = p_sink * delta
            for i in range(4):
                ds_ref[2 * i + hp:2 * i + hp + 1, :] -= sum(
                    _sum0(t[(b * 4 + i) * ATTN_BLOCK:(b * 4 + i + 1) * ATTN_BLOCK]) for b in range(nb))
        ds = jnp.concatenate(ds, axis=1)
        p = jnp.concatenate([probs[0][0], probs[1][0]], axis=1)
        for b in range(nb):
            dkc, dvc = [], []
            for j in range(2):
                unit = (b * 4 + 2 * j) * ATTN_BLOCK
                dsb = _band_split_group(ds, unit, tri)
                _put_kv_group(dq_ref, b, j, _dot(dsb, kcats[b][j]))
                dkc.append(_dot_tn(dsb, _kv_group(q_ref, b, j)))
                dvc.append(_dot_tn(_band_split_group(p, unit, tri), _kv_group(dob, b, j)))
            dk_ref[pl.ds(kstarts[b], 256), :] += _kv_collect(dkc[0], dkc[1], lo)
            dv_ref[pl.ds(kstarts[b], 256), :] += _kv_collect(dvc[0], dvc[1], lo)

    r = _rows_spec(TILE_Q, D_ATTN, nt)
    return _call(
        body, "attn_bwd", nt,
        [r, _whole(), _whole(), _whole(), r, r],
        [r, _whole(), _whole(), _whole()],
        [_sds((n, D_ATTN)), _sds((n + ATTN_BLOCK, D_KV)), _sds((n + ATTN_BLOCK, D_KV)), _sds((8, 128))],
    )(q, k, v, sink_cols, o, do)


S5_TABLES = ('pw_re', 'pw_im', 'dbl_re', 'dbl_im', 'seg_re', 'seg_im', 'segr_re', 'segr_im')
S5_WEIGHTS = ('b_re', 'b_im', 'c_re', 'c_im', 'd', 'glu_w', 'glu_b', 'perm', 'perm_t')


def _s5_states(u, carry_r, carry_i, b_re, b_im, tab, hr_s, hi_s):
    ub = _mx(u)
    hr_s[...] = _dot(ub, b_re[...])
    hi_s[...] = _dot(ub, b_im[...])
    return ub, _cscan(hr_s, hi_s, tab, carry_r, carry_i, reverse=False)


def _s5_fwd(u, prm):
    n = u.shape[0]
    nt = n // (S5_PAIR * TILE)

    def body(u_ref, *refs):
        tab = dict(zip(S5_TABLES, refs[:8]))
        b_re, b_im, c_re, c_im, d_ref, gw_ref, gb_ref, perm, perm_t = refs[8:17]
        y_ref, cr_out, ci_out, cr_s, ci_s = refs[17:22]
        states = [refs[22 + 2 * j:24 + 2 * j] for j in range(S5_PAIR)]
        rows = [slice(j * TILE, (j + 1) * TILE) for j in range(S5_PAIR)]

        @pl.when(pl.program_id(0) == 0)
        def _():
            cr_s[...] = jnp.zeros_like(cr_s)
            ci_s[...] = jnp.zeros_like(ci_s)

        us = []
        for j in range(S5_PAIR):
            us.append(_permute_rows(perm[...], u_ref[rows[j], :]))
            ub = _mx(us[j])
            states[j][0][...] = _dot(ub, b_re[...])
            states[j][1][...] = _dot(ub, b_im[...])
        for j in range(S5_PAIR):
            cr, ci = cr_s[...], ci_s[...]
            cr_out[8 * j:8 * j + 8, :] = jnp.broadcast_to(cr, (8, N_STATE))
            ci_out[8 * j:8 * j + 8, :] = jnp.broadcast_to(ci, (8, N_STATE))
            cr_s[...], ci_s[...] = _cscan(*states[j], tab, cr, ci, reverse=False)
        for j in range(S5_PAIR):
            hr_s, hi_s = states[j]
            y = _dot(_mx(hr_s[...]), c_re[...]) - _dot(_mx(hi_s[...]), c_im[...]) + d_ref[...] * us[j]
            z = _gelu(y)
            y_ref[rows[j], :] = _permute_rows(perm_t[...], z * _sigmoid(_dot(_mx(z), gw_ref[...]) + gb_ref[...]))

    r = functools.partial(_rows_spec, n_tiles=nt)
    return _call(
        body, "s5_fwd", nt,
        [r(S5_PAIR * TILE, D_S5)] + [_whole()] * 17,
        [r(S5_PAIR * TILE, D_S5), r(S5_PAIR * 8, N_STATE), r(S5_PAIR * 8, N_STATE)],
        [_sds((n, D_S5)), _sds((n // TILE * 8, N_STATE)), _sds((n // TILE * 8, N_STATE))],
        scratch=[pltpu.VMEM((1, N_STATE), F32)] * 2 + [pltpu.VMEM((TILE, N_STATE), F32)] * (2 * S5_PAIR),
    )(u, *[prm[k] for k in S5_TABLES + S5_WEIGHTS])


def _s5_bwd(u, dys, carry_re, carry_im, prm):
    n = u.shape[0]
    nt = n // (S5_PAIR * TILE)

    def body(u_ref, dy_ref, cin_r, cin_i, *refs):
        tab = dict(zip(S5_TABLES, refs[:8]))
        b_re, b_im, c_re, c_im, d_ref, gw_ref, gb_ref, perm, perm_t = refs[8:17]
        du_ref, dbr_ref, dbi_ref, dcr_ref, dci_ref, dar_ref, dai_ref, dd_ref, dgw_ref, dgb_ref = refs[17:27]
        gr_s, gi_s = refs[27:29]
        scratch = [refs[29 + 4 * j:33 + 4 * j] for j in range(S5_PAIR)]
        later_first = list(reversed(range(S5_PAIR)))
        rows = [slice(j * TILE, (j + 1) * TILE) for j in range(S5_PAIR)]

        @pl.when(pl.program_id(0) == 0)
        def _():
            for ref in (dbr_ref, dbi_ref, dcr_ref, dci_ref, dar_ref, dai_ref, dd_ref, dgw_ref, dgb_ref, gr_s, gi_s):
                ref[...] = jnp.zeros_like(ref)

        us, ubs, carries, dy_of = {}, {}, {}, {}
        for j in later_first:
            us[j] = _permute_rows(perm[...], u_ref[rows[j], :])
            carries[j] = (cin_r[8 * j:8 * j + 1, :], cin_i[8 * j:8 * j + 1, :])
            ubs[j], _ = _s5_states(us[j], *carries[j], b_re, b_im, tab, *scratch[j][:2])
        for j in later_first:
            u = us[j]
            hr_s, hi_s, gr_t, gi_t = scratch[j]
            hrb, hib = _mx(hr_s[...]), _mx(hi_s[...])
            y = _dot(hrb, c_re[...]) - _dot(hib, c_im[...]) + d_ref[...] * u
            z = _gelu(y)
            zb = _mx(z)
            sg = _sigmoid(_dot(zb, gw_ref[...]) + gb_ref[...])
            dout = _permute_rows(perm[...], dy_ref[rows[j], :])
            dpre = dout * z * sg * (1.0 - sg)
            dgb_ref[...] += _sum0(dpre)
            dpb = _mx(dpre)
            dgw_ref[...] += _dot_tn(zb, dpb)
            dy = (dout * sg + _dot_nt(dpb, gw_ref[...])) * _gelu_grad(y)
            dd_ref[...] += _sum0(dy * u)
            dyb = _mx(dy)
            dcr_ref[...] += _dot_tn(hrb, dyb)
            dci_ref[...] -= _dot_tn(hib, dyb)
            gr_t[...] = _dot_nt(dyb, c_re[...])
            gi_t[...] = -_dot_nt(dyb, c_im[...])
            dy_of[j] = dy
        for j in later_first:
            gr_s[...], gi_s[...] = _cscan(*scratch[j][2:], tab, gr_s[...], gi_s[...], reverse=True)
        for j in later_first:
            hr_s, hi_s, gr_t, gi_t = scratch[j]
            cr, ci = carries[j]
            sub = _row_iota((N_SEG, N_STATE))
            acc_r = acc_i = jnp.zeros((N_SEG, N_STATE), F32)
            for k in range(SEG):
                if k == 0:
                    hpr = jnp.where(sub >= 1, pltpu.roll(hr_s[_seg_rows(SEG - 1), :], 1, 0), cr)
                    hpi = jnp.where(sub >= 1, pltpu.roll(hi_s[_seg_rows(SEG - 1), :], 1, 0), ci)
                else:
                    hpr, hpi = hr_s[_seg_rows(k - 1), :], hi_s[_seg_rows(k - 1), :]
                gr, gi = gr_t[_seg_rows(k), :], gi_t[_seg_rows(k), :]
                acc_r = acc_r + gr * hpr + gi * hpi
                acc_i = acc_i + gi * hpr - gr * hpi
            dar_ref[...] += _sum0(acc_r)
            dai_ref[...] += _sum0(acc_i)
            grb, gib = _mx(gr_t[...]), _mx(gi_t[...])
            dbr_ref[...] += _dot_tn(ubs[j], grb)
            dbi_ref[...] += _dot_tn(ubs[j], gib)
            du_ref[rows[j], :] = _permute_rows(perm_t[...], dy_of[j] * d_ref[...] + _dot_nt(grb, b_re[...]) + _dot_nt(gib, b_im[...]))

    r = functools.partial(_rows_spec, n_tiles=nt, reverse=True)
    return _call(
        body, "s5_bwd", nt,
        [r(S5_PAIR * TILE, D_S5), r(S5_PAIR * TILE, D_S5), r(S5_PAIR * 8, N_STATE), r(S5_PAIR * 8, N_STATE)] + [_whole()] * 17,
        [r(S5_PAIR * TILE, D_S5)] + [_whole()] * 9,
        [_sds((n, D_S5)), _sds((D_S5, N_STATE)), _sds((D_S5, N_STATE)), _sds((N_STATE, D_S5)), _sds((N_STATE, D_S5)),
         _sds((1, N_STATE)), _sds((1, N_STATE)), _sds((1, D_S5)), _sds((D_S5, D_S5)), _sds((1, D_S5))],
        scratch=[pltpu.VMEM((1, N_STATE), F32)] * 2 + [pltpu.VMEM((TILE, N_STATE), F32)] * (4 * S5_PAIR),
    )(u, dys, carry_re, carry_im, *[prm[k] for k in S5_TABLES + S5_WEIGHTS])


def _lru_gates(xr, halo, tile_index, cw_ref, cb_ref, wx_ref, wa_ref, bx_ref, ba_ref, sp_ref):
    ext = jnp.concatenate([halo, xr], axis=0)
    sh = [xr] + [_shift_down(ext, j, TILE) for j in (1, 2, 3)]
    xc = cb_ref[...] + cw_ref[3:4, :] * sh[0] + cw_ref[2:3, :] * sh[1] + cw_ref[1:2, :] * sh[2] + cw_ref[0:1, :] * sh[3]
    xb = _mx(xc)
    gx = _sigmoid(_dot(xb, wx_ref[...]) + bx_ref[...])
    ga = _sigmoid(_dot(xb, wa_ref[...]) + ba_ref[...])
    la = -LRU_C * ga * sp_ref[...]
    a = jnp.exp(la)
    start = (tile_index * TILE + _row_iota(xr.shape)) == 0
    mult = jnp.where(start, 1.0, jnp.sqrt(-jnp.tanh(la) * (a * a + 1.0)))
    return sh, xc, xb, gx, ga, a, mult, start


def _lru_fwd(xr, gate, prm):
    n = xr.shape[0]
    nt = n // TILE

    def body(x_ref, g_ref, cw_ref, cb_ref, wx_ref, wa_ref, bx_ref, ba_ref, sp_ref, y_ref, c_out, halo_s, c_s):
        first_tile = pl.program_id(0) == 0

        @pl.when(first_tile)
        def _():
            halo_s[...] = jnp.zeros_like(halo_s)
            c_s[...] = jnp.zeros_like(c_s)

        xr = x_ref[...]
        _, xc, _, gx, _, a, mult, _ = _lru_gates(xr, halo_s[...], pl.program_id(0), cw_ref, cb_ref, wx_ref, wa_ref, bx_ref, ba_ref,
                                                 sp_ref)
        halo_s[...] = xr[TILE - 8:]
        acum, h = _rscan(a, mult * gx * xc, reverse=False)
        c = c_s[...]
        c_out[...] = jnp.broadcast_to(c, (8, D_LRU))
        h = h + acum * c
        c_s[...] = h[TILE - 1:TILE]
        y_ref[...] = h * _gelu(g_ref[...])

    r = functools.partial(_rows_spec, n_tiles=nt)
    return _call(
        body, "lru_fwd", nt,
        [r(TILE, D_LRU), r(TILE, D_LRU)] + [_whole()] * 7,
        [r(TILE, D_LRU), r(8, D_LRU)],
        [_sds((n, D_LRU)), _sds((nt * 8, D_LRU))],
        scratch=[pltpu.VMEM((8, D_LRU), F32), pltpu.VMEM((1, D_LRU), F32)],
    )(xr, gate, prm['conv_w'], prm['conv_b'], prm['wx'], prm['wa'], prm['bx'], prm['ba'], prm['sp'])


def _lru_bwd(xr, gate, dyl, carry, prm):
    n = xr.shape[0]
    nt = n // TILE

    def body(x_ref, xh_ref, g_ref, dy_ref, cin_ref, cw_ref, cb_ref, wx_ref, wa_ref, bx_ref, ba_ref, sp_ref,
             dx_ref, dg_ref, dcw0, dcw1, dcw2, dcw3, dcb_ref, dwx_ref, dwa_ref, dbx_ref, dba_ref, dsp_ref, an_s, gn_s, dn_s):
        first_tile = pl.program_id(0) == nt - 1

        @pl.when(pl.program_id(0) == 0)
        def _():
            for ref in (dcw0, dcw1, dcw2, dcw3, dcb_ref, dwx_ref, dwa_ref, dbx_ref, dba_ref, dsp_ref, gn_s, dn_s):
                ref[...] = jnp.zeros_like(ref)
            an_s[...] = jnp.ones_like(an_s)

        xr = x_ref[...]
        halo = jnp.where(first_tile, 0.0, xh_ref[...])
        sh, xc, xb, gx, ga, a, mult, start = _lru_gates(xr, halo, nt - 1 - pl.program_id(0), cw_ref, cb_ref, wx_ref, wa_ref, bx_ref,
                                                        ba_ref, sp_ref)
        acum, h = _rscan(a, mult * gx * xc, reverse=False)
        cin = cin_ref[0:1, :]
        h = h + acum * cin
        gate = g_ref[...]
        dyl = dy_ref[...]
        dg_ref[...] = dyl * h * _gelu_grad(gate)
        row = _row_iota(xr.shape)
        alpha = jnp.where(row < TILE - 1, pltpu.roll(a, TILE - 1, 0), an_s[...])
        racc, g = _rscan(alpha, dyl * _gelu(gate), reverse=True)
        g = g + racc * gn_s[...]
        an_s[...] = a[0:1]
        gn_s[...] = g[0:1]
        hprev = jnp.where(row == 0, cin, pltpu.roll(h, 1, 0))
        da = g * hprev
        dmult = jnp.where(start, 0.0, g * gx * xc)
        dla = da * a - dmult * a * a / mult
        dsp_ref[...] += _sum0(-LRU_C * ga * dla)
        dpa = (-LRU_C * sp_ref[...] * dla) * ga * (1.0 - ga)
        dpx = (g * mult * xc) * gx * (1.0 - gx)
        dba_ref[...] += _sum0(dpa)
        dbx_ref[...] += _sum0(dpx)
        dpab, dpxb = _mx(dpa), _mx(dpx)
        dwa_ref[...] += _dot_tn(xb, dpab)
        dwx_ref[...] += _dot_tn(xb, dpxb)
        dxc = g * mult * gx + _dot_nt(dpab, wa_ref[...]) + _dot_nt(dpxb, wx_ref[...])
        dcb_ref[...] += _sum0(dxc)
        dcw3[...] += _sum0(dxc * sh[0])
        dcw2[...] += _sum0(dxc * sh[1])
        dcw1[...] += _sum0(dxc * sh[2])
        dcw0[...] += _sum0(dxc * sh[3])
        ext = jnp.concatenate([dxc, dn_s[...]], axis=0)
        dx_ref[...] = (cw_ref[3:4, :] * dxc + cw_ref[2:3, :] * _shift_up(ext, 1, TILE) + cw_ref[1:2, :] * _shift_up(ext, 2, TILE)
                       + cw_ref[0:1, :] * _shift_up(ext, 3, TILE))
        dn_s[...] = dxc[:8]

    r = functools.partial(_rows_spec, n_tiles=nt, reverse=True)
    vec = _sds((1, D_LRU))
    return _call(
        body, "lru_bwd", nt,
        [r(TILE, D_LRU), _halo_spec(D_LRU, TILE, nt, reverse=True), r(TILE, D_LRU), r(TILE, D_LRU), r(8, D_LRU)] + [_whole()] * 7,
        [r(TILE, D_LRU), r(TILE, D_LRU)] + [_whole()] * 10,
        [_sds((n, D_LRU)), _sds((n, D_LRU)), vec, vec, vec, vec, vec, _sds((D_LRU, D_LRU)), _sds((D_LRU, D_LRU)), vec, vec, vec],
        scratch=[pltpu.VMEM((1, D_LRU), F32), pltpu.VMEM((1, D_LRU), F32), pltpu.VMEM((8, D_LRU), F32)],
    )(xr, xr, gate, dyl, carry, prm['conv_w'], prm['conv_b'], prm['wx'], prm['wa'], prm['bx'], prm['ba'], prm['sp'])


def _normed_parts(ya, ys, yl):
    return jnp.concatenate([ya * _rms(ya), ys * _rms(ys), yl * _rms(yl)], axis=1)


def _mixout_fwd(ya, ys, yl, x0, g_mix, w_out, b_out, g1, b1):
    n = x0.shape[0]
    nt = n // TILE_WIDE

    def body(ya_ref, ys_ref, yl_ref, x_ref, gm_ref, w_ref, b_ref, g_ref, be_ref, mix_ref, r_ref, x1_ref):
        mixb = _mx(_normed_parts(ya_ref[...], ys_ref[...], yl_ref[...]) * gm_ref[...])
        mix_ref[...] = mixb
        r1 = ALPHA * x_ref[...] + _dot(mixb, w_ref[...]) + b_ref[...]
        r_ref[...] = r1
        xhat, _ = _ln_stats(r1)
        x1_ref[...] = xhat * g_ref[...] + be_ref[...]

    r = functools.partial(_rows_spec, n_tiles=nt)
    return _call(
        body, "mixout_fwd", nt,
        [r(TILE_WIDE, D_ATTN), r(TILE_WIDE, D_S5), r(TILE_WIDE, D_LRU), r(TILE_WIDE, D)] + [_whole()] * 5,
        [r(TILE_WIDE, D), r(TILE_WIDE, D), r(TILE_WIDE, D)],
        [_sds((n, D), MXU), _sds((n, D)), _sds((n, D))],
    )(ya, ys, yl, x0, g_mix, w_out, b_out, g1, b1)


def _mixout_bwd(dr1, mix, ya, ys, yl, g_mix, w_out):
    n = dr1.shape[0]
    nt = n // TILE_WIDE

    def body(dr_ref, mix_ref, ya_ref, ys_ref, yl_ref, gm_ref, w_ref, dya_ref, dys_ref, dyl_ref, dw_ref, db_ref, dgm_ref):
        @pl.when(pl.program_id(0) == 0)
        def _():
            for ref in (dw_ref, db_ref, dgm_ref):
                ref[...] = jnp.zeros_like(ref)

        dr = dr_ref[...]
        db_ref[...] += _sum0(dr)
        drb = _mx(dr)
        dw_ref[...] += _dot_tn(mix_ref[...], drb)
        dmix = _dot(drb, w_ref[...])
        parts = (ya_ref[...], ys_ref[...], yl_ref[...])
        dgm_ref[...] += _sum0(dmix * _normed_parts(*parts))
        dn = dmix * gm_ref[...]
        lo = 0
        for y, out in zip(parts, (dya_ref, dys_ref, dyl_ref)):
            w = y.shape[1]
            rs = _rms(y)
            nrm = y * rs
            dnp = dn[:, lo:lo + w]
            out[...] = rs * (dnp - nrm * jnp.mean(dnp * nrm, axis=-1, keepdims=True))
            lo += w

    r = functools.partial(_rows_spec, n_tiles=nt)
    return _call(
        body, "mixout_bwd", nt,
        [r(TILE_WIDE, D), r(TILE_WIDE, D), r(TILE_WIDE, D_ATTN), r(TILE_WIDE, D_S5), r(TILE_WIDE, D_LRU), _whole(), _whole()],
        [r(TILE_WIDE, D_ATTN), r(TILE_WIDE, D_S5), r(TILE_WIDE, D_LRU), _whole(), _whole(), _whole()],
        [_sds((n, D_ATTN)), _sds((n, D_S5)), _sds((n, D_LRU)), _sds((D, D)), _sds((1, D)), _sds((1, D))],
    )(dr1, mix, ya, ys, yl, g_mix, w_out)


def _ffn_conv(gp, halo, cw_ref, cb_ref, cs):
    ext = jnp.concatenate([halo, gp], axis=0)
    s1 = _shift_down(ext, 1, TILE)
    s2 = _shift_down(ext, 2, TILE)
    return s1, s2, cb_ref[:, cs] + cw_ref[2:3, cs] * gp + cw_ref[1:2, cs] * s1 + cw_ref[0:1, cs] * s2


def _ffn_fwd(x1, wg, wu, cw, cb, wd, g2, b2):
    n = x1.shape[0]
    nt = n // TILE

    def body(x_ref, wg_ref, wu_ref, cw_ref, cb_ref, wd_ref, g_ref, be_ref, gp_ref, up_ref, r_ref, x2_ref, halo_s, act_s):
        @pl.when(pl.program_id(0) == 0)
        def _():
            halo_s[...] = jnp.zeros_like(halo_s)

        x1 = x_ref[...]
        xb = _mx(x1)
        for c in range(D_FF // FF_CHUNK):
            cs = slice(c * FF_CHUNK, (c + 1) * FF_CHUNK)
            gp = _dot(xb, wg_ref[:, cs])
            up = _dot(xb, wu_ref[:, cs])
            gp_ref[:, cs] = gp
            up_ref[:, cs] = up
            _, _, gc = _ffn_conv(gp, halo_s[:, cs], cw_ref, cb_ref, cs)
            halo_s[:, cs] = gp[TILE - 8:]
            act_s[:, cs] = _mx(gc * _sigmoid(gc) * up)
        r2 = ALPHA * x1 + _dot(act_s[...], wd_ref[...])
        r_ref[...] = r2
        xhat, _ = _ln_stats(r2)
        x2_ref[...] = xhat * g_ref[...] + be_ref[...]

    r = functools.partial(_rows_spec, n_tiles=nt)
    return _call(
        body, "ffn_fwd", nt,
        [r(TILE, D)] + [_whole()] * 7,
        [r(TILE, D_FF), r(TILE, D_FF), r(TILE, D), r(TILE, D)],
        [_sds((n, D_FF)), _sds((n, D_FF)), _sds((n, D)), _sds((n, D))],
        scratch=[pltpu.VMEM((8, D_FF), F32), pltpu.VMEM((TILE, D_FF), MXU)],
    )(x1, wg, wu, cw, cb, wd, g2, b2)


def _ffn_bwd_down(dx2, r2, g2, gp, up, cw, cb, wd_t):
    n = dx2.shape[0]
    nt = n // TILE

    def body(dx_ref, r_ref, g_ref, gp_ref, gh_ref, up_ref, cw_ref, cb_ref, wd_ref,
             dr_ref, dgp_ref, dup_ref, dwd_ref, dcw0, dcw1, dcw2, dcb_ref, dg_ref, db_ref, next_s):
        first_tile = pl.program_id(0) == nt - 1

        @pl.when(pl.program_id(0) == 0)
        def _():
            for ref in (dwd_ref, dcw0, dcw1, dcw2, dcb_ref, dg_ref, db_ref, next_s):
                ref[...] = jnp.zeros_like(ref)

        dx2 = dx_ref[...]
        xhat, rstd = _ln_stats(r_ref[...])
        dg_ref[...] += _sum0(dx2 * xhat)
        db_ref[...] += _sum0(dx2)
        dr2 = _ln_bwd(dx2, g_ref[...], xhat, rstd)
        dr_ref[...] = dr2
        dfb = _mx(dr2)
        for c in range(D_FF // FF_CHUNK):
            cs = slice(c * FF_CHUNK, (c + 1) * FF_CHUNK)
            gp = gp_ref[:, cs]
            up = up_ref[:, cs]
            s1, s2, gc = _ffn_conv(gp, jnp.where(first_tile, 0.0, gh_ref[:, cs]), cw_ref, cb_ref, cs)
            sg = _sigmoid(gc)
            silu = gc * sg
            dact = _dot(dfb, wd_ref[:, cs])
            dwd_ref[cs, :] += _dot_tn(_mx(silu * up), dfb)
            dup_ref[:, cs] = _mx(dact * silu)
            dgc = dact * up * (sg + silu * (1.0 - sg))
            dcb_ref[:, cs] += _sum0(dgc)
            dcw2[:, cs] += _sum0(dgc * gp)
            dcw1[:, cs] += _sum0(dgc * s1)
            dcw0[:, cs] += _sum0(dgc * s2)
            ext = jnp.concatenate([dgc, next_s[:, cs]], axis=0)
            dgp_ref[:, cs] = _mx(cw_ref[2:3, cs] * dgc + cw_ref[1:2, cs] * _shift_up(ext, 1, TILE)
                                 + cw_ref[0:1, cs] * _shift_up(ext, 2, TILE))
            next_s[:, cs] = dgc[:8]

    r = functools.partial(_rows_spec, n_tiles=nt, reverse=True)
    vff = _sds((1, D_FF))
    return _call(
        body, "ffn_bwd_down", nt,
        [r(TILE, D), r(TILE, D), _whole(), r(TILE, D_FF), _halo_spec(D_FF, TILE, nt, reverse=True), r(TILE, D_FF), _whole(), _whole(),
         _whole()],
        [r(TILE, D), r(TILE, D_FF), r(TILE, D_FF)] + [_whole()] * 7,
        [_sds((n, D)), _sds((n, D_FF), MXU), _sds((n, D_FF), MXU), _sds((D_FF, D)), vff, vff, vff, vff, _sds((1, D)), _sds((1, D))],
        scratch=[pltpu.VMEM((8, D_FF), F32)],
    )(dx2, r2, g2, gp, gp, up, cw, cb, wd_t)


def _ffn_bwd_dx(dr2, dgp, dup, r1, g1, wg_t, wu_t):
    n = dr2.shape[0]
    rows = TILE_BIG
    nt = n // rows

    def body(dr2_ref, dgp_ref, dup_ref, r_ref, g_ref, wg_ref, wu_ref, dr1_ref, dg_ref, db_ref):
        @pl.when(pl.program_id(0) == 0)
        def _():
            for ref in (dg_ref, db_ref):
                ref[...] = jnp.zeros_like(ref)

        dx1 = ALPHA * dr2_ref[...] + _dot(dgp_ref[...], wg_ref[...]) + _dot(dup_ref[...], wu_ref[...])
        xhat, rstd = _ln_stats(r_ref[...])
        dg_ref[...] += _sum0(dx1 * xhat)
        db_ref[...] += _sum0(dx1)
        dr1_ref[...] = _ln_bwd(dx1, g_ref[...], xhat, rstd)

    r = functools.partial(_rows_spec, n_tiles=nt)
    return _call(
        body, "ffn_bwd_dx", nt,
        [r(rows, D), r(rows, D_FF), r(rows, D_FF), r(rows, D), _whole(), _whole(), _whole()],
        [r(rows, D), _whole(), _whole()],
        [_sds((n, D)), _sds((1, D)), _sds((1, D))],
    )(dr2, dgp, dup, r1, g1, wg_t, wu_t)


def _ffn_bwd_dw(x1, dgp, dup):
    n = x1.shape[0]
    rows = TILE_BIG
    nt = n // rows

    def body(x_ref, dgp_ref, dup_ref, dwg_ref, dwu_ref):
        @pl.when(pl.program_id(0) == 0)
        def _():
            for ref in (dwg_ref, dwu_ref):
                ref[...] = jnp.zeros_like(ref)

        xb = _mx(x_ref[...])
        for c in range(D_FF // FF_CHUNK):
            cs = slice(c * FF_CHUNK, (c + 1) * FF_CHUNK)
            dwg_ref[cs, :] += _dot_tn(dgp_ref[:, cs], xb)
            dwu_ref[cs, :] += _dot_tn(dup_ref[:, cs], xb)

    r = functools.partial(_rows_spec, n_tiles=nt)
    return _call(
        body, "ffn_bwd_dw", nt,
        [r(rows, D), r(rows, D_FF), r(rows, D_FF)], [_whole(), _whole()], [_sds((D_FF, D)), _sds((D_FF, D))],
    )(x1, dgp, dup)


def _loss_head(y, target):
    n = y.shape[0]
    nt = n // TILE_WIDE

    def body(y_ref, t_ref, loss_ref, dy_ref):
        @pl.when(pl.program_id(0) == 0)
        def _():
            loss_ref[...] = jnp.zeros_like(loss_ref)

        e = y_ref[...] - t_ref[...]
        dy_ref[...] = e * (1.0 / D)
        loss_ref[...] += _sum0(jnp.sum(e * e, axis=1, keepdims=True)) * (0.5 / D)

    r = functools.partial(_rows_spec, n_tiles=nt)
    return _call(body, "loss_head", nt, [r(TILE_WIDE, D), r(TILE_WIDE, D)], [_whole(), r(TILE_WIDE, D)],
                 [_sds((1, 1)), _sds((n, D))])(y, target)


def _place():
    x, y, c = lax.axis_index("x"), lax.axis_index("y"), lax.axis_index("c")
    return x, y, c, 4 * x + 2 * y + c


def _peer(x, y, c, k):
    px, py, pc = x ^ ((k >> 2) & 1), y ^ ((k >> 1) & 1), c ^ (k & 1)
    return (px, py, pc), 4 * px + 2 * py + pc


def _all_gather(name, blocks, small):
    srcs = list(blocks) + [small]
    n = len(srcs)
    out_shapes = [_sds((a.shape[0], N_DEV * a.shape[1], LANES), a.dtype) for a in blocks] + [_sds((N_DEV,) + small.shape, small.dtype)]

    def body(*refs):
        src_refs, out_refs = refs[:n], refs[n:2 * n]
        send_sems, recv_sems, local_sems = refs[2 * n:]
        x, y, c, me = _place()

        def landing(a, slot):
            if a == n - 1:
                return out_refs[a].at[slot]
            r = src_refs[a].shape[1]
            return out_refs[a].at[:, pl.ds(slot * r, r), :]

        def remote(a, k, slot):
            peer, _ = _peer(x, y, c, k)
            return pltpu.make_async_remote_copy(
                src_ref=src_refs[a], dst_ref=landing(a, slot), send_sem=send_sems.at[a * N_DEV + k],
                recv_sem=recv_sems.at[a * N_DEV + k], device_id=peer, device_id_type=pl.DeviceIdType.MESH)

        mine = [pltpu.make_async_copy(src_refs[a], landing(a, me), local_sems.at[a]) for a in range(n)]
        sends = [remote(a, k, me) for a in range(n) for k in range(1, N_DEV)]
        for cp in mine + sends:
            cp.start()
        for a in range(n):
            for k in range(1, N_DEV):
                remote(a, k, _peer(x, y, c, k)[1]).wait_recv()
        for cp in sends:
            cp.wait_send()
        for cp in mine:
            cp.wait()

    any_space = pl.BlockSpec(memory_space=pl.ANY)
    return pl.pallas_call(
        body, name=name, out_shape=out_shapes, in_specs=[any_space] * n, out_specs=[any_space] * n,
        scratch_shapes=[pltpu.SemaphoreType.DMA((n * N_DEV,)), pltpu.SemaphoreType.DMA((n * N_DEV,)), pltpu.SemaphoreType.DMA((n,))],
    )(*srcs)


_HBM = pl.BlockSpec(memory_space=pltpu.HBM)
_SEM = pl.BlockSpec(memory_space=pltpu.SEMAPHORE)
_EFFECT = pltpu.SideEffectType.DATAFLOW_SIDE_EFFECTING


def _in_hbm(a):
    return pltpu.with_memory_space_constraint(a, pltpu.HBM)


def _scatter_start(name, srcs):
    ns = len(srcs)
    rows_a = [a.shape[0] // N_DEV for a in srcs]
    offs = [sum(rows_a[:a]) for a in range(ns)]
    total = sum(rows_a)

    def body(*refs):
        src_refs, land_ref, send_sems, recv_sems, token = refs[:ns], refs[ns], refs[ns + 1], refs[ns + 2], refs[-1]
        x, y, c, me = _place()
        for k in range(1, N_DEV):
            peer, peer_slot = _peer(x, y, c, k)
            for a in range(ns):
                pltpu.make_async_remote_copy(
                    src_ref=src_refs[a].at[pl.ds(peer_slot * rows_a[a], rows_a[a]), :],
                    dst_ref=land_ref.at[me, pl.ds(offs[a], rows_a[a]), :], send_sem=send_sems.at[k], recv_sem=recv_sems.at[k],
                    device_id=peer, device_id_type=pl.DeviceIdType.MESH).start()
        token[...] = jnp.zeros_like(token)

    landing = lax.empty((N_DEV, total, LANES), F32)
    out = pl.pallas_call(
        body, name=name,
        out_shape=(pltpu.SemaphoreType.DMA((N_DEV,)), pltpu.SemaphoreType.DMA((N_DEV,)), *[pltpu.HBM(a.shape, a.dtype) for a in srcs],
                   pltpu.HBM(landing.shape, F32), _sds((8, 128))),
        in_specs=[_HBM] * (ns + 1), out_specs=(_SEM, _SEM, *[_HBM] * (ns + 1), pl.BlockSpec(memory_space=pltpu.VMEM)),
        input_output_aliases={a: 2 + a for a in range(ns + 1)},
        compiler_params=pltpu.CompilerParams(has_side_effects=_EFFECT),
    )(*[_in_hbm(a) for a in srcs], _in_hbm(landing))
    return out[0], out[1], out[2:2 + ns], out[2 + ns], out[-1]


def _scatter_wait(name, send_sems, recv_sems, srcs, landing, after):
    ns = len(srcs)

    def body(*refs):
        land_ref, send_ref, recv_ref = refs[ns], refs[ns + 1], refs[ns + 2]
        x, y, c, me = _place()
        for k in range(1, N_DEV):
            peer, peer_slot = _peer(x, y, c, k)
            slot = pltpu.make_async_remote_copy(
                src_ref=land_ref.at[me], dst_ref=land_ref.at[peer_slot], send_sem=send_ref.at[k], recv_sem=recv_ref.at[k],
                device_id=peer, device_id_type=pl.DeviceIdType.MESH)
            slot.wait_send()
            slot.wait_recv()

    out = pl.pallas_call(
        body, name=name, out_shape=(*[pltpu.HBM(a.shape, a.dtype) for a in srcs], pltpu.HBM(landing.shape, landing.dtype)),
        in_specs=[_HBM] * (ns + 1) + [_SEM, _SEM, pl.BlockSpec(memory_space=pl.ANY)], out_specs=[_HBM] * (ns + 1),
        input_output_aliases={a: a for a in range(ns + 1)},
        compiler_params=pltpu.CompilerParams(has_side_effects=_EFFECT),
    )(*srcs, landing, send_sems, recv_sems, after)
    return out[:ns], out[ns]


def _gather_start(name, blocks):
    n = len(blocks)

    def body(*refs):
        src_refs, land_refs, send_sems, recv_sems, token = refs[:n], refs[n:2 * n], refs[2 * n], refs[2 * n + 1], refs[-1]
        x, y, c, me = _place()
        for a in range(n):
            r = src_refs[a].shape[1]
            for k in range(1, N_DEV):
                pltpu.make_async_remote_copy(
                    src_ref=src_refs[a], dst_ref=land_refs[a].at[:, pl.ds(me * r, r), :], send_sem=send_sems.at[a * N_DEV + k],
                    recv_sem=recv_sems.at[a * N_DEV + k], device_id=_peer(x, y, c, k)[0], device_id_type=pl.DeviceIdType.MESH).start()
        token[...] = jnp.zeros_like(token)

    wholes = [lax.empty((a.shape[0], N_DEV * a.shape[1], LANES), a.dtype) for a in blocks]
    out = pl.pallas_call(
        body, name=name,
        out_shape=(pltpu.SemaphoreType.DMA((n * N_DEV,)), pltpu.SemaphoreType.DMA((n * N_DEV,)),
                   *[pltpu.HBM(a.shape, a.dtype) for a in blocks + wholes], _sds((8, 128))),
        in_specs=[_HBM] * (2 * n), out_specs=(_SEM, _SEM, *[_HBM] * (2 * n), pl.BlockSpec(memory_space=pltpu.VMEM)),
        input_output_aliases={a: 2 + a for a in range(2 * n)},
        compiler_params=pltpu.CompilerParams(has_side_effects=_EFFECT),
    )(*[_in_hbm(a) for a in blocks + wholes])
    return out[0], out[1], out[2:2 + n], out[2 + n:2 + 2 * n], out[-1]


def _gather_wait(name, send_sems, recv_sems, blocks, wholes, after):
    n = len(blocks)

    def body(*refs):
        src_refs, land_refs, send_ref, recv_ref = refs[:n], refs[n:2 * n], refs[2 * n], refs[2 * n + 1]
        x, y, c, me = _place()
        for a in range(n):
            r = src_refs[a].shape[1]
            for k in range(1, N_DEV):
                peer, peer_slot = _peer(x, y, c, k)
                cp = pltpu.make_async_remote_copy(
                    src_ref=src_refs[a], dst_ref=land_refs[a].at[:, pl.ds(peer_slot * r, r), :], send_sem=send_ref.at[a * N_DEV + k],
                    recv_sem=recv_ref.at[a * N_DEV + k], device_id=peer, device_id_type=pl.DeviceIdType.MESH)
                cp.wait_send()
                cp.wait_recv()

    return pl.pallas_call(
        body, name=name, out_shape=tuple(pltpu.HBM(a.shape, a.dtype) for a in list(blocks) + list(wholes)),
        in_specs=[_HBM] * (2 * n) + [_SEM, _SEM, pl.BlockSpec(memory_space=pl.ANY)], out_specs=[_HBM] * (2 * n),
        input_output_aliases={a: a for a in range(2 * n)},
        compiler_params=pltpu.CompilerParams(has_side_effects=_EFFECT),
    )(*blocks, *wholes, send_sems, recv_sems, after)


def _reduce_adamw(parts, w, m, v, tile_rows=PACK_TILE):
    rows = w.shape[0]
    nt = rows // tile_rows
    slots = parts.shape[0]
    c1 = 1.0 - ADAM_B1 ** ADAM_STEP
    c2 = 1.0 - ADAM_B2 ** ADAM_STEP

    def body(p_ref, w_ref, m_ref, v_ref, g_out, d_out, m_out, v_out):
        g = p_ref[0]
        for s in range(1, slots):
            g = g + p_ref[s]
        m_new = ADAM_B1 * m_ref[...] + (1.0 - ADAM_B1) * g
        v_new = ADAM_B2 * v_ref[...] + (1.0 - ADAM_B2) * (g * g)
        g_out[...] = g
        m_out[...] = m_new
        v_out[...] = v_new
        d_out[...] = -ADAM_LR * ((m_new / c1) / (jnp.sqrt(v_new / c2) + ADAM_EPS) + ADAM_WD * w_ref[...])

    r = _rows_spec(tile_rows, LANES, nt)
    out = _sds((rows, LANES))
    return _call(
        body, "reduce_adamw", nt,
        [pl.BlockSpec((slots, tile_rows, LANES), lambda i: (0, i, 0)), r, r, r], [r, r, r, r], [out, out, out, out],
    )(parts, w, m, v)


def _pack_rows(a, lead=0):
    head = a.shape[:lead]
    flat = a.reshape(head + (-1,))
    size = flat.shape[-1]
    rows = -(-size // (16 * LANES)) * 16
    flat = jnp.pad(flat, [(0, 0)] * lead + [(0, rows * LANES - size)])
    return flat.reshape(head + (rows, LANES))


def _packed_rows(shape):
    return -(-math.prod(shape) // (16 * LANES)) * 16


def _to_blocks(full, axis):
    l, a, b = full.shape
    if axis == 2:
        return full.reshape(l, a, N_DEV, b // N_DEV).transpose(2, 0, 1, 3)
    return full.reshape(l, N_DEV, a // N_DEV, b).transpose(1, 0, 2, 3)


def _from_blocks(blocks, axis):
    _, l, a, b = blocks.shape
    if axis == 2:
        return blocks.transpose(1, 2, 0, 3).reshape(l, a, N_DEV * b)
    return blocks.transpose(1, 0, 2, 3).reshape(l, N_DEV * a, b)


def _row_form(shard, transposed):
    return shard.transpose(0, 2, 1) if transposed else shard


def _me():
    return 4 * lax.axis_index("x") + 2 * lax.axis_index("y") + lax.axis_index("c")


def _both_forms(names, wholes):
    out = {}
    for name, w in zip(names, wholes):
        t = dict(BIG)[name]
        out[name + '_t' if t else name] = w
        out[name if t else name + '_t'] = w.transpose(0, 2, 1)
    return out


def _gather_weights(local):
    segs, meta = [], []
    for name in SMALL_SHARDED:
        blk = local[name]
        if name in GATHER_F32:
            bits = lax.bitcast_convert_type(blk, MXU)
        else:
            bits = _mx(blk)
        seg = _pack_rows(bits)
        meta.append((name, bits.shape, seg.shape[0]))
        segs.append(seg)
    blocks = {name: _mx(_row_form(local[name], t)) for name, t in BIG}
    mix, ffn = PARTS['mix'], PARTS['ffn']
    *first, gathered = _all_gather("gather_weights", [blocks[n][:1] for n in mix], jnp.concatenate(segs, axis=0))
    flights = {'ffn0': _gather_start("gather_ffn0_start", [blocks[n][:1] for n in ffn]),
               'later': _gather_start("gather_later_start", [blocks[n][1:] for n in mix + ffn])}
    out, lo = {}, 0
    for name, bits_shape, rows in meta:
        seg = gathered[:, lo:lo + rows].reshape(N_DEV, -1)[:, :math.prod(bits_shape)].reshape((N_DEV,) + bits_shape)
        if name in GATHER_F32:
            seg = lax.bitcast_convert_type(seg, F32)
        out[name] = _from_blocks(seg, SHARD_AXIS[name])
        lo += rows
    ready = {(0, 'mix'): {k: v[0] for k, v in _both_forms(mix, first).items()}}

    def landed(flight, names, after):
        send_sems, recv_sems, mine, wholes, _ = flights[flight]
        done = _gather_wait(f"gather_{flight}_wait", send_sems, recv_sems, mine, wholes, after)
        own = [lax.dynamic_update_slice(w, b, (0, _me() * b.shape[1], 0)) for b, w in zip(done[:len(names)], done[len(names):])]
        return _both_forms(names, own)

    def big_weights(l, part, after):
        if (l, part) not in ready and l == 0:
            ready[(0, 'ffn')] = {k: v[0] for k, v in landed('ffn0', ffn, after).items()}
        elif (l, part) not in ready:
            forms = landed('later', mix + ffn, after)
            for j in range(1, DEPTH):
                for p, names in PARTS.items():
                    ready[(j, p)] = {k: forms[k][j - 1] for n in names for k in (n, n + '_t')}
        return ready[(l, part)]

    return out, big_weights, flights['ffn0'][-1][0, 0] + flights['later'][-1][0, 0]


def _s5_discretize(a_re, a_im, log_dt, b_re, b_im):
    lam_re = jnp.minimum(a_re, -1e-4)
    lam_im = a_im
    dt = jnp.exp(log_dt)[:, None]
    decay = jnp.exp(dt * lam_re)
    ang = dt * lam_im
    abar_re = decay * jnp.cos(ang)
    abar_im = decay * jnp.sin(ang)
    den = jnp.square(lam_re) + jnp.square(lam_im)
    nr = abar_re - 1.0
    ni = abar_im
    coef_re = (nr * lam_re + ni * lam_im) / den
    coef_im = (ni * lam_re - nr * lam_im) / den
    bbar_re = coef_re[..., None] * b_re - coef_im[..., None] * b_im
    bbar_im = coef_re[..., None] * b_im + coef_im[..., None] * b_re
    return abar_re, abar_im, bbar_re, bbar_im


def _complex_powers(ar, ai, count):
    def combine(e1, e2):
        return e2[0] * e1[0] - e2[1] * e1[1], e2[0] * e1[1] + e2[1] * e1[0]

    shape = (count,) + ar.shape
    return lax.associative_scan(combine, (jnp.broadcast_to(ar, shape), jnp.broadcast_to(ai, shape)), axis=0)


_EYE16 = functools.partial(jnp.eye, 16, dtype=F32)


def _s5_params(p, l):
    disc, disc_vjp = jax.vjp(_s5_discretize, p['s5_a_re'][l], p['s5_a_im'][l], p['s5_log_dt'][l], p['s5_b_re'][l], p['s5_b_im'][l])
    abar_re, abar_im, bbar_re, bbar_im = disc
    ar, ai = abar_re.reshape(N_STATE), abar_im.reshape(N_STATE)
    ap_re, ap_im = _complex_powers(ar, ai, TILE)
    one, zero = jnp.ones((1, N_STATE), F32), jnp.zeros((1, N_STATE), F32)
    seg_re = jnp.concatenate([one, ap_re[SEG - 1:TILE - SEG:SEG]], axis=0)
    seg_im = jnp.concatenate([zero, ap_im[SEG - 1:TILE - SEG:SEG]], axis=0)
    doubling = [SEG - 1, 2 * SEG - 1, 4 * SEG - 1]

    src = (jnp.arange(TILE) % N_SEG) * SEG + jnp.arange(TILE) // N_SEG
    perm = (src[:, None] == jnp.arange(TILE)[None, :]).astype(MXU)
    prm = {
        'perm': perm, 'perm_t': perm.T,
        'pw_re': ap_re[:SEG], 'pw_im': ap_im[:SEG],
        'dbl_re': jnp.concatenate([jnp.stack([ap_re[k] for k in doubling]), jnp.zeros((5, N_STATE), F32)], axis=0),
        'dbl_im': jnp.concatenate([jnp.stack([ap_im[k] for k in doubling]), jnp.zeros((5, N_STATE), F32)], axis=0),
        'seg_re': seg_re, 'seg_im': seg_im, 'segr_re': seg_re[::-1], 'segr_im': seg_im[::-1],
        'b_re': _mx(jnp.einsum('gpc,gh->gchp', bbar_re, _EYE16()).reshape(D_S5, N_STATE)),
        'b_im': _mx(jnp.einsum('gpc,gh->gchp', bbar_im, _EYE16()).reshape(D_S5, N_STATE)),
        'c_re': _mx(jnp.einsum('gcp,gh->gphc', p['s5_c_re'][l], _EYE16()).reshape(N_STATE, D_S5)),
        'c_im': _mx(jnp.einsum('gcp,gh->gphc', p['s5_c_im'][l], _EYE16()).reshape(N_STATE, D_S5)),
        'd': p['s5_d'][l][None, :], 'glu_w': p['s5_glu_w'][l], 'glu_b': p['s5_glu_b'][l][None, :],
    }
    return prm, disc_vjp


def _lru_params(p, l):
    eye4 = jnp.eye(4, dtype=F32)
    return {
        'conv_w': p['lru_conv_w'][l], 'conv_b': p['lru_conv_b'][l][None, :],
        'wx': _mx(jnp.einsum('hij,hk->hikj', p['lru_wx'][l], eye4).reshape(D_LRU, D_LRU)),
        'wa': _mx(jnp.einsum('hij,hk->hikj', p['lru_wa'][l], eye4).reshape(D_LRU, D_LRU)),
        'bx': p['lru_bx'][l][None, :], 'ba': p['lru_ba'][l][None, :],
        'sp': jax.nn.softplus(-p['lru_a_param'][l])[None, :],
    }


def _rope_tables(n):
    inv_freq = ROPE_THETA ** (-jnp.arange(0, 64, 2, dtype=F32) / 64)
    ang = jnp.arange(n, dtype=F32)[:, None] * inv_freq[None, :]
    cos, sin = jnp.cos(ang), jnp.sin(ang)
    return jnp.concatenate([cos, cos, cos, cos], axis=1), jnp.concatenate([-sin, sin, -sin, sin], axis=1)


def _sink_cols(sinks):
    nb = TILE_Q // ATTN_BLOCK
    per_unit = sinks.reshape(4, 2).T
    return jnp.broadcast_to(per_unit[:, None, :, None, None], (2, nb, 4, ATTN_BLOCK, 128)).reshape(2, nb * 4 * ATTN_BLOCK, 128)


def _local_step(x, target, p, big_weights=None, emit_grads=None):
    if big_weights is None:
        big_weights = lambda l, part, after: {k: p[k][l] for name in PARTS[part] for k in (name, name + '_t')}
    if emit_grads is None:
        emit_grads = lambda l, part, grads: 0.0
    n = x.shape[0]
    cos_t, sin_t = _rope_tables(n)
    row = lambda a: a[None, :]
    saved = []
    h = x
    for l in range(DEPTH):
        s = {'x0': h}
        bw = s['bw'] = dict(big_weights(l, 'mix', h))
        s['q'], k, v, s['u'], s['xr'], s['gate'] = _inproj_fwd(h, bw['w_in'], row(p['b_in'][l]), cos_t, sin_t)
        no_keys = jnp.zeros((ATTN_BLOCK, D_KV), MXU)
        s['k'], s['v'] = jnp.concatenate([no_keys, k], axis=0), jnp.concatenate([no_keys, v], axis=0)
        s['sinks'] = _sink_cols(p['attn_sinks'][l])
        s['ya'] = _attn_fwd(s['q'], s['k'], s['v'], s['sinks'])
        s['s5'], s['s5_vjp'] = _s5_params(p, l)
        s['lru'] = _lru_params(p, l)
        s['ys'], s['s5_cr'], s['s5_ci'] = _s5_fwd(s['u'], s['s5'])
        s['yl'], s['lru_c'] = _lru_fwd(s['xr'], s['gate'], s['lru'])
        s['mix'], s['r1'], s['x1'] = _mixout_fwd(s['ya'], s['ys'], s['yl'], h, row(p['mix_norm_g'][l]), bw['w_out'],
                                                 row(p['b_out'][l]), row(p['ln1_g'][l]), row(p['ln1_b'][l]))
        bw.update(big_weights(l, 'ffn', s['x1']))
        s['gp'], s['up'], s['r2'], h = _ffn_fwd(s['x1'], bw['ffn_w_gate'], bw['ffn_w_up'], p['ffn_conv_w'][l],
                                                row(p['ffn_conv_b'][l]), bw['ffn_w_down'], row(p['ln2_g'][l]), row(p['ln2_b'][l]))
        saved.append(s)
    loss, dh = _loss_head(h, target)
    placed = 0.0

    grads = {name: [None] * DEPTH for name in WEIGHTS}
    for l in reversed(range(DEPTH)):
        s = saved[l]
        g = {}
        (dr2, dgp, dup, g['ffn_w_down'], cw0, cw1, cw2, dcb, dg2, db2) = _ffn_bwd_down(
            dh, s['r2'], row(p['ln2_g'][l]) + placed, s['gp'], s['up'], p['ffn_conv_w'][l], row(p['ffn_conv_b'][l]),
            s['bw']['ffn_w_down_t'])
        g['ffn_conv_w'] = jnp.concatenate([cw0, cw1, cw2], axis=0)
        g['ffn_conv_b'], g['ln2_g'], g['ln2_b'] = dcb[0], dg2[0], db2[0]
        dr1, dg1, db1 = _ffn_bwd_dx(dr2, dgp, dup, s['r1'], row(p['ln1_g'][l]), s['bw']['ffn_w_gate_t'], s['bw']['ffn_w_up_t'])
        g['ffn_w_gate'], g['ffn_w_up'] = _ffn_bwd_dw(s['x1'], dgp, dup)
        g['ln1_g'], g['ln1_b'] = dg1[0], db1[0]
        placed = emit_grads(l, 'ffn', [g[name] for name in PARTS['ffn']])
        dya, dys, dyl, g['w_out'], dbo, dgm = _mixout_bwd(dr1, s['mix'], s['ya'], s['ys'], s['yl'],
                                                         row(p['mix_norm_g'][l]) + placed, s['bw']['w_out_t'])
        g['b_out'], g['mix_norm_g'] = dbo[0], dgm[0]

        du, dbr, dbi, dcr, dci, dar, dai, dd, g['s5_glu_w'], dgb = _s5_bwd(s['u'], dys, s['s5_cr'], s['s5_ci'], s['s5'])
        dxr, dgate, lw0, lw1, lw2, lw3, lcb, dwx, dwa, dbx, dba, dsp = _lru_bwd(s['xr'], s['gate'], dyl, s['lru_c'], s['lru'])
        g['lru_conv_w'] = jnp.concatenate([lw0, lw1, lw2, lw3], axis=0)
        g['lru_conv_b'], g['lru_bx'], g['lru_ba'] = lcb[0], dbx[0], dba[0]
        g['lru_wx'] = jnp.einsum('hihj->hij', dwx.reshape(4, 64, 4, 64))
        g['lru_wa'] = jnp.einsum('hihj->hij', dwa.reshape(4, 64, 4, 64))
        g['lru_a_param'] = -dsp[0] * jax.nn.sigmoid(-p['lru_a_param'][l])

        g['s5_c_re'] = jnp.einsum('gpgc->gcp', dcr.reshape(16, 64, 16, 16))
        g['s5_c_im'] = jnp.einsum('gpgc->gcp', dci.reshape(16, 64, 16, 16))
        g['s5_d'], g['s5_glu_b'] = dd[0], dgb[0]
        g['s5_a_re'], g['s5_a_im'], g['s5_log_dt'], g['s5_b_re'], g['s5_b_im'] = s['s5_vjp']((
            dar.reshape(16, 64), dai.reshape(16, 64), jnp.einsum('gcgp->gpc', dbr.reshape(16, 16, 16, 64)),
            jnp.einsum('gcgp->gpc', dbi.reshape(16, 16, 16, 64))))

        dq, dk, dv, dsink = _attn_bwd(s['q'], s['k'], s['v'], s['sinks'], s['ya'], dya)
        g['attn_sinks'] = dsink[:, 0]
        dh, g['w_in'], dbin = _inproj_bwd(dq, dk[ATTN_BLOCK:], dv[ATTN_BLOCK:], du, dxr, dgate, cos_t, sin_t, s['x0'], dr1,
                                          s['bw']['w_in_t'])
        g['b_in'] = dbin[0]
        placed = emit_grads(l, 'mix', [g[name] for name in PARTS['mix']])
        for name in WEIGHTS:
            grads[name][l] = g[name]
    big = dict(BIG)
    return loss, dh, {name: grads[name] if name in big else jnp.stack(grads[name]) for name in WEIGHTS}


def kernel(x, w_in, b_in, attn_sinks, s5_a_re, s5_a_im, s5_b_re, s5_b_im, s5_c_re, s5_c_im, s5_d, s5_log_dt, s5_glu_w, s5_glu_b, lru_conv_w, lru_conv_b, lru_wx, lru_bx, lru_wa, lru_ba, lru_a_param, mix_norm_g, w_out, b_out, ln1_g, ln1_b, ffn_w_gate, ffn_w_up, ffn_conv_w, ffn_conv_b, ffn_w_down, ln2_g, ln2_b, loss_target, m_w_in, m_b_in, m_attn_sinks, m_s5_a_re, m_s5_a_im, m_s5_b_re, m_s5_b_im, m_s5_c_re, m_s5_c_im, m_s5_d, m_s5_log_dt, m_s5_glu_w, m_s5_glu_b, m_lru_conv_w, m_lru_conv_b, m_lru_wx, m_lru_bx, m_lru_wa, m_lru_ba, m_lru_a_param, m_mix_norm_g, m_w_out, m_b_out, m_ln1_g, m_ln1_b, m_ffn_w_gate, m_ffn_w_up, m_ffn_conv_w, m_ffn_conv_b, m_ffn_w_down, m_ln2_g, m_ln2_b, v_w_in, v_b_in, v_attn_sinks, v_s5_a_re, v_s5_a_im, v_s5_b_re, v_s5_b_im, v_s5_c_re, v_s5_c_im, v_s5_d, v_s5_log_dt, v_s5_glu_w, v_s5_glu_b, v_lru_conv_w, v_lru_conv_b, v_lru_wx, v_lru_bx, v_lru_wa, v_lru_ba, v_lru_a_param, v_mix_norm_g, v_w_out, v_b_out, v_ln1_g, v_ln1_b, v_ffn_w_gate, v_ffn_w_up, v_ffn_conv_w, v_ffn_conv_b, v_ffn_w_down, v_ln2_g, v_ln2_b):
    given = dict(locals())
    whole = {name: given[name] for name in WEIGHTS if name not in dict(BIG)}
    small_whole, big_weights, placed = _gather_weights({name: given[name] for name in SHARDED})
    whole.update(small_whole)
    whole['b_in'] = whole['b_in'] + placed
    in_flight = {}

    def emit_grads(l, part, grads):
        in_flight[(l, part)] = _scatter_start(f"grads_start_{part}{l}", grads)
        return in_flight[(l, part)][-1][0, 0]

    loss, grad_x, grads = _local_step(x[0], loss_target[0], whole, big_weights, emit_grads)
    total = lax.psum(loss[0, 0], ("x", "y", "c"))
    return (total, grad_x[None], *_update(given, grads, in_flight, grad_x))


def _update(given, grads, in_flight, after):
    local_w = {name: given[name] for name in WEIGHTS}
    me = _me()
    outs = {}

    shard_rows = sum(_packed_rows(local_w[name].shape) for name in SMALL_SHARDED)
    rep_pad = -sum(_packed_rows(local_w[name].shape) for name in REPLICATED) % PACK_TILE

    def packed_rep(arrays):
        return jnp.concatenate([_pack_rows(arrays[name]) for name in REPLICATED] + [jnp.zeros((rep_pad, LANES), F32)], axis=0)

    rep_grads = packed_rep(grads)
    chunk = rep_grads.shape[0] // N_DEV
    small = jnp.concatenate(
        [_pack_rows(_to_blocks(grads[name], SHARD_AXIS[name]), lead=1) for name in SMALL_SHARDED]
        + [rep_grads.reshape(N_DEV, chunk, LANES)], axis=1)
    small_rows = shard_rows + chunk
    small_flight = _scatter_start("grads_start_small", [small.reshape(N_DEV * small_rows, LANES)])

    def summed(name, flight, rows_of, packed, after):
        send_sems, recv_sems, srcs, landing, _ = flight
        srcs, landing = _scatter_wait(name, send_sems, recv_sems, srcs, landing, after)
        own = jnp.concatenate([lax.dynamic_slice_in_dim(g, me * r, r, axis=0) for g, r in zip(srcs, rows_of)], axis=0)
        parts = lax.dynamic_update_slice(landing, own[None], (me, 0, 0))
        return _reduce_adamw(parts, *packed, tile_rows=own.shape[0] // 4)

    forms = {part: [(name, dict(BIG)[name], _row_form(local_w[name], dict(BIG)[name]).shape[1]) for name in names]
             for part, names in PARTS.items()}
    for (l, part), flight in in_flight.items():
        packed = [jnp.concatenate([_row_form(given[prefix + name], t)[l] for name, t, _ in forms[part]], axis=0)
                  for prefix in ('', 'm_', 'v_')]
        outs[(l, part)] = summed(f"grads_wait_{part}{l}", flight, [r for _, _, r in forms[part]], packed, after)
        after = outs[(l, part)][0]

    rep_state = [packed_rep({name: given[prefix + name] for name in REPLICATED}) for prefix in ('', 'm_', 'v_')]
    small_state = [jnp.concatenate([_pack_rows(given[prefix + name]) for name in SMALL_SHARDED]
                                   + [lax.dynamic_slice_in_dim(rep, me * chunk, chunk, axis=0)], axis=0)
                   for prefix, rep in zip(('', 'm_', 'v_'), rep_state)]
    small_outs = summed("grads_wait_small", small_flight, [small_rows], small_state, after)
    rep_sum = _all_gather("gather_small_grads", [], small_outs[0][shard_rows:])[0].reshape(N_DEV * chunk, LANES)
    rep_outs = _reduce_adamw(rep_sum[None], *rep_state, tile_rows=N_DEV * chunk // 4)

    def unpack(i):
        res = {}
        for part in PARTS:
            lo = 0
            for name, t, r in forms[part]:
                res[name] = _row_form(jnp.stack([outs[(l, part)][i][lo:lo + r] for l in range(DEPTH)]), t)
                lo += r
        for names, packed in ((SMALL_SHARDED, small_outs[i]), (REPLICATED, rep_outs[i])):
            lo = 0
            for name in names:
                shape = local_w[name].shape
                res[name] = packed[lo:lo + _packed_rows(shape)].reshape(-1)[:math.prod(shape)].reshape(shape)
                lo += _packed_rows(shape)
        return [res[name] for name in WEIGHTS]

    return (*unpack(0), *unpack(1), *unpack(2), *unpack(3))
```

```python
import functools
import math

import jax
import jax.numpy as jnp
from jax import lax
from jax.experimental import pallas as pl
from jax.experimental.pallas import tpu as pltpu

F32 = jnp.float32
MXU = jnp.bfloat16

N_DEV = 8
DEPTH = 4
D = 1024
D_ATTN, D_KV, D_S5, D_LRU = 512, 128, 256, 256
D_IN = 1536
D_FF = 2816
FF_CHUNK = 256
N_STATE = 1024
LANES = 1024
ALPHA = (2 * DEPTH) ** 0.25
LN_EPS = 1e-5
RMS_EPS = 1e-6
LRU_C = 8.0
ROPE_THETA = 10000.0
ADAM_LR, ADAM_B1, ADAM_B2, ADAM_EPS, ADAM_WD, ADAM_STEP = 0.001, 0.9, 0.999, 1e-08, 0.01, 10

TILE = 256
S5_PAIR = 2
N_SEG = 8
SEG = TILE // N_SEG
TILE_Q = 512
TILE_BIG = 512
TILE_WIDE = 512
ATTN_BLOCK = 128
PACK_TILE = 256
VMEM_MB = 56

WEIGHTS = ['w_in', 'b_in', 'attn_sinks', 's5_a_re', 's5_a_im', 's5_b_re', 's5_b_im', 's5_c_re', 's5_c_im', 's5_d', 's5_log_dt',
           's5_glu_w', 's5_glu_b', 'lru_conv_w', 'lru_conv_b', 'lru_wx', 'lru_bx', 'lru_wa', 'lru_ba', 'lru_a_param', 'mix_norm_g',
           'w_out', 'b_out', 'ln1_g', 'ln1_b', 'ffn_w_gate', 'ffn_w_up', 'ffn_conv_w', 'ffn_conv_b', 'ffn_w_down', 'ln2_g', 'ln2_b']
SHARD_AXIS = {'w_in': 2, 's5_glu_w': 1, 'lru_conv_w': 2, 'w_out': 1, 'ffn_w_gate': 2, 'ffn_w_up': 2, 'ffn_conv_w': 2,
              'ffn_w_down': 1}
SHARDED = [n for n in WEIGHTS if n in SHARD_AXIS]
REPLICATED = [n for n in WEIGHTS if n not in SHARD_AXIS]
BIG = [(n, SHARD_AXIS[n] == 2) for n in ('w_in', 'w_out', 'ffn_w_gate', 'ffn_w_up', 'ffn_w_down')]
SMALL_SHARDED = [n for n in SHARDED if n not in dict(BIG)]
PARTS = {'mix': ['w_in', 'w_out'], 'ffn': ['ffn_w_gate', 'ffn_w_up', 'ffn_w_down']}
GATHER_F32 = ('lru_conv_w', 'ffn_conv_w')


def _dot(a, b):
    return jnp.dot(a, b, preferred_element_type=F32)


def _dot_nt(a, b):
    return lax.dot_general(a, b, (((1,), (1,)), ((), ())), preferred_element_type=F32)


def _dot_tn(a, b):
    return lax.dot_general(a, b, (((0,), (0,)), ((), ())), preferred_element_type=F32)


def _mx(a):
    return a.astype(MXU)


_GELU_C = math.sqrt(2.0 / math.pi)


def _gelu(x):
    th = jnp.tanh(_GELU_C * (x + 0.044715 * x * x * x))
    return 0.5 * x * (1.0 + th)


def _gelu_grad(x):
    th = jnp.tanh(_GELU_C * (x + 0.044715 * x * x * x))
    return 0.5 * (1.0 + th) + 0.5 * x * (1.0 - th * th) * _GELU_C * (1.0 + 3.0 * 0.044715 * x * x)


def _sigmoid(x):
    return 0.5 * jnp.tanh(0.5 * x) + 0.5


def _ln_stats(r):
    mu = jnp.mean(r, axis=-1, keepdims=True)
    xc = r - mu
    var = jnp.mean(xc * xc, axis=-1, keepdims=True)
    rstd = lax.rsqrt(var + LN_EPS)
    return xc * rstd, rstd


def _ln_bwd(dy, g, xhat, rstd):
    dxh = dy * g
    return rstd * (dxh - jnp.mean(dxh, axis=-1, keepdims=True) - xhat * jnp.mean(dxh * xhat, axis=-1, keepdims=True))


def _rms(y):
    return lax.rsqrt(jnp.mean(y * y, axis=-1, keepdims=True) + RMS_EPS)


def _sum0(a):
    return jnp.sum(a, axis=0, keepdims=True)


def _row_iota(shape):
    return lax.broadcasted_iota(jnp.int32, shape, 0)


def _shift_down(ext, j, rows):
    return pltpu.roll(ext, j, 0)[8:8 + rows]


def _shift_up(ext, j, rows):
    return pltpu.roll(ext, ext.shape[0] - j, 0)[:rows]


def _swap_halves(t):
    w = t.shape[1]
    lane = lax.broadcasted_iota(jnp.int32, t.shape, 1)
    return jnp.where((lane & 32) == 0, pltpu.roll(t, w - 32, 1), pltpu.roll(t, 32, 1))


def _rope(t, cos, sin_signed):
    return t * cos + _swap_halves(t) * sin_signed


def _rope_t(d, cos, sin_signed):
    return d * cos + _swap_halves(d * sin_signed)


def _cmul_add(ar, ai, xr, xi, yr, yi):
    return ar * xr - ai * xi + yr, ar * xi + ai * xr + yi


def _seg_rows(k):
    return slice(N_SEG * k, N_SEG * (k + 1))


def _permute_rows(perm, x):
    hi = _mx(x)
    rest = x - hi.astype(F32)
    mid = _mx(rest)
    lo = _mx(rest - mid.astype(F32))
    return _dot(perm, hi) + _dot(perm, mid) + _dot(perm, lo)


def _cscan(sr, si, tab, cin_r, cin_i, reverse):
    sgn = -1.0 if reverse else 1.0
    pw_re, pw_im, dbl_re, dbl_im = tab['pw_re'], tab['pw_im'], tab['dbl_re'], tab['dbl_im']
    ar, ai = pw_re[0:1, :], sgn * pw_im[0:1, :]
    shape = (N_SEG, sr.shape[1])
    hr = hi = jnp.zeros(shape, F32)
    for k in (range(SEG - 1, -1, -1) if reverse else range(SEG)):
        hr, hi = _cmul_add(ar, ai, hr, hi, sr[_seg_rows(k), :], si[_seg_rows(k), :])
        sr[_seg_rows(k), :] = hr
        si[_seg_rows(k), :] = hi
    sub = _row_iota(shape)

    def shifted(v, d):
        if reverse:
            return jnp.where(sub < N_SEG - d, pltpu.roll(v, N_SEG - d, 0), 0.0)
        return jnp.where(sub >= d, pltpu.roll(v, d, 0), 0.0)

    fr, fi = hr, hi
    for j, d in enumerate((1, 2, 4)):
        fr, fi = _cmul_add(dbl_re[j:j + 1, :], sgn * dbl_im[j:j + 1, :], shifted(fr, d), shifted(fi, d), fr, fi)
    seg_re, seg_im = (tab['segr_re'], tab['segr_im']) if reverse else (tab['seg_re'], tab['seg_im'])
    cr, ci = _cmul_add(seg_re[...], sgn * seg_im[...], cin_r, cin_i, shifted(fr, 1), shifted(fi, 1))
    nr, ni = _cmul_add(dbl_re[0:1, :], sgn * dbl_im[0:1, :], cr, ci, hr, hi)
    for k in range(SEG):
        j = SEG - 1 - k if reverse else k
        xr, xi = _cmul_add(pw_re[j:j + 1, :], sgn * pw_im[j:j + 1, :], cr, ci, sr[_seg_rows(k), :], si[_seg_rows(k), :])
        sr[_seg_rows(k), :] = xr
        si[_seg_rows(k), :] = xi
    edge = slice(0, 1) if reverse else slice(N_SEG - 1, N_SEG)
    return nr[edge], ni[edge]


def _rscan(a, b, reverse):
    rows = a.shape[0]
    row = _row_iota(a.shape)
    s = 1
    while s < rows:
        if reverse:
            keep = row < rows - s
            sa = jnp.where(keep, pltpu.roll(a, rows - s, 0), 1.0)
            sb = jnp.where(keep, pltpu.roll(b, rows - s, 0), 0.0)
        else:
            keep = row >= s
            sa = jnp.where(keep, pltpu.roll(a, s, 0), 1.0)
            sb = jnp.where(keep, pltpu.roll(b, s, 0), 0.0)
        b = b + a * sb
        a = a * sa
        s *= 2
    return a, b


def _whole():
    return pl.BlockSpec(memory_space=pltpu.VMEM)


def _rows_spec(rows, cols, n_tiles, reverse=False):
    if reverse:
        return pl.BlockSpec((rows, cols), lambda i: (n_tiles - 1 - i, 0))
    return pl.BlockSpec((rows, cols), lambda i: (i, 0))


def _halo_spec(cols, tile_rows, n_tiles, reverse=False):
    per = tile_rows // 8
    if reverse:
        return pl.BlockSpec((8, cols), lambda i: (jnp.maximum((n_tiles - 1 - i) * per - 1, 0), 0))
    return pl.BlockSpec((8, cols), lambda i: (jnp.maximum(i * per - 1, 0), 0))


def _call(body, name, n_tiles, in_specs, out_specs, out_shape, scratch=()):
    return pl.pallas_call(
        body, name=name, grid=(n_tiles,), in_specs=in_specs, out_specs=out_specs, out_shape=out_shape,
        scratch_shapes=list(scratch),
        compiler_params=pltpu.CompilerParams(dimension_semantics=("arbitrary",), vmem_limit_bytes=VMEM_MB << 20))


def _sds(shape, dtype=F32):
    return jax.ShapeDtypeStruct(shape, dtype)


def _inproj_fwd(x, w, b, cos_t, sin_t):
    n = x.shape[0]
    nt = n // TILE_WIDE

    def body(x_ref, w_ref, b_ref, c_ref, s_ref, q_ref, k_ref, v_ref, u_ref, xr_ref, g_ref):
        p = _dot(_mx(x_ref[...]), w_ref[...]) + b_ref[...]
        cos, sin = c_ref[...], s_ref[...]
        q_ref[...] = _mx(_rope(p[:, :D_ATTN], jnp.tile(cos, (1, 4)), jnp.tile(sin, (1, 4))))
        k_ref[...] = _mx(_rope(p[:, 512:640], cos, sin))
        v_ref[...] = _mx(p[:, 640:768])
        u_ref[...] = p[:, 768:1024]
        xr_ref[...] = p[:, 1024:1280]
        g_ref[...] = p[:, 1280:1536]

    r = functools.partial(_rows_spec, n_tiles=nt)
    return _call(
        body, "inproj_fwd", nt,
        [r(TILE_WIDE, D), _whole(), _whole(), r(TILE_WIDE, 128), r(TILE_WIDE, 128)],
        [r(TILE_WIDE, D_ATTN), r(TILE_WIDE, D_KV), r(TILE_WIDE, D_KV), r(TILE_WIDE, D_S5), r(TILE_WIDE, D_LRU), r(TILE_WIDE, D_LRU)],
        [_sds((n, D_ATTN), MXU), _sds((n, D_KV), MXU), _sds((n, D_KV), MXU), _sds((n, D_S5)), _sds((n, D_LRU)), _sds((n, D_LRU))],
    )(x, w, b, cos_t, sin_t)


def _inproj_bwd(dq, dk, dv, du, dxr, dgate, cos_t, sin_t, x0, dr1, w_t):
    n = x0.shape[0]
    nt = n // TILE_WIDE

    def body(dq_ref, dk_ref, dv_ref, du_ref, dxr_ref, dg_ref, c_ref, s_ref, x_ref, dr_ref, w_ref, dx_ref, dw_ref, db_ref):
        @pl.when(pl.program_id(0) == 0)
        def _():
            dw_ref[...] = jnp.zeros_like(dw_ref)
            db_ref[...] = jnp.zeros_like(db_ref)

        cos, sin = c_ref[...], s_ref[...]
        dtq = _rope_t(dq_ref[...], jnp.tile(cos, (1, 4)), jnp.tile(sin, (1, 4)))
        dtk = _rope_t(dk_ref[...], cos, sin)
        dp = jnp.concatenate([dtq, dtk, dv_ref[...], du_ref[...], dxr_ref[...], dg_ref[...]], axis=1)
        db_ref[...] += _sum0(dp)
        dpb = _mx(dp)
        dw_ref[...] += _dot_tn(dpb, _mx(x_ref[...]))
        dx_ref[...] = ALPHA * dr_ref[...] + _dot(dpb, w_ref[...])

    r = functools.partial(_rows_spec, n_tiles=nt)
    return _call(
        body, "inproj_bwd", nt,
        [r(TILE_WIDE, D_ATTN), r(TILE_WIDE, D_KV), r(TILE_WIDE, D_KV), r(TILE_WIDE, D_S5), r(TILE_WIDE, D_LRU), r(TILE_WIDE, D_LRU),
         r(TILE_WIDE, 128), r(TILE_WIDE, 128), r(TILE_WIDE, D), r(TILE_WIDE, D), _whole()],
        [r(TILE_WIDE, D), _whole(), _whole()],
        [_sds((n, D)), _sds((D_IN, D)), _sds((1, D_IN))],
    )(dq, dk, dv, du, dxr, dgate, cos_t, sin_t, x0, dr1, w_t)


def _kv_variants(t, lo):
    tr = pltpu.roll(t, 64, 1)
    out = []
    for j in range(2):
        first = jnp.where(lo, t if j == 0 else tr, 0.0)
        second = jnp.where(lo, 0.0, tr if j == 0 else t)
        out.append(_mx(jnp.concatenate([first, second], axis=0)))
    return out


def _kv_collect(x0, x1, lo):
    a = x0[:256] + pltpu.roll(x0[256:], 64, 1)
    b = pltpu.roll(x1[:256], 64, 1) + x1[256:]
    return jnp.where(lo, a, b)


def _row_sums(x):
    ones = jnp.ones((128, 128), MXU)
    hi = _mx(x)
    lo = _mx(x - hi.astype(F32))
    return _dot(hi, ones) + _dot(lo, ones)


def _attn_probs(s, sink_ref):
    out = []
    for hp in range(2):
        sh = s[:, hp * 128:(hp + 1) * 128]
        sink = sink_ref[hp]
        m = jnp.maximum(jnp.broadcast_to(jnp.max(sh, axis=1, keepdims=True), sh.shape), sink)
        p = jnp.exp(sh - m)
        es = jnp.exp(sink - m)
        inv = 1.0 / (_row_sums(p) + es)
        out.append((p * inv, es * inv))
    return out


def _band_merge(x, tri, no_previous=None):
    bands = []
    for hp in range(2):
        prev, own = x[:, hp * 256:hp * 256 + 128], x[:, hp * 256 + 128:hp * 256 + 256]
        if no_previous is not None:
            prev = jnp.where(no_previous, -jnp.inf, prev)
        bands.append(jnp.where(tri, own, prev))
    return jnp.concatenate(bands, axis=1)


def _band_split(y, tri):
    parts = []
    for hp in range(2):
        band = y[:, hp * 128:(hp + 1) * 128]
        parts += [jnp.where(tri, 0.0, band), jnp.where(tri, band, 0.0)]
    return jnp.concatenate(parts, axis=1)


def _tri():
    shape = (ATTN_BLOCK, ATTN_BLOCK)
    return lax.broadcasted_iota(jnp.int32, shape, 0) >= lax.broadcasted_iota(jnp.int32, shape, 1)


def _attn_scores(q_ref, k_ref, v_ref, nb, tri):
    lo = lax.broadcasted_iota(jnp.int32, (256, 128), 1) < 64
    kcats, vcats, kstarts, parts = [], [], [], []
    for b in range(nb):
        block = pl.program_id(0) * nb + b
        kstart = pl.multiple_of(block * ATTN_BLOCK, ATTN_BLOCK)
        kcat = _kv_variants(k_ref[pl.ds(kstart, 256), :].astype(F32), lo)
        kcats.append(kcat)
        vcats.append(_kv_variants(v_ref[pl.ds(kstart, 256), :].astype(F32), lo))
        kstarts.append(kstart)
        for j in range(2):
            s = _dot_nt(_kv_group(q_ref, b, j), kcat[j]) * 0.125
            parts += [_band_merge(s[:ATTN_BLOCK], tri, block == 0), _band_merge(s[ATTN_BLOCK:], tri, block == 0)]
    return jnp.concatenate(parts, axis=0), kcats, vcats, kstarts


def _kv_group(a, b, j):
    rows = slice(b * ATTN_BLOCK, (b + 1) * ATTN_BLOCK)
    return jnp.concatenate([a[rows, 2 * j * 128:(2 * j + 1) * 128], a[rows, (2 * j + 1) * 128:(2 * j + 2) * 128]], axis=0)


def _put_kv_group(ref, b, j, x):
    rows = slice(b * ATTN_BLOCK, (b + 1) * ATTN_BLOCK)
    ref[rows, 2 * j * 128:(2 * j + 1) * 128] = x[:ATTN_BLOCK]
    ref[rows, (2 * j + 1) * 128:(2 * j + 2) * 128] = x[ATTN_BLOCK:]


def _band_split_group(y, unit, tri):
    return _mx(jnp.concatenate([_band_split(y[unit:unit + ATTN_BLOCK], tri),
                                _band_split(y[unit + ATTN_BLOCK:unit + 2 * ATTN_BLOCK], tri)], axis=0))


def _attn_fwd(q, k, v, sink_cols):
    n = q.shape[0]
    nt = n // TILE_Q
    nb = TILE_Q // ATTN_BLOCK

    def body(q_ref, k_ref, v_ref, s_ref, o_ref):
        tri = _tri()
        s, _, vcats, _ = _attn_scores(q_ref, k_ref, v_ref, nb, tri)
        (p0, _), (p1, _) = _attn_probs(s, s_ref)
        p = jnp.concatenate([p0, p1], axis=1)
        for b in range(nb):
            for j in range(2):
                _put_kv_group(o_ref, b, j, _dot(_band_split_group(p, (b * 4 + 2 * j) * ATTN_BLOCK, tri), vcats[b][j]))

    return _call(
        body, "attn_fwd", nt,
        [_rows_spec(TILE_Q, D_ATTN, nt), _whole(), _whole(), _whole()],
        _rows_spec(TILE_Q, D_ATTN, nt), _sds((n, D_ATTN)),
    )(q, k, v, sink_cols)


def _attn_bwd(q, k, v, sink_cols, o, do):
    n = q.shape[0]
    nt = n // TILE_Q
    nb = TILE_Q // ATTN_BLOCK

    def body(q_ref, k_ref, v_ref, s_ref, o_ref, do_ref, dq_ref, dk_ref, dv_ref, ds_ref):
        @pl.when(pl.program_id(0) == 0)
        def _():
            dk_ref[...] = jnp.zeros_like(dk_ref)
            dv_ref[...] = jnp.zeros_like(dv_ref)
            ds_ref[...] = jnp.zeros_like(ds_ref)

        lo = lax.broadcasted_iota(jnp.int32, (256, 128), 1) < 64
        tri = _tri()
        s, kcats, vcats, kstarts = _attn_scores(q_ref, k_ref, v_ref, nb, tri)
        probs = _attn_probs(s, s_ref)
        do = do_ref[...]
        dob = _mx(do)
        od = do * o_ref[...]
        lo_q = (lax.broadcasted_iota(jnp.int32, od.shape, 1) & 64) == 0
        od_head = (jnp.where(lo_q, od, 0.0), jnp.where(lo_q, 0.0, od))
        units = [(b, i) for b in range(nb) for i in range(4)]

        def tile_part(a, b, i):
            return a[b * ATTN_BLOCK:(b + 1) * ATTN_BLOCK, i * 128:(i + 1) * 128]

        dp = []
        for b in range(nb):
            for j in range(2):
                x = _dot_nt(_kv_group(dob, b, j), vcats[b][j])
                dp += [_band_merge(x[:ATTN_BLOCK], tri), _band_merge(x[ATTN_BLOCK:], tri)]
        dp = jnp.concatenate(dp, axis=0)
        ds = []
        for hp in range(2):
            p, p_sink = probs[hp]
            delta = _row_sums(jnp.concatenate([tile_part(od_head[hp], b, i) for b, i in units], axis=0))
            ds.append(p * (dp[:, hp * 128:(hp + 1) * 128] - delta) * 0.125)
            t = p_sink * delta
            for i in range(4):
                ds_ref[2 * i + hp:2 * i + hp + 1, :] -= sum(
                    _sum0(t[(b * 4 + i) * ATTN_BLOCK:(b * 4 + i + 1) * ATTN_BLOCK]) for b in range(nb))
        ds = jnp.concatenate(ds, axis=1)
        p = jnp.concatenate([probs[0][0], probs[1][0]], axis=1)
        for b in range(nb):
            dkc, dvc = [], []
            for j in range(2):
                unit = (b * 4 + 2 * j) * ATTN_BLOCK
                dsb = _band_split_group(ds, unit, tri)
                _put_kv_group(dq_ref, b, j, _dot(dsb, kcats[b][j]))
                dkc.append(_dot_tn(dsb, _kv_group(q_ref, b, j)))
                dvc.append(_dot_tn(_band_split_group(p, unit, tri), _kv_group(dob, b, j)))
            dk_ref[pl.ds(kstarts[b], 256), :] += _kv_collect(dkc[0], dkc[1], lo)
            dv_ref[pl.ds(kstarts[b], 256), :] += _kv_collect(dvc[0], dvc[1], lo)

    r = _rows_spec(TILE_Q, D_ATTN, nt)
    return _call(
        body, "attn_bwd", nt,
        [r, _whole(), _whole(), _whole(), r, r],
        [r, _whole(), _whole(), _whole()],
        [_sds((n, D_ATTN)), _sds((n + ATTN_BLOCK, D_KV)), _sds((n + ATTN_BLOCK, D_KV)), _sds((8, 128))],
    )(q, k, v, sink_cols, o, do)


S5_DISC = ('s5_a_re', 's5_a_im', 's5_log_dt', 's5_b_re', 's5_b_im')
S5_TABLES = ('pw_re', 'pw_im', 'dbl_re', 'dbl_im', 'seg_re', 'seg_im', 'segr_re', 'segr_im')
S5_WEIGHTS = ('b_re', 'b_im', 'c_re', 'c_im', 'd', 'glu_w', 'glu_b', 'perm', 'perm_t')


def _s5_states(u, carry_r, carry_i, b_re, b_im, tab, hr_s, hi_s):
    ub = _mx(u)
    hr_s[...] = _dot(ub, b_re[...])
    hi_s[...] = _dot(ub, b_im[...])
    return ub, _cscan(hr_s, hi_s, tab, carry_r, carry_i, reverse=False)


def _s5_fwd(u, prm):
    n = u.shape[0]
    nt = n // (S5_PAIR * TILE)

    def body(u_ref, *refs):
        tab = dict(zip(S5_TABLES, refs[:8]))
        b_re, b_im, c_re, c_im, d_ref, gw_ref, gb_ref, perm, perm_t = refs[8:17]
        y_ref, cr_out, ci_out, cr_s, ci_s = refs[17:22]
        states = [refs[22 + 2 * j:24 + 2 * j] for j in range(S5_PAIR)]
        rows = [slice(j * TILE, (j + 1) * TILE) for j in range(S5_PAIR)]

        @pl.when(pl.program_id(0) == 0)
        def _():
            cr_s[...] = jnp.zeros_like(cr_s)
            ci_s[...] = jnp.zeros_like(ci_s)

        us = []
        for j in range(S5_PAIR):
            us.append(_permute_rows(perm[...], u_ref[rows[j], :]))
            ub = _mx(us[j])
            states[j][0][...] = _dot(ub, b_re[...])
            states[j][1][...] = _dot(ub, b_im[...])
        for j in range(S5_PAIR):
            cr, ci = cr_s[...], ci_s[...]
            cr_out[8 * j:8 * j + 8, :] = jnp.broadcast_to(cr, (8, N_STATE))
            ci_out[8 * j:8 * j + 8, :] = jnp.broadcast_to(ci, (8, N_STATE))
            cr_s[...], ci_s[...] = _cscan(*states[j], tab, cr, ci, reverse=False)
        for j in range(S5_PAIR):
            hr_s, hi_s = states[j]
            y = _dot(_mx(hr_s[...]), c_re[...]) - _dot(_mx(hi_s[...]), c_im[...]) + d_ref[...] * us[j]
            z = _gelu(y)
            y_ref[rows[j], :] = _permute_rows(perm_t[...], z * _sigmoid(_dot(_mx(z), gw_ref[...]) + gb_ref[...]))

    r = functools.partial(_rows_spec, n_tiles=nt)
    return _call(
        body, "s5_fwd", nt,
        [r(S5_PAIR * TILE, D_S5)] + [_whole()] * 17,
        [r(S5_PAIR * TILE, D_S5), r(S5_PAIR * 8, N_STATE), r(S5_PAIR * 8, N_STATE)],
        [_sds((n, D_S5)), _sds((n // TILE * 8, N_STATE)), _sds((n // TILE * 8, N_STATE))],
        scratch=[pltpu.VMEM((1, N_STATE), F32)] * 2 + [pltpu.VMEM((TILE, N_STATE), F32)] * (2 * S5_PAIR),
    )(u, *[prm[k] for k in S5_TABLES + S5_WEIGHTS])


def _s5_bwd(u, dys, carry_re, carry_im, prm):
    n = u.shape[0]
    nt = n // (S5_PAIR * TILE)

    def body(u_ref, dy_ref, cin_r, cin_i, *refs):
        tab = dict(zip(S5_TABLES, refs[:8]))
        b_re, b_im, c_re, c_im, d_ref, gw_ref, gb_ref, perm, perm_t = refs[8:17]
        du_ref, dbr_ref, dbi_ref, dcr_ref, dci_ref, dar_ref, dai_ref, dd_ref, dgw_ref, dgb_ref = refs[17:27]
        gr_s, gi_s = refs[27:29]
        scratch = [refs[29 + 4 * j:33 + 4 * j] for j in range(S5_PAIR)]
        later_first = list(reversed(range(S5_PAIR)))
        rows = [slice(j * TILE, (j + 1) * TILE) for j in range(S5_PAIR)]

        @pl.when(pl.program_id(0) == 0)
        def _():
            for ref in (dbr_ref, dbi_ref, dcr_ref, dci_ref, dar_ref, dai_ref, dd_ref, dgw_ref, dgb_ref, gr_s, gi_s):
                ref[...] = jnp.zeros_like(ref)

        us, ubs, carries, dy_of = {}, {}, {}, {}
        for j in later_first:
            us[j] = _permute_rows(perm[...], u_ref[rows[j], :])
            carries[j] = (cin_r[8 * j:8 * j + 1, :], cin_i[8 * j:8 * j + 1, :])
            ubs[j], _ = _s5_states(us[j], *carries[j], b_re, b_im, tab, *scratch[j][:2])
        for j in later_first:
            u = us[j]
            hr_s, hi_s, gr_t, gi_t = scratch[j]
            hrb, hib = _mx(hr_s[...]), _mx(hi_s[...])
            y = _dot(hrb, c_re[...]) - _dot(hib, c_im[...]) + d_ref[...] * u
            z = _gelu(y)
            zb = _mx(z)
            sg = _sigmoid(_dot(zb, gw_ref[...]) + gb_ref[...])
            dout = _permute_rows(perm[...], dy_ref[rows[j], :])
            dpre = dout * z * sg * (1.0 - sg)
            dgb_ref[...] += _sum0(dpre)
            dpb = _mx(dpre)
            dgw_ref[...] += _dot_tn(zb, dpb)
            dy = (dout * sg + _dot_nt(dpb, gw_ref[...])) * _gelu_grad(y)
            dd_ref[...] += _sum0(dy * u)
            dyb = _mx(dy)
            dcr_ref[...] += _dot_tn(hrb, dyb)
            dci_ref[...] -= _dot_tn(hib, dyb)
            gr_t[...] = _dot_nt(dyb, c_re[...])
            gi_t[...] = -_dot_nt(dyb, c_im[...])
            dy_of[j] = dy
        for j in later_first:
            gr_s[...], gi_s[...] = _cscan(*scratch[j][2:], tab, gr_s[...], gi_s[...], reverse=True)
        for j in later_first:
            hr_s, hi_s, gr_t, gi_t = scratch[j]
            cr, ci = carries[j]
            sub = _row_iota((N_SEG, N_STATE))
            acc_r = acc_i = jnp.zeros((N_SEG, N_STATE), F32)
            for k in range(SEG):
                if k == 0:
                    hpr = jnp.where(sub >= 1, pltpu.roll(hr_s[_seg_rows(SEG - 1), :], 1, 0), cr)
                    hpi = jnp.where(sub >= 1, pltpu.roll(hi_s[_seg_rows(SEG - 1), :], 1, 0), ci)
                else:
                    hpr, hpi = hr_s[_seg_rows(k - 1), :], hi_s[_seg_rows(k - 1), :]
                gr, gi = gr_t[_seg_rows(k), :], gi_t[_seg_rows(k), :]
                acc_r = acc_r + gr * hpr + gi * hpi
                acc_i = acc_i + gi * hpr - gr * hpi
            dar_ref[...] += _sum0(acc_r)
            dai_ref[...] += _sum0(acc_i)
            grb, gib = _mx(gr_t[...]), _mx(gi_t[...])
            dbr_ref[...] += _dot_tn(ubs[j], grb)
            dbi_ref[...] += _dot_tn(ubs[j], gib)
            du_ref[rows[j], :] = _permute_rows(perm_t[...], dy_of[j] * d_ref[...] + _dot_nt(grb, b_re[...]) + _dot_nt(gib, b_im[...]))

    r = functools.partial(_rows_spec, n_tiles=nt, reverse=True)
    return _call(
        body, "s5_bwd", nt,
        [r(S5_PAIR * TILE, D_S5), r(S5_PAIR * TILE, D_S5), r(S5_PAIR * 8, N_STATE), r(S5_PAIR * 8, N_STATE)] + [_whole()] * 17,
        [r(S5_PAIR * TILE, D_S5)] + [_whole()] * 9,
        [_sds((n, D_S5)), _sds((D_S5, N_STATE)), _sds((D_S5, N_STATE)), _sds((N_STATE, D_S5)), _sds((N_STATE, D_S5)),
         _sds((1, N_STATE)), _sds((1, N_STATE)), _sds((1, D_S5)), _sds((D_S5, D_S5)), _sds((1, D_S5))],
        scratch=[pltpu.VMEM((1, N_STATE), F32)] * 2 + [pltpu.VMEM((TILE, N_STATE), F32)] * (4 * S5_PAIR),
    )(u, dys, carry_re, carry_im, *[prm[k] for k in S5_TABLES + S5_WEIGHTS])


def _lru_gates(xr, halo, tile_index, cw_ref, cb_ref, wx_ref, wa_ref, bx_ref, ba_ref, sp_ref):
    ext = jnp.concatenate([halo, xr], axis=0)
    sh = [xr] + [_shift_down(ext, j, TILE) for j in (1, 2, 3)]
    xc = cb_ref[...] + cw_ref[3:4, :] * sh[0] + cw_ref[2:3, :] * sh[1] + cw_ref[1:2, :] * sh[2] + cw_ref[0:1, :] * sh[3]
    xb = _mx(xc)
    gx = _sigmoid(_dot(xb, wx_ref[...]) + bx_ref[...])
    ga = _sigmoid(_dot(xb, wa_ref[...]) + ba_ref[...])
    la = -LRU_C * ga * sp_ref[...]
    a = jnp.exp(la)
    start = (tile_index * TILE + _row_iota(xr.shape)) == 0
    mult = jnp.where(start, 1.0, jnp.sqrt(-jnp.tanh(la) * (a * a + 1.0)))
    return sh, xc, xb, gx, ga, a, mult, start


def _lru_fwd(xr, gate, prm):
    n = xr.shape[0]
    nt = n // TILE

    def body(x_ref, g_ref, cw_ref, cb_ref, wx_ref, wa_ref, bx_ref, ba_ref, sp_ref, y_ref, c_out, halo_s, c_s):
        first_tile = pl.program_id(0) == 0

        @pl.when(first_tile)
        def _():
            halo_s[...] = jnp.zeros_like(halo_s)
            c_s[...] = jnp.zeros_like(c_s)

        xr = x_ref[...]
        _, xc, _, gx, _, a, mult, _ = _lru_gates(xr, halo_s[...], pl.program_id(0), cw_ref, cb_ref, wx_ref, wa_ref, bx_ref, ba_ref,
                                                 sp_ref)
        halo_s[...] = xr[TILE - 8:]
        acum, h = _rscan(a, mult * gx * xc, reverse=False)
        c = c_s[...]
        c_out[...] = jnp.broadcast_to(c, (8, D_LRU))
        h = h + acum * c
        c_s[...] = h[TILE - 1:TILE]
        y_ref[...] = h * _gelu(g_ref[...])

    r = functools.partial(_rows_spec, n_tiles=nt)
    return _call(
        body, "lru_fwd", nt,
        [r(TILE, D_LRU), r(TILE, D_LRU)] + [_whole()] * 7,
        [r(TILE, D_LRU), r(8, D_LRU)],
        [_sds((n, D_LRU)), _sds((nt * 8, D_LRU))],
        scratch=[pltpu.VMEM((8, D_LRU), F32), pltpu.VMEM((1, D_LRU), F32)],
    )(xr, gate, prm['conv_w'], prm['conv_b'], prm['wx'], prm['wa'], prm['bx'], prm['ba'], prm['sp'])


def _lru_bwd(xr, gate, dyl, carry, prm):
    n = xr.shape[0]
    nt = n // TILE

    def body(x_ref, xh_ref, g_ref, dy_ref, cin_ref, cw_ref, cb_ref, wx_ref, wa_ref, bx_ref, ba_ref, sp_ref,
             dx_ref, dg_ref, dcw0, dcw1, dcw2, dcw3, dcb_ref, dwx_ref, dwa_ref, dbx_ref, dba_ref, dsp_ref, an_s, gn_s, dn_s):
        first_tile = pl.program_id(0) == nt - 1

        @pl.when(pl.program_id(0) == 0)
        def _():
            for ref in (dcw0, dcw1, dcw2, dcw3, dcb_ref, dwx_ref, dwa_ref, dbx_ref, dba_ref, dsp_ref, gn_s, dn_s):
                ref[...] = jnp.zeros_like(ref)
            an_s[...] = jnp.ones_like(an_s)

        xr = x_ref[...]
        halo = jnp.where(first_tile, 0.0, xh_ref[...])
        sh, xc, xb, gx, ga, a, mult, start = _lru_gates(xr, halo, nt - 1 - pl.program_id(0), cw_ref, cb_ref, wx_ref, wa_ref, bx_ref,
                                                        ba_ref, sp_ref)
        acum, h = _rscan(a, mult * gx * xc, reverse=False)
        cin = cin_ref[0:1, :]
        h = h + acum * cin
        gate = g_ref[...]
        dyl = dy_ref[...]
        dg_ref[...] = dyl * h * _gelu_grad(gate)
        row = _row_iota(xr.shape)
        alpha = jnp.where(row < TILE - 1, pltpu.roll(a, TILE - 1, 0), an_s[...])
        racc, g = _rscan(alpha, dyl * _gelu(gate), reverse=True)
        g = g + racc * gn_s[...]
        an_s[...] = a[0:1]
        gn_s[...] = g[0:1]
        hprev = jnp.where(row == 0, cin, pltpu.roll(h, 1, 0))
        da = g * hprev
        dmult = jnp.where(start, 0.0, g * gx * xc)
        dla = da * a - dmult * a * a / mult
        dsp_ref[...] += _sum0(-LRU_C * ga * dla)
        dpa = (-LRU_C * sp_ref[...] * dla) * ga * (1.0 - ga)
        dpx = (g * mult * xc) * gx * (1.0 - gx)
        dba_ref[...] += _sum0(dpa)
        dbx_ref[...] += _sum0(dpx)
        dpab, dpxb = _mx(dpa), _mx(dpx)
        dwa_ref[...] += _dot_tn(xb, dpab)
        dwx_ref[...] += _dot_tn(xb, dpxb)
        dxc = g * mult * gx + _dot_nt(dpab, wa_ref[...]) + _dot_nt(dpxb, wx_ref[...])
        dcb_ref[...] += _sum0(dxc)
        dcw3[...] += _sum0(dxc * sh[0])
        dcw2[...] += _sum0(dxc * sh[1])
        dcw1[...] += _sum0(dxc * sh[2])
        dcw0[...] += _sum0(dxc * sh[3])
        ext = jnp.concatenate([dxc, dn_s[...]], axis=0)
        dx_ref[...] = (cw_ref[3:4, :] * dxc + cw_ref[2:3, :] * _shift_up(ext, 1, TILE) + cw_ref[1:2, :] * _shift_up(ext, 2, TILE)
                       + cw_ref[0:1, :] * _shift_up(ext, 3, TILE))
        dn_s[...] = dxc[:8]

    r = functools.partial(_rows_spec, n_tiles=nt, reverse=True)
    vec = _sds((1, D_LRU))
    return _call(
        body, "lru_bwd", nt,
        [r(TILE, D_LRU), _halo_spec(D_LRU, TILE, nt, reverse=True), r(TILE, D_LRU), r(TILE, D_LRU), r(8, D_LRU)] + [_whole()] * 7,
        [r(TILE, D_LRU), r(TILE, D_LRU)] + [_whole()] * 10,
        [_sds((n, D_LRU)), _sds((n, D_LRU)), vec, vec, vec, vec, vec, _sds((D_LRU, D_LRU)), _sds((D_LRU, D_LRU)), vec, vec, vec],
        scratch=[pltpu.VMEM((1, D_LRU), F32), pltpu.VMEM((1, D_LRU), F32), pltpu.VMEM((8, D_LRU), F32)],
    )(xr, xr, gate, dyl, carry, prm['conv_w'], prm['conv_b'], prm['wx'], prm['wa'], prm['bx'], prm['ba'], prm['sp'])


def _normed_parts(ya, ys, yl):
    return jnp.concatenate([ya * _rms(ya), ys * _rms(ys), yl * _rms(yl)], axis=1)


def _mixout_fwd(ya, ys, yl, x0, g_mix, w_out, b_out, g1, b1):
    n = x0.shape[0]
    nt = n // TILE_WIDE

    def body(ya_ref, ys_ref, yl_ref, x_ref, gm_ref, w_ref, b_ref, g_ref, be_ref, mix_ref, r_ref, x1_ref):
        mixb = _mx(_normed_parts(ya_ref[...], ys_ref[...], yl_ref[...]) * gm_ref[...])
        mix_ref[...] = mixb
        r1 = ALPHA * x_ref[...] + _dot(mixb, w_ref[...]) + b_ref[...]
        r_ref[...] = r1
        xhat, _ = _ln_stats(r1)
        x1_ref[...] = xhat * g_ref[...] + be_ref[...]

    r = functools.partial(_rows_spec, n_tiles=nt)
    return _call(
        body, "mixout_fwd", nt,
        [r(TILE_WIDE, D_ATTN), r(TILE_WIDE, D_S5), r(TILE_WIDE, D_LRU), r(TILE_WIDE, D)] + [_whole()] * 5,
        [r(TILE_WIDE, D), r(TILE_WIDE, D), r(TILE_WIDE, D)],
        [_sds((n, D), MXU), _sds((n, D)), _sds((n, D))],
    )(ya, ys, yl, x0, g_mix, w_out, b_out, g1, b1)


def _mixout_bwd(dr1, mix, ya, ys, yl, g_mix, w_out):
    n = dr1.shape[0]
    nt = n // TILE_WIDE

    def body(dr_ref, mix_ref, ya_ref, ys_ref, yl_ref, gm_ref, w_ref, dya_ref, dys_ref, dyl_ref, dw_ref, db_ref, dgm_ref):
        @pl.when(pl.program_id(0) == 0)
        def _():
            for ref in (dw_ref, db_ref, dgm_ref):
                ref[...] = jnp.zeros_like(ref)

        dr = dr_ref[...]
        db_ref[...] += _sum0(dr)
        drb = _mx(dr)
        dw_ref[...] += _dot_tn(mix_ref[...], drb)
        dmix = _dot(drb, w_ref[...])
        parts = (ya_ref[...], ys_ref[...], yl_ref[...])
        dgm_ref[...] += _sum0(dmix * _normed_parts(*parts))
        dn = dmix * gm_ref[...]
        lo = 0
        for y, out in zip(parts, (dya_ref, dys_ref, dyl_ref)):
            w = y.shape[1]
            rs = _rms(y)
            nrm = y * rs
            dnp = dn[:, lo:lo + w]
            out[...] = rs * (dnp - nrm * jnp.mean(dnp * nrm, axis=-1, keepdims=True))
            lo += w

    r = functools.partial(_rows_spec, n_tiles=nt)
    return _call(
        body, "mixout_bwd", nt,
        [r(TILE_WIDE, D), r(TILE_WIDE, D), r(TILE_WIDE, D_ATTN), r(TILE_WIDE, D_S5), r(TILE_WIDE, D_LRU), _whole(), _whole()],
        [r(TILE_WIDE, D_ATTN), r(TILE_WIDE, D_S5), r(TILE_WIDE, D_LRU), _whole(), _whole(), _whole()],
        [_sds((n, D_ATTN)), _sds((n, D_S5)), _sds((n, D_LRU)), _sds((D, D)), _sds((1, D)), _sds((1, D))],
    )(dr1, mix, ya, ys, yl, g_mix, w_out)


def _ffn_conv(gp, halo, cw_ref, cb_ref, cs):
    ext = jnp.concatenate([halo, gp], axis=0)
    s1 = _shift_down(ext, 1, TILE)
    s2 = _shift_down(ext, 2, TILE)
    return s1, s2, cb_ref[:, cs] + cw_ref[2:3, cs] * gp + cw_ref[1:2, cs] * s1 + cw_ref[0:1, cs] * s2


def _ffn_fwd(x1, wg, wu, cw, cb, wd, g2, b2):
    n = x1.shape[0]
    nt = n // TILE

    def body(x_ref, wg_ref, wu_ref, cw_ref, cb_ref, wd_ref, g_ref, be_ref, gp_ref, up_ref, r_ref, x2_ref, halo_s, act_s):
        @pl.when(pl.program_id(0) == 0)
        def _():
            halo_s[...] = jnp.zeros_like(halo_s)

        x1 = x_ref[...]
        xb = _mx(x1)
        for c in range(D_FF // FF_CHUNK):
            cs = slice(c * FF_CHUNK, (c + 1) * FF_CHUNK)
            gp = _dot(xb, wg_ref[:, cs])
            up = _dot(xb, wu_ref[:, cs])
            gp_ref[:, cs] = gp
            up_ref[:, cs] = up
            _, _, gc = _ffn_conv(gp, halo_s[:, cs], cw_ref, cb_ref, cs)
            halo_s[:, cs] = gp[TILE - 8:]
            act_s[:, cs] = _mx(gc * _sigmoid(gc) * up)
        r2 = ALPHA * x1 + _dot(act_s[...], wd_ref[...])
        r_ref[...] = r2
        xhat, _ = _ln_stats(r2)
        x2_ref[...] = xhat * g_ref[...] + be_ref[...]

    r = functools.partial(_rows_spec, n_tiles=nt)
    return _call(
        body, "ffn_fwd", nt,
        [r(TILE, D)] + [_whole()] * 7,
        [r(TILE, D_FF), r(TILE, D_FF), r(TILE, D), r(TILE, D)],
        [_sds((n, D_FF)), _sds((n, D_FF)), _sds((n, D)), _sds((n, D))],
        scratch=[pltpu.VMEM((8, D_FF), F32), pltpu.VMEM((TILE, D_FF), MXU)],
    )(x1, wg, wu, cw, cb, wd, g2, b2)


def _ffn_bwd_down(dx2, r2, g2, gp, up, cw, cb, wd_t):
    n = dx2.shape[0]
    nt = n // TILE

    def body(dx_ref, r_ref, g_ref, gp_ref, gh_ref, up_ref, cw_ref, cb_ref, wd_ref,
             dr_ref, dgp_ref, dup_ref, dwd_ref, dcw0, dcw1, dcw2, dcb_ref, dg_ref, db_ref, next_s):
        first_tile = pl.program_id(0) == nt - 1

        @pl.when(pl.program_id(0) == 0)
        def _():
            for ref in (dwd_ref, dcw0, dcw1, dcw2, dcb_ref, dg_ref, db_ref, next_s):
                ref[...] = jnp.zeros_like(ref)

        dx2 = dx_ref[...]
        xhat, rstd = _ln_stats(r_ref[...])
        dg_ref[...] += _sum0(dx2 * xhat)
        db_ref[...] += _sum0(dx2)
        dr2 = _ln_bwd(dx2, g_ref[...], xhat, rstd)
        dr_ref[...] = dr2
        dfb = _mx(dr2)
        for c in range(D_FF // FF_CHUNK):
            cs = slice(c * FF_CHUNK, (c + 1) * FF_CHUNK)
            gp = gp_ref[:, cs]
            up = up_ref[:, cs]
            s1, s2, gc = _ffn_conv(gp, jnp.where(first_tile, 0.0, gh_ref[:, cs]), cw_ref, cb_ref, cs)
            sg = _sigmoid(gc)
            silu = gc * sg
            dact = _dot(dfb, wd_ref[:, cs])
            dwd_ref[cs, :] += _dot_tn(_mx(silu * up), dfb)
            dup_ref[:, cs] = _mx(dact * silu)
            dgc = dact * up * (sg + silu * (1.0 - sg))
            dcb_ref[:, cs] += _sum0(dgc)
            dcw2[:, cs] += _sum0(dgc * gp)
            dcw1[:, cs] += _sum0(dgc * s1)
            dcw0[:, cs] += _sum0(dgc * s2)
            ext = jnp.concatenate([dgc, next_s[:, cs]], axis=0)
            dgp_ref[:, cs] = _mx(cw_ref[2:3, cs] * dgc + cw_ref[1:2, cs] * _shift_up(ext, 1, TILE)
                                 + cw_ref[0:1, cs] * _shift_up(ext, 2, TILE))
            next_s[:, cs] = dgc[:8]

    r = functools.partial(_rows_spec, n_tiles=nt, reverse=True)
    vff = _sds((1, D_FF))
    return _call(
        body, "ffn_bwd_down", nt,
        [r(TILE, D), r(TILE, D), _whole(), r(TILE, D_FF), _halo_spec(D_FF, TILE, nt, reverse=True), r(TILE, D_FF), _whole(), _whole(),
         _whole()],
        [r(TILE, D), r(TILE, D_FF), r(TILE, D_FF)] + [_whole()] * 7,
        [_sds((n, D)), _sds((n, D_FF), MXU), _sds((n, D_FF), MXU), _sds((D_FF, D)), vff, vff, vff, vff, _sds((1, D)), _sds((1, D))],
        scratch=[pltpu.VMEM((8, D_FF), F32)],
    )(dx2, r2, g2, gp, gp, up, cw, cb, wd_t)


def _ffn_bwd_dx(dr2, dgp, dup, r1, g1, wg_t, wu_t):
    n = dr2.shape[0]
    rows = TILE_BIG
    nt = n // rows

    def body(dr2_ref, dgp_ref, dup_ref, r_ref, g_ref, wg_ref, wu_ref, dr1_ref, dg_ref, db_ref):
        @pl.when(pl.program_id(0) == 0)
        def _():
            for ref in (dg_ref, db_ref):
                ref[...] = jnp.zeros_like(ref)

        dx1 = ALPHA * dr2_ref[...] + _dot(dgp_ref[...], wg_ref[...]) + _dot(dup_ref[...], wu_ref[...])
        xhat, rstd = _ln_stats(r_ref[...])
        dg_ref[...] += _sum0(dx1 * xhat)
        db_ref[...] += _sum0(dx1)
        dr1_ref[...] = _ln_bwd(dx1, g_ref[...], xhat, rstd)

    r = functools.partial(_rows_spec, n_tiles=nt)
    return _call(
        body, "ffn_bwd_dx", nt,
        [r(rows, D), r(rows, D_FF), r(rows, D_FF), r(rows, D), _whole(), _whole(), _whole()],
        [r(rows, D), _whole(), _whole()],
        [_sds((n, D)), _sds((1, D)), _sds((1, D))],
    )(dr2, dgp, dup, r1, g1, wg_t, wu_t)


def _ffn_bwd_dw(x1, dgp, dup):
    n = x1.shape[0]
    rows = TILE_BIG
    nt = n // rows

    def body(x_ref, dgp_ref, dup_ref, dwg_ref, dwu_ref):
        @pl.when(pl.program_id(0) == 0)
        def _():
            for ref in (dwg_ref, dwu_ref):
                ref[...] = jnp.zeros_like(ref)

        xb = _mx(x_ref[...])
        for c in range(D_FF // FF_CHUNK):
            cs = slice(c * FF_CHUNK, (c + 1) * FF_CHUNK)
            dwg_ref[cs, :] += _dot_tn(dgp_ref[:, cs], xb)
            dwu_ref[cs, :] += _dot_tn(dup_ref[:, cs], xb)

    r = functools.partial(_rows_spec, n_tiles=nt)
    return _call(
        body, "ffn_bwd_dw", nt,
        [r(rows, D), r(rows, D_FF), r(rows, D_FF)], [_whole(), _whole()], [_sds((D_FF, D)), _sds((D_FF, D))],
    )(x1, dgp, dup)


def _loss_head(y, target):
    n = y.shape[0]
    nt = n // TILE_WIDE

    def body(y_ref, t_ref, loss_ref, dy_ref):
        @pl.when(pl.program_id(0) == 0)
        def _():
            loss_ref[...] = jnp.zeros_like(loss_ref)

        e = y_ref[...] - t_ref[...]
        dy_ref[...] = e * (1.0 / D)
        loss_ref[...] += _sum0(jnp.sum(e * e, axis=1, keepdims=True)) * (0.5 / D)

    r = functools.partial(_rows_spec, n_tiles=nt)
    return _call(body, "loss_head", nt, [r(TILE_WIDE, D), r(TILE_WIDE, D)], [_whole(), r(TILE_WIDE, D)],
                 [_sds((1, 1)), _sds((n, D))])(y, target)


def _place():
    x, y, c = lax.axis_index("x"), lax.axis_index("y"), lax.axis_index("c")
    return x, y, c, 4 * x + 2 * y + c


def _peer(x, y, c, k):
    px, py, pc = x ^ ((k >> 2) & 1), y ^ ((k >> 1) & 1), c ^ (k & 1)
    return (px, py, pc), 4 * px + 2 * py + pc


def _all_gather(name, blocks, small):
    srcs = list(blocks) + [small]
    n = len(srcs)
    out_shapes = [_sds((a.shape[0], N_DEV * a.shape[1], LANES), a.dtype) for a in blocks] + [_sds((N_DEV,) + small.shape, small.dtype)]

    def body(*refs):
        src_refs, out_refs = refs[:n], refs[n:2 * n]
        send_sems, recv_sems, local_sems = refs[2 * n:]
        x, y, c, me = _place()

        def landing(a, slot):
            if a == n - 1:
                return out_refs[a].at[slot]
            r = src_refs[a].shape[1]
            return out_refs[a].at[:, pl.ds(slot * r, r), :]

        def remote(a, k, slot):
            peer, _ = _peer(x, y, c, k)
            return pltpu.make_async_remote_copy(
                src_ref=src_refs[a], dst_ref=landing(a, slot), send_sem=send_sems.at[a * N_DEV + k],
                recv_sem=recv_sems.at[a * N_DEV + k], device_id=peer, device_id_type=pl.DeviceIdType.MESH)

        mine = [pltpu.make_async_copy(src_refs[a], landing(a, me), local_sems.at[a]) for a in range(n)]
        sends = [remote(a, k, me) for a in range(n) for k in range(1, N_DEV)]
        for cp in mine + sends:
            cp.start()
        for a in range(n):
            for k in range(1, N_DEV):
                remote(a, k, _peer(x, y, c, k)[1]).wait_recv()
        for cp in sends:
            cp.wait_send()
        for cp in mine:
            cp.wait()

    any_space = pl.BlockSpec(memory_space=pl.ANY)
    return pl.pallas_call(
        body, name=name, out_shape=out_shapes, in_specs=[any_space] * n, out_specs=[any_space] * n,
        scratch_shapes=[pltpu.SemaphoreType.DMA((n * N_DEV,)), pltpu.SemaphoreType.DMA((n * N_DEV,)), pltpu.SemaphoreType.DMA((n,))],
    )(*srcs)


_HBM = pl.BlockSpec(memory_space=pltpu.HBM)
_SEM = pl.BlockSpec(memory_space=pltpu.SEMAPHORE)
_EFFECT = pltpu.SideEffectType.DATAFLOW_SIDE_EFFECTING


def _in_hbm(a):
    return pltpu.with_memory_space_constraint(a, pltpu.HBM)


def _scatter_start(name, srcs):
    ns = len(srcs)
    rows_a = [a.shape[0] // N_DEV for a in srcs]
    offs = [sum(rows_a[:a]) for a in range(ns)]
    total = sum(rows_a)

    def body(*refs):
        src_refs, land_ref, send_sems, recv_sems, token = refs[:ns], refs[ns], refs[ns + 1], refs[ns + 2], refs[-1]
        x, y, c, me = _place()
        for k in range(1, N_DEV):
            peer, peer_slot = _peer(x, y, c, k)
            for a in range(ns):
                pltpu.make_async_remote_copy(
                    src_ref=src_refs[a].at[pl.ds(peer_slot * rows_a[a], rows_a[a]), :],
                    dst_ref=land_ref.at[me, pl.ds(offs[a], rows_a[a]), :], send_sem=send_sems.at[k], recv_sem=recv_sems.at[k],
                    device_id=peer, device_id_type=pl.DeviceIdType.MESH).start()
        token[...] = jnp.zeros_like(token)

    landing = lax.empty((N_DEV, total, LANES), F32)
    out = pl.pallas_call(
        body, name=name,
        out_shape=(pltpu.SemaphoreType.DMA((N_DEV,)), pltpu.SemaphoreType.DMA((N_DEV,)), *[pltpu.HBM(a.shape, a.dtype) for a in srcs],
                   pltpu.HBM(landing.shape, F32), _sds((8, 128))),
        in_specs=[_HBM] * (ns + 1), out_specs=(_SEM, _SEM, *[_HBM] * (ns + 1), pl.BlockSpec(memory_space=pltpu.VMEM)),
        input_output_aliases={a: 2 + a for a in range(ns + 1)},
        compiler_params=pltpu.CompilerParams(has_side_effects=_EFFECT),
    )(*[_in_hbm(a) for a in srcs], _in_hbm(landing))
    return out[0], out[1], out[2:2 + ns], out[2 + ns], out[-1]


def _scatter_wait(name, send_sems, recv_sems, srcs, landing, after):
    ns = len(srcs)

    def body(*refs):
        land_ref, send_ref, recv_ref = refs[ns], refs[ns + 1], refs[ns + 2]
        x, y, c, me = _place()
        for k in range(1, N_DEV):
            peer, peer_slot = _peer(x, y, c, k)
            slot = pltpu.make_async_remote_copy(
                src_ref=land_ref.at[me], dst_ref=land_ref.at[peer_slot], send_sem=send_ref.at[k], recv_sem=recv_ref.at[k],
                device_id=peer, device_id_type=pl.DeviceIdType.MESH)
            slot.wait_send()
            slot.wait_recv()

    out = pl.pallas_call(
        body, name=name, out_shape=(*[pltpu.HBM(a.shape, a.dtype) for a in srcs], pltpu.HBM(landing.shape, landing.dtype)),
        in_specs=[_HBM] * (ns + 1) + [_SEM, _SEM, pl.BlockSpec(memory_space=pl.ANY)], out_specs=[_HBM] * (ns + 1),
        input_output_aliases={a: a for a in range(ns + 1)},
        compiler_params=pltpu.CompilerParams(has_side_effects=_EFFECT),
    )(*srcs, landing, send_sems, recv_sems, after)
    return out[:ns], out[ns]


def _gather_start(name, blocks):
    n = len(blocks)

    def body(*refs):
        src_refs, land_refs, send_sems, recv_sems, token = refs[:n], refs[n:2 * n], refs[2 * n], refs[2 * n + 1], refs[-1]
        x, y, c, me = _place()
        for a in range(n):
            r = src_refs[a].shape[1]
            for k in range(1, N_DEV):
                pltpu.make_async_remote_copy(
                    src_ref=src_refs[a], dst_ref=land_refs[a].at[:, pl.ds(me * r, r), :], send_sem=send_sems.at[a * N_DEV + k],
                    recv_sem=recv_sems.at[a * N_DEV + k], device_id=_peer(x, y, c, k)[0], device_id_type=pl.DeviceIdType.MESH).start()
        token[...] = jnp.zeros_like(token)

    wholes = [lax.empty((a.shape[0], N_DEV * a.shape[1], LANES), a.dtype) for a in blocks]
    out = pl.pallas_call(
        body, name=name,
        out_shape=(pltpu.SemaphoreType.DMA((n * N_DEV,)), pltpu.SemaphoreType.DMA((n * N_DEV,)),
                   *[pltpu.HBM(a.shape, a.dtype) for a in blocks + wholes], _sds((8, 128))),
        in_specs=[_HBM] * (2 * n), out_specs=(_SEM, _SEM, *[_HBM] * (2 * n), pl.BlockSpec(memory_space=pltpu.VMEM)),
        input_output_aliases={a: 2 + a for a in range(2 * n)},
        compiler_params=pltpu.CompilerParams(has_side_effects=_EFFECT),
    )(*[_in_hbm(a) for a in blocks + wholes])
    return out[0], out[1], out[2:2 + n], out[2 + n:2 + 2 * n], out[-1]


def _gather_wait(name, send_sems, recv_sems, blocks, wholes, after):
    n = len(blocks)

    def body(*refs):
        src_refs, land_refs, send_ref, recv_ref = refs[:n], refs[n:2 * n], refs[2 * n], refs[2 * n + 1]
        x, y, c, me = _place()
        for a in range(n):
            r = src_refs[a].shape[1]
            for k in range(1, N_DEV):
                peer, peer_slot = _peer(x, y, c, k)
                cp = pltpu.make_async_remote_copy(
                    src_ref=src_refs[a], dst_ref=land_refs[a].at[:, pl.ds(peer_slot * r, r), :], send_sem=send_ref.at[a * N_DEV + k],
                    recv_sem=recv_ref.at[a * N_DEV + k], device_id=peer, device_id_type=pl.DeviceIdType.MESH)
                cp.wait_send()
                cp.wait_recv()

    return pl.pallas_call(
        body, name=name, out_shape=tuple(pltpu.HBM(a.shape, a.dtype) for a in list(blocks) + list(wholes)),
        in_specs=[_HBM] * (2 * n) + [_SEM, _SEM, pl.BlockSpec(memory_space=pl.ANY)], out_specs=[_HBM] * (2 * n),
        input_output_aliases={a: a for a in range(2 * n)},
        compiler_params=pltpu.CompilerParams(has_side_effects=_EFFECT),
    )(*blocks, *wholes, send_sems, recv_sems, after)


def _reduce_adamw(parts, w, m, v, tile_rows=PACK_TILE):
    rows = w.shape[0]
    nt = rows // tile_rows
    slots = parts.shape[0]
    c1 = 1.0 - ADAM_B1 ** ADAM_STEP
    c2 = 1.0 - ADAM_B2 ** ADAM_STEP

    def body(p_ref, w_ref, m_ref, v_ref, g_out, d_out, m_out, v_out):
        g = p_ref[0]
        for s in range(1, slots):
            g = g + p_ref[s]
        m_new = ADAM_B1 * m_ref[...] + (1.0 - ADAM_B1) * g
        v_new = ADAM_B2 * v_ref[...] + (1.0 - ADAM_B2) * (g * g)
        g_out[...] = g
        m_out[...] = m_new
        v_out[...] = v_new
        d_out[...] = -ADAM_LR * ((m_new / c1) / (jnp.sqrt(v_new / c2) + ADAM_EPS) + ADAM_WD * w_ref[...])

    r = _rows_spec(tile_rows, LANES, nt)
    out = _sds((rows, LANES))
    return _call(
        body, "reduce_adamw", nt,
        [pl.BlockSpec((slots, tile_rows, LANES), lambda i: (0, i, 0)), r, r, r], [r, r, r, r], [out, out, out, out],
    )(parts, w, m, v)


def _pack_rows(a, lead=0):
    head = a.shape[:lead]
    flat = a.reshape(head + (-1,))
    size = flat.shape[-1]
    rows = -(-size // (16 * LANES)) * 16
    flat = jnp.pad(flat, [(0, 0)] * lead + [(0, rows * LANES - size)])
    return flat.reshape(head + (rows, LANES))


def _packed_rows(shape):
    return -(-math.prod(shape) // (16 * LANES)) * 16


def _to_blocks(full, axis):
    l, a, b = full.shape
    if axis == 2:
        return full.reshape(l, a, N_DEV, b // N_DEV).transpose(2, 0, 1, 3)
    return full.reshape(l, N_DEV, a // N_DEV, b).transpose(1, 0, 2, 3)


def _from_blocks(blocks, axis):
    _, l, a, b = blocks.shape
    if axis == 2:
        return blocks.transpose(1, 2, 0, 3).reshape(l, a, N_DEV * b)
    return blocks.transpose(1, 0, 2, 3).reshape(l, N_DEV * a, b)


def _row_form(shard, transposed):
    return shard.transpose(0, 2, 1) if transposed else shard


def _me():
    return 4 * lax.axis_index("x") + 2 * lax.axis_index("y") + lax.axis_index("c")


def _both_forms(names, wholes):
    out = {}
    for name, w in zip(names, wholes):
        t = dict(BIG)[name]
        out[name + '_t' if t else name] = w
        out[name if t else name + '_t'] = w.transpose(0, 2, 1)
    return out


def _gather_weights(local):
    segs, meta = [], []
    for name in SMALL_SHARDED:
        blk = local[name]
        if name in GATHER_F32:
            bits = lax.bitcast_convert_type(blk, MXU)
        else:
            bits = _mx(blk)
        seg = _pack_rows(bits)
        meta.append((name, bits.shape, seg.shape[0]))
        segs.append(seg)
    blocks = {name: _mx(_row_form(local[name], t)) for name, t in BIG}
    mix, ffn = PARTS['mix'], PARTS['ffn']
    *first, gathered = _all_gather("gather_weights", [blocks[n][:1] for n in mix], jnp.concatenate(segs, axis=0))
    flights = {'ffn0': _gather_start("gather_ffn0_start", [blocks[n][:1] for n in ffn]),
               'later': _gather_start("gather_later_start", [blocks[n][1:] for n in mix + ffn])}
    out, lo = {}, 0
    for name, bits_shape, rows in meta:
        seg = gathered[:, lo:lo + rows].reshape(N_DEV, -1)[:, :math.prod(bits_shape)].reshape((N_DEV,) + bits_shape)
        if name in GATHER_F32:
            seg = lax.bitcast_convert_type(seg, F32)
        out[name] = _from_blocks(seg, SHARD_AXIS[name])
        lo += rows
    ready = {(0, 'mix'): {k: v[0] for k, v in _both_forms(mix, first).items()}}

    def landed(flight, names, after):
        send_sems, recv_sems, mine, wholes, _ = flights[flight]
        done = _gather_wait(f"gather_{flight}_wait", send_sems, recv_sems, mine, wholes, after)
        own = [lax.dynamic_update_slice(w, b, (0, _me() * b.shape[1], 0)) for b, w in zip(done[:len(names)], done[len(names):])]
        return _both_forms(names, own)

    def big_weights(l, part, after):
        if (l, part) not in ready and l == 0:
            ready[(0, 'ffn')] = {k: v[0] for k, v in landed('ffn0', ffn, after).items()}
        elif (l, part) not in ready:
            forms = landed('later', mix + ffn, after)
            for j in range(1, DEPTH):
                for p, names in PARTS.items():
                    ready[(j, p)] = {k: forms[k][j - 1] for n in names for k in (n, n + '_t')}
        return ready[(l, part)]

    return out, big_weights, flights['ffn0'][-1][0, 0] + flights['later'][-1][0, 0]


def _s5_discretize(a_re, a_im, log_dt, b_re, b_im):
    lam_re = jnp.minimum(a_re, -1e-4)
    lam_im = a_im
    dt = jnp.exp(log_dt)[:, None]
    decay = jnp.exp(dt * lam_re)
    ang = dt * lam_im
    abar_re = decay * jnp.cos(ang)
    abar_im = decay * jnp.sin(ang)
    den = jnp.square(lam_re) + jnp.square(lam_im)
    nr = abar_re - 1.0
    ni = abar_im
    coef_re = (nr * lam_re + ni * lam_im) / den
    coef_im = (ni * lam_re - nr * lam_im) / den
    bbar_re = coef_re[..., None] * b_re - coef_im[..., None] * b_im
    bbar_im = coef_re[..., None] * b_im + coef_im[..., None] * b_re
    return abar_re, abar_im, bbar_re, bbar_im


def _complex_powers(ar, ai, count):
    def combine(e1, e2):
        return e2[0] * e1[0] - e2[1] * e1[1], e2[0] * e1[1] + e2[1] * e1[0]

    shape = (count,) + ar.shape
    return lax.associative_scan(combine, (jnp.broadcast_to(ar, shape), jnp.broadcast_to(ai, shape)), axis=0)


_EYE16 = functools.partial(jnp.eye, 16, dtype=F32)


def _s5_params(p):
    disc, disc_vjp = jax.vjp(jax.vmap(_s5_discretize), p['s5_a_re'], p['s5_a_im'], p['s5_log_dt'], p['s5_b_re'], p['s5_b_im'])
    abar_re, abar_im, bbar_re, bbar_im = disc
    pw_re, pw_im = _complex_powers(abar_re.reshape(DEPTH, N_STATE), abar_im.reshape(DEPTH, N_STATE), SEG)
    sp_re, sp_im = _complex_powers(pw_re[SEG - 1], pw_im[SEG - 1], N_SEG - 1)
    one, zero = jnp.ones((1, DEPTH, N_STATE), F32), jnp.zeros((1, DEPTH, N_STATE), F32)
    seg_re = jnp.concatenate([one, sp_re], axis=0)
    seg_im = jnp.concatenate([zero, sp_im], axis=0)
    doubling = [0, 1, 3]
    blank = jnp.zeros((5, DEPTH, N_STATE), F32)
    tables = {
        'pw_re': pw_re, 'pw_im': pw_im,
        'dbl_re': jnp.concatenate([jnp.stack([sp_re[k] for k in doubling]), blank], axis=0),
        'dbl_im': jnp.concatenate([jnp.stack([sp_im[k] for k in doubling]), blank], axis=0),
        'seg_re': seg_re, 'seg_im': seg_im, 'segr_re': seg_re[::-1], 'segr_im': seg_im[::-1],
    }
    weights = {
        'b_re': _mx(jnp.einsum('lgpc,gh->lgchp', bbar_re, _EYE16()).reshape(DEPTH, D_S5, N_STATE)),
        'b_im': _mx(jnp.einsum('lgpc,gh->lgchp', bbar_im, _EYE16()).reshape(DEPTH, D_S5, N_STATE)),
        'c_re': _mx(jnp.einsum('lgcp,gh->lgphc', p['s5_c_re'], _EYE16()).reshape(DEPTH, N_STATE, D_S5)),
        'c_im': _mx(jnp.einsum('lgcp,gh->lgphc', p['s5_c_im'], _EYE16()).reshape(DEPTH, N_STATE, D_S5)),
    }
    src = (jnp.arange(TILE) % N_SEG) * SEG + jnp.arange(TILE) // N_SEG
    perm = (src[:, None] == jnp.arange(TILE)[None, :]).astype(MXU)
    layers = []
    for l in range(DEPTH):
        prm = {k: v[:, l] for k, v in tables.items()}
        prm.update({k: v[l] for k, v in weights.items()})
        prm.update({'perm': perm, 'perm_t': perm.T, 'd': p['s5_d'][l][None, :], 'glu_w': p['s5_glu_w'][l],
                    'glu_b': p['s5_glu_b'][l][None, :]})
        layers.append(prm)
    return layers, disc_vjp


def _lru_params(p, l):
    eye4 = jnp.eye(4, dtype=F32)
    return {
        'conv_w': p['lru_conv_w'][l], 'conv_b': p['lru_conv_b'][l][None, :],
        'wx': _mx(jnp.einsum('hij,hk->hikj', p['lru_wx'][l], eye4).reshape(D_LRU, D_LRU)),
        'wa': _mx(jnp.einsum('hij,hk->hikj', p['lru_wa'][l], eye4).reshape(D_LRU, D_LRU)),
        'bx': p['lru_bx'][l][None, :], 'ba': p['lru_ba'][l][None, :],
        'sp': jax.nn.softplus(-p['lru_a_param'][l])[None, :],
    }


def _rope_tables(n):
    inv_freq = ROPE_THETA ** (-jnp.arange(0, 64, 2, dtype=F32) / 64)
    ang = jnp.arange(n, dtype=F32)[:, None] * inv_freq[None, :]
    cos, sin = jnp.cos(ang), jnp.sin(ang)
    return jnp.concatenate([cos, cos, cos, cos], axis=1), jnp.concatenate([-sin, sin, -sin, sin], axis=1)


def _sink_cols(sinks):
    nb = TILE_Q // ATTN_BLOCK
    per_unit = sinks.reshape(4, 2).T
    return jnp.broadcast_to(per_unit[:, None, :, None, None], (2, nb, 4, ATTN_BLOCK, 128)).reshape(2, nb * 4 * ATTN_BLOCK, 128)


def _local_step(x, target, p, big_weights=None, emit_grads=None):
    if big_weights is None:
        big_weights = lambda l, part, after: {k: p[k][l] for name in PARTS[part] for k in (name, name + '_t')}
    if emit_grads is None:
        emit_grads = lambda l, part, grads: 0.0
    n = x.shape[0]
    cos_t, sin_t = _rope_tables(n)
    s5_layers, s5_vjp = _s5_params(p)
    row = lambda a: a[None, :]
    saved = []
    h = x
    for l in range(DEPTH):
        s = {'x0': h}
        bw = s['bw'] = dict(big_weights(l, 'mix', h))
        s['q'], k, v, s['u'], s['xr'], s['gate'] = _inproj_fwd(h, bw['w_in'], row(p['b_in'][l]), cos_t, sin_t)
        no_keys = jnp.zeros((ATTN_BLOCK, D_KV), MXU)
        s['k'], s['v'] = jnp.concatenate([no_keys, k], axis=0), jnp.concatenate([no_keys, v], axis=0)
        s['sinks'] = _sink_cols(p['attn_sinks'][l])
        s['ya'] = _attn_fwd(s['q'], s['k'], s['v'], s['sinks'])
        s['s5'] = s5_layers[l]
        s['lru'] = _lru_params(p, l)
        s['ys'], s['s5_cr'], s['s5_ci'] = _s5_fwd(s['u'], s['s5'])
        s['yl'], s['lru_c'] = _lru_fwd(s['xr'], s['gate'], s['lru'])
        s['mix'], s['r1'], s['x1'] = _mixout_fwd(s['ya'], s['ys'], s['yl'], h, row(p['mix_norm_g'][l]), bw['w_out'],
                                                 row(p['b_out'][l]), row(p['ln1_g'][l]), row(p['ln1_b'][l]))
        bw.update(big_weights(l, 'ffn', s['x1']))
        s['gp'], s['up'], s['r2'], h = _ffn_fwd(s['x1'], bw['ffn_w_gate'], bw['ffn_w_up'], p['ffn_conv_w'][l],
                                                row(p['ffn_conv_b'][l]), bw['ffn_w_down'], row(p['ln2_g'][l]), row(p['ln2_b'][l]))
        saved.append(s)
    loss, dh = _loss_head(h, target)
    placed = 0.0

    grads = {name: [None] * DEPTH for name in WEIGHTS}
    d_disc = [None] * DEPTH
    for l in reversed(range(DEPTH)):
        s = saved[l]
        g = {}
        (dr2, dgp, dup, g['ffn_w_down'], cw0, cw1, cw2, dcb, dg2, db2) = _ffn_bwd_down(
            dh, s['r2'], row(p['ln2_g'][l]) + placed, s['gp'], s['up'], p['ffn_conv_w'][l], row(p['ffn_conv_b'][l]),
            s['bw']['ffn_w_down_t'])
        g['ffn_conv_w'] = jnp.concatenate([cw0, cw1, cw2], axis=0)
        g['ffn_conv_b'], g['ln2_g'], g['ln2_b'] = dcb[0], dg2[0], db2[0]
        dr1, dg1, db1 = _ffn_bwd_dx(dr2, dgp, dup, s['r1'], row(p['ln1_g'][l]), s['bw']['ffn_w_gate_t'], s['bw']['ffn_w_up_t'])
        g['ffn_w_gate'], g['ffn_w_up'] = _ffn_bwd_dw(s['x1'], dgp, dup)
        g['ln1_g'], g['ln1_b'] = dg1[0], db1[0]
        placed = emit_grads(l, 'ffn', [g[name] for name in PARTS['ffn']])
        dya, dys, dyl, g['w_out'], dbo, dgm = _mixout_bwd(dr1, s['mix'], s['ya'], s['ys'], s['yl'],
                                                         row(p['mix_norm_g'][l]) + placed, s['bw']['w_out_t'])
        g['b_out'], g['mix_norm_g'] = dbo[0], dgm[0]

        du, dbr, dbi, dcr, dci, dar, dai, dd, g['s5_glu_w'], dgb = _s5_bwd(s['u'], dys, s['s5_cr'], s['s5_ci'], s['s5'])
        dxr, dgate, lw0, lw1, lw2, lw3, lcb, dwx, dwa, dbx, dba, dsp = _lru_bwd(s['xr'], s['gate'], dyl, s['lru_c'], s['lru'])
        g['lru_conv_w'] = jnp.concatenate([lw0, lw1, lw2, lw3], axis=0)
        g['lru_conv_b'], g['lru_bx'], g['lru_ba'] = lcb[0], dbx[0], dba[0]
        g['lru_wx'] = jnp.einsum('hihj->hij', dwx.reshape(4, 64, 4, 64))
        g['lru_wa'] = jnp.einsum('hihj->hij', dwa.reshape(4, 64, 4, 64))
        g['lru_a_param'] = -dsp[0] * jax.nn.sigmoid(-p['lru_a_param'][l])

        g['s5_c_re'] = jnp.einsum('gpgc->gcp', dcr.reshape(16, 64, 16, 16))
        g['s5_c_im'] = jnp.einsum('gpgc->gcp', dci.reshape(16, 64, 16, 16))
        g['s5_d'], g['s5_glu_b'] = dd[0], dgb[0]
        d_disc[l] = (dar.reshape(16, 64), dai.reshape(16, 64), jnp.einsum('gcgp->gpc', dbr.reshape(16, 16, 16, 64)),
                     jnp.einsum('gcgp->gpc', dbi.reshape(16, 16, 16, 64)))

        dq, dk, dv, dsink = _attn_bwd(s['q'], s['k'], s['v'], s['sinks'], s['ya'], dya)
        g['attn_sinks'] = dsink[:, 0]
        dh, g['w_in'], dbin = _inproj_bwd(dq, dk[ATTN_BLOCK:], dv[ATTN_BLOCK:], du, dxr, dgate, cos_t, sin_t, s['x0'], dr1,
                                          s['bw']['w_in_t'])
        g['b_in'] = dbin[0]
        placed = emit_grads(l, 'mix', [g[name] for name in PARTS['mix']])
        for name in g:
            grads[name][l] = g[name]
    big = dict(BIG)
    out = {name: grads[name] if name in big else jnp.stack(grads[name]) for name in WEIGHTS if name not in S5_DISC}
    out.update(zip(S5_DISC, s5_vjp(tuple(jnp.stack([d_disc[l][i] for l in range(DEPTH)]) for i in range(4)))))
    return loss, dh, out


def kernel(x, w_in, b_in, attn_sinks, s5_a_re, s5_a_im, s5_b_re, s5_b_im, s5_c_re, s5_c_im, s5_d, s5_log_dt, s5_glu_w, s5_glu_b, lru_conv_w, lru_conv_b, lru_wx, lru_bx, lru_wa, lru_ba, lru_a_param, mix_norm_g, w_out, b_out, ln1_g, ln1_b, ffn_w_gate, ffn_w_up, ffn_conv_w, ffn_conv_b, ffn_w_down, ln2_g, ln2_b, loss_target, m_w_in, m_b_in, m_attn_sinks, m_s5_a_re, m_s5_a_im, m_s5_b_re, m_s5_b_im, m_s5_c_re, m_s5_c_im, m_s5_d, m_s5_log_dt, m_s5_glu_w, m_s5_glu_b, m_lru_conv_w, m_lru_conv_b, m_lru_wx, m_lru_bx, m_lru_wa, m_lru_ba, m_lru_a_param, m_mix_norm_g, m_w_out, m_b_out, m_ln1_g, m_ln1_b, m_ffn_w_gate, m_ffn_w_up, m_ffn_conv_w, m_ffn_conv_b, m_ffn_w_down, m_ln2_g, m_ln2_b, v_w_in, v_b_in, v_attn_sinks, v_s5_a_re, v_s5_a_im, v_s5_b_re, v_s5_b_im, v_s5_c_re, v_s5_c_im, v_s5_d, v_s5_log_dt, v_s5_glu_w, v_s5_glu_b, v_lru_conv_w, v_lru_conv_b, v_lru_wx, v_lru_bx, v_lru_wa, v_lru_ba, v_lru_a_param, v_mix_norm_g, v_w_out, v_b_out, v_ln1_g, v_ln1_b, v_ffn_w_gate, v_ffn_w_up, v_ffn_conv_w, v_ffn_conv_b, v_ffn_w_down, v_ln2_g, v_ln2_b):
    given = dict(locals())
    whole = {name: given[name] for name in WEIGHTS if name not in dict(BIG)}
    small_whole, big_weights, placed = _gather_weights({name: given[name] for name in SHARDED})
    whole.update(small_whole)
    whole['b_in'] = whole['b_in'] + placed
    in_flight = {}

    def emit_grads(l, part, grads):
        in_flight[(l, part)] = _scatter_start(f"grads_start_{part}{l}", grads)
        return in_flight[(l, part)][-1][0, 0]

    loss, grad_x, grads = _local_step(x[0], loss_target[0], whole, big_weights, emit_grads)
    total = lax.psum(loss[0, 0], ("x", "y", "c"))
    return (total, grad_x[None], *_update(given, grads, in_flight, grad_x))


def _update(given, grads, in_flight, after):
    local_w = {name: given[name] for name in WEIGHTS}
    me = _me()
    outs = {}

    shard_rows = sum(_packed_rows(local_w[name].shape) for name in SMALL_SHARDED)
    rep_pad = -sum(_packed_rows(local_w[name].shape) for name in REPLICATED) % PACK_TILE

    def packed_rep(arrays):
        return jnp.concatenate([_pack_rows(arrays[name]) for name in REPLICATED] + [jnp.zeros((rep_pad, LANES), F32)], axis=0)

    rep_grads = packed_rep(grads)
    chunk = rep_grads.shape[0] // N_DEV
    small = jnp.concatenate(
        [_pack_rows(_to_blocks(grads[name], SHARD_AXIS[name]), lead=1) for name in SMALL_SHARDED]
        + [rep_grads.reshape(N_DEV, chunk, LANES)], axis=1)
    small_rows = shard_rows + chunk
    small_flight = _scatter_start("grads_start_small", [small.reshape(N_DEV * small_rows, LANES)])

    def summed(name, flight, rows_of, packed, after):
        send_sems, recv_sems, srcs, landing, _ = flight
        srcs, landing = _scatter_wait(name, send_sems, recv_sems, srcs, landing, after)
        own = jnp.concatenate([lax.dynamic_slice_in_dim(g, me * r, r, axis=0) for g, r in zip(srcs, rows_of)], axis=0)
        parts = lax.dynamic_update_slice(landing, own[None], (me, 0, 0))
        return _reduce_adamw(parts, *packed, tile_rows=own.shape[0] // 4)

    forms = {part: [(name, dict(BIG)[name], _row_form(local_w[name], dict(BIG)[name]).shape[1]) for name in names]
             for part, names in PARTS.items()}
    for (l, part), flight in in_flight.items():
        packed = [jnp.concatenate([_row_form(given[prefix + name], t)[l] for name, t, _ in forms[part]], axis=0)
                  for prefix in ('', 'm_', 'v_')]
        outs[(l, part)] = summed(f"grads_wait_{part}{l}", flight, [r for _, _, r in forms[part]], packed, after)
        after = outs[(l, part)][0]

    rep_state = [packed_rep({name: given[prefix + name] for name in REPLICATED}) for prefix in ('', 'm_', 'v_')]
    small_state = [jnp.concatenate([_pack_rows(given[prefix + name]) for name in SMALL_SHARDED]
                                   + [lax.dynamic_slice_in_dim(rep, me * chunk, chunk, axis=0)], axis=0)
                   for prefix, rep in zip(('', 'm_', 'v_'), rep_state)]
    small_outs = summed("grads_wait_small", small_flight, [small_rows], small_state, after)
    rep_sum = _all_gather("gather_small_grads", [], small_outs[0][shard_rows:])[0].reshape(N_DEV * chunk, LANES)
    rep_outs = _reduce_adamw(rep_sum[None], *rep_state, tile_rows=N_DEV * chunk // 4)

    def unpack(i):
        res = {}
        for part in PARTS:
            lo = 0
            for name, t, r in forms[part]:
                res[name] = _row_form(jnp.stack([outs[(l, part)][i][lo:lo + r] for l in range(DEPTH)]), t)
                lo += r
        for names, packed in ((SMALL_SHARDED, small_outs[i]), (REPLICATED, rep_outs[i])):
            lo = 0
            for name in names:
                shape = local_w[name].shape
                res[name] = packed[lo:lo + _packed_rows(shape)].reshape(-1)[:math.prod(shape)].reshape(shape)
                lo += _packed_rows(shape)
        return [res[name] for name in WEIGHTS]

    return (*unpack(0), *unpack(1), *unpack(2), *unpack(3))
```

```python
import functools
import math

import jax
import jax.numpy as jnp
from jax import lax
from jax.experimental import pallas as pl
from jax.experimental.pallas import tpu as pltpu

F32 = jnp.float32
MXU = jnp.bfloat16

N_DEV = 8
DEPTH = 4
D = 1024
D_ATTN, D_KV, D_S5, D_LRU = 512, 128, 256, 256
D_IN = 1536
D_FF = 2816
FF_CHUNK = 256
N_STATE = 1024
LANES = 1024
ALPHA = (2 * DEPTH) ** 0.25
LN_EPS = 1e-5
RMS_EPS = 1e-6
LRU_C = 8.0
ROPE_THETA = 10000.0
ADAM_LR, ADAM_B1, ADAM_B2, ADAM_EPS, ADAM_WD, ADAM_STEP = 0.001, 0.9, 0.999, 1e-08, 0.01, 10

TILE = 256
S5_PAIR = 4
N_SEG = 8
SEG = TILE // N_SEG
TILE_Q = 512
TILE_BIG = 512
TILE_WIDE = 512
ATTN_BLOCK = 128
PACK_TILE = 256
VMEM_MB = 56

WEIGHTS = ['w_in', 'b_in', 'attn_sinks', 's5_a_re', 's5_a_im', 's5_b_re', 's5_b_im', 's5_c_re', 's5_c_im', 's5_d', 's5_log_dt',
           's5_glu_w', 's5_glu_b', 'lru_conv_w', 'lru_conv_b', 'lru_wx', 'lru_bx', 'lru_wa', 'lru_ba', 'lru_a_param', 'mix_norm_g',
           'w_out', 'b_out', 'ln1_g', 'ln1_b', 'ffn_w_gate', 'ffn_w_up', 'ffn_conv_w', 'ffn_conv_b', 'ffn_w_down', 'ln2_g', 'ln2_b']
SHARD_AXIS = {'w_in': 2, 's5_glu_w': 1, 'lru_conv_w': 2, 'w_out': 1, 'ffn_w_gate': 2, 'ffn_w_up': 2, 'ffn_conv_w': 2,
              'ffn_w_down': 1}
SHARDED = [n for n in WEIGHTS if n in SHARD_AXIS]
REPLICATED = [n for n in WEIGHTS if n not in SHARD_AXIS]
BIG = [(n, SHARD_AXIS[n] == 2) for n in ('w_in', 'w_out', 'ffn_w_gate', 'ffn_w_up', 'ffn_w_down')]
SMALL_SHARDED = [n for n in SHARDED if n not in dict(BIG)]
PARTS = {'mix': ['w_in', 'w_out'], 'ffn': ['ffn_w_gate', 'ffn_w_up', 'ffn_w_down']}
GATHER_F32 = ('lru_conv_w', 'ffn_conv_w')


def _dot(a, b):
    return jnp.dot(a, b, preferred_element_type=F32)


def _dot_nt(a, b):
    return lax.dot_general(a, b, (((1,), (1,)), ((), ())), preferred_element_type=F32)


def _dot_tn(a, b):
    return lax.dot_general(a, b, (((0,), (0,)), ((), ())), preferred_element_type=F32)


def _mx(a):
    return a.astype(MXU)


_GELU_C = math.sqrt(2.0 / math.pi)


def _gelu(x):
    th = jnp.tanh(_GELU_C * (x + 0.044715 * x * x * x))
    return 0.5 * x * (1.0 + th)


def _gelu_grad(x):
    th = jnp.tanh(_GELU_C * (x + 0.044715 * x * x * x))
    return 0.5 * (1.0 + th) + 0.5 * x * (1.0 - th * th) * _GELU_C * (1.0 + 3.0 * 0.044715 * x * x)


def _sigmoid(x):
    return 0.5 * jnp.tanh(0.5 * x) + 0.5


def _ln_stats(r):
    mu = jnp.mean(r, axis=-1, keepdims=True)
    xc = r - mu
    var = jnp.mean(xc * xc, axis=-1, keepdims=True)
    rstd = lax.rsqrt(var + LN_EPS)
    return xc * rstd, rstd


def _ln_bwd(dy, g, xhat, rstd):
    dxh = dy * g
    return rstd * (dxh - jnp.mean(dxh, axis=-1, keepdims=True) - xhat * jnp.mean(dxh * xhat, axis=-1, keepdims=True))


def _rms(y):
    return lax.rsqrt(jnp.mean(y * y, axis=-1, keepdims=True) + RMS_EPS)


def _sum0(a):
    return jnp.sum(a, axis=0, keepdims=True)


def _row_iota(shape):
    return lax.broadcasted_iota(jnp.int32, shape, 0)


def _shift_down(ext, j, rows):
    return pltpu.roll(ext, j, 0)[8:8 + rows]


def _shift_up(ext, j, rows):
    return pltpu.roll(ext, ext.shape[0] - j, 0)[:rows]


def _swap_halves(t):
    w = t.shape[1]
    lane = lax.broadcasted_iota(jnp.int32, t.shape, 1)
    return jnp.where((lane & 32) == 0, pltpu.roll(t, w - 32, 1), pltpu.roll(t, 32, 1))


def _rope(t, cos, sin_signed):
    return t * cos + _swap_halves(t) * sin_signed


def _rope_t(d, cos, sin_signed):
    return d * cos + _swap_halves(d * sin_signed)


def _cmul_add(ar, ai, xr, xi, yr, yi):
    return ar * xr - ai * xi + yr, ar * xi + ai * xr + yi


def _seg_rows(k):
    return slice(N_SEG * k, N_SEG * (k + 1))


def _permute_rows(perm, x):
    hi = _mx(x)
    rest = x - hi.astype(F32)
    mid = _mx(rest)
    lo = _mx(rest - mid.astype(F32))
    return _dot(perm, hi) + _dot(perm, mid) + _dot(perm, lo)


def _cscan(sr, si, tab, cin_r, cin_i, reverse):
    sgn = -1.0 if reverse else 1.0
    pw_re, pw_im, dbl_re, dbl_im = tab['pw_re'], tab['pw_im'], tab['dbl_re'], tab['dbl_im']
    ar, ai = pw_re[0:1, :], sgn * pw_im[0:1, :]
    shape = (N_SEG, sr.shape[1])
    hr = hi = jnp.zeros(shape, F32)
    for k in (range(SEG - 1, -1, -1) if reverse else range(SEG)):
        hr, hi = _cmul_add(ar, ai, hr, hi, sr[_seg_rows(k), :], si[_seg_rows(k), :])
        sr[_seg_rows(k), :] = hr
        si[_seg_rows(k), :] = hi
    sub = _row_iota(shape)

    def shifted(v, d):
        if reverse:
            return jnp.where(sub < N_SEG - d, pltpu.roll(v, N_SEG - d, 0), 0.0)
        return jnp.where(sub >= d, pltpu.roll(v, d, 0), 0.0)

    fr, fi = hr, hi
    for j, d in enumerate((1, 2, 4)):
        fr, fi = _cmul_add(dbl_re[j:j + 1, :], sgn * dbl_im[j:j + 1, :], shifted(fr, d), shifted(fi, d), fr, fi)
    seg_re, seg_im = (tab['segr_re'], tab['segr_im']) if reverse else (tab['seg_re'], tab['seg_im'])
    cr, ci = _cmul_add(seg_re[...], sgn * seg_im[...], cin_r, cin_i, shifted(fr, 1), shifted(fi, 1))
    nr, ni = _cmul_add(dbl_re[0:1, :], sgn * dbl_im[0:1, :], cr, ci, hr, hi)
    for k in range(SEG):
        j = SEG - 1 - k if reverse else k
        xr, xi = _cmul_add(pw_re[j:j + 1, :], sgn * pw_im[j:j + 1, :], cr, ci, sr[_seg_rows(k), :], si[_seg_rows(k), :])
        sr[_seg_rows(k), :] = xr
        si[_seg_rows(k), :] = xi
    edge = slice(0, 1) if reverse else slice(N_SEG - 1, N_SEG)
    return nr[edge], ni[edge]


def _rscan(a, b, reverse):
    rows = a.shape[0]
    row = _row_iota(a.shape)
    s = 1
    while s < rows:
        if reverse:
            keep = row < rows - s
            sa = jnp.where(keep, pltpu.roll(a, rows - s, 0), 1.0)
            sb = jnp.where(keep, pltpu.roll(b, rows - s, 0), 0.0)
        else:
            keep = row >= s
            sa = jnp.where(keep, pltpu.roll(a, s, 0), 1.0)
            sb = jnp.where(keep, pltpu.roll(b, s, 0), 0.0)
        b = b + a * sb
        a = a * sa
        s *= 2
    return a, b


def _whole():
    return pl.BlockSpec(memory_space=pltpu.VMEM)


def _rows_spec(rows, cols, n_tiles, reverse=False):
    if reverse:
        return pl.BlockSpec((rows, cols), lambda i: (n_tiles - 1 - i, 0))
    return pl.BlockSpec((rows, cols), lambda i: (i, 0))


def _halo_spec(cols, tile_rows, n_tiles, reverse=False):
    per = tile_rows // 8
    if reverse:
        return pl.BlockSpec((8, cols), lambda i: (jnp.maximum((n_tiles - 1 - i) * per - 1, 0), 0))
    return pl.BlockSpec((8, cols), lambda i: (jnp.maximum(i * per - 1, 0), 0))


def _call(body, name, n_tiles, in_specs, out_specs, out_shape, scratch=()):
    return pl.pallas_call(
        body, name=name, grid=(n_tiles,), in_specs=in_specs, out_specs=out_specs, out_shape=out_shape,
        scratch_shapes=list(scratch),
        compiler_params=pltpu.CompilerParams(dimension_semantics=("arbitrary",), vmem_limit_bytes=VMEM_MB << 20))


def _sds(shape, dtype=F32):
    return jax.ShapeDtypeStruct(shape, dtype)


def _inproj_fwd(x, w, b, cos_t, sin_t):
    n = x.shape[0]
    nt = n // TILE_WIDE

    def body(x_ref, w_ref, b_ref, c_ref, s_ref, q_ref, k_ref, v_ref, u_ref, xr_ref, g_ref):
        p = _dot(_mx(x_ref[...]), w_ref[...]) + b_ref[...]
        cos, sin = c_ref[...], s_ref[...]
        q_ref[...] = _mx(_rope(p[:, :D_ATTN], jnp.tile(cos, (1, 4)), jnp.tile(sin, (1, 4))))
        k_ref[...] = _mx(_rope(p[:, 512:640], cos, sin))
        v_ref[...] = _mx(p[:, 640:768])
        u_ref[...] = p[:, 768:1024]
        xr_ref[...] = p[:, 1024:1280]
        g_ref[...] = p[:, 1280:1536]

    r = functools.partial(_rows_spec, n_tiles=nt)
    return _call(
        body, "inproj_fwd", nt,
        [r(TILE_WIDE, D), _whole(), _whole(), r(TILE_WIDE, 128), r(TILE_WIDE, 128)],
        [r(TILE_WIDE, D_ATTN), r(TILE_WIDE, D_KV), r(TILE_WIDE, D_KV), r(TILE_WIDE, D_S5), r(TILE_WIDE, D_LRU), r(TILE_WIDE, D_LRU)],
        [_sds((n, D_ATTN), MXU), _sds((n, D_KV), MXU), _sds((n, D_KV), MXU), _sds((n, D_S5)), _sds((n, D_LRU)), _sds((n, D_LRU))],
    )(x, w, b, cos_t, sin_t)


def _inproj_bwd(dq, dk, dv, du, dxr, dgate, cos_t, sin_t, x0, dr1, w_t):
    n = x0.shape[0]
    nt = n // TILE_WIDE

    def body(dq_ref, dk_ref, dv_ref, du_ref, dxr_ref, dg_ref, c_ref, s_ref, x_ref, dr_ref, w_ref, dx_ref, dw_ref, db_ref):
        @pl.when(pl.program_id(0) == 0)
        def _():
            dw_ref[...] = jnp.zeros_like(dw_ref)
            db_ref[...] = jnp.zeros_like(db_ref)

        cos, sin = c_ref[...], s_ref[...]
        dtq = _rope_t(dq_ref[...], jnp.tile(cos, (1, 4)), jnp.tile(sin, (1, 4)))
        dtk = _rope_t(dk_ref[...], cos, sin)
        dp = jnp.concatenate([dtq, dtk, dv_ref[...], du_ref[...], dxr_ref[...], dg_ref[...]], axis=1)
        db_ref[...] += _sum0(dp)
        dpb = _mx(dp)
        dw_ref[...] += _dot_tn(dpb, _mx(x_ref[...]))
        dx_ref[...] = ALPHA * dr_ref[...] + _dot(dpb, w_ref[...])

    r = functools.partial(_rows_spec, n_tiles=nt)
    return _call(
        body, "inproj_bwd", nt,
        [r(TILE_WIDE, D_ATTN), r(TILE_WIDE, D_KV), r(TILE_WIDE, D_KV), r(TILE_WIDE, D_S5), r(TILE_WIDE, D_LRU), r(TILE_WIDE, D_LRU),
         r(TILE_WIDE, 128), r(TILE_WIDE, 128), r(TILE_WIDE, D), r(TILE_WIDE, D), _whole()],
        [r(TILE_WIDE, D), _whole(), _whole()],
        [_sds((n, D)), _sds((D_IN, D)), _sds((1, D_IN))],
    )(dq, dk, dv, du, dxr, dgate, cos_t, sin_t, x0, dr1, w_t)


def _kv_variants(t, lo):
    tr = pltpu.roll(t, 64, 1)
    out = []
    for j in range(2):
        first = jnp.where(lo, t if j == 0 else tr, 0.0)
        second = jnp.where(lo, 0.0, tr if j == 0 else t)
        out.append(_mx(jnp.concatenate([first, second], axis=0)))
    return out


def _kv_collect(x0, x1, lo):
    a = x0[:256] + pltpu.roll(x0[256:], 64, 1)
    b = pltpu.roll(x1[:256], 64, 1) + x1[256:]
    return jnp.where(lo, a, b)


def _row_sums(x):
    ones = jnp.ones((128, 128), MXU)
    hi = _mx(x)
    lo = _mx(x - hi.astype(F32))
    return _dot(hi, ones) + _dot(lo, ones)


def _attn_probs(s, sink_ref):
    out = []
    for hp in range(2):
        sh = s[:, hp * 128:(hp + 1) * 128]
        sink = sink_ref[hp]
        m = jnp.maximum(jnp.broadcast_to(jnp.max(sh, axis=1, keepdims=True), sh.shape), sink)
        p = jnp.exp(sh - m)
        es = jnp.exp(sink - m)
        inv = 1.0 / (_row_sums(p) + es)
        out.append((p * inv, es * inv))
    return out


def _band_merge(x, tri, no_previous=None):
    bands = []
    for hp in range(2):
        prev, own = x[:, hp * 256:hp * 256 + 128], x[:, hp * 256 + 128:hp * 256 + 256]
        if no_previous is not None:
            prev = jnp.where(no_previous, -jnp.inf, prev)
        bands.append(jnp.where(tri, own, prev))
    return jnp.concatenate(bands, axis=1)


def _band_split(y, tri):
    parts = []
    for hp in range(2):
        band = y[:, hp * 128:(hp + 1) * 128]
        parts += [jnp.where(tri, 0.0, band), jnp.where(tri, band, 0.0)]
    return jnp.concatenate(parts, axis=1)


def _tri():
    shape = (ATTN_BLOCK, ATTN_BLOCK)
    return lax.broadcasted_iota(jnp.int32, shape, 0) >= lax.broadcasted_iota(jnp.int32, shape, 1)


def _attn_scores(q_ref, k_ref, v_ref, nb, tri):
    lo = lax.broadcasted_iota(jnp.int32, (256, 128), 1) < 64
    kcats, vcats, kstarts, parts = [], [], [], []
    for b in range(nb):
        block = pl.program_id(0) * nb + b
        kstart = pl.multiple_of(block * ATTN_BLOCK, ATTN_BLOCK)
        kcat = _kv_variants(k_ref[pl.ds(kstart, 256), :].astype(F32), lo)
        kcats.append(kcat)
        vcats.append(_kv_variants(v_ref[pl.ds(kstart, 256), :].astype(F32), lo))
        kstarts.append(kstart)
        for j in range(2):
            s = _dot_nt(_kv_group(q_ref, b, j), kcat[j]) * 0.125
            parts += [_band_merge(s[:ATTN_BLOCK], tri, block == 0), _band_merge(s[ATTN_BLOCK:], tri, block == 0)]
    return jnp.concatenate(parts, axis=0), kcats, vcats, kstarts


def _kv_group(a, b, j):
    rows = slice(b * ATTN_BLOCK, (b + 1) * ATTN_BLOCK)
    return jnp.concatenate([a[rows, 2 * j * 128:(2 * j + 1) * 128], a[rows, (2 * j + 1) * 128:(2 * j + 2) * 128]], axis=0)


def _put_kv_group(ref, b, j, x):
    rows = slice(b * ATTN_BLOCK, (b + 1) * ATTN_BLOCK)
    ref[rows, 2 * j * 128:(2 * j + 1) * 128] = x[:ATTN_BLOCK]
    ref[rows, (2 * j + 1) * 128:(2 * j + 2) * 128] = x[ATTN_BLOCK:]


def _band_split_group(y, unit, tri):
    return _mx(jnp.concatenate([_band_split(y[unit:unit + ATTN_BLOCK], tri),
                                _band_split(y[unit + ATTN_BLOCK:unit + 2 * ATTN_BLOCK], tri)], axis=0))


def _attn_fwd(q, k, v, sink_cols):
    n = q.shape[0]
    nt = n // TILE_Q
    nb = TILE_Q // ATTN_BLOCK

    def body(q_ref, k_ref, v_ref, s_ref, o_ref):
        tri = _tri()
        s, _, vcats, _ = _attn_scores(q_ref, k_ref, v_ref, nb, tri)
        (p0, _), (p1, _) = _attn_probs(s, s_ref)
        p = jnp.concatenate([p0, p1], axis=1)
        for b in range(nb):
            for j in range(2):
                _put_kv_group(o_ref, b, j, _dot(_band_split_group(p, (b * 4 + 2 * j) * ATTN_BLOCK, tri), vcats[b][j]))

    return _call(
        body, "attn_fwd", nt,
        [_rows_spec(TILE_Q, D_ATTN, nt), _whole(), _whole(), _whole()],
        _rows_spec(TILE_Q, D_ATTN, nt), _sds((n, D_ATTN)),
    )(q, k, v, sink_cols)


def _attn_bwd(q, k, v, sink_cols, o, do):
    n = q.shape[0]
    nt = n // TILE_Q
    nb = TILE_Q // ATTN_BLOCK

    def body(q_ref, k_ref, v_ref, s_ref, o_ref, do_ref, dq_ref, dk_ref, dv_ref, ds_ref):
        @pl.when(pl.program_id(0) == 0)
        def _():
            dk_ref[...] = jnp.zeros_like(dk_ref)
            dv_ref[...] = jnp.zeros_like(dv_ref)
            ds_ref[...] = jnp.zeros_like(ds_ref)

        lo = lax.broadcasted_iota(jnp.int32, (256, 128), 1) < 64
        tri = _tri()
        s, kcats, vcats, kstarts = _attn_scores(q_ref, k_ref, v_ref, nb, tri)
        probs = _attn_probs(s, s_ref)
        do = do_ref[...]
        dob = _mx(do)
        od = do * o_ref[...]
        lo_q = (lax.broadcasted_iota(jnp.int32, od.shape, 1) & 64) == 0
        od_head = (jnp.where(lo_q, od, 0.0), jnp.where(lo_q, 0.0, od))
        units = [(b, i) for b in range(nb) for i in range(4)]

        def tile_part(a, b, i):
            return a[b * ATTN_BLOCK:(b + 1) * ATTN_BLOCK, i * 128:(i + 1) * 128]

        dp = []
        for b in range(nb):
            for j in range(2):
                x = _dot_nt(_kv_group(dob, b, j), vcats[b][j])
                dp += [_band_merge(x[:ATTN_BLOCK], tri), _band_merge(x[ATTN_BLOCK:], tri)]
        dp = jnp.concatenate(dp, axis=0)
        ds = []
        for hp in range(2):
            p, p_sink = probs[hp]
            delta = _row_sums(jnp.concatenate([tile_part(od_head[hp], b, i) for b, i in units], axis=0))
            ds.append(p * (dp[:, hp * 128:(hp + 1) * 128] - delta) * 0.125)
            t = p_sink * delta
            for i in range(4):
                ds_ref[2 * i + hp:2 * i + hp + 1, :] -= sum(
                    _sum0(t[(b * 4 + i) * ATTN_BLOCK:(b * 4 + i + 1) * ATTN_BLOCK]) for b in range(nb))
        ds = jnp.concatenate(ds, axis=1)
        p = jnp.concatenate([probs[0][0], probs[1][0]], axis=1)
        for b in range(nb):
            dkc, dvc = [], []
            for j in range(2):
                unit = (b * 4 + 2 * j) * ATTN_BLOCK
                dsb = _band_split_group(ds, unit, tri)
                _put_kv_group(dq_ref, b, j, _dot(dsb, kcats[b][j]))
                dkc.append(_dot_tn(dsb, _kv_group(q_ref, b, j)))
                dvc.append(_dot_tn(_band_split_group(p, unit, tri), _kv_group(dob, b, j)))
            dk_ref[pl.ds(kstarts[b], 256), :] += _kv_collect(dkc[0], dkc[1], lo)
            dv_ref[pl.ds(kstarts[b], 256), :] += _kv_collect(dvc[0], dvc[1], lo)

    r = _rows_spec(TILE_Q, D_ATTN, nt)
    return _call(
        body, "attn_bwd", nt,
        [r, _whole(), _whole(), _whole(), r, r],
        [r, _whole(), _whole(), _whole()],
        [_sds((n, D_ATTN)), _sds((n + ATTN_BLOCK, D_KV)), _sds((n + ATTN_BLOCK, D_KV)), _sds((8, 128))],
    )(q, k, v, sink_cols, o, do)


S5_DISC = ('s5_a_re', 's5_a_im', 's5_log_dt', 's5_b_re', 's5_b_im')
S5_TABLES = ('pw_re', 'pw_im', 'dbl_re', 'dbl_im', 'seg_re', 'seg_im', 'segr_re', 'segr_im')
S5_WEIGHTS = ('b_re', 'b_im', 'c_re', 'c_im', 'd', 'glu_w', 'glu_b', 'perm', 'perm_t')


def _s5_states(u, carry_r, carry_i, b_re, b_im, tab, hr_s, hi_s):
    ub = _mx(u)
    hr_s[...] = _dot(ub, b_re[...])
    hi_s[...] = _dot(ub, b_im[...])
    return ub, _cscan(hr_s, hi_s, tab, carry_r, carry_i, reverse=False)


def _s5_fwd(u, prm):
    n = u.shape[0]
    nt = n // (S5_PAIR * TILE)

    def body(u_ref, *refs):
        tab = dict(zip(S5_TABLES, refs[:8]))
        b_re, b_im, c_re, c_im, d_ref, gw_ref, gb_ref, perm, perm_t = refs[8:17]
        y_ref, cr_out, ci_out, cr_s, ci_s = refs[17:22]
        states = [refs[22 + 2 * j:24 + 2 * j] for j in range(S5_PAIR)]
        rows = [slice(j * TILE, (j + 1) * TILE) for j in range(S5_PAIR)]

        @pl.when(pl.program_id(0) == 0)
        def _():
            cr_s[...] = jnp.zeros_like(cr_s)
            ci_s[...] = jnp.zeros_like(ci_s)

        us = []
        for j in range(S5_PAIR):
            us.append(_permute_rows(perm[...], u_ref[rows[j], :]))
            ub = _mx(us[j])
            states[j][0][...] = _dot(ub, b_re[...])
            states[j][1][...] = _dot(ub, b_im[...])
        for j in range(S5_PAIR):
            cr, ci = cr_s[...], ci_s[...]
            cr_out[8 * j:8 * j + 8, :] = jnp.broadcast_to(cr, (8, N_STATE))
            ci_out[8 * j:8 * j + 8, :] = jnp.broadcast_to(ci, (8, N_STATE))
            cr_s[...], ci_s[...] = _cscan(*states[j], tab, cr, ci, reverse=False)
        for j in range(S5_PAIR):
            hr_s, hi_s = states[j]
            y = _dot(_mx(hr_s[...]), c_re[...]) - _dot(_mx(hi_s[...]), c_im[...]) + d_ref[...] * us[j]
            z = _gelu(y)
            y_ref[rows[j], :] = _permute_rows(perm_t[...], z * _sigmoid(_dot(_mx(z), gw_ref[...]) + gb_ref[...]))

    r = functools.partial(_rows_spec, n_tiles=nt)
    return _call(
        body, "s5_fwd", nt,
        [r(S5_PAIR * TILE, D_S5)] + [_whole()] * 17,
        [r(S5_PAIR * TILE, D_S5), r(S5_PAIR * 8, N_STATE), r(S5_PAIR * 8, N_STATE)],
        [_sds((n, D_S5)), _sds((n // TILE * 8, N_STATE)), _sds((n // TILE * 8, N_STATE))],
        scratch=[pltpu.VMEM((1, N_STATE), F32)] * 2 + [pltpu.VMEM((TILE, N_STATE), F32)] * (2 * S5_PAIR),
    )(u, *[prm[k] for k in S5_TABLES + S5_WEIGHTS])


def _s5_bwd(u, dys, carry_re, carry_im, prm):
    n = u.shape[0]
    nt = n // (S5_PAIR * TILE)

    def body(u_ref, dy_ref, cin_r, cin_i, *refs):
        tab = dict(zip(S5_TABLES, refs[:8]))
        b_re, b_im, c_re, c_im, d_ref, gw_ref, gb_ref, perm, perm_t = refs[8:17]
        du_ref, dbr_ref, dbi_ref, dcr_ref, dci_ref, dar_ref, dai_ref, dd_ref, dgw_ref, dgb_ref = refs[17:27]
        gr_s, gi_s = refs[27:29]
        scratch = [refs[29 + 4 * j:33 + 4 * j] for j in range(S5_PAIR)]
        later_first = list(reversed(range(S5_PAIR)))
        rows = [slice(j * TILE, (j + 1) * TILE) for j in range(S5_PAIR)]

        @pl.when(pl.program_id(0) == 0)
        def _():
            for ref in (dbr_ref, dbi_ref, dcr_ref, dci_ref, dar_ref, dai_ref, dd_ref, dgw_ref, dgb_ref, gr_s, gi_s):
                ref[...] = jnp.zeros_like(ref)

        us, ubs, carries, dy_of = {}, {}, {}, {}
        for j in later_first:
            us[j] = _permute_rows(perm[...], u_ref[rows[j], :])
            carries[j] = (cin_r[8 * j:8 * j + 1, :], cin_i[8 * j:8 * j + 1, :])
            ubs[j], _ = _s5_states(us[j], *carries[j], b_re, b_im, tab, *scratch[j][:2])
        for j in later_first:
            u = us[j]
            hr_s, hi_s, gr_t, gi_t = scratch[j]
            hrb, hib = _mx(hr_s[...]), _mx(hi_s[...])
            y = _dot(hrb, c_re[...]) - _dot(hib, c_im[...]) + d_ref[...] * u
            z = _gelu(y)
            zb = _mx(z)
            sg = _sigmoid(_dot(zb, gw_ref[...]) + gb_ref[...])
            dout = _permute_rows(perm[...], dy_ref[rows[j], :])
            dpre = dout * z * sg * (1.0 - sg)
            dgb_ref[...] += _sum0(dpre)
            dpb = _mx(dpre)
            dgw_ref[...] += _dot_tn(zb, dpb)
            dy = (dout * sg + _dot_nt(dpb, gw_ref[...])) * _gelu_grad(y)
            dd_ref[...] += _sum0(dy * u)
            dyb = _mx(dy)
            dcr_ref[...] += _dot_tn(hrb, dyb)
            dci_ref[...] -= _dot_tn(hib, dyb)
            gr_t[...] = _dot_nt(dyb, c_re[...])
            gi_t[...] = -_dot_nt(dyb, c_im[...])
            dy_of[j] = dy
        for j in later_first:
            gr_s[...], gi_s[...] = _cscan(*scratch[j][2:], tab, gr_s[...], gi_s[...], reverse=True)
        for j in later_first:
            hr_s, hi_s, gr_t, gi_t = scratch[j]
            cr, ci = carries[j]
            sub = _row_iota((N_SEG, N_STATE))
            acc_r = acc_i = jnp.zeros((N_SEG, N_STATE), F32)
            for k in range(SEG):
                if k == 0:
                    hpr = jnp.where(sub >= 1, pltpu.roll(hr_s[_seg_rows(SEG - 1), :], 1, 0), cr)
                    hpi = jnp.where(sub >= 1, pltpu.roll(hi_s[_seg_rows(SEG - 1), :], 1, 0), ci)
                else:
                    hpr, hpi = hr_s[_seg_rows(k - 1), :], hi_s[_seg_rows(k - 1), :]
                gr, gi = gr_t[_seg_rows(k), :], gi_t[_seg_rows(k), :]
                acc_r = acc_r + gr * hpr + gi * hpi
                acc_i = acc_i + gi * hpr - gr * hpi
            dar_ref[...] += _sum0(acc_r)
            dai_ref[...] += _sum0(acc_i)
            grb, gib = _mx(gr_t[...]), _mx(gi_t[...])
            dbr_ref[...] += _dot_tn(ubs[j], grb)
            dbi_ref[...] += _dot_tn(ubs[j], gib)
            du_ref[rows[j], :] = _permute_rows(perm_t[...], dy_of[j] * d_ref[...] + _dot_nt(grb, b_re[...]) + _dot_nt(gib, b_im[...]))

    r = functools.partial(_rows_spec, n_tiles=nt, reverse=True)
    return _call(
        body, "s5_bwd", nt,
        [r(S5_PAIR * TILE, D_S5), r(S5_PAIR * TILE, D_S5), r(S5_PAIR * 8, N_STATE), r(S5_PAIR * 8, N_STATE)] + [_whole()] * 17,
        [r(S5_PAIR * TILE, D_S5)] + [_whole()] * 9,
        [_sds((n, D_S5)), _sds((D_S5, N_STATE)), _sds((D_S5, N_STATE)), _sds((N_STATE, D_S5)), _sds((N_STATE, D_S5)),
         _sds((1, N_STATE)), _sds((1, N_STATE)), _sds((1, D_S5)), _sds((D_S5, D_S5)), _sds((1, D_S5))],
        scratch=[pltpu.VMEM((1, N_STATE), F32)] * 2 + [pltpu.VMEM((TILE, N_STATE), F32)] * (4 * S5_PAIR),
    )(u, dys, carry_re, carry_im, *[prm[k] for k in S5_TABLES + S5_WEIGHTS])


def _lru_gates(xr, halo, tile_index, cw_ref, cb_ref, wx_ref, wa_ref, bx_ref, ba_ref, sp_ref):
    ext = jnp.concatenate([halo, xr], axis=0)
    sh = [xr] + [_shift_down(ext, j, TILE) for j in (1, 2, 3)]
    xc = cb_ref[...] + cw_ref[3:4, :] * sh[0] + cw_ref[2:3, :] * sh[1] + cw_ref[1:2, :] * sh[2] + cw_ref[0:1, :] * sh[3]
    xb = _mx(xc)
    gx = _sigmoid(_dot(xb, wx_ref[...]) + bx_ref[...])
    ga = _sigmoid(_dot(xb, wa_ref[...]) + ba_ref[...])
    la = -LRU_C * ga * sp_ref[...]
    a = jnp.exp(la)
    start = (tile_index * TILE + _row_iota(xr.shape)) == 0
    mult = jnp.where(start, 1.0, jnp.sqrt(-jnp.tanh(la) * (a * a + 1.0)))
    return sh, xc, xb, gx, ga, a, mult, start


def _lru_fwd(xr, gate, prm):
    n = xr.shape[0]
    nt = n // TILE

    def body(x_ref, g_ref, cw_ref, cb_ref, wx_ref, wa_ref, bx_ref, ba_ref, sp_ref, y_ref, c_out, halo_s, c_s):
        first_tile = pl.program_id(0) == 0

        @pl.when(first_tile)
        def _():
            halo_s[...] = jnp.zeros_like(halo_s)
            c_s[...] = jnp.zeros_like(c_s)

        xr = x_ref[...]
        _, xc, _, gx, _, a, mult, _ = _lru_gates(xr, halo_s[...], pl.program_id(0), cw_ref, cb_ref, wx_ref, wa_ref, bx_ref, ba_ref,
                                                 sp_ref)
        halo_s[...] = xr[TILE - 8:]
        acum, h = _rscan(a, mult * gx * xc, reverse=False)
        c = c_s[...]
        c_out[...] = jnp.broadcast_to(c, (8, D_LRU))
        h = h + acum * c
        c_s[...] = h[TILE - 1:TILE]
        y_ref[...] = h * _gelu(g_ref[...])

    r = functools.partial(_rows_spec, n_tiles=nt)
    return _call(
        body, "lru_fwd", nt,
        [r(TILE, D_LRU), r(TILE, D_LRU)] + [_whole()] * 7,
        [r(TILE, D_LRU), r(8, D_LRU)],
        [_sds((n, D_LRU)), _sds((nt * 8, D_LRU))],
        scratch=[pltpu.VMEM((8, D_LRU), F32), pltpu.VMEM((1, D_LRU), F32)],
    )(xr, gate, prm['conv_w'], prm['conv_b'], prm['wx'], prm['wa'], prm['bx'], prm['ba'], prm['sp'])


def _lru_bwd(xr, gate, dyl, carry, prm):
    n = xr.shape[0]
    nt = n // TILE

    def body(x_ref, xh_ref, g_ref, dy_ref, cin_ref, cw_ref, cb_ref, wx_ref, wa_ref, bx_ref, ba_ref, sp_ref,
             dx_ref, dg_ref, dcw0, dcw1, dcw2, dcw3, dcb_ref, dwx_ref, dwa_ref, dbx_ref, dba_ref, dsp_ref, an_s, gn_s, dn_s):
        first_tile = pl.program_id(0) == nt - 1

        @pl.when(pl.program_id(0) == 0)
        def _():
            for ref in (dcw0, dcw1, dcw2, dcw3, dcb_ref, dwx_ref, dwa_ref, dbx_ref, dba_ref, dsp_ref, gn_s, dn_s):
                ref[...] = jnp.zeros_like(ref)
            an_s[...] = jnp.ones_like(an_s)

        xr = x_ref[...]
        halo = jnp.where(first_tile, 0.0, xh_ref[...])
        sh, xc, xb, gx, ga, a, mult, start = _lru_gates(xr, halo, nt - 1 - pl.program_id(0), cw_ref, cb_ref, wx_ref, wa_ref, bx_ref,
                                                        ba_ref, sp_ref)
        acum, h = _rscan(a, mult * gx * xc, reverse=False)
        cin = cin_ref[0:1, :]
        h = h + acum * cin
        gate = g_ref[...]
        dyl = dy_ref[...]
        dg_ref[...] = dyl * h * _gelu_grad(gate)
        row = _row_iota(xr.shape)
        alpha = jnp.where(row < TILE - 1, pltpu.roll(a, TILE - 1, 0), an_s[...])
        racc, g = _rscan(alpha, dyl * _gelu(gate), reverse=True)
        g = g + racc * gn_s[...]
        an_s[...] = a[0:1]
        gn_s[...] = g[0:1]
        hprev = jnp.where(row == 0, cin, pltpu.roll(h, 1, 0))
        da = g * hprev
        dmult = jnp.where(start, 0.0, g * gx * xc)
        dla = da * a - dmult * a * a / mult
        dsp_ref[...] += _sum0(-LRU_C * ga * dla)
        dpa = (-LRU_C * sp_ref[...] * dla) * ga * (1.0 - ga)
        dpx = (g * mult * xc) * gx * (1.0 - gx)
        dba_ref[...] += _sum0(dpa)
        dbx_ref[...] += _sum0(dpx)
        dpab, dpxb = _mx(dpa), _mx(dpx)
        dwa_ref[...] += _dot_tn(xb, dpab)
        dwx_ref[...] += _dot_tn(xb, dpxb)
        dxc = g * mult * gx + _dot_nt(dpab, wa_ref[...]) + _dot_nt(dpxb, wx_ref[...])
        dcb_ref[...] += _sum0(dxc)
        dcw3[...] += _sum0(dxc * sh[0])
        dcw2[...] += _sum0(dxc * sh[1])
        dcw1[...] += _sum0(dxc * sh[2])
        dcw0[...] += _sum0(dxc * sh[3])
        ext = jnp.concatenate([dxc, dn_s[...]], axis=0)
        dx_ref[...] = (cw_ref[3:4, :] * dxc + cw_ref[2:3, :] * _shift_up(ext, 1, TILE) + cw_ref[1:2, :] * _shift_up(ext, 2, TILE)
                       + cw_ref[0:1, :] * _shift_up(ext, 3, TILE))
        dn_s[...] = dxc[:8]

    r = functools.partial(_rows_spec, n_tiles=nt, reverse=True)
    vec = _sds((1, D_LRU))
    return _call(
        body, "lru_bwd", nt,
        [r(TILE, D_LRU), _halo_spec(D_LRU, TILE, nt, reverse=True), r(TILE, D_LRU), r(TILE, D_LRU), r(8, D_LRU)] + [_whole()] * 7,
        [r(TILE, D_LRU), r(TILE, D_LRU)] + [_whole()] * 10,
        [_sds((n, D_LRU)), _sds((n, D_LRU)), vec, vec, vec, vec, vec, _sds((D_LRU, D_LRU)), _sds((D_LRU, D_LRU)), vec, vec, vec],
        scratch=[pltpu.VMEM((1, D_LRU), F32), pltpu.VMEM((1, D_LRU), F32), pltpu.VMEM((8, D_LRU), F32)],
    )(xr, xr, gate, dyl, carry, prm['conv_w'], prm['conv_b'], prm['wx'], prm['wa'], prm['bx'], prm['ba'], prm['sp'])


def _normed_parts(ya, ys, yl):
    return jnp.concatenate([ya * _rms(ya), ys * _rms(ys), yl * _rms(yl)], axis=1)


def _mixout_fwd(ya, ys, yl, x0, g_mix, w_out, b_out, g1, b1):
    n = x0.shape[0]
    nt = n // TILE_WIDE

    def body(ya_ref, ys_ref, yl_ref, x_ref, gm_ref, w_ref, b_ref, g_ref, be_ref, mix_ref, r_ref, x1_ref):
        mixb = _mx(_normed_parts(ya_ref[...], ys_ref[...], yl_ref[...]) * gm_ref[...])
        mix_ref[...] = mixb
        r1 = ALPHA * x_ref[...] + _dot(mixb, w_ref[...]) + b_ref[...]
        r_ref[...] = r1
        xhat, _ = _ln_stats(r1)
        x1_ref[...] = xhat * g_ref[...] + be_ref[...]

    r = functools.partial(_rows_spec, n_tiles=nt)
    return _call(
        body, "mixout_fwd", nt,
        [r(TILE_WIDE, D_ATTN), r(TILE_WIDE, D_S5), r(TILE_WIDE, D_LRU), r(TILE_WIDE, D)] + [_whole()] * 5,
        [r(TILE_WIDE, D), r(TILE_WIDE, D), r(TILE_WIDE, D)],
        [_sds((n, D), MXU), _sds((n, D)), _sds((n, D))],
    )(ya, ys, yl, x0, g_mix, w_out, b_out, g1, b1)


def _mixout_bwd(dr1, mix, ya, ys, yl, g_mix, w_out):
    n = dr1.shape[0]
    nt = n // TILE_WIDE

    def body(dr_ref, mix_ref, ya_ref, ys_ref, yl_ref, gm_ref, w_ref, dya_ref, dys_ref, dyl_ref, dw_ref, db_ref, dgm_ref):
        @pl.when(pl.program_id(0) == 0)
        def _():
            for ref in (dw_ref, db_ref, dgm_ref):
                ref[...] = jnp.zeros_like(ref)

        dr = dr_ref[...]
        db_ref[...] += _sum0(dr)
        drb = _mx(dr)
        dw_ref[...] += _dot_tn(mix_ref[...], drb)
        dmix = _dot(drb, w_ref[...])
        parts = (ya_ref[...], ys_ref[...], yl_ref[...])
        dgm_ref[...] += _sum0(dmix * _normed_parts(*parts))
        dn = dmix * gm_ref[...]
        lo = 0
        for y, out in zip(parts, (dya_ref, dys_ref, dyl_ref)):
            w = y.shape[1]
            rs = _rms(y)
            nrm = y * rs
            dnp = dn[:, lo:lo + w]
            out[...] = rs * (dnp - nrm * jnp.mean(dnp * nrm, axis=-1, keepdims=True))
            lo += w

    r = functools.partial(_rows_spec, n_tiles=nt)
    return _call(
        body, "mixout_bwd", nt,
        [r(TILE_WIDE, D), r(TILE_WIDE, D), r(TILE_WIDE, D_ATTN), r(TILE_WIDE, D_S5), r(TILE_WIDE, D_LRU), _whole(), _whole()],
        [r(TILE_WIDE, D_ATTN), r(TILE_WIDE, D_S5), r(TILE_WIDE, D_LRU), _whole(), _whole(), _whole()],
        [_sds((n, D_ATTN)), _sds((n, D_S5)), _sds((n, D_LRU)), _sds((D, D)), _sds((1, D)), _sds((1, D))],
    )(dr1, mix, ya, ys, yl, g_mix, w_out)


def _ffn_conv(gp, halo, cw_ref, cb_ref, cs):
    ext = jnp.concatenate([halo, gp], axis=0)
    s1 = _shift_down(ext, 1, TILE)
    s2 = _shift_down(ext, 2, TILE)
    return s1, s2, cb_ref[:, cs] + cw_ref[2:3, cs] * gp + cw_ref[1:2, cs] * s1 + cw_ref[0:1, cs] * s2


def _ffn_fwd(x1, wg, wu, cw, cb, wd, g2, b2):
    n = x1.shape[0]
    nt = n // TILE

    def body(x_ref, wg_ref, wu_ref, cw_ref, cb_ref, wd_ref, g_ref, be_ref, gp_ref, up_ref, r_ref, x2_ref, halo_s, act_s):
        @pl.when(pl.program_id(0) == 0)
        def _():
            halo_s[...] = jnp.zeros_like(halo_s)

        x1 = x_ref[...]
        xb = _mx(x1)
        for c in range(D_FF // FF_CHUNK):
            cs = slice(c * FF_CHUNK, (c + 1) * FF_CHUNK)
            gp = _dot(xb, wg_ref[:, cs])
            up = _dot(xb, wu_ref[:, cs])
            gp_ref[:, cs] = gp
            up_ref[:, cs] = up
            _, _, gc = _ffn_conv(gp, halo_s[:, cs], cw_ref, cb_ref, cs)
            halo_s[:, cs] = gp[TILE - 8:]
            act_s[:, cs] = _mx(gc * _sigmoid(gc) * up)
        r2 = ALPHA * x1 + _dot(act_s[...], wd_ref[...])
        r_ref[...] = r2
        xhat, _ = _ln_stats(r2)
        x2_ref[...] = xhat * g_ref[...] + be_ref[...]

    r = functools.partial(_rows_spec, n_tiles=nt)
    return _call(
        body, "ffn_fwd", nt,
        [r(TILE, D)] + [_whole()] * 7,
        [r(TILE, D_FF), r(TILE, D_FF), r(TILE, D), r(TILE, D)],
        [_sds((n, D_FF)), _sds((n, D_FF)), _sds((n, D)), _sds((n, D))],
        scratch=[pltpu.VMEM((8, D_FF), F32), pltpu.VMEM((TILE, D_FF), MXU)],
    )(x1, wg, wu, cw, cb, wd, g2, b2)


def _ffn_bwd_down(dx2, r2, g2, gp, up, cw, cb, wd_t):
    n = dx2.shape[0]
    nt = n // TILE

    def body(dx_ref, r_ref, g_ref, gp_ref, gh_ref, up_ref, cw_ref, cb_ref, wd_ref,
             dr_ref, dgp_ref, dup_ref, dwd_ref, dcw0, dcw1, dcw2, dcb_ref, dg_ref, db_ref, next_s):
        first_tile = pl.program_id(0) == nt - 1

        @pl.when(pl.program_id(0) == 0)
        def _():
            for ref in (dwd_ref, dcw0, dcw1, dcw2, dcb_ref, dg_ref, db_ref, next_s):
                ref[...] = jnp.zeros_like(ref)

        dx2 = dx_ref[...]
        xhat, rstd = _ln_stats(r_ref[...])
        dg_ref[...] += _sum0(dx2 * xhat)
        db_ref[...] += _sum0(dx2)
        dr2 = _ln_bwd(dx2, g_ref[...], xhat, rstd)
        dr_ref[...] = dr2
        dfb = _mx(dr2)
        for c in range(D_FF // FF_CHUNK):
            cs = slice(c * FF_CHUNK, (c + 1) * FF_CHUNK)
            gp = gp_ref[:, cs]
            up = up_ref[:, cs]
            s1, s2, gc = _ffn_conv(gp, jnp.where(first_tile, 0.0, gh_ref[:, cs]), cw_ref, cb_ref, cs)
            sg = _sigmoid(gc)
            silu = gc * sg
            dact = _dot(dfb, wd_ref[:, cs])
            dwd_ref[cs, :] += _dot_tn(_mx(silu * up), dfb)
            dup_ref[:, cs] = _mx(dact * silu)
            dgc = dact * up * (sg + silu * (1.0 - sg))
            dcb_ref[:, cs] += _sum0(dgc)
            dcw2[:, cs] += _sum0(dgc * gp)
            dcw1[:, cs] += _sum0(dgc * s1)
            dcw0[:, cs] += _sum0(dgc * s2)
            ext = jnp.concatenate([dgc, next_s[:, cs]], axis=0)
            dgp_ref[:, cs] = _mx(cw_ref[2:3, cs] * dgc + cw_ref[1:2, cs] * _shift_up(ext, 1, TILE)
                                 + cw_ref[0:1, cs] * _shift_up(ext, 2, TILE))
            next_s[:, cs] = dgc[:8]

    r = functools.partial(_rows_spec, n_tiles=nt, reverse=True)
    vff = _sds((1, D_FF))
    return _call(
        body, "ffn_bwd_down", nt,
        [r(TILE, D), r(TILE, D), _whole(), r(TILE, D_FF), _halo_spec(D_FF, TILE, nt, reverse=True), r(TILE, D_FF), _whole(), _whole(),
         _whole()],
        [r(TILE, D), r(TILE, D_FF), r(TILE, D_FF)] + [_whole()] * 7,
        [_sds((n, D)), _sds((n, D_FF), MXU), _sds((n, D_FF), MXU), _sds((D_FF, D)), vff, vff, vff, vff, _sds((1, D)), _sds((1, D))],
        scratch=[pltpu.VMEM((8, D_FF), F32)],
    )(dx2, r2, g2, gp, gp, up, cw, cb, wd_t)


def _ffn_bwd_dx(dr2, dgp, dup, r1, g1, wg_t, wu_t):
    n = dr2.shape[0]
    rows = TILE_BIG
    nt = n // rows

    def body(dr2_ref, dgp_ref, dup_ref, r_ref, g_ref, wg_ref, wu_ref, dr1_ref, dg_ref, db_ref):
        @pl.when(pl.program_id(0) == 0)
        def _():
            for ref in (dg_ref, db_ref):
                ref[...] = jnp.zeros_like(ref)

        dx1 = ALPHA * dr2_ref[...] + _dot(dgp_ref[...], wg_ref[...]) + _dot(dup_ref[...], wu_ref[...])
        xhat, rstd = _ln_stats(r_ref[...])
        dg_ref[...] += _sum0(dx1 * xhat)
        db_ref[...] += _sum0(dx1)
        dr1_ref[...] = _ln_bwd(dx1, g_ref[...], xhat, rstd)

    r = functools.partial(_rows_spec, n_tiles=nt)
    return _call(
        body, "ffn_bwd_dx", nt,
        [r(rows, D), r(rows, D_FF), r(rows, D_FF), r(rows, D), _whole(), _whole(), _whole()],
        [r(rows, D), _whole(), _whole()],
        [_sds((n, D)), _sds((1, D)), _sds((1, D))],
    )(dr2, dgp, dup, r1, g1, wg_t, wu_t)


def _ffn_bwd_dw(x1, dgp, dup):
    n = x1.shape[0]
    rows = TILE_BIG
    nt = n // rows

    def body(x_ref, dgp_ref, dup_ref, dwg_ref, dwu_ref):
        @pl.when(pl.program_id(0) == 0)
        def _():
            for ref in (dwg_ref, dwu_ref):
                ref[...] = jnp.zeros_like(ref)

        xb = _mx(x_ref[...])
        for c in range(D_FF // FF_CHUNK):
            cs = slice(c * FF_CHUNK, (c + 1) * FF_CHUNK)
            dwg_ref[cs, :] += _dot_tn(dgp_ref[:, cs], xb)
            dwu_ref[cs, :] += _dot_tn(dup_ref[:, cs], xb)

    r = functools.partial(_rows_spec, n_tiles=nt)
    return _call(
        body, "ffn_bwd_dw", nt,
        [r(rows, D), r(rows, D_FF), r(rows, D_FF)], [_whole(), _whole()], [_sds((D_FF, D)), _sds((D_FF, D))],
    )(x1, dgp, dup)


def _loss_head(y, target):
    n = y.shape[0]
    nt = n // TILE_WIDE

    def body(y_ref, t_ref, loss_ref, dy_ref):
        @pl.when(pl.program_id(0) == 0)
        def _():
            loss_ref[...] = jnp.zeros_like(loss_ref)

        e = y_ref[...] - t_ref[...]
        dy_ref[...] = e * (1.0 / D)
        loss_ref[...] += _sum0(jnp.sum(e * e, axis=1, keepdims=True)) * (0.5 / D)

    r = functools.partial(_rows_spec, n_tiles=nt)
    return _call(body, "loss_head", nt, [r(TILE_WIDE, D), r(TILE_WIDE, D)], [_whole(), r(TILE_WIDE, D)],
                 [_sds((1, 1)), _sds((n, D))])(y, target)


def _place():
    x, y, c = lax.axis_index("x"), lax.axis_index("y"), lax.axis_index("c")
    return x, y, c, 4 * x + 2 * y + c


def _peer(x, y, c, k):
    px, py, pc = x ^ ((k >> 2) & 1), y ^ ((k >> 1) & 1), c ^ (k & 1)
    return (px, py, pc), 4 * px + 2 * py + pc


def _all_gather(name, blocks, small):
    srcs = list(blocks) + [small]
    n = len(srcs)
    out_shapes = [_sds((a.shape[0], N_DEV * a.shape[1], LANES), a.dtype) for a in blocks] + [_sds((N_DEV,) + small.shape, small.dtype)]

    def body(*refs):
        src_refs, out_refs = refs[:n], refs[n:2 * n]
        send_sems, recv_sems, local_sems = refs[2 * n:]
        x, y, c, me = _place()

        def landing(a, slot):
            if a == n - 1:
                return out_refs[a].at[slot]
            r = src_refs[a].shape[1]
            return out_refs[a].at[:, pl.ds(slot * r, r), :]

        def remote(a, k, slot):
            peer, _ = _peer(x, y, c, k)
            return pltpu.make_async_remote_copy(
                src_ref=src_refs[a], dst_ref=landing(a, slot), send_sem=send_sems.at[a * N_DEV + k],
                recv_sem=recv_sems.at[a * N_DEV + k], device_id=peer, device_id_type=pl.DeviceIdType.MESH)

        mine = [pltpu.make_async_copy(src_refs[a], landing(a, me), local_sems.at[a]) for a in range(n)]
        sends = [remote(a, k, me) for a in range(n) for k in range(1, N_DEV)]
        for cp in mine + sends:
            cp.start()
        for a in range(n):
            for k in range(1, N_DEV):
                remote(a, k, _peer(x, y, c, k)[1]).wait_recv()
        for cp in sends:
            cp.wait_send()
        for cp in mine:
            cp.wait()

    any_space = pl.BlockSpec(memory_space=pl.ANY)
    return pl.pallas_call(
        body, name=name, out_shape=out_shapes, in_specs=[any_space] * n, out_specs=[any_space] * n,
        scratch_shapes=[pltpu.SemaphoreType.DMA((n * N_DEV,)), pltpu.SemaphoreType.DMA((n * N_DEV,)), pltpu.SemaphoreType.DMA((n,))],
    )(*srcs)


_HBM = pl.BlockSpec(memory_space=pltpu.HBM)
_SEM = pl.BlockSpec(memory_space=pltpu.SEMAPHORE)
_EFFECT = pltpu.SideEffectType.DATAFLOW_SIDE_EFFECTING


def _in_hbm(a):
    return pltpu.with_memory_space_constraint(a, pltpu.HBM)


def _scatter_start(name, srcs):
    ns = len(srcs)
    rows_a = [a.shape[0] // N_DEV for a in srcs]
    offs = [sum(rows_a[:a]) for a in range(ns)]
    total = sum(rows_a)

    def body(*refs):
        src_refs, land_ref, send_sems, recv_sems, token = refs[:ns], refs[ns], refs[ns + 1], refs[ns + 2], refs[-1]
        x, y, c, me = _place()
        for k in range(1, N_DEV):
            peer, peer_slot = _peer(x, y, c, k)
            for a in range(ns):
                pltpu.make_async_remote_copy(
                    src_ref=src_refs[a].at[pl.ds(peer_slot * rows_a[a], rows_a[a]), :],
                    dst_ref=land_ref.at[me, pl.ds(offs[a], rows_a[a]), :], send_sem=send_sems.at[k], recv_sem=recv_sems.at[k],
                    device_id=peer, device_id_type=pl.DeviceIdType.MESH).start()
        token[...] = jnp.zeros_like(token)

    landing = lax.empty((N_DEV, total, LANES), F32)
    out = pl.pallas_call(
        body, name=name,
        out_shape=(pltpu.SemaphoreType.DMA((N_DEV,)), pltpu.SemaphoreType.DMA((N_DEV,)), *[pltpu.HBM(a.shape, a.dtype) for a in srcs],
                   pltpu.HBM(landing.shape, F32), _sds((8, 128))),
        in_specs=[_HBM] * (ns + 1), out_specs=(_SEM, _SEM, *[_HBM] * (ns + 1), pl.BlockSpec(memory_space=pltpu.VMEM)),
        input_output_aliases={a: 2 + a for a in range(ns + 1)},
        compiler_params=pltpu.CompilerParams(has_side_effects=_EFFECT),
    )(*[_in_hbm(a) for a in srcs], _in_hbm(landing))
    return out[0], out[1], out[2:2 + ns], out[2 + ns], out[-1]


def _scatter_wait(name, send_sems, recv_sems, srcs, landing, after):
    ns = len(srcs)

    def body(*refs):
        land_ref, send_ref, recv_ref = refs[ns], refs[ns + 1], refs[ns + 2]
        x, y, c, me = _place()
        for k in range(1, N_DEV):
            peer, peer_slot = _peer(x, y, c, k)
            slot = pltpu.make_async_remote_copy(
                src_ref=land_ref.at[me], dst_ref=land_ref.at[peer_slot], send_sem=send_ref.at[k], recv_sem=recv_ref.at[k],
                device_id=peer, device_id_type=pl.DeviceIdType.MESH)
            slot.wait_send()
            slot.wait_recv()

    out = pl.pallas_call(
        body, name=name, out_shape=(*[pltpu.HBM(a.shape, a.dtype) for a in srcs], pltpu.HBM(landing.shape, landing.dtype)),
        in_specs=[_HBM] * (ns + 1) + [_SEM, _SEM, pl.BlockSpec(memory_space=pl.ANY)], out_specs=[_HBM] * (ns + 1),
        input_output_aliases={a: a for a in range(ns + 1)},
        compiler_params=pltpu.CompilerParams(has_side_effects=_EFFECT),
    )(*srcs, landing, send_sems, recv_sems, after)
    return out[:ns], out[ns]


def _gather_start(name, blocks):
    n = len(blocks)

    def body(*refs):
        src_refs, land_refs, send_sems, recv_sems, token = refs[:n], refs[n:2 * n], refs[2 * n], refs[2 * n + 1], refs[-1]
        x, y, c, me = _place()
        for a in range(n):
            r = src_refs[a].shape[1]
            for k in range(1, N_DEV):
                pltpu.make_async_remote_copy(
                    src_ref=src_refs[a], dst_ref=land_refs[a].at[:, pl.ds(me * r, r), :], send_sem=send_sems.at[a * N_DEV + k],
                    recv_sem=recv_sems.at[a * N_DEV + k], device_id=_peer(x, y, c, k)[0], device_id_type=pl.DeviceIdType.MESH).start()
        token[...] = jnp.zeros_like(token)

    wholes = [lax.empty((a.shape[0], N_DEV * a.shape[1], LANES), a.dtype) for a in blocks]
    out = pl.pallas_call(
        body, name=name,
        out_shape=(pltpu.SemaphoreType.DMA((n * N_DEV,)), pltpu.SemaphoreType.DMA((n * N_DEV,)),
                   *[pltpu.HBM(a.shape, a.dtype) for a in blocks + wholes], _sds((8, 128))),
        in_specs=[_HBM] * (2 * n), out_specs=(_SEM, _SEM, *[_HBM] * (2 * n), pl.BlockSpec(memory_space=pltpu.VMEM)),
        input_output_aliases={a: 2 + a for a in range(2 * n)},
        compiler_params=pltpu.CompilerParams(has_side_effects=_EFFECT),
    )(*[_in_hbm(a) for a in blocks + wholes])
    return out[0], out[1], out[2:2 + n], out[2 + n:2 + 2 * n], out[-1]


def _gather_wait(name, send_sems, recv_sems, blocks, wholes, after):
    n = len(blocks)

    def body(*refs):
        src_refs, land_refs, send_ref, recv_ref = refs[:n], refs[n:2 * n], refs[2 * n], refs[2 * n + 1]
        x, y, c, me = _place()
        for a in range(n):
            r = src_refs[a].shape[1]
            for k in range(1, N_DEV):
                peer, peer_slot = _peer(x, y, c, k)
                cp = pltpu.make_async_remote_copy(
                    src_ref=src_refs[a], dst_ref=land_refs[a].at[:, pl.ds(peer_slot * r, r), :], send_sem=send_ref.at[a * N_DEV + k],
                    recv_sem=recv_ref.at[a * N_DEV + k], device_id=peer, device_id_type=pl.DeviceIdType.MESH)
                cp.wait_send()
                cp.wait_recv()

    return pl.pallas_call(
        body, name=name, out_shape=tuple(pltpu.HBM(a.shape, a.dtype) for a in list(blocks) + list(wholes)),
        in_specs=[_HBM] * (2 * n) + [_SEM, _SEM, pl.BlockSpec(memory_space=pl.ANY)], out_specs=[_HBM] * (2 * n),
        input_output_aliases={a: a for a in range(2 * n)},
        compiler_params=pltpu.CompilerParams(has_side_effects=_EFFECT),
    )(*blocks, *wholes, send_sems, recv_sems, after)


def _reduce_adamw(parts, w, m, v, tile_rows=PACK_TILE):
    rows = w.shape[0]
    nt = rows // tile_rows
    slots = parts.shape[0]
    c1 = 1.0 - ADAM_B1 ** ADAM_STEP
    c2 = 1.0 - ADAM_B2 ** ADAM_STEP

    def body(p_ref, w_ref, m_ref, v_ref, g_out, d_out, m_out, v_out):
        g = p_ref[0]
        for s in range(1, slots):
            g = g + p_ref[s]
        m_new = ADAM_B1 * m_ref[...] + (1.0 - ADAM_B1) * g
        v_new = ADAM_B2 * v_ref[...] + (1.0 - ADAM_B2) * (g * g)
        g_out[...] = g
        m_out[...] = m_new
        v_out[...] = v_new
        d_out[...] = -ADAM_LR * ((m_new / c1) / (jnp.sqrt(v_new / c2) + ADAM_EPS) + ADAM_WD * w_ref[...])

    r = _rows_spec(tile_rows, LANES, nt)
    out = _sds((rows, LANES))
    return _call(
        body, "reduce_adamw", nt,
        [pl.BlockSpec((slots, tile_rows, LANES), lambda i: (0, i, 0)), r, r, r], [r, r, r, r], [out, out, out, out],
    )(parts, w, m, v)


def _pack_rows(a, lead=0):
    head = a.shape[:lead]
    flat = a.reshape(head + (-1,))
    size = flat.shape[-1]
    rows = -(-size // (16 * LANES)) * 16
    flat = jnp.pad(flat, [(0, 0)] * lead + [(0, rows * LANES - size)])
    return flat.reshape(head + (rows, LANES))


def _packed_rows(shape):
    return -(-math.prod(shape) // (16 * LANES)) * 16


def _to_blocks(full, axis):
    l, a, b = full.shape
    if axis == 2:
        return full.reshape(l, a, N_DEV, b // N_DEV).transpose(2, 0, 1, 3)
    return full.reshape(l, N_DEV, a // N_DEV, b).transpose(1, 0, 2, 3)


def _from_blocks(blocks, axis):
    _, l, a, b = blocks.shape
    if axis == 2:
        return blocks.transpose(1, 2, 0, 3).reshape(l, a, N_DEV * b)
    return blocks.transpose(1, 0, 2, 3).reshape(l, N_DEV * a, b)


def _row_form(shard, transposed):
    return shard.transpose(0, 2, 1) if transposed else shard


def _me():
    return 4 * lax.axis_index("x") + 2 * lax.axis_index("y") + lax.axis_index("c")


def _both_forms(names, wholes, layer):
    out = {}
    for name, w in zip(names, wholes):
        t = dict(BIG)[name]
        out[name + '_t' if t else name] = w[layer]
        out[name if t else name + '_t'] = w[layer].T
    return out


def _gather_weights(local):
    segs, meta = [], []
    for name in SMALL_SHARDED:
        blk = local[name]
        if name in GATHER_F32:
            bits = lax.bitcast_convert_type(blk, MXU)
        else:
            bits = _mx(blk)
        seg = _pack_rows(bits)
        meta.append((name, bits.shape, seg.shape[0]))
        segs.append(seg)
    blocks = {name: _mx(_row_form(local[name], t)) for name, t in BIG}
    mix, ffn = PARTS['mix'], PARTS['ffn']
    *first, gathered = _all_gather("gather_weights", [blocks[n][:1] for n in mix], jnp.concatenate(segs, axis=0))
    flights = {'ffn0': _gather_start("gather_ffn0_start", [blocks[n][:1] for n in ffn]),
               'later': _gather_start("gather_later_start", [blocks[n][1:] for n in mix + ffn])}
    out, lo = {}, 0
    for name, bits_shape, rows in meta:
        seg = gathered[:, lo:lo + rows].reshape(N_DEV, -1)[:, :math.prod(bits_shape)].reshape((N_DEV,) + bits_shape)
        if name in GATHER_F32:
            seg = lax.bitcast_convert_type(seg, F32)
        out[name] = _from_blocks(seg, SHARD_AXIS[name])
        lo += rows
    ready = {(0, 'mix'): _both_forms(mix, first, 0)}

    def landed(flight, names, after):
        send_sems, recv_sems, mine, wholes, _ = flights[flight]
        done = _gather_wait(f"gather_{flight}_wait", send_sems, recv_sems, mine, wholes, after)
        return [lax.dynamic_update_slice(w, b, (0, _me() * b.shape[1], 0)) for b, w in zip(done[:len(names)], done[len(names):])]

    def big_weights(l, part, after):
        if (l, part) not in ready and l == 0:
            ready[(0, 'ffn')] = _both_forms(ffn, landed('ffn0', ffn, after), 0)
        elif (l, part) not in ready:
            wholes = landed('later', mix + ffn, after)
            for j in range(1, DEPTH):
                forms = _both_forms(mix + ffn, wholes, j - 1)
                for p, names in PARTS.items():
                    ready[(j, p)] = {k: forms[k] for n in names for k in (n, n + '_t')}
        return ready[(l, part)]

    return out, big_weights, flights['ffn0'][-1][0, 0] + flights['later'][-1][0, 0]


def _s5_discretize(a_re, a_im, log_dt, b_re, b_im):
    lam_re = jnp.minimum(a_re, -1e-4)
    lam_im = a_im
    dt = jnp.exp(log_dt)[:, None]
    decay = jnp.exp(dt * lam_re)
    ang = dt * lam_im
    abar_re = decay * jnp.cos(ang)
    abar_im = decay * jnp.sin(ang)
    den = jnp.square(lam_re) + jnp.square(lam_im)
    nr = abar_re - 1.0
    ni = abar_im
    coef_re = (nr * lam_re + ni * lam_im) / den
    coef_im = (ni * lam_re - nr * lam_im) / den
    bbar_re = coef_re[..., None] * b_re - coef_im[..., None] * b_im
    bbar_im = coef_re[..., None] * b_im + coef_im[..., None] * b_re
    return abar_re, abar_im, bbar_re, bbar_im


def _complex_powers(ar, ai, count):
    def combine(e1, e2):
        return e2[0] * e1[0] - e2[1] * e1[1], e2[0] * e1[1] + e2[1] * e1[0]

    shape = (count,) + ar.shape
    return lax.associative_scan(combine, (jnp.broadcast_to(ar, shape), jnp.broadcast_to(ai, shape)), axis=0)


_EYE16 = functools.partial(jnp.eye, 16, dtype=F32)


def _s5_params(p):
    disc, disc_vjp = jax.vjp(jax.vmap(_s5_discretize), p['s5_a_re'], p['s5_a_im'], p['s5_log_dt'], p['s5_b_re'], p['s5_b_im'])
    abar_re, abar_im, bbar_re, bbar_im = disc
    pw_re, pw_im = _complex_powers(abar_re.reshape(DEPTH, N_STATE), abar_im.reshape(DEPTH, N_STATE), SEG)
    sp_re, sp_im = _complex_powers(pw_re[SEG - 1], pw_im[SEG - 1], N_SEG - 1)
    one, zero = jnp.ones((1, DEPTH, N_STATE), F32), jnp.zeros((1, DEPTH, N_STATE), F32)
    seg_re = jnp.concatenate([one, sp_re], axis=0)
    seg_im = jnp.concatenate([zero, sp_im], axis=0)
    doubling = [0, 1, 3]
    blank = jnp.zeros((5, DEPTH, N_STATE), F32)
    tables = {
        'pw_re': pw_re, 'pw_im': pw_im,
        'dbl_re': jnp.concatenate([jnp.stack([sp_re[k] for k in doubling]), blank], axis=0),
        'dbl_im': jnp.concatenate([jnp.stack([sp_im[k] for k in doubling]), blank], axis=0),
        'seg_re': seg_re, 'seg_im': seg_im, 'segr_re': seg_re[::-1], 'segr_im': seg_im[::-1],
    }
    weights = {
        'b_re': _mx(jnp.einsum('lgpc,gh->lgchp', bbar_re, _EYE16()).reshape(DEPTH, D_S5, N_STATE)),
        'b_im': _mx(jnp.einsum('lgpc,gh->lgchp', bbar_im, _EYE16()).reshape(DEPTH, D_S5, N_STATE)),
        'c_re': _mx(jnp.einsum('lgcp,gh->lgphc', p['s5_c_re'], _EYE16()).reshape(DEPTH, N_STATE, D_S5)),
        'c_im': _mx(jnp.einsum('lgcp,gh->lgphc', p['s5_c_im'], _EYE16()).reshape(DEPTH, N_STATE, D_S5)),
    }
    src = (jnp.arange(TILE) % N_SEG) * SEG + jnp.arange(TILE) // N_SEG
    perm = (src[:, None] == jnp.arange(TILE)[None, :]).astype(MXU)
    layers = []
    for l in range(DEPTH):
        prm = {k: v[:, l] for k, v in tables.items()}
        prm.update({k: v[l] for k, v in weights.items()})
        prm.update({'perm': perm, 'perm_t': perm.T, 'd': p['s5_d'][l][None, :], 'glu_w': p['s5_glu_w'][l],
                    'glu_b': p['s5_glu_b'][l][None, :]})
        layers.append(prm)
    return layers, disc_vjp


def _lru_params(p, l):
    eye4 = jnp.eye(4, dtype=F32)
    return {
        'conv_w': p['lru_conv_w'][l], 'conv_b': p['lru_conv_b'][l][None, :],
        'wx': _mx(jnp.einsum('hij,hk->hikj', p['lru_wx'][l], eye4).reshape(D_LRU, D_LRU)),
        'wa': _mx(jnp.einsum('hij,hk->hikj', p['lru_wa'][l], eye4).reshape(D_LRU, D_LRU)),
        'bx': p['lru_bx'][l][None, :], 'ba': p['lru_ba'][l][None, :],
        'sp': jax.nn.softplus(-p['lru_a_param'][l])[None, :],
    }


def _rope_tables(n):
    inv_freq = ROPE_THETA ** (-jnp.arange(0, 64, 2, dtype=F32) / 64)
    ang = jnp.arange(n, dtype=F32)[:, None] * inv_freq[None, :]
    cos, sin = jnp.cos(ang), jnp.sin(ang)
    return jnp.concatenate([cos, cos, cos, cos], axis=1), jnp.concatenate([-sin, sin, -sin, sin], axis=1)


def _sink_cols(sinks):
    nb = TILE_Q // ATTN_BLOCK
    per_unit = sinks.reshape(4, 2).T
    return jnp.broadcast_to(per_unit[:, None, :, None, None], (2, nb, 4, ATTN_BLOCK, 128)).reshape(2, nb * 4 * ATTN_BLOCK, 128)


def _local_step(x, target, p, big_weights=None, emit_grads=None):
    if big_weights is None:
        big_weights = lambda l, part, after: {k: p[k][l] for name in PARTS[part] for k in (name, name + '_t')}
    if emit_grads is None:
        emit_grads = lambda l, part, grads: 0.0
    n = x.shape[0]
    cos_t, sin_t = _rope_tables(n)
    s5_layers, s5_vjp = _s5_params(p)
    row = lambda a: a[None, :]
    saved = []
    h = x
    for l in range(DEPTH):
        s = {'x0': h}
        bw = s['bw'] = dict(big_weights(l, 'mix', h))
        s['q'], k, v, s['u'], s['xr'], s['gate'] = _inproj_fwd(h, bw['w_in'], row(p['b_in'][l]), cos_t, sin_t)
        no_keys = jnp.zeros((ATTN_BLOCK, D_KV), MXU)
        s['k'], s['v'] = jnp.concatenate([no_keys, k], axis=0), jnp.concatenate([no_keys, v], axis=0)
        s['sinks'] = _sink_cols(p['attn_sinks'][l])
        s['ya'] = _attn_fwd(s['q'], s['k'], s['v'], s['sinks'])
        s['s5'] = s5_layers[l]
        s['lru'] = _lru_params(p, l)
        s['ys'], s['s5_cr'], s['s5_ci'] = _s5_fwd(s['u'], s['s5'])
        s['yl'], s['lru_c'] = _lru_fwd(s['xr'], s['gate'], s['lru'])
        s['mix'], s['r1'], s['x1'] = _mixout_fwd(s['ya'], s['ys'], s['yl'], h, row(p['mix_norm_g'][l]), bw['w_out'],
                                                 row(p['b_out'][l]), row(p['ln1_g'][l]), row(p['ln1_b'][l]))
        bw.update(big_weights(l, 'ffn', s['x1']))
        s['gp'], s['up'], s['r2'], h = _ffn_fwd(s['x1'], bw['ffn_w_gate'], bw['ffn_w_up'], p['ffn_conv_w'][l],
                                                row(p['ffn_conv_b'][l]), bw['ffn_w_down'], row(p['ln2_g'][l]), row(p['ln2_b'][l]))
        saved.append(s)
    loss, dh = _loss_head(h, target)
    placed = 0.0

    grads = {name: [None] * DEPTH for name in WEIGHTS}
    d_disc = [None] * DEPTH
    for l in reversed(range(DEPTH)):
        s = saved[l]
        g = {}
        (dr2, dgp, dup, g['ffn_w_down'], cw0, cw1, cw2, dcb, dg2, db2) = _ffn_bwd_down(
            dh, s['r2'], row(p['ln2_g'][l]) + placed, s['gp'], s['up'], p['ffn_conv_w'][l], row(p['ffn_conv_b'][l]),
            s['bw']['ffn_w_down_t'])
        g['ffn_conv_w'] = jnp.concatenate([cw0, cw1, cw2], axis=0)
        g['ffn_conv_b'], g['ln2_g'], g['ln2_b'] = dcb[0], dg2[0], db2[0]
        dr1, dg1, db1 = _ffn_bwd_dx(dr2, dgp, dup, s['r1'], row(p['ln1_g'][l]), s['bw']['ffn_w_gate_t'], s['bw']['ffn_w_up_t'])
        g['ffn_w_gate'], g['ffn_w_up'] = _ffn_bwd_dw(s['x1'], dgp, dup)
        g['ln1_g'], g['ln1_b'] = dg1[0], db1[0]
        placed = emit_grads(l, 'ffn', [g[name] for name in PARTS['ffn']])
        dya, dys, dyl, g['w_out'], dbo, dgm = _mixout_bwd(dr1, s['mix'], s['ya'], s['ys'], s['yl'],
                                                         row(p['mix_norm_g'][l]) + placed, s['bw']['w_out_t'])
        g['b_out'], g['mix_norm_g'] = dbo[0], dgm[0]

        du, dbr, dbi, dcr, dci, dar, dai, dd, g['s5_glu_w'], dgb = _s5_bwd(s['u'], dys, s['s5_cr'], s['s5_ci'], s['s5'])
        dxr, dgate, lw0, lw1, lw2, lw3, lcb, dwx, dwa, dbx, dba, dsp = _lru_bwd(s['xr'], s['gate'], dyl, s['lru_c'], s['lru'])
        g['lru_conv_w'] = jnp.concatenate([lw0, lw1, lw2, lw3], axis=0)
        g['lru_conv_b'], g['lru_bx'], g['lru_ba'] = lcb[0], dbx[0], dba[0]
        g['lru_wx'] = jnp.einsum('hihj->hij', dwx.reshape(4, 64, 4, 64))
        g['lru_wa'] = jnp.einsum('hihj->hij', dwa.reshape(4, 64, 4, 64))
        g['lru_a_param'] = -dsp[0] * jax.nn.sigmoid(-p['lru_a_param'][l])

        g['s5_c_re'] = jnp.einsum('gpgc->gcp', dcr.reshape(16, 64, 16, 16))
        g['s5_c_im'] = jnp.einsum('gpgc->gcp', dci.reshape(16, 64, 16, 16))
        g['s5_d'], g['s5_glu_b'] = dd[0], dgb[0]
        d_disc[l] = (dar.reshape(16, 64), dai.reshape(16, 64), jnp.einsum('gcgp->gpc', dbr.reshape(16, 16, 16, 64)),
                     jnp.einsum('gcgp->gpc', dbi.reshape(16, 16, 16, 64)))

        dq, dk, dv, dsink = _attn_bwd(s['q'], s['k'], s['v'], s['sinks'], s['ya'], dya)
        g['attn_sinks'] = dsink[:, 0]
        dh, g['w_in'], dbin = _inproj_bwd(dq, dk[ATTN_BLOCK:], dv[ATTN_BLOCK:], du, dxr, dgate, cos_t, sin_t, s['x0'], dr1,
                                          s['bw']['w_in_t'])
        g['b_in'] = dbin[0]
        placed = emit_grads(l, 'mix', [g[name] for name in PARTS['mix']])
        for name in g:
            grads[name][l] = g[name]
    big = dict(BIG)
    out = {name: grads[name] if name in big else jnp.stack(grads[name]) for name in WEIGHTS if name not in S5_DISC}
    out.update(zip(S5_DISC, s5_vjp(tuple(jnp.stack([d_disc[l][i] for l in range(DEPTH)]) for i in range(4)))))
    return loss, dh, out


def kernel(x, w_in, b_in, attn_sinks, s5_a_re, s5_a_im, s5_b_re, s5_b_im, s5_c_re, s5_c_im, s5_d, s5_log_dt, s5_glu_w, s5_glu_b, lru_conv_w, lru_conv_b, lru_wx, lru_bx, lru_wa, lru_ba, lru_a_param, mix_norm_g, w_out, b_out, ln1_g, ln1_b, ffn_w_gate, ffn_w_up, ffn_conv_w, ffn_conv_b, ffn_w_down, ln2_g, ln2_b, loss_target, m_w_in, m_b_in, m_attn_sinks, m_s5_a_re, m_s5_a_im, m_s5_b_re, m_s5_b_im, m_s5_c_re, m_s5_c_im, m_s5_d, m_s5_log_dt, m_s5_glu_w, m_s5_glu_b, m_lru_conv_w, m_lru_conv_b, m_lru_wx, m_lru_bx, m_lru_wa, m_lru_ba, m_lru_a_param, m_mix_norm_g, m_w_out, m_b_out, m_ln1_g, m_ln1_b, m_ffn_w_gate, m_ffn_w_up, m_ffn_conv_w, m_ffn_conv_b, m_ffn_w_down, m_ln2_g, m_ln2_b, v_w_in, v_b_in, v_attn_sinks, v_s5_a_re, v_s5_a_im, v_s5_b_re, v_s5_b_im, v_s5_c_re, v_s5_c_im, v_s5_d, v_s5_log_dt, v_s5_glu_w, v_s5_glu_b, v_lru_conv_w, v_lru_conv_b, v_lru_wx, v_lru_bx, v_lru_wa, v_lru_ba, v_lru_a_param, v_mix_norm_g, v_w_out, v_b_out, v_ln1_g, v_ln1_b, v_ffn_w_gate, v_ffn_w_up, v_ffn_conv_w, v_ffn_conv_b, v_ffn_w_down, v_ln2_g, v_ln2_b):
    given = dict(locals())
    whole = {name: given[name] for name in WEIGHTS if name not in dict(BIG)}
    small_whole, big_weights, placed = _gather_weights({name: given[name] for name in SHARDED})
    whole.update(small_whole)
    whole['b_in'] = whole['b_in'] + placed
    in_flight = {}

    def emit_grads(l, part, grads):
        in_flight[(l, part)] = _scatter_start(f"grads_start_{part}{l}", grads)
        return in_flight[(l, part)][-1][0, 0]

    loss, grad_x, grads = _local_step(x[0], loss_target[0], whole, big_weights, emit_grads)
    total = lax.psum(loss[0, 0], ("x", "y", "c"))
    return (total, grad_x[None], *_update(given, grads, in_flight, grad_x))


def _update(given, grads, in_flight, after):
    local_w = {name: given[name] for name in WEIGHTS}
    me = _me()
    outs = {}

    shard_rows = sum(_packed_rows(local_w[name].shape) for name in SMALL_SHARDED)
    rep_pad = -sum(_packed_rows(local_w[name].shape) for name in REPLICATED) % PACK_TILE

    def packed_rep(arrays):
        return jnp.concatenate([_pack_rows(arrays[name]) for name in REPLICATED] + [jnp.zeros((rep_pad, LANES), F32)], axis=0)

    rep_grads = packed_rep(grads)
    chunk = rep_grads.shape[0] // N_DEV
    small = jnp.concatenate(
        [_pack_rows(_to_blocks(grads[name], SHARD_AXIS[name]), lead=1) for name in SMALL_SHARDED]
        + [rep_grads.reshape(N_DEV, chunk, LANES)], axis=1)
    small_rows = shard_rows + chunk
    small_flight = _scatter_start("grads_start_small", [small.reshape(N_DEV * small_rows, LANES)])

    def summed(name, flight, rows_of, packed, after):
        send_sems, recv_sems, srcs, landing, _ = flight
        srcs, landing = _scatter_wait(name, send_sems, recv_sems, srcs, landing, after)
        own = jnp.concatenate([lax.dynamic_slice_in_dim(g, me * r, r, axis=0) for g, r in zip(srcs, rows_of)], axis=0)
        parts = lax.dynamic_update_slice(landing, own[None], (me, 0, 0))
        return _reduce_adamw(parts, *packed, tile_rows=own.shape[0] // 4)

    forms = {part: [(name, dict(BIG)[name], _row_form(local_w[name], dict(BIG)[name]).shape[1]) for name in names]
             for part, names in PARTS.items()}
    for (l, part), flight in in_flight.items():
        packed = [jnp.concatenate([_row_form(given[prefix + name], t)[l] for name, t, _ in forms[part]], axis=0)
                  for prefix in ('', 'm_', 'v_')]
        outs[(l, part)] = summed(f"grads_wait_{part}{l}", flight, [r for _, _, r in forms[part]], packed, after)
        after = outs[(l, part)][0]

    rep_state = [packed_rep({name: given[prefix + name] for name in REPLICATED}) for prefix in ('', 'm_', 'v_')]
    small_state = [jnp.concatenate([_pack_rows(given[prefix + name]) for name in SMALL_SHARDED]
                                   + [lax.dynamic_slice_in_dim(rep, me * chunk, chunk, axis=0)], axis=0)
                   for prefix, rep in zip(('', 'm_', 'v_'), rep_state)]
    small_outs = summed("grads_wait_small", small_flight, [small_rows], small_state, after)
    rep_sum = _all_gather("gather_small_grads", [], small_outs[0][shard_rows:])[0].reshape(N_DEV * chunk, LANES)
    rep_outs = _reduce_adamw(rep_sum[None], *rep_state, tile_rows=N_DEV * chunk // 4)

    def unpack(i):
        res = {}
        for part in PARTS:
            lo = 0
            for name, t, r in forms[part]:
                res[name] = _row_form(jnp.stack([outs[(l, part)][i][lo:lo + r] for l in range(DEPTH)]), t)
                lo += r
        for names, packed in ((SMALL_SHARDED, small_outs[i]), (REPLICATED, rep_outs[i])):
            lo = 0
            for name in names:
                shape = local_w[name].shape
                res[name] = packed[lo:lo + _packed_rows(shape)].reshape(-1)[:math.prod(shape)].reshape(shape)
                lo += _packed_rows(shape)
        return [res[name] for name in WEIGHTS]

    return (*unpack(0), *unpack(1), *unpack(2), *unpack(3))
```

```python
import functools
import math

import jax
import jax.numpy as jnp
from jax import lax
from jax.experimental import pallas as pl
from jax.experimental.pallas import tpu as pltpu

F32 = jnp.float32
MXU = jnp.bfloat16

N_DEV = 8
DEPTH = 4
D = 1024
D_ATTN, D_KV, D_S5, D_LRU = 512, 128, 256, 256
D_IN = 1536
D_FF = 2816
FF_CHUNK = 256
N_STATE = 1024
LANES = 1024
ALPHA = (2 * DEPTH) ** 0.25
LN_EPS = 1e-5
RMS_EPS = 1e-6
LRU_C = 8.0
ROPE_THETA = 10000.0
ADAM_LR, ADAM_B1, ADAM_B2, ADAM_EPS, ADAM_WD, ADAM_STEP = 0.001, 0.9, 0.999, 1e-08, 0.01, 10

TILE = 256
S5_PAIR = 4
N_SEG = 8
SEG = TILE // N_SEG
TILE_Q = 512
TILE_BIG = 512
TILE_WIDE = 512
ATTN_BLOCK = 128
PACK_TILE = 256
VMEM_MB = 56

WEIGHTS = ['w_in', 'b_in', 'attn_sinks', 's5_a_re', 's5_a_im', 's5_b_re', 's5_b_im', 's5_c_re', 's5_c_im', 's5_d', 's5_log_dt',
           's5_glu_w', 's5_glu_b', 'lru_conv_w', 'lru_conv_b', 'lru_wx', 'lru_bx', 'lru_wa', 'lru_ba', 'lru_a_param', 'mix_norm_g',
           'w_out', 'b_out', 'ln1_g', 'ln1_b', 'ffn_w_gate', 'ffn_w_up', 'ffn_conv_w', 'ffn_conv_b', 'ffn_w_down', 'ln2_g', 'ln2_b']
SHARD_AXIS = {'w_in': 2, 's5_glu_w': 1, 'lru_conv_w': 2, 'w_out': 1, 'ffn_w_gate': 2, 'ffn_w_up': 2, 'ffn_conv_w': 2,
              'ffn_w_down': 1}
SHARDED = [n for n in WEIGHTS if n in SHARD_AXIS]
REPLICATED = [n for n in WEIGHTS if n not in SHARD_AXIS]
BIG = [(n, SHARD_AXIS[n] == 2) for n in ('w_in', 'w_out', 'ffn_w_gate', 'ffn_w_up', 'ffn_w_down')]
SMALL_SHARDED = [n for n in SHARDED if n not in dict(BIG)]
PARTS = {'mix': ['w_in', 'w_out'], 'ffn': ['ffn_w_gate', 'ffn_w_up', 'ffn_w_down']}
GATHER_F32 = ('lru_conv_w', 'ffn_conv_w')


def _dot(a, b):
    return jnp.dot(a, b, preferred_element_type=F32)


def _dot_nt(a, b):
    return lax.dot_general(a, b, (((1,), (1,)), ((), ())), preferred_element_type=F32)


def _dot_tn(a, b):
    return lax.dot_general(a, b, (((0,), (0,)), ((), ())), preferred_element_type=F32)


def _mx(a):
    return a.astype(MXU)


_GELU_C = math.sqrt(2.0 / math.pi)


def _gelu(x):
    th = jnp.tanh(_GELU_C * (x + 0.044715 * x * x * x))
    return 0.5 * x * (1.0 + th)


def _gelu_grad(x):
    th = jnp.tanh(_GELU_C * (x + 0.044715 * x * x * x))
    return 0.5 * (1.0 + th) + 0.5 * x * (1.0 - th * th) * _GELU_C * (1.0 + 3.0 * 0.044715 * x * x)


def _sigmoid(x):
    return 0.5 * jnp.tanh(0.5 * x) + 0.5


def _ln_stats(r):
    mu = jnp.mean(r, axis=-1, keepdims=True)
    xc = r - mu
    var = jnp.mean(xc * xc, axis=-1, keepdims=True)
    rstd = lax.rsqrt(var + LN_EPS)
    return xc * rstd, rstd


def _ln_bwd(dy, g, xhat, rstd):
    dxh = dy * g
    return rstd * (dxh - jnp.mean(dxh, axis=-1, keepdims=True) - xhat * jnp.mean(dxh * xhat, axis=-1, keepdims=True))


def _rms(y):
    return lax.rsqrt(jnp.mean(y * y, axis=-1, keepdims=True) + RMS_EPS)


def _sum0(a):
    return jnp.sum(a, axis=0, keepdims=True)


def _row_iota(shape):
    return lax.broadcasted_iota(jnp.int32, shape, 0)


def _shift_down(ext, j, rows):
    return pltpu.roll(ext, j, 0)[8:8 + rows]


def _shift_up(ext, j, rows):
    return pltpu.roll(ext, ext.shape[0] - j, 0)[:rows]


def _swap_halves(t):
    w = t.shape[1]
    lane = lax.broadcasted_iota(jnp.int32, t.shape, 1)
    return jnp.where((lane & 32) == 0, pltpu.roll(t, w - 32, 1), pltpu.roll(t, 32, 1))


def _rope(t, cos, sin_signed):
    return t * cos + _swap_halves(t) * sin_signed


def _rope_t(d, cos, sin_signed):
    return d * cos + _swap_halves(d * sin_signed)


def _cmul_add(ar, ai, xr, xi, yr, yi):
    return ar * xr - ai * xi + yr, ar * xi + ai * xr + yi


def _seg_rows(k):
    return slice(N_SEG * k, N_SEG * (k + 1))


def _permute_rows(perm, x):
    hi = _mx(x)
    rest = x - hi.astype(F32)
    mid = _mx(rest)
    lo = _mx(rest - mid.astype(F32))
    return _dot(perm, hi) + _dot(perm, mid) + _dot(perm, lo)


def _cscan(sr, si, tab, cin_r, cin_i, reverse):
    sgn = -1.0 if reverse else 1.0
    pw_re, pw_im, dbl_re, dbl_im = tab['pw_re'], tab['pw_im'], tab['dbl_re'], tab['dbl_im']
    ar, ai = pw_re[0:1, :], sgn * pw_im[0:1, :]
    shape = (N_SEG, sr.shape[1])
    hr = hi = jnp.zeros(shape, F32)
    for k in (range(SEG - 1, -1, -1) if reverse else range(SEG)):
        hr, hi = _cmul_add(ar, ai, hr, hi, sr[_seg_rows(k), :], si[_seg_rows(k), :])
        sr[_seg_rows(k), :] = hr
        si[_seg_rows(k), :] = hi
    sub = _row_iota(shape)

    def shifted(v, d):
        if reverse:
            return jnp.where(sub < N_SEG - d, pltpu.roll(v, N_SEG - d, 0), 0.0)
        return jnp.where(sub >= d, pltpu.roll(v, d, 0), 0.0)

    fr, fi = hr, hi
    for j, d in enumerate((1, 2, 4)):
        fr, fi = _cmul_add(dbl_re[j:j + 1, :], sgn * dbl_im[j:j + 1, :], shifted(fr, d), shifted(fi, d), fr, fi)
    seg_re, seg_im = (tab['segr_re'], tab['segr_im']) if reverse else (tab['seg_re'], tab['seg_im'])
    cr, ci = _cmul_add(seg_re[...], sgn * seg_im[...], cin_r, cin_i, shifted(fr, 1), shifted(fi, 1))
    nr, ni = _cmul_add(dbl_re[0:1, :], sgn * dbl_im[0:1, :], cr, ci, hr, hi)
    for k in range(SEG):
        j = SEG - 1 - k if reverse else k
        xr, xi = _cmul_add(pw_re[j:j + 1, :], sgn * pw_im[j:j + 1, :], cr, ci, sr[_seg_rows(k), :], si[_seg_rows(k), :])
        sr[_seg_rows(k), :] = xr
        si[_seg_rows(k), :] = xi
    edge = slice(0, 1) if reverse else slice(N_SEG - 1, N_SEG)
    return nr[edge], ni[edge]


def _rscan(a, b, reverse):
    rows = a.shape[0]
    row = _row_iota(a.shape)
    s = 1
    while s < rows:
        if reverse:
            keep = row < rows - s
            sa = jnp.where(keep, pltpu.roll(a, rows - s, 0), 1.0)
            sb = jnp.where(keep, pltpu.roll(b, rows - s, 0), 0.0)
        else:
            keep = row >= s
            sa = jnp.where(keep, pltpu.roll(a, s, 0), 1.0)
            sb = jnp.where(keep, pltpu.roll(b, s, 0), 0.0)
        b = b + a * sb
        a = a * sa
        s *= 2
    return a, b


def _whole():
    return pl.BlockSpec(memory_space=pltpu.VMEM)


def _rows_spec(rows, cols, n_tiles, reverse=False):
    if reverse:
        return pl.BlockSpec((rows, cols), lambda i: (n_tiles - 1 - i, 0))
    return pl.BlockSpec((rows, cols), lambda i: (i, 0))


def _halo_spec(cols, tile_rows, n_tiles, reverse=False):
    per = tile_rows // 8
    if reverse:
        return pl.BlockSpec((8, cols), lambda i: (jnp.maximum((n_tiles - 1 - i) * per - 1, 0), 0))
    return pl.BlockSpec((8, cols), lambda i: (jnp.maximum(i * per - 1, 0), 0))


def _call(body, name, n_tiles, in_specs, out_specs, out_shape, scratch=()):
    return pl.pallas_call(
        body, name=name, grid=(n_tiles,), in_specs=in_specs, out_specs=out_specs, out_shape=out_shape,
        scratch_shapes=list(scratch),
        compiler_params=pltpu.CompilerParams(dimension_semantics=("arbitrary",), vmem_limit_bytes=VMEM_MB << 20))


def _sds(shape, dtype=F32):
    return jax.ShapeDtypeStruct(shape, dtype)


def _inproj_fwd(x, w, b, cos_t, sin_t):
    n = x.shape[0]
    nt = n // TILE_WIDE

    def body(x_ref, w_ref, b_ref, c_ref, s_ref, q_ref, k_ref, v_ref, u_ref, xr_ref, g_ref):
        p = _dot(_mx(x_ref[...]), w_ref[...]) + b_ref[...]
        cos, sin = c_ref[...], s_ref[...]
        q_ref[...] = _mx(_rope(p[:, :D_ATTN], jnp.tile(cos, (1, 4)), jnp.tile(sin, (1, 4))))
        k_ref[...] = _mx(_rope(p[:, 512:640], cos, sin))
        v_ref[...] = _mx(p[:, 640:768])
        u_ref[...] = p[:, 768:1024]
        xr_ref[...] = p[:, 1024:1280]
        g_ref[...] = p[:, 1280:1536]

    r = functools.partial(_rows_spec, n_tiles=nt)
    return _call(
        body, "inproj_fwd", nt,
        [r(TILE_WIDE, D), _whole(), _whole(), r(TILE_WIDE, 128), r(TILE_WIDE, 128)],
        [r(TILE_WIDE, D_ATTN), r(TILE_WIDE, D_KV), r(TILE_WIDE, D_KV), r(TILE_WIDE, D_S5), r(TILE_WIDE, D_LRU), r(TILE_WIDE, D_LRU)],
        [_sds((n, D_ATTN), MXU), _sds((n, D_KV), MXU), _sds((n, D_KV), MXU), _sds((n, D_S5)), _sds((n, D_LRU)), _sds((n, D_LRU))],
    )(x, w, b, cos_t, sin_t)


def _inproj_bwd(dq, dk, dv, du, dxr, dgate, cos_t, sin_t, x0, dr1, w_t):
    n = x0.shape[0]
    nt = n // TILE_WIDE

    def body(dq_ref, dk_ref, dv_ref, du_ref, dxr_ref, dg_ref, c_ref, s_ref, x_ref, dr_ref, w_ref, dx_ref, dw_ref, db_ref):
        @pl.when(pl.program_id(0) == 0)
        def _():
            dw_ref[...] = jnp.zeros_like(dw_ref)
            db_ref[...] = jnp.zeros_like(db_ref)

        cos, sin = c_ref[...], s_ref[...]
        dtq = _rope_t(dq_ref[...], jnp.tile(cos, (1, 4)), jnp.tile(sin, (1, 4)))
        dtk = _rope_t(dk_ref[...], cos, sin)
        dp = jnp.concatenate([dtq, dtk, dv_ref[...], du_ref[...], dxr_ref[...], dg_ref[...]], axis=1)
        db_ref[...] += _sum0(dp)
        dpb = _mx(dp)
        dw_ref[...] += _dot_tn(dpb, _mx(x_ref[...]))
        dx_ref[...] = ALPHA * dr_ref[...] + _dot(dpb, w_ref[...])

    r = functools.partial(_rows_spec, n_tiles=nt)
    return _call(
        body, "inproj_bwd", nt,
        [r(TILE_WIDE, D_ATTN), r(TILE_WIDE, D_KV), r(TILE_WIDE, D_KV), r(TILE_WIDE, D_S5), r(TILE_WIDE, D_LRU), r(TILE_WIDE, D_LRU),
         r(TILE_WIDE, 128), r(TILE_WIDE, 128), r(TILE_WIDE, D), r(TILE_WIDE, D), _whole()],
        [r(TILE_WIDE, D), _whole(), _whole()],
        [_sds((n, D)), _sds((D_IN, D)), _sds((1, D_IN))],
    )(dq, dk, dv, du, dxr, dgate, cos_t, sin_t, x0, dr1, w_t)


def _kv_variants(t, lo):
    tr = pltpu.roll(t, 64, 1)
    out = []
    for j in range(2):
        first = jnp.where(lo, t if j == 0 else tr, 0.0)
        second = jnp.where(lo, 0.0, tr if j == 0 else t)
        out.append(_mx(jnp.concatenate([first, second], axis=0)))
    return out


def _kv_collect(x0, x1, lo):
    a = x0[:256] + pltpu.roll(x0[256:], 64, 1)
    b = pltpu.roll(x1[:256], 64, 1) + x1[256:]
    return jnp.where(lo, a, b)


def _row_sums(x):
    ones = jnp.ones((128, 128), MXU)
    hi = _mx(x)
    lo = _mx(x - hi.astype(F32))
    return _dot(hi, ones) + _dot(lo, ones)


def _attn_probs(s, sink_ref):
    out = []
    for hp in range(2):
        sh = s[:, hp * 128:(hp + 1) * 128]
        sink = sink_ref[hp]
        m = jnp.maximum(jnp.broadcast_to(jnp.max(sh, axis=1, keepdims=True), sh.shape), sink)
        p = jnp.exp(sh - m)
        es = jnp.exp(sink - m)
        inv = 1.0 / (_row_sums(p) + es)
        out.append((p * inv, es * inv))
    return out


def _band_merge(x, tri, no_previous=None):
    bands = []
    for hp in range(2):
        prev, own = x[:, hp * 256:hp * 256 + 128], x[:, hp * 256 + 128:hp * 256 + 256]
        if no_previous is not None:
            prev = jnp.where(no_previous, -jnp.inf, prev)
        bands.append(jnp.where(tri, own, prev))
    return jnp.concatenate(bands, axis=1)


def _band_split(y, tri):
    parts = []
    for hp in range(2):
        band = y[:, hp * 128:(hp + 1) * 128]
        parts += [jnp.where(tri, 0.0, band), jnp.where(tri, band, 0.0)]
    return jnp.concatenate(parts, axis=1)


def _tri():
    shape = (ATTN_BLOCK, ATTN_BLOCK)
    return lax.broadcasted_iota(jnp.int32, shape, 0) >= lax.broadcasted_iota(jnp.int32, shape, 1)


def _attn_scores(q_ref, k_ref, v_ref, nb, tri):
    lo = lax.broadcasted_iota(jnp.int32, (256, 128), 1) < 64
    kcats, vcats, kstarts, parts = [], [], [], []
    for b in range(nb):
        block = pl.program_id(0) * nb + b
        kstart = pl.multiple_of(block * ATTN_BLOCK, ATTN_BLOCK)
        kcat = _kv_variants(k_ref[pl.ds(kstart, 256), :].astype(F32), lo)
        kcats.append(kcat)
        vcats.append(_kv_variants(v_ref[pl.ds(kstart, 256), :].astype(F32), lo))
        kstarts.append(kstart)
        for j in range(2):
            s = _dot_nt(_kv_group(q_ref, b, j), kcat[j]) * 0.125
            parts += [_band_merge(s[:ATTN_BLOCK], tri, block == 0), _band_merge(s[ATTN_BLOCK:], tri, block == 0)]
    return jnp.concatenate(parts, axis=0), kcats, vcats, kstarts


def _kv_group(a, b, j):
    rows = slice(b * ATTN_BLOCK, (b + 1) * ATTN_BLOCK)
    return jnp.concatenate([a[rows, 2 * j * 128:(2 * j + 1) * 128], a[rows, (2 * j + 1) * 128:(2 * j + 2) * 128]], axis=0)


def _put_kv_group(ref, b, j, x):
    rows = slice(b * ATTN_BLOCK, (b + 1) * ATTN_BLOCK)
    ref[rows, 2 * j * 128:(2 * j + 1) * 128] = x[:ATTN_BLOCK]
    ref[rows, (2 * j + 1) * 128:(2 * j + 2) * 128] = x[ATTN_BLOCK:]


def _band_split_group(y, unit, tri):
    return _mx(jnp.concatenate([_band_split(y[unit:unit + ATTN_BLOCK], tri),
                                _band_split(y[unit + ATTN_BLOCK:unit + 2 * ATTN_BLOCK], tri)], axis=0))


def _attn_fwd(q, k, v, sink_cols):
    n = q.shape[0]
    nt = n // TILE_Q
    nb = TILE_Q // ATTN_BLOCK

    def body(q_ref, k_ref, v_ref, s_ref, o_ref):
        tri = _tri()
        s, _, vcats, _ = _attn_scores(q_ref, k_ref, v_ref, nb, tri)
        (p0, _), (p1, _) = _attn_probs(s, s_ref)
        p = jnp.concatenate([p0, p1], axis=1)
        for b in range(nb):
            for j in range(2):
                _put_kv_group(o_ref, b, j, _dot(_band_split_group(p, (b * 4 + 2 * j) * ATTN_BLOCK, tri), vcats[b][j]))

    return _call(
        body, "attn_fwd", nt,
        [_rows_spec(TILE_Q, D_ATTN, nt), _whole(), _whole(), _whole()],
        _rows_spec(TILE_Q, D_ATTN, nt), _sds((n, D_ATTN)),
    )(q, k, v, sink_cols)


def _attn_bwd(q, k, v, sink_cols, o, do):
    n = q.shape[0]
    nt = n // TILE_Q
    nb = TILE_Q // ATTN_BLOCK

    def body(q_ref, k_ref, v_ref, s_ref, o_ref, do_ref, dq_ref, dk_ref, dv_ref, ds_ref):
        @pl.when(pl.program_id(0) == 0)
        def _():
            dk_ref[...] = jnp.zeros_like(dk_ref)
            dv_ref[...] = jnp.zeros_like(dv_ref)
            ds_ref[...] = jnp.zeros_like(ds_ref)

        lo = lax.broadcasted_iota(jnp.int32, (256, 128), 1) < 64
        tri = _tri()
        s, kcats, vcats, kstarts = _attn_scores(q_ref, k_ref, v_ref, nb, tri)
        probs = _attn_probs(s, s_ref)
        do = do_ref[...]
        dob = _mx(do)
        od = do * o_ref[...]
        lo_q = (lax.broadcasted_iota(jnp.int32, od.shape, 1) & 64) == 0
        od_head = (jnp.where(lo_q, od, 0.0), jnp.where(lo_q, 0.0, od))
        units = [(b, i) for b in range(nb) for i in range(4)]

        def tile_part(a, b, i):
            return a[b * ATTN_BLOCK:(b + 1) * ATTN_BLOCK, i * 128:(i + 1) * 128]

        dp = []
        for b in range(nb):
            for j in range(2):
                x = _dot_nt(_kv_group(dob, b, j), vcats[b][j])
                dp += [_band_merge(x[:ATTN_BLOCK], tri), _band_merge(x[ATTN_BLOCK:], tri)]
        dp = jnp.concatenate(dp, axis=0)
        ds = []
        for hp in range(2):
            p, p_sink = probs[hp]
            delta = _row_sums(jnp.concatenate([tile_part(od_head[hp], b, i) for b, i in units], axis=0))
            ds.append(p * (dp[:, hp * 128:(hp + 1) * 128] - delta) * 0.125)
            t = p_sink * delta
            for i in range(4):
                ds_ref[2 * i + hp:2 * i + hp + 1, :] -= sum(
                    _sum0(t[(b * 4 + i) * ATTN_BLOCK:(b * 4 + i + 1) * ATTN_BLOCK]) for b in range(nb))
        ds = jnp.concatenate(ds, axis=1)
        p = jnp.concatenate([probs[0][0], probs[1][0]], axis=1)
        for b in range(nb):
            dkc, dvc = [], []
            for j in range(2):
                unit = (b * 4 + 2 * j) * ATTN_BLOCK
                dsb = _band_split_group(ds, unit, tri)
                _put_kv_group(dq_ref, b, j, _dot(dsb, kcats[b][j]))
                dkc.append(_dot_tn(dsb, _kv_group(q_ref, b, j)))
                dvc.append(_dot_tn(_band_split_group(p, unit, tri), _kv_group(dob, b, j)))
            dk_ref[pl.ds(kstarts[b], 256), :] += _kv_collect(dkc[0], dkc[1], lo)
            dv_ref[pl.ds(kstarts[b], 256), :] += _kv_collect(dvc[0], dvc[1], lo)

    r = _rows_spec(TILE_Q, D_ATTN, nt)
    return _call(
        body, "attn_bwd", nt,
        [r, _whole(), _whole(), _whole(), r, r],
        [r, _whole(), _whole(), _whole()],
        [_sds((n, D_ATTN)), _sds((n + ATTN_BLOCK, D_KV)), _sds((n + ATTN_BLOCK, D_KV)), _sds((8, 128))],
    )(q, k, v, sink_cols, o, do)


S5_DISC = ('s5_a_re', 's5_a_im', 's5_log_dt', 's5_b_re', 's5_b_im')
S5_TABLES = ('pw_re', 'pw_im', 'dbl_re', 'dbl_im', 'seg_re', 'seg_im', 'segr_re', 'segr_im')
S5_WEIGHTS = ('b_re', 'b_im', 'c_re', 'c_im', 'd', 'glu_w', 'glu_b', 'perm', 'perm_t')


def _s5_states(u, carry_r, carry_i, b_re, b_im, tab, hr_s, hi_s):
    ub = _mx(u)
    hr_s[...] = _dot(ub, b_re[...])
    hi_s[...] = _dot(ub, b_im[...])
    return ub, _cscan(hr_s, hi_s, tab, carry_r, carry_i, reverse=False)


def _s5_fwd(u, prm):
    n = u.shape[0]
    nt = n // (S5_PAIR * TILE)

    def body(u_ref, *refs):
        tab = dict(zip(S5_TABLES, refs[:8]))
        b_re, b_im, c_re, c_im, d_ref, gw_ref, gb_ref, perm, perm_t = refs[8:17]
        y_ref, cr_out, ci_out, cr_s, ci_s = refs[17:22]
        states = [refs[22 + 2 * j:24 + 2 * j] for j in range(S5_PAIR)]
        rows = [slice(j * TILE, (j + 1) * TILE) for j in range(S5_PAIR)]

        @pl.when(pl.program_id(0) == 0)
        def _():
            cr_s[...] = jnp.zeros_like(cr_s)
            ci_s[...] = jnp.zeros_like(ci_s)

        us = []
        for j in range(S5_PAIR):
            us.append(_permute_rows(perm[...], u_ref[rows[j], :]))
            ub = _mx(us[j])
            states[j][0][...] = _dot(ub, b_re[...])
            states[j][1][...] = _dot(ub, b_im[...])
        for j in range(S5_PAIR):
            cr, ci = cr_s[...], ci_s[...]
            cr_out[8 * j:8 * j + 8, :] = jnp.broadcast_to(cr, (8, N_STATE))
            ci_out[8 * j:8 * j + 8, :] = jnp.broadcast_to(ci, (8, N_STATE))
            cr_s[...], ci_s[...] = _cscan(*states[j], tab, cr, ci, reverse=False)
        for j in range(S5_PAIR):
            hr_s, hi_s = states[j]
            y = _dot(_mx(hr_s[...]), c_re[...]) - _dot(_mx(hi_s[...]), c_im[...]) + d_ref[...] * us[j]
            z = _gelu(y)
            y_ref[rows[j], :] = _permute_rows(perm_t[...], z * _sigmoid(_dot(_mx(z), gw_ref[...]) + gb_ref[...]))

    r = functools.partial(_rows_spec, n_tiles=nt)
    return _call(
        body, "s5_fwd", nt,
        [r(S5_PAIR * TILE, D_S5)] + [_whole()] * 17,
        [r(S5_PAIR * TILE, D_S5), r(S5_PAIR * 8, N_STATE), r(S5_PAIR * 8, N_STATE)],
        [_sds((n, D_S5)), _sds((n // TILE * 8, N_STATE)), _sds((n // TILE * 8, N_STATE))],
        scratch=[pltpu.VMEM((1, N_STATE), F32)] * 2 + [pltpu.VMEM((TILE, N_STATE), F32)] * (2 * S5_PAIR),
    )(u, *[prm[k] for k in S5_TABLES + S5_WEIGHTS])


def _s5_bwd(u, dys, carry_re, carry_im, prm):
    n = u.shape[0]
    nt = n // (S5_PAIR * TILE)

    def body(u_ref, dy_ref, cin_r, cin_i, *refs):
        tab = dict(zip(S5_TABLES, refs[:8]))
        b_re, b_im, c_re, c_im, d_ref, gw_ref, gb_ref, perm, perm_t = refs[8:17]
        du_ref, dbr_ref, dbi_ref, dcr_ref, dci_ref, dar_ref, dai_ref, dd_ref, dgw_ref, dgb_ref = refs[17:27]
        gr_s, gi_s = refs[27:29]
        scratch = [refs[29 + 4 * j:33 + 4 * j] for j in range(S5_PAIR)]
        later_first = list(reversed(range(S5_PAIR)))
        rows = [slice(j * TILE, (j + 1) * TILE) for j in range(S5_PAIR)]

        @pl.when(pl.program_id(0) == 0)
        def _():
            for ref in (dbr_ref, dbi_ref, dcr_ref, dci_ref, dar_ref, dai_ref, dd_ref, dgw_ref, dgb_ref, gr_s, gi_s):
                ref[...] = jnp.zeros_like(ref)

        us, ubs, carries, dy_of = {}, {}, {}, {}
        for j in later_first:
            us[j] = _permute_rows(perm[...], u_ref[rows[j], :])
            carries[j] = (cin_r[8 * j:8 * j + 1, :], cin_i[8 * j:8 * j + 1, :])
            ubs[j], _ = _s5_states(us[j], *carries[j], b_re, b_im, tab, *scratch[j][:2])
        for j in later_first:
            u = us[j]
            hr_s, hi_s, gr_t, gi_t = scratch[j]
            hrb, hib = _mx(hr_s[...]), _mx(hi_s[...])
            y = _dot(hrb, c_re[...]) - _dot(hib, c_im[...]) + d_ref[...] * u
            z = _gelu(y)
            zb = _mx(z)
            sg = _sigmoid(_dot(zb, gw_ref[...]) + gb_ref[...])
            dout = _permute_rows(perm[...], dy_ref[rows[j], :])
            dpre = dout * z * sg * (1.0 - sg)
            dgb_ref[...] += _sum0(dpre)
            dpb = _mx(dpre)
            dgw_ref[...] += _dot_tn(zb, dpb)
            dy = (dout * sg + _dot_nt(dpb, gw_ref[...])) * _gelu_grad(y)
            dd_ref[...] += _sum0(dy * u)
            dyb = _mx(dy)
            dcr_ref[...] += _dot_tn(hrb, dyb)
            dci_ref[...] -= _dot_tn(hib, dyb)
            gr_t[...] = _dot_nt(dyb, c_re[...])
            gi_t[...] = -_dot_nt(dyb, c_im[...])
            dy_of[j] = dy
        for j in later_first:
            gr_s[...], gi_s[...] = _cscan(*scratch[j][2:], tab, gr_s[...], gi_s[...], reverse=True)
        for j in later_first:
            hr_s, hi_s, gr_t, gi_t = scratch[j]
            cr, ci = carries[j]
            sub = _row_iota((N_SEG, N_STATE))
            acc_r = acc_i = jnp.zeros((N_SEG, N_STATE), F32)
            for k in range(SEG):
                if k == 0:
                    hpr = jnp.where(sub >= 1, pltpu.roll(hr_s[_seg_rows(SEG - 1), :], 1, 0), cr)
                    hpi = jnp.where(sub >= 1, pltpu.roll(hi_s[_seg_rows(SEG - 1), :], 1, 0), ci)
                else:
                    hpr, hpi = hr_s[_seg_rows(k - 1), :], hi_s[_seg_rows(k - 1), :]
                gr, gi = gr_t[_seg_rows(k), :], gi_t[_seg_rows(k), :]
                acc_r = acc_r + gr * hpr + gi * hpi
                acc_i = acc_i + gi * hpr - gr * hpi
            dar_ref[...] += _sum0(acc_r)
            dai_ref[...] += _sum0(acc_i)
            grb, gib = _mx(gr_t[...]), _mx(gi_t[...])
            dbr_ref[...] += _dot_tn(ubs[j], grb)
            dbi_ref[...] += _dot_tn(ubs[j], gib)
            du_ref[rows[j], :] = _permute_rows(perm_t[...], dy_of[j] * d_ref[...] + _dot_nt(grb, b_re[...]) + _dot_nt(gib, b_im[...]))

    r = functools.partial(_rows_spec, n_tiles=nt, reverse=True)
    return _call(
        body, "s5_bwd", nt,
        [r(S5_PAIR * TILE, D_S5), r(S5_PAIR * TILE, D_S5), r(S5_PAIR * 8, N_STATE), r(S5_PAIR * 8, N_STATE)] + [_whole()] * 17,
        [r(S5_PAIR * TILE, D_S5)] + [_whole()] * 9,
        [_sds((n, D_S5)), _sds((D_S5, N_STATE)), _sds((D_S5, N_STATE)), _sds((N_STATE, D_S5)), _sds((N_STATE, D_S5)),
         _sds((1, N_STATE)), _sds((1, N_STATE)), _sds((1, D_S5)), _sds((D_S5, D_S5)), _sds((1, D_S5))],
        scratch=[pltpu.VMEM((1, N_STATE), F32)] * 2 + [pltpu.VMEM((TILE, N_STATE), F32)] * (4 * S5_PAIR),
    )(u, dys, carry_re, carry_im, *[prm[k] for k in S5_TABLES + S5_WEIGHTS])


def _lru_gates(xr, halo, tile_index, cw_ref, cb_ref, wx_ref, wa_ref, bx_ref, ba_ref, sp_ref):
    ext = jnp.concatenate([halo, xr], axis=0)
    sh = [xr] + [_shift_down(ext, j, TILE) for j in (1, 2, 3)]
    xc = cb_ref[...] + cw_ref[3:4, :] * sh[0] + cw_ref[2:3, :] * sh[1] + cw_ref[1:2, :] * sh[2] + cw_ref[0:1, :] * sh[3]
    xb = _mx(xc)
    gx = _sigmoid(_dot(xb, wx_ref[...]) + bx_ref[...])
    ga = _sigmoid(_dot(xb, wa_ref[...]) + ba_ref[...])
    la = -LRU_C * ga * sp_ref[...]
    a = jnp.exp(la)
    start = (tile_index * TILE + _row_iota(xr.shape)) == 0
    mult = jnp.where(start, 1.0, jnp.sqrt(-jnp.tanh(la) * (a * a + 1.0)))
    return sh, xc, xb, gx, ga, a, mult, start


def _lru_fwd(xr, gate, prm):
    n = xr.shape[0]
    nt = n // TILE

    def body(x_ref, g_ref, cw_ref, cb_ref, wx_ref, wa_ref, bx_ref, ba_ref, sp_ref, y_ref, c_out, halo_s, c_s):
        first_tile = pl.program_id(0) == 0

        @pl.when(first_tile)
        def _():
            halo_s[...] = jnp.zeros_like(halo_s)
            c_s[...] = jnp.zeros_like(c_s)

        xr = x_ref[...]
        _, xc, _, gx, _, a, mult, _ = _lru_gates(xr, halo_s[...], pl.program_id(0), cw_ref, cb_ref, wx_ref, wa_ref, bx_ref, ba_ref,
                                                 sp_ref)
        halo_s[...] = xr[TILE - 8:]
        acum, h = _rscan(a, mult * gx * xc, reverse=False)
        c = c_s[...]
        c_out[...] = jnp.broadcast_to(c, (8, D_LRU))
        h = h + acum * c
        c_s[...] = h[TILE - 1:TILE]
        y_ref[...] = h * _gelu(g_ref[...])

    r = functools.partial(_rows_spec, n_tiles=nt)
    return _call(
        body, "lru_fwd", nt,
        [r(TILE, D_LRU), r(TILE, D_LRU)] + [_whole()] * 7,
        [r(TILE, D_LRU), r(8, D_LRU)],
        [_sds((n, D_LRU)), _sds((nt * 8, D_LRU))],
        scratch=[pltpu.VMEM((8, D_LRU), F32), pltpu.VMEM((1, D_LRU), F32)],
    )(xr, gate, prm['conv_w'], prm['conv_b'], prm['wx'], prm['wa'], prm['bx'], prm['ba'], prm['sp'])


def _lru_bwd(xr, gate, dyl, carry, prm):
    n = xr.shape[0]
    nt = n // TILE

    def body(x_ref, xh_ref, g_ref, dy_ref, cin_ref, cw_ref, cb_ref, wx_ref, wa_ref, bx_ref, ba_ref, sp_ref,
             dx_ref, dg_ref, dcw0, dcw1, dcw2, dcw3, dcb_ref, dwx_ref, dwa_ref, dbx_ref, dba_ref, dsp_ref, an_s, gn_s, dn_s):
        first_tile = pl.program_id(0) == nt - 1

        @pl.when(pl.program_id(0) == 0)
        def _():
            for ref in (dcw0, dcw1, dcw2, dcw3, dcb_ref, dwx_ref, dwa_ref, dbx_ref, dba_ref, dsp_ref, gn_s, dn_s):
                ref[...] = jnp.zeros_like(ref)
            an_s[...] = jnp.ones_like(an_s)

        xr = x_ref[...]
        halo = jnp.where(first_tile, 0.0, xh_ref[...])
        sh, xc, xb, gx, ga, a, mult, start = _lru_gates(xr, halo, nt - 1 - pl.program_id(0), cw_ref, cb_ref, wx_ref, wa_ref, bx_ref,
                                                        ba_ref, sp_ref)
        acum, h = _rscan(a, mult * gx * xc, reverse=False)
        cin = cin_ref[0:1, :]
        h = h + acum * cin
        gate = g_ref[...]
        dyl = dy_ref[...]
        dg_ref[...] = dyl * h * _gelu_grad(gate)
        row = _row_iota(xr.shape)
        alpha = jnp.where(row < TILE - 1, pltpu.roll(a, TILE - 1, 0), an_s[...])
        racc, g = _rscan(alpha, dyl * _gelu(gate), reverse=True)
        g = g + racc * gn_s[...]
        an_s[...] = a[0:1]
        gn_s[...] = g[0:1]
        hprev = jnp.where(row == 0, cin, pltpu.roll(h, 1, 0))
        da = g * hprev
        dmult = jnp.where(start, 0.0, g * gx * xc)
        dla = da * a - dmult * a * a / mult
        dsp_ref[...] += _sum0(-LRU_C * ga * dla)
        dpa = (-LRU_C * sp_ref[...] * dla) * ga * (1.0 - ga)
        dpx = (g * mult * xc) * gx * (1.0 - gx)
        dba_ref[...] += _sum0(dpa)
        dbx_ref[...] += _sum0(dpx)
        dpab, dpxb = _mx(dpa), _mx(dpx)
        dwa_ref[...] += _dot_tn(xb, dpab)
        dwx_ref[...] += _dot_tn(xb, dpxb)
        dxc = g * mult * gx + _dot_nt(dpab, wa_ref[...]) + _dot_nt(dpxb, wx_ref[...])
        dcb_ref[...] += _sum0(dxc)
        dcw3[...] += _sum0(dxc * sh[0])
        dcw2[...] += _sum0(dxc * sh[1])
        dcw1[...] += _sum0(dxc * sh[2])
        dcw0[...] += _sum0(dxc * sh[3])
        ext = jnp.concatenate([dxc, dn_s[...]], axis=0)
        dx_ref[...] = (cw_ref[3:4, :] * dxc + cw_ref[2:3, :] * _shift_up(ext, 1, TILE) + cw_ref[1:2, :] * _shift_up(ext, 2, TILE)
                       + cw_ref[0:1, :] * _shift_up(ext, 3, TILE))
        dn_s[...] = dxc[:8]

    r = functools.partial(_rows_spec, n_tiles=nt, reverse=True)
    vec = _sds((1, D_LRU))
    return _call(
        body, "lru_bwd", nt,
        [r(TILE, D_LRU), _halo_spec(D_LRU, TILE, nt, reverse=True), r(TILE, D_LRU), r(TILE, D_LRU), r(8, D_LRU)] + [_whole()] * 7,
        [r(TILE, D_LRU), r(TILE, D_LRU)] + [_whole()] * 10,
        [_sds((n, D_LRU)), _sds((n, D_LRU)), vec, vec, vec, vec, vec, _sds((D_LRU, D_LRU)), _sds((D_LRU, D_LRU)), vec, vec, vec],
        scratch=[pltpu.VMEM((1, D_LRU), F32), pltpu.VMEM((1, D_LRU), F32), pltpu.VMEM((8, D_LRU), F32)],
    )(xr, xr, gate, dyl, carry, prm['conv_w'], prm['conv_b'], prm['wx'], prm['wa'], prm['bx'], prm['ba'], prm['sp'])


def _normed_parts(ya, ys, yl):
    return jnp.concatenate([ya * _rms(ya), ys * _rms(ys), yl * _rms(yl)], axis=1)


def _mixout_fwd(ya, ys, yl, x0, g_mix, w_out, b_out, g1, b1):
    n = x0.shape[0]
    nt = n // TILE_WIDE

    def body(ya_ref, ys_ref, yl_ref, x_ref, gm_ref, w_ref, b_ref, g_ref, be_ref, mix_ref, r_ref, x1_ref):
        mixb = _mx(_normed_parts(ya_ref[...], ys_ref[...], yl_ref[...]) * gm_ref[...])
        mix_ref[...] = mixb
        r1 = ALPHA * x_ref[...] + _dot(mixb, w_ref[...]) + b_ref[...]
        r_ref[...] = r1
        xhat, _ = _ln_stats(r1)
        x1_ref[...] = xhat * g_ref[...] + be_ref[...]

    r = functools.partial(_rows_spec, n_tiles=nt)
    return _call(
        body, "mixout_fwd", nt,
        [r(TILE_WIDE, D_ATTN), r(TILE_WIDE, D_S5), r(TILE_WIDE, D_LRU), r(TILE_WIDE, D)] + [_whole()] * 5,
        [r(TILE_WIDE, D), r(TILE_WIDE, D), r(TILE_WIDE, D)],
        [_sds((n, D), MXU), _sds((n, D)), _sds((n, D))],
    )(ya, ys, yl, x0, g_mix, w_out, b_out, g1, b1)


def _mixout_bwd(dr1, mix, ya, ys, yl, g_mix, w_out):
    n = dr1.shape[0]
    nt = n // TILE_WIDE

    def body(dr_ref, mix_ref, ya_ref, ys_ref, yl_ref, gm_ref, w_ref, dya_ref, dys_ref, dyl_ref, dw_ref, db_ref, dgm_ref):
        @pl.when(pl.program_id(0) == 0)
        def _():
            for ref in (dw_ref, db_ref, dgm_ref):
                ref[...] = jnp.zeros_like(ref)

        dr = dr_ref[...]
        db_ref[...] += _sum0(dr)
        drb = _mx(dr)
        dw_ref[...] += _dot_tn(mix_ref[...], drb)
        dmix = _dot(drb, w_ref[...])
        parts = (ya_ref[...], ys_ref[...], yl_ref[...])
        dgm_ref[...] += _sum0(dmix * _normed_parts(*parts))
        dn = dmix * gm_ref[...]
        lo = 0
        for y, out in zip(parts, (dya_ref, dys_ref, dyl_ref)):
            w = y.shape[1]
            rs = _rms(y)
            nrm = y * rs
            dnp = dn[:, lo:lo + w]
            out[...] = rs * (dnp - nrm * jnp.mean(dnp * nrm, axis=-1, keepdims=True))
            lo += w

    r = functools.partial(_rows_spec, n_tiles=nt)
    return _call(
        body, "mixout_bwd", nt,
        [r(TILE_WIDE, D), r(TILE_WIDE, D), r(TILE_WIDE, D_ATTN), r(TILE_WIDE, D_S5), r(TILE_WIDE, D_LRU), _whole(), _whole()],
        [r(TILE_WIDE, D_ATTN), r(TILE_WIDE, D_S5), r(TILE_WIDE, D_LRU), _whole(), _whole(), _whole()],
        [_sds((n, D_ATTN)), _sds((n, D_S5)), _sds((n, D_LRU)), _sds((D, D)), _sds((1, D)), _sds((1, D))],
    )(dr1, mix, ya, ys, yl, g_mix, w_out)


def _ffn_conv(gp, halo, cw_ref, cb_ref, cs):
    ext = jnp.concatenate([halo, gp], axis=0)
    s1 = _shift_down(ext, 1, TILE)
    s2 = _shift_down(ext, 2, TILE)
    return s1, s2, cb_ref[:, cs] + cw_ref[2:3, cs] * gp + cw_ref[1:2, cs] * s1 + cw_ref[0:1, cs] * s2


def _ffn_fwd(x1, wg, wu, cw, cb, wd, g2, b2):
    n = x1.shape[0]
    nt = n // TILE

    def body(x_ref, wg_ref, wu_ref, cw_ref, cb_ref, wd_ref, g_ref, be_ref, gp_ref, up_ref, r_ref, x2_ref, halo_s, act_s):
        @pl.when(pl.program_id(0) == 0)
        def _():
            halo_s[...] = jnp.zeros_like(halo_s)

        x1 = x_ref[...]
        xb = _mx(x1)
        for c in range(D_FF // FF_CHUNK):
            cs = slice(c * FF_CHUNK, (c + 1) * FF_CHUNK)
            gp = _dot(xb, wg_ref[:, cs])
            up = _dot(xb, wu_ref[:, cs])
            gp_ref[:, cs] = gp
            up_ref[:, cs] = up
            _, _, gc = _ffn_conv(gp, halo_s[:, cs], cw_ref, cb_ref, cs)
            halo_s[:, cs] = gp[TILE - 8:]
            act_s[:, cs] = _mx(gc * _sigmoid(gc) * up)
        r2 = ALPHA * x1 + _dot(act_s[...], wd_ref[...])
        r_ref[...] = r2
        xhat, _ = _ln_stats(r2)
        x2_ref[...] = xhat * g_ref[...] + be_ref[...]

    r = functools.partial(_rows_spec, n_tiles=nt)
    return _call(
        body, "ffn_fwd", nt,
        [r(TILE, D)] + [_whole()] * 7,
        [r(TILE, D_FF), r(TILE, D_FF), r(TILE, D), r(TILE, D)],
        [_sds((n, D_FF)), _sds((n, D_FF)), _sds((n, D)), _sds((n, D))],
        scratch=[pltpu.VMEM((8, D_FF), F32), pltpu.VMEM((TILE, D_FF), MXU)],
    )(x1, wg, wu, cw, cb, wd, g2, b2)


def _ffn_bwd_down(dx2, r2, g2, gp, up, cw, cb, wd_t):
    n = dx2.shape[0]
    nt = n // TILE

    def body(dx_ref, r_ref, g_ref, gp_ref, gh_ref, up_ref, cw_ref, cb_ref, wd_ref,
             dr_ref, dgp_ref, dup_ref, dwd_ref, dcw0, dcw1, dcw2, dcb_ref, dg_ref, db_ref, next_s):
        first_tile = pl.program_id(0) == nt - 1

        @pl.when(pl.program_id(0) == 0)
        def _():
            for ref in (dwd_ref, dcw0, dcw1, dcw2, dcb_ref, dg_ref, db_ref, next_s):
                ref[...] = jnp.zeros_like(ref)

        dx2 = dx_ref[...]
        xhat, rstd = _ln_stats(r_ref[...])
        dg_ref[...] += _sum0(dx2 * xhat)
        db_ref[...] += _sum0(dx2)
        dr2 = _ln_bwd(dx2, g_ref[...], xhat, rstd)
        dr_ref[...] = dr2
        dfb = _mx(dr2)
        for c in range(D_FF // FF_CHUNK):
            cs = slice(c * FF_CHUNK, (c + 1) * FF_CHUNK)
            gp = gp_ref[:, cs]
            up = up_ref[:, cs]
            s1, s2, gc = _ffn_conv(gp, jnp.where(first_tile, 0.0, gh_ref[:, cs]), cw_ref, cb_ref, cs)
            sg = _sigmoid(gc)
            silu = gc * sg
            dact = _dot(dfb, wd_ref[:, cs])
            dwd_ref[cs, :] += _dot_tn(_mx(silu * up), dfb)
            dup_ref[:, cs] = _mx(dact * silu)
            dgc = dact * up * (sg + silu * (1.0 - sg))
            dcb_ref[:, cs] += _sum0(dgc)
            dcw2[:, cs] += _sum0(dgc * gp)
            dcw1[:, cs] += _sum0(dgc * s1)
            dcw0[:, cs] += _sum0(dgc * s2)
            ext = jnp.concatenate([dgc, next_s[:, cs]], axis=0)
            dgp_ref[:, cs] = _mx(cw_ref[2:3, cs] * dgc + cw_ref[1:2, cs] * _shift_up(ext, 1, TILE)
                                 + cw_ref[0:1, cs] * _shift_up(ext, 2, TILE))
            next_s[:, cs] = dgc[:8]

    r = functools.partial(_rows_spec, n_tiles=nt, reverse=True)
    vff = _sds((1, D_FF))
    return _call(
        body, "ffn_bwd_down", nt,
        [r(TILE, D), r(TILE, D), _whole(), r(TILE, D_FF), _halo_spec(D_FF, TILE, nt, reverse=True), r(TILE, D_FF), _whole(), _whole(),
         _whole()],
        [r(TILE, D), r(TILE, D_FF), r(TILE, D_FF)] + [_whole()] * 7,
        [_sds((n, D)), _sds((n, D_FF), MXU), _sds((n, D_FF), MXU), _sds((D_FF, D)), vff, vff, vff, vff, _sds((1, D)), _sds((1, D))],
        scratch=[pltpu.VMEM((8, D_FF), F32)],
    )(dx2, r2, g2, gp, gp, up, cw, cb, wd_t)


def _ffn_bwd_dx(dr2, dgp, dup, r1, g1, wg_t, wu_t):
    n = dr2.shape[0]
    rows = TILE_BIG
    nt = n // rows

    def body(dr2_ref, dgp_ref, dup_ref, r_ref, g_ref, wg_ref, wu_ref, dr1_ref, dg_ref, db_ref):
        @pl.when(pl.program_id(0) == 0)
        def _():
            for ref in (dg_ref, db_ref):
                ref[...] = jnp.zeros_like(ref)

        dx1 = ALPHA * dr2_ref[...] + _dot(dgp_ref[...], wg_ref[...]) + _dot(dup_ref[...], wu_ref[...])
        xhat, rstd = _ln_stats(r_ref[...])
        dg_ref[...] += _sum0(dx1 * xhat)
        db_ref[...] += _sum0(dx1)
        dr1_ref[...] = _ln_bwd(dx1, g_ref[...], xhat, rstd)

    r = functools.partial(_rows_spec, n_tiles=nt)
    return _call(
        body, "ffn_bwd_dx", nt,
        [r(rows, D), r(rows, D_FF), r(rows, D_FF), r(rows, D), _whole(), _whole(), _whole()],
        [r(rows, D), _whole(), _whole()],
        [_sds((n, D)), _sds((1, D)), _sds((1, D))],
    )(dr2, dgp, dup, r1, g1, wg_t, wu_t)


def _ffn_bwd_dw(x1, dgp, dup):
    n = x1.shape[0]
    rows = TILE_BIG
    nt = n // rows

    def body(x_ref, dgp_ref, dup_ref, dwg_ref, dwu_ref):
        @pl.when(pl.program_id(0) == 0)
        def _():
            for ref in (dwg_ref, dwu_ref):
                ref[...] = jnp.zeros_like(ref)

        xb = _mx(x_ref[...])
        for c in range(D_FF // FF_CHUNK):
            cs = slice(c * FF_CHUNK, (c + 1) * FF_CHUNK)
            dwg_ref[cs, :] += _dot_tn(dgp_ref[:, cs], xb)
            dwu_ref[cs, :] += _dot_tn(dup_ref[:, cs], xb)

    r = functools.partial(_rows_spec, n_tiles=nt)
    return _call(
        body, "ffn_bwd_dw", nt,
        [r(rows, D), r(rows, D_FF), r(rows, D_FF)], [_whole(), _whole()], [_sds((D_FF, D)), _sds((D_FF, D))],
    )(x1, dgp, dup)


def _loss_head(y, target):
    n = y.shape[0]
    nt = n // TILE_WIDE

    def body(y_ref, t_ref, loss_ref, dy_ref):
        @pl.when(pl.program_id(0) == 0)
        def _():
            loss_ref[...] = jnp.zeros_like(loss_ref)

        e = y_ref[...] - t_ref[...]
        dy_ref[...] = e * (1.0 / D)
        loss_ref[...] += _sum0(jnp.sum(e * e, axis=1, keepdims=True)) * (0.5 / D)

    r = functools.partial(_rows_spec, n_tiles=nt)
    return _call(body, "loss_head", nt, [r(TILE_WIDE, D), r(TILE_WIDE, D)], [_whole(), r(TILE_WIDE, D)],
                 [_sds((1, 1)), _sds((n, D))])(y, target)


def _place():
    x, y, c = lax.axis_index("x"), lax.axis_index("y"), lax.axis_index("c")
    return x, y, c, 4 * x + 2 * y + c


def _peer(x, y, c, k):
    px, py, pc = x ^ ((k >> 2) & 1), y ^ ((k >> 1) & 1), c ^ (k & 1)
    return (px, py, pc), 4 * px + 2 * py + pc


def _all_gather(name, blocks, small):
    srcs = list(blocks) + [small]
    n = len(srcs)
    out_shapes = [_sds((a.shape[0], N_DEV * a.shape[1], LANES), a.dtype) for a in blocks] + [_sds((N_DEV,) + small.shape, small.dtype)]

    def body(*refs):
        src_refs, out_refs = refs[:n], refs[n:2 * n]
        send_sems, recv_sems, local_sems = refs[2 * n:]
        x, y, c, me = _place()

        def landing(a, slot):
            if a == n - 1:
                return out_refs[a].at[slot]
            r = src_refs[a].shape[1]
            return out_refs[a].at[:, pl.ds(slot * r, r), :]

        def remote(a, k, slot):
            peer, _ = _peer(x, y, c, k)
            return pltpu.make_async_remote_copy(
                src_ref=src_refs[a], dst_ref=landing(a, slot), send_sem=send_sems.at[a * N_DEV + k],
                recv_sem=recv_sems.at[a * N_DEV + k], device_id=peer, device_id_type=pl.DeviceIdType.MESH)

        mine = [pltpu.make_async_copy(src_refs[a], landing(a, me), local_sems.at[a]) for a in range(n)]
        sends = [remote(a, k, me) for a in range(n) for k in range(1, N_DEV)]
        for cp in mine + sends:
            cp.start()
        for a in range(n):
            for k in range(1, N_DEV):
                remote(a, k, _peer(x, y, c, k)[1]).wait_recv()
        for cp in sends:
            cp.wait_send()
        for cp in mine:
            cp.wait()

    any_space = pl.BlockSpec(memory_space=pl.ANY)
    return pl.pallas_call(
        body, name=name, out_shape=out_shapes, in_specs=[any_space] * n, out_specs=[any_space] * n,
        scratch_shapes=[pltpu.SemaphoreType.DMA((n * N_DEV,)), pltpu.SemaphoreType.DMA((n * N_DEV,)), pltpu.SemaphoreType.DMA((n,))],
    )(*srcs)


_HBM = pl.BlockSpec(memory_space=pltpu.HBM)
_SEM = pl.BlockSpec(memory_space=pltpu.SEMAPHORE)
_EFFECT = pltpu.SideEffectType.DATAFLOW_SIDE_EFFECTING


def _in_hbm(a):
    return pltpu.with_memory_space_constraint(a, pltpu.HBM)


def _scatter_start(name, srcs):
    ns = len(srcs)
    rows_a = [a.shape[0] // N_DEV for a in srcs]
    offs = [sum(rows_a[:a]) for a in range(ns)]
    total = sum(rows_a)

    def body(*refs):
        src_refs, land_ref, send_sems, recv_sems, token = refs[:ns], refs[ns], refs[ns + 1], refs[ns + 2], refs[-1]
        x, y, c, me = _place()
        for a in range(ns):
            pltpu.make_async_copy(src_refs[a].at[pl.ds(me * rows_a[a], rows_a[a]), :],
                                  land_ref.at[me, pl.ds(offs[a], rows_a[a]), :], send_sems.at[0]).start()
        for k in range(1, N_DEV):
            peer, peer_slot = _peer(x, y, c, k)
            for a in range(ns):
                pltpu.make_async_remote_copy(
                    src_ref=src_refs[a].at[pl.ds(peer_slot * rows_a[a], rows_a[a]), :],
                    dst_ref=land_ref.at[me, pl.ds(offs[a], rows_a[a]), :], send_sem=send_sems.at[k], recv_sem=recv_sems.at[k],
                    device_id=peer, device_id_type=pl.DeviceIdType.MESH).start()
        token[...] = jnp.zeros_like(token)

    landing = lax.empty((N_DEV, total, LANES), F32)
    out = pl.pallas_call(
        body, name=name,
        out_shape=(pltpu.SemaphoreType.DMA((N_DEV,)), pltpu.SemaphoreType.DMA((N_DEV,)), *[pltpu.HBM(a.shape, a.dtype) for a in srcs],
                   pltpu.HBM(landing.shape, F32), _sds((8, 128))),
        in_specs=[_HBM] * (ns + 1), out_specs=(_SEM, _SEM, *[_HBM] * (ns + 1), pl.BlockSpec(memory_space=pltpu.VMEM)),
        input_output_aliases={a: 2 + a for a in range(ns + 1)},
        compiler_params=pltpu.CompilerParams(has_side_effects=_EFFECT),
    )(*[_in_hbm(a) for a in srcs], _in_hbm(landing))
    return out[0], out[1], out[2:2 + ns], out[2 + ns], out[-1]


def _scatter_wait(name, send_sems, recv_sems, srcs, landing, after):
    ns = len(srcs)

    def body(*refs):
        land_ref, send_ref, recv_ref = refs[ns], refs[ns + 1], refs[ns + 2]
        x, y, c, me = _place()
        pltpu.make_async_copy(land_ref.at[me], land_ref.at[me], send_ref.at[0]).wait()
        for k in range(1, N_DEV):
            peer, peer_slot = _peer(x, y, c, k)
            slot = pltpu.make_async_remote_copy(
                src_ref=land_ref.at[me], dst_ref=land_ref.at[peer_slot], send_sem=send_ref.at[k], recv_sem=recv_ref.at[k],
                device_id=peer, device_id_type=pl.DeviceIdType.MESH)
            slot.wait_send()
            slot.wait_recv()

    out = pl.pallas_call(
        body, name=name, out_shape=(*[pltpu.HBM(a.shape, a.dtype) for a in srcs], pltpu.HBM(landing.shape, landing.dtype)),
        in_specs=[_HBM] * (ns + 1) + [_SEM, _SEM, pl.BlockSpec(memory_space=pl.ANY)], out_specs=[_HBM] * (ns + 1),
        input_output_aliases={a: a for a in range(ns + 1)},
        compiler_params=pltpu.CompilerParams(has_side_effects=_EFFECT),
    )(*srcs, landing, send_sems, recv_sems, after)
    return out[:ns], out[ns]


def _gather_start(name, blocks):
    n = len(blocks)

    def body(*refs):
        src_refs, land_refs, send_sems, recv_sems, token = refs[:n], refs[n:2 * n], refs[2 * n], refs[2 * n + 1], refs[-1]
        x, y, c, me = _place()
        for a in range(n):
            r = src_refs[a].shape[1]
            mine = land_refs[a].at[:, pl.ds(me * r, r), :]
            pltpu.make_async_copy(src_refs[a], mine, send_sems.at[a * N_DEV]).start()
            for k in range(1, N_DEV):
                pltpu.make_async_remote_copy(
                    src_ref=src_refs[a], dst_ref=mine, send_sem=send_sems.at[a * N_DEV + k],
                    recv_sem=recv_sems.at[a * N_DEV + k], device_id=_peer(x, y, c, k)[0], device_id_type=pl.DeviceIdType.MESH).start()
        token[...] = jnp.zeros_like(token)

    wholes = [lax.empty((a.shape[0], N_DEV * a.shape[1], LANES), a.dtype) for a in blocks]
    out = pl.pallas_call(
        body, name=name,
        out_shape=(pltpu.SemaphoreType.DMA((n * N_DEV,)), pltpu.SemaphoreType.DMA((n * N_DEV,)),
                   *[pltpu.HBM(a.shape, a.dtype) for a in blocks + wholes], _sds((8, 128))),
        in_specs=[_HBM] * (2 * n), out_specs=(_SEM, _SEM, *[_HBM] * (2 * n), pl.BlockSpec(memory_space=pltpu.VMEM)),
        input_output_aliases={a: 2 + a for a in range(2 * n)},
        compiler_params=pltpu.CompilerParams(has_side_effects=_EFFECT),
    )(*[_in_hbm(a) for a in blocks + wholes])
    return out[0], out[1], out[2:2 + n], out[2 + n:2 + 2 * n], out[-1]


def _gather_wait(name, send_sems, recv_sems, blocks, wholes, after):
    n = len(blocks)

    def body(*refs):
        src_refs, land_refs, send_ref, recv_ref = refs[:n], refs[n:2 * n], refs[2 * n], refs[2 * n + 1]
        x, y, c, me = _place()
        for a in range(n):
            r = src_refs[a].shape[1]
            pltpu.make_async_copy(src_refs[a], land_refs[a].at[:, pl.ds(me * r, r), :], send_ref.at[a * N_DEV]).wait()
            for k in range(1, N_DEV):
                peer, peer_slot = _peer(x, y, c, k)
                cp = pltpu.make_async_remote_copy(
                    src_ref=src_refs[a], dst_ref=land_refs[a].at[:, pl.ds(peer_slot * r, r), :], send_sem=send_ref.at[a * N_DEV + k],
                    recv_sem=recv_ref.at[a * N_DEV + k], device_id=peer, device_id_type=pl.DeviceIdType.MESH)
                cp.wait_send()
                cp.wait_recv()

    return pl.pallas_call(
        body, name=name, out_shape=tuple(pltpu.HBM(a.shape, a.dtype) for a in list(blocks) + list(wholes)),
        in_specs=[_HBM] * (2 * n) + [_SEM, _SEM, pl.BlockSpec(memory_space=pl.ANY)], out_specs=[_HBM] * (2 * n),
        input_output_aliases={a: a for a in range(2 * n)},
        compiler_params=pltpu.CompilerParams(has_side_effects=_EFFECT),
    )(*blocks, *wholes, send_sems, recv_sems, after)


def _reduce_adamw(parts, w, m, v, tile_rows=PACK_TILE):
    rows = w.shape[0]
    nt = rows // tile_rows
    slots = parts.shape[0]
    c1 = 1.0 - ADAM_B1 ** ADAM_STEP
    c2 = 1.0 - ADAM_B2 ** ADAM_STEP

    def body(p_ref, w_ref, m_ref, v_ref, g_out, d_out, m_out, v_out):
        g = p_ref[0]
        for s in range(1, slots):
            g = g + p_ref[s]
        m_new = ADAM_B1 * m_ref[...] + (1.0 - ADAM_B1) * g
        v_new = ADAM_B2 * v_ref[...] + (1.0 - ADAM_B2) * (g * g)
        g_out[...] = g
        m_out[...] = m_new
        v_out[...] = v_new
        d_out[...] = -ADAM_LR * ((m_new / c1) / (jnp.sqrt(v_new / c2) + ADAM_EPS) + ADAM_WD * w_ref[...])

    r = _rows_spec(tile_rows, LANES, nt)
    out = _sds((rows, LANES))
    return _call(
        body, "reduce_adamw", nt,
        [pl.BlockSpec((slots, tile_rows, LANES), lambda i: (0, i, 0)), r, r, r], [r, r, r, r], [out, out, out, out],
    )(parts, w, m, v)


def _pack_rows(a, lead=0):
    head = a.shape[:lead]
    flat = a.reshape(head + (-1,))
    size = flat.shape[-1]
    rows = -(-size // (16 * LANES)) * 16
    flat = jnp.pad(flat, [(0, 0)] * lead + [(0, rows * LANES - size)])
    return flat.reshape(head + (rows, LANES))


def _packed_rows(shape):
    return -(-math.prod(shape) // (16 * LANES)) * 16


def _to_blocks(full, axis):
    l, a, b = full.shape
    if axis == 2:
        return full.reshape(l, a, N_DEV, b // N_DEV).transpose(2, 0, 1, 3)
    return full.reshape(l, N_DEV, a // N_DEV, b).transpose(1, 0, 2, 3)


def _from_blocks(blocks, axis):
    _, l, a, b = blocks.shape
    if axis == 2:
        return blocks.transpose(1, 2, 0, 3).reshape(l, a, N_DEV * b)
    return blocks.transpose(1, 0, 2, 3).reshape(l, N_DEV * a, b)


def _row_form(shard, transposed):
    return shard.transpose(0, 2, 1) if transposed else shard


def _me():
    return 4 * lax.axis_index("x") + 2 * lax.axis_index("y") + lax.axis_index("c")


def _both_forms(names, wholes, layer):
    out = {}
    for name, w in zip(names, wholes):
        t = dict(BIG)[name]
        out[name + '_t' if t else name] = w[layer]
        out[name if t else name + '_t'] = w[layer].T
    return out


def _gather_weights(local):
    segs, meta = [], []
    for name in SMALL_SHARDED:
        blk = local[name]
        if name in GATHER_F32:
            bits = lax.bitcast_convert_type(blk, MXU)
        else:
            bits = _mx(blk)
        seg = _pack_rows(bits)
        meta.append((name, bits.shape, seg.shape[0]))
        segs.append(seg)
    blocks = {name: _mx(_row_form(local[name], t)) for name, t in BIG}
    mix, ffn = PARTS['mix'], PARTS['ffn']
    *first, gathered = _all_gather("gather_weights", [blocks[n][:1] for n in mix], jnp.concatenate(segs, axis=0))
    flights = {'ffn0': _gather_start("gather_ffn0_start", [blocks[n][:1] for n in ffn]),
               'later': _gather_start("gather_later_start", [blocks[n][1:] for n in mix + ffn])}
    out, lo = {}, 0
    for name, bits_shape, rows in meta:
        seg = gathered[:, lo:lo + rows].reshape(N_DEV, -1)[:, :math.prod(bits_shape)].reshape((N_DEV,) + bits_shape)
        if name in GATHER_F32:
            seg = lax.bitcast_convert_type(seg, F32)
        out[name] = _from_blocks(seg, SHARD_AXIS[name])
        lo += rows
    ready = {(0, 'mix'): _both_forms(mix, first, 0)}

    def landed(flight, names, after):
        send_sems, recv_sems, mine, wholes, _ = flights[flight]
        return _gather_wait(f"gather_{flight}_wait", send_sems, recv_sems, mine, wholes, after)[len(names):]

    def big_weights(l, part, after):
        if (l, part) not in ready and l == 0:
            ready[(0, 'ffn')] = _both_forms(ffn, landed('ffn0', ffn, after), 0)
        elif (l, part) not in ready:
            wholes = landed('later', mix + ffn, after)
            for j in range(1, DEPTH):
                forms = _both_forms(mix + ffn, wholes, j - 1)
                for p, names in PARTS.items():
                    ready[(j, p)] = {k: forms[k] for n in names for k in (n, n + '_t')}
        return ready[(l, part)]

    return out, big_weights, flights['ffn0'][-1][0, 0] + flights['later'][-1][0, 0]


def _s5_discretize(a_re, a_im, log_dt, b_re, b_im):
    lam_re = jnp.minimum(a_re, -1e-4)
    lam_im = a_im
    dt = jnp.exp(log_dt)[:, None]
    decay = jnp.exp(dt * lam_re)
    ang = dt * lam_im
    abar_re = decay * jnp.cos(ang)
    abar_im = decay * jnp.sin(ang)
    den = jnp.square(lam_re) + jnp.square(lam_im)
    nr = abar_re - 1.0
    ni = abar_im
    coef_re = (nr * lam_re + ni * lam_im) / den
    coef_im = (ni * lam_re - nr * lam_im) / den
    bbar_re = coef_re[..., None] * b_re - coef_im[..., None] * b_im
    bbar_im = coef_re[..., None] * b_im + coef_im[..., None] * b_re
    return abar_re, abar_im, bbar_re, bbar_im


def _complex_powers(ar, ai, count):
    def combine(e1, e2):
        return e2[0] * e1[0] - e2[1] * e1[1], e2[0] * e1[1] + e2[1] * e1[0]

    shape = (count,) + ar.shape
    return lax.associative_scan(combine, (jnp.broadcast_to(ar, shape), jnp.broadcast_to(ai, shape)), axis=0)


_EYE16 = functools.partial(jnp.eye, 16, dtype=F32)


def _s5_params(p):
    disc, disc_vjp = jax.vjp(jax.vmap(_s5_discretize), p['s5_a_re'], p['s5_a_im'], p['s5_log_dt'], p['s5_b_re'], p['s5_b_im'])
    abar_re, abar_im, bbar_re, bbar_im = disc
    pw_re, pw_im = _complex_powers(abar_re.reshape(DEPTH, N_STATE), abar_im.reshape(DEPTH, N_STATE), SEG)
    sp_re, sp_im = _complex_powers(pw_re[SEG - 1], pw_im[SEG - 1], N_SEG - 1)
    one, zero = jnp.ones((1, DEPTH, N_STATE), F32), jnp.zeros((1, DEPTH, N_STATE), F32)
    seg_re = jnp.concatenate([one, sp_re], axis=0)
    seg_im = jnp.concatenate([zero, sp_im], axis=0)
    doubling = [0, 1, 3]
    blank = jnp.zeros((5, DEPTH, N_STATE), F32)
    tables = {
        'pw_re': pw_re, 'pw_im': pw_im,
        'dbl_re': jnp.concatenate([jnp.stack([sp_re[k] for k in doubling]), blank], axis=0),
        'dbl_im': jnp.concatenate([jnp.stack([sp_im[k] for k in doubling]), blank], axis=0),
        'seg_re': seg_re, 'seg_im': seg_im, 'segr_re': seg_re[::-1], 'segr_im': seg_im[::-1],
    }
    weights = {
        'b_re': _mx(jnp.einsum('lgpc,gh->lgchp', bbar_re, _EYE16()).reshape(DEPTH, D_S5, N_STATE)),
        'b_im': _mx(jnp.einsum('lgpc,gh->lgchp', bbar_im, _EYE16()).reshape(DEPTH, D_S5, N_STATE)),
        'c_re': _mx(jnp.einsum('lgcp,gh->lgphc', p['s5_c_re'], _EYE16()).reshape(DEPTH, N_STATE, D_S5)),
        'c_im': _mx(jnp.einsum('lgcp,gh->lgphc', p['s5_c_im'], _EYE16()).reshape(DEPTH, N_STATE, D_S5)),
    }
    src = (jnp.arange(TILE) % N_SEG) * SEG + jnp.arange(TILE) // N_SEG
    perm = (src[:, None] == jnp.arange(TILE)[None, :]).astype(MXU)
    layers = []
    for l in range(DEPTH):
        prm = {k: v[:, l] for k, v in tables.items()}
        prm.update({k: v[l] for k, v in weights.items()})
        prm.update({'perm': perm, 'perm_t': perm.T, 'd': p['s5_d'][l][None, :], 'glu_w': p['s5_glu_w'][l],
                    'glu_b': p['s5_glu_b'][l][None, :]})
        layers.append(prm)
    return layers, disc_vjp


def _lru_params(p, l):
    eye4 = jnp.eye(4, dtype=F32)
    return {
        'conv_w': p['lru_conv_w'][l], 'conv_b': p['lru_conv_b'][l][None, :],
        'wx': _mx(jnp.einsum('hij,hk->hikj', p['lru_wx'][l], eye4).reshape(D_LRU, D_LRU)),
        'wa': _mx(jnp.einsum('hij,hk->hikj', p['lru_wa'][l], eye4).reshape(D_LRU, D_LRU)),
        'bx': p['lru_bx'][l][None, :], 'ba': p['lru_ba'][l][None, :],
        'sp': jax.nn.softplus(-p['lru_a_param'][l])[None, :],
    }


def _rope_tables(n):
    inv_freq = ROPE_THETA ** (-jnp.arange(0, 64, 2, dtype=F32) / 64)
    ang = jnp.arange(n, dtype=F32)[:, None] * inv_freq[None, :]
    cos, sin = jnp.cos(ang), jnp.sin(ang)
    return jnp.concatenate([cos, cos, cos, cos], axis=1), jnp.concatenate([-sin, sin, -sin, sin], axis=1)


def _sink_cols(sinks):
    nb = TILE_Q // ATTN_BLOCK
    per_unit = sinks.reshape(4, 2).T
    return jnp.broadcast_to(per_unit[:, None, :, None, None], (2, nb, 4, ATTN_BLOCK, 128)).reshape(2, nb * 4 * ATTN_BLOCK, 128)


def _local_step(x, target, p, big_weights=None, emit_grads=None):
    if big_weights is None:
        big_weights = lambda l, part, after: {k: p[k][l] for name in PARTS[part] for k in (name, name + '_t')}
    if emit_grads is None:
        emit_grads = lambda l, part, grads: 0.0
    n = x.shape[0]
    cos_t, sin_t = _rope_tables(n)
    s5_layers, s5_vjp = _s5_params(p)
    row = lambda a: a[None, :]
    saved = []
    h = x
    for l in range(DEPTH):
        s = {'x0': h}
        bw = s['bw'] = dict(big_weights(l, 'mix', h))
        s['q'], k, v, s['u'], s['xr'], s['gate'] = _inproj_fwd(h, bw['w_in'], row(p['b_in'][l]), cos_t, sin_t)
        no_keys = jnp.zeros((ATTN_BLOCK, D_KV), MXU)
        s['k'], s['v'] = jnp.concatenate([no_keys, k], axis=0), jnp.concatenate([no_keys, v], axis=0)
        s['sinks'] = _sink_cols(p['attn_sinks'][l])
        s['ya'] = _attn_fwd(s['q'], s['k'], s['v'], s['sinks'])
        s['s5'] = s5_layers[l]
        s['lru'] = _lru_params(p, l)
        s['ys'], s['s5_cr'], s['s5_ci'] = _s5_fwd(s['u'], s['s5'])
        s['yl'], s['lru_c'] = _lru_fwd(s['xr'], s['gate'], s['lru'])
        s['mix'], s['r1'], s['x1'] = _mixout_fwd(s['ya'], s['ys'], s['yl'], h, row(p['mix_norm_g'][l]), bw['w_out'],
                                                 row(p['b_out'][l]), row(p['ln1_g'][l]), row(p['ln1_b'][l]))
        bw.update(big_weights(l, 'ffn', s['x1']))
        s['gp'], s['up'], s['r2'], h = _ffn_fwd(s['x1'], bw['ffn_w_gate'], bw['ffn_w_up'], p['ffn_conv_w'][l],
                                                row(p['ffn_conv_b'][l]), bw['ffn_w_down'], row(p['ln2_g'][l]), row(p['ln2_b'][l]))
        saved.append(s)
    loss, dh = _loss_head(h, target)
    placed = 0.0

    grads = {name: [None] * DEPTH for name in WEIGHTS}
    d_disc = [None] * DEPTH
    for l in reversed(range(DEPTH)):
        s = saved[l]
        g = {}
        (dr2, dgp, dup, g['ffn_w_down'], cw0, cw1, cw2, dcb, dg2, db2) = _ffn_bwd_down(
            dh, s['r2'], row(p['ln2_g'][l]) + placed, s['gp'], s['up'], p['ffn_conv_w'][l], row(p['ffn_conv_b'][l]),
            s['bw']['ffn_w_down_t'])
        g['ffn_conv_w'] = jnp.concatenate([cw0, cw1, cw2], axis=0)
        g['ffn_conv_b'], g['ln2_g'], g['ln2_b'] = dcb[0], dg2[0], db2[0]
        dr1, dg1, db1 = _ffn_bwd_dx(dr2, dgp, dup, s['r1'], row(p['ln1_g'][l]), s['bw']['ffn_w_gate_t'], s['bw']['ffn_w_up_t'])
        g['ffn_w_gate'], g['ffn_w_up'] = _ffn_bwd_dw(s['x1'], dgp, dup)
        g['ln1_g'], g['ln1_b'] = dg1[0], db1[0]
        placed = emit_grads(l, 'ffn', [g[name] for name in PARTS['ffn']])
        dya, dys, dyl, g['w_out'], dbo, dgm = _mixout_bwd(dr1, s['mix'], s['ya'], s['ys'], s['yl'],
                                                         row(p['mix_norm_g'][l]) + placed, s['bw']['w_out_t'])
        g['b_out'], g['mix_norm_g'] = dbo[0], dgm[0]

        du, dbr, dbi, dcr, dci, dar, dai, dd, g['s5_glu_w'], dgb = _s5_bwd(s['u'], dys, s['s5_cr'], s['s5_ci'], s['s5'])
        dxr, dgate, lw0, lw1, lw2, lw3, lcb, dwx, dwa, dbx, dba, dsp = _lru_bwd(s['xr'], s['gate'], dyl, s['lru_c'], s['lru'])
        g['lru_conv_w'] = jnp.concatenate([lw0, lw1, lw2, lw3], axis=0)
        g['lru_conv_b'], g['lru_bx'], g['lru_ba'] = lcb[0], dbx[0], dba[0]
        g['lru_wx'] = jnp.einsum('hihj->hij', dwx.reshape(4, 64, 4, 64))
        g['lru_wa'] = jnp.einsum('hihj->hij', dwa.reshape(4, 64, 4, 64))
        g['lru_a_param'] = -dsp[0] * jax.nn.sigmoid(-p['lru_a_param'][l])

        g['s5_c_re'] = jnp.einsum('gpgc->gcp', dcr.reshape(16, 64, 16, 16))
        g['s5_c_im'] = jnp.einsum('gpgc->gcp', dci.reshape(16, 64, 16, 16))
        g['s5_d'], g['s5_glu_b'] = dd[0], dgb[0]
        d_disc[l] = (dar.reshape(16, 64), dai.reshape(16, 64), jnp.einsum('gcgp->gpc', dbr.reshape(16, 16, 16, 64)),
                     jnp.einsum('gcgp->gpc', dbi.reshape(16, 16, 16, 64)))

        dq, dk, dv, dsink = _attn_bwd(s['q'], s['k'], s['v'], s['sinks'], s['ya'], dya)
        g['attn_sinks'] = dsink[:, 0]
        dh, g['w_in'], dbin = _inproj_bwd(dq, dk[ATTN_BLOCK:], dv[ATTN_BLOCK:], du, dxr, dgate, cos_t, sin_t, s['x0'], dr1,
                                          s['bw']['w_in_t'])
        g['b_in'] = dbin[0]
        placed = emit_grads(l, 'mix', [g[name] for name in PARTS['mix']])
        for name in g:
            grads[name][l] = g[name]
    big = dict(BIG)
    out = {name: grads[name] if name in big else jnp.stack(grads[name]) for name in WEIGHTS if name not in S5_DISC}
    out.update(zip(S5_DISC, s5_vjp(tuple(jnp.stack([d_disc[l][i] for l in range(DEPTH)]) for i in range(4)))))
    return loss, dh, out


def kernel(x, w_in, b_in, attn_sinks, s5_a_re, s5_a_im, s5_b_re, s5_b_im, s5_c_re, s5_c_im, s5_d, s5_log_dt, s5_glu_w, s5_glu_b, lru_conv_w, lru_conv_b, lru_wx, lru_bx, lru_wa, lru_ba, lru_a_param, mix_norm_g, w_out, b_out, ln1_g, ln1_b, ffn_w_gate, ffn_w_up, ffn_conv_w, ffn_conv_b, ffn_w_down, ln2_g, ln2_b, loss_target, m_w_in, m_b_in, m_attn_sinks, m_s5_a_re, m_s5_a_im, m_s5_b_re, m_s5_b_im, m_s5_c_re, m_s5_c_im, m_s5_d, m_s5_log_dt, m_s5_glu_w, m_s5_glu_b, m_lru_conv_w, m_lru_conv_b, m_lru_wx, m_lru_bx, m_lru_wa, m_lru_ba, m_lru_a_param, m_mix_norm_g, m_w_out, m_b_out, m_ln1_g, m_ln1_b, m_ffn_w_gate, m_ffn_w_up, m_ffn_conv_w, m_ffn_conv_b, m_ffn_w_down, m_ln2_g, m_ln2_b, v_w_in, v_b_in, v_attn_sinks, v_s5_a_re, v_s5_a_im, v_s5_b_re, v_s5_b_im, v_s5_c_re, v_s5_c_im, v_s5_d, v_s5_log_dt, v_s5_glu_w, v_s5_glu_b, v_lru_conv_w, v_lru_conv_b, v_lru_wx, v_lru_bx, v_lru_wa, v_lru_ba, v_lru_a_param, v_mix_norm_g, v_w_out, v_b_out, v_ln1_g, v_ln1_b, v_ffn_w_gate, v_ffn_w_up, v_ffn_conv_w, v_ffn_conv_b, v_ffn_w_down, v_ln2_g, v_ln2_b):
    given = dict(locals())
    whole = {name: given[name] for name in WEIGHTS if name not in dict(BIG)}
    small_whole, big_weights, placed = _gather_weights({name: given[name] for name in SHARDED})
    whole.update(small_whole)
    whole['b_in'] = whole['b_in'] + placed
    in_flight = {}

    def emit_grads(l, part, grads):
        in_flight[(l, part)] = _scatter_start(f"grads_start_{part}{l}", grads)
        return in_flight[(l, part)][-1][0, 0]

    loss, grad_x, grads = _local_step(x[0], loss_target[0], whole, big_weights, emit_grads)
    total = lax.psum(loss[0, 0], ("x", "y", "c"))
    return (total, grad_x[None], *_update(given, grads, in_flight, grad_x))


def _update(given, grads, in_flight, after):
    local_w = {name: given[name] for name in WEIGHTS}
    me = _me()
    outs = {}

    shard_rows = sum(_packed_rows(local_w[name].shape) for name in SMALL_SHARDED)
    rep_pad = -sum(_packed_rows(local_w[name].shape) for name in REPLICATED) % PACK_TILE

    def packed_rep(arrays):
        return jnp.concatenate([_pack_rows(arrays[name]) for name in REPLICATED] + [jnp.zeros((rep_pad, LANES), F32)], axis=0)

    rep_grads = packed_rep(grads)
    chunk = rep_grads.shape[0] // N_DEV
    small = jnp.concatenate(
        [_pack_rows(_to_blocks(grads[name], SHARD_AXIS[name]), lead=1) for name in SMALL_SHARDED]
        + [rep_grads.reshape(N_DEV, chunk, LANES)], axis=1)
    small_rows = shard_rows + chunk
    small_flight = _scatter_start("grads_start_small", [small.reshape(N_DEV * small_rows, LANES)])

    def summed(name, flight, rows_of, packed, after):
        send_sems, recv_sems, srcs, landing, _ = flight
        _, landing = _scatter_wait(name, send_sems, recv_sems, srcs, landing, after)
        return _reduce_adamw(landing, *packed, tile_rows=sum(rows_of) // 4)

    forms = {part: [(name, dict(BIG)[name], _row_form(local_w[name], dict(BIG)[name]).shape[1]) for name in names]
             for part, names in PARTS.items()}
    for (l, part), flight in in_flight.items():
        packed = [jnp.concatenate([_row_form(given[prefix + name], t)[l] for name, t, _ in forms[part]], axis=0)
                  for prefix in ('', 'm_', 'v_')]
        outs[(l, part)] = summed(f"grads_wait_{part}{l}", flight, [r for _, _, r in forms[part]], packed, after)
        after = outs[(l, part)][0]

    rep_state = [packed_rep({name: given[prefix + name] for name in REPLICATED}) for prefix in ('', 'm_', 'v_')]
    small_state = [jnp.concatenate([_pack_rows(given[prefix + name]) for name in SMALL_SHARDED]
                                   + [lax.dynamic_slice_in_dim(rep, me * chunk, chunk, axis=0)], axis=0)
                   for prefix, rep in zip(('', 'm_', 'v_'), rep_state)]
    small_outs = summed("grads_wait_small", small_flight, [small_rows], small_state, after)
    rep_sum = _all_gather("gather_small_grads", [], small_outs[0][shard_rows:])[0].reshape(N_DEV * chunk, LANES)
    rep_outs = _reduce_adamw(rep_sum[None], *rep_state, tile_rows=N_DEV * chunk // 4)

    def unpack(i):
        res = {}
        for part in PARTS:
            lo = 0
            for name, t, r in forms[part]:
                res[name] = _row_form(jnp.stack([outs[(l, part)][i][lo:lo + r] for l in range(DEPTH)]), t)
                lo += r
        for names, packed in ((SMALL_SHARDED, small_outs[i]), (REPLICATED, rep_outs[i])):
            lo = 0
            for name in names:
                shape = local_w[name].shape
                res[name] = packed[lo:lo + _packed_rows(shape)].reshape(-1)[:math.prod(shape)].reshape(shape)
                lo += _packed_rows(shape)
        return [res[name] for name in WEIGHTS]

    return (*unpack(0), *unpack(1), *unpack(2), *unpack(3))
```

```python
import functools
import math

import jax
import jax.numpy as jnp
from jax import lax
from jax.experimental import pallas as pl
from jax.experimental.pallas import tpu as pltpu

F32 = jnp.float32
MXU = jnp.bfloat16

N_DEV = 8
DEPTH = 4
D = 1024
D_ATTN, D_KV, D_S5, D_LRU = 512, 128, 256, 256
D_IN = 1536
D_FF = 2816
FF_CHUNK = 256
N_STATE = 1024
LANES = 1024
ALPHA = (2 * DEPTH) ** 0.25
LN_EPS = 1e-5
RMS_EPS = 1e-6
LRU_C = 8.0
ROPE_THETA = 10000.0
ADAM_LR, ADAM_B1, ADAM_B2, ADAM_EPS, ADAM_WD, ADAM_STEP = 0.001, 0.9, 0.999, 1e-08, 0.01, 10

TILE = 256
S5_PAIR = 4
N_SEG = 8
SEG = TILE // N_SEG
TILE_Q = 512
TILE_BIG = 512
TILE_WIDE = 512
ATTN_BLOCK = 128
PACK_TILE = 256
VMEM_MB = 56

WEIGHTS = ['w_in', 'b_in', 'attn_sinks', 's5_a_re', 's5_a_im', 's5_b_re', 's5_b_im', 's5_c_re', 's5_c_im', 's5_d', 's5_log_dt',
           's5_glu_w', 's5_glu_b', 'lru_conv_w', 'lru_conv_b', 'lru_wx', 'lru_bx', 'lru_wa', 'lru_ba', 'lru_a_param', 'mix_norm_g',
           'w_out', 'b_out', 'ln1_g', 'ln1_b', 'ffn_w_gate', 'ffn_w_up', 'ffn_conv_w', 'ffn_conv_b', 'ffn_w_down', 'ln2_g', 'ln2_b']
SHARD_AXIS = {'w_in': 2, 's5_glu_w': 1, 'lru_conv_w': 2, 'w_out': 1, 'ffn_w_gate': 2, 'ffn_w_up': 2, 'ffn_conv_w': 2,
              'ffn_w_down': 1}
SHARDED = [n for n in WEIGHTS if n in SHARD_AXIS]
REPLICATED = [n for n in WEIGHTS if n not in SHARD_AXIS]
BIG = [(n, SHARD_AXIS[n] == 2) for n in ('w_in', 'w_out', 'ffn_w_gate', 'ffn_w_up', 'ffn_w_down')]
SMALL_SHARDED = [n for n in SHARDED if n not in dict(BIG)]
PARTS = {'mix': ['w_in', 'w_out'], 'ffn': ['ffn_w_gate', 'ffn_w_up', 'ffn_w_down']}
GATHER_F32 = ('lru_conv_w', 'ffn_conv_w')


def _dot(a, b):
    return jnp.dot(a, b, preferred_element_type=F32)


def _dot_nt(a, b):
    return lax.dot_general(a, b, (((1,), (1,)), ((), ())), preferred_element_type=F32)


def _dot_tn(a, b):
    return lax.dot_general(a, b, (((0,), (0,)), ((), ())), preferred_element_type=F32)


def _mx(a):
    return a.astype(MXU)


_GELU_C = math.sqrt(2.0 / math.pi)


def _gelu(x):
    th = jnp.tanh(_GELU_C * (x + 0.044715 * x * x * x))
    return 0.5 * x * (1.0 + th)


def _gelu_grad(x):
    th = jnp.tanh(_GELU_C * (x + 0.044715 * x * x * x))
    return 0.5 * (1.0 + th) + 0.5 * x * (1.0 - th * th) * _GELU_C * (1.0 + 3.0 * 0.044715 * x * x)


def _sigmoid(x):
    return 0.5 * jnp.tanh(0.5 * x) + 0.5


def _ln_stats(r):
    mu = jnp.mean(r, axis=-1, keepdims=True)
    xc = r - mu
    var = jnp.mean(xc * xc, axis=-1, keepdims=True)
    rstd = lax.rsqrt(var + LN_EPS)
    return xc * rstd, rstd


def _ln_bwd(dy, g, xhat, rstd):
    dxh = dy * g
    return rstd * (dxh - jnp.mean(dxh, axis=-1, keepdims=True) - xhat * jnp.mean(dxh * xhat, axis=-1, keepdims=True))


def _rms(y):
    return lax.rsqrt(jnp.mean(y * y, axis=-1, keepdims=True) + RMS_EPS)


def _sum0(a):
    return jnp.sum(a, axis=0, keepdims=True)


def _row_iota(shape):
    return lax.broadcasted_iota(jnp.int32, shape, 0)


def _shift_down(ext, j, rows):
    return pltpu.roll(ext, j, 0)[8:8 + rows]


def _shift_up(ext, j, rows):
    return pltpu.roll(ext, ext.shape[0] - j, 0)[:rows]


def _swap_halves(t):
    w = t.shape[1]
    lane = lax.broadcasted_iota(jnp.int32, t.shape, 1)
    return jnp.where((lane & 32) == 0, pltpu.roll(t, w - 32, 1), pltpu.roll(t, 32, 1))


def _rope(t, cos, sin_signed):
    return t * cos + _swap_halves(t) * sin_signed


def _rope_t(d, cos, sin_signed):
    return d * cos + _swap_halves(d * sin_signed)


def _cmul_add(ar, ai, xr, xi, yr, yi):
    return ar * xr - ai * xi + yr, ar * xi + ai * xr + yi


def _seg_rows(k):
    return slice(N_SEG * k, N_SEG * (k + 1))


def _permute_rows(perm, x):
    hi = _mx(x)
    rest = x - hi.astype(F32)
    mid = _mx(rest)
    lo = _mx(rest - mid.astype(F32))
    return _dot(perm, hi) + _dot(perm, mid) + _dot(perm, lo)


def _cscan(sr, si, tab, cin_r, cin_i, reverse):
    sgn = -1.0 if reverse else 1.0
    pw_re, pw_im, dbl_re, dbl_im = tab['pw_re'], tab['pw_im'], tab['dbl_re'], tab['dbl_im']
    ar, ai = pw_re[0:1, :], sgn * pw_im[0:1, :]
    shape = (N_SEG, sr.shape[1])
    hr = hi = jnp.zeros(shape, F32)
    for k in (range(SEG - 1, -1, -1) if reverse else range(SEG)):
        hr, hi = _cmul_add(ar, ai, hr, hi, sr[_seg_rows(k), :], si[_seg_rows(k), :])
        sr[_seg_rows(k), :] = hr
        si[_seg_rows(k), :] = hi
    sub = _row_iota(shape)

    def shifted(v, d):
        if reverse:
            return jnp.where(sub < N_SEG - d, pltpu.roll(v, N_SEG - d, 0), 0.0)
        return jnp.where(sub >= d, pltpu.roll(v, d, 0), 0.0)

    fr, fi = hr, hi
    for j, d in enumerate((1, 2, 4)):
        fr, fi = _cmul_add(dbl_re[j:j + 1, :], sgn * dbl_im[j:j + 1, :], shifted(fr, d), shifted(fi, d), fr, fi)
    seg_re, seg_im = (tab['segr_re'], tab['segr_im']) if reverse else (tab['seg_re'], tab['seg_im'])
    cr, ci = _cmul_add(seg_re[...], sgn * seg_im[...], cin_r, cin_i, shifted(fr, 1), shifted(fi, 1))
    nr, ni = _cmul_add(dbl_re[0:1, :], sgn * dbl_im[0:1, :], cr, ci, hr, hi)
    for k in range(SEG):
        j = SEG - 1 - k if reverse else k
        xr, xi = _cmul_add(pw_re[j:j + 1, :], sgn * pw_im[j:j + 1, :], cr, ci, sr[_seg_rows(k), :], si[_seg_rows(k), :])
        sr[_seg_rows(k), :] = xr
        si[_seg_rows(k), :] = xi
    edge = slice(0, 1) if reverse else slice(N_SEG - 1, N_SEG)
    return nr[edge], ni[edge]


def _rscan(a, b, reverse):
    rows = a.shape[0]
    row = _row_iota(a.shape)
    s = 1
    while s < rows:
        if reverse:
            keep = row < rows - s
            sa = jnp.where(keep, pltpu.roll(a, rows - s, 0), 1.0)
            sb = jnp.where(keep, pltpu.roll(b, rows - s, 0), 0.0)
        else:
            keep = row >= s
            sa = jnp.where(keep, pltpu.roll(a, s, 0), 1.0)
            sb = jnp.where(keep, pltpu.roll(b, s, 0), 0.0)
        b = b + a * sb
        a = a * sa
        s *= 2
    return a, b


def _whole():
    return pl.BlockSpec(memory_space=pltpu.VMEM)


def _rows_spec(rows, cols, n_tiles, reverse=False):
    if reverse:
        return pl.BlockSpec((rows, cols), lambda i: (n_tiles - 1 - i, 0))
    return pl.BlockSpec((rows, cols), lambda i: (i, 0))


def _halo_spec(cols, tile_rows, n_tiles, reverse=False):
    per = tile_rows // 8
    if reverse:
        return pl.BlockSpec((8, cols), lambda i: (jnp.maximum((n_tiles - 1 - i) * per - 1, 0), 0))
    return pl.BlockSpec((8, cols), lambda i: (jnp.maximum(i * per - 1, 0), 0))


def _call(body, name, n_tiles, in_specs, out_specs, out_shape, scratch=()):
    return pl.pallas_call(
        body, name=name, grid=(n_tiles,), in_specs=in_specs, out_specs=out_specs, out_shape=out_shape,
        scratch_shapes=list(scratch),
        compiler_params=pltpu.CompilerParams(dimension_semantics=("arbitrary",), vmem_limit_bytes=VMEM_MB << 20))


def _sds(shape, dtype=F32):
    return jax.ShapeDtypeStruct(shape, dtype)


def _inproj_fwd(x, w, b, cos_t, sin_t):
    n = x.shape[0]
    nt = n // TILE_WIDE

    def body(x_ref, w_ref, b_ref, c_ref, s_ref, q_ref, k_ref, v_ref, u_ref, xr_ref, g_ref):
        p = _dot(_mx(x_ref[...]), w_ref[...]) + b_ref[...]
        cos, sin = c_ref[...], s_ref[...]
        q_ref[...] = _mx(_rope(p[:, :D_ATTN], jnp.tile(cos, (1, 4)), jnp.tile(sin, (1, 4))))
        k_ref[...] = _mx(_rope(p[:, 512:640], cos, sin))
        v_ref[...] = _mx(p[:, 640:768])
        u_ref[...] = p[:, 768:1024]
        xr_ref[...] = p[:, 1024:1280]
        g_ref[...] = p[:, 1280:1536]

    r = functools.partial(_rows_spec, n_tiles=nt)
    return _call(
        body, "inproj_fwd", nt,
        [r(TILE_WIDE, D), _whole(), _whole(), r(TILE_WIDE, 128), r(TILE_WIDE, 128)],
        [r(TILE_WIDE, D_ATTN), r(TILE_WIDE, D_KV), r(TILE_WIDE, D_KV), r(TILE_WIDE, D_S5), r(TILE_WIDE, D_LRU), r(TILE_WIDE, D_LRU)],
        [_sds((n, D_ATTN), MXU), _sds((n, D_KV), MXU), _sds((n, D_KV), MXU), _sds((n, D_S5)), _sds((n, D_LRU)), _sds((n, D_LRU))],
    )(x, w, b, cos_t, sin_t)


def _inproj_bwd(dq, dk, dv, du, dxr, dgate, cos_t, sin_t, x0, dr1, w_t):
    n = x0.shape[0]
    nt = n // TILE_WIDE

    def body(dq_ref, dk_ref, dv_ref, du_ref, dxr_ref, dg_ref, c_ref, s_ref, x_ref, dr_ref, w_ref, dx_ref, dw_ref, db_ref):
        @pl.when(pl.program_id(0) == 0)
        def _():
            dw_ref[...] = jnp.zeros_like(dw_ref)
            db_ref[...] = jnp.zeros_like(db_ref)

        cos, sin = c_ref[...], s_ref[...]
        dtq = _rope_t(dq_ref[...], jnp.tile(cos, (1, 4)), jnp.tile(sin, (1, 4)))
        dtk = _rope_t(dk_ref[...], cos, sin)
        dp = jnp.concatenate([dtq, dtk, dv_ref[...], du_ref[...], dxr_ref[...], dg_ref[...]], axis=1)
        db_ref[...] += _sum0(dp)
        dpb = _mx(dp)
        dw_ref[...] += _dot_tn(dpb, _mx(x_ref[...]))
        dx_ref[...] = ALPHA * dr_ref[...] + _dot(dpb, w_ref[...])

    r = functools.partial(_rows_spec, n_tiles=nt)
    return _call(
        body, "inproj_bwd", nt,
        [r(TILE_WIDE, D_ATTN), r(TILE_WIDE, D_KV), r(TILE_WIDE, D_KV), r(TILE_WIDE, D_S5), r(TILE_WIDE, D_LRU), r(TILE_WIDE, D_LRU),
         r(TILE_WIDE, 128), r(TILE_WIDE, 128), r(TILE_WIDE, D), r(TILE_WIDE, D), _whole()],
        [r(TILE_WIDE, D), _whole(), _whole()],
        [_sds((n, D)), _sds((D_IN, D)), _sds((1, D_IN))],
    )(dq, dk, dv, du, dxr, dgate, cos_t, sin_t, x0, dr1, w_t)


def _kv_variants(t, lo):
    tr = pltpu.roll(t, 64, 1)
    out = []
    for j in range(2):
        first = jnp.where(lo, t if j == 0 else tr, 0.0)
        second = jnp.where(lo, 0.0, tr if j == 0 else t)
        out.append(_mx(jnp.concatenate([first, second], axis=0)))
    return out


def _kv_collect(x0, x1, lo):
    a = x0[:256] + pltpu.roll(x0[256:], 64, 1)
    b = pltpu.roll(x1[:256], 64, 1) + x1[256:]
    return jnp.where(lo, a, b)


def _row_sums(x):
    ones = jnp.ones((128, 128), MXU)
    hi = _mx(x)
    lo = _mx(x - hi.astype(F32))
    return _dot(hi, ones) + _dot(lo, ones)


def _attn_probs(s, sink_ref):
    out = []
    for hp in range(2):
        sh = s[:, hp * 128:(hp + 1) * 128]
        sink = sink_ref[hp]
        m = jnp.maximum(jnp.broadcast_to(jnp.max(sh, axis=1, keepdims=True), sh.shape), sink)
        p = jnp.exp(sh - m)
        es = jnp.exp(sink - m)
        inv = 1.0 / (_row_sums(p) + es)
        out.append((p * inv, es * inv))
    return out


def _band_merge(x, tri, no_previous=None):
    bands = []
    for hp in range(2):
        prev, own = x[:, hp * 256:hp * 256 + 128], x[:, hp * 256 + 128:hp * 256 + 256]
        if no_previous is not None:
            prev = jnp.where(no_previous, -jnp.inf, prev)
        bands.append(jnp.where(tri, own, prev))
    return jnp.concatenate(bands, axis=1)


def _band_split(y, tri):
    parts = []
    for hp in range(2):
        band = y[:, hp * 128:(hp + 1) * 128]
        parts += [jnp.where(tri, 0.0, band), jnp.where(tri, band, 0.0)]
    return jnp.concatenate(parts, axis=1)


def _tri():
    shape = (ATTN_BLOCK, ATTN_BLOCK)
    return lax.broadcasted_iota(jnp.int32, shape, 0) >= lax.broadcasted_iota(jnp.int32, shape, 1)


def _attn_scores(q_ref, k_ref, v_ref, nb, tri):
    lo = lax.broadcasted_iota(jnp.int32, (256, 128), 1) < 64
    kcats, vcats, kstarts, parts = [], [], [], []
    for b in range(nb):
        block = pl.program_id(0) * nb + b
        kstart = pl.multiple_of(block * ATTN_BLOCK, ATTN_BLOCK)
        kcat = _kv_variants(k_ref[pl.ds(kstart, 256), :].astype(F32), lo)
        kcats.append(kcat)
        vcats.append(_kv_variants(v_ref[pl.ds(kstart, 256), :].astype(F32), lo))
        kstarts.append(kstart)
        for j in range(2):
            s = _dot_nt(_kv_group(q_ref, b, j), kcat[j]) * 0.125
            parts += [_band_merge(s[:ATTN_BLOCK], tri, block == 0), _band_merge(s[ATTN_BLOCK:], tri, block == 0)]
    return jnp.concatenate(parts, axis=0), kcats, vcats, kstarts


def _kv_group(a, b, j):
    rows = slice(b * ATTN_BLOCK, (b + 1) * ATTN_BLOCK)
    return jnp.concatenate([a[rows, 2 * j * 128:(2 * j + 1) * 128], a[rows, (2 * j + 1) * 128:(2 * j + 2) * 128]], axis=0)


def _put_kv_group(ref, b, j, x):
    rows = slice(b * ATTN_BLOCK, (b + 1) * ATTN_BLOCK)
    ref[rows, 2 * j * 128:(2 * j + 1) * 128] = x[:ATTN_BLOCK]
    ref[rows, (2 * j + 1) * 128:(2 * j + 2) * 128] = x[ATTN_BLOCK:]


def _band_split_group(y, unit, tri):
    return _mx(jnp.concatenate([_band_split(y[unit:unit + ATTN_BLOCK], tri),
                                _band_split(y[unit + ATTN_BLOCK:unit + 2 * ATTN_BLOCK], tri)], axis=0))


def _attn_fwd(q, k, v, sink_cols):
    n = q.shape[0]
    nt = n // TILE_Q
    nb = TILE_Q // ATTN_BLOCK

    def body(q_ref, k_ref, v_ref, s_ref, o_ref):
        tri = _tri()
        s, _, vcats, _ = _attn_scores(q_ref, k_ref, v_ref, nb, tri)
        (p0, _), (p1, _) = _attn_probs(s, s_ref)
        p = jnp.concatenate([p0, p1], axis=1)
        for b in range(nb):
            for j in range(2):
                _put_kv_group(o_ref, b, j, _dot(_band_split_group(p, (b * 4 + 2 * j) * ATTN_BLOCK, tri), vcats[b][j]))

    return _call(
        body, "attn_fwd", nt,
        [_rows_spec(TILE_Q, D_ATTN, nt), _whole(), _whole(), _whole()],
        _rows_spec(TILE_Q, D_ATTN, nt), _sds((n, D_ATTN)),
    )(q, k, v, sink_cols)


def _attn_bwd(q, k, v, sink_cols, o, do):
    n = q.shape[0]
    nt = n // TILE_Q
    nb = TILE_Q // ATTN_BLOCK

    def body(q_ref, k_ref, v_ref, s_ref, o_ref, do_ref, dq_ref, dk_ref, dv_ref, ds_ref):
        @pl.when(pl.program_id(0) == 0)
        def _():
            dk_ref[...] = jnp.zeros_like(dk_ref)
            dv_ref[...] = jnp.zeros_like(dv_ref)
            ds_ref[...] = jnp.zeros_like(ds_ref)

        lo = lax.broadcasted_iota(jnp.int32, (256, 128), 1) < 64
        tri = _tri()
        s, kcats, vcats, kstarts = _attn_scores(q_ref, k_ref, v_ref, nb, tri)
        probs = _attn_probs(s, s_ref)
        do = do_ref[...]
        dob = _mx(do)
        od = do * o_ref[...]
        lo_q = (lax.broadcasted_iota(jnp.int32, od.shape, 1) & 64) == 0
        od_head = (jnp.where(lo_q, od, 0.0), jnp.where(lo_q, 0.0, od))
        units = [(b, i) for b in range(nb) for i in range(4)]

        def tile_part(a, b, i):
            return a[b * ATTN_BLOCK:(b + 1) * ATTN_BLOCK, i * 128:(i + 1) * 128]

        dp = []
        for b in range(nb):
            for j in range(2):
                x = _dot_nt(_kv_group(dob, b, j), vcats[b][j])
                dp += [_band_merge(x[:ATTN_BLOCK], tri), _band_merge(x[ATTN_BLOCK:], tri)]
        dp = jnp.concatenate(dp, axis=0)
        ds = []
        for hp in range(2):
            p, p_sink = probs[hp]
            delta = _row_sums(jnp.concatenate([tile_part(od_head[hp], b, i) for b, i in units], axis=0))
            ds.append(p * (dp[:, hp * 128:(hp + 1) * 128] - delta) * 0.125)
            t = p_sink * delta
            for i in range(4):
                ds_ref[2 * i + hp:2 * i + hp + 1, :] -= sum(
                    _sum0(t[(b * 4 + i) * ATTN_BLOCK:(b * 4 + i + 1) * ATTN_BLOCK]) for b in range(nb))
        ds = jnp.concatenate(ds, axis=1)
        p = jnp.concatenate([probs[0][0], probs[1][0]], axis=1)
        for b in range(nb):
            dkc, dvc = [], []
            for j in range(2):
                unit = (b * 4 + 2 * j) * ATTN_BLOCK
                dsb = _band_split_group(ds, unit, tri)
                _put_kv_group(dq_ref, b, j, _dot(dsb, kcats[b][j]))
                dkc.append(_dot_tn(dsb, _kv_group(q_ref, b, j)))
                dvc.append(_dot_tn(_band_split_group(p, unit, tri), _kv_group(dob, b, j)))
            dk_ref[pl.ds(kstarts[b], 256), :] += _kv_collect(dkc[0], dkc[1], lo)
            dv_ref[pl.ds(kstarts[b], 256), :] += _kv_collect(dvc[0], dvc[1], lo)

    r = _rows_spec(TILE_Q, D_ATTN, nt)
    return _call(
        body, "attn_bwd", nt,
        [r, _whole(), _whole(), _whole(), r, r],
        [r, _whole(), _whole(), _whole()],
        [_sds((n, D_ATTN)), _sds((n + ATTN_BLOCK, D_KV)), _sds((n + ATTN_BLOCK, D_KV)), _sds((8, 128))],
    )(q, k, v, sink_cols, o, do)


S5_DISC = ('s5_a_re', 's5_a_im', 's5_log_dt', 's5_b_re', 's5_b_im')
S5_TABLES = ('pw_re', 'pw_im', 'dbl_re', 'dbl_im', 'seg_re', 'seg_im', 'segr_re', 'segr_im')
S5_WEIGHTS = ('b_re', 'b_im', 'c_re', 'c_im', 'd', 'glu_w', 'glu_b', 'perm', 'perm_t')


def _s5_states(u, carry_r, carry_i, b_re, b_im, tab, hr_s, hi_s):
    ub = _mx(u)
    hr_s[...] = _dot(ub, b_re[...])
    hi_s[...] = _dot(ub, b_im[...])
    return ub, _cscan(hr_s, hi_s, tab, carry_r, carry_i, reverse=False)


def _s5_fwd(u, prm):
    n = u.shape[0]
    nt = n // (S5_PAIR * TILE)

    def body(u_ref, *refs):
        tab = dict(zip(S5_TABLES, refs[:8]))
        b_re, b_im, c_re, c_im, d_ref, gw_ref, gb_ref, perm, perm_t = refs[8:17]
        y_ref, cr_out, ci_out, cr_s, ci_s = refs[17:22]
        states = [refs[22 + 2 * j:24 + 2 * j] for j in range(S5_PAIR)]
        rows = [slice(j * TILE, (j + 1) * TILE) for j in range(S5_PAIR)]

        @pl.when(pl.program_id(0) == 0)
        def _():
            cr_s[...] = jnp.zeros_like(cr_s)
            ci_s[...] = jnp.zeros_like(ci_s)

        us = []
        for j in range(S5_PAIR):
            us.append(_permute_rows(perm[...], u_ref[rows[j], :]))
            ub = _mx(us[j])
            states[j][0][...] = _dot(ub, b_re[...])
            states[j][1][...] = _dot(ub, b_im[...])
        for j in range(S5_PAIR):
            cr, ci = cr_s[...], ci_s[...]
            cr_out[8 * j:8 * j + 8, :] = jnp.broadcast_to(cr, (8, N_STATE))
            ci_out[8 * j:8 * j + 8, :] = jnp.broadcast_to(ci, (8, N_STATE))
            cr_s[...], ci_s[...] = _cscan(*states[j], tab, cr, ci, reverse=False)
        for j in range(S5_PAIR):
            hr_s, hi_s = states[j]
            y = _dot(_mx(hr_s[...]), c_re[...]) - _dot(_mx(hi_s[...]), c_im[...]) + d_ref[...] * us[j]
            z = _gelu(y)
            y_ref[rows[j], :] = _permute_rows(perm_t[...], z * _sigmoid(_dot(_mx(z), gw_ref[...]) + gb_ref[...]))

    r = functools.partial(_rows_spec, n_tiles=nt)
    return _call(
        body, "s5_fwd", nt,
        [r(S5_PAIR * TILE, D_S5)] + [_whole()] * 17,
        [r(S5_PAIR * TILE, D_S5), r(S5_PAIR * 8, N_STATE), r(S5_PAIR * 8, N_STATE)],
        [_sds((n, D_S5)), _sds((n // TILE * 8, N_STATE)), _sds((n // TILE * 8, N_STATE))],
        scratch=[pltpu.VMEM((1, N_STATE), F32)] * 2 + [pltpu.VMEM((TILE, N_STATE), F32)] * (2 * S5_PAIR),
    )(u, *[prm[k] for k in S5_TABLES + S5_WEIGHTS])


def _s5_bwd(u, dys, carry_re, carry_im, prm):
    n = u.shape[0]
    nt = n // (S5_PAIR * TILE)

    def body(u_ref, dy_ref, cin_r, cin_i, *refs):
        tab = dict(zip(S5_TABLES, refs[:8]))
        b_re, b_im, c_re, c_im, d_ref, gw_ref, gb_ref, perm, perm_t = refs[8:17]
        du_ref, dbr_ref, dbi_ref, dcr_ref, dci_ref, dar_ref, dai_ref, dd_ref, dgw_ref, dgb_ref = refs[17:27]
        gr_s, gi_s = refs[27:29]
        scratch = [refs[29 + 4 * j:33 + 4 * j] for j in range(S5_PAIR)]
        later_first = list(reversed(range(S5_PAIR)))
        rows = [slice(j * TILE, (j + 1) * TILE) for j in range(S5_PAIR)]

        @pl.when(pl.program_id(0) == 0)
        def _():
            for ref in (dbr_ref, dbi_ref, dcr_ref, dci_ref, dar_ref, dai_ref, dd_ref, dgw_ref, dgb_ref, gr_s, gi_s):
                ref[...] = jnp.zeros_like(ref)

        us, ubs, carries, dy_of = {}, {}, {}, {}
        for j in later_first:
            us[j] = _permute_rows(perm[...], u_ref[rows[j], :])
            carries[j] = (cin_r[8 * j:8 * j + 1, :], cin_i[8 * j:8 * j + 1, :])
            ubs[j], _ = _s5_states(us[j], *carries[j], b_re, b_im, tab, *scratch[j][:2])
        for j in later_first:
            u = us[j]
            hr_s, hi_s, gr_t, gi_t = scratch[j]
            hrb, hib = _mx(hr_s[...]), _mx(hi_s[...])
            y = _dot(hrb, c_re[...]) - _dot(hib, c_im[...]) + d_ref[...] * u
            z = _gelu(y)
            zb = _mx(z)
            sg = _sigmoid(_dot(zb, gw_ref[...]) + gb_ref[...])
            dout = _permute_rows(perm[...], dy_ref[rows[j], :])
            dpre = dout * z * sg * (1.0 - sg)
            dgb_ref[...] += _sum0(dpre)
            dpb = _mx(dpre)
            dgw_ref[...] += _dot_tn(zb, dpb)
            dy = (dout * sg + _dot_nt(dpb, gw_ref[...])) * _gelu_grad(y)
            dd_ref[...] += _sum0(dy * u)
            dyb = _mx(dy)
            dcr_ref[...] += _dot_tn(hrb, dyb)
            dci_ref[...] -= _dot_tn(hib, dyb)
            gr_t[...] = _dot_nt(dyb, c_re[...])
            gi_t[...] = -_dot_nt(dyb, c_im[...])
            dy_of[j] = dy
        for j in later_first:
            gr_s[...], gi_s[...] = _cscan(*scratch[j][2:], tab, gr_s[...], gi_s[...], reverse=True)
        for j in later_first:
            hr_s, hi_s, gr_t, gi_t = scratch[j]
            cr, ci = carries[j]
            sub = _row_iota((N_SEG, N_STATE))
            acc_r = acc_i = jnp.zeros((N_SEG, N_STATE), F32)
            for k in range(SEG):
                if k == 0:
                    hpr = jnp.where(sub >= 1, pltpu.roll(hr_s[_seg_rows(SEG - 1), :], 1, 0), cr)
                    hpi = jnp.where(sub >= 1, pltpu.roll(hi_s[_seg_rows(SEG - 1), :], 1, 0), ci)
                else:
                    hpr, hpi = hr_s[_seg_rows(k - 1), :], hi_s[_seg_rows(k - 1), :]
                gr, gi = gr_t[_seg_rows(k), :], gi_t[_seg_rows(k), :]
                acc_r = acc_r + gr * hpr + gi * hpi
                acc_i = acc_i + gi * hpr - gr * hpi
            dar_ref[...] += _sum0(acc_r)
            dai_ref[...] += _sum0(acc_i)
            grb, gib = _mx(gr_t[...]), _mx(gi_t[...])
            dbr_ref[...] += _dot_tn(ubs[j], grb)
            dbi_ref[...] += _dot_tn(ubs[j], gib)
            du_ref[rows[j], :] = _permute_rows(perm_t[...], dy_of[j] * d_ref[...] + _dot_nt(grb, b_re[...]) + _dot_nt(gib, b_im[...]))

    r = functools.partial(_rows_spec, n_tiles=nt, reverse=True)
    return _call(
        body, "s5_bwd", nt,
        [r(S5_PAIR * TILE, D_S5), r(S5_PAIR * TILE, D_S5), r(S5_PAIR * 8, N_STATE), r(S5_PAIR * 8, N_STATE)] + [_whole()] * 17,
        [r(S5_PAIR * TILE, D_S5)] + [_whole()] * 9,
        [_sds((n, D_S5)), _sds((D_S5, N_STATE)), _sds((D_S5, N_STATE)), _sds((N_STATE, D_S5)), _sds((N_STATE, D_S5)),
         _sds((1, N_STATE)), _sds((1, N_STATE)), _sds((1, D_S5)), _sds((D_S5, D_S5)), _sds((1, D_S5))],
        scratch=[pltpu.VMEM((1, N_STATE), F32)] * 2 + [pltpu.VMEM((TILE, N_STATE), F32)] * (4 * S5_PAIR),
    )(u, dys, carry_re, carry_im, *[prm[k] for k in S5_TABLES + S5_WEIGHTS])


def _lru_gates(xr, halo, tile_index, cw_ref, cb_ref, wx_ref, wa_ref, bx_ref, ba_ref, sp_ref):
    ext = jnp.concatenate([halo, xr], axis=0)
    sh = [xr] + [_shift_down(ext, j, TILE) for j in (1, 2, 3)]
    xc = cb_ref[...] + cw_ref[3:4, :] * sh[0] + cw_ref[2:3, :] * sh[1] + cw_ref[1:2, :] * sh[2] + cw_ref[0:1, :] * sh[3]
    xb = _mx(xc)
    gx = _sigmoid(_dot(xb, wx_ref[...]) + bx_ref[...])
    ga = _sigmoid(_dot(xb, wa_ref[...]) + ba_ref[...])
    la = -LRU_C * ga * sp_ref[...]
    a = jnp.exp(la)
    start = (tile_index * TILE + _row_iota(xr.shape)) == 0
    mult = jnp.where(start, 1.0, jnp.sqrt(-jnp.tanh(la) * (a * a + 1.0)))
    return sh, xc, xb, gx, ga, a, mult, start


def _lru_fwd(xr, gate, prm):
    n = xr.shape[0]
    nt = n // TILE

    def body(x_ref, g_ref, cw_ref, cb_ref, wx_ref, wa_ref, bx_ref, ba_ref, sp_ref, y_ref, c_out, halo_s, c_s):
        first_tile = pl.program_id(0) == 0

        @pl.when(first_tile)
        def _():
            halo_s[...] = jnp.zeros_like(halo_s)
            c_s[...] = jnp.zeros_like(c_s)

        xr = x_ref[...]
        _, xc, _, gx, _, a, mult, _ = _lru_gates(xr, halo_s[...], pl.program_id(0), cw_ref, cb_ref, wx_ref, wa_ref, bx_ref, ba_ref,
                                                 sp_ref)
        halo_s[...] = xr[TILE - 8:]
        acum, h = _rscan(a, mult * gx * xc, reverse=False)
        c = c_s[...]
        c_out[...] = jnp.broadcast_to(c, (8, D_LRU))
        h = h + acum * c
        c_s[...] = h[TILE - 1:TILE]
        y_ref[...] = h * _gelu(g_ref[...])

    r = functools.partial(_rows_spec, n_tiles=nt)
    return _call(
        body, "lru_fwd", nt,
        [r(TILE, D_LRU), r(TILE, D_LRU)] + [_whole()] * 7,
        [r(TILE, D_LRU), r(8, D_LRU)],
        [_sds((n, D_LRU)), _sds((nt * 8, D_LRU))],
        scratch=[pltpu.VMEM((8, D_LRU), F32), pltpu.VMEM((1, D_LRU), F32)],
    )(xr, gate, prm['conv_w'], prm['conv_b'], prm['wx'], prm['wa'], prm['bx'], prm['ba'], prm['sp'])


def _lru_bwd(xr, gate, dyl, carry, prm):
    n = xr.shape[0]
    nt = n // TILE

    def body(x_ref, xh_ref, g_ref, dy_ref, cin_ref, cw_ref, cb_ref, wx_ref, wa_ref, bx_ref, ba_ref, sp_ref,
             dx_ref, dg_ref, dcw0, dcw1, dcw2, dcw3, dcb_ref, dwx_ref, dwa_ref, dbx_ref, dba_ref, dsp_ref, an_s, gn_s, dn_s):
        first_tile = pl.program_id(0) == nt - 1

        @pl.when(pl.program_id(0) == 0)
        def _():
            for ref in (dcw0, dcw1, dcw2, dcw3, dcb_ref, dwx_ref, dwa_ref, dbx_ref, dba_ref, dsp_ref, gn_s, dn_s):
                ref[...] = jnp.zeros_like(ref)
            an_s[...] = jnp.ones_like(an_s)

        xr = x_ref[...]
        halo = jnp.where(first_tile, 0.0, xh_ref[...])
        sh, xc, xb, gx, ga, a, mult, start = _lru_gates(xr, halo, nt - 1 - pl.program_id(0), cw_ref, cb_ref, wx_ref, wa_ref, bx_ref,
                                                        ba_ref, sp_ref)
        acum, h = _rscan(a, mult * gx * xc, reverse=False)
        cin = cin_ref[0:1, :]
        h = h + acum * cin
        gate = g_ref[...]
        dyl = dy_ref[...]
        dg_ref[...] = dyl * h * _gelu_grad(gate)
        row = _row_iota(xr.shape)
        alpha = jnp.where(row < TILE - 1, pltpu.roll(a, TILE - 1, 0), an_s[...])
        racc, g = _rscan(alpha, dyl * _gelu(gate), reverse=True)
        g = g + racc * gn_s[...]
        an_s[...] = a[0:1]
        gn_s[...] = g[0:1]
        hprev = jnp.where(row == 0, cin, pltpu.roll(h, 1, 0))
        da = g * hprev
        dmult = jnp.where(start, 0.0, g * gx * xc)
        dla = da * a - dmult * a * a / mult
        dsp_ref[...] += _sum0(-LRU_C * ga * dla)
        dpa = (-LRU_C * sp_ref[...] * dla) * ga * (1.0 - ga)
        dpx = (g * mult * xc) * gx * (1.0 - gx)
        dba_ref[...] += _sum0(dpa)
        dbx_ref[...] += _sum0(dpx)
        dpab, dpxb = _mx(dpa), _mx(dpx)
        dwa_ref[...] += _dot_tn(xb, dpab)
        dwx_ref[...] += _dot_tn(xb, dpxb)
        dxc = g * mult * gx + _dot_nt(dpab, wa_ref[...]) + _dot_nt(dpxb, wx_ref[...])
        dcb_ref[...] += _sum0(dxc)
        dcw3[...] += _sum0(dxc * sh[0])
        dcw2[...] += _sum0(dxc * sh[1])
        dcw1[...] += _sum0(dxc * sh[2])
        dcw0[...] += _sum0(dxc * sh[3])
        ext = jnp.concatenate([dxc, dn_s[...]], axis=0)
        dx_ref[...] = (cw_ref[3:4, :] * dxc + cw_ref[2:3, :] * _shift_up(ext, 1, TILE) + cw_ref[1:2, :] * _shift_up(ext, 2, TILE)
                       + cw_ref[0:1, :] * _shift_up(ext, 3, TILE))
        dn_s[...] = dxc[:8]

    r = functools.partial(_rows_spec, n_tiles=nt, reverse=True)
    vec = _sds((1, D_LRU))
    return _call(
        body, "lru_bwd", nt,
        [r(TILE, D_LRU), _halo_spec(D_LRU, TILE, nt, reverse=True), r(TILE, D_LRU), r(TILE, D_LRU), r(8, D_LRU)] + [_whole()] * 7,
        [r(TILE, D_LRU), r(TILE, D_LRU)] + [_whole()] * 10,
        [_sds((n, D_LRU)), _sds((n, D_LRU)), vec, vec, vec, vec, vec, _sds((D_LRU, D_LRU)), _sds((D_LRU, D_LRU)), vec, vec, vec],
        scratch=[pltpu.VMEM((1, D_LRU), F32), pltpu.VMEM((1, D_LRU), F32), pltpu.VMEM((8, D_LRU), F32)],
    )(xr, xr, gate, dyl, carry, prm['conv_w'], prm['conv_b'], prm['wx'], prm['wa'], prm['bx'], prm['ba'], prm['sp'])


def _normed_parts(ya, ys, yl):
    return jnp.concatenate([ya * _rms(ya), ys * _rms(ys), yl * _rms(yl)], axis=1)


def _mixout_fwd(ya, ys, yl, x0, g_mix, w_out, b_out, g1, b1):
    n = x0.shape[0]
    nt = n // TILE_WIDE

    def body(ya_ref, ys_ref, yl_ref, x_ref, gm_ref, w_ref, b_ref, g_ref, be_ref, mix_ref, r_ref, x1_ref):
        mixb = _mx(_normed_parts(ya_ref[...], ys_ref[...], yl_ref[...]) * gm_ref[...])
        mix_ref[...] = mixb
        r1 = ALPHA * x_ref[...] + _dot(mixb, w_ref[...]) + b_ref[...]
        r_ref[...] = r1
        xhat, _ = _ln_stats(r1)
        x1_ref[...] = xhat * g_ref[...] + be_ref[...]

    r = functools.partial(_rows_spec, n_tiles=nt)
    return _call(
        body, "mixout_fwd", nt,
        [r(TILE_WIDE, D_ATTN), r(TILE_WIDE, D_S5), r(TILE_WIDE, D_LRU), r(TILE_WIDE, D)] + [_whole()] * 5,
        [r(TILE_WIDE, D), r(TILE_WIDE, D), r(TILE_WIDE, D)],
        [_sds((n, D), MXU), _sds((n, D)), _sds((n, D))],
    )(ya, ys, yl, x0, g_mix, w_out, b_out, g1, b1)


def _mixout_bwd(dr1, mix, ya, ys, yl, g_mix, w_out):
    n = dr1.shape[0]
    nt = n // TILE_WIDE

    def body(dr_ref, mix_ref, ya_ref, ys_ref, yl_ref, gm_ref, w_ref, dya_ref, dys_ref, dyl_ref, dw_ref, db_ref, dgm_ref):
        @pl.when(pl.program_id(0) == 0)
        def _():
            for ref in (dw_ref, db_ref, dgm_ref):
                ref[...] = jnp.zeros_like(ref)

        dr = dr_ref[...]
        db_ref[...] += _sum0(dr)
        drb = _mx(dr)
        dw_ref[...] += _dot_tn(mix_ref[...], drb)
        dmix = _dot(drb, w_ref[...])
        parts = (ya_ref[...], ys_ref[...], yl_ref[...])
        dgm_ref[...] += _sum0(dmix * _normed_parts(*parts))
        dn = dmix * gm_ref[...]
        lo = 0
        for y, out in zip(parts, (dya_ref, dys_ref, dyl_ref)):
            w = y.shape[1]
            rs = _rms(y)
            nrm = y * rs
            dnp = dn[:, lo:lo + w]
            out[...] = rs * (dnp - nrm * jnp.mean(dnp * nrm, axis=-1, keepdims=True))
            lo += w

    r = functools.partial(_rows_spec, n_tiles=nt)
    return _call(
        body, "mixout_bwd", nt,
        [r(TILE_WIDE, D), r(TILE_WIDE, D), r(TILE_WIDE, D_ATTN), r(TILE_WIDE, D_S5), r(TILE_WIDE, D_LRU), _whole(), _whole()],
        [r(TILE_WIDE, D_ATTN), r(TILE_WIDE, D_S5), r(TILE_WIDE, D_LRU), _whole(), _whole(), _whole()],
        [_sds((n, D_ATTN)), _sds((n, D_S5)), _sds((n, D_LRU)), _sds((D, D)), _sds((1, D)), _sds((1, D))],
    )(dr1, mix, ya, ys, yl, g_mix, w_out)


def _ffn_conv(gp, halo, cw_ref, cb_ref, cs):
    ext = jnp.concatenate([halo, gp], axis=0)
    s1 = _shift_down(ext, 1, TILE)
    s2 = _shift_down(ext, 2, TILE)
    return s1, s2, cb_ref[:, cs] + cw_ref[2:3, cs] * gp + cw_ref[1:2, cs] * s1 + cw_ref[0:1, cs] * s2


def _ffn_fwd(x1, wg, wu, cw, cb, wd, g2, b2):
    n = x1.shape[0]
    nt = n // TILE

    def body(x_ref, wg_ref, wu_ref, cw_ref, cb_ref, wd_ref, g_ref, be_ref, gp_ref, up_ref, r_ref, x2_ref, halo_s, act_s):
        @pl.when(pl.program_id(0) == 0)
        def _():
            halo_s[...] = jnp.zeros_like(halo_s)

        x1 = x_ref[...]
        xb = _mx(x1)
        for c in range(D_FF // FF_CHUNK):
            cs = slice(c * FF_CHUNK, (c + 1) * FF_CHUNK)
            gp = _dot(xb, wg_ref[:, cs])
            up = _dot(xb, wu_ref[:, cs])
            gp_ref[:, cs] = gp
            up_ref[:, cs] = up
            _, _, gc = _ffn_conv(gp, halo_s[:, cs], cw_ref, cb_ref, cs)
            halo_s[:, cs] = gp[TILE - 8:]
            act_s[:, cs] = _mx(gc * _sigmoid(gc) * up)
        r2 = ALPHA * x1 + _dot(act_s[...], wd_ref[...])
        r_ref[...] = r2
        xhat, _ = _ln_stats(r2)
        x2_ref[...] = xhat * g_ref[...] + be_ref[...]

    r = functools.partial(_rows_spec, n_tiles=nt)
    return _call(
        body, "ffn_fwd", nt,
        [r(TILE, D)] + [_whole()] * 7,
        [r(TILE, D_FF), r(TILE, D_FF), r(TILE, D), r(TILE, D)],
        [_sds((n, D_FF)), _sds((n, D_FF)), _sds((n, D)), _sds((n, D))],
        scratch=[pltpu.VMEM((8, D_FF), F32), pltpu.VMEM((TILE, D_FF), MXU)],
    )(x1, wg, wu, cw, cb, wd, g2, b2)


def _ffn_bwd_down(dx2, r2, g2, gp, up, cw, cb, wd_t):
    n = dx2.shape[0]
    nt = n // TILE

    def body(dx_ref, r_ref, g_ref, gp_ref, gh_ref, up_ref, cw_ref, cb_ref, wd_ref,
             dr_ref, dgp_ref, dup_ref, dwd_ref, dcw0, dcw1, dcw2, dcb_ref, dg_ref, db_ref, next_s):
        first_tile = pl.program_id(0) == nt - 1

        @pl.when(pl.program_id(0) == 0)
        def _():
            for ref in (dwd_ref, dcw0, dcw1, dcw2, dcb_ref, dg_ref, db_ref, next_s):
                ref[...] = jnp.zeros_like(ref)

        dx2 = dx_ref[...]
        xhat, rstd = _ln_stats(r_ref[...])
        dg_ref[...] += _sum0(dx2 * xhat)
        db_ref[...] += _sum0(dx2)
        dr2 = _ln_bwd(dx2, g_ref[...], xhat, rstd)
        dr_ref[...] = dr2
        dfb = _mx(dr2)
        for c in range(D_FF // FF_CHUNK):
            cs = slice(c * FF_CHUNK, (c + 1) * FF_CHUNK)
            gp = gp_ref[:, cs]
            up = up_ref[:, cs]
            s1, s2, gc = _ffn_conv(gp, jnp.where(first_tile, 0.0, gh_ref[:, cs]), cw_ref, cb_ref, cs)
            sg = _sigmoid(gc)
            silu = gc * sg
            dact = _dot(dfb, wd_ref[:, cs])
            dwd_ref[cs, :] += _dot_tn(_mx(silu * up), dfb)
            dup_ref[:, cs] = _mx(dact * silu)
            dgc = dact * up * (sg + silu * (1.0 - sg))
            dcb_ref[:, cs] += _sum0(dgc)
            dcw2[:, cs] += _sum0(dgc * gp)
            dcw1[:, cs] += _sum0(dgc * s1)
            dcw0[:, cs] += _sum0(dgc * s2)
            ext = jnp.concatenate([dgc, next_s[:, cs]], axis=0)
            dgp_ref[:, cs] = _mx(cw_ref[2:3, cs] * dgc + cw_ref[1:2, cs] * _shift_up(ext, 1, TILE)
                                 + cw_ref[0:1, cs] * _shift_up(ext, 2, TILE))
            next_s[:, cs] = dgc[:8]

    r = functools.partial(_rows_spec, n_tiles=nt, reverse=True)
    vff = _sds((1, D_FF))
    return _call(
        body, "ffn_bwd_down", nt,
        [r(TILE, D), r(TILE, D), _whole(), r(TILE, D_FF), _halo_spec(D_FF, TILE, nt, reverse=True), r(TILE, D_FF), _whole(), _whole(),
         _whole()],
        [r(TILE, D), r(TILE, D_FF), r(TILE, D_FF)] + [_whole()] * 7,
        [_sds((n, D)), _sds((n, D_FF), MXU), _sds((n, D_FF), MXU), _sds((D_FF, D)), vff, vff, vff, vff, _sds((1, D)), _sds((1, D))],
        scratch=[pltpu.VMEM((8, D_FF), F32)],
    )(dx2, r2, g2, gp, gp, up, cw, cb, wd_t)


def _ffn_bwd_dx(dr2, dgp, dup, r1, g1, wg_t, wu_t):
    n = dr2.shape[0]
    rows = TILE_BIG
    nt = n // rows

    def body(dr2_ref, dgp_ref, dup_ref, r_ref, g_ref, wg_ref, wu_ref, dr1_ref, dg_ref, db_ref):
        @pl.when(pl.program_id(0) == 0)
        def _():
            for ref in (dg_ref, db_ref):
                ref[...] = jnp.zeros_like(ref)

        dx1 = ALPHA * dr2_ref[...] + _dot(dgp_ref[...], wg_ref[...]) + _dot(dup_ref[...], wu_ref[...])
        xhat, rstd = _ln_stats(r_ref[...])
        dg_ref[...] += _sum0(dx1 * xhat)
        db_ref[...] += _sum0(dx1)
        dr1_ref[...] = _ln_bwd(dx1, g_ref[...], xhat, rstd)

    r = functools.partial(_rows_spec, n_tiles=nt)
    return _call(
        body, "ffn_bwd_dx", nt,
        [r(rows, D), r(rows, D_FF), r(rows, D_FF), r(rows, D), _whole(), _whole(), _whole()],
        [r(rows, D), _whole(), _whole()],
        [_sds((n, D)), _sds((1, D)), _sds((1, D))],
    )(dr2, dgp, dup, r1, g1, wg_t, wu_t)


def _ffn_bwd_dw(x1, dgp, dup):
    n = x1.shape[0]
    rows = TILE_BIG
    nt = n // rows

    def body(x_ref, dgp_ref, dup_ref, dwg_ref, dwu_ref):
        @pl.when(pl.program_id(0) == 0)
        def _():
            for ref in (dwg_ref, dwu_ref):
                ref[...] = jnp.zeros_like(ref)

        xb = _mx(x_ref[...])
        for c in range(D_FF // FF_CHUNK):
            cs = slice(c * FF_CHUNK, (c + 1) * FF_CHUNK)
            dwg_ref[cs, :] += _dot_tn(dgp_ref[:, cs], xb)
            dwu_ref[cs, :] += _dot_tn(dup_ref[:, cs], xb)

    r = functools.partial(_rows_spec, n_tiles=nt)
    return _call(
        body, "ffn_bwd_dw", nt,
        [r(rows, D), r(rows, D_FF), r(rows, D_FF)], [_whole(), _whole()], [_sds((D_FF, D)), _sds((D_FF, D))],
    )(x1, dgp, dup)


def _loss_head(y, target):
    n = y.shape[0]
    nt = n // TILE_WIDE

    def body(y_ref, t_ref, loss_ref, dy_ref):
        @pl.when(pl.program_id(0) == 0)
        def _():
            loss_ref[...] = jnp.zeros_like(loss_ref)

        e = y_ref[...] - t_ref[...]
        dy_ref[...] = e * (1.0 / D)
        loss_ref[...] += _sum0(jnp.sum(e * e, axis=1, keepdims=True)) * (0.5 / D)

    r = functools.partial(_rows_spec, n_tiles=nt)
    return _call(body, "loss_head", nt, [r(TILE_WIDE, D), r(TILE_WIDE, D)], [_whole(), r(TILE_WIDE, D)],
                 [_sds((1, 1)), _sds((n, D))])(y, target)


def _place():
    x, y, c = lax.axis_index("x"), lax.axis_index("y"), lax.axis_index("c")
    return x, y, c, 4 * x + 2 * y + c


def _peer(x, y, c, k):
    px, py, pc = x ^ ((k >> 2) & 1), y ^ ((k >> 1) & 1), c ^ (k & 1)
    return (px, py, pc), 4 * px + 2 * py + pc


def _all_gather(name, blocks, small):
    srcs = list(blocks) + [small]
    n = len(srcs)
    out_shapes = [_sds((a.shape[0], N_DEV * a.shape[1], LANES), a.dtype) for a in blocks] + [_sds((N_DEV,) + small.shape, small.dtype)]

    def body(*refs):
        src_refs, out_refs = refs[:n], refs[n:2 * n]
        send_sems, recv_sems, local_sems = refs[2 * n:]
        x, y, c, me = _place()

        def landing(a, slot):
            if a == n - 1:
                return out_refs[a].at[slot]
            r = src_refs[a].shape[1]
            return out_refs[a].at[:, pl.ds(slot * r, r), :]

        def remote(a, k, slot):
            peer, _ = _peer(x, y, c, k)
            return pltpu.make_async_remote_copy(
                src_ref=src_refs[a], dst_ref=landing(a, slot), send_sem=send_sems.at[a * N_DEV + k],
                recv_sem=recv_sems.at[a * N_DEV + k], device_id=peer, device_id_type=pl.DeviceIdType.MESH)

        mine = [pltpu.make_async_copy(src_refs[a], landing(a, me), local_sems.at[a]) for a in range(n)]
        sends = [remote(a, k, me) for a in range(n) for k in range(1, N_DEV)]
        for cp in mine + sends:
            cp.start()
        for a in range(n):
            for k in range(1, N_DEV):
                remote(a, k, _peer(x, y, c, k)[1]).wait_recv()
        for cp in sends:
            cp.wait_send()
        for cp in mine:
            cp.wait()

    any_space = pl.BlockSpec(memory_space=pl.ANY)
    return pl.pallas_call(
        body, name=name, out_shape=out_shapes, in_specs=[any_space] * n, out_specs=[any_space] * n,
        scratch_shapes=[pltpu.SemaphoreType.DMA((n * N_DEV,)), pltpu.SemaphoreType.DMA((n * N_DEV,)), pltpu.SemaphoreType.DMA((n,))],
    )(*srcs)


_HBM = pl.BlockSpec(memory_space=pltpu.HBM)
_SEM = pl.BlockSpec(memory_space=pltpu.SEMAPHORE)
_EFFECT = pltpu.SideEffectType.DATAFLOW_SIDE_EFFECTING


def _in_hbm(a):
    return pltpu.with_memory_space_constraint(a, pltpu.HBM)


def _scatter_start(name, srcs):
    ns = len(srcs)
    rows_a = [a.shape[0] // N_DEV for a in srcs]
    offs = [sum(rows_a[:a]) for a in range(ns)]
    total = sum(rows_a)

    def body(*refs):
        src_refs, land_ref, send_sems, recv_sems, token = refs[:ns], refs[ns], refs[ns + 1], refs[ns + 2], refs[-1]
        x, y, c, me = _place()
        for a in range(ns):
            pltpu.make_async_copy(src_refs[a].at[pl.ds(me * rows_a[a], rows_a[a]), :],
                                  land_ref.at[me, pl.ds(offs[a], rows_a[a]), :], send_sems.at[0]).start()
        for k in range(1, N_DEV):
            peer, peer_slot = _peer(x, y, c, k)
            for a in range(ns):
                pltpu.make_async_remote_copy(
                    src_ref=src_refs[a].at[pl.ds(peer_slot * rows_a[a], rows_a[a]), :],
                    dst_ref=land_ref.at[me, pl.ds(offs[a], rows_a[a]), :], send_sem=send_sems.at[k], recv_sem=recv_sems.at[k],
                    device_id=peer, device_id_type=pl.DeviceIdType.MESH).start()
        token[...] = jnp.zeros_like(token)

    landing = lax.empty((N_DEV, total, LANES), F32)
    out = pl.pallas_call(
        body, name=name,
        out_shape=(pltpu.SemaphoreType.DMA((N_DEV,)), pltpu.SemaphoreType.DMA((N_DEV,)), *[pltpu.HBM(a.shape, a.dtype) for a in srcs],
                   pltpu.HBM(landing.shape, F32), _sds((8, 128))),
        in_specs=[_HBM] * (ns + 1), out_specs=(_SEM, _SEM, *[_HBM] * (ns + 1), pl.BlockSpec(memory_space=pltpu.VMEM)),
        input_output_aliases={a: 2 + a for a in range(ns + 1)},
        compiler_params=pltpu.CompilerParams(has_side_effects=_EFFECT),
    )(*[_in_hbm(a) for a in srcs], _in_hbm(landing))
    return out[0], out[1], out[2:2 + ns], out[2 + ns], out[-1]


def _scatter_wait(name, send_sems, recv_sems, srcs, landing, after):
    ns = len(srcs)

    def body(*refs):
        land_ref, send_ref, recv_ref = refs[ns], refs[ns + 1], refs[ns + 2]
        x, y, c, me = _place()
        pltpu.make_async_copy(land_ref.at[me], land_ref.at[me], send_ref.at[0]).wait()
        for k in range(1, N_DEV):
            peer, peer_slot = _peer(x, y, c, k)
            slot = pltpu.make_async_remote_copy(
                src_ref=land_ref.at[me], dst_ref=land_ref.at[peer_slot], send_sem=send_ref.at[k], recv_sem=recv_ref.at[k],
                device_id=peer, device_id_type=pl.DeviceIdType.MESH)
            slot.wait_send()
            slot.wait_recv()

    out = pl.pallas_call(
        body, name=name, out_shape=(*[pltpu.HBM(a.shape, a.dtype) for a in srcs], pltpu.HBM(landing.shape, landing.dtype)),
        in_specs=[_HBM] * (ns + 1) + [_SEM, _SEM, pl.BlockSpec(memory_space=pl.ANY)], out_specs=[_HBM] * (ns + 1),
        input_output_aliases={a: a for a in range(ns + 1)},
        compiler_params=pltpu.CompilerParams(has_side_effects=_EFFECT),
    )(*srcs, landing, send_sems, recv_sems, after)
    return out[:ns], out[ns]


def _gather_start(name, blocks):
    n = len(blocks)

    def body(*refs):
        src_refs, land_refs, send_sems, recv_sems, token = refs[:n], refs[n:2 * n], refs[2 * n], refs[2 * n + 1], refs[-1]
        x, y, c, me = _place()
        for a in range(n):
            r = src_refs[a].shape[1]
            mine = land_refs[a].at[:, pl.ds(me * r, r), :]
            pltpu.make_async_copy(src_refs[a], mine, send_sems.at[a * N_DEV]).start()
            for k in range(1, N_DEV):
                pltpu.make_async_remote_copy(
                    src_ref=src_refs[a], dst_ref=mine, send_sem=send_sems.at[a * N_DEV + k],
                    recv_sem=recv_sems.at[a * N_DEV + k], device_id=_peer(x, y, c, k)[0], device_id_type=pl.DeviceIdType.MESH).start()
        token[...] = jnp.zeros_like(token)

    wholes = [lax.empty((a.shape[0], N_DEV * a.shape[1], LANES), a.dtype) for a in blocks]
    out = pl.pallas_call(
        body, name=name,
        out_shape=(pltpu.SemaphoreType.DMA((n * N_DEV,)), pltpu.SemaphoreType.DMA((n * N_DEV,)),
                   *[pltpu.HBM(a.shape, a.dtype) for a in blocks + wholes], _sds((8, 128))),
        in_specs=[_HBM] * (2 * n), out_specs=(_SEM, _SEM, *[_HBM] * (2 * n), pl.BlockSpec(memory_space=pltpu.VMEM)),
        input_output_aliases={a: 2 + a for a in range(2 * n)},
        compiler_params=pltpu.CompilerParams(has_side_effects=_EFFECT),
    )(*[_in_hbm(a) for a in blocks + wholes])
    return out[0], out[1], out[2:2 + n], out[2 + n:2 + 2 * n], out[-1]


def _gather_wait(name, send_sems, recv_sems, blocks, wholes, after):
    n = len(blocks)

    def body(*refs):
        src_refs, land_refs, send_ref, recv_ref = refs[:n], refs[n:2 * n], refs[2 * n], refs[2 * n + 1]
        x, y, c, me = _place()
        for a in range(n):
            r = src_refs[a].shape[1]
            pltpu.make_async_copy(src_refs[a], land_refs[a].at[:, pl.ds(me * r, r), :], send_ref.at[a * N_DEV]).wait()
            for k in range(1, N_DEV):
                peer, peer_slot = _peer(x, y, c, k)
                cp = pltpu.make_async_remote_copy(
                    src_ref=src_refs[a], dst_ref=land_refs[a].at[:, pl.ds(peer_slot * r, r), :], send_sem=send_ref.at[a * N_DEV + k],
                    recv_sem=recv_ref.at[a * N_DEV + k], device_id=peer, device_id_type=pl.DeviceIdType.MESH)
                cp.wait_send()
                cp.wait_recv()

    return pl.pallas_call(
        body, name=name, out_shape=tuple(pltpu.HBM(a.shape, a.dtype) for a in list(blocks) + list(wholes)),
        in_specs=[_HBM] * (2 * n) + [_SEM, _SEM, pl.BlockSpec(memory_space=pl.ANY)], out_specs=[_HBM] * (2 * n),
        input_output_aliases={a: a for a in range(2 * n)},
        compiler_params=pltpu.CompilerParams(has_side_effects=_EFFECT),
    )(*blocks, *wholes, send_sems, recv_sems, after)


def _reduce_adamw(parts, w, m, v, tile_rows=PACK_TILE):
    rows = w.shape[0]
    nt = rows // tile_rows
    slots = parts.shape[0]
    c1 = 1.0 - ADAM_B1 ** ADAM_STEP
    c2 = 1.0 - ADAM_B2 ** ADAM_STEP

    def body(p_ref, w_ref, m_ref, v_ref, g_out, d_out, m_out, v_out):
        g = p_ref[0]
        for s in range(1, slots):
            g = g + p_ref[s]
        m_new = ADAM_B1 * m_ref[...] + (1.0 - ADAM_B1) * g
        v_new = ADAM_B2 * v_ref[...] + (1.0 - ADAM_B2) * (g * g)
        g_out[...] = g
        m_out[...] = m_new
        v_out[...] = v_new
        d_out[...] = -ADAM_LR * ((m_new / c1) / (jnp.sqrt(v_new / c2) + ADAM_EPS) + ADAM_WD * w_ref[...])

    r = _rows_spec(tile_rows, LANES, nt)
    out = _sds((rows, LANES))
    return _call(
        body, "reduce_adamw", nt,
        [pl.BlockSpec((slots, tile_rows, LANES), lambda i: (0, i, 0)), r, r, r], [r, r, r, r], [out, out, out, out],
    )(parts, w, m, v)


def _pack_rows(a, lead=0):
    head = a.shape[:lead]
    flat = a.reshape(head + (-1,))
    size = flat.shape[-1]
    rows = -(-size // (16 * LANES)) * 16
    flat = jnp.pad(flat, [(0, 0)] * lead + [(0, rows * LANES - size)])
    return flat.reshape(head + (rows, LANES))


def _packed_rows(shape):
    return -(-math.prod(shape) // (16 * LANES)) * 16


def _to_blocks(full, axis):
    l, a, b = full.shape
    if axis == 2:
        return full.reshape(l, a, N_DEV, b // N_DEV).transpose(2, 0, 1, 3)
    return full.reshape(l, N_DEV, a // N_DEV, b).transpose(1, 0, 2, 3)


def _from_blocks(blocks, axis):
    _, l, a, b = blocks.shape
    if axis == 2:
        return blocks.transpose(1, 2, 0, 3).reshape(l, a, N_DEV * b)
    return blocks.transpose(1, 0, 2, 3).reshape(l, N_DEV * a, b)


def _row_form(shard, transposed):
    return shard.transpose(0, 2, 1) if transposed else shard


def _me():
    return 4 * lax.axis_index("x") + 2 * lax.axis_index("y") + lax.axis_index("c")


def _both_forms(names, wholes, layer):
    out = {}
    for name, w in zip(names, wholes):
        t = dict(BIG)[name]
        out[name + '_t' if t else name] = w[layer]
        out[name if t else name + '_t'] = w[layer].T
    return out


def _gather_weights(local):
    segs, meta = [], []
    for name in SMALL_SHARDED:
        blk = local[name]
        if name in GATHER_F32:
            bits = lax.bitcast_convert_type(blk, MXU)
        else:
            bits = _mx(blk)
        seg = _pack_rows(bits)
        meta.append((name, bits.shape, seg.shape[0]))
        segs.append(seg)
    blocks = {name: _mx(_row_form(local[name], t)) for name, t in BIG}
    mix, ffn = PARTS['mix'], PARTS['ffn']
    *first, gathered = _all_gather("gather_weights", [blocks[n][:1] for n in mix], jnp.concatenate(segs, axis=0))
    flights = {'ffn0': _gather_start("gather_ffn0_start", [blocks[n][:1] for n in ffn]),
               'later': _gather_start("gather_later_start", [blocks[n][1:] for n in mix + ffn])}
    out, lo = {}, 0
    for name, bits_shape, rows in meta:
        seg = gathered[:, lo:lo + rows].reshape(N_DEV, -1)[:, :math.prod(bits_shape)].reshape((N_DEV,) + bits_shape)
        if name in GATHER_F32:
            seg = lax.bitcast_convert_type(seg, F32)
        out[name] = _from_blocks(seg, SHARD_AXIS[name])
        lo += rows
    ready = {(0, 'mix'): _both_forms(mix, first, 0)}

    def landed(flight, names, after):
        send_sems, recv_sems, mine, wholes, _ = flights[flight]
        return _gather_wait(f"gather_{flight}_wait", send_sems, recv_sems, mine, wholes, after)[len(names):]

    def big_weights(l, part, after):
        if (l, part) not in ready and l == 0:
            ready[(0, 'ffn')] = _both_forms(ffn, landed('ffn0', ffn, after), 0)
        elif (l, part) not in ready:
            wholes = landed('later', mix + ffn, after)
            for j in range(1, DEPTH):
                forms = _both_forms(mix + ffn, wholes, j - 1)
                for p, names in PARTS.items():
                    ready[(j, p)] = {k: forms[k] for n in names for k in (n, n + '_t')}
        return ready[(l, part)]

    return out, big_weights, flights['ffn0'][-1][0, 0] + flights['later'][-1][0, 0]


def _s5_discretize(a_re, a_im, log_dt, b_re, b_im):
    lam_re = jnp.minimum(a_re, -1e-4)
    lam_im = a_im
    dt = jnp.exp(log_dt)[:, None]
    decay = jnp.exp(dt * lam_re)
    ang = dt * lam_im
    abar_re = decay * jnp.cos(ang)
    abar_im = decay * jnp.sin(ang)
    den = jnp.square(lam_re) + jnp.square(lam_im)
    nr = abar_re - 1.0
    ni = abar_im
    coef_re = (nr * lam_re + ni * lam_im) / den
    coef_im = (ni * lam_re - nr * lam_im) / den
    bbar_re = coef_re[..., None] * b_re - coef_im[..., None] * b_im
    bbar_im = coef_re[..., None] * b_im + coef_im[..., None] * b_re
    return abar_re, abar_im, bbar_re, bbar_im


def _complex_powers(ar, ai, count):
    exponents = jnp.arange(1, count + 1).reshape((count,) + (1,) * ar.ndim)
    pr = jnp.ones((count,) + ar.shape, F32)
    pi = jnp.zeros_like(pr)
    sr, si = ar, ai
    for b in range(count.bit_length()):
        bit = ((exponents >> b) & 1) == 1
        fr, fi = jnp.where(bit, sr[None], 1.0), jnp.where(bit, si[None], 0.0)
        pr, pi = pr * fr - pi * fi, pr * fi + pi * fr
        sr, si = sr * sr - si * si, 2.0 * sr * si
    return pr, pi


_EYE16 = functools.partial(jnp.eye, 16, dtype=F32)


def _s5_params(p):
    disc, disc_vjp = jax.vjp(jax.vmap(_s5_discretize), p['s5_a_re'], p['s5_a_im'], p['s5_log_dt'], p['s5_b_re'], p['s5_b_im'])
    abar_re, abar_im, bbar_re, bbar_im = disc
    pw_re, pw_im = _complex_powers(abar_re.reshape(DEPTH, N_STATE), abar_im.reshape(DEPTH, N_STATE), SEG)
    sp_re, sp_im = _complex_powers(pw_re[SEG - 1], pw_im[SEG - 1], N_SEG - 1)
    one, zero = jnp.ones((1, DEPTH, N_STATE), F32), jnp.zeros((1, DEPTH, N_STATE), F32)
    seg_re = jnp.concatenate([one, sp_re], axis=0)
    seg_im = jnp.concatenate([zero, sp_im], axis=0)
    doubling = [0, 1, 3]
    blank = jnp.zeros((5, DEPTH, N_STATE), F32)
    tables = {
        'pw_re': pw_re, 'pw_im': pw_im,
        'dbl_re': jnp.concatenate([jnp.stack([sp_re[k] for k in doubling]), blank], axis=0),
        'dbl_im': jnp.concatenate([jnp.stack([sp_im[k] for k in doubling]), blank], axis=0),
        'seg_re': seg_re, 'seg_im': seg_im, 'segr_re': seg_re[::-1], 'segr_im': seg_im[::-1],
    }
    weights = {
        'b_re': _mx(jnp.einsum('lgpc,gh->lgchp', bbar_re, _EYE16()).reshape(DEPTH, D_S5, N_STATE)),
        'b_im': _mx(jnp.einsum('lgpc,gh->lgchp', bbar_im, _EYE16()).reshape(DEPTH, D_S5, N_STATE)),
        'c_re': _mx(jnp.einsum('lgcp,gh->lgphc', p['s5_c_re'], _EYE16()).reshape(DEPTH, N_STATE, D_S5)),
        'c_im': _mx(jnp.einsum('lgcp,gh->lgphc', p['s5_c_im'], _EYE16()).reshape(DEPTH, N_STATE, D_S5)),
    }
    src = (jnp.arange(TILE) % N_SEG) * SEG + jnp.arange(TILE) // N_SEG
    perm = (src[:, None] == jnp.arange(TILE)[None, :]).astype(MXU)
    layers = []
    for l in range(DEPTH):
        prm = {k: v[:, l] for k, v in tables.items()}
        prm.update({k: v[l] for k, v in weights.items()})
        prm.update({'perm': perm, 'perm_t': perm.T, 'd': p['s5_d'][l][None, :], 'glu_w': p['s5_glu_w'][l],
                    'glu_b': p['s5_glu_b'][l][None, :]})
        layers.append(prm)
    return layers, disc_vjp


def _lru_params(p, l):
    eye4 = jnp.eye(4, dtype=F32)
    return {
        'conv_w': p['lru_conv_w'][l], 'conv_b': p['lru_conv_b'][l][None, :],
        'wx': _mx(jnp.einsum('hij,hk->hikj', p['lru_wx'][l], eye4).reshape(D_LRU, D_LRU)),
        'wa': _mx(jnp.einsum('hij,hk->hikj', p['lru_wa'][l], eye4).reshape(D_LRU, D_LRU)),
        'bx': p['lru_bx'][l][None, :], 'ba': p['lru_ba'][l][None, :],
        'sp': jax.nn.softplus(-p['lru_a_param'][l])[None, :],
    }


def _rope_tables(n):
    inv_freq = ROPE_THETA ** (-jnp.arange(0, 64, 2, dtype=F32) / 64)
    ang = jnp.arange(n, dtype=F32)[:, None] * inv_freq[None, :]
    cos, sin = jnp.cos(ang), jnp.sin(ang)
    return jnp.concatenate([cos, cos, cos, cos], axis=1), jnp.concatenate([-sin, sin, -sin, sin], axis=1)


def _sink_cols(sinks):
    nb = TILE_Q // ATTN_BLOCK
    per_unit = sinks.reshape(4, 2).T
    return jnp.broadcast_to(per_unit[:, None, :, None, None], (2, nb, 4, ATTN_BLOCK, 128)).reshape(2, nb * 4 * ATTN_BLOCK, 128)


def _local_step(x, target, p, big_weights=None, emit_grads=None):
    if big_weights is None:
        big_weights = lambda l, part, after: {k: p[k][l] for name in PARTS[part] for k in (name, name + '_t')}
    if emit_grads is None:
        emit_grads = lambda l, part, grads: 0.0
    n = x.shape[0]
    cos_t, sin_t = _rope_tables(n)
    s5_layers, s5_vjp = _s5_params(p)
    row = lambda a: a[None, :]
    saved = []
    h = x
    for l in range(DEPTH):
        s = {'x0': h}
        bw = s['bw'] = dict(big_weights(l, 'mix', h))
        s['q'], k, v, s['u'], s['xr'], s['gate'] = _inproj_fwd(h, bw['w_in'], row(p['b_in'][l]), cos_t, sin_t)
        no_keys = jnp.zeros((ATTN_BLOCK, D_KV), MXU)
        s['k'], s['v'] = jnp.concatenate([no_keys, k], axis=0), jnp.concatenate([no_keys, v], axis=0)
        s['sinks'] = _sink_cols(p['attn_sinks'][l])
        s['ya'] = _attn_fwd(s['q'], s['k'], s['v'], s['sinks'])
        s['s5'] = s5_layers[l]
        s['lru'] = _lru_params(p, l)
        s['ys'], s['s5_cr'], s['s5_ci'] = _s5_fwd(s['u'], s['s5'])
        s['yl'], s['lru_c'] = _lru_fwd(s['xr'], s['gate'], s['lru'])
        s['mix'], s['r1'], s['x1'] = _mixout_fwd(s['ya'], s['ys'], s['yl'], h, row(p['mix_norm_g'][l]), bw['w_out'],
                                                 row(p['b_out'][l]), row(p['ln1_g'][l]), row(p['ln1_b'][l]))
        bw.update(big_weights(l, 'ffn', s['x1']))
        s['gp'], s['up'], s['r2'], h = _ffn_fwd(s['x1'], bw['ffn_w_gate'], bw['ffn_w_up'], p['ffn_conv_w'][l],
                                                row(p['ffn_conv_b'][l]), bw['ffn_w_down'], row(p['ln2_g'][l]), row(p['ln2_b'][l]))
        saved.append(s)
    loss, dh = _loss_head(h, target)
    placed = 0.0

    grads = {name: [None] * DEPTH for name in WEIGHTS}
    d_disc = [None] * DEPTH
    for l in reversed(range(DEPTH)):
        s = saved[l]
        g = {}
        (dr2, dgp, dup, g['ffn_w_down'], cw0, cw1, cw2, dcb, dg2, db2) = _ffn_bwd_down(
            dh, s['r2'], row(p['ln2_g'][l]) + placed, s['gp'], s['up'], p['ffn_conv_w'][l], row(p['ffn_conv_b'][l]),
            s['bw']['ffn_w_down_t'])
        g['ffn_conv_w'] = jnp.concatenate([cw0, cw1, cw2], axis=0)
        g['ffn_conv_b'], g['ln2_g'], g['ln2_b'] = dcb[0], dg2[0], db2[0]
        dr1, dg1, db1 = _ffn_bwd_dx(dr2, dgp, dup, s['r1'], row(p['ln1_g'][l]), s['bw']['ffn_w_gate_t'], s['bw']['ffn_w_up_t'])
        g['ffn_w_gate'], g['ffn_w_up'] = _ffn_bwd_dw(s['x1'], dgp, dup)
        g['ln1_g'], g['ln1_b'] = dg1[0], db1[0]
        placed = emit_grads(l, 'ffn', [g[name] for name in PARTS['ffn']])
        dya, dys, dyl, g['w_out'], dbo, dgm = _mixout_bwd(dr1, s['mix'], s['ya'], s['ys'], s['yl'],
                                                         row(p['mix_norm_g'][l]) + placed, s['bw']['w_out_t'])
        g['b_out'], g['mix_norm_g'] = dbo[0], dgm[0]

        du, dbr, dbi, dcr, dci, dar, dai, dd, g['s5_glu_w'], dgb = _s5_bwd(s['u'], dys, s['s5_cr'], s['s5_ci'], s['s5'])
        dxr, dgate, lw0, lw1, lw2, lw3, lcb, dwx, dwa, dbx, dba, dsp = _lru_bwd(s['xr'], s['gate'], dyl, s['lru_c'], s['lru'])
        g['lru_conv_w'] = jnp.concatenate([lw0, lw1, lw2, lw3], axis=0)
        g['lru_conv_b'], g['lru_bx'], g['lru_ba'] = lcb[0], dbx[0], dba[0]
        g['lru_wx'] = jnp.einsum('hihj->hij', dwx.reshape(4, 64, 4, 64))
        g['lru_wa'] = jnp.einsum('hihj->hij', dwa.reshape(4, 64, 4, 64))
        g['lru_a_param'] = -dsp[0] * jax.nn.sigmoid(-p['lru_a_param'][l])

        g['s5_c_re'] = jnp.einsum('gpgc->gcp', dcr.reshape(16, 64, 16, 16))
        g['s5_c_im'] = jnp.einsum('gpgc->gcp', dci.reshape(16, 64, 16, 16))
        g['s5_d'], g['s5_glu_b'] = dd[0], dgb[0]
        d_disc[l] = (dar.reshape(16, 64), dai.reshape(16, 64), jnp.einsum('gcgp->gpc', dbr.reshape(16, 16, 16, 64)),
                     jnp.einsum('gcgp->gpc', dbi.reshape(16, 16, 16, 64)))

        dq, dk, dv, dsink = _attn_bwd(s['q'], s['k'], s['v'], s['sinks'], s['ya'], dya)
        g['attn_sinks'] = dsink[:, 0]
        dh, g['w_in'], dbin = _inproj_bwd(dq, dk[ATTN_BLOCK:], dv[ATTN_BLOCK:], du, dxr, dgate, cos_t, sin_t, s['x0'], dr1,
                                          s['bw']['w_in_t'])
        g['b_in'] = dbin[0]
        placed = emit_grads(l, 'mix', [g[name] for name in PARTS['mix']])
        for name in g:
            grads[name][l] = g[name]
    big = dict(BIG)
    out = {name: grads[name] if name in big else jnp.stack(grads[name]) for name in WEIGHTS if name not in S5_DISC}
    out.update(zip(S5_DISC, s5_vjp(tuple(jnp.stack([d_disc[l][i] for l in range(DEPTH)]) for i in range(4)))))
    return loss, dh, out


def kernel(x, w_in, b_in, attn_sinks, s5_a_re, s5_a_im, s5_b_re, s5_b_im, s5_c_re, s5_c_im, s5_d, s5_log_dt, s5_glu_w, s5_glu_b, lru_conv_w, lru_conv_b, lru_wx, lru_bx, lru_wa, lru_ba, lru_a_param, mix_norm_g, w_out, b_out, ln1_g, ln1_b, ffn_w_gate, ffn_w_up, ffn_conv_w, ffn_conv_b, ffn_w_down, ln2_g, ln2_b, loss_target, m_w_in, m_b_in, m_attn_sinks, m_s5_a_re, m_s5_a_im, m_s5_b_re, m_s5_b_im, m_s5_c_re, m_s5_c_im, m_s5_d, m_s5_log_dt, m_s5_glu_w, m_s5_glu_b, m_lru_conv_w, m_lru_conv_b, m_lru_wx, m_lru_bx, m_lru_wa, m_lru_ba, m_lru_a_param, m_mix_norm_g, m_w_out, m_b_out, m_ln1_g, m_ln1_b, m_ffn_w_gate, m_ffn_w_up, m_ffn_conv_w, m_ffn_conv_b, m_ffn_w_down, m_ln2_g, m_ln2_b, v_w_in, v_b_in, v_attn_sinks, v_s5_a_re, v_s5_a_im, v_s5_b_re, v_s5_b_im, v_s5_c_re, v_s5_c_im, v_s5_d, v_s5_log_dt, v_s5_glu_w, v_s5_glu_b, v_lru_conv_w, v_lru_conv_b, v_lru_wx, v_lru_bx, v_lru_wa, v_lru_ba, v_lru_a_param, v_mix_norm_g, v_w_out, v_b_out, v_ln1_g, v_ln1_b, v_ffn_w_gate, v_ffn_w_up, v_ffn_conv_w, v_ffn_conv_b, v_ffn_w_down, v_ln2_g, v_ln2_b):
    given = dict(locals())
    whole = {name: given[name] for name in WEIGHTS if name not in dict(BIG)}
    small_whole, big_weights, placed = _gather_weights({name: given[name] for name in SHARDED})
    whole.update(small_whole)
    whole['b_in'] = whole['b_in'] + placed
    in_flight = {}

    def emit_grads(l, part, grads):
        in_flight[(l, part)] = _scatter_start(f"grads_start_{part}{l}", grads)
        return in_flight[(l, part)][-1][0, 0]

    loss, grad_x, grads = _local_step(x[0], loss_target[0], whole, big_weights, emit_grads)
    total = lax.psum(loss[0, 0], ("x", "y", "c"))
    return (total, grad_x[None], *_update(given, grads, in_flight, grad_x))


def _update(given, grads, in_flight, after):
    local_w = {name: given[name] for name in WEIGHTS}
    me = _me()
    outs = {}

    shard_rows = sum(_packed_rows(local_w[name].shape) for name in SMALL_SHARDED)
    rep_pad = -sum(_packed_rows(local_w[name].shape) for name in REPLICATED) % PACK_TILE

    def packed_rep(arrays):
        return jnp.concatenate([_pack_rows(arrays[name]) for name in REPLICATED] + [jnp.zeros((rep_pad, LANES), F32)], axis=0)

    rep_grads = packed_rep(grads)
    chunk = rep_grads.shape[0] // N_DEV
    small = jnp.concatenate(
        [_pack_rows(_to_blocks(grads[name], SHARD_AXIS[name]), lead=1) for name in SMALL_SHARDED]
        + [rep_grads.reshape(N_DEV, chunk, LANES)], axis=1)
    small_rows = shard_rows + chunk
    small_flight = _scatter_start("grads_start_small", [small.reshape(N_DEV * small_rows, LANES)])

    def summed(name, flight, rows_of, packed, after):
        send_sems, recv_sems, srcs, landing, _ = flight
        _, landing = _scatter_wait(name, send_sems, recv_sems, srcs, landing, after)
        return _reduce_adamw(landing, *packed, tile_rows=sum(rows_of) // 4)

    forms = {part: [(name, dict(BIG)[name], _row_form(local_w[name], dict(BIG)[name]).shape[1]) for name in names]
             for part, names in PARTS.items()}
    for (l, part), flight in in_flight.items():
        packed = [jnp.concatenate([_row_form(given[prefix + name], t)[l] for name, t, _ in forms[part]], axis=0)
                  for prefix in ('', 'm_', 'v_')]
        outs[(l, part)] = summed(f"grads_wait_{part}{l}", flight, [r for _, _, r in forms[part]], packed, after)
        after = outs[(l, part)][0]

    rep_state = [packed_rep({name: given[prefix + name] for name in REPLICATED}) for prefix in ('', 'm_', 'v_')]
    small_state = [jnp.concatenate([_pack_rows(given[prefix + name]) for name in SMALL_SHARDED]
                                   + [lax.dynamic_slice_in_dim(rep, me * chunk, chunk, axis=0)], axis=0)
                   for prefix, rep in zip(('', 'm_', 'v_'), rep_state)]
    small_outs = summed("grads_wait_small", small_flight, [small_rows], small_state, after)
    rep_sum = _all_gather("gather_small_grads", [], small_outs[0][shard_rows:])[0].reshape(N_DEV * chunk, LANES)
    rep_outs = _reduce_adamw(rep_sum[None], *rep_state, tile_rows=N_DEV * chunk // 4)

    def unpack(i):
        res = {}
        for part in PARTS:
            lo = 0
            for name, t, r in forms[part]:
                res[name] = _row_form(jnp.stack([outs[(l, part)][i][lo:lo + r] for l in range(DEPTH)]), t)
                lo += r
        for names, packed in ((SMALL_SHARDED, small_outs[i]), (REPLICATED, rep_outs[i])):
            lo = 0
            for name in names:
                shape = local_w[name].shape
                res[name] = packed[lo:lo + _packed_rows(shape)].reshape(-1)[:math.prod(shape)].reshape(shape)
                lo += _packed_rows(shape)
        return [res[name] for name in WEIGHTS]

    return (*unpack(0), *unpack(1), *unpack(2), *unpack(3))
```

```python
import functools
import math

import jax
import jax.numpy as jnp
from jax import lax
from jax.experimental import pallas as pl
from jax.experimental.pallas import tpu as pltpu

F32 = jnp.float32
MXU = jnp.bfloat16

N_DEV = 8
DEPTH = 4
D = 1024
D_ATTN, D_KV, D_S5, D_LRU = 512, 128, 256, 256
D_IN = 1536
D_FF = 2816
FF_CHUNK = 256
N_STATE = 1024
LANES = 1024
ALPHA = (2 * DEPTH) ** 0.25
LN_EPS = 1e-5
RMS_EPS = 1e-6
LRU_C = 8.0
ROPE_THETA = 10000.0
ADAM_LR, ADAM_B1, ADAM_B2, ADAM_EPS, ADAM_WD, ADAM_STEP = 0.001, 0.9, 0.999, 1e-08, 0.01, 10

TILE = 256
S5_PAIR = 4
N_SEG = 8
SEG = TILE // N_SEG
TILE_Q = 512
TILE_BIG = 512
TILE_WIDE = 512
ATTN_BLOCK = 128
PACK_TILE = 256
VMEM_MB = 56

WEIGHTS = ['w_in', 'b_in', 'attn_sinks', 's5_a_re', 's5_a_im', 's5_b_re', 's5_b_im', 's5_c_re', 's5_c_im', 's5_d', 's5_log_dt',
           's5_glu_w', 's5_glu_b', 'lru_conv_w', 'lru_conv_b', 'lru_wx', 'lru_bx', 'lru_wa', 'lru_ba', 'lru_a_param', 'mix_norm_g',
           'w_out', 'b_out', 'ln1_g', 'ln1_b', 'ffn_w_gate', 'ffn_w_up', 'ffn_conv_w', 'ffn_conv_b', 'ffn_w_down', 'ln2_g', 'ln2_b']
SHARD_AXIS = {'w_in': 2, 's5_glu_w': 1, 'lru_conv_w': 2, 'w_out': 1, 'ffn_w_gate': 2, 'ffn_w_up': 2, 'ffn_conv_w': 2,
              'ffn_w_down': 1}
SHARDED = [n for n in WEIGHTS if n in SHARD_AXIS]
REPLICATED = [n for n in WEIGHTS if n not in SHARD_AXIS]
BIG = [(n, SHARD_AXIS[n] == 2) for n in ('w_in', 'w_out', 'ffn_w_gate', 'ffn_w_up', 'ffn_w_down')]
SMALL_SHARDED = [n for n in SHARDED if n not in dict(BIG)]
PARTS = {'mix': ['w_in', 'w_out'], 'ffn': ['ffn_w_gate', 'ffn_w_up', 'ffn_w_down']}
GATHER_F32 = ('lru_conv_w', 'ffn_conv_w')


def _dot(a, b):
    return jnp.dot(a, b, preferred_element_type=F32)


def _dot_nt(a, b):
    return lax.dot_general(a, b, (((1,), (1,)), ((), ())), preferred_element_type=F32)


def _dot_tn(a, b):
    return lax.dot_general(a, b, (((0,), (0,)), ((), ())), preferred_element_type=F32)


def _mx(a):
    return a.astype(MXU)


_GELU_C = math.sqrt(2.0 / math.pi)


def _gelu(x):
    th = jnp.tanh(_GELU_C * (x + 0.044715 * x * x * x))
    return 0.5 * x * (1.0 + th)


def _gelu_grad(x):
    th = jnp.tanh(_GELU_C * (x + 0.044715 * x * x * x))
    return 0.5 * (1.0 + th) + 0.5 * x * (1.0 - th * th) * _GELU_C * (1.0 + 3.0 * 0.044715 * x * x)


def _sigmoid(x):
    return 0.5 * jnp.tanh(0.5 * x) + 0.5


def _ln_stats(r):
    mu = jnp.mean(r, axis=-1, keepdims=True)
    xc = r - mu
    var = jnp.mean(xc * xc, axis=-1, keepdims=True)
    rstd = lax.rsqrt(var + LN_EPS)
    return xc * rstd, rstd


def _ln_bwd(dy, g, xhat, rstd):
    dxh = dy * g
    return rstd * (dxh - jnp.mean(dxh, axis=-1, keepdims=True) - xhat * jnp.mean(dxh * xhat, axis=-1, keepdims=True))


def _rms(y):
    return lax.rsqrt(jnp.mean(y * y, axis=-1, keepdims=True) + RMS_EPS)


def _sum0(a):
    return jnp.sum(a, axis=0, keepdims=True)


def _row_iota(shape):
    return lax.broadcasted_iota(jnp.int32, shape, 0)


def _shift_down(ext, j, rows):
    return pltpu.roll(ext, j, 0)[8:8 + rows]


def _shift_up(ext, j, rows):
    return pltpu.roll(ext, ext.shape[0] - j, 0)[:rows]


def _swap_halves(t):
    w = t.shape[1]
    lane = lax.broadcasted_iota(jnp.int32, t.shape, 1)
    return jnp.where((lane & 32) == 0, pltpu.roll(t, w - 32, 1), pltpu.roll(t, 32, 1))


def _rope(t, cos, sin_signed):
    return t * cos + _swap_halves(t) * sin_signed


def _rope_t(d, cos, sin_signed):
    return d * cos + _swap_halves(d * sin_signed)


def _cmul_add(ar, ai, xr, xi, yr, yi):
    return ar * xr - ai * xi + yr, ar * xi + ai * xr + yi


def _seg_rows(k):
    return slice(N_SEG * k, N_SEG * (k + 1))


def _permute_rows(perm, x):
    hi = _mx(x)
    rest = x - hi.astype(F32)
    mid = _mx(rest)
    lo = _mx(rest - mid.astype(F32))
    return _dot(perm, hi) + _dot(perm, mid) + _dot(perm, lo)


def _cscan(sr, si, tab, cin_r, cin_i, reverse):
    sgn = -1.0 if reverse else 1.0
    pw_re, pw_im, dbl_re, dbl_im = tab['pw_re'], tab['pw_im'], tab['dbl_re'], tab['dbl_im']
    ar, ai = pw_re[0:1, :], sgn * pw_im[0:1, :]
    shape = (N_SEG, sr.shape[1])
    hr = hi = jnp.zeros(shape, F32)
    for k in (range(SEG - 1, -1, -1) if reverse else range(SEG)):
        hr, hi = _cmul_add(ar, ai, hr, hi, sr[_seg_rows(k), :], si[_seg_rows(k), :])
        sr[_seg_rows(k), :] = hr
        si[_seg_rows(k), :] = hi
    sub = _row_iota(shape)

    def shifted(v, d):
        if reverse:
            return jnp.where(sub < N_SEG - d, pltpu.roll(v, N_SEG - d, 0), 0.0)
        return jnp.where(sub >= d, pltpu.roll(v, d, 0), 0.0)

    fr, fi = hr, hi
    for j, d in enumerate((1, 2, 4)):
        fr, fi = _cmul_add(dbl_re[j:j + 1, :], sgn * dbl_im[j:j + 1, :], shifted(fr, d), shifted(fi, d), fr, fi)
    seg_re, seg_im = (tab['segr_re'], tab['segr_im']) if reverse else (tab['seg_re'], tab['seg_im'])
    cr, ci = _cmul_add(seg_re[...], sgn * seg_im[...], cin_r, cin_i, shifted(fr, 1), shifted(fi, 1))
    nr, ni = _cmul_add(dbl_re[0:1, :], sgn * dbl_im[0:1, :], cr, ci, hr, hi)
    for k in range(SEG):
        j = SEG - 1 - k if reverse else k
        xr, xi = _cmul_add(pw_re[j:j + 1, :], sgn * pw_im[j:j + 1, :], cr, ci, sr[_seg_rows(k), :], si[_seg_rows(k), :])
        sr[_seg_rows(k), :] = xr
        si[_seg_rows(k), :] = xi
    edge = slice(0, 1) if reverse else slice(N_SEG - 1, N_SEG)
    return nr[edge], ni[edge]


def _rscan(a, b, reverse):
    rows = a.shape[0]
    row = _row_iota(a.shape)
    s = 1
    while s < rows:
        if reverse:
            keep = row < rows - s
            sa = jnp.where(keep, pltpu.roll(a, rows - s, 0), 1.0)
            sb = jnp.where(keep, pltpu.roll(b, rows - s, 0), 0.0)
        else:
            keep = row >= s
            sa = jnp.where(keep, pltpu.roll(a, s, 0), 1.0)
            sb = jnp.where(keep, pltpu.roll(b, s, 0), 0.0)
        b = b + a * sb
        a = a * sa
        s *= 2
    return a, b


def _whole():
    return pl.BlockSpec(memory_space=pltpu.VMEM)


def _rows_spec(rows, cols, n_tiles, reverse=False):
    if reverse:
        return pl.BlockSpec((rows, cols), lambda i: (n_tiles - 1 - i, 0))
    return pl.BlockSpec((rows, cols), lambda i: (i, 0))


def _halo_spec(cols, tile_rows, n_tiles, reverse=False):
    per = tile_rows // 8
    if reverse:
        return pl.BlockSpec((8, cols), lambda i: (jnp.maximum((n_tiles - 1 - i) * per - 1, 0), 0))
    return pl.BlockSpec((8, cols), lambda i: (jnp.maximum(i * per - 1, 0), 0))


def _call(body, name, n_tiles, in_specs, out_specs, out_shape, scratch=()):
    return pl.pallas_call(
        body, name=name, grid=(n_tiles,), in_specs=in_specs, out_specs=out_specs, out_shape=out_shape,
        scratch_shapes=list(scratch),
        compiler_params=pltpu.CompilerParams(dimension_semantics=("arbitrary",), vmem_limit_bytes=VMEM_MB << 20))


def _sds(shape, dtype=F32):
    return jax.ShapeDtypeStruct(shape, dtype)


def _inproj_fwd(x, w, b, cos_t, sin_t):
    n = x.shape[0]
    nt = n // TILE_WIDE

    def body(x_ref, w_ref, b_ref, c_ref, s_ref, q_ref, k_ref, v_ref, u_ref, xr_ref, g_ref):
        p = _dot(_mx(x_ref[...]), w_ref[...]) + b_ref[...]
        cos, sin = c_ref[...], s_ref[...]
        q_ref[...] = _mx(_rope(p[:, :D_ATTN], jnp.tile(cos, (1, 4)), jnp.tile(sin, (1, 4))))
        k_ref[...] = _mx(_rope(p[:, 512:640], cos, sin))
        v_ref[...] = _mx(p[:, 640:768])
        u_ref[...] = p[:, 768:1024]
        xr_ref[...] = p[:, 1024:1280]
        g_ref[...] = p[:, 1280:1536]

    r = functools.partial(_rows_spec, n_tiles=nt)
    return _call(
        body, "inproj_fwd", nt,
        [r(TILE_WIDE, D), _whole(), _whole(), r(TILE_WIDE, 128), r(TILE_WIDE, 128)],
        [r(TILE_WIDE, D_ATTN), r(TILE_WIDE, D_KV), r(TILE_WIDE, D_KV), r(TILE_WIDE, D_S5), r(TILE_WIDE, D_LRU), r(TILE_WIDE, D_LRU)],
        [_sds((n, D_ATTN), MXU), _sds((n, D_KV), MXU), _sds((n, D_KV), MXU), _sds((n, D_S5)), _sds((n, D_LRU)), _sds((n, D_LRU))],
    )(x, w, b, cos_t, sin_t)


def _inproj_bwd(dq, dk, dv, du, dxr, dgate, cos_t, sin_t, x0, dr1, w_t):
    n = x0.shape[0]
    nt = n // TILE_WIDE

    def body(dq_ref, dk_ref, dv_ref, du_ref, dxr_ref, dg_ref, c_ref, s_ref, x_ref, dr_ref, w_ref, dx_ref, dw_ref, db_ref):
        @pl.when(pl.program_id(0) == 0)
        def _():
            dw_ref[...] = jnp.zeros_like(dw_ref)
            db_ref[...] = jnp.zeros_like(db_ref)

        cos, sin = c_ref[...], s_ref[...]
        dtq = _rope_t(dq_ref[...], jnp.tile(cos, (1, 4)), jnp.tile(sin, (1, 4)))
        dtk = _rope_t(dk_ref[...], cos, sin)
        dp = jnp.concatenate([dtq, dtk, dv_ref[...], du_ref[...], dxr_ref[...], dg_ref[...]], axis=1)
        db_ref[...] += _sum0(dp)
        dpb = _mx(dp)
        dw_ref[...] += _dot_tn(dpb, _mx(x_ref[...]))
        dx_ref[...] = ALPHA * dr_ref[...] + _dot(dpb, w_ref[...])

    r = functools.partial(_rows_spec, n_tiles=nt)
    return _call(
        body, "inproj_bwd", nt,
        [r(TILE_WIDE, D_ATTN), r(TILE_WIDE, D_KV), r(TILE_WIDE, D_KV), r(TILE_WIDE, D_S5), r(TILE_WIDE, D_LRU), r(TILE_WIDE, D_LRU),
         r(TILE_WIDE, 128), r(TILE_WIDE, 128), r(TILE_WIDE, D), r(TILE_WIDE, D), _whole()],
        [r(TILE_WIDE, D), _whole(), _whole()],
        [_sds((n, D)), _sds((D_IN, D)), _sds((1, D_IN))],
    )(dq, dk, dv, du, dxr, dgate, cos_t, sin_t, x0, dr1, w_t)


def _kv_variants(t, lo):
    tr = pltpu.roll(t, 64, 1)
    out = []
    for j in range(2):
        first = jnp.where(lo, t if j == 0 else tr, 0.0)
        second = jnp.where(lo, 0.0, tr if j == 0 else t)
        out.append(_mx(jnp.concatenate([first, second], axis=0)))
    return out


def _kv_collect(x0, x1, lo):
    a = x0[:256] + pltpu.roll(x0[256:], 64, 1)
    b = pltpu.roll(x1[:256], 64, 1) + x1[256:]
    return jnp.where(lo, a, b)


def _row_sums(x):
    ones = jnp.ones((128, 128), MXU)
    hi = _mx(x)
    lo = _mx(x - hi.astype(F32))
    return _dot(hi, ones) + _dot(lo, ones)


def _attn_probs(s, sink_ref):
    out = []
    for hp in range(2):
        sh = s[:, hp * 128:(hp + 1) * 128]
        sink = sink_ref[hp]
        m = jnp.maximum(jnp.broadcast_to(jnp.max(sh, axis=1, keepdims=True), sh.shape), sink)
        p = jnp.exp(sh - m)
        es = jnp.exp(sink - m)
        inv = 1.0 / (_row_sums(p) + es)
        out.append((p * inv, es * inv))
    return out


def _band_merge(x, tri, no_previous=None):
    bands = []
    for hp in range(2):
        prev, own = x[:, hp * 256:hp * 256 + 128], x[:, hp * 256 + 128:hp * 256 + 256]
        if no_previous is not None:
            prev = jnp.where(no_previous, -jnp.inf, prev)
        bands.append(jnp.where(tri, own, prev))
    return jnp.concatenate(bands, axis=1)


def _band_split(y, tri):
    parts = []
    for hp in range(2):
        band = y[:, hp * 128:(hp + 1) * 128]
        parts += [jnp.where(tri, 0.0, band), jnp.where(tri, band, 0.0)]
    return jnp.concatenate(parts, axis=1)


def _tri():
    shape = (ATTN_BLOCK, ATTN_BLOCK)
    return lax.broadcasted_iota(jnp.int32, shape, 0) >= lax.broadcasted_iota(jnp.int32, shape, 1)


def _attn_scores(q_ref, k_ref, v_ref, nb, tri):
    lo = lax.broadcasted_iota(jnp.int32, (256, 128), 1) < 64
    kcats, vcats, kstarts, parts = [], [], [], []
    for b in range(nb):
        block = pl.program_id(0) * nb + b
        kstart = pl.multiple_of(block * ATTN_BLOCK, ATTN_BLOCK)
        kcat = _kv_variants(k_ref[pl.ds(kstart, 256), :].astype(F32), lo)
        kcats.append(kcat)
        vcats.append(_kv_variants(v_ref[pl.ds(kstart, 256), :].astype(F32), lo))
        kstarts.append(kstart)
        for j in range(2):
            s = _dot_nt(_kv_group(q_ref, b, j), kcat[j]) * 0.125
            parts += [_band_merge(s[:ATTN_BLOCK], tri, block == 0), _band_merge(s[ATTN_BLOCK:], tri, block == 0)]
    return jnp.concatenate(parts, axis=0), kcats, vcats, kstarts


def _kv_group(a, b, j):
    rows = slice(b * ATTN_BLOCK, (b + 1) * ATTN_BLOCK)
    return jnp.concatenate([a[rows, 2 * j * 128:(2 * j + 1) * 128], a[rows, (2 * j + 1) * 128:(2 * j + 2) * 128]], axis=0)


def _put_kv_group(ref, b, j, x):
    rows = slice(b * ATTN_BLOCK, (b + 1) * ATTN_BLOCK)
    ref[rows, 2 * j * 128:(2 * j + 1) * 128] = x[:ATTN_BLOCK]
    ref[rows, (2 * j + 1) * 128:(2 * j + 2) * 128] = x[ATTN_BLOCK:]


def _band_split_group(y, unit, tri):
    return _mx(jnp.concatenate([_band_split(y[unit:unit + ATTN_BLOCK], tri),
                                _band_split(y[unit + ATTN_BLOCK:unit + 2 * ATTN_BLOCK], tri)], axis=0))


def _attn_fwd(q, k, v, sink_cols):
    n = q.shape[0]
    nt = n // TILE_Q
    nb = TILE_Q // ATTN_BLOCK

    def body(q_ref, k_ref, v_ref, s_ref, o_ref):
        tri = _tri()
        s, _, vcats, _ = _attn_scores(q_ref, k_ref, v_ref, nb, tri)
        (p0, _), (p1, _) = _attn_probs(s, s_ref)
        p = jnp.concatenate([p0, p1], axis=1)
        for b in range(nb):
            for j in range(2):
                _put_kv_group(o_ref, b, j, _dot(_band_split_group(p, (b * 4 + 2 * j) * ATTN_BLOCK, tri), vcats[b][j]))

    return _call(
        body, "attn_fwd", nt,
        [_rows_spec(TILE_Q, D_ATTN, nt), _whole(), _whole(), _whole()],
        _rows_spec(TILE_Q, D_ATTN, nt), _sds((n, D_ATTN)),
    )(q, k, v, sink_cols)


def _attn_bwd(q, k, v, sink_cols, o, do):
    n = q.shape[0]
    nt = n // TILE_Q
    nb = TILE_Q // ATTN_BLOCK

    def body(q_ref, k_ref, v_ref, s_ref, o_ref, do_ref, dq_ref, dk_ref, dv_ref, ds_ref):
        @pl.when(pl.program_id(0) == 0)
        def _():
            dk_ref[...] = jnp.zeros_like(dk_ref)
            dv_ref[...] = jnp.zeros_like(dv_ref)
            ds_ref[...] = jnp.zeros_like(ds_ref)

        lo = lax.broadcasted_iota(jnp.int32, (256, 128), 1) < 64
        tri = _tri()
        s, kcats, vcats, kstarts = _attn_scores(q_ref, k_ref, v_ref, nb, tri)
        probs = _attn_probs(s, s_ref)
        do = do_ref[...]
        dob = _mx(do)
        od = do * o_ref[...]
        lo_q = (lax.broadcasted_iota(jnp.int32, od.shape, 1) & 64) == 0
        od_head = (jnp.where(lo_q, od, 0.0), jnp.where(lo_q, 0.0, od))
        units = [(b, i) for b in range(nb) for i in range(4)]

        def tile_part(a, b, i):
            return a[b * ATTN_BLOCK:(b + 1) * ATTN_BLOCK, i * 128:(i + 1) * 128]

        dp = []
        for b in range(nb):
            for j in range(2):
                x = _dot_nt(_kv_group(dob, b, j), vcats[b][j])
                dp += [_band_merge(x[:ATTN_BLOCK], tri), _band_merge(x[ATTN_BLOCK:], tri)]
        dp = jnp.concatenate(dp, axis=0)
        ds = []
        for hp in range(2):
            p, p_sink = probs[hp]
            delta = _row_sums(jnp.concatenate([tile_part(od_head[hp], b, i) for b, i in units], axis=0))
            ds.append(p * (dp[:, hp * 128:(hp + 1) * 128] - delta) * 0.125)
            t = p_sink * delta
            for i in range(4):
                ds_ref[2 * i + hp:2 * i + hp + 1, :] -= sum(
                    _sum0(t[(b * 4 + i) * ATTN_BLOCK:(b * 4 + i + 1) * ATTN_BLOCK]) for b in range(nb))
        ds = jnp.concatenate(ds, axis=1)
        p = jnp.concatenate([probs[0][0], probs[1][0]], axis=1)
        for b in range(nb):
            dkc, dvc = [], []
            for j in range(2):
                unit = (b * 4 + 2 * j) * ATTN_BLOCK
                dsb = _band_split_group(ds, unit, tri)
                _put_kv_group(dq_ref, b, j, _dot(dsb, kcats[b][j]))
                dkc.append(_dot_tn(dsb, _kv_group(q_ref, b, j)))
                dvc.append(_dot_tn(_band_split_group(p, unit, tri), _kv_group(dob, b, j)))
            dk_ref[pl.ds(kstarts[b], 256), :] += _kv_collect(dkc[0], dkc[1], lo)
            dv_ref[pl.ds(kstarts[b], 256), :] += _kv_collect(dvc[0], dvc[1], lo)

    r = _rows_spec(TILE_Q, D_ATTN, nt)
    return _call(
        body, "attn_bwd", nt,
        [r, _whole(), _whole(), _whole(), r, r],
        [r, _whole(), _whole(), _whole()],
        [_sds((n, D_ATTN)), _sds((n + ATTN_BLOCK, D_KV)), _sds((n + ATTN_BLOCK, D_KV)), _sds((8, 128))],
    )(q, k, v, sink_cols, o, do)


S5_DISC = ('s5_a_re', 's5_a_im', 's5_log_dt', 's5_b_re', 's5_b_im')
S5_TABLES = ('pw_re', 'pw_im', 'dbl_re', 'dbl_im', 'seg_re', 'seg_im', 'segr_re', 'segr_im')
S5_WEIGHTS = ('b_re', 'b_im', 'c_re', 'c_im', 'd', 'glu_w', 'glu_b', 'perm', 'perm_t')


def _s5_states(u, carry_r, carry_i, b_re, b_im, tab, hr_s, hi_s):
    ub = _mx(u)
    hr_s[...] = _dot(ub, b_re[...])
    hi_s[...] = _dot(ub, b_im[...])
    return ub, _cscan(hr_s, hi_s, tab, carry_r, carry_i, reverse=False)


def _s5_fwd(u, prm):
    n = u.shape[0]
    nt = n // (S5_PAIR * TILE)

    def body(u_ref, *refs):
        tab = dict(zip(S5_TABLES, refs[:8]))
        b_re, b_im, c_re, c_im, d_ref, gw_ref, gb_ref, perm, perm_t = refs[8:17]
        y_ref, cr_out, ci_out, cr_s, ci_s = refs[17:22]
        states = [refs[22 + 2 * j:24 + 2 * j] for j in range(S5_PAIR)]
        rows = [slice(j * TILE, (j + 1) * TILE) for j in range(S5_PAIR)]

        @pl.when(pl.program_id(0) == 0)
        def _():
            cr_s[...] = jnp.zeros_like(cr_s)
            ci_s[...] = jnp.zeros_like(ci_s)

        us = []
        for j in range(S5_PAIR):
            us.append(_permute_rows(perm[...], u_ref[rows[j], :]))
            ub = _mx(us[j])
            states[j][0][...] = _dot(ub, b_re[...])
            states[j][1][...] = _dot(ub, b_im[...])
        for j in range(S5_PAIR):
            cr, ci = cr_s[...], ci_s[...]
            cr_out[8 * j:8 * j + 8, :] = jnp.broadcast_to(cr, (8, N_STATE))
            ci_out[8 * j:8 * j + 8, :] = jnp.broadcast_to(ci, (8, N_STATE))
            cr_s[...], ci_s[...] = _cscan(*states[j], tab, cr, ci, reverse=False)
        for j in range(S5_PAIR):
            hr_s, hi_s = states[j]
            y = _dot(_mx(hr_s[...]), c_re[...]) - _dot(_mx(hi_s[...]), c_im[...]) + d_ref[...] * us[j]
            z = _gelu(y)
            y_ref[rows[j], :] = _permute_rows(perm_t[...], z * _sigmoid(_dot(_mx(z), gw_ref[...]) + gb_ref[...]))

    r = functools.partial(_rows_spec, n_tiles=nt)
    return _call(
        body, "s5_fwd", nt,
        [r(S5_PAIR * TILE, D_S5)] + [_whole()] * 17,
        [r(S5_PAIR * TILE, D_S5), r(S5_PAIR * 8, N_STATE), r(S5_PAIR * 8, N_STATE)],
        [_sds((n, D_S5)), _sds((n // TILE * 8, N_STATE)), _sds((n // TILE * 8, N_STATE))],
        scratch=[pltpu.VMEM((1, N_STATE), F32)] * 2 + [pltpu.VMEM((TILE, N_STATE), F32)] * (2 * S5_PAIR),
    )(u, *[prm[k] for k in S5_TABLES + S5_WEIGHTS])


def _s5_bwd(u, dys, carry_re, carry_im, prm):
    n = u.shape[0]
    nt = n // (S5_PAIR * TILE)

    def body(u_ref, dy_ref, cin_r, cin_i, *refs):
        tab = dict(zip(S5_TABLES, refs[:8]))
        b_re, b_im, c_re, c_im, d_ref, gw_ref, gb_ref, perm, perm_t = refs[8:17]
        du_ref, dbr_ref, dbi_ref, dcr_ref, dci_ref, dar_ref, dai_ref, dd_ref, dgw_ref, dgb_ref = refs[17:27]
        gr_s, gi_s = refs[27:29]
        scratch = [refs[29 + 4 * j:33 + 4 * j] for j in range(S5_PAIR)]
        later_first = list(reversed(range(S5_PAIR)))
        rows = [slice(j * TILE, (j + 1) * TILE) for j in range(S5_PAIR)]

        @pl.when(pl.program_id(0) == 0)
        def _():
            for ref in (dbr_ref, dbi_ref, dcr_ref, dci_ref, dar_ref, dai_ref, dd_ref, dgw_ref, dgb_ref, gr_s, gi_s):
                ref[...] = jnp.zeros_like(ref)

        us, ubs, carries, dy_of = {}, {}, {}, {}
        for j in later_first:
            us[j] = _permute_rows(perm[...], u_ref[rows[j], :])
            carries[j] = (cin_r[8 * j:8 * j + 1, :], cin_i[8 * j:8 * j + 1, :])
            ubs[j], _ = _s5_states(us[j], *carries[j], b_re, b_im, tab, *scratch[j][:2])
        for j in later_first:
            u = us[j]
            hr_s, hi_s, gr_t, gi_t = scratch[j]
            hrb, hib = _mx(hr_s[...]), _mx(hi_s[...])
            y = _dot(hrb, c_re[...]) - _dot(hib, c_im[...]) + d_ref[...] * u
            z = _gelu(y)
            zb = _mx(z)
            sg = _sigmoid(_dot(zb, gw_ref[...]) + gb_ref[...])
            dout = _permute_rows(perm[...], dy_ref[rows[j], :])
            dpre = dout * z * sg * (1.0 - sg)
            dgb_ref[...] += _sum0(dpre)
            dpb = _mx(dpre)
            dgw_ref[...] += _dot_tn(zb, dpb)
            dy = (dout * sg + _dot_nt(dpb, gw_ref[...])) * _gelu_grad(y)
            dd_ref[...] += _sum0(dy * u)
            dyb = _mx(dy)
            dcr_ref[...] += _dot_tn(hrb, dyb)
            dci_ref[...] -= _dot_tn(hib, dyb)
            gr_t[...] = _dot_nt(dyb, c_re[...])
            gi_t[...] = -_dot_nt(dyb, c_im[...])
            dy_of[j] = dy
        for j in later_first:
            gr_s[...], gi_s[...] = _cscan(*scratch[j][2:], tab, gr_s[...], gi_s[...], reverse=True)
        for j in later_first:
            hr_s, hi_s, gr_t, gi_t = scratch[j]
            cr, ci = carries[j]
            sub = _row_iota((N_SEG, N_STATE))
            acc_r = acc_i = jnp.zeros((N_SEG, N_STATE), F32)
            for k in range(SEG):
                if k == 0:
                    hpr = jnp.where(sub >= 1, pltpu.roll(hr_s[_seg_rows(SEG - 1), :], 1, 0), cr)
                    hpi = jnp.where(sub >= 1, pltpu.roll(hi_s[_seg_rows(SEG - 1), :], 1, 0), ci)
                else:
                    hpr, hpi = hr_s[_seg_rows(k - 1), :], hi_s[_seg_rows(k - 1), :]
                gr, gi = gr_t[_seg_rows(k), :], gi_t[_seg_rows(k), :]
                acc_r = acc_r + gr * hpr + gi * hpi
                acc_i = acc_i + gi * hpr - gr * hpi
            dar_ref[...] += _sum0(acc_r)
            dai_ref[...] += _sum0(acc_i)
            grb, gib = _mx(gr_t[...]), _mx(gi_t[...])
            dbr_ref[...] += _dot_tn(ubs[j], grb)
            dbi_ref[...] += _dot_tn(ubs[j], gib)
            du_ref[rows[j], :] = _permute_rows(perm_t[...], dy_of[j] * d_ref[...] + _dot_nt(grb, b_re[...]) + _dot_nt(gib, b_im[...]))

    r = functools.partial(_rows_spec, n_tiles=nt, reverse=True)
    return _call(
        body, "s5_bwd", nt,
        [r(S5_PAIR * TILE, D_S5), r(S5_PAIR * TILE, D_S5), r(S5_PAIR * 8, N_STATE), r(S5_PAIR * 8, N_STATE)] + [_whole()] * 17,
        [r(S5_PAIR * TILE, D_S5)] + [_whole()] * 9,
        [_sds((n, D_S5)), _sds((D_S5, N_STATE)), _sds((D_S5, N_STATE)), _sds((N_STATE, D_S5)), _sds((N_STATE, D_S5)),
         _sds((1, N_STATE)), _sds((1, N_STATE)), _sds((1, D_S5)), _sds((D_S5, D_S5)), _sds((1, D_S5))],
        scratch=[pltpu.VMEM((1, N_STATE), F32)] * 2 + [pltpu.VMEM((TILE, N_STATE), F32)] * (4 * S5_PAIR),
    )(u, dys, carry_re, carry_im, *[prm[k] for k in S5_TABLES + S5_WEIGHTS])


def _lru_gates(xr, halo, tile_index, cw_ref, cb_ref, wx_ref, wa_ref, bx_ref, ba_ref, sp_ref):
    ext = jnp.concatenate([halo, xr], axis=0)
    sh = [xr] + [_shift_down(ext, j, TILE) for j in (1, 2, 3)]
    xc = cb_ref[...] + cw_ref[3:4, :] * sh[0] + cw_ref[2:3, :] * sh[1] + cw_ref[1:2, :] * sh[2] + cw_ref[0:1, :] * sh[3]
    xb = _mx(xc)
    gx = _sigmoid(_dot(xb, wx_ref[...]) + bx_ref[...])
    ga = _sigmoid(_dot(xb, wa_ref[...]) + ba_ref[...])
    la = -LRU_C * ga * sp_ref[...]
    a = jnp.exp(la)
    start = (tile_index * TILE + _row_iota(xr.shape)) == 0
    mult = jnp.where(start, 1.0, jnp.sqrt(-jnp.tanh(la) * (a * a + 1.0)))
    return sh, xc, xb, gx, ga, a, mult, start


def _lru_fwd(xr, gate, prm):
    n = xr.shape[0]
    nt = n // TILE

    def body(x_ref, g_ref, cw_ref, cb_ref, wx_ref, wa_ref, bx_ref, ba_ref, sp_ref, y_ref, c_out, halo_s, c_s):
        first_tile = pl.program_id(0) == 0

        @pl.when(first_tile)
        def _():
            halo_s[...] = jnp.zeros_like(halo_s)
            c_s[...] = jnp.zeros_like(c_s)

        xr = x_ref[...]
        _, xc, _, gx, _, a, mult, _ = _lru_gates(xr, halo_s[...], pl.program_id(0), cw_ref, cb_ref, wx_ref, wa_ref, bx_ref, ba_ref,
                                                 sp_ref)
        halo_s[...] = xr[TILE - 8:]
        acum, h = _rscan(a, mult * gx * xc, reverse=False)
        c = c_s[...]
        c_out[...] = jnp.broadcast_to(c, (8, D_LRU))
        h = h + acum * c
        c_s[...] = h[TILE - 1:TILE]
        y_ref[...] = h * _gelu(g_ref[...])

    r = functools.partial(_rows_spec, n_tiles=nt)
    return _call(
        body, "lru_fwd", nt,
        [r(TILE, D_LRU), r(TILE, D_LRU)] + [_whole()] * 7,
        [r(TILE, D_LRU), r(8, D_LRU)],
        [_sds((n, D_LRU)), _sds((nt * 8, D_LRU))],
        scratch=[pltpu.VMEM((8, D_LRU), F32), pltpu.VMEM((1, D_LRU), F32)],
    )(xr, gate, prm['conv_w'], prm['conv_b'], prm['wx'], prm['wa'], prm['bx'], prm['ba'], prm['sp'])


def _lru_bwd(xr, gate, dyl, carry, prm):
    n = xr.shape[0]
    nt = n // TILE

    def body(x_ref, xh_ref, g_ref, dy_ref, cin_ref, cw_ref, cb_ref, wx_ref, wa_ref, bx_ref, ba_ref, sp_ref,
             dx_ref, dg_ref, dcw0, dcw1, dcw2, dcw3, dcb_ref, dwx_ref, dwa_ref, dbx_ref, dba_ref, dsp_ref, an_s, gn_s, dn_s):
        first_tile = pl.program_id(0) == nt - 1

        @pl.when(pl.program_id(0) == 0)
        def _():
            for ref in (dcw0, dcw1, dcw2, dcw3, dcb_ref, dwx_ref, dwa_ref, dbx_ref, dba_ref, dsp_ref, gn_s, dn_s):
                ref[...] = jnp.zeros_like(ref)
            an_s[...] = jnp.ones_like(an_s)

        xr = x_ref[...]
        halo = jnp.where(first_tile, 0.0, xh_ref[...])
        sh, xc, xb, gx, ga, a, mult, start = _lru_gates(xr, halo, nt - 1 - pl.program_id(0), cw_ref, cb_ref, wx_ref, wa_ref, bx_ref,
                                                        ba_ref, sp_ref)
        acum, h = _rscan(a, mult * gx * xc, reverse=False)
        cin = cin_ref[0:1, :]
        h = h + acum * cin
        gate = g_ref[...]
        dyl = dy_ref[...]
        dg_ref[...] = dyl * h * _gelu_grad(gate)
        row = _row_iota(xr.shape)
        alpha = jnp.where(row < TILE - 1, pltpu.roll(a, TILE - 1, 0), an_s[...])
        racc, g = _rscan(alpha, dyl * _gelu(gate), reverse=True)
        g = g + racc * gn_s[...]
        an_s[...] = a[0:1]
        gn_s[...] = g[0:1]
        hprev = jnp.where(row == 0, cin, pltpu.roll(h, 1, 0))
        da = g * hprev
        dmult = jnp.where(start, 0.0, g * gx * xc)
        dla = da * a - dmult * a * a / mult
        dsp_ref[...] += _sum0(-LRU_C * ga * dla)
        dpa = (-LRU_C * sp_ref[...] * dla) * ga * (1.0 - ga)
        dpx = (g * mult * xc) * gx * (1.0 - gx)
        dba_ref[...] += _sum0(dpa)
        dbx_ref[...] += _sum0(dpx)
        dpab, dpxb = _mx(dpa), _mx(dpx)
        dwa_ref[...] += _dot_tn(xb, dpab)
        dwx_ref[...] += _dot_tn(xb, dpxb)
        dxc = g * mult * gx + _dot_nt(dpab, wa_ref[...]) + _dot_nt(dpxb, wx_ref[...])
        dcb_ref[...] += _sum0(dxc)
        dcw3[...] += _sum0(dxc * sh[0])
        dcw2[...] += _sum0(dxc * sh[1])
        dcw1[...] += _sum0(dxc * sh[2])
        dcw0[...] += _sum0(dxc * sh[3])
        ext = jnp.concatenate([dxc, dn_s[...]], axis=0)
        dx_ref[...] = (cw_ref[3:4, :] * dxc + cw_ref[2:3, :] * _shift_up(ext, 1, TILE) + cw_ref[1:2, :] * _shift_up(ext, 2, TILE)
                       + cw_ref[0:1, :] * _shift_up(ext, 3, TILE))
        dn_s[...] = dxc[:8]

    r = functools.partial(_rows_spec, n_tiles=nt, reverse=True)
    vec = _sds((1, D_LRU))
    return _call(
        body, "lru_bwd", nt,
        [r(TILE, D_LRU), _halo_spec(D_LRU, TILE, nt, reverse=True), r(TILE, D_LRU), r(TILE, D_LRU), r(8, D_LRU)] + [_whole()] * 7,
        [r(TILE, D_LRU), r(TILE, D_LRU)] + [_whole()] * 10,
        [_sds((n, D_LRU)), _sds((n, D_LRU)), vec, vec, vec, vec, vec, _sds((D_LRU, D_LRU)), _sds((D_LRU, D_LRU)), vec, vec, vec],
        scratch=[pltpu.VMEM((1, D_LRU), F32), pltpu.VMEM((1, D_LRU), F32), pltpu.VMEM((8, D_LRU), F32)],
    )(xr, xr, gate, dyl, carry, prm['conv_w'], prm['conv_b'], prm['wx'], prm['wa'], prm['bx'], prm['ba'], prm['sp'])


def _normed_parts(ya, ys, yl):
    return jnp.concatenate([ya * _rms(ya), ys * _rms(ys), yl * _rms(yl)], axis=1)


def _mixout_fwd(ya, ys, yl, x0, g_mix, w_out, b_out, g1, b1):
    n = x0.shape[0]
    nt = n // TILE_WIDE

    def body(ya_ref, ys_ref, yl_ref, x_ref, gm_ref, w_ref, b_ref, g_ref, be_ref, mix_ref, r_ref, x1_ref):
        mixb = _mx(_normed_parts(ya_ref[...], ys_ref[...], yl_ref[...]) * gm_ref[...])
        mix_ref[...] = mixb
        r1 = ALPHA * x_ref[...] + _dot(mixb, w_ref[...]) + b_ref[...]
        r_ref[...] = r1
        xhat, _ = _ln_stats(r1)
        x1_ref[...] = xhat * g_ref[...] + be_ref[...]

    r = functools.partial(_rows_spec, n_tiles=nt)
    return _call(
        body, "mixout_fwd", nt,
        [r(TILE_WIDE, D_ATTN), r(TILE_WIDE, D_S5), r(TILE_WIDE, D_LRU), r(TILE_WIDE, D)] + [_whole()] * 5,
        [r(TILE_WIDE, D), r(TILE_WIDE, D), r(TILE_WIDE, D)],
        [_sds((n, D), MXU), _sds((n, D)), _sds((n, D))],
    )(ya, ys, yl, x0, g_mix, w_out, b_out, g1, b1)


def _mixout_bwd(dr1, mix, ya, ys, yl, g_mix, w_out):
    n = dr1.shape[0]
    nt = n // TILE_WIDE

    def body(dr_ref, mix_ref, ya_ref, ys_ref, yl_ref, gm_ref, w_ref, dya_ref, dys_ref, dyl_ref, dw_ref, db_ref, dgm_ref):
        @pl.when(pl.program_id(0) == 0)
        def _():
            for ref in (dw_ref, db_ref, dgm_ref):
                ref[...] = jnp.zeros_like(ref)

        dr = dr_ref[...]
        db_ref[...] += _sum0(dr)
        drb = _mx(dr)
        dw_ref[...] += _dot_tn(mix_ref[...], drb)
        dmix = _dot(drb, w_ref[...])
        parts = (ya_ref[...], ys_ref[...], yl_ref[...])
        dgm_ref[...] += _sum0(dmix * _normed_parts(*parts))
        dn = dmix * gm_ref[...]
        lo = 0
        for y, out in zip(parts, (dya_ref, dys_ref, dyl_ref)):
            w = y.shape[1]
            rs = _rms(y)
            nrm = y * rs
            dnp = dn[:, lo:lo + w]
            out[...] = rs * (dnp - nrm * jnp.mean(dnp * nrm, axis=-1, keepdims=True))
            lo += w

    r = functools.partial(_rows_spec, n_tiles=nt)
    return _call(
        body, "mixout_bwd", nt,
        [r(TILE_WIDE, D), r(TILE_WIDE, D), r(TILE_WIDE, D_ATTN), r(TILE_WIDE, D_S5), r(TILE_WIDE, D_LRU), _whole(), _whole()],
        [r(TILE_WIDE, D_ATTN), r(TILE_WIDE, D_S5), r(TILE_WIDE, D_LRU), _whole(), _whole(), _whole()],
        [_sds((n, D_ATTN)), _sds((n, D_S5)), _sds((n, D_LRU)), _sds((D, D)), _sds((1, D)), _sds((1, D))],
    )(dr1, mix, ya, ys, yl, g_mix, w_out)


def _ffn_conv(gp, halo, cw_ref, cb_ref, cs):
    ext = jnp.concatenate([halo, gp], axis=0)
    s1 = _shift_down(ext, 1, TILE)
    s2 = _shift_down(ext, 2, TILE)
    return s1, s2, cb_ref[:, cs] + cw_ref[2:3, cs] * gp + cw_ref[1:2, cs] * s1 + cw_ref[0:1, cs] * s2


def _ffn_fwd(x1, wg, wu, cw, cb, wd, g2, b2):
    n = x1.shape[0]
    nt = n // TILE

    def body(x_ref, wg_ref, wu_ref, cw_ref, cb_ref, wd_ref, g_ref, be_ref, gp_ref, up_ref, r_ref, x2_ref, halo_s, act_s):
        @pl.when(pl.program_id(0) == 0)
        def _():
            halo_s[...] = jnp.zeros_like(halo_s)

        x1 = x_ref[...]
        xb = _mx(x1)
        for c in range(D_FF // FF_CHUNK):
            cs = slice(c * FF_CHUNK, (c + 1) * FF_CHUNK)
            gp = _dot(xb, wg_ref[:, cs])
            up = _dot(xb, wu_ref[:, cs])
            gp_ref[:, cs] = gp
            up_ref[:, cs] = up
            _, _, gc = _ffn_conv(gp, halo_s[:, cs], cw_ref, cb_ref, cs)
            halo_s[:, cs] = gp[TILE - 8:]
            act_s[:, cs] = _mx(gc * _sigmoid(gc) * up)
        r2 = ALPHA * x1 + _dot(act_s[...], wd_ref[...])
        r_ref[...] = r2
        xhat, _ = _ln_stats(r2)
        x2_ref[...] = xhat * g_ref[...] + be_ref[...]

    r = functools.partial(_rows_spec, n_tiles=nt)
    return _call(
        body, "ffn_fwd", nt,
        [r(TILE, D)] + [_whole()] * 7,
        [r(TILE, D_FF), r(TILE, D_FF), r(TILE, D), r(TILE, D)],
        [_sds((n, D_FF)), _sds((n, D_FF)), _sds((n, D)), _sds((n, D))],
        scratch=[pltpu.VMEM((8, D_FF), F32), pltpu.VMEM((TILE, D_FF), MXU)],
    )(x1, wg, wu, cw, cb, wd, g2, b2)


def _ffn_bwd_down(dx2, r2, g2, gp, up, cw, cb, wd_t):
    n = dx2.shape[0]
    nt = n // TILE

    def body(dx_ref, r_ref, g_ref, gp_ref, gh_ref, up_ref, cw_ref, cb_ref, wd_ref,
             dr_ref, dgp_ref, dup_ref, dwd_ref, dcw0, dcw1, dcw2, dcb_ref, dg_ref, db_ref, next_s):
        first_tile = pl.program_id(0) == nt - 1

        @pl.when(pl.program_id(0) == 0)
        def _():
            for ref in (dwd_ref, dcw0, dcw1, dcw2, dcb_ref, dg_ref, db_ref, next_s):
                ref[...] = jnp.zeros_like(ref)

        dx2 = dx_ref[...]
        xhat, rstd = _ln_stats(r_ref[...])
        dg_ref[...] += _sum0(dx2 * xhat)
        db_ref[...] += _sum0(dx2)
        dr2 = _ln_bwd(dx2, g_ref[...], xhat, rstd)
        dr_ref[...] = dr2
        dfb = _mx(dr2)
        for c in range(D_FF // FF_CHUNK):
            cs = slice(c * FF_CHUNK, (c + 1) * FF_CHUNK)
            gp = gp_ref[:, cs]
            up = up_ref[:, cs]
            s1, s2, gc = _ffn_conv(gp, jnp.where(first_tile, 0.0, gh_ref[:, cs]), cw_ref, cb_ref, cs)
            sg = _sigmoid(gc)
            silu = gc * sg
            dact = _dot(dfb, wd_ref[:, cs])
            dwd_ref[cs, :] += _dot_tn(_mx(silu * up), dfb)
            dup_ref[:, cs] = _mx(dact * silu)
            dgc = dact * up * (sg + silu * (1.0 - sg))
            dcb_ref[:, cs] += _sum0(dgc)
            dcw2[:, cs] += _sum0(dgc * gp)
            dcw1[:, cs] += _sum0(dgc * s1)
            dcw0[:, cs] += _sum0(dgc * s2)
            ext = jnp.concatenate([dgc, next_s[:, cs]], axis=0)
            dgp_ref[:, cs] = _mx(cw_ref[2:3, cs] * dgc + cw_ref[1:2, cs] * _shift_up(ext, 1, TILE)
                                 + cw_ref[0:1, cs] * _shift_up(ext, 2, TILE))
            next_s[:, cs] = dgc[:8]

    r = functools.partial(_rows_spec, n_tiles=nt, reverse=True)
    vff = _sds((1, D_FF))
    return _call(
        body, "ffn_bwd_down", nt,
        [r(TILE, D), r(TILE, D), _whole(), r(TILE, D_FF), _halo_spec(D_FF, TILE, nt, reverse=True), r(TILE, D_FF), _whole(), _whole(),
         _whole()],
        [r(TILE, D), r(TILE, D_FF), r(TILE, D_FF)] + [_whole()] * 7,
        [_sds((n, D)), _sds((n, D_FF), MXU), _sds((n, D_FF), MXU), _sds((D_FF, D)), vff, vff, vff, vff, _sds((1, D)), _sds((1, D))],
        scratch=[pltpu.VMEM((8, D_FF), F32)],
    )(dx2, r2, g2, gp, gp, up, cw, cb, wd_t)


def _ffn_bwd_dx(dr2, dgp, dup, r1, g1, wg_t, wu_t):
    n = dr2.shape[0]
    rows = TILE_BIG
    nt = n // rows

    def body(dr2_ref, dgp_ref, dup_ref, r_ref, g_ref, wg_ref, wu_ref, dr1_ref, dg_ref, db_ref):
        @pl.when(pl.program_id(0) == 0)
        def _():
            for ref in (dg_ref, db_ref):
                ref[...] = jnp.zeros_like(ref)

        dx1 = ALPHA * dr2_ref[...] + _dot(dgp_ref[...], wg_ref[...]) + _dot(dup_ref[...], wu_ref[...])
        xhat, rstd = _ln_stats(r_ref[...])
        dg_ref[...] += _sum0(dx1 * xhat)
        db_ref[...] += _sum0(dx1)
        dr1_ref[...] = _ln_bwd(dx1, g_ref[...], xhat, rstd)

    r = functools.partial(_rows_spec, n_tiles=nt)
    return _call(
        body, "ffn_bwd_dx", nt,
        [r(rows, D), r(rows, D_FF), r(rows, D_FF), r(rows, D), _whole(), _whole(), _whole()],
        [r(rows, D), _whole(), _whole()],
        [_sds((n, D)), _sds((1, D)), _sds((1, D))],
    )(dr2, dgp, dup, r1, g1, wg_t, wu_t)


def _ffn_bwd_dw(x1, dgp, dup):
    n = x1.shape[0]
    rows = TILE_BIG
    nt = n // rows

    def body(x_ref, dgp_ref, dup_ref, dwg_ref, dwu_ref):
        @pl.when(pl.program_id(0) == 0)
        def _():
            for ref in (dwg_ref, dwu_ref):
                ref[...] = jnp.zeros_like(ref)

        xb = _mx(x_ref[...])
        for c in range(D_FF // FF_CHUNK):
            cs = slice(c * FF_CHUNK, (c + 1) * FF_CHUNK)
            dwg_ref[cs, :] += _dot_tn(dgp_ref[:, cs], xb)
            dwu_ref[cs, :] += _dot_tn(dup_ref[:, cs], xb)

    r = functools.partial(_rows_spec, n_tiles=nt)
    return _call(
        body, "ffn_bwd_dw", nt,
        [r(rows, D), r(rows, D_FF), r(rows, D_FF)], [_whole(), _whole()], [_sds((D_FF, D)), _sds((D_FF, D))],
    )(x1, dgp, dup)


def _loss_head(y, target):
    n = y.shape[0]
    nt = n // TILE_WIDE

    def body(y_ref, t_ref, loss_ref, dy_ref):
        @pl.when(pl.program_id(0) == 0)
        def _():
            loss_ref[...] = jnp.zeros_like(loss_ref)

        e = y_ref[...] - t_ref[...]
        dy_ref[...] = e * (1.0 / D)
        loss_ref[...] += _sum0(jnp.sum(e * e, axis=1, keepdims=True)) * (0.5 / D)

    r = functools.partial(_rows_spec, n_tiles=nt)
    return _call(body, "loss_head", nt, [r(TILE_WIDE, D), r(TILE_WIDE, D)], [_whole(), r(TILE_WIDE, D)],
                 [_sds((1, 1)), _sds((n, D))])(y, target)


def _place():
    x, y, c = lax.axis_index("x"), lax.axis_index("y"), lax.axis_index("c")
    return x, y, c, 4 * x + 2 * y + c


def _peer(x, y, c, k):
    px, py, pc = x ^ ((k >> 2) & 1), y ^ ((k >> 1) & 1), c ^ (k & 1)
    return (px, py, pc), 4 * px + 2 * py + pc


def _all_gather(name, blocks, small):
    srcs = list(blocks) + [small]
    n = len(srcs)
    out_shapes = [_sds((a.shape[0], N_DEV * a.shape[1], LANES), a.dtype) for a in blocks] + [_sds((N_DEV,) + small.shape, small.dtype)]

    def body(*refs):
        src_refs, out_refs = refs[:n], refs[n:2 * n]
        send_sems, recv_sems, local_sems = refs[2 * n:]
        x, y, c, me = _place()

        def landing(a, slot):
            if a == n - 1:
                return out_refs[a].at[slot]
            r = src_refs[a].shape[1]
            return out_refs[a].at[:, pl.ds(slot * r, r), :]

        def remote(a, k, slot):
            peer, _ = _peer(x, y, c, k)
            return pltpu.make_async_remote_copy(
                src_ref=src_refs[a], dst_ref=landing(a, slot), send_sem=send_sems.at[a * N_DEV + k],
                recv_sem=recv_sems.at[a * N_DEV + k], device_id=peer, device_id_type=pl.DeviceIdType.MESH)

        mine = [pltpu.make_async_copy(src_refs[a], landing(a, me), local_sems.at[a]) for a in range(n)]
        sends = [remote(a, k, me) for a in range(n) for k in range(1, N_DEV)]
        for cp in mine + sends:
            cp.start()
        for a in range(n):
            for k in range(1, N_DEV):
                remote(a, k, _peer(x, y, c, k)[1]).wait_recv()
        for cp in sends:
            cp.wait_send()
        for cp in mine:
            cp.wait()

    any_space = pl.BlockSpec(memory_space=pl.ANY)
    return pl.pallas_call(
        body, name=name, out_shape=out_shapes, in_specs=[any_space] * n, out_specs=[any_space] * n,
        scratch_shapes=[pltpu.SemaphoreType.DMA((n * N_DEV,)), pltpu.SemaphoreType.DMA((n * N_DEV,)), pltpu.SemaphoreType.DMA((n,))],
    )(*srcs)


_HBM = pl.BlockSpec(memory_space=pltpu.HBM)
_SEM = pl.BlockSpec(memory_space=pltpu.SEMAPHORE)
_EFFECT = pltpu.SideEffectType.DATAFLOW_SIDE_EFFECTING


def _in_hbm(a):
    return pltpu.with_memory_space_constraint(a, pltpu.HBM)


def _scatter_start(name, srcs):
    ns = len(srcs)
    rows_a = [a.shape[0] // N_DEV for a in srcs]
    offs = [sum(rows_a[:a]) for a in range(ns)]
    total = sum(rows_a)

    def body(*refs):
        src_refs, land_ref, send_sems, recv_sems, token = refs[:ns], refs[ns], refs[ns + 1], refs[ns + 2], refs[-1]
        x, y, c, me = _place()
        for a in range(ns):
            pltpu.make_async_copy(src_refs[a].at[pl.ds(me * rows_a[a], rows_a[a]), :],
                                  land_ref.at[me, pl.ds(offs[a], rows_a[a]), :], send_sems.at[0]).start()
        for k in range(1, N_DEV):
            peer, peer_slot = _peer(x, y, c, k)
            for a in range(ns):
                pltpu.make_async_remote_copy(
                    src_ref=src_refs[a].at[pl.ds(peer_slot * rows_a[a], rows_a[a]), :],
                    dst_ref=land_ref.at[me, pl.ds(offs[a], rows_a[a]), :], send_sem=send_sems.at[k], recv_sem=recv_sems.at[k],
                    device_id=peer, device_id_type=pl.DeviceIdType.MESH).start()
        token[...] = jnp.zeros_like(token)

    landing = lax.empty((N_DEV, total, LANES), F32)
    out = pl.pallas_call(
        body, name=name,
        out_shape=(pltpu.SemaphoreType.DMA((N_DEV,)), pltpu.SemaphoreType.DMA((N_DEV,)), *[pltpu.HBM(a.shape, a.dtype) for a in srcs],
                   pltpu.HBM(landing.shape, F32), _sds((8, 128))),
        in_specs=[_HBM] * (ns + 1), out_specs=(_SEM, _SEM, *[_HBM] * (ns + 1), pl.BlockSpec(memory_space=pltpu.VMEM)),
        input_output_aliases={a: 2 + a for a in range(ns + 1)},
        compiler_params=pltpu.CompilerParams(has_side_effects=_EFFECT),
    )(*[_in_hbm(a) for a in srcs], _in_hbm(landing))
    return out[0], out[1], out[2:2 + ns], out[2 + ns], out[-1]


def _scatter_wait(name, send_sems, recv_sems, srcs, landing, after):
    ns = len(srcs)

    def body(*refs):
        land_ref, send_ref, recv_ref = refs[ns], refs[ns + 1], refs[ns + 2]
        x, y, c, me = _place()
        pltpu.make_async_copy(land_ref.at[me], land_ref.at[me], send_ref.at[0]).wait()
        for k in range(1, N_DEV):
            peer, peer_slot = _peer(x, y, c, k)
            slot = pltpu.make_async_remote_copy(
                src_ref=land_ref.at[me], dst_ref=land_ref.at[peer_slot], send_sem=send_ref.at[k], recv_sem=recv_ref.at[k],
                device_id=peer, device_id_type=pl.DeviceIdType.MESH)
            slot.wait_send()
            slot.wait_recv()

    out = pl.pallas_call(
        body, name=name, out_shape=(*[pltpu.HBM(a.shape, a.dtype) for a in srcs], pltpu.HBM(landing.shape, landing.dtype)),
        in_specs=[_HBM] * (ns + 1) + [_SEM, _SEM, pl.BlockSpec(memory_space=pl.ANY)], out_specs=[_HBM] * (ns + 1),
        input_output_aliases={a: a for a in range(ns + 1)},
        compiler_params=pltpu.CompilerParams(has_side_effects=_EFFECT),
    )(*srcs, landing, send_sems, recv_sems, after)
    return out[:ns], out[ns]


def _gather_start(name, blocks):
    n = len(blocks)

    def body(*refs):
        src_refs, land_refs, send_sems, recv_sems, token = refs[:n], refs[n:2 * n], refs[2 * n], refs[2 * n + 1], refs[-1]
        x, y, c, me = _place()
        for a in range(n):
            r = src_refs[a].shape[1]
            mine = land_refs[a].at[:, pl.ds(me * r, r), :]
            pltpu.make_async_copy(src_refs[a], mine, send_sems.at[a * N_DEV]).start()
            for k in range(1, N_DEV):
                pltpu.make_async_remote_copy(
                    src_ref=src_refs[a], dst_ref=mine, send_sem=send_sems.at[a * N_DEV + k],
                    recv_sem=recv_sems.at[a * N_DEV + k], device_id=_peer(x, y, c, k)[0], device_id_type=pl.DeviceIdType.MESH).start()
        token[...] = jnp.zeros_like(token)

    wholes = [lax.empty((a.shape[0], N_DEV * a.shape[1], LANES), a.dtype) for a in blocks]
    out = pl.pallas_call(
        body, name=name,
        out_shape=(pltpu.SemaphoreType.DMA((n * N_DEV,)), pltpu.SemaphoreType.DMA((n * N_DEV,)),
                   *[pltpu.HBM(a.shape, a.dtype) for a in blocks + wholes], _sds((8, 128))),
        in_specs=[_HBM] * (2 * n), out_specs=(_SEM, _SEM, *[_HBM] * (2 * n), pl.BlockSpec(memory_space=pltpu.VMEM)),
        input_output_aliases={a: 2 + a for a in range(2 * n)},
        compiler_params=pltpu.CompilerParams(has_side_effects=_EFFECT),
    )(*[_in_hbm(a) for a in blocks + wholes])
    return out[0], out[1], out[2:2 + n], out[2 + n:2 + 2 * n], out[-1]


def _gather_wait(name, send_sems, recv_sems, blocks, wholes, after):
    n = len(blocks)

    def body(*refs):
        src_refs, land_refs, send_ref, recv_ref = refs[:n], refs[n:2 * n], refs[2 * n], refs[2 * n + 1]
        x, y, c, me = _place()
        for a in range(n):
            r = src_refs[a].shape[1]
            pltpu.make_async_copy(src_refs[a], land_refs[a].at[:, pl.ds(me * r, r), :], send_ref.at[a * N_DEV]).wait()
            for k in range(1, N_DEV):
                peer, peer_slot = _peer(x, y, c, k)
                cp = pltpu.make_async_remote_copy(
                    src_ref=src_refs[a], dst_ref=land_refs[a].at[:, pl.ds(peer_slot * r, r), :], send_sem=send_ref.at[a * N_DEV + k],
                    recv_sem=recv_ref.at[a * N_DEV + k], device_id=peer, device_id_type=pl.DeviceIdType.MESH)
                cp.wait_send()
                cp.wait_recv()

    return pl.pallas_call(
        body, name=name, out_shape=tuple(pltpu.HBM(a.shape, a.dtype) for a in list(blocks) + list(wholes)),
        in_specs=[_HBM] * (2 * n) + [_SEM, _SEM, pl.BlockSpec(memory_space=pl.ANY)], out_specs=[_HBM] * (2 * n),
        input_output_aliases={a: a for a in range(2 * n)},
        compiler_params=pltpu.CompilerParams(has_side_effects=_EFFECT),
    )(*blocks, *wholes, send_sems, recv_sems, after)


def _reduce_adamw(parts, w, m, v, tile_rows=PACK_TILE):
    rows = w.shape[0]
    nt = rows // tile_rows
    slots = parts.shape[0]
    c1 = 1.0 - ADAM_B1 ** ADAM_STEP
    c2 = 1.0 - ADAM_B2 ** ADAM_STEP

    def body(p_ref, w_ref, m_ref, v_ref, g_out, d_out, m_out, v_out):
        g = p_ref[0]
        for s in range(1, slots):
            g = g + p_ref[s]
        m_new = ADAM_B1 * m_ref[...] + (1.0 - ADAM_B1) * g
        v_new = ADAM_B2 * v_ref[...] + (1.0 - ADAM_B2) * (g * g)
        g_out[...] = g
        m_out[...] = m_new
        v_out[...] = v_new
        d_out[...] = -ADAM_LR * ((m_new / c1) / (jnp.sqrt(v_new / c2) + ADAM_EPS) + ADAM_WD * w_ref[...])

    r = _rows_spec(tile_rows, LANES, nt)
    out = _sds((rows, LANES))
    return _call(
        body, "reduce_adamw", nt,
        [pl.BlockSpec((slots, tile_rows, LANES), lambda i: (0, i, 0)), r, r, r], [r, r, r, r], [out, out, out, out],
    )(parts, w, m, v)


def _pack_rows(a, lead=0):
    head = a.shape[:lead]
    flat = a.reshape(head + (-1,))
    size = flat.shape[-1]
    rows = -(-size // (16 * LANES)) * 16
    flat = jnp.pad(flat, [(0, 0)] * lead + [(0, rows * LANES - size)])
    return flat.reshape(head + (rows, LANES))


def _packed_rows(shape):
    return -(-math.prod(shape) // (16 * LANES)) * 16


def _to_blocks(full, axis):
    l, a, b = full.shape
    if axis == 2:
        return full.reshape(l, a, N_DEV, b // N_DEV).transpose(2, 0, 1, 3)
    return full.reshape(l, N_DEV, a // N_DEV, b).transpose(1, 0, 2, 3)


def _from_blocks(blocks, axis):
    _, l, a, b = blocks.shape
    if axis == 2:
        return blocks.transpose(1, 2, 0, 3).reshape(l, a, N_DEV * b)
    return blocks.transpose(1, 0, 2, 3).reshape(l, N_DEV * a, b)


def _row_form(shard, transposed):
    return shard.transpose(0, 2, 1) if transposed else shard


def _me():
    return 4 * lax.axis_index("x") + 2 * lax.axis_index("y") + lax.axis_index("c")


def _both_forms(names, wholes, layer):
    out = {}
    for name, w in zip(names, wholes):
        t = dict(BIG)[name]
        out[name + '_t' if t else name] = w[layer]
        out[name if t else name + '_t'] = w[layer].T
    return out


def _gather_weights(local):
    segs, meta = [], []
    for name in SMALL_SHARDED:
        blk = local[name]
        if name in GATHER_F32:
            bits = lax.bitcast_convert_type(blk, MXU)
        else:
            bits = _mx(blk)
        seg = _pack_rows(bits)
        meta.append((name, bits.shape, seg.shape[0]))
        segs.append(seg)
    blocks = {name: _mx(_row_form(local[name], t)) for name, t in BIG}
    mix, ffn = PARTS['mix'], PARTS['ffn']
    *first, gathered = _all_gather("gather_weights", [blocks[n][:1] for n in mix], jnp.concatenate(segs, axis=0))
    flights = {'ffn0': _gather_start("gather_ffn0_start", [blocks[n][:1] for n in ffn]),
               'later': _gather_start("gather_later_start", [blocks[n][1:] for n in mix + ffn])}
    out, lo = {}, 0
    for name, bits_shape, rows in meta:
        seg = gathered[:, lo:lo + rows].reshape(N_DEV, -1)[:, :math.prod(bits_shape)].reshape((N_DEV,) + bits_shape)
        if name in GATHER_F32:
            seg = lax.bitcast_convert_type(seg, F32)
        out[name] = _from_blocks(seg, SHARD_AXIS[name])
        lo += rows
    ready = {(0, 'mix'): _both_forms(mix, first, 0)}

    def landed(flight, names, after):
        send_sems, recv_sems, mine, wholes, _ = flights[flight]
        return _gather_wait(f"gather_{flight}_wait", send_sems, recv_sems, mine, wholes, after)[len(names):]

    def big_weights(l, part, after):
        if (l, part) not in ready and l == 0:
            ready[(0, 'ffn')] = _both_forms(ffn, landed('ffn0', ffn, after), 0)
        elif (l, part) not in ready:
            wholes = landed('later', mix + ffn, after)
            for j in range(1, DEPTH):
                forms = _both_forms(mix + ffn, wholes, j - 1)
                for p, names in PARTS.items():
                    ready[(j, p)] = {k: forms[k] for n in names for k in (n, n + '_t')}
        return ready[(l, part)]

    return out, big_weights, flights['ffn0'][-1][0, 0] + flights['later'][-1][0, 0]


def _s5_discretize(a_re, a_im, log_dt, b_re, b_im):
    lam_re = jnp.minimum(a_re, -1e-4)
    lam_im = a_im
    dt = jnp.exp(log_dt)[:, None]
    decay = jnp.exp(dt * lam_re)
    ang = dt * lam_im
    abar_re = decay * jnp.cos(ang)
    abar_im = decay * jnp.sin(ang)
    den = jnp.square(lam_re) + jnp.square(lam_im)
    nr = abar_re - 1.0
    ni = abar_im
    coef_re = (nr * lam_re + ni * lam_im) / den
    coef_im = (ni * lam_re - nr * lam_im) / den
    bbar_re = coef_re[..., None] * b_re - coef_im[..., None] * b_im
    bbar_im = coef_re[..., None] * b_im + coef_im[..., None] * b_re
    return abar_re, abar_im, bbar_re, bbar_im


def _complex_powers(ar, ai, count):
    exponents = jnp.arange(1, count + 1).reshape((count,) + (1,) * ar.ndim)
    pr = jnp.ones((count,) + ar.shape, F32)
    pi = jnp.zeros_like(pr)
    sr, si = ar, ai
    for b in range(count.bit_length()):
        bit = ((exponents >> b) & 1) == 1
        fr, fi = jnp.where(bit, sr[None], 1.0), jnp.where(bit, si[None], 0.0)
        pr, pi = pr * fr - pi * fi, pr * fi + pi * fr
        sr, si = sr * sr - si * si, 2.0 * sr * si
    return pr, pi


_EYE16 = functools.partial(jnp.eye, 16, dtype=F32)


def _s5_params(p):
    disc, disc_vjp = jax.vjp(jax.vmap(_s5_discretize), p['s5_a_re'], p['s5_a_im'], p['s5_log_dt'], p['s5_b_re'], p['s5_b_im'])
    abar_re, abar_im, bbar_re, bbar_im = disc
    pw_re, pw_im = _complex_powers(abar_re.reshape(DEPTH, N_STATE), abar_im.reshape(DEPTH, N_STATE), SEG)
    sp_re, sp_im = _complex_powers(pw_re[SEG - 1], pw_im[SEG - 1], N_SEG - 1)
    one, zero = jnp.ones((1, DEPTH, N_STATE), F32), jnp.zeros((1, DEPTH, N_STATE), F32)
    seg_re = jnp.concatenate([one, sp_re], axis=0)
    seg_im = jnp.concatenate([zero, sp_im], axis=0)
    doubling = [0, 1, 3]
    blank = jnp.zeros((5, DEPTH, N_STATE), F32)
    tables = {
        'pw_re': pw_re, 'pw_im': pw_im,
        'dbl_re': jnp.concatenate([jnp.stack([sp_re[k] for k in doubling]), blank], axis=0),
        'dbl_im': jnp.concatenate([jnp.stack([sp_im[k] for k in doubling]), blank], axis=0),
        'seg_re': seg_re, 'seg_im': seg_im, 'segr_re': seg_re[::-1], 'segr_im': seg_im[::-1],
    }
    weights = {
        'b_re': _mx(jnp.einsum('lgpc,gh->lgchp', bbar_re, _EYE16()).reshape(DEPTH, D_S5, N_STATE)),
        'b_im': _mx(jnp.einsum('lgpc,gh->lgchp', bbar_im, _EYE16()).reshape(DEPTH, D_S5, N_STATE)),
        'c_re': _mx(jnp.einsum('lgcp,gh->lgphc', p['s5_c_re'], _EYE16()).reshape(DEPTH, N_STATE, D_S5)),
        'c_im': _mx(jnp.einsum('lgcp,gh->lgphc', p['s5_c_im'], _EYE16()).reshape(DEPTH, N_STATE, D_S5)),
    }
    src = (jnp.arange(TILE) % N_SEG) * SEG + jnp.arange(TILE) // N_SEG
    perm = (src[:, None] == jnp.arange(TILE)[None, :]).astype(MXU)
    layers = []
    for l in range(DEPTH):
        prm = {k: v[:, l] for k, v in tables.items()}
        prm.update({k: v[l] for k, v in weights.items()})
        prm.update({'perm': perm, 'perm_t': perm.T, 'd': p['s5_d'][l][None, :], 'glu_w': p['s5_glu_w'][l],
                    'glu_b': p['s5_glu_b'][l][None, :]})
        layers.append(prm)
    return layers, disc_vjp


def _lru_params(p, l):
    eye4 = jnp.eye(4, dtype=F32)
    return {
        'conv_w': p['lru_conv_w'][l], 'conv_b': p['lru_conv_b'][l][None, :],
        'wx': _mx(jnp.einsum('hij,hk->hikj', p['lru_wx'][l], eye4).reshape(D_LRU, D_LRU)),
        'wa': _mx(jnp.einsum('hij,hk->hikj', p['lru_wa'][l], eye4).reshape(D_LRU, D_LRU)),
        'bx': p['lru_bx'][l][None, :], 'ba': p['lru_ba'][l][None, :],
        'sp': jax.nn.softplus(-p['lru_a_param'][l])[None, :],
    }


def _rope_tables(n):
    inv_freq = ROPE_THETA ** (-jnp.arange(0, 64, 2, dtype=F32) / 64)
    ang = jnp.arange(n, dtype=F32)[:, None] * inv_freq[None, :]
    cos, sin = jnp.cos(ang), jnp.sin(ang)
    return jnp.concatenate([cos, cos, cos, cos], axis=1), jnp.concatenate([-sin, sin, -sin, sin], axis=1)


def _sink_cols(sinks):
    nb = TILE_Q // ATTN_BLOCK
    per_unit = sinks.reshape(4, 2).T
    return jnp.broadcast_to(per_unit[:, None, :, None, None], (2, nb, 4, ATTN_BLOCK, 128)).reshape(2, nb * 4 * ATTN_BLOCK, 128)


def _local_step(x, target, p, big_weights=None, emit_grads=None):
    if big_weights is None:
        big_weights = lambda l, part, after: {k: p[k][l] for name in PARTS[part] for k in (name, name + '_t')}
    if emit_grads is None:
        emit_grads = lambda l, part, grads: 0.0
    n = x.shape[0]
    cos_t, sin_t = _rope_tables(n)
    s5_layers, s5_vjp = _s5_params(p)
    row = lambda a: a[None, :]
    saved = []
    h = x
    for l in range(DEPTH):
        s = {'x0': h}
        bw = s['bw'] = dict(big_weights(l, 'mix', h))
        s['q'], k, v, s['u'], s['xr'], s['gate'] = _inproj_fwd(h, bw['w_in'], row(p['b_in'][l]), cos_t, sin_t)
        no_keys = jnp.zeros((ATTN_BLOCK, D_KV), MXU)
        s['k'], s['v'] = jnp.concatenate([no_keys, k], axis=0), jnp.concatenate([no_keys, v], axis=0)
        s['sinks'] = _sink_cols(p['attn_sinks'][l])
        s['ya'] = _attn_fwd(s['q'], s['k'], s['v'], s['sinks'])
        s['s5'] = s5_layers[l]
        s['lru'] = _lru_params(p, l)
        s['ys'], s['s5_cr'], s['s5_ci'] = _s5_fwd(s['u'], s['s5'])
        s['yl'], s['lru_c'] = _lru_fwd(s['xr'], s['gate'], s['lru'])
        s['mix'], s['r1'], s['x1'] = _mixout_fwd(s['ya'], s['ys'], s['yl'], h, row(p['mix_norm_g'][l]), bw['w_out'],
                                                 row(p['b_out'][l]), row(p['ln1_g'][l]), row(p['ln1_b'][l]))
        bw.update(big_weights(l, 'ffn', s['x1']))
        s['gp'], s['up'], s['r2'], h = _ffn_fwd(s['x1'], bw['ffn_w_gate'], bw['ffn_w_up'], p['ffn_conv_w'][l],
                                                row(p['ffn_conv_b'][l]), bw['ffn_w_down'], row(p['ln2_g'][l]), row(p['ln2_b'][l]))
        saved.append(s)
    loss, dh = _loss_head(h, target)
    placed = 0.0

    grads = {name: [None] * DEPTH for name in WEIGHTS}
    d_disc = [None] * DEPTH
    for l in reversed(range(DEPTH)):
        s = saved[l]
        g = {}
        (dr2, dgp, dup, g['ffn_w_down'], cw0, cw1, cw2, dcb, dg2, db2) = _ffn_bwd_down(
            dh, s['r2'], row(p['ln2_g'][l]) + placed, s['gp'], s['up'], p['ffn_conv_w'][l], row(p['ffn_conv_b'][l]),
            s['bw']['ffn_w_down_t'])
        g['ffn_conv_w'] = jnp.concatenate([cw0, cw1, cw2], axis=0)
        g['ffn_conv_b'], g['ln2_g'], g['ln2_b'] = dcb[0], dg2[0], db2[0]
        dr1, dg1, db1 = _ffn_bwd_dx(dr2, dgp, dup, s['r1'], row(p['ln1_g'][l]), s['bw']['ffn_w_gate_t'], s['bw']['ffn_w_up_t'])
        g['ffn_w_gate'], g['ffn_w_up'] = _ffn_bwd_dw(s['x1'], dgp, dup)
        g['ln1_g'], g['ln1_b'] = dg1[0], db1[0]
        placed = emit_grads(l, 'ffn', [g[name] for name in PARTS['ffn']])
        dya, dys, dyl, g['w_out'], dbo, dgm = _mixout_bwd(dr1, s['mix'], s['ya'], s['ys'], s['yl'],
                                                         row(p['mix_norm_g'][l]) + placed, s['bw']['w_out_t'])
        g['b_out'], g['mix_norm_g'] = dbo[0], dgm[0]

        du, dbr, dbi, dcr, dci, dar, dai, dd, g['s5_glu_w'], dgb = _s5_bwd(s['u'], dys, s['s5_cr'], s['s5_ci'], s['s5'])
        dxr, dgate, lw0, lw1, lw2, lw3, lcb, dwx, dwa, dbx, dba, dsp = _lru_bwd(s['xr'], s['gate'], dyl, s['lru_c'], s['lru'])
        g['lru_conv_w'] = jnp.concatenate([lw0, lw1, lw2, lw3], axis=0)
        g['lru_conv_b'], g['lru_bx'], g['lru_ba'] = lcb[0], dbx[0], dba[0]
        g['lru_wx'] = jnp.einsum('hihj->hij', dwx.reshape(4, 64, 4, 64))
        g['lru_wa'] = jnp.einsum('hihj->hij', dwa.reshape(4, 64, 4, 64))
        g['lru_a_param'] = -dsp[0] * jax.nn.sigmoid(-p['lru_a_param'][l])

        g['s5_c_re'] = jnp.einsum('gpgc->gcp', dcr.reshape(16, 64, 16, 16))
        g['s5_c_im'] = jnp.einsum('gpgc->gcp', dci.reshape(16, 64, 16, 16))
        g['s5_d'], g['s5_glu_b'] = dd[0], dgb[0]
        d_disc[l] = (dar.reshape(16, 64), dai.reshape(16, 64), jnp.einsum('gcgp->gpc', dbr.reshape(16, 16, 16, 64)),
                     jnp.einsum('gcgp->gpc', dbi.reshape(16, 16, 16, 64)))

        dq, dk, dv, dsink = _attn_bwd(s['q'], s['k'], s['v'], s['sinks'], s['ya'], dya)
        g['attn_sinks'] = dsink[:, 0]
        dh, g['w_in'], dbin = _inproj_bwd(dq, dk[ATTN_BLOCK:], dv[ATTN_BLOCK:], du, dxr, dgate, cos_t, sin_t, s['x0'], dr1,
                                          s['bw']['w_in_t'])
        g['b_in'] = dbin[0]
        placed = emit_grads(l, 'mix', [g[name] for name in PARTS['mix']])
        for name in g:
            grads[name][l] = g[name]
    big = dict(BIG)
    out = {name: grads[name] if name in big else jnp.stack(grads[name]) for name in WEIGHTS if name not in S5_DISC}
    out.update(zip(S5_DISC, s5_vjp(tuple(jnp.stack([d_disc[l][i] for l in range(DEPTH)]) for i in range(4)))))
    return loss, dh, out


def kernel(x, w_in, b_in, attn_sinks, s5_a_re, s5_a_im, s5_b_re, s5_b_im, s5_c_re, s5_c_im, s5_d, s5_log_dt, s5_glu_w, s5_glu_b, lru_conv_w, lru_conv_b, lru_wx, lru_bx, lru_wa, lru_ba, lru_a_param, mix_norm_g, w_out, b_out, ln1_g, ln1_b, ffn_w_gate, ffn_w_up, ffn_conv_w, ffn_conv_b, ffn_w_down, ln2_g, ln2_b, loss_target, m_w_in, m_b_in, m_attn_sinks, m_s5_a_re, m_s5_a_im, m_s5_b_re, m_s5_b_im, m_s5_c_re, m_s5_c_im, m_s5_d, m_s5_log_dt, m_s5_glu_w, m_s5_glu_b, m_lru_conv_w, m_lru_conv_b, m_lru_wx, m_lru_bx, m_lru_wa, m_lru_ba, m_lru_a_param, m_mix_norm_g, m_w_out, m_b_out, m_ln1_g, m_ln1_b, m_ffn_w_gate, m_ffn_w_up, m_ffn_conv_w, m_ffn_conv_b, m_ffn_w_down, m_ln2_g, m_ln2_b, v_w_in, v_b_in, v_attn_sinks, v_s5_a_re, v_s5_a_im, v_s5_b_re, v_s5_b_im, v_s5_c_re, v_s5_c_im, v_s5_d, v_s5_log_dt, v_s5_glu_w, v_s5_glu_b, v_lru_conv_w, v_lru_conv_b, v_lru_wx, v_lru_bx, v_lru_wa, v_lru_ba, v_lru_a_param, v_mix_norm_g, v_w_out, v_b_out, v_ln1_g, v_ln1_b, v_ffn_w_gate, v_ffn_w_up, v_ffn_conv_w, v_ffn_conv_b, v_ffn_w_down, v_ln2_g, v_ln2_b):
    given = dict(locals())
    whole = {name: given[name] for name in WEIGHTS if name not in dict(BIG)}
    small_whole, big_weights, placed = _gather_weights({name: given[name] for name in SHARDED})
    whole.update(small_whole)
    whole['b_in'] = whole['b_in'] + placed
    in_flight = {}

    def emit_grads(l, part, grads):
        in_flight[(l, part)] = _scatter_start(f"grads_start_{part}{l}", grads)
        return in_flight[(l, part)][-1][0, 0]

    loss, grad_x, grads = _local_step(x[0], loss_target[0], whole, big_weights, emit_grads)
    total = lax.psum(loss[0, 0], ("x", "y", "c"))
    return (total, grad_x[None], *_update(given, grads, in_flight, grad_x))


def _update(given, grads, in_flight, after):
    local_w = {name: given[name] for name in WEIGHTS}
    me = _me()
    outs = {}

    shard_rows = sum(_packed_rows(local_w[name].shape) for name in SMALL_SHARDED)
    rep_pad = -sum(_packed_rows(local_w[name].shape) for name in REPLICATED) % PACK_TILE

    def packed_rep(arrays):
        return jnp.concatenate([_pack_rows(arrays[name]) for name in REPLICATED] + [jnp.zeros((rep_pad, LANES), F32)], axis=0)

    rep_grads = packed_rep(grads)
    chunk = rep_grads.shape[0] // N_DEV
    small = jnp.concatenate(
        [_pack_rows(_to_blocks(grads[name], SHARD_AXIS[name]), lead=1) for name in SMALL_SHARDED]
        + [rep_grads.reshape(N_DEV, chunk, LANES)], axis=1)
    small_rows = shard_rows + chunk
    small_flight = _scatter_start("grads_start_small", [small.reshape(N_DEV * small_rows, LANES)])

    def summed(name, flight, rows_of, packed, after):
        send_sems, recv_sems, srcs, landing, _ = flight
        _, landing = _scatter_wait(name, send_sems, recv_sems, srcs, landing, after)
        return _reduce_adamw(landing, *packed, tile_rows=sum(rows_of) // 4)

    forms = {part: [(name, dict(BIG)[name], _row_form(local_w[name], dict(BIG)[name]).shape[1]) for name in names]
             for part, names in PARTS.items()}

    def big_part(key, after):
        l, part = key
        packed = [jnp.concatenate([_row_form(given[prefix + name], t)[l] for name, t, _ in forms[part]], axis=0)
                  for prefix in ('', 'm_', 'v_')]
        outs[key] = summed(f"grads_wait_{part}{l}", in_flight[key], [r for _, _, r in forms[part]], packed, after)
        return outs[key][0]

    *earlier, last = in_flight
    for key in earlier:
        after = big_part(key, after)
    rep_state = [packed_rep({name: given[prefix + name] for name in REPLICATED}) for prefix in ('', 'm_', 'v_')]
    small_state = [jnp.concatenate([_pack_rows(given[prefix + name]) for name in SMALL_SHARDED]
                                   + [lax.dynamic_slice_in_dim(rep, me * chunk, chunk, axis=0)], axis=0)
                   for prefix, rep in zip(('', 'm_', 'v_'), rep_state)]
    small_outs = summed("grads_wait_small", small_flight, [small_rows], small_state, after)
    rep_sum = _all_gather("gather_small_grads", [], small_outs[0][shard_rows:])[0].reshape(N_DEV * chunk, LANES)
    rep_outs = _reduce_adamw(rep_sum[None], *rep_state, tile_rows=N_DEV * chunk // 4)
    big_part(last, rep_outs[0])

    def unpack(i):
        res = {}
        for part in PARTS:
            lo = 0
            for name, t, r in forms[part]:
                res[name] = _row_form(jnp.stack([outs[(l, part)][i][lo:lo + r] for l in range(DEPTH)]), t)
                lo += r
        for names, packed in ((SMALL_SHARDED, small_outs[i]), (REPLICATED, rep_outs[i])):
            lo = 0
            for name in names:
                shape = local_w[name].shape
                res[name] = packed[lo:lo + _packed_rows(shape)].reshape(-1)[:math.prod(shape)].reshape(shape)
                lo += _packed_rows(shape)
        return [res[name] for name in WEIGHTS]

    return (*unpack(0), *unpack(1), *unpack(2), *unpack(3))
```

```python
import functools
import math

import jax
import jax.numpy as jnp
from jax import lax
from jax.experimental import pallas as pl
from jax.experimental.pallas import tpu as pltpu

F32 = jnp.float32
MXU = jnp.bfloat16

N_DEV = 8
DEPTH = 4
D = 1024
D_ATTN, D_KV, D_S5, D_LRU = 512, 128, 256, 256
D_IN = 1536
D_FF = 2816
FF_CHUNK = 256
N_STATE = 1024
LANES = 1024
ALPHA = (2 * DEPTH) ** 0.25
LN_EPS = 1e-5
RMS_EPS = 1e-6
LRU_C = 8.0
ROPE_THETA = 10000.0
ADAM_LR, ADAM_B1, ADAM_B2, ADAM_EPS, ADAM_WD, ADAM_STEP = 0.001, 0.9, 0.999, 1e-08, 0.01, 10

TILE = 256
S5_PAIR = 4
N_SEG = 8
SEG = TILE // N_SEG
TILE_Q = 512
TILE_BIG = 512
TILE_WIDE = 512
ATTN_BLOCK = 128
PACK_TILE = 256
VMEM_MB = 56

WEIGHTS = ['w_in', 'b_in', 'attn_sinks', 's5_a_re', 's5_a_im', 's5_b_re', 's5_b_im', 's5_c_re', 's5_c_im', 's5_d', 's5_log_dt',
           's5_glu_w', 's5_glu_b', 'lru_conv_w', 'lru_conv_b', 'lru_wx', 'lru_bx', 'lru_wa', 'lru_ba', 'lru_a_param', 'mix_norm_g',
           'w_out', 'b_out', 'ln1_g', 'ln1_b', 'ffn_w_gate', 'ffn_w_up', 'ffn_conv_w', 'ffn_conv_b', 'ffn_w_down', 'ln2_g', 'ln2_b']
SHARD_AXIS = {'w_in': 2, 's5_glu_w': 1, 'lru_conv_w': 2, 'w_out': 1, 'ffn_w_gate': 2, 'ffn_w_up': 2, 'ffn_conv_w': 2,
              'ffn_w_down': 1}
SHARDED = [n for n in WEIGHTS if n in SHARD_AXIS]
REPLICATED = [n for n in WEIGHTS if n not in SHARD_AXIS]
BIG = [(n, SHARD_AXIS[n] == 2) for n in ('w_in', 'w_out', 'ffn_w_gate', 'ffn_w_up', 'ffn_w_down')]
SMALL_SHARDED = [n for n in SHARDED if n not in dict(BIG)]
PARTS = {'mix': ['w_in', 'w_out'], 'ffn': ['ffn_w_gate', 'ffn_w_up', 'ffn_w_down']}
GATHER_F32 = ('lru_conv_w', 'ffn_conv_w')


def _dot(a, b):
    return jnp.dot(a, b, preferred_element_type=F32)


def _dot_nt(a, b):
    return lax.dot_general(a, b, (((1,), (1,)), ((), ())), preferred_element_type=F32)


def _dot_tn(a, b):
    return lax.dot_general(a, b, (((0,), (0,)), ((), ())), preferred_element_type=F32)


def _mx(a):
    return a.astype(MXU)


_GELU_C = math.sqrt(2.0 / math.pi)


def _gelu(x):
    th = jnp.tanh(_GELU_C * (x + 0.044715 * x * x * x))
    return 0.5 * x * (1.0 + th)


def _gelu_grad(x):
    th = jnp.tanh(_GELU_C * (x + 0.044715 * x * x * x))
    return 0.5 * (1.0 + th) + 0.5 * x * (1.0 - th * th) * _GELU_C * (1.0 + 3.0 * 0.044715 * x * x)


def _sigmoid(x):
    return 0.5 * jnp.tanh(0.5 * x) + 0.5


def _ln_stats(r):
    mu = jnp.mean(r, axis=-1, keepdims=True)
    xc = r - mu
    var = jnp.mean(xc * xc, axis=-1, keepdims=True)
    rstd = lax.rsqrt(var + LN_EPS)
    return xc * rstd, rstd


def _ln_bwd(dy, g, xhat, rstd):
    dxh = dy * g
    return rstd * (dxh - jnp.mean(dxh, axis=-1, keepdims=True) - xhat * jnp.mean(dxh * xhat, axis=-1, keepdims=True))


def _rms(y):
    return lax.rsqrt(jnp.mean(y * y, axis=-1, keepdims=True) + RMS_EPS)


def _sum0(a):
    return jnp.sum(a, axis=0, keepdims=True)


def _row_iota(shape):
    return lax.broadcasted_iota(jnp.int32, shape, 0)


def _shift_down(ext, j, rows):
    return pltpu.roll(ext, j, 0)[8:8 + rows]


def _shift_up(ext, j, rows):
    return pltpu.roll(ext, ext.shape[0] - j, 0)[:rows]


def _swap_halves(t):
    w = t.shape[1]
    lane = lax.broadcasted_iota(jnp.int32, t.shape, 1)
    return jnp.where((lane & 32) == 0, pltpu.roll(t, w - 32, 1), pltpu.roll(t, 32, 1))


def _rope(t, cos, sin_signed):
    return t * cos + _swap_halves(t) * sin_signed


def _rope_t(d, cos, sin_signed):
    return d * cos + _swap_halves(d * sin_signed)


def _cmul_add(ar, ai, xr, xi, yr, yi):
    return ar * xr - ai * xi + yr, ar * xi + ai * xr + yi


def _seg_rows(k):
    return slice(N_SEG * k, N_SEG * (k + 1))


def _permute_rows(perm, x):
    hi = _mx(x)
    rest = x - hi.astype(F32)
    mid = _mx(rest)
    lo = _mx(rest - mid.astype(F32))
    return _dot(perm, hi) + _dot(perm, mid) + _dot(perm, lo)


def _cscan(sr, si, tab, cin_r, cin_i, reverse):
    sgn = -1.0 if reverse else 1.0
    pw_re, pw_im, dbl_re, dbl_im = tab['pw_re'], tab['pw_im'], tab['dbl_re'], tab['dbl_im']
    ar, ai = pw_re[0:1, :], sgn * pw_im[0:1, :]
    shape = (N_SEG, sr.shape[1])
    hr = hi = jnp.zeros(shape, F32)
    for k in (range(SEG - 1, -1, -1) if reverse else range(SEG)):
        hr, hi = _cmul_add(ar, ai, hr, hi, sr[_seg_rows(k), :], si[_seg_rows(k), :])
        sr[_seg_rows(k), :] = hr
        si[_seg_rows(k), :] = hi
    sub = _row_iota(shape)

    def shifted(v, d):
        if reverse:
            return jnp.where(sub < N_SEG - d, pltpu.roll(v, N_SEG - d, 0), 0.0)
        return jnp.where(sub >= d, pltpu.roll(v, d, 0), 0.0)

    fr, fi = hr, hi
    for j, d in enumerate((1, 2, 4)):
        fr, fi = _cmul_add(dbl_re[j:j + 1, :], sgn * dbl_im[j:j + 1, :], shifted(fr, d), shifted(fi, d), fr, fi)
    seg_re, seg_im = (tab['segr_re'], tab['segr_im']) if reverse else (tab['seg_re'], tab['seg_im'])
    cr, ci = _cmul_add(seg_re[...], sgn * seg_im[...], cin_r, cin_i, shifted(fr, 1), shifted(fi, 1))
    nr, ni = _cmul_add(dbl_re[0:1, :], sgn * dbl_im[0:1, :], cr, ci, hr, hi)
    for k in range(SEG):
        j = SEG - 1 - k if reverse else k
        xr, xi = _cmul_add(pw_re[j:j + 1, :], sgn * pw_im[j:j + 1, :], cr, ci, sr[_seg_rows(k), :], si[_seg_rows(k), :])
        sr[_seg_rows(k), :] = xr
        si[_seg_rows(k), :] = xi
    edge = slice(0, 1) if reverse else slice(N_SEG - 1, N_SEG)
    return nr[edge], ni[edge]


def _rscan(a, b, reverse):
    rows = a.shape[0]
    row = _row_iota(a.shape)
    s = 1
    while s < rows:
        if reverse:
            keep = row < rows - s
            sa = jnp.where(keep, pltpu.roll(a, rows - s, 0), 1.0)
            sb = jnp.where(keep, pltpu.roll(b, rows - s, 0), 0.0)
        else:
            keep = row >= s
            sa = jnp.where(keep, pltpu.roll(a, s, 0), 1.0)
            sb = jnp.where(keep, pltpu.roll(b, s, 0), 0.0)
        b = b + a * sb
        a = a * sa
        s *= 2
    return a, b


def _whole():
    return pl.BlockSpec(memory_space=pltpu.VMEM)


def _rows_spec(rows, cols, n_tiles, reverse=False):
    if reverse:
        return pl.BlockSpec((rows, cols), lambda i: (n_tiles - 1 - i, 0))
    return pl.BlockSpec((rows, cols), lambda i: (i, 0))


def _halo_spec(cols, tile_rows, n_tiles, reverse=False):
    per = tile_rows // 8
    if reverse:
        return pl.BlockSpec((8, cols), lambda i: (jnp.maximum((n_tiles - 1 - i) * per - 1, 0), 0))
    return pl.BlockSpec((8, cols), lambda i: (jnp.maximum(i * per - 1, 0), 0))


def _call(body, name, n_tiles, in_specs, out_specs, out_shape, scratch=()):
    return pl.pallas_call(
        body, name=name, grid=(n_tiles,), in_specs=in_specs, out_specs=out_specs, out_shape=out_shape,
        scratch_shapes=list(scratch),
        compiler_params=pltpu.CompilerParams(dimension_semantics=("arbitrary",), vmem_limit_bytes=VMEM_MB << 20))


def _sds(shape, dtype=F32):
    return jax.ShapeDtypeStruct(shape, dtype)


def _inproj_fwd(x, w, b, cos_t, sin_t):
    n = x.shape[0]
    nt = n // TILE_WIDE

    def body(x_ref, w_ref, b_ref, c_ref, s_ref, q_ref, k_ref, v_ref, u_ref, xr_ref, g_ref):
        p = _dot(_mx(x_ref[...]), w_ref[...]) + b_ref[...]
        cos, sin = c_ref[...], s_ref[...]
        q_ref[...] = _mx(_rope(p[:, :D_ATTN], jnp.tile(cos, (1, 4)), jnp.tile(sin, (1, 4))))
        k_ref[...] = _mx(_rope(p[:, 512:640], cos, sin))
        v_ref[...] = _mx(p[:, 640:768])
        u_ref[...] = p[:, 768:1024]
        xr_ref[...] = p[:, 1024:1280]
        g_ref[...] = p[:, 1280:1536]

    r = functools.partial(_rows_spec, n_tiles=nt)
    return _call(
        body, "inproj_fwd", nt,
        [r(TILE_WIDE, D), _whole(), _whole(), r(TILE_WIDE, 128), r(TILE_WIDE, 128)],
        [r(TILE_WIDE, D_ATTN), r(TILE_WIDE, D_KV), r(TILE_WIDE, D_KV), r(TILE_WIDE, D_S5), r(TILE_WIDE, D_LRU), r(TILE_WIDE, D_LRU)],
        [_sds((n, D_ATTN), MXU), _sds((n, D_KV), MXU), _sds((n, D_KV), MXU), _sds((n, D_S5)), _sds((n, D_LRU)), _sds((n, D_LRU))],
    )(x, w, b, cos_t, sin_t)


def _inproj_bwd(dq, dk, dv, du, dxr, dgate, cos_t, sin_t, x0, dr1, w_t):
    n = x0.shape[0]
    nt = n // TILE_WIDE

    def body(dq_ref, dk_ref, dv_ref, du_ref, dxr_ref, dg_ref, c_ref, s_ref, x_ref, dr_ref, w_ref, dx_ref, dw_ref, db_ref):
        @pl.when(pl.program_id(0) == 0)
        def _():
            dw_ref[...] = jnp.zeros_like(dw_ref)
            db_ref[...] = jnp.zeros_like(db_ref)

        cos, sin = c_ref[...], s_ref[...]
        dtq = _rope_t(dq_ref[...], jnp.tile(cos, (1, 4)), jnp.tile(sin, (1, 4)))
        dtk = _rope_t(dk_ref[...], cos, sin)
        dp = jnp.concatenate([dtq, dtk, dv_ref[...], du_ref[...], dxr_ref[...], dg_ref[...]], axis=1)
        db_ref[...] += _sum0(dp)
        dpb = _mx(dp)
        dw_ref[...] += _dot_tn(dpb, _mx(x_ref[...]))
        dx_ref[...] = ALPHA * dr_ref[...] + _dot(dpb, w_ref[...])

    r = functools.partial(_rows_spec, n_tiles=nt)
    return _call(
        body, "inproj_bwd", nt,
        [r(TILE_WIDE, D_ATTN), r(TILE_WIDE, D_KV), r(TILE_WIDE, D_KV), r(TILE_WIDE, D_S5), r(TILE_WIDE, D_LRU), r(TILE_WIDE, D_LRU),
         r(TILE_WIDE, 128), r(TILE_WIDE, 128), r(TILE_WIDE, D), r(TILE_WIDE, D), _whole()],
        [r(TILE_WIDE, D), _whole(), _whole()],
        [_sds((n, D)), _sds((D_IN, D)), _sds((1, D_IN))],
    )(dq, dk, dv, du, dxr, dgate, cos_t, sin_t, x0, dr1, w_t)


def _kv_variants(t, lo):
    tr = pltpu.roll(t, 64, 1)
    out = []
    for j in range(2):
        first = jnp.where(lo, t if j == 0 else tr, 0.0)
        second = jnp.where(lo, 0.0, tr if j == 0 else t)
        out.append(_mx(jnp.concatenate([first, second], axis=0)))
    return out


def _kv_collect(x0, x1, lo):
    a = x0[:256] + pltpu.roll(x0[256:], 64, 1)
    b = pltpu.roll(x1[:256], 64, 1) + x1[256:]
    return jnp.where(lo, a, b)


def _row_sums(x):
    ones = jnp.ones((128, 128), MXU)
    hi = _mx(x)
    lo = _mx(x - hi.astype(F32))
    return _dot(hi, ones) + _dot(lo, ones)


def _attn_probs(s, sink_ref):
    out = []
    for hp in range(2):
        sh = s[:, hp * 128:(hp + 1) * 128]
        sink = sink_ref[hp]
        m = jnp.maximum(jnp.broadcast_to(jnp.max(sh, axis=1, keepdims=True), sh.shape), sink)
        p = jnp.exp(sh - m)
        es = jnp.exp(sink - m)
        inv = 1.0 / (_row_sums(p) + es)
        out.append((p * inv, es * inv))
    return out


def _band_merge(x, tri, no_previous=None):
    bands = []
    for hp in range(2):
        prev, own = x[:, hp * 256:hp * 256 + 128], x[:, hp * 256 + 128:hp * 256 + 256]
        if no_previous is not None:
            prev = jnp.where(no_previous, -jnp.inf, prev)
        bands.append(jnp.where(tri, own, prev))
    return jnp.concatenate(bands, axis=1)


def _band_split(y, tri):
    parts = []
    for hp in range(2):
        band = y[:, hp * 128:(hp + 1) * 128]
        parts += [jnp.where(tri, 0.0, band), jnp.where(tri, band, 0.0)]
    return jnp.concatenate(parts, axis=1)


def _tri():
    shape = (ATTN_BLOCK, ATTN_BLOCK)
    return lax.broadcasted_iota(jnp.int32, shape, 0) >= lax.broadcasted_iota(jnp.int32, shape, 1)


def _attn_scores(q_ref, k_ref, v_ref, nb, tri):
    lo = lax.broadcasted_iota(jnp.int32, (256, 128), 1) < 64
    kcats, vcats, kstarts, parts = [], [], [], []
    for b in range(nb):
        block = pl.program_id(0) * nb + b
        kstart = pl.multiple_of(block * ATTN_BLOCK, ATTN_BLOCK)
        kcat = _kv_variants(k_ref[pl.ds(kstart, 256), :].astype(F32), lo)
        kcats.append(kcat)
        vcats.append(_kv_variants(v_ref[pl.ds(kstart, 256), :].astype(F32), lo))
        kstarts.append(kstart)
        for j in range(2):
            s = _dot_nt(_kv_group(q_ref, b, j), kcat[j]) * 0.125
            parts += [_band_merge(s[:ATTN_BLOCK], tri, block == 0), _band_merge(s[ATTN_BLOCK:], tri, block == 0)]
    return jnp.concatenate(parts, axis=0), kcats, vcats, kstarts


def _kv_group(a, b, j):
    rows = slice(b * ATTN_BLOCK, (b + 1) * ATTN_BLOCK)
    return jnp.concatenate([a[rows, 2 * j * 128:(2 * j + 1) * 128], a[rows, (2 * j + 1) * 128:(2 * j + 2) * 128]], axis=0)


def _put_kv_group(ref, b, j, x):
    rows = slice(b * ATTN_BLOCK, (b + 1) * ATTN_BLOCK)
    ref[rows, 2 * j * 128:(2 * j + 1) * 128] = x[:ATTN_BLOCK]
    ref[rows, (2 * j + 1) * 128:(2 * j + 2) * 128] = x[ATTN_BLOCK:]


def _band_split_group(y, unit, tri):
    return _mx(jnp.concatenate([_band_split(y[unit:unit + ATTN_BLOCK], tri),
                                _band_split(y[unit + ATTN_BLOCK:unit + 2 * ATTN_BLOCK], tri)], axis=0))


def _attn_fwd(q, k, v, sink_cols):
    n = q.shape[0]
    nt = n // TILE_Q
    nb = TILE_Q // ATTN_BLOCK

    def body(q_ref, k_ref, v_ref, s_ref, o_ref):
        tri = _tri()
        s, _, vcats, _ = _attn_scores(q_ref, k_ref, v_ref, nb, tri)
        (p0, _), (p1, _) = _attn_probs(s, s_ref)
        p = jnp.concatenate([p0, p1], axis=1)
        for b in range(nb):
            for j in range(2):
                _put_kv_group(o_ref, b, j, _dot(_band_split_group(p, (b * 4 + 2 * j) * ATTN_BLOCK, tri), vcats[b][j]))

    return _call(
        body, "attn_fwd", nt,
        [_rows_spec(TILE_Q, D_ATTN, nt), _whole(), _whole(), _whole()],
        _rows_spec(TILE_Q, D_ATTN, nt), _sds((n, D_ATTN)),
    )(q, k, v, sink_cols)


def _attn_bwd(q, k, v, sink_cols, o, do):
    n = q.shape[0]
    nt = n // TILE_Q
    nb = TILE_Q // ATTN_BLOCK

    def body(q_ref, k_ref, v_ref, s_ref, o_ref, do_ref, dq_ref, dk_ref, dv_ref, ds_ref):
        @pl.when(pl.program_id(0) == 0)
        def _():
            dk_ref[...] = jnp.zeros_like(dk_ref)
            dv_ref[...] = jnp.zeros_like(dv_ref)
            ds_ref[...] = jnp.zeros_like(ds_ref)

        lo = lax.broadcasted_iota(jnp.int32, (256, 128), 1) < 64
        tri = _tri()
        s, kcats, vcats, kstarts = _attn_scores(q_ref, k_ref, v_ref, nb, tri)
        probs = _attn_probs(s, s_ref)
        do = do_ref[...]
        dob = _mx(do)
        od = do * o_ref[...]
        lo_q = (lax.broadcasted_iota(jnp.int32, od.shape, 1) & 64) == 0
        od_head = (jnp.where(lo_q, od, 0.0), jnp.where(lo_q, 0.0, od))
        units = [(b, i) for b in range(nb) for i in range(4)]

        def tile_part(a, b, i):
            return a[b * ATTN_BLOCK:(b + 1) * ATTN_BLOCK, i * 128:(i + 1) * 128]

        dp = []
        for b in range(nb):
            for j in range(2):
                x = _dot_nt(_kv_group(dob, b, j), vcats[b][j])
                dp += [_band_merge(x[:ATTN_BLOCK], tri), _band_merge(x[ATTN_BLOCK:], tri)]
        dp = jnp.concatenate(dp, axis=0)
        ds = []
        for hp in range(2):
            p, p_sink = probs[hp]
            delta = _row_sums(jnp.concatenate([tile_part(od_head[hp], b, i) for b, i in units], axis=0))
            ds.append(p * (dp[:, hp * 128:(hp + 1) * 128] - delta) * 0.125)
            t = p_sink * delta
            for i in range(4):
                ds_ref[2 * i + hp:2 * i + hp + 1, :] -= sum(
                    _sum0(t[(b * 4 + i) * ATTN_BLOCK:(b * 4 + i + 1) * ATTN_BLOCK]) for b in range(nb))
        ds = jnp.concatenate(ds, axis=1)
        p = jnp.concatenate([probs[0][0], probs[1][0]], axis=1)
        for b in range(nb):
            dkc, dvc = [], []
            for j in range(2):
                unit = (b * 4 + 2 * j) * ATTN_BLOCK
                dsb = _band_split_group(ds, unit, tri)
                _put_kv_group(dq_ref, b, j, _dot(dsb, kcats[b][j]))
                dkc.append(_dot_tn(dsb, _kv_group(q_ref, b, j)))
                dvc.append(_dot_tn(_band_split_group(p, unit, tri), _kv_group(dob, b, j)))
            dk_ref[pl.ds(kstarts[b], 256), :] += _kv_collect(dkc[0], dkc[1], lo)
            dv_ref[pl.ds(kstarts[b], 256), :] += _kv_collect(dvc[0], dvc[1], lo)

    r = _rows_spec(TILE_Q, D_ATTN, nt)
    return _call(
        body, "attn_bwd", nt,
        [r, _whole(), _whole(), _whole(), r, r],
        [r, _whole(), _whole(), _whole()],
        [_sds((n, D_ATTN)), _sds((n + ATTN_BLOCK, D_KV)), _sds((n + ATTN_BLOCK, D_KV)), _sds((8, 128))],
    )(q, k, v, sink_cols, o, do)


S5_DISC = ('s5_a_re', 's5_a_im', 's5_log_dt', 's5_b_re', 's5_b_im')
S5_TABLES = ('pw_re', 'pw_im', 'dbl_re', 'dbl_im', 'seg_re', 'seg_im', 'segr_re', 'segr_im')
S5_WEIGHTS = ('b_re', 'b_im', 'c_re', 'c_im', 'd', 'glu_w', 'glu_b', 'perm', 'perm_t')


def _s5_states(u, carry_r, carry_i, b_re, b_im, tab, hr_s, hi_s):
    ub = _mx(u)
    hr_s[...] = _dot(ub, b_re[...])
    hi_s[...] = _dot(ub, b_im[...])
    return ub, _cscan(hr_s, hi_s, tab, carry_r, carry_i, reverse=False)


def _s5_fwd(u, prm):
    n = u.shape[0]
    nt = n // (S5_PAIR * TILE)

    def body(u_ref, *refs):
        tab = dict(zip(S5_TABLES, refs[:8]))
        b_re, b_im, c_re, c_im, d_ref, gw_ref, gb_ref, perm, perm_t = refs[8:17]
        y_ref, cr_out, ci_out, cr_s, ci_s = refs[17:22]
        states = [refs[22 + 2 * j:24 + 2 * j] for j in range(S5_PAIR)]
        rows = [slice(j * TILE, (j + 1) * TILE) for j in range(S5_PAIR)]

        @pl.when(pl.program_id(0) == 0)
        def _():
            cr_s[...] = jnp.zeros_like(cr_s)
            ci_s[...] = jnp.zeros_like(ci_s)

        us = []
        for j in range(S5_PAIR):
            us.append(_permute_rows(perm[...], u_ref[rows[j], :]))
            ub = _mx(us[j])
            states[j][0][...] = _dot(ub, b_re[...])
            states[j][1][...] = _dot(ub, b_im[...])
        for j in range(S5_PAIR):
            cr, ci = cr_s[...], ci_s[...]
            cr_out[8 * j:8 * j + 8, :] = jnp.broadcast_to(cr, (8, N_STATE))
            ci_out[8 * j:8 * j + 8, :] = jnp.broadcast_to(ci, (8, N_STATE))
            cr_s[...], ci_s[...] = _cscan(*states[j], tab, cr, ci, reverse=False)
        for j in range(S5_PAIR):
            hr_s, hi_s = states[j]
            y = _dot(_mx(hr_s[...]), c_re[...]) - _dot(_mx(hi_s[...]), c_im[...]) + d_ref[...] * us[j]
            z = _gelu(y)
            y_ref[rows[j], :] = _permute_rows(perm_t[...], z * _sigmoid(_dot(_mx(z), gw_ref[...]) + gb_ref[...]))

    r = functools.partial(_rows_spec, n_tiles=nt)
    return _call(
        body, "s5_fwd", nt,
        [r(S5_PAIR * TILE, D_S5)] + [_whole()] * 17,
        [r(S5_PAIR * TILE, D_S5), r(S5_PAIR * 8, N_STATE), r(S5_PAIR * 8, N_STATE)],
        [_sds((n, D_S5)), _sds((n // TILE * 8, N_STATE)), _sds((n // TILE * 8, N_STATE))],
        scratch=[pltpu.VMEM((1, N_STATE), F32)] * 2 + [pltpu.VMEM((TILE, N_STATE), F32)] * (2 * S5_PAIR),
    )(u, *[prm[k] for k in S5_TABLES + S5_WEIGHTS])


def _s5_bwd(u, dys, carry_re, carry_im, prm):
    n = u.shape[0]
    nt = n // (S5_PAIR * TILE)

    def body(u_ref, dy_ref, cin_r, cin_i, *refs):
        tab = dict(zip(S5_TABLES, refs[:8]))
        b_re, b_im, c_re, c_im, d_ref, gw_ref, gb_ref, perm, perm_t = refs[8:17]
        du_ref, dbr_ref, dbi_ref, dcr_ref, dci_ref, dar_ref, dai_ref, dd_ref, dgw_ref, dgb_ref = refs[17:27]
        gr_s, gi_s = refs[27:29]
        scratch = [refs[29 + 4 * j:33 + 4 * j] for j in range(S5_PAIR)]
        later_first = list(reversed(range(S5_PAIR)))
        rows = [slice(j * TILE, (j + 1) * TILE) for j in range(S5_PAIR)]

        @pl.when(pl.program_id(0) == 0)
        def _():
            for ref in (dbr_ref, dbi_ref, dcr_ref, dci_ref, dar_ref, dai_ref, dd_ref, dgw_ref, dgb_ref, gr_s, gi_s):
                ref[...] = jnp.zeros_like(ref)

        us, ubs, carries, dy_of = {}, {}, {}, {}
        for j in later_first:
            us[j] = _permute_rows(perm[...], u_ref[rows[j], :])
            carries[j] = (cin_r[8 * j:8 * j + 1, :], cin_i[8 * j:8 * j + 1, :])
            ubs[j], _ = _s5_states(us[j], *carries[j], b_re, b_im, tab, *scratch[j][:2])
        for j in later_first:
            u = us[j]
            hr_s, hi_s, gr_t, gi_t = scratch[j]
            hrb, hib = _mx(hr_s[...]), _mx(hi_s[...])
            y = _dot(hrb, c_re[...]) - _dot(hib, c_im[...]) + d_ref[...] * u
            z = _gelu(y)
            zb = _mx(z)
            sg = _sigmoid(_dot(zb, gw_ref[...]) + gb_ref[...])
            dout = _permute_rows(perm[...], dy_ref[rows[j], :])
            dpre = dout * z * sg * (1.0 - sg)
            dgb_ref[...] += _sum0(dpre)
            dpb = _mx(dpre)
            dgw_ref[...] += _dot_tn(zb, dpb)
            dy = (dout * sg + _dot_nt(dpb, gw_ref[...])) * _gelu_grad(y)
            dd_ref[...] += _sum0(dy * u)
            dyb = _mx(dy)
            dcr_ref[...] += _dot_tn(hrb, dyb)
            dci_ref[...] -= _dot_tn(hib, dyb)
            gr_t[...] = _dot_nt(dyb, c_re[...])
            gi_t[...] = -_dot_nt(dyb, c_im[...])
            dy_of[j] = dy
        for j in later_first:
            gr_s[...], gi_s[...] = _cscan(*scratch[j][2:], tab, gr_s[...], gi_s[...], reverse=True)
        for j in later_first:
            hr_s, hi_s, gr_t, gi_t = scratch[j]
            cr, ci = carries[j]
            sub = _row_iota((N_SEG, N_STATE))
            acc_r = acc_i = jnp.zeros((N_SEG, N_STATE), F32)
            for k in range(SEG):
                if k == 0:
                    hpr = jnp.where(sub >= 1, pltpu.roll(hr_s[_seg_rows(SEG - 1), :], 1, 0), cr)
                    hpi = jnp.where(sub >= 1, pltpu.roll(hi_s[_seg_rows(SEG - 1), :], 1, 0), ci)
                else:
                    hpr, hpi = hr_s[_seg_rows(k - 1), :], hi_s[_seg_rows(k - 1), :]
                gr, gi = gr_t[_seg_rows(k), :], gi_t[_seg_rows(k), :]
                acc_r = acc_r + gr * hpr + gi * hpi
                acc_i = acc_i + gi * hpr - gr * hpi
            dar_ref[...] += _sum0(acc_r)
            dai_ref[...] += _sum0(acc_i)
            grb, gib = _mx(gr_t[...]), _mx(gi_t[...])
            dbr_ref[...] += _dot_tn(ubs[j], grb)
            dbi_ref[...] += _dot_tn(ubs[j], gib)
            du_ref[rows[j], :] = _permute_rows(perm_t[...], dy_of[j] * d_ref[...] + _dot_nt(grb, b_re[...]) + _dot_nt(gib, b_im[...]))

    r = functools.partial(_rows_spec, n_tiles=nt, reverse=True)
    return _call(
        body, "s5_bwd", nt,
        [r(S5_PAIR * TILE, D_S5), r(S5_PAIR * TILE, D_S5), r(S5_PAIR * 8, N_STATE), r(S5_PAIR * 8, N_STATE)] + [_whole()] * 17,
        [r(S5_PAIR * TILE, D_S5)] + [_whole()] * 9,
        [_sds((n, D_S5)), _sds((D_S5, N_STATE)), _sds((D_S5, N_STATE)), _sds((N_STATE, D_S5)), _sds((N_STATE, D_S5)),
         _sds((1, N_STATE)), _sds((1, N_STATE)), _sds((1, D_S5)), _sds((D_S5, D_S5)), _sds((1, D_S5))],
        scratch=[pltpu.VMEM((1, N_STATE), F32)] * 2 + [pltpu.VMEM((TILE, N_STATE), F32)] * (4 * S5_PAIR),
    )(u, dys, carry_re, carry_im, *[prm[k] for k in S5_TABLES + S5_WEIGHTS])


def _lru_gates(xr, halo, tile_index, cw_ref, cb_ref, wx_ref, wa_ref, bx_ref, ba_ref, sp_ref):
    ext = jnp.concatenate([halo, xr], axis=0)
    sh = [xr] + [_shift_down(ext, j, TILE) for j in (1, 2, 3)]
    xc = cb_ref[...] + cw_ref[3:4, :] * sh[0] + cw_ref[2:3, :] * sh[1] + cw_ref[1:2, :] * sh[2] + cw_ref[0:1, :] * sh[3]
    xb = _mx(xc)
    gx = _sigmoid(_dot(xb, wx_ref[...]) + bx_ref[...])
    ga = _sigmoid(_dot(xb, wa_ref[...]) + ba_ref[...])
    la = -LRU_C * ga * sp_ref[...]
    a = jnp.exp(la)
    start = (tile_index * TILE + _row_iota(xr.shape)) == 0
    mult = jnp.where(start, 1.0, jnp.sqrt(-jnp.tanh(la) * (a * a + 1.0)))
    return sh, xc, xb, gx, ga, a, mult, start


def _lru_fwd(xr, gate, prm):
    n = xr.shape[0]
    nt = n // TILE

    def body(x_ref, g_ref, cw_ref, cb_ref, wx_ref, wa_ref, bx_ref, ba_ref, sp_ref, y_ref, c_out, halo_s, c_s):
        first_tile = pl.program_id(0) == 0

        @pl.when(first_tile)
        def _():
            halo_s[...] = jnp.zeros_like(halo_s)
            c_s[...] = jnp.zeros_like(c_s)

        xr = x_ref[...]
        _, xc, _, gx, _, a, mult, _ = _lru_gates(xr, halo_s[...], pl.program_id(0), cw_ref, cb_ref, wx_ref, wa_ref, bx_ref, ba_ref,
                                                 sp_ref)
        halo_s[...] = xr[TILE - 8:]
        acum, h = _rscan(a, mult * gx * xc, reverse=False)
        c = c_s[...]
        c_out[...] = jnp.broadcast_to(c, (8, D_LRU))
        h = h + acum * c
        c_s[...] = h[TILE - 1:TILE]
        y_ref[...] = h * _gelu(g_ref[...])

    r = functools.partial(_rows_spec, n_tiles=nt)
    return _call(
        body, "lru_fwd", nt,
        [r(TILE, D_LRU), r(TILE, D_LRU)] + [_whole()] * 7,
        [r(TILE, D_LRU), r(8, D_LRU)],
        [_sds((n, D_LRU)), _sds((nt * 8, D_LRU))],
        scratch=[pltpu.VMEM((8, D_LRU), F32), pltpu.VMEM((1, D_LRU), F32)],
    )(xr, gate, prm['conv_w'], prm['conv_b'], prm['wx'], prm['wa'], prm['bx'], prm['ba'], prm['sp'])


def _lru_bwd(xr, gate, dyl, carry, prm):
    n = xr.shape[0]
    nt = n // TILE

    def body(x_ref, xh_ref, g_ref, dy_ref, cin_ref, cw_ref, cb_ref, wx_ref, wa_ref, bx_ref, ba_ref, sp_ref,
             dx_ref, dg_ref, dcw0, dcw1, dcw2, dcw3, dcb_ref, dwx_ref, dwa_ref, dbx_ref, dba_ref, dsp_ref, an_s, gn_s, dn_s):
        first_tile = pl.program_id(0) == nt - 1

        @pl.when(pl.program_id(0) == 0)
        def _():
            for ref in (dcw0, dcw1, dcw2, dcw3, dcb_ref, dwx_ref, dwa_ref, dbx_ref, dba_ref, dsp_ref, gn_s, dn_s):
                ref[...] = jnp.zeros_like(ref)
            an_s[...] = jnp.ones_like(an_s)

        xr = x_ref[...]
        halo = jnp.where(first_tile, 0.0, xh_ref[...])
        sh, xc, xb, gx, ga, a, mult, start = _lru_gates(xr, halo, nt - 1 - pl.program_id(0), cw_ref, cb_ref, wx_ref, wa_ref, bx_ref,
                                                        ba_ref, sp_ref)
        acum, h = _rscan(a, mult * gx * xc, reverse=False)
        cin = cin_ref[0:1, :]
        h = h + acum * cin
        gate = g_ref[...]
        dyl = dy_ref[...]
        dg_ref[...] = dyl * h * _gelu_grad(gate)
        row = _row_iota(xr.shape)
        alpha = jnp.where(row < TILE - 1, pltpu.roll(a, TILE - 1, 0), an_s[...])
        racc, g = _rscan(alpha, dyl * _gelu(gate), reverse=True)
        g = g + racc * gn_s[...]
        an_s[...] = a[0:1]
        gn_s[...] = g[0:1]
        hprev = jnp.where(row == 0, cin, pltpu.roll(h, 1, 0))
        da = g * hprev
        dmult = jnp.where(start, 0.0, g * gx * xc)
        dla = da * a - dmult * a * a / mult
        dsp_ref[...] += _sum0(-LRU_C * ga * dla)
        dpa = (-LRU_C * sp_ref[...] * dla) * ga * (1.0 - ga)
        dpx = (g * mult * xc) * gx * (1.0 - gx)
        dba_ref[...] += _sum0(dpa)
        dbx_ref[...] += _sum0(dpx)
        dpab, dpxb = _mx(dpa), _mx(dpx)
        dwa_ref[...] += _dot_tn(xb, dpab)
        dwx_ref[...] += _dot_tn(xb, dpxb)
        dxc = g * mult * gx + _dot_nt(dpab, wa_ref[...]) + _dot_nt(dpxb, wx_ref[...])
        dcb_ref[...] += _sum0(dxc)
        dcw3[...] += _sum0(dxc * sh[0])
        dcw2[...] += _sum0(dxc * sh[1])
        dcw1[...] += _sum0(dxc * sh[2])
        dcw0[...] += _sum0(dxc * sh[3])
        ext = jnp.concatenate([dxc, dn_s[...]], axis=0)
        dx_ref[...] = (cw_ref[3:4, :] * dxc + cw_ref[2:3, :] * _shift_up(ext, 1, TILE) + cw_ref[1:2, :] * _shift_up(ext, 2, TILE)
                       + cw_ref[0:1, :] * _shift_up(ext, 3, TILE))
        dn_s[...] = dxc[:8]

    r = functools.partial(_rows_spec, n_tiles=nt, reverse=True)
    vec = _sds((1, D_LRU))
    return _call(
        body, "lru_bwd", nt,
        [r(TILE, D_LRU), _halo_spec(D_LRU, TILE, nt, reverse=True), r(TILE, D_LRU), r(TILE, D_LRU), r(8, D_LRU)] + [_whole()] * 7,
        [r(TILE, D_LRU), r(TILE, D_LRU)] + [_whole()] * 10,
        [_sds((n, D_LRU)), _sds((n, D_LRU)), vec, vec, vec, vec, vec, _sds((D_LRU, D_LRU)), _sds((D_LRU, D_LRU)), vec, vec, vec],
        scratch=[pltpu.VMEM((1, D_LRU), F32), pltpu.VMEM((1, D_LRU), F32), pltpu.VMEM((8, D_LRU), F32)],
    )(xr, xr, gate, dyl, carry, prm['conv_w'], prm['conv_b'], prm['wx'], prm['wa'], prm['bx'], prm['ba'], prm['sp'])


def _normed_parts(ya, ys, yl):
    return jnp.concatenate([ya * _rms(ya), ys * _rms(ys), yl * _rms(yl)], axis=1)


def _mixout_fwd(ya, ys, yl, x0, g_mix, w_out, b_out, g1, b1):
    n = x0.shape[0]
    nt = n // TILE_WIDE

    def body(ya_ref, ys_ref, yl_ref, x_ref, gm_ref, w_ref, b_ref, g_ref, be_ref, mix_ref, r_ref, x1_ref):
        mixb = _mx(_normed_parts(ya_ref[...], ys_ref[...], yl_ref[...]) * gm_ref[...])
        mix_ref[...] = mixb
        r1 = ALPHA * x_ref[...] + _dot(mixb, w_ref[...]) + b_ref[...]
        r_ref[...] = r1
        xhat, _ = _ln_stats(r1)
        x1_ref[...] = xhat * g_ref[...] + be_ref[...]

    r = functools.partial(_rows_spec, n_tiles=nt)
    return _call(
        body, "mixout_fwd", nt,
        [r(TILE_WIDE, D_ATTN), r(TILE_WIDE, D_S5), r(TILE_WIDE, D_LRU), r(TILE_WIDE, D)] + [_whole()] * 5,
        [r(TILE_WIDE, D), r(TILE_WIDE, D), r(TILE_WIDE, D)],
        [_sds((n, D), MXU), _sds((n, D)), _sds((n, D))],
    )(ya, ys, yl, x0, g_mix, w_out, b_out, g1, b1)


def _mixout_bwd(dr1, mix, ya, ys, yl, g_mix, w_out):
    n = dr1.shape[0]
    nt = n // TILE_WIDE

    def body(dr_ref, mix_ref, ya_ref, ys_ref, yl_ref, gm_ref, w_ref, dya_ref, dys_ref, dyl_ref, dw_ref, db_ref, dgm_ref):
        @pl.when(pl.program_id(0) == 0)
        def _():
            for ref in (dw_ref, db_ref, dgm_ref):
                ref[...] = jnp.zeros_like(ref)

        dr = dr_ref[...]
        db_ref[...] += _sum0(dr)
        drb = _mx(dr)
        dw_ref[...] += _dot_tn(mix_ref[...], drb)
        dmix = _dot(drb, w_ref[...])
        parts = (ya_ref[...], ys_ref[...], yl_ref[...])
        dgm_ref[...] += _sum0(dmix * _normed_parts(*parts))
        dn = dmix * gm_ref[...]
        lo = 0
        for y, out in zip(parts, (dya_ref, dys_ref, dyl_ref)):
            w = y.shape[1]
            rs = _rms(y)
            nrm = y * rs
            dnp = dn[:, lo:lo + w]
            out[...] = rs * (dnp - nrm * jnp.mean(dnp * nrm, axis=-1, keepdims=True))
            lo += w

    r = functools.partial(_rows_spec, n_tiles=nt)
    return _call(
        body, "mixout_bwd", nt,
        [r(TILE_WIDE, D), r(TILE_WIDE, D), r(TILE_WIDE, D_ATTN), r(TILE_WIDE, D_S5), r(TILE_WIDE, D_LRU), _whole(), _whole()],
        [r(TILE_WIDE, D_ATTN), r(TILE_WIDE, D_S5), r(TILE_WIDE, D_LRU), _whole(), _whole(), _whole()],
        [_sds((n, D_ATTN)), _sds((n, D_S5)), _sds((n, D_LRU)), _sds((D, D)), _sds((1, D)), _sds((1, D))],
    )(dr1, mix, ya, ys, yl, g_mix, w_out)


def _ffn_conv(gp, halo, cw_ref, cb_ref, cs):
    ext = jnp.concatenate([halo, gp], axis=0)
    s1 = _shift_down(ext, 1, TILE)
    s2 = _shift_down(ext, 2, TILE)
    return s1, s2, cb_ref[:, cs] + cw_ref[2:3, cs] * gp + cw_ref[1:2, cs] * s1 + cw_ref[0:1, cs] * s2


def _ffn_fwd(x1, wg, wu, cw, cb, wd, g2, b2):
    n = x1.shape[0]
    nt = n // TILE

    chunks = D_FF // FF_CHUNK

    def body(x_ref, wg_hbm, wu_hbm, cw_ref, cb_ref, wd_hbm, g_ref, be_ref, gp_ref, up_ref, r_ref, x2_ref, halo_s, act_s,
             wg_ref, wu_ref, wd_ref, sems):
        first = pl.program_id(0) == 0

        def fetch(which, c):
            cs = pl.ds(c * FF_CHUNK, FF_CHUNK)
            if which == 2:
                return pltpu.make_async_copy(wd_hbm.at[cs, :], wd_ref.at[cs, :], sems.at[2, c])
            src, dst = ((wg_hbm, wg_ref), (wu_hbm, wu_ref))[which]
            return pltpu.make_async_copy(src.at[:, cs], dst.at[:, cs], sems.at[which, c])

        @pl.when(first)
        def _():
            halo_s[...] = jnp.zeros_like(halo_s)
            for which in (0, 1, 2):
                for c in range(chunks):
                    fetch(which, c).start()

        x1 = x_ref[...]
        xb = _mx(x1)
        for c in range(chunks):
            @pl.when(first)
            def _():
                fetch(0, c).wait()
                fetch(1, c).wait()

            cs = slice(c * FF_CHUNK, (c + 1) * FF_CHUNK)
            gp = _dot(xb, wg_ref[:, cs])
            up = _dot(xb, wu_ref[:, cs])
            gp_ref[:, cs] = gp
            up_ref[:, cs] = up
            _, _, gc = _ffn_conv(gp, halo_s[:, cs], cw_ref, cb_ref, cs)
            halo_s[:, cs] = gp[TILE - 8:]
            act_s[:, cs] = _mx(gc * _sigmoid(gc) * up)

        @pl.when(first)
        def _():
            for c in range(chunks):
                fetch(2, c).wait()

        r2 = ALPHA * x1 + _dot(act_s[...], wd_ref[...])
        r_ref[...] = r2
        xhat, _ = _ln_stats(r2)
        x2_ref[...] = xhat * g_ref[...] + be_ref[...]

    r = functools.partial(_rows_spec, n_tiles=nt)
    hbm = pl.BlockSpec(memory_space=pl.ANY)
    return _call(
        body, "ffn_fwd", nt,
        [r(TILE, D), hbm, hbm, _whole(), _whole(), hbm, _whole(), _whole()],
        [r(TILE, D_FF), r(TILE, D_FF), r(TILE, D), r(TILE, D)],
        [_sds((n, D_FF)), _sds((n, D_FF)), _sds((n, D)), _sds((n, D))],
        scratch=[pltpu.VMEM((8, D_FF), F32), pltpu.VMEM((TILE, D_FF), MXU), pltpu.VMEM((D, D_FF), MXU), pltpu.VMEM((D, D_FF), MXU),
                 pltpu.VMEM((D_FF, D), MXU), pltpu.SemaphoreType.DMA((3, chunks))],
    )(x1, wg, wu, cw, cb, wd, g2, b2)


def _ffn_bwd_down(dx2, r2, g2, gp, up, cw, cb, wd_t):
    n = dx2.shape[0]
    nt = n // TILE

    def body(dx_ref, r_ref, g_ref, gp_ref, gh_ref, up_ref, cw_ref, cb_ref, wd_ref,
             dr_ref, dgp_ref, dup_ref, dwd_ref, dcw0, dcw1, dcw2, dcb_ref, dg_ref, db_ref, next_s):
        first_tile = pl.program_id(0) == nt - 1

        @pl.when(pl.program_id(0) == 0)
        def _():
            for ref in (dwd_ref, dcw0, dcw1, dcw2, dcb_ref, dg_ref, db_ref, next_s):
                ref[...] = jnp.zeros_like(ref)

        dx2 = dx_ref[...]
        xhat, rstd = _ln_stats(r_ref[...])
        dg_ref[...] += _sum0(dx2 * xhat)
        db_ref[...] += _sum0(dx2)
        dr2 = _ln_bwd(dx2, g_ref[...], xhat, rstd)
        dr_ref[...] = dr2
        dfb = _mx(dr2)
        for c in range(D_FF // FF_CHUNK):
            cs = slice(c * FF_CHUNK, (c + 1) * FF_CHUNK)
            gp = gp_ref[:, cs]
            up = up_ref[:, cs]
            s1, s2, gc = _ffn_conv(gp, jnp.where(first_tile, 0.0, gh_ref[:, cs]), cw_ref, cb_ref, cs)
            sg = _sigmoid(gc)
            silu = gc * sg
            dact = _dot(dfb, wd_ref[:, cs])
            dwd_ref[cs, :] += _dot_tn(_mx(silu * up), dfb)
            dup_ref[:, cs] = _mx(dact * silu)
            dgc = dact * up * (sg + silu * (1.0 - sg))
            dcb_ref[:, cs] += _sum0(dgc)
            dcw2[:, cs] += _sum0(dgc * gp)
            dcw1[:, cs] += _sum0(dgc * s1)
            dcw0[:, cs] += _sum0(dgc * s2)
            ext = jnp.concatenate([dgc, next_s[:, cs]], axis=0)
            dgp_ref[:, cs] = _mx(cw_ref[2:3, cs] * dgc + cw_ref[1:2, cs] * _shift_up(ext, 1, TILE)
                                 + cw_ref[0:1, cs] * _shift_up(ext, 2, TILE))
            next_s[:, cs] = dgc[:8]

    r = functools.partial(_rows_spec, n_tiles=nt, reverse=True)
    vff = _sds((1, D_FF))
    return _call(
        body, "ffn_bwd_down", nt,
        [r(TILE, D), r(TILE, D), _whole(), r(TILE, D_FF), _halo_spec(D_FF, TILE, nt, reverse=True), r(TILE, D_FF), _whole(), _whole(),
         _whole()],
        [r(TILE, D), r(TILE, D_FF), r(TILE, D_FF)] + [_whole()] * 7,
        [_sds((n, D)), _sds((n, D_FF), MXU), _sds((n, D_FF), MXU), _sds((D_FF, D)), vff, vff, vff, vff, _sds((1, D)), _sds((1, D))],
        scratch=[pltpu.VMEM((8, D_FF), F32)],
    )(dx2, r2, g2, gp, gp, up, cw, cb, wd_t)


def _ffn_bwd_dx(dr2, dgp, dup, r1, g1, wg_t, wu_t):
    n = dr2.shape[0]
    rows = TILE_BIG
    nt = n // rows

    def body(dr2_ref, dgp_ref, dup_ref, r_ref, g_ref, wg_ref, wu_ref, dr1_ref, dg_ref, db_ref):
        @pl.when(pl.program_id(0) == 0)
        def _():
            for ref in (dg_ref, db_ref):
                ref[...] = jnp.zeros_like(ref)

        dx1 = ALPHA * dr2_ref[...] + _dot(dgp_ref[...], wg_ref[...]) + _dot(dup_ref[...], wu_ref[...])
        xhat, rstd = _ln_stats(r_ref[...])
        dg_ref[...] += _sum0(dx1 * xhat)
        db_ref[...] += _sum0(dx1)
        dr1_ref[...] = _ln_bwd(dx1, g_ref[...], xhat, rstd)

    r = functools.partial(_rows_spec, n_tiles=nt)
    return _call(
        body, "ffn_bwd_dx", nt,
        [r(rows, D), r(rows, D_FF), r(rows, D_FF), r(rows, D), _whole(), _whole(), _whole()],
        [r(rows, D), _whole(), _whole()],
        [_sds((n, D)), _sds((1, D)), _sds((1, D))],
    )(dr2, dgp, dup, r1, g1, wg_t, wu_t)


def _ffn_bwd_dw(x1, dgp, dup):
    n = x1.shape[0]
    rows = TILE_BIG
    nt = n // rows

    def body(x_ref, dgp_ref, dup_ref, dwg_ref, dwu_ref):
        @pl.when(pl.program_id(0) == 0)
        def _():
            for ref in (dwg_ref, dwu_ref):
                ref[...] = jnp.zeros_like(ref)

        xb = _mx(x_ref[...])
        for c in range(D_FF // FF_CHUNK):
            cs = slice(c * FF_CHUNK, (c + 1) * FF_CHUNK)
            dwg_ref[cs, :] += _dot_tn(dgp_ref[:, cs], xb)
            dwu_ref[cs, :] += _dot_tn(dup_ref[:, cs], xb)

    r = functools.partial(_rows_spec, n_tiles=nt)
    return _call(
        body, "ffn_bwd_dw", nt,
        [r(rows, D), r(rows, D_FF), r(rows, D_FF)], [_whole(), _whole()], [_sds((D_FF, D)), _sds((D_FF, D))],
    )(x1, dgp, dup)


def _loss_head(y, target):
    n = y.shape[0]
    nt = n // TILE_WIDE

    def body(y_ref, t_ref, loss_ref, dy_ref):
        @pl.when(pl.program_id(0) == 0)
        def _():
            loss_ref[...] = jnp.zeros_like(loss_ref)

        e = y_ref[...] - t_ref[...]
        dy_ref[...] = e * (1.0 / D)
        loss_ref[...] += _sum0(jnp.sum(e * e, axis=1, keepdims=True)) * (0.5 / D)

    r = functools.partial(_rows_spec, n_tiles=nt)
    return _call(body, "loss_head", nt, [r(TILE_WIDE, D), r(TILE_WIDE, D)], [_whole(), r(TILE_WIDE, D)],
                 [_sds((1, 1)), _sds((n, D))])(y, target)


def _place():
    x, y, c = lax.axis_index("x"), lax.axis_index("y"), lax.axis_index("c")
    return x, y, c, 4 * x + 2 * y + c


def _peer(x, y, c, k):
    px, py, pc = x ^ ((k >> 2) & 1), y ^ ((k >> 1) & 1), c ^ (k & 1)
    return (px, py, pc), 4 * px + 2 * py + pc


def _all_gather(name, blocks, small):
    srcs = list(blocks) + [small]
    n = len(srcs)
    out_shapes = [_sds((a.shape[0], N_DEV * a.shape[1], LANES), a.dtype) for a in blocks] + [_sds((N_DEV,) + small.shape, small.dtype)]

    def body(*refs):
        src_refs, out_refs = refs[:n], refs[n:2 * n]
        send_sems, recv_sems, local_sems = refs[2 * n:]
        x, y, c, me = _place()

        def landing(a, slot):
            if a == n - 1:
                return out_refs[a].at[slot]
            r = src_refs[a].shape[1]
            return out_refs[a].at[:, pl.ds(slot * r, r), :]

        def remote(a, k, slot):
            peer, _ = _peer(x, y, c, k)
            return pltpu.make_async_remote_copy(
                src_ref=src_refs[a], dst_ref=landing(a, slot), send_sem=send_sems.at[a * N_DEV + k],
                recv_sem=recv_sems.at[a * N_DEV + k], device_id=peer, device_id_type=pl.DeviceIdType.MESH)

        mine = [pltpu.make_async_copy(src_refs[a], landing(a, me), local_sems.at[a]) for a in range(n)]
        sends = [remote(a, k, me) for a in range(n) for k in range(1, N_DEV)]
        for cp in mine + sends:
            cp.start()
        for a in range(n):
            for k in range(1, N_DEV):
                remote(a, k, _peer(x, y, c, k)[1]).wait_recv()
        for cp in sends:
            cp.wait_send()
        for cp in mine:
            cp.wait()

    any_space = pl.BlockSpec(memory_space=pl.ANY)
    return pl.pallas_call(
        body, name=name, out_shape=out_shapes, in_specs=[any_space] * n, out_specs=[any_space] * n,
        scratch_shapes=[pltpu.SemaphoreType.DMA((n * N_DEV,)), pltpu.SemaphoreType.DMA((n * N_DEV,)), pltpu.SemaphoreType.DMA((n,))],
    )(*srcs)


_HBM = pl.BlockSpec(memory_space=pltpu.HBM)
_SEM = pl.BlockSpec(memory_space=pltpu.SEMAPHORE)
_EFFECT = pltpu.SideEffectType.DATAFLOW_SIDE_EFFECTING


def _in_hbm(a):
    return pltpu.with_memory_space_constraint(a, pltpu.HBM)


def _scatter_start(name, srcs):
    ns = len(srcs)
    rows_a = [a.shape[0] // N_DEV for a in srcs]
    offs = [sum(rows_a[:a]) for a in range(ns)]
    total = sum(rows_a)

    def body(*refs):
        src_refs, land_ref, send_sems, recv_sems, token = refs[:ns], refs[ns], refs[ns + 1], refs[ns + 2], refs[-1]
        x, y, c, me = _place()
        for a in range(ns):
            pltpu.make_async_copy(src_refs[a].at[pl.ds(me * rows_a[a], rows_a[a]), :],
                                  land_ref.at[me, pl.ds(offs[a], rows_a[a]), :], send_sems.at[0]).start()
        for k in range(1, N_DEV):
            peer, peer_slot = _peer(x, y, c, k)
            for a in range(ns):
                pltpu.make_async_remote_copy(
                    src_ref=src_refs[a].at[pl.ds(peer_slot * rows_a[a], rows_a[a]), :],
                    dst_ref=land_ref.at[me, pl.ds(offs[a], rows_a[a]), :], send_sem=send_sems.at[k], recv_sem=recv_sems.at[k],
                    device_id=peer, device_id_type=pl.DeviceIdType.MESH).start()
        token[...] = jnp.zeros_like(token)

    landing = lax.empty((N_DEV, total, LANES), F32)
    out = pl.pallas_call(
        body, name=name,
        out_shape=(pltpu.SemaphoreType.DMA((N_DEV,)), pltpu.SemaphoreType.DMA((N_DEV,)), *[pltpu.HBM(a.shape, a.dtype) for a in srcs],
                   pltpu.HBM(landing.shape, F32), _sds((8, 128))),
        in_specs=[_HBM] * (ns + 1), out_specs=(_SEM, _SEM, *[_HBM] * (ns + 1), pl.BlockSpec(memory_space=pltpu.VMEM)),
        input_output_aliases={a: 2 + a for a in range(ns + 1)},
        compiler_params=pltpu.CompilerParams(has_side_effects=_EFFECT),
    )(*[_in_hbm(a) for a in srcs], _in_hbm(landing))
    return out[0], out[1], out[2:2 + ns], out[2 + ns], out[-1]


def _scatter_wait(name, send_sems, recv_sems, srcs, landing, after):
    ns = len(srcs)

    def body(*refs):
        land_ref, send_ref, recv_ref = refs[ns], refs[ns + 1], refs[ns + 2]
        x, y, c, me = _place()
        pltpu.make_async_copy(land_ref.at[me], land_ref.at[me], send_ref.at[0]).wait()
        for k in range(1, N_DEV):
            peer, peer_slot = _peer(x, y, c, k)
            slot = pltpu.make_async_remote_copy(
                src_ref=land_ref.at[me], dst_ref=land_ref.at[peer_slot], send_sem=send_ref.at[k], recv_sem=recv_ref.at[k],
                device_id=peer, device_id_type=pl.DeviceIdType.MESH)
            slot.wait_send()
            slot.wait_recv()

    out = pl.pallas_call(
        body, name=name, out_shape=(*[pltpu.HBM(a.shape, a.dtype) for a in srcs], pltpu.HBM(landing.shape, landing.dtype)),
        in_specs=[_HBM] * (ns + 1) + [_SEM, _SEM, pl.BlockSpec(memory_space=pl.ANY)], out_specs=[_HBM] * (ns + 1),
        input_output_aliases={a: a for a in range(ns + 1)},
        compiler_params=pltpu.CompilerParams(has_side_effects=_EFFECT),
    )(*srcs, landing, send_sems, recv_sems, after)
    return out[:ns], out[ns]


def _gather_start(name, blocks):
    n = len(blocks)

    def body(*refs):
        src_refs, land_refs, send_sems, recv_sems, token = refs[:n], refs[n:2 * n], refs[2 * n], refs[2 * n + 1], refs[-1]
        x, y, c, me = _place()
        for a in range(n):
            r = src_refs[a].shape[1]
            mine = land_refs[a].at[:, pl.ds(me * r, r), :]
            pltpu.make_async_copy(src_refs[a], mine, send_sems.at[a * N_DEV]).start()
            for k in range(1, N_DEV):
                pltpu.make_async_remote_copy(
                    src_ref=src_refs[a], dst_ref=mine, send_sem=send_sems.at[a * N_DEV + k],
                    recv_sem=recv_sems.at[a * N_DEV + k], device_id=_peer(x, y, c, k)[0], device_id_type=pl.DeviceIdType.MESH).start()
        token[...] = jnp.zeros_like(token)

    wholes = [lax.empty((a.shape[0], N_DEV * a.shape[1], LANES), a.dtype) for a in blocks]
    out = pl.pallas_call(
        body, name=name,
        out_shape=(pltpu.SemaphoreType.DMA((n * N_DEV,)), pltpu.SemaphoreType.DMA((n * N_DEV,)),
                   *[pltpu.HBM(a.shape, a.dtype) for a in blocks + wholes], _sds((8, 128))),
        in_specs=[_HBM] * (2 * n), out_specs=(_SEM, _SEM, *[_HBM] * (2 * n), pl.BlockSpec(memory_space=pltpu.VMEM)),
        input_output_aliases={a: 2 + a for a in range(2 * n)},
        compiler_params=pltpu.CompilerParams(has_side_effects=_EFFECT),
    )(*[_in_hbm(a) for a in blocks + wholes])
    return out[0], out[1], out[2:2 + n], out[2 + n:2 + 2 * n], out[-1]


def _gather_wait(name, send_sems, recv_sems, blocks, wholes, after):
    n = len(blocks)

    def body(*refs):
        src_refs, land_refs, send_ref, recv_ref = refs[:n], refs[n:2 * n], refs[2 * n], refs[2 * n + 1]
        x, y, c, me = _place()
        for a in range(n):
            r = src_refs[a].shape[1]
            pltpu.make_async_copy(src_refs[a], land_refs[a].at[:, pl.ds(me * r, r), :], send_ref.at[a * N_DEV]).wait()
            for k in range(1, N_DEV):
                peer, peer_slot = _peer(x, y, c, k)
                cp = pltpu.make_async_remote_copy(
                    src_ref=src_refs[a], dst_ref=land_refs[a].at[:, pl.ds(peer_slot * r, r), :], send_sem=send_ref.at[a * N_DEV + k],
                    recv_sem=recv_ref.at[a * N_DEV + k], device_id=peer, device_id_type=pl.DeviceIdType.MESH)
                cp.wait_send()
                cp.wait_recv()

    return pl.pallas_call(
        body, name=name, out_shape=tuple(pltpu.HBM(a.shape, a.dtype) for a in list(blocks) + list(wholes)),
        in_specs=[_HBM] * (2 * n) + [_SEM, _SEM, pl.BlockSpec(memory_space=pl.ANY)], out_specs=[_HBM] * (2 * n),
        input_output_aliases={a: a for a in range(2 * n)},
        compiler_params=pltpu.CompilerParams(has_side_effects=_EFFECT),
    )(*blocks, *wholes, send_sems, recv_sems, after)


def _reduce_adamw(parts, w, m, v, tile_rows=PACK_TILE):
    rows = w.shape[0]
    nt = rows // tile_rows
    slots = parts.shape[0]
    c1 = 1.0 - ADAM_B1 ** ADAM_STEP
    c2 = 1.0 - ADAM_B2 ** ADAM_STEP

    def body(p_ref, w_ref, m_ref, v_ref, g_out, d_out, m_out, v_out):
        g = p_ref[0]
        for s in range(1, slots):
            g = g + p_ref[s]
        m_new = ADAM_B1 * m_ref[...] + (1.0 - ADAM_B1) * g
        v_new = ADAM_B2 * v_ref[...] + (1.0 - ADAM_B2) * (g * g)
        g_out[...] = g
        m_out[...] = m_new
        v_out[...] = v_new
        d_out[...] = -ADAM_LR * ((m_new / c1) / (jnp.sqrt(v_new / c2) + ADAM_EPS) + ADAM_WD * w_ref[...])

    r = _rows_spec(tile_rows, LANES, nt)
    out = _sds((rows, LANES))
    return _call(
        body, "reduce_adamw", nt,
        [pl.BlockSpec((slots, tile_rows, LANES), lambda i: (0, i, 0)), r, r, r], [r, r, r, r], [out, out, out, out],
    )(parts, w, m, v)


def _pack_rows(a, lead=0):
    head = a.shape[:lead]
    flat = a.reshape(head + (-1,))
    size = flat.shape[-1]
    rows = -(-size // (16 * LANES)) * 16
    flat = jnp.pad(flat, [(0, 0)] * lead + [(0, rows * LANES - size)])
    return flat.reshape(head + (rows, LANES))


def _packed_rows(shape):
    return -(-math.prod(shape) // (16 * LANES)) * 16


def _to_blocks(full, axis):
    l, a, b = full.shape
    if axis == 2:
        return full.reshape(l, a, N_DEV, b // N_DEV).transpose(2, 0, 1, 3)
    return full.reshape(l, N_DEV, a // N_DEV, b).transpose(1, 0, 2, 3)


def _from_blocks(blocks, axis):
    _, l, a, b = blocks.shape
    if axis == 2:
        return blocks.transpose(1, 2, 0, 3).reshape(l, a, N_DEV * b)
    return blocks.transpose(1, 0, 2, 3).reshape(l, N_DEV * a, b)


def _row_form(shard, transposed):
    return shard.transpose(0, 2, 1) if transposed else shard


def _me():
    return 4 * lax.axis_index("x") + 2 * lax.axis_index("y") + lax.axis_index("c")


def _both_forms(names, wholes, layer):
    out = {}
    for name, w in zip(names, wholes):
        t = dict(BIG)[name]
        out[name + '_t' if t else name] = w[layer]
        out[name if t else name + '_t'] = w[layer].T
    return out


def _gather_weights(local):
    segs, meta = [], []
    for name in SMALL_SHARDED:
        blk = local[name]
        if name in GATHER_F32:
            bits = lax.bitcast_convert_type(blk, MXU)
        else:
            bits = _mx(blk)
        seg = _pack_rows(bits)
        meta.append((name, bits.shape, seg.shape[0]))
        segs.append(seg)
    blocks = {name: _mx(_row_form(local[name], t)) for name, t in BIG}
    mix, ffn = PARTS['mix'], PARTS['ffn']
    *first, gathered = _all_gather("gather_weights", [blocks[n][:1] for n in mix], jnp.concatenate(segs, axis=0))
    flights = {'ffn0': _gather_start("gather_ffn0_start", [blocks[n][:1] for n in ffn]),
               'later': _gather_start("gather_later_start", [blocks[n][1:] for n in mix + ffn])}
    out, lo = {}, 0
    for name, bits_shape, rows in meta:
        seg = gathered[:, lo:lo + rows].reshape(N_DEV, -1)[:, :math.prod(bits_shape)].reshape((N_DEV,) + bits_shape)
        if name in GATHER_F32:
            seg = lax.bitcast_convert_type(seg, F32)
        out[name] = _from_blocks(seg, SHARD_AXIS[name])
        lo += rows
    ready = {(0, 'mix'): _both_forms(mix, first, 0)}

    def landed(flight, names, after):
        send_sems, recv_sems, mine, wholes, _ = flights[flight]
        return _gather_wait(f"gather_{flight}_wait", send_sems, recv_sems, mine, wholes, after)[len(names):]

    def big_weights(l, part, after):
        if (l, part) not in ready and l == 0:
            ready[(0, 'ffn')] = _both_forms(ffn, landed('ffn0', ffn, after), 0)
        elif (l, part) not in ready:
            wholes = landed('later', mix + ffn, after)
            for j in range(1, DEPTH):
                forms = _both_forms(mix + ffn, wholes, j - 1)
                for p, names in PARTS.items():
                    ready[(j, p)] = {k: forms[k] for n in names for k in (n, n + '_t')}
        return ready[(l, part)]

    return out, big_weights, flights['ffn0'][-1][0, 0] + flights['later'][-1][0, 0]


def _s5_discretize(a_re, a_im, log_dt, b_re, b_im):
    lam_re = jnp.minimum(a_re, -1e-4)
    lam_im = a_im
    dt = jnp.exp(log_dt)[:, None]
    decay = jnp.exp(dt * lam_re)
    ang = dt * lam_im
    abar_re = decay * jnp.cos(ang)
    abar_im = decay * jnp.sin(ang)
    den = jnp.square(lam_re) + jnp.square(lam_im)
    nr = abar_re - 1.0
    ni = abar_im
    coef_re = (nr * lam_re + ni * lam_im) / den
    coef_im = (ni * lam_re - nr * lam_im) / den
    bbar_re = coef_re[..., None] * b_re - coef_im[..., None] * b_im
    bbar_im = coef_re[..., None] * b_im + coef_im[..., None] * b_re
    return abar_re, abar_im, bbar_re, bbar_im


def _complex_powers(ar, ai, count):
    exponents = jnp.arange(1, count + 1).reshape((count,) + (1,) * ar.ndim)
    pr = jnp.ones((count,) + ar.shape, F32)
    pi = jnp.zeros_like(pr)
    sr, si = ar, ai
    for b in range(count.bit_length()):
        bit = ((exponents >> b) & 1) == 1
        fr, fi = jnp.where(bit, sr[None], 1.0), jnp.where(bit, si[None], 0.0)
        pr, pi = pr * fr - pi * fi, pr * fi + pi * fr
        sr, si = sr * sr - si * si, 2.0 * sr * si
    return pr, pi


_EYE16 = functools.partial(jnp.eye, 16, dtype=F32)


def _s5_params(p):
    disc, disc_vjp = jax.vjp(jax.vmap(_s5_discretize), p['s5_a_re'], p['s5_a_im'], p['s5_log_dt'], p['s5_b_re'], p['s5_b_im'])
    abar_re, abar_im, bbar_re, bbar_im = disc
    pw_re, pw_im = _complex_powers(abar_re.reshape(DEPTH, N_STATE), abar_im.reshape(DEPTH, N_STATE), SEG)
    sp_re, sp_im = _complex_powers(pw_re[SEG - 1], pw_im[SEG - 1], N_SEG - 1)
    one, zero = jnp.ones((1, DEPTH, N_STATE), F32), jnp.zeros((1, DEPTH, N_STATE), F32)
    seg_re = jnp.concatenate([one, sp_re], axis=0)
    seg_im = jnp.concatenate([zero, sp_im], axis=0)
    doubling = [0, 1, 3]
    blank = jnp.zeros((5, DEPTH, N_STATE), F32)
    tables = {
        'pw_re': pw_re, 'pw_im': pw_im,
        'dbl_re': jnp.concatenate([jnp.stack([sp_re[k] for k in doubling]), blank], axis=0),
        'dbl_im': jnp.concatenate([jnp.stack([sp_im[k] for k in doubling]), blank], axis=0),
        'seg_re': seg_re, 'seg_im': seg_im, 'segr_re': seg_re[::-1], 'segr_im': seg_im[::-1],
    }
    weights = {
        'b_re': _mx(jnp.einsum('lgpc,gh->lgchp', bbar_re, _EYE16()).reshape(DEPTH, D_S5, N_STATE)),
        'b_im': _mx(jnp.einsum('lgpc,gh->lgchp', bbar_im, _EYE16()).reshape(DEPTH, D_S5, N_STATE)),
        'c_re': _mx(jnp.einsum('lgcp,gh->lgphc', p['s5_c_re'], _EYE16()).reshape(DEPTH, N_STATE, D_S5)),
        'c_im': _mx(jnp.einsum('lgcp,gh->lgphc', p['s5_c_im'], _EYE16()).reshape(DEPTH, N_STATE, D_S5)),
    }
    src = (jnp.arange(TILE) % N_SEG) * SEG + jnp.arange(TILE) // N_SEG
    perm = (src[:, None] == jnp.arange(TILE)[None, :]).astype(MXU)
    layers = []
    for l in range(DEPTH):
        prm = {k: v[:, l] for k, v in tables.items()}
        prm.update({k: v[l] for k, v in weights.items()})
        prm.update({'perm': perm, 'perm_t': perm.T, 'd': p['s5_d'][l][None, :], 'glu_w': p['s5_glu_w'][l],
                    'glu_b': p['s5_glu_b'][l][None, :]})
        layers.append(prm)
    return layers, disc_vjp


def _lru_params(p, l):
    eye4 = jnp.eye(4, dtype=F32)
    return {
        'conv_w': p['lru_conv_w'][l], 'conv_b': p['lru_conv_b'][l][None, :],
        'wx': _mx(jnp.einsum('hij,hk->hikj', p['lru_wx'][l], eye4).reshape(D_LRU, D_LRU)),
        'wa': _mx(jnp.einsum('hij,hk->hikj', p['lru_wa'][l], eye4).reshape(D_LRU, D_LRU)),
        'bx': p['lru_bx'][l][None, :], 'ba': p['lru_ba'][l][None, :],
        'sp': jax.nn.softplus(-p['lru_a_param'][l])[None, :],
    }


def _rope_tables(n):
    inv_freq = ROPE_THETA ** (-jnp.arange(0, 64, 2, dtype=F32) / 64)
    ang = jnp.arange(n, dtype=F32)[:, None] * inv_freq[None, :]
    cos, sin = jnp.cos(ang), jnp.sin(ang)
    return jnp.concatenate([cos, cos, cos, cos], axis=1), jnp.concatenate([-sin, sin, -sin, sin], axis=1)


def _sink_cols(sinks):
    nb = TILE_Q // ATTN_BLOCK
    per_unit = sinks.reshape(4, 2).T
    return jnp.broadcast_to(per_unit[:, None, :, None, None], (2, nb, 4, ATTN_BLOCK, 128)).reshape(2, nb * 4 * ATTN_BLOCK, 128)


def _local_step(x, target, p, big_weights=None, emit_grads=None):
    if big_weights is None:
        big_weights = lambda l, part, after: {k: p[k][l] for name in PARTS[part] for k in (name, name + '_t')}
    if emit_grads is None:
        emit_grads = lambda l, part, grads: 0.0
    n = x.shape[0]
    cos_t, sin_t = _rope_tables(n)
    s5_layers, s5_vjp = _s5_params(p)
    row = lambda a: a[None, :]
    saved = []
    h = x
    for l in range(DEPTH):
        s = {'x0': h}
        bw = s['bw'] = dict(big_weights(l, 'mix', h))
        s['q'], k, v, s['u'], s['xr'], s['gate'] = _inproj_fwd(h, bw['w_in'], row(p['b_in'][l]), cos_t, sin_t)
        no_keys = jnp.zeros((ATTN_BLOCK, D_KV), MXU)
        s['k'], s['v'] = jnp.concatenate([no_keys, k], axis=0), jnp.concatenate([no_keys, v], axis=0)
        s['sinks'] = _sink_cols(p['attn_sinks'][l])
        s['ya'] = _attn_fwd(s['q'], s['k'], s['v'], s['sinks'])
        s['s5'] = s5_layers[l]
        s['lru'] = _lru_params(p, l)
        s['ys'], s['s5_cr'], s['s5_ci'] = _s5_fwd(s['u'], s['s5'])
        s['yl'], s['lru_c'] = _lru_fwd(s['xr'], s['gate'], s['lru'])
        s['mix'], s['r1'], s['x1'] = _mixout_fwd(s['ya'], s['ys'], s['yl'], h, row(p['mix_norm_g'][l]), bw['w_out'],
                                                 row(p['b_out'][l]), row(p['ln1_g'][l]), row(p['ln1_b'][l]))
        bw.update(big_weights(l, 'ffn', s['x1']))
        s['gp'], s['up'], s['r2'], h = _ffn_fwd(s['x1'], bw['ffn_w_gate'], bw['ffn_w_up'], p['ffn_conv_w'][l],
                                                row(p['ffn_conv_b'][l]), bw['ffn_w_down'], row(p['ln2_g'][l]), row(p['ln2_b'][l]))
        saved.append(s)
    loss, dh = _loss_head(h, target)
    placed = 0.0

    grads = {name: [None] * DEPTH for name in WEIGHTS}
    d_disc = [None] * DEPTH
    for l in reversed(range(DEPTH)):
        s = saved[l]
        g = {}
        (dr2, dgp, dup, g['ffn_w_down'], cw0, cw1, cw2, dcb, dg2, db2) = _ffn_bwd_down(
            dh, s['r2'], row(p['ln2_g'][l]) + placed, s['gp'], s['up'], p['ffn_conv_w'][l], row(p['ffn_conv_b'][l]),
            s['bw']['ffn_w_down_t'])
        g['ffn_conv_w'] = jnp.concatenate([cw0, cw1, cw2], axis=0)
        g['ffn_conv_b'], g['ln2_g'], g['ln2_b'] = dcb[0], dg2[0], db2[0]
        dr1, dg1, db1 = _ffn_bwd_dx(dr2, dgp, dup, s['r1'], row(p['ln1_g'][l]), s['bw']['ffn_w_gate_t'], s['bw']['ffn_w_up_t'])
        g['ffn_w_gate'], g['ffn_w_up'] = _ffn_bwd_dw(s['x1'], dgp, dup)
        g['ln1_g'], g['ln1_b'] = dg1[0], db1[0]
        placed = emit_grads(l, 'ffn', [g[name] for name in PARTS['ffn']])
        dya, dys, dyl, g['w_out'], dbo, dgm = _mixout_bwd(dr1, s['mix'], s['ya'], s['ys'], s['yl'],
                                                         row(p['mix_norm_g'][l]) + placed, s['bw']['w_out_t'])
        g['b_out'], g['mix_norm_g'] = dbo[0], dgm[0]

        du, dbr, dbi, dcr, dci, dar, dai, dd, g['s5_glu_w'], dgb = _s5_bwd(s['u'], dys, s['s5_cr'], s['s5_ci'], s['s5'])
        dxr, dgate, lw0, lw1, lw2, lw3, lcb, dwx, dwa, dbx, dba, dsp = _lru_bwd(s['xr'], s['gate'], dyl, s['lru_c'], s['lru'])
        g['lru_conv_w'] = jnp.concatenate([lw0, lw1, lw2, lw3], axis=0)
        g['lru_conv_b'], g['lru_bx'], g['lru_ba'] = lcb[0], dbx[0], dba[0]
        g['lru_wx'] = jnp.einsum('hihj->hij', dwx.reshape(4, 64, 4, 64))
        g['lru_wa'] = jnp.einsum('hihj->hij', dwa.reshape(4, 64, 4, 64))
        g['lru_a_param'] = -dsp[0] * jax.nn.sigmoid(-p['lru_a_param'][l])

        g['s5_c_re'] = jnp.einsum('gpgc->gcp', dcr.reshape(16, 64, 16, 16))
        g['s5_c_im'] = jnp.einsum('gpgc->gcp', dci.reshape(16, 64, 16, 16))
        g['s5_d'], g['s5_glu_b'] = dd[0], dgb[0]
        d_disc[l] = (dar.reshape(16, 64), dai.reshape(16, 64), jnp.einsum('gcgp->gpc', dbr.reshape(16, 16, 16, 64)),
                     jnp.einsum('gcgp->gpc', dbi.reshape(16, 16, 16, 64)))

        dq, dk, dv, dsink = _attn_bwd(s['q'], s['k'], s['v'], s['sinks'], s['ya'], dya)
        g['attn_sinks'] = dsink[:, 0]
        dh, g['w_in'], dbin = _inproj_bwd(dq, dk[ATTN_BLOCK:], dv[ATTN_BLOCK:], du, dxr, dgate, cos_t, sin_t, s['x0'], dr1,
                                          s['bw']['w_in_t'])
        g['b_in'] = dbin[0]
        placed = emit_grads(l, 'mix', [g[name] for name in PARTS['mix']])
        for name in g:
            grads[name][l] = g[name]
    big = dict(BIG)
    out = {name: grads[name] if name in big else jnp.stack(grads[name]) for name in WEIGHTS if name not in S5_DISC}
    out.update(zip(S5_DISC, s5_vjp(tuple(jnp.stack([d_disc[l][i] for l in range(DEPTH)]) for i in range(4)))))
    return loss, dh, out


def kernel(x, w_in, b_in, attn_sinks, s5_a_re, s5_a_im, s5_b_re, s5_b_im, s5_c_re, s5_c_im, s5_d, s5_log_dt, s5_glu_w, s5_glu_b, lru_conv_w, lru_conv_b, lru_wx, lru_bx, lru_wa, lru_ba, lru_a_param, mix_norm_g, w_out, b_out, ln1_g, ln1_b, ffn_w_gate, ffn_w_up, ffn_conv_w, ffn_conv_b, ffn_w_down, ln2_g, ln2_b, loss_target, m_w_in, m_b_in, m_attn_sinks, m_s5_a_re, m_s5_a_im, m_s5_b_re, m_s5_b_im, m_s5_c_re, m_s5_c_im, m_s5_d, m_s5_log_dt, m_s5_glu_w, m_s5_glu_b, m_lru_conv_w, m_lru_conv_b, m_lru_wx, m_lru_bx, m_lru_wa, m_lru_ba, m_lru_a_param, m_mix_norm_g, m_w_out, m_b_out, m_ln1_g, m_ln1_b, m_ffn_w_gate, m_ffn_w_up, m_ffn_conv_w, m_ffn_conv_b, m_ffn_w_down, m_ln2_g, m_ln2_b, v_w_in, v_b_in, v_attn_sinks, v_s5_a_re, v_s5_a_im, v_s5_b_re, v_s5_b_im, v_s5_c_re, v_s5_c_im, v_s5_d, v_s5_log_dt, v_s5_glu_w, v_s5_glu_b, v_lru_conv_w, v_lru_conv_b, v_lru_wx, v_lru_bx, v_lru_wa, v_lru_ba, v_lru_a_param, v_mix_norm_g, v_w_out, v_b_out, v_ln1_g, v_ln1_b, v_ffn_w_gate, v_ffn_w_up, v_ffn_conv_w, v_ffn_conv_b, v_ffn_w_down, v_ln2_g, v_ln2_b):
    given = dict(locals())
    whole = {name: given[name] for name in WEIGHTS if name not in dict(BIG)}
    small_whole, big_weights, placed = _gather_weights({name: given[name] for name in SHARDED})
    whole.update(small_whole)
    whole['b_in'] = whole['b_in'] + placed
    in_flight = {}

    def emit_grads(l, part, grads):
        in_flight[(l, part)] = _scatter_start(f"grads_start_{part}{l}", grads)
        return in_flight[(l, part)][-1][0, 0]

    loss, grad_x, grads = _local_step(x[0], loss_target[0], whole, big_weights, emit_grads)
    total = lax.psum(loss[0, 0], ("x", "y", "c"))
    return (total, grad_x[None], *_update(given, grads, in_flight, grad_x))


def _update(given, grads, in_flight, after):
    local_w = {name: given[name] for name in WEIGHTS}
    me = _me()
    outs = {}

    shard_rows = sum(_packed_rows(local_w[name].shape) for name in SMALL_SHARDED)
    rep_pad = -sum(_packed_rows(local_w[name].shape) for name in REPLICATED) % PACK_TILE

    def packed_rep(arrays):
        return jnp.concatenate([_pack_rows(arrays[name]) for name in REPLICATED] + [jnp.zeros((rep_pad, LANES), F32)], axis=0)

    rep_grads = packed_rep(grads)
    chunk = rep_grads.shape[0] // N_DEV
    small = jnp.concatenate(
        [_pack_rows(_to_blocks(grads[name], SHARD_AXIS[name]), lead=1) for name in SMALL_SHARDED]
        + [rep_grads.reshape(N_DEV, chunk, LANES)], axis=1)
    small_rows = shard_rows + chunk
    small_flight = _scatter_start("grads_start_small", [small.reshape(N_DEV * small_rows, LANES)])

    def summed(name, flight, rows_of, packed, after):
        send_sems, recv_sems, srcs, landing, _ = flight
        _, landing = _scatter_wait(name, send_sems, recv_sems, srcs, landing, after)
        return _reduce_adamw(landing, *packed, tile_rows=sum(rows_of) // 4)

    forms = {part: [(name, dict(BIG)[name], _row_form(local_w[name], dict(BIG)[name]).shape[1]) for name in names]
             for part, names in PARTS.items()}

    def big_part(key, after):
        l, part = key
        packed = [jnp.concatenate([_row_form(given[prefix + name], t)[l] for name, t, _ in forms[part]], axis=0)
                  for prefix in ('', 'm_', 'v_')]
        outs[key] = summed(f"grads_wait_{part}{l}", in_flight[key], [r for _, _, r in forms[part]], packed, after)
        return outs[key][0]

    *earlier, last = in_flight
    for key in earlier:
        after = big_part(key, after)
    rep_state = [packed_rep({name: given[prefix + name] for name in REPLICATED}) for prefix in ('', 'm_', 'v_')]
    small_state = [jnp.concatenate([_pack_rows(given[prefix + name]) for name in SMALL_SHARDED]
                                   + [lax.dynamic_slice_in_dim(rep, me * chunk, chunk, axis=0)], axis=0)
                   for prefix, rep in zip(('', 'm_', 'v_'), rep_state)]
    small_outs = summed("grads_wait_small", small_flight, [small_rows], small_state, after)
    rep_sum = _all_gather("gather_small_grads", [], small_outs[0][shard_rows:])[0].reshape(N_DEV * chunk, LANES)
    rep_outs = _reduce_adamw(rep_sum[None], *rep_state, tile_rows=N_DEV * chunk // 4)
    big_part(last, rep_outs[0])

    def unpack(i):
        res = {}
        for part in PARTS:
            lo = 0
            for name, t, r in forms[part]:
                res[name] = _row_form(jnp.stack([outs[(l, part)][i][lo:lo + r] for l in range(DEPTH)]), t)
                lo += r
        for names, packed in ((SMALL_SHARDED, small_outs[i]), (REPLICATED, rep_outs[i])):
            lo = 0
            for name in names:
                shape = local_w[name].shape
                res[name] = packed[lo:lo + _packed_rows(shape)].reshape(-1)[:math.prod(shape)].reshape(shape)
                lo += _packed_rows(shape)
        return [res[name] for name in WEIGHTS]

    return (*unpack(0), *unpack(1), *unpack(2), *unpack(3))
```

```python
import functools
import math

import jax
import jax.numpy as jnp
from jax import lax
from jax.experimental import pallas as pl
from jax.experimental.pallas import tpu as pltpu

F32 = jnp.float32
MXU = jnp.bfloat16

N_DEV = 8
DEPTH = 4
D = 1024
D_ATTN, D_KV, D_S5, D_LRU = 512, 128, 256, 256
D_IN = 1536
D_FF = 2816
FF_CHUNK = 256
N_STATE = 1024
LANES = 1024
ALPHA = (2 * DEPTH) ** 0.25
LN_EPS = 1e-5
RMS_EPS = 1e-6
LRU_C = 8.0
ROPE_THETA = 10000.0
ADAM_LR, ADAM_B1, ADAM_B2, ADAM_EPS, ADAM_WD, ADAM_STEP = 0.001, 0.9, 0.999, 1e-08, 0.01, 10

TILE = 256
S5_PAIR = 4
N_SEG = 8
SEG = TILE // N_SEG
TILE_Q = 256
TILE_BIG = 512
TILE_WIDE = 512
ATTN_BLOCK = 128
PACK_TILE = 256
VMEM_MB = 56

WEIGHTS = ['w_in', 'b_in', 'attn_sinks', 's5_a_re', 's5_a_im', 's5_b_re', 's5_b_im', 's5_c_re', 's5_c_im', 's5_d', 's5_log_dt',
           's5_glu_w', 's5_glu_b', 'lru_conv_w', 'lru_conv_b', 'lru_wx', 'lru_bx', 'lru_wa', 'lru_ba', 'lru_a_param', 'mix_norm_g',
           'w_out', 'b_out', 'ln1_g', 'ln1_b', 'ffn_w_gate', 'ffn_w_up', 'ffn_conv_w', 'ffn_conv_b', 'ffn_w_down', 'ln2_g', 'ln2_b']
SHARD_AXIS = {'w_in': 2, 's5_glu_w': 1, 'lru_conv_w': 2, 'w_out': 1, 'ffn_w_gate': 2, 'ffn_w_up': 2, 'ffn_conv_w': 2,
              'ffn_w_down': 1}
SHARDED = [n for n in WEIGHTS if n in SHARD_AXIS]
REPLICATED = [n for n in WEIGHTS if n not in SHARD_AXIS]
BIG = [(n, SHARD_AXIS[n] == 2) for n in ('w_in', 'w_out', 'ffn_w_gate', 'ffn_w_up', 'ffn_w_down')]
SMALL_SHARDED = [n for n in SHARDED if n not in dict(BIG)]
PARTS = {'mix': ['w_in', 'w_out'], 'ffn': ['ffn_w_gate', 'ffn_w_up', 'ffn_w_down']}
GATHER_F32 = ('lru_conv_w', 'ffn_conv_w')


def _dot(a, b):
    return jnp.dot(a, b, preferred_element_type=F32)


def _dot_nt(a, b):
    return lax.dot_general(a, b, (((1,), (1,)), ((), ())), preferred_element_type=F32)


def _dot_tn(a, b):
    return lax.dot_general(a, b, (((0,), (0,)), ((), ())), preferred_element_type=F32)


def _mx(a):
    return a.astype(MXU)


_GELU_C = math.sqrt(2.0 / math.pi)


def _gelu(x):
    th = jnp.tanh(_GELU_C * (x + 0.044715 * x * x * x))
    return 0.5 * x * (1.0 + th)


def _gelu_grad(x):
    th = jnp.tanh(_GELU_C * (x + 0.044715 * x * x * x))
    return 0.5 * (1.0 + th) + 0.5 * x * (1.0 - th * th) * _GELU_C * (1.0 + 3.0 * 0.044715 * x * x)


def _sigmoid(x):
    return 0.5 * jnp.tanh(0.5 * x) + 0.5


def _ln_stats(r):
    mu = jnp.mean(r, axis=-1, keepdims=True)
    xc = r - mu
    var = jnp.mean(xc * xc, axis=-1, keepdims=True)
    rstd = lax.rsqrt(var + LN_EPS)
    return xc * rstd, rstd


def _ln_bwd(dy, g, xhat, rstd):
    dxh = dy * g
    return rstd * (dxh - jnp.mean(dxh, axis=-1, keepdims=True) - xhat * jnp.mean(dxh * xhat, axis=-1, keepdims=True))


def _rms(y):
    return lax.rsqrt(jnp.mean(y * y, axis=-1, keepdims=True) + RMS_EPS)


def _sum0(a):
    return jnp.sum(a, axis=0, keepdims=True)


def _row_iota(shape):
    return lax.broadcasted_iota(jnp.int32, shape, 0)


def _shift_down(ext, j, rows):
    return pltpu.roll(ext, j, 0)[8:8 + rows]


def _shift_up(ext, j, rows):
    return pltpu.roll(ext, ext.shape[0] - j, 0)[:rows]


def _swap_halves(t):
    w = t.shape[1]
    lane = lax.broadcasted_iota(jnp.int32, t.shape, 1)
    return jnp.where((lane & 32) == 0, pltpu.roll(t, w - 32, 1), pltpu.roll(t, 32, 1))


def _rope(t, cos, sin_signed):
    return t * cos + _swap_halves(t) * sin_signed


def _rope_t(d, cos, sin_signed):
    return d * cos + _swap_halves(d * sin_signed)


def _cmul_add(ar, ai, xr, xi, yr, yi):
    return ar * xr - ai * xi + yr, ar * xi + ai * xr + yi


def _seg_rows(k):
    return slice(N_SEG * k, N_SEG * (k + 1))


def _permute_rows(perm, x):
    hi = _mx(x)
    rest = x - hi.astype(F32)
    mid = _mx(rest)
    lo = _mx(rest - mid.astype(F32))
    return _dot(perm, hi) + _dot(perm, mid) + _dot(perm, lo)


def _cscan(sr, si, tab, cin_r, cin_i, reverse):
    sgn = -1.0 if reverse else 1.0
    pw_re, pw_im, dbl_re, dbl_im = tab['pw_re'], tab['pw_im'], tab['dbl_re'], tab['dbl_im']
    ar, ai = pw_re[0:1, :], sgn * pw_im[0:1, :]
    shape = (N_SEG, sr.shape[1])
    hr = hi = jnp.zeros(shape, F32)
    for k in (range(SEG - 1, -1, -1) if reverse else range(SEG)):
        hr, hi = _cmul_add(ar, ai, hr, hi, sr[_seg_rows(k), :], si[_seg_rows(k), :])
        sr[_seg_rows(k), :] = hr
        si[_seg_rows(k), :] = hi
    sub = _row_iota(shape)

    def shifted(v, d):
        if reverse:
            return jnp.where(sub < N_SEG - d, pltpu.roll(v, N_SEG - d, 0), 0.0)
        return jnp.where(sub >= d, pltpu.roll(v, d, 0), 0.0)

    fr, fi = hr, hi
    for j, d in enumerate((1, 2, 4)):
        fr, fi = _cmul_add(dbl_re[j:j + 1, :], sgn * dbl_im[j:j + 1, :], shifted(fr, d), shifted(fi, d), fr, fi)
    seg_re, seg_im = (tab['segr_re'], tab['segr_im']) if reverse else (tab['seg_re'], tab['seg_im'])
    cr, ci = _cmul_add(seg_re[...], sgn * seg_im[...], cin_r, cin_i, shifted(fr, 1), shifted(fi, 1))
    nr, ni = _cmul_add(dbl_re[0:1, :], sgn * dbl_im[0:1, :], cr, ci, hr, hi)
    for k in range(SEG):
        j = SEG - 1 - k if reverse else k
        xr, xi = _cmul_add(pw_re[j:j + 1, :], sgn * pw_im[j:j + 1, :], cr, ci, sr[_seg_rows(k), :], si[_seg_rows(k), :])
        sr[_seg_rows(k), :] = xr
        si[_seg_rows(k), :] = xi
    edge = slice(0, 1) if reverse else slice(N_SEG - 1, N_SEG)
    return nr[edge], ni[edge]


def _rscan(a, b, reverse):
    rows = a.shape[0]
    row = _row_iota(a.shape)
    s = 1
    while s < rows:
        if reverse:
            keep = row < rows - s
            sa = jnp.where(keep, pltpu.roll(a, rows - s, 0), 1.0)
            sb = jnp.where(keep, pltpu.roll(b, rows - s, 0), 0.0)
        else:
            keep = row >= s
            sa = jnp.where(keep, pltpu.roll(a, s, 0), 1.0)
            sb = jnp.where(keep, pltpu.roll(b, s, 0), 0.0)
        b = b + a * sb
        a = a * sa
        s *= 2
    return a, b


def _whole():
    return pl.BlockSpec(memory_space=pltpu.VMEM)


def _rows_spec(rows, cols, n_tiles, reverse=False):
    if reverse:
        return pl.BlockSpec((rows, cols), lambda i: (n_tiles - 1 - i, 0))
    return pl.BlockSpec((rows, cols), lambda i: (i, 0))


def _halo_spec(cols, tile_rows, n_tiles, reverse=False):
    per = tile_rows // 8
    if reverse:
        return pl.BlockSpec((8, cols), lambda i: (jnp.maximum((n_tiles - 1 - i) * per - 1, 0), 0))
    return pl.BlockSpec((8, cols), lambda i: (jnp.maximum(i * per - 1, 0), 0))


def _call(body, name, n_tiles, in_specs, out_specs, out_shape, scratch=()):
    return pl.pallas_call(
        body, name=name, grid=(n_tiles,), in_specs=in_specs, out_specs=out_specs, out_shape=out_shape,
        scratch_shapes=list(scratch),
        compiler_params=pltpu.CompilerParams(dimension_semantics=("arbitrary",), vmem_limit_bytes=VMEM_MB << 20))


def _sds(shape, dtype=F32):
    return jax.ShapeDtypeStruct(shape, dtype)


def _inproj_fwd(x, w, b, cos_t, sin_t):
    n = x.shape[0]
    nt = n // TILE_WIDE

    def body(x_ref, w_ref, b_ref, c_ref, s_ref, q_ref, k_ref, v_ref, u_ref, xr_ref, g_ref):
        p = _dot(_mx(x_ref[...]), w_ref[...]) + b_ref[...]
        cos, sin = c_ref[...], s_ref[...]
        q_ref[...] = _mx(_rope(p[:, :D_ATTN], jnp.tile(cos, (1, 4)), jnp.tile(sin, (1, 4))))
        k_ref[...] = _mx(_rope(p[:, 512:640], cos, sin))
        v_ref[...] = _mx(p[:, 640:768])
        u_ref[...] = p[:, 768:1024]
        xr_ref[...] = p[:, 1024:1280]
        g_ref[...] = p[:, 1280:1536]

    r = functools.partial(_rows_spec, n_tiles=nt)
    return _call(
        body, "inproj_fwd", nt,
        [r(TILE_WIDE, D), _whole(), _whole(), r(TILE_WIDE, 128), r(TILE_WIDE, 128)],
        [r(TILE_WIDE, D_ATTN), r(TILE_WIDE, D_KV), r(TILE_WIDE, D_KV), r(TILE_WIDE, D_S5), r(TILE_WIDE, D_LRU), r(TILE_WIDE, D_LRU)],
        [_sds((n, D_ATTN), MXU), _sds((n, D_KV), MXU), _sds((n, D_KV), MXU), _sds((n, D_S5)), _sds((n, D_LRU)), _sds((n, D_LRU))],
    )(x, w, b, cos_t, sin_t)


def _inproj_bwd(dq, dk, dv, du, dxr, dgate, cos_t, sin_t, x0, dr1, w_t):
    n = x0.shape[0]
    nt = n // TILE_WIDE

    def body(dq_ref, dk_ref, dv_ref, du_ref, dxr_ref, dg_ref, c_ref, s_ref, x_ref, dr_ref, w_ref, dx_ref, dw_ref, db_ref):
        @pl.when(pl.program_id(0) == 0)
        def _():
            dw_ref[...] = jnp.zeros_like(dw_ref)
            db_ref[...] = jnp.zeros_like(db_ref)

        cos, sin = c_ref[...], s_ref[...]
        dtq = _rope_t(dq_ref[...], jnp.tile(cos, (1, 4)), jnp.tile(sin, (1, 4)))
        dtk = _rope_t(dk_ref[...], cos, sin)
        dp = jnp.concatenate([dtq, dtk, dv_ref[...], du_ref[...], dxr_ref[...], dg_ref[...]], axis=1)
        db_ref[...] += _sum0(dp)
        dpb = _mx(dp)
        dw_ref[...] += _dot_tn(dpb, _mx(x_ref[...]))
        dx_ref[...] = ALPHA * dr_ref[...] + _dot(dpb, w_ref[...])

    r = functools.partial(_rows_spec, n_tiles=nt)
    return _call(
        body, "inproj_bwd", nt,
        [r(TILE_WIDE, D_ATTN), r(TILE_WIDE, D_KV), r(TILE_WIDE, D_KV), r(TILE_WIDE, D_S5), r(TILE_WIDE, D_LRU), r(TILE_WIDE, D_LRU),
         r(TILE_WIDE, 128), r(TILE_WIDE, 128), r(TILE_WIDE, D), r(TILE_WIDE, D), _whole()],
        [r(TILE_WIDE, D), _whole(), _whole()],
        [_sds((n, D)), _sds((D_IN, D)), _sds((1, D_IN))],
    )(dq, dk, dv, du, dxr, dgate, cos_t, sin_t, x0, dr1, w_t)


def _kv_variants(t, lo):
    tr = pltpu.roll(t, 64, 1)
    out = []
    for j in range(2):
        first = jnp.where(lo, t if j == 0 else tr, 0.0)
        second = jnp.where(lo, 0.0, tr if j == 0 else t)
        out.append(_mx(jnp.concatenate([first, second], axis=0)))
    return out


def _kv_collect(x0, x1, lo):
    a = x0[:256] + pltpu.roll(x0[256:], 64, 1)
    b = pltpu.roll(x1[:256], 64, 1) + x1[256:]
    return jnp.where(lo, a, b)


def _row_sums(x):
    ones = jnp.ones((128, 128), MXU)
    hi = _mx(x)
    lo = _mx(x - hi.astype(F32))
    return _dot(hi, ones) + _dot(lo, ones)


def _attn_probs(s, sink_ref):
    out = []
    for hp in range(2):
        sh = s[:, hp * 128:(hp + 1) * 128]
        sink = sink_ref[hp]
        m = jnp.maximum(jnp.broadcast_to(jnp.max(sh, axis=1, keepdims=True), sh.shape), sink)
        p = jnp.exp(sh - m)
        es = jnp.exp(sink - m)
        inv = 1.0 / (_row_sums(p) + es)
        out.append((p * inv, es * inv))
    return out


def _band_merge(x, tri, no_previous=None):
    bands = []
    for hp in range(2):
        prev, own = x[:, hp * 256:hp * 256 + 128], x[:, hp * 256 + 128:hp * 256 + 256]
        if no_previous is not None:
            prev = jnp.where(no_previous, -jnp.inf, prev)
        bands.append(jnp.where(tri, own, prev))
    return jnp.concatenate(bands, axis=1)


def _band_split(y, tri):
    parts = []
    for hp in range(2):
        band = y[:, hp * 128:(hp + 1) * 128]
        parts += [jnp.where(tri, 0.0, band), jnp.where(tri, band, 0.0)]
    return jnp.concatenate(parts, axis=1)


def _tri():
    shape = (ATTN_BLOCK, ATTN_BLOCK)
    return lax.broadcasted_iota(jnp.int32, shape, 0) >= lax.broadcasted_iota(jnp.int32, shape, 1)


def _attn_scores(q_ref, k_ref, v_ref, nb, tri):
    lo = lax.broadcasted_iota(jnp.int32, (256, 128), 1) < 64
    kcats, vcats, kstarts, parts = [], [], [], []
    for b in range(nb):
        block = pl.program_id(0) * nb + b
        kstart = pl.multiple_of(block * ATTN_BLOCK, ATTN_BLOCK)
        kcat = _kv_variants(k_ref[pl.ds(kstart, 256), :].astype(F32), lo)
        kcats.append(kcat)
        vcats.append(_kv_variants(v_ref[pl.ds(kstart, 256), :].astype(F32), lo))
        kstarts.append(kstart)
        for j in range(2):
            s = _dot_nt(_kv_group(q_ref, b, j), kcat[j]) * 0.125
            parts += [_band_merge(s[:ATTN_BLOCK], tri, block == 0), _band_merge(s[ATTN_BLOCK:], tri, block == 0)]
    return jnp.concatenate(parts, axis=0), kcats, vcats, kstarts


def _kv_group(a, b, j):
    rows = slice(b * ATTN_BLOCK, (b + 1) * ATTN_BLOCK)
    return jnp.concatenate([a[rows, 2 * j * 128:(2 * j + 1) * 128], a[rows, (2 * j + 1) * 128:(2 * j + 2) * 128]], axis=0)


def _put_kv_group(ref, b, j, x):
    rows = slice(b * ATTN_BLOCK, (b + 1) * ATTN_BLOCK)
    ref[rows, 2 * j * 128:(2 * j + 1) * 128] = x[:ATTN_BLOCK]
    ref[rows, (2 * j + 1) * 128:(2 * j + 2) * 128] = x[ATTN_BLOCK:]


def _band_split_group(y, unit, tri):
    return _mx(jnp.concatenate([_band_split(y[unit:unit + ATTN_BLOCK], tri),
                                _band_split(y[unit + ATTN_BLOCK:unit + 2 * ATTN_BLOCK], tri)], axis=0))


def _attn_fwd(q, k, v, sink_cols):
    n = q.shape[0]
    nt = n // TILE_Q
    nb = TILE_Q // ATTN_BLOCK

    def body(q_ref, k_ref, v_ref, s_ref, o_ref):
        tri = _tri()
        s, _, vcats, _ = _attn_scores(q_ref, k_ref, v_ref, nb, tri)
        (p0, _), (p1, _) = _attn_probs(s, s_ref)
        p = jnp.concatenate([p0, p1], axis=1)
        for b in range(nb):
            for j in range(2):
                _put_kv_group(o_ref, b, j, _dot(_band_split_group(p, (b * 4 + 2 * j) * ATTN_BLOCK, tri), vcats[b][j]))

    return _call(
        body, "attn_fwd", nt,
        [_rows_spec(TILE_Q, D_ATTN, nt), _whole(), _whole(), _whole()],
        _rows_spec(TILE_Q, D_ATTN, nt), _sds((n, D_ATTN)),
    )(q, k, v, sink_cols)


def _attn_bwd(q, k, v, sink_cols, o, do):
    n = q.shape[0]
    nt = n // TILE_Q
    nb = TILE_Q // ATTN_BLOCK

    def body(q_ref, k_ref, v_ref, s_ref, o_ref, do_ref, dq_ref, dk_ref, dv_ref, ds_ref):
        @pl.when(pl.program_id(0) == 0)
        def _():
            dk_ref[...] = jnp.zeros_like(dk_ref)
            dv_ref[...] = jnp.zeros_like(dv_ref)
            ds_ref[...] = jnp.zeros_like(ds_ref)

        lo = lax.broadcasted_iota(jnp.int32, (256, 128), 1) < 64
        tri = _tri()
        s, kcats, vcats, kstarts = _attn_scores(q_ref, k_ref, v_ref, nb, tri)
        probs = _attn_probs(s, s_ref)
        do = do_ref[...]
        dob = _mx(do)
        od = do * o_ref[...]
        lo_q = (lax.broadcasted_iota(jnp.int32, od.shape, 1) & 64) == 0
        od_head = (jnp.where(lo_q, od, 0.0), jnp.where(lo_q, 0.0, od))
        units = [(b, i) for b in range(nb) for i in range(4)]

        def tile_part(a, b, i):
            return a[b * ATTN_BLOCK:(b + 1) * ATTN_BLOCK, i * 128:(i + 1) * 128]

        dp = []
        for b in range(nb):
            for j in range(2):
                x = _dot_nt(_kv_group(dob, b, j), vcats[b][j])
                dp += [_band_merge(x[:ATTN_BLOCK], tri), _band_merge(x[ATTN_BLOCK:], tri)]
        dp = jnp.concatenate(dp, axis=0)
        ds = []
        for hp in range(2):
            p, p_sink = probs[hp]
            delta = _row_sums(jnp.concatenate([tile_part(od_head[hp], b, i) for b, i in units], axis=0))
            ds.append(p * (dp[:, hp * 128:(hp + 1) * 128] - delta) * 0.125)
            t = p_sink * delta
            for i in range(4):
                ds_ref[2 * i + hp:2 * i + hp + 1, :] -= sum(
                    _sum0(t[(b * 4 + i) * ATTN_BLOCK:(b * 4 + i + 1) * ATTN_BLOCK]) for b in range(nb))
        ds = jnp.concatenate(ds, axis=1)
        p = jnp.concatenate([probs[0][0], probs[1][0]], axis=1)
        for b in range(nb):
            dkc, dvc = [], []
            for j in range(2):
                unit = (b * 4 + 2 * j) * ATTN_BLOCK
                dsb = _band_split_group(ds, unit, tri)
                _put_kv_group(dq_ref, b, j, _dot(dsb, kcats[b][j]))
                dkc.append(_dot_tn(dsb, _kv_group(q_ref, b, j)))
                dvc.append(_dot_tn(_band_split_group(p, unit, tri), _kv_group(dob, b, j)))
            dk_ref[pl.ds(kstarts[b], 256), :] += _kv_collect(dkc[0], dkc[1], lo)
            dv_ref[pl.ds(kstarts[b], 256), :] += _kv_collect(dvc[0], dvc[1], lo)

    r = _rows_spec(TILE_Q, D_ATTN, nt)
    return _call(
        body, "attn_bwd", nt,
        [r, _whole(), _whole(), _whole(), r, r],
        [r, _whole(), _whole(), _whole()],
        [_sds((n, D_ATTN)), _sds((n + ATTN_BLOCK, D_KV)), _sds((n + ATTN_BLOCK, D_KV)), _sds((8, 128))],
    )(q, k, v, sink_cols, o, do)


S5_DISC = ('s5_a_re', 's5_a_im', 's5_log_dt', 's5_b_re', 's5_b_im')
S5_TABLES = ('pw_re', 'pw_im', 'dbl_re', 'dbl_im', 'seg_re', 'seg_im', 'segr_re', 'segr_im')
S5_WEIGHTS = ('b_re', 'b_im', 'c_re', 'c_im', 'd', 'glu_w', 'glu_b', 'perm', 'perm_t')


def _s5_states(u, carry_r, carry_i, b_re, b_im, tab, hr_s, hi_s):
    ub = _mx(u)
    hr_s[...] = _dot(ub, b_re[...])
    hi_s[...] = _dot(ub, b_im[...])
    return ub, _cscan(hr_s, hi_s, tab, carry_r, carry_i, reverse=False)


def _s5_fwd(u, prm):
    n = u.shape[0]
    nt = n // (S5_PAIR * TILE)

    def body(u_ref, *refs):
        tab = dict(zip(S5_TABLES, refs[:8]))
        b_re, b_im, c_re, c_im, d_ref, gw_ref, gb_ref, perm, perm_t = refs[8:17]
        y_ref, cr_out, ci_out, cr_s, ci_s = refs[17:22]
        states = [refs[22 + 2 * j:24 + 2 * j] for j in range(S5_PAIR)]
        rows = [slice(j * TILE, (j + 1) * TILE) for j in range(S5_PAIR)]

        @pl.when(pl.program_id(0) == 0)
        def _():
            cr_s[...] = jnp.zeros_like(cr_s)
            ci_s[...] = jnp.zeros_like(ci_s)

        us = []
        for j in range(S5_PAIR):
            us.append(_permute_rows(perm[...], u_ref[rows[j], :]))
            ub = _mx(us[j])
            states[j][0][...] = _dot(ub, b_re[...])
            states[j][1][...] = _dot(ub, b_im[...])
        for j in range(S5_PAIR):
            cr, ci = cr_s[...], ci_s[...]
            cr_out[8 * j:8 * j + 8, :] = jnp.broadcast_to(cr, (8, N_STATE))
            ci_out[8 * j:8 * j + 8, :] = jnp.broadcast_to(ci, (8, N_STATE))
            cr_s[...], ci_s[...] = _cscan(*states[j], tab, cr, ci, reverse=False)
        for j in range(S5_PAIR):
            hr_s, hi_s = states[j]
            y = _dot(_mx(hr_s[...]), c_re[...]) - _dot(_mx(hi_s[...]), c_im[...]) + d_ref[...] * us[j]
            z = _gelu(y)
            y_ref[rows[j], :] = _permute_rows(perm_t[...], z * _sigmoid(_dot(_mx(z), gw_ref[...]) + gb_ref[...]))

    r = functools.partial(_rows_spec, n_tiles=nt)
    return _call(
        body, "s5_fwd", nt,
        [r(S5_PAIR * TILE, D_S5)] + [_whole()] * 17,
        [r(S5_PAIR * TILE, D_S5), r(S5_PAIR * 8, N_STATE), r(S5_PAIR * 8, N_STATE)],
        [_sds((n, D_S5)), _sds((n // TILE * 8, N_STATE)), _sds((n // TILE * 8, N_STATE))],
        scratch=[pltpu.VMEM((1, N_STATE), F32)] * 2 + [pltpu.VMEM((TILE, N_STATE), F32)] * (2 * S5_PAIR),
    )(u, *[prm[k] for k in S5_TABLES + S5_WEIGHTS])


def _s5_bwd(u, dys, carry_re, carry_im, prm):
    n = u.shape[0]
    nt = n // (S5_PAIR * TILE)

    def body(u_ref, dy_ref, cin_r, cin_i, *refs):
        tab = dict(zip(S5_TABLES, refs[:8]))
        b_re, b_im, c_re, c_im, d_ref, gw_ref, gb_ref, perm, perm_t = refs[8:17]
        du_ref, dbr_ref, dbi_ref, dcr_ref, dci_ref, dar_ref, dai_ref, dd_ref, dgw_ref, dgb_ref = refs[17:27]
        gr_s, gi_s = refs[27:29]
        scratch = [refs[29 + 4 * j:33 + 4 * j] for j in range(S5_PAIR)]
        later_first = list(reversed(range(S5_PAIR)))
        rows = [slice(j * TILE, (j + 1) * TILE) for j in range(S5_PAIR)]

        @pl.when(pl.program_id(0) == 0)
        def _():
            for ref in (dbr_ref, dbi_ref, dcr_ref, dci_ref, dar_ref, dai_ref, dd_ref, dgw_ref, dgb_ref, gr_s, gi_s):
                ref[...] = jnp.zeros_like(ref)

        us, ubs, carries, dy_of = {}, {}, {}, {}
        for j in later_first:
            us[j] = _permute_rows(perm[...], u_ref[rows[j], :])
            carries[j] = (cin_r[8 * j:8 * j + 1, :], cin_i[8 * j:8 * j + 1, :])
            ubs[j], _ = _s5_states(us[j], *carries[j], b_re, b_im, tab, *scratch[j][:2])
        for j in later_first:
            u = us[j]
            hr_s, hi_s, gr_t, gi_t = scratch[j]
            hrb, hib = _mx(hr_s[...]), _mx(hi_s[...])
            y = _dot(hrb, c_re[...]) - _dot(hib, c_im[...]) + d_ref[...] * u
            z = _gelu(y)
            zb = _mx(z)
            sg = _sigmoid(_dot(zb, gw_ref[...]) + gb_ref[...])
            dout = _permute_rows(perm[...], dy_ref[rows[j], :])
            dpre = dout * z * sg * (1.0 - sg)
            dgb_ref[...] += _sum0(dpre)
            dpb = _mx(dpre)
            dgw_ref[...] += _dot_tn(zb, dpb)
            dy = (dout * sg + _dot_nt(dpb, gw_ref[...])) * _gelu_grad(y)
            dd_ref[...] += _sum0(dy * u)
            dyb = _mx(dy)
            dcr_ref[...] += _dot_tn(hrb, dyb)
            dci_ref[...] -= _dot_tn(hib, dyb)
            gr_t[...] = _dot_nt(dyb, c_re[...])
            gi_t[...] = -_dot_nt(dyb, c_im[...])
            dy_of[j] = dy
        for j in later_first:
            gr_s[...], gi_s[...] = _cscan(*scratch[j][2:], tab, gr_s[...], gi_s[...], reverse=True)
        for j in later_first:
            hr_s, hi_s, gr_t, gi_t = scratch[j]
            cr, ci = carries[j]
            sub = _row_iota((N_SEG, N_STATE))
            acc_r = acc_i = jnp.zeros((N_SEG, N_STATE), F32)
            for k in range(SEG):
                if k == 0:
                    hpr = jnp.where(sub >= 1, pltpu.roll(hr_s[_seg_rows(SEG - 1), :], 1, 0), cr)
                    hpi = jnp.where(sub >= 1, pltpu.roll(hi_s[_seg_rows(SEG - 1), :], 1, 0), ci)
                else:
                    hpr, hpi = hr_s[_seg_rows(k - 1), :], hi_s[_seg_rows(k - 1), :]
                gr, gi = gr_t[_seg_rows(k), :], gi_t[_seg_rows(k), :]
                acc_r = acc_r + gr * hpr + gi * hpi
                acc_i = acc_i + gi * hpr - gr * hpi
            dar_ref[...] += _sum0(acc_r)
            dai_ref[...] += _sum0(acc_i)
            grb, gib = _mx(gr_t[...]), _mx(gi_t[...])
            dbr_ref[...] += _dot_tn(ubs[j], grb)
            dbi_ref[...] += _dot_tn(ubs[j], gib)
            du_ref[rows[j], :] = _permute_rows(perm_t[...], dy_of[j] * d_ref[...] + _dot_nt(grb, b_re[...]) + _dot_nt(gib, b_im[...]))

    r = functools.partial(_rows_spec, n_tiles=nt, reverse=True)
    return _call(
        body, "s5_bwd", nt,
        [r(S5_PAIR * TILE, D_S5), r(S5_PAIR * TILE, D_S5), r(S5_PAIR * 8, N_STATE), r(S5_PAIR * 8, N_STATE)] + [_whole()] * 17,
        [r(S5_PAIR * TILE, D_S5)] + [_whole()] * 9,
        [_sds((n, D_S5)), _sds((D_S5, N_STATE)), _sds((D_S5, N_STATE)), _sds((N_STATE, D_S5)), _sds((N_STATE, D_S5)),
         _sds((1, N_STATE)), _sds((1, N_STATE)), _sds((1, D_S5)), _sds((D_S5, D_S5)), _sds((1, D_S5))],
        scratch=[pltpu.VMEM((1, N_STATE), F32)] * 2 + [pltpu.VMEM((TILE, N_STATE), F32)] * (4 * S5_PAIR),
    )(u, dys, carry_re, carry_im, *[prm[k] for k in S5_TABLES + S5_WEIGHTS])


def _lru_gates(xr, halo, tile_index, cw_ref, cb_ref, wx_ref, wa_ref, bx_ref, ba_ref, sp_ref):
    ext = jnp.concatenate([halo, xr], axis=0)
    sh = [xr] + [_shift_down(ext, j, TILE) for j in (1, 2, 3)]
    xc = cb_ref[...] + cw_ref[3:4, :] * sh[0] + cw_ref[2:3, :] * sh[1] + cw_ref[1:2, :] * sh[2] + cw_ref[0:1, :] * sh[3]
    xb = _mx(xc)
    gx = _sigmoid(_dot(xb, wx_ref[...]) + bx_ref[...])
    ga = _sigmoid(_dot(xb, wa_ref[...]) + ba_ref[...])
    la = -LRU_C * ga * sp_ref[...]
    a = jnp.exp(la)
    start = (tile_index * TILE + _row_iota(xr.shape)) == 0
    mult = jnp.where(start, 1.0, jnp.sqrt(-jnp.tanh(la) * (a * a + 1.0)))
    return sh, xc, xb, gx, ga, a, mult, start


def _lru_fwd(xr, gate, prm):
    n = xr.shape[0]
    nt = n // TILE

    def body(x_ref, g_ref, cw_ref, cb_ref, wx_ref, wa_ref, bx_ref, ba_ref, sp_ref, y_ref, c_out, halo_s, c_s):
        first_tile = pl.program_id(0) == 0

        @pl.when(first_tile)
        def _():
            halo_s[...] = jnp.zeros_like(halo_s)
            c_s[...] = jnp.zeros_like(c_s)

        xr = x_ref[...]
        _, xc, _, gx, _, a, mult, _ = _lru_gates(xr, halo_s[...], pl.program_id(0), cw_ref, cb_ref, wx_ref, wa_ref, bx_ref, ba_ref,
                                                 sp_ref)
        halo_s[...] = xr[TILE - 8:]
        acum, h = _rscan(a, mult * gx * xc, reverse=False)
        c = c_s[...]
        c_out[...] = jnp.broadcast_to(c, (8, D_LRU))
        h = h + acum * c
        c_s[...] = h[TILE - 1:TILE]
        y_ref[...] = h * _gelu(g_ref[...])

    r = functools.partial(_rows_spec, n_tiles=nt)
    return _call(
        body, "lru_fwd", nt,
        [r(TILE, D_LRU), r(TILE, D_LRU)] + [_whole()] * 7,
        [r(TILE, D_LRU), r(8, D_LRU)],
        [_sds((n, D_LRU)), _sds((nt * 8, D_LRU))],
        scratch=[pltpu.VMEM((8, D_LRU), F32), pltpu.VMEM((1, D_LRU), F32)],
    )(xr, gate, prm['conv_w'], prm['conv_b'], prm['wx'], prm['wa'], prm['bx'], prm['ba'], prm['sp'])


def _lru_bwd(xr, gate, dyl, carry, prm):
    n = xr.shape[0]
    nt = n // TILE

    def body(x_ref, xh_ref, g_ref, dy_ref, cin_ref, cw_ref, cb_ref, wx_ref, wa_ref, bx_ref, ba_ref, sp_ref,
             dx_ref, dg_ref, dcw0, dcw1, dcw2, dcw3, dcb_ref, dwx_ref, dwa_ref, dbx_ref, dba_ref, dsp_ref, an_s, gn_s, dn_s):
        first_tile = pl.program_id(0) == nt - 1

        @pl.when(pl.program_id(0) == 0)
        def _():
            for ref in (dcw0, dcw1, dcw2, dcw3, dcb_ref, dwx_ref, dwa_ref, dbx_ref, dba_ref, dsp_ref, gn_s, dn_s):
                ref[...] = jnp.zeros_like(ref)
            an_s[...] = jnp.ones_like(an_s)

        xr = x_ref[...]
        halo = jnp.where(first_tile, 0.0, xh_ref[...])
        sh, xc, xb, gx, ga, a, mult, start = _lru_gates(xr, halo, nt - 1 - pl.program_id(0), cw_ref, cb_ref, wx_ref, wa_ref, bx_ref,
                                                        ba_ref, sp_ref)
        acum, h = _rscan(a, mult * gx * xc, reverse=False)
        cin = cin_ref[0:1, :]
        h = h + acum * cin
        gate = g_ref[...]
        dyl = dy_ref[...]
        dg_ref[...] = dyl * h * _gelu_grad(gate)
        row = _row_iota(xr.shape)
        alpha = jnp.where(row < TILE - 1, pltpu.roll(a, TILE - 1, 0), an_s[...])
        racc, g = _rscan(alpha, dyl * _gelu(gate), reverse=True)
        g = g + racc * gn_s[...]
        an_s[...] = a[0:1]
        gn_s[...] = g[0:1]
        hprev = jnp.where(row == 0, cin, pltpu.roll(h, 1, 0))
        da = g * hprev
        dmult = jnp.where(start, 0.0, g * gx * xc)
        dla = da * a - dmult * a * a / mult
        dsp_ref[...] += _sum0(-LRU_C * ga * dla)
        dpa = (-LRU_C * sp_ref[...] * dla) * ga * (1.0 - ga)
        dpx = (g * mult * xc) * gx * (1.0 - gx)
        dba_ref[...] += _sum0(dpa)
        dbx_ref[...] += _sum0(dpx)
        dpab, dpxb = _mx(dpa), _mx(dpx)
        dwa_ref[...] += _dot_tn(xb, dpab)
        dwx_ref[...] += _dot_tn(xb, dpxb)
        dxc = g * mult * gx + _dot_nt(dpab, wa_ref[...]) + _dot_nt(dpxb, wx_ref[...])
        dcb_ref[...] += _sum0(dxc)
        dcw3[...] += _sum0(dxc * sh[0])
        dcw2[...] += _sum0(dxc * sh[1])
        dcw1[...] += _sum0(dxc * sh[2])
        dcw0[...] += _sum0(dxc * sh[3])
        ext = jnp.concatenate([dxc, dn_s[...]], axis=0)
        dx_ref[...] = (cw_ref[3:4, :] * dxc + cw_ref[2:3, :] * _shift_up(ext, 1, TILE) + cw_ref[1:2, :] * _shift_up(ext, 2, TILE)
                       + cw_ref[0:1, :] * _shift_up(ext, 3, TILE))
        dn_s[...] = dxc[:8]

    r = functools.partial(_rows_spec, n_tiles=nt, reverse=True)
    vec = _sds((1, D_LRU))
    return _call(
        body, "lru_bwd", nt,
        [r(TILE, D_LRU), _halo_spec(D_LRU, TILE, nt, reverse=True), r(TILE, D_LRU), r(TILE, D_LRU), r(8, D_LRU)] + [_whole()] * 7,
        [r(TILE, D_LRU), r(TILE, D_LRU)] + [_whole()] * 10,
        [_sds((n, D_LRU)), _sds((n, D_LRU)), vec, vec, vec, vec, vec, _sds((D_LRU, D_LRU)), _sds((D_LRU, D_LRU)), vec, vec, vec],
        scratch=[pltpu.VMEM((1, D_LRU), F32), pltpu.VMEM((1, D_LRU), F32), pltpu.VMEM((8, D_LRU), F32)],
    )(xr, xr, gate, dyl, carry, prm['conv_w'], prm['conv_b'], prm['wx'], prm['wa'], prm['bx'], prm['ba'], prm['sp'])


def _normed_parts(ya, ys, yl):
    return jnp.concatenate([ya * _rms(ya), ys * _rms(ys), yl * _rms(yl)], axis=1)


def _mixout_fwd(ya, ys, yl, x0, g_mix, w_out, b_out, g1, b1):
    n = x0.shape[0]
    nt = n // TILE_WIDE

    def body(ya_ref, ys_ref, yl_ref, x_ref, gm_ref, w_ref, b_ref, g_ref, be_ref, mix_ref, r_ref, x1_ref):
        mixb = _mx(_normed_parts(ya_ref[...], ys_ref[...], yl_ref[...]) * gm_ref[...])
        mix_ref[...] = mixb
        r1 = ALPHA * x_ref[...] + _dot(mixb, w_ref[...]) + b_ref[...]
        r_ref[...] = r1
        xhat, _ = _ln_stats(r1)
        x1_ref[...] = xhat * g_ref[...] + be_ref[...]

    r = functools.partial(_rows_spec, n_tiles=nt)
    return _call(
        body, "mixout_fwd", nt,
        [r(TILE_WIDE, D_ATTN), r(TILE_WIDE, D_S5), r(TILE_WIDE, D_LRU), r(TILE_WIDE, D)] + [_whole()] * 5,
        [r(TILE_WIDE, D), r(TILE_WIDE, D), r(TILE_WIDE, D)],
        [_sds((n, D), MXU), _sds((n, D)), _sds((n, D))],
    )(ya, ys, yl, x0, g_mix, w_out, b_out, g1, b1)


def _mixout_bwd(dr1, mix, ya, ys, yl, g_mix, w_out):
    n = dr1.shape[0]
    nt = n // TILE_WIDE

    def body(dr_ref, mix_ref, ya_ref, ys_ref, yl_ref, gm_ref, w_ref, dya_ref, dys_ref, dyl_ref, dw_ref, db_ref, dgm_ref):
        @pl.when(pl.program_id(0) == 0)
        def _():
            for ref in (dw_ref, db_ref, dgm_ref):
                ref[...] = jnp.zeros_like(ref)

        dr = dr_ref[...]
        db_ref[...] += _sum0(dr)
        drb = _mx(dr)
        dw_ref[...] += _dot_tn(mix_ref[...], drb)
        dmix = _dot(drb, w_ref[...])
        parts = (ya_ref[...], ys_ref[...], yl_ref[...])
        dgm_ref[...] += _sum0(dmix * _normed_parts(*parts))
        dn = dmix * gm_ref[...]
        lo = 0
        for y, out in zip(parts, (dya_ref, dys_ref, dyl_ref)):
            w = y.shape[1]
            rs = _rms(y)
            nrm = y * rs
            dnp = dn[:, lo:lo + w]
            out[...] = rs * (dnp - nrm * jnp.mean(dnp * nrm, axis=-1, keepdims=True))
            lo += w

    r = functools.partial(_rows_spec, n_tiles=nt)
    return _call(
        body, "mixout_bwd", nt,
        [r(TILE_WIDE, D), r(TILE_WIDE, D), r(TILE_WIDE, D_ATTN), r(TILE_WIDE, D_S5), r(TILE_WIDE, D_LRU), _whole(), _whole()],
        [r(TILE_WIDE, D_ATTN), r(TILE_WIDE, D_S5), r(TILE_WIDE, D_LRU), _whole(), _whole(), _whole()],
        [_sds((n, D_ATTN)), _sds((n, D_S5)), _sds((n, D_LRU)), _sds((D, D)), _sds((1, D)), _sds((1, D))],
    )(dr1, mix, ya, ys, yl, g_mix, w_out)


def _ffn_conv(gp, halo, cw_ref, cb_ref, cs):
    ext = jnp.concatenate([halo, gp], axis=0)
    s1 = _shift_down(ext, 1, TILE)
    s2 = _shift_down(ext, 2, TILE)
    return s1, s2, cb_ref[:, cs] + cw_ref[2:3, cs] * gp + cw_ref[1:2, cs] * s1 + cw_ref[0:1, cs] * s2


def _ffn_fwd(x1, wg, wu, cw, cb, wd, g2, b2):
    n = x1.shape[0]
    nt = n // TILE

    def body(x_ref, wg_ref, wu_ref, cw_ref, cb_ref, wd_ref, g_ref, be_ref, gp_ref, up_ref, r_ref, x2_ref, halo_s, act_s):
        @pl.when(pl.program_id(0) == 0)
        def _():
            halo_s[...] = jnp.zeros_like(halo_s)

        x1 = x_ref[...]
        xb = _mx(x1)
        for c in range(D_FF // FF_CHUNK):
            cs = slice(c * FF_CHUNK, (c + 1) * FF_CHUNK)
            gp = _dot(xb, wg_ref[:, cs])
            up = _dot(xb, wu_ref[:, cs])
            gp_ref[:, cs] = gp
            up_ref[:, cs] = up
            _, _, gc = _ffn_conv(gp, halo_s[:, cs], cw_ref, cb_ref, cs)
            halo_s[:, cs] = gp[TILE - 8:]
            act_s[:, cs] = _mx(gc * _sigmoid(gc) * up)
        r2 = ALPHA * x1 + _dot(act_s[...], wd_ref[...])
        r_ref[...] = r2
        xhat, _ = _ln_stats(r2)
        x2_ref[...] = xhat * g_ref[...] + be_ref[...]

    r = functools.partial(_rows_spec, n_tiles=nt)
    return _call(
        body, "ffn_fwd", nt,
        [r(TILE, D)] + [_whole()] * 7,
        [r(TILE, D_FF), r(TILE, D_FF), r(TILE, D), r(TILE, D)],
        [_sds((n, D_FF)), _sds((n, D_FF)), _sds((n, D)), _sds((n, D))],
        scratch=[pltpu.VMEM((8, D_FF), F32), pltpu.VMEM((TILE, D_FF), MXU)],
    )(x1, wg, wu, cw, cb, wd, g2, b2)


def _ffn_bwd_down(dx2, r2, g2, gp, up, cw, cb, wd_t):
    n = dx2.shape[0]
    nt = n // TILE

    def body(dx_ref, r_ref, g_ref, gp_ref, gh_ref, up_ref, cw_ref, cb_ref, wd_ref,
             dr_ref, dgp_ref, dup_ref, dwd_ref, dcw0, dcw1, dcw2, dcb_ref, dg_ref, db_ref, next_s):
        first_tile = pl.program_id(0) == nt - 1

        @pl.when(pl.program_id(0) == 0)
        def _():
            for ref in (dwd_ref, dcw0, dcw1, dcw2, dcb_ref, dg_ref, db_ref, next_s):
                ref[...] = jnp.zeros_like(ref)

        dx2 = dx_ref[...]
        xhat, rstd = _ln_stats(r_ref[...])
        dg_ref[...] += _sum0(dx2 * xhat)
        db_ref[...] += _sum0(dx2)
        dr2 = _ln_bwd(dx2, g_ref[...], xhat, rstd)
        dr_ref[...] = dr2
        dfb = _mx(dr2)
        for c in range(D_FF // FF_CHUNK):
            cs = slice(c * FF_CHUNK, (c + 1) * FF_CHUNK)
            gp = gp_ref[:, cs]
            up = up_ref[:, cs]
            s1, s2, gc = _ffn_conv(gp, jnp.where(first_tile, 0.0, gh_ref[:, cs]), cw_ref, cb_ref, cs)
            sg = _sigmoid(gc)
            silu = gc * sg
            dact = _dot(dfb, wd_ref[:, cs])
            dwd_ref[cs, :] += _dot_tn(_mx(silu * up), dfb)
            dup_ref[:, cs] = _mx(dact * silu)
            dgc = dact * up * (sg + silu * (1.0 - sg))
            dcb_ref[:, cs] += _sum0(dgc)
            dcw2[:, cs] += _sum0(dgc * gp)
            dcw1[:, cs] += _sum0(dgc * s1)
            dcw0[:, cs] += _sum0(dgc * s2)
            ext = jnp.concatenate([dgc, next_s[:, cs]], axis=0)
            dgp_ref[:, cs] = _mx(cw_ref[2:3, cs] * dgc + cw_ref[1:2, cs] * _shift_up(ext, 1, TILE)
                                 + cw_ref[0:1, cs] * _shift_up(ext, 2, TILE))
            next_s[:, cs] = dgc[:8]

    r = functools.partial(_rows_spec, n_tiles=nt, reverse=True)
    vff = _sds((1, D_FF))
    return _call(
        body, "ffn_bwd_down", nt,
        [r(TILE, D), r(TILE, D), _whole(), r(TILE, D_FF), _halo_spec(D_FF, TILE, nt, reverse=True), r(TILE, D_FF), _whole(), _whole(),
         _whole()],
        [r(TILE, D), r(TILE, D_FF), r(TILE, D_FF)] + [_whole()] * 7,
        [_sds((n, D)), _sds((n, D_FF), MXU), _sds((n, D_FF), MXU), _sds((D_FF, D)), vff, vff, vff, vff, _sds((1, D)), _sds((1, D))],
        scratch=[pltpu.VMEM((8, D_FF), F32)],
    )(dx2, r2, g2, gp, gp, up, cw, cb, wd_t)


def _ffn_bwd_dx(dr2, dgp, dup, r1, g1, wg_t, wu_t):
    n = dr2.shape[0]
    rows = TILE_BIG
    nt = n // rows

    def body(dr2_ref, dgp_ref, dup_ref, r_ref, g_ref, wg_ref, wu_ref, dr1_ref, dg_ref, db_ref):
        @pl.when(pl.program_id(0) == 0)
        def _():
            for ref in (dg_ref, db_ref):
                ref[...] = jnp.zeros_like(ref)

        dx1 = ALPHA * dr2_ref[...] + _dot(dgp_ref[...], wg_ref[...]) + _dot(dup_ref[...], wu_ref[...])
        xhat, rstd = _ln_stats(r_ref[...])
        dg_ref[...] += _sum0(dx1 * xhat)
        db_ref[...] += _sum0(dx1)
        dr1_ref[...] = _ln_bwd(dx1, g_ref[...], xhat, rstd)

    r = functools.partial(_rows_spec, n_tiles=nt)
    return _call(
        body, "ffn_bwd_dx", nt,
        [r(rows, D), r(rows, D_FF), r(rows, D_FF), r(rows, D), _whole(), _whole(), _whole()],
        [r(rows, D), _whole(), _whole()],
        [_sds((n, D)), _sds((1, D)), _sds((1, D))],
    )(dr2, dgp, dup, r1, g1, wg_t, wu_t)


def _ffn_bwd_dw(x1, dgp, dup):
    n = x1.shape[0]
    rows = TILE_BIG
    nt = n // rows

    def body(x_ref, dgp_ref, dup_ref, dwg_ref, dwu_ref):
        @pl.when(pl.program_id(0) == 0)
        def _():
            for ref in (dwg_ref, dwu_ref):
                ref[...] = jnp.zeros_like(ref)

        xb = _mx(x_ref[...])
        for c in range(D_FF // FF_CHUNK):
            cs = slice(c * FF_CHUNK, (c + 1) * FF_CHUNK)
            dwg_ref[cs, :] += _dot_tn(dgp_ref[:, cs], xb)
            dwu_ref[cs, :] += _dot_tn(dup_ref[:, cs], xb)

    r = functools.partial(_rows_spec, n_tiles=nt)
    return _call(
        body, "ffn_bwd_dw", nt,
        [r(rows, D), r(rows, D_FF), r(rows, D_FF)], [_whole(), _whole()], [_sds((D_FF, D)), _sds((D_FF, D))],
    )(x1, dgp, dup)


def _loss_head(y, target):
    n = y.shape[0]
    nt = n // TILE_WIDE

    def body(y_ref, t_ref, loss_ref, dy_ref):
        @pl.when(pl.program_id(0) == 0)
        def _():
            loss_ref[...] = jnp.zeros_like(loss_ref)

        e = y_ref[...] - t_ref[...]
        dy_ref[...] = e * (1.0 / D)
        loss_ref[...] += _sum0(jnp.sum(e * e, axis=1, keepdims=True)) * (0.5 / D)

    r = functools.partial(_rows_spec, n_tiles=nt)
    return _call(body, "loss_head", nt, [r(TILE_WIDE, D), r(TILE_WIDE, D)], [_whole(), r(TILE_WIDE, D)],
                 [_sds((1, 1)), _sds((n, D))])(y, target)


def _place():
    x, y, c = lax.axis_index("x"), lax.axis_index("y"), lax.axis_index("c")
    return x, y, c, 4 * x + 2 * y + c


def _peer(x, y, c, k):
    px, py, pc = x ^ ((k >> 2) & 1), y ^ ((k >> 1) & 1), c ^ (k & 1)
    return (px, py, pc), 4 * px + 2 * py + pc


def _all_gather(name, blocks, small):
    srcs = list(blocks) + [small]
    n = len(srcs)
    out_shapes = [_sds((a.shape[0], N_DEV * a.shape[1], LANES), a.dtype) for a in blocks] + [_sds((N_DEV,) + small.shape, small.dtype)]

    def body(*refs):
        src_refs, out_refs = refs[:n], refs[n:2 * n]
        send_sems, recv_sems, local_sems = refs[2 * n:]
        x, y, c, me = _place()

        def landing(a, slot):
            if a == n - 1:
                return out_refs[a].at[slot]
            r = src_refs[a].shape[1]
            return out_refs[a].at[:, pl.ds(slot * r, r), :]

        def remote(a, k, slot):
            peer, _ = _peer(x, y, c, k)
            return pltpu.make_async_remote_copy(
                src_ref=src_refs[a], dst_ref=landing(a, slot), send_sem=send_sems.at[a * N_DEV + k],
                recv_sem=recv_sems.at[a * N_DEV + k], device_id=peer, device_id_type=pl.DeviceIdType.MESH)

        mine = [pltpu.make_async_copy(src_refs[a], landing(a, me), local_sems.at[a]) for a in range(n)]
        sends = [remote(a, k, me) for a in range(n) for k in range(1, N_DEV)]
        for cp in mine + sends:
            cp.start()
        for a in range(n):
            for k in range(1, N_DEV):
                remote(a, k, _peer(x, y, c, k)[1]).wait_recv()
        for cp in sends:
            cp.wait_send()
        for cp in mine:
            cp.wait()

    any_space = pl.BlockSpec(memory_space=pl.ANY)
    return pl.pallas_call(
        body, name=name, out_shape=out_shapes, in_specs=[any_space] * n, out_specs=[any_space] * n,
        scratch_shapes=[pltpu.SemaphoreType.DMA((n * N_DEV,)), pltpu.SemaphoreType.DMA((n * N_DEV,)), pltpu.SemaphoreType.DMA((n,))],
    )(*srcs)


_HBM = pl.BlockSpec(memory_space=pltpu.HBM)
_SEM = pl.BlockSpec(memory_space=pltpu.SEMAPHORE)
_EFFECT = pltpu.SideEffectType.DATAFLOW_SIDE_EFFECTING


def _in_hbm(a):
    return pltpu.with_memory_space_constraint(a, pltpu.HBM)


def _scatter_start(name, srcs):
    ns = len(srcs)
    rows_a = [a.shape[0] // N_DEV for a in srcs]
    offs = [sum(rows_a[:a]) for a in range(ns)]
    total = sum(rows_a)

    def body(*refs):
        src_refs, land_ref, send_sems, recv_sems, token = refs[:ns], refs[ns], refs[ns + 1], refs[ns + 2], refs[-1]
        x, y, c, me = _place()
        for a in range(ns):
            pltpu.make_async_copy(src_refs[a].at[pl.ds(me * rows_a[a], rows_a[a]), :],
                                  land_ref.at[me, pl.ds(offs[a], rows_a[a]), :], send_sems.at[0]).start()
        for k in range(1, N_DEV):
            peer, peer_slot = _peer(x, y, c, k)
            for a in range(ns):
                pltpu.make_async_remote_copy(
                    src_ref=src_refs[a].at[pl.ds(peer_slot * rows_a[a], rows_a[a]), :],
                    dst_ref=land_ref.at[me, pl.ds(offs[a], rows_a[a]), :], send_sem=send_sems.at[k], recv_sem=recv_sems.at[k],
                    device_id=peer, device_id_type=pl.DeviceIdType.MESH).start()
        token[...] = jnp.zeros_like(token)

    landing = lax.empty((N_DEV, total, LANES), F32)
    out = pl.pallas_call(
        body, name=name,
        out_shape=(pltpu.SemaphoreType.DMA((N_DEV,)), pltpu.SemaphoreType.DMA((N_DEV,)), *[pltpu.HBM(a.shape, a.dtype) for a in srcs],
                   pltpu.HBM(landing.shape, F32), _sds((8, 128))),
        in_specs=[_HBM] * (ns + 1), out_specs=(_SEM, _SEM, *[_HBM] * (ns + 1), pl.BlockSpec(memory_space=pltpu.VMEM)),
        input_output_aliases={a: 2 + a for a in range(ns + 1)},
        compiler_params=pltpu.CompilerParams(has_side_effects=_EFFECT),
    )(*[_in_hbm(a) for a in srcs], _in_hbm(landing))
    return out[0], out[1], out[2:2 + ns], out[2 + ns], out[-1]


def _scatter_wait(name, send_sems, recv_sems, srcs, landing, after):
    ns = len(srcs)

    def body(*refs):
        land_ref, send_ref, recv_ref = refs[ns], refs[ns + 1], refs[ns + 2]
        x, y, c, me = _place()
        pltpu.make_async_copy(land_ref.at[me], land_ref.at[me], send_ref.at[0]).wait()
        for k in range(1, N_DEV):
            peer, peer_slot = _peer(x, y, c, k)
            slot = pltpu.make_async_remote_copy(
                src_ref=land_ref.at[me], dst_ref=land_ref.at[peer_slot], send_sem=send_ref.at[k], recv_sem=recv_ref.at[k],
                device_id=peer, device_id_type=pl.DeviceIdType.MESH)
            slot.wait_send()
            slot.wait_recv()

    out = pl.pallas_call(
        body, name=name, out_shape=(*[pltpu.HBM(a.shape, a.dtype) for a in srcs], pltpu.HBM(landing.shape, landing.dtype)),
        in_specs=[_HBM] * (ns + 1) + [_SEM, _SEM, pl.BlockSpec(memory_space=pl.ANY)], out_specs=[_HBM] * (ns + 1),
        input_output_aliases={a: a for a in range(ns + 1)},
        compiler_params=pltpu.CompilerParams(has_side_effects=_EFFECT),
    )(*srcs, landing, send_sems, recv_sems, after)
    return out[:ns], out[ns]


def _gather_start(name, blocks):
    n = len(blocks)

    def body(*refs):
        src_refs, land_refs, send_sems, recv_sems, token = refs[:n], refs[n:2 * n], refs[2 * n], refs[2 * n + 1], refs[-1]
        x, y, c, me = _place()
        for a in range(n):
            r = src_refs[a].shape[1]
            mine = land_refs[a].at[:, pl.ds(me * r, r), :]
            pltpu.make_async_copy(src_refs[a], mine, send_sems.at[a * N_DEV]).start()
            for k in range(1, N_DEV):
                pltpu.make_async_remote_copy(
                    src_ref=src_refs[a], dst_ref=mine, send_sem=send_sems.at[a * N_DEV + k],
                    recv_sem=recv_sems.at[a * N_DEV + k], device_id=_peer(x, y, c, k)[0], device_id_type=pl.DeviceIdType.MESH).start()
        token[...] = jnp.zeros_like(token)

    wholes = [lax.empty((a.shape[0], N_DEV * a.shape[1], LANES), a.dtype) for a in blocks]
    out = pl.pallas_call(
        body, name=name,
        out_shape=(pltpu.SemaphoreType.DMA((n * N_DEV,)), pltpu.SemaphoreType.DMA((n * N_DEV,)),
                   *[pltpu.HBM(a.shape, a.dtype) for a in blocks + wholes], _sds((8, 128))),
        in_specs=[_HBM] * (2 * n), out_specs=(_SEM, _SEM, *[_HBM] * (2 * n), pl.BlockSpec(memory_space=pltpu.VMEM)),
        input_output_aliases={a: 2 + a for a in range(2 * n)},
        compiler_params=pltpu.CompilerParams(has_side_effects=_EFFECT),
    )(*[_in_hbm(a) for a in blocks + wholes])
    return out[0], out[1], out[2:2 + n], out[2 + n:2 + 2 * n], out[-1]


def _gather_wait(name, send_sems, recv_sems, blocks, wholes, after):
    n = len(blocks)

    def body(*refs):
        src_refs, land_refs, send_ref, recv_ref = refs[:n], refs[n:2 * n], refs[2 * n], refs[2 * n + 1]
        x, y, c, me = _place()
        for a in range(n):
            r = src_refs[a].shape[1]
            pltpu.make_async_copy(src_refs[a], land_refs[a].at[:, pl.ds(me * r, r), :], send_ref.at[a * N_DEV]).wait()
            for k in range(1, N_DEV):
                peer, peer_slot = _peer(x, y, c, k)
                cp = pltpu.make_async_remote_copy(
                    src_ref=src_refs[a], dst_ref=land_refs[a].at[:, pl.ds(peer_slot * r, r), :], send_sem=send_ref.at[a * N_DEV + k],
                    recv_sem=recv_ref.at[a * N_DEV + k], device_id=peer, device_id_type=pl.DeviceIdType.MESH)
                cp.wait_send()
                cp.wait_recv()

    return pl.pallas_call(
        body, name=name, out_shape=tuple(pltpu.HBM(a.shape, a.dtype) for a in list(blocks) + list(wholes)),
        in_specs=[_HBM] * (2 * n) + [_SEM, _SEM, pl.BlockSpec(memory_space=pl.ANY)], out_specs=[_HBM] * (2 * n),
        input_output_aliases={a: a for a in range(2 * n)},
        compiler_params=pltpu.CompilerParams(has_side_effects=_EFFECT),
    )(*blocks, *wholes, send_sems, recv_sems, after)


def _reduce_adamw(parts, w, m, v, tile_rows=PACK_TILE):
    rows = w.shape[0]
    nt = rows // tile_rows
    slots = parts.shape[0]
    c1 = 1.0 - ADAM_B1 ** ADAM_STEP
    c2 = 1.0 - ADAM_B2 ** ADAM_STEP

    def body(p_ref, w_ref, m_ref, v_ref, g_out, d_out, m_out, v_out):
        g = p_ref[0]
        for s in range(1, slots):
            g = g + p_ref[s]
        m_new = ADAM_B1 * m_ref[...] + (1.0 - ADAM_B1) * g
        v_new = ADAM_B2 * v_ref[...] + (1.0 - ADAM_B2) * (g * g)
        g_out[...] = g
        m_out[...] = m_new
        v_out[...] = v_new
        d_out[...] = -ADAM_LR * ((m_new / c1) / (jnp.sqrt(v_new / c2) + ADAM_EPS) + ADAM_WD * w_ref[...])

    r = _rows_spec(tile_rows, LANES, nt)
    out = _sds((rows, LANES))
    return _call(
        body, "reduce_adamw", nt,
        [pl.BlockSpec((slots, tile_rows, LANES), lambda i: (0, i, 0)), r, r, r], [r, r, r, r], [out, out, out, out],
    )(parts, w, m, v)


def _pack_rows(a, lead=0):
    head = a.shape[:lead]
    flat = a.reshape(head + (-1,))
    size = flat.shape[-1]
    rows = -(-size // (16 * LANES)) * 16
    flat = jnp.pad(flat, [(0, 0)] * lead + [(0, rows * LANES - size)])
    return flat.reshape(head + (rows, LANES))


def _packed_rows(shape):
    return -(-math.prod(shape) // (16 * LANES)) * 16


def _to_blocks(full, axis):
    l, a, b = full.shape
    if axis == 2:
        return full.reshape(l, a, N_DEV, b // N_DEV).transpose(2, 0, 1, 3)
    return full.reshape(l, N_DEV, a // N_DEV, b).transpose(1, 0, 2, 3)


def _from_blocks(blocks, axis):
    _, l, a, b = blocks.shape
    if axis == 2:
        return blocks.transpose(1, 2, 0, 3).reshape(l, a, N_DEV * b)
    return blocks.transpose(1, 0, 2, 3).reshape(l, N_DEV * a, b)


def _row_form(shard, transposed):
    return shard.transpose(0, 2, 1) if transposed else shard


def _me():
    return 4 * lax.axis_index("x") + 2 * lax.axis_index("y") + lax.axis_index("c")


def _both_forms(names, wholes, layer):
    out = {}
    for name, w in zip(names, wholes):
        t = dict(BIG)[name]
        out[name + '_t' if t else name] = w[layer]
        out[name if t else name + '_t'] = w[layer].T
    return out


def _gather_weights(local):
    segs, meta = [], []
    for name in SMALL_SHARDED:
        blk = local[name]
        if name in GATHER_F32:
            bits = lax.bitcast_convert_type(blk, MXU)
        else:
            bits = _mx(blk)
        seg = _pack_rows(bits)
        meta.append((name, bits.shape, seg.shape[0]))
        segs.append(seg)
    blocks = {name: _mx(_row_form(local[name], t)) for name, t in BIG}
    mix, ffn = PARTS['mix'], PARTS['ffn']
    *first, gathered = _all_gather("gather_weights", [blocks[n][:1] for n in mix], jnp.concatenate(segs, axis=0))
    flights = {'ffn0': _gather_start("gather_ffn0_start", [blocks[n][:1] for n in ffn]),
               'later': _gather_start("gather_later_start", [blocks[n][1:] for n in mix + ffn])}
    out, lo = {}, 0
    for name, bits_shape, rows in meta:
        seg = gathered[:, lo:lo + rows].reshape(N_DEV, -1)[:, :math.prod(bits_shape)].reshape((N_DEV,) + bits_shape)
        if name in GATHER_F32:
            seg = lax.bitcast_convert_type(seg, F32)
        out[name] = _from_blocks(seg, SHARD_AXIS[name])
        lo += rows
    ready = {(0, 'mix'): _both_forms(mix, first, 0)}

    def landed(flight, names, after):
        send_sems, recv_sems, mine, wholes, _ = flights[flight]
        return _gather_wait(f"gather_{flight}_wait", send_sems, recv_sems, mine, wholes, after)[len(names):]

    def big_weights(l, part, after):
        if (l, part) not in ready and l == 0:
            ready[(0, 'ffn')] = _both_forms(ffn, landed('ffn0', ffn, after), 0)
        elif (l, part) not in ready:
            wholes = landed('later', mix + ffn, after)
            for j in range(1, DEPTH):
                forms = _both_forms(mix + ffn, wholes, j - 1)
                for p, names in PARTS.items():
                    ready[(j, p)] = {k: forms[k] for n in names for k in (n, n + '_t')}
        return ready[(l, part)]

    return out, big_weights, flights['ffn0'][-1][0, 0] + flights['later'][-1][0, 0]


def _s5_discretize(a_re, a_im, log_dt, b_re, b_im):
    lam_re = jnp.minimum(a_re, -1e-4)
    lam_im = a_im
    dt = jnp.exp(log_dt)[:, None]
    decay = jnp.exp(dt * lam_re)
    ang = dt * lam_im
    abar_re = decay * jnp.cos(ang)
    abar_im = decay * jnp.sin(ang)
    den = jnp.square(lam_re) + jnp.square(lam_im)
    nr = abar_re - 1.0
    ni = abar_im
    coef_re = (nr * lam_re + ni * lam_im) / den
    coef_im = (ni * lam_re - nr * lam_im) / den
    bbar_re = coef_re[..., None] * b_re - coef_im[..., None] * b_im
    bbar_im = coef_re[..., None] * b_im + coef_im[..., None] * b_re
    return abar_re, abar_im, bbar_re, bbar_im


def _complex_powers(ar, ai, count):
    exponents = jnp.arange(1, count + 1).reshape((count,) + (1,) * ar.ndim)
    pr = jnp.ones((count,) + ar.shape, F32)
    pi = jnp.zeros_like(pr)
    sr, si = ar, ai
    for b in range(count.bit_length()):
        bit = ((exponents >> b) & 1) == 1
        fr, fi = jnp.where(bit, sr[None], 1.0), jnp.where(bit, si[None], 0.0)
        pr, pi = pr * fr - pi * fi, pr * fi + pi * fr
        sr, si = sr * sr - si * si, 2.0 * sr * si
    return pr, pi


_EYE16 = functools.partial(jnp.eye, 16, dtype=F32)


def _s5_params(p):
    disc, disc_vjp = jax.vjp(jax.vmap(_s5_discretize), p['s5_a_re'], p['s5_a_im'], p['s5_log_dt'], p['s5_b_re'], p['s5_b_im'])
    abar_re, abar_im, bbar_re, bbar_im = disc
    pw_re, pw_im = _complex_powers(abar_re.reshape(DEPTH, N_STATE), abar_im.reshape(DEPTH, N_STATE), SEG)
    sp_re, sp_im = _complex_powers(pw_re[SEG - 1], pw_im[SEG - 1], N_SEG - 1)
    one, zero = jnp.ones((1, DEPTH, N_STATE), F32), jnp.zeros((1, DEPTH, N_STATE), F32)
    seg_re = jnp.concatenate([one, sp_re], axis=0)
    seg_im = jnp.concatenate([zero, sp_im], axis=0)
    doubling = [0, 1, 3]
    blank = jnp.zeros((5, DEPTH, N_STATE), F32)
    tables = {
        'pw_re': pw_re, 'pw_im': pw_im,
        'dbl_re': jnp.concatenate([jnp.stack([sp_re[k] for k in doubling]), blank], axis=0),
        'dbl_im': jnp.concatenate([jnp.stack([sp_im[k] for k in doubling]), blank], axis=0),
        'seg_re': seg_re, 'seg_im': seg_im, 'segr_re': seg_re[::-1], 'segr_im': seg_im[::-1],
    }
    weights = {
        'b_re': _mx(jnp.einsum('lgpc,gh->lgchp', bbar_re, _EYE16()).reshape(DEPTH, D_S5, N_STATE)),
        'b_im': _mx(jnp.einsum('lgpc,gh->lgchp', bbar_im, _EYE16()).reshape(DEPTH, D_S5, N_STATE)),
        'c_re': _mx(jnp.einsum('lgcp,gh->lgphc', p['s5_c_re'], _EYE16()).reshape(DEPTH, N_STATE, D_S5)),
        'c_im': _mx(jnp.einsum('lgcp,gh->lgphc', p['s5_c_im'], _EYE16()).reshape(DEPTH, N_STATE, D_S5)),
    }
    src = (jnp.arange(TILE) % N_SEG) * SEG + jnp.arange(TILE) // N_SEG
    perm = (src[:, None] == jnp.arange(TILE)[None, :]).astype(MXU)
    layers = []
    for l in range(DEPTH):
        prm = {k: v[:, l] for k, v in tables.items()}
        prm.update({k: v[l] for k, v in weights.items()})
        prm.update({'perm': perm, 'perm_t': perm.T, 'd': p['s5_d'][l][None, :], 'glu_w': p['s5_glu_w'][l],
                    'glu_b': p['s5_glu_b'][l][None, :]})
        layers.append(prm)
    return layers, disc_vjp


def _lru_params(p, l):
    eye4 = jnp.eye(4, dtype=F32)
    return {
        'conv_w': p['lru_conv_w'][l], 'conv_b': p['lru_conv_b'][l][None, :],
        'wx': _mx(jnp.einsum('hij,hk->hikj', p['lru_wx'][l], eye4).reshape(D_LRU, D_LRU)),
        'wa': _mx(jnp.einsum('hij,hk->hikj', p['lru_wa'][l], eye4).reshape(D_LRU, D_LRU)),
        'bx': p['lru_bx'][l][None, :], 'ba': p['lru_ba'][l][None, :],
        'sp': jax.nn.softplus(-p['lru_a_param'][l])[None, :],
    }


def _rope_tables(n):
    inv_freq = ROPE_THETA ** (-jnp.arange(0, 64, 2, dtype=F32) / 64)
    ang = jnp.arange(n, dtype=F32)[:, None] * inv_freq[None, :]
    cos, sin = jnp.cos(ang), jnp.sin(ang)
    return jnp.concatenate([cos, cos, cos, cos], axis=1), jnp.concatenate([-sin, sin, -sin, sin], axis=1)


def _sink_cols(sinks):
    nb = TILE_Q // ATTN_BLOCK
    per_unit = sinks.reshape(4, 2).T
    return jnp.broadcast_to(per_unit[:, None, :, None, None], (2, nb, 4, ATTN_BLOCK, 128)).reshape(2, nb * 4 * ATTN_BLOCK, 128)


def _local_step(x, target, p, big_weights=None, emit_grads=None):
    if big_weights is None:
        big_weights = lambda l, part, after: {k: p[k][l] for name in PARTS[part] for k in (name, name + '_t')}
    if emit_grads is None:
        emit_grads = lambda l, part, grads: 0.0
    n = x.shape[0]
    cos_t, sin_t = _rope_tables(n)
    s5_layers, s5_vjp = _s5_params(p)
    row = lambda a: a[None, :]
    saved = []
    h = x
    for l in range(DEPTH):
        s = {'x0': h}
        bw = s['bw'] = dict(big_weights(l, 'mix', h))
        s['q'], k, v, s['u'], s['xr'], s['gate'] = _inproj_fwd(h, bw['w_in'], row(p['b_in'][l]), cos_t, sin_t)
        no_keys = jnp.zeros((ATTN_BLOCK, D_KV), MXU)
        s['k'], s['v'] = jnp.concatenate([no_keys, k], axis=0), jnp.concatenate([no_keys, v], axis=0)
        s['sinks'] = _sink_cols(p['attn_sinks'][l])
        s['ya'] = _attn_fwd(s['q'], s['k'], s['v'], s['sinks'])
        s['s5'] = s5_layers[l]
        s['lru'] = _lru_params(p, l)
        s['ys'], s['s5_cr'], s['s5_ci'] = _s5_fwd(s['u'], s['s5'])
        s['yl'], s['lru_c'] = _lru_fwd(s['xr'], s['gate'], s['lru'])
        s['mix'], s['r1'], s['x1'] = _mixout_fwd(s['ya'], s['ys'], s['yl'], h, row(p['mix_norm_g'][l]), bw['w_out'],
                                                 row(p['b_out'][l]), row(p['ln1_g'][l]), row(p['ln1_b'][l]))
        bw.update(big_weights(l, 'ffn', s['x1']))
        s['gp'], s['up'], s['r2'], h = _ffn_fwd(s['x1'], bw['ffn_w_gate'], bw['ffn_w_up'], p['ffn_conv_w'][l],
                                                row(p['ffn_conv_b'][l]), bw['ffn_w_down'], row(p['ln2_g'][l]), row(p['ln2_b'][l]))
        saved.append(s)
    loss, dh = _loss_head(h, target)
    placed = 0.0

    grads = {name: [None] * DEPTH for name in WEIGHTS}
    d_disc = [None] * DEPTH
    for l in reversed(range(DEPTH)):
        s = saved[l]
        g = {}
        (dr2, dgp, dup, g['ffn_w_down'], cw0, cw1, cw2, dcb, dg2, db2) = _ffn_bwd_down(
            dh, s['r2'], row(p['ln2_g'][l]) + placed, s['gp'], s['up'], p['ffn_conv_w'][l], row(p['ffn_conv_b'][l]),
            s['bw']['ffn_w_down_t'])
        g['ffn_conv_w'] = jnp.concatenate([cw0, cw1, cw2], axis=0)
        g['ffn_conv_b'], g['ln2_g'], g['ln2_b'] = dcb[0], dg2[0], db2[0]
        dr1, dg1, db1 = _ffn_bwd_dx(dr2, dgp, dup, s['r1'], row(p['ln1_g'][l]), s['bw']['ffn_w_gate_t'], s['bw']['ffn_w_up_t'])
        g['ffn_w_gate'], g['ffn_w_up'] = _ffn_bwd_dw(s['x1'], dgp, dup)
        g['ln1_g'], g['ln1_b'] = dg1[0], db1[0]
        placed = emit_grads(l, 'ffn', [g[name] for name in PARTS['ffn']])
        dya, dys, dyl, g['w_out'], dbo, dgm = _mixout_bwd(dr1, s['mix'], s['ya'], s['ys'], s['yl'],
                                                         row(p['mix_norm_g'][l]) + placed, s['bw']['w_out_t'])
        g['b_out'], g['mix_norm_g'] = dbo[0], dgm[0]

        du, dbr, dbi, dcr, dci, dar, dai, dd, g['s5_glu_w'], dgb = _s5_bwd(s['u'], dys, s['s5_cr'], s['s5_ci'], s['s5'])
        dxr, dgate, lw0, lw1, lw2, lw3, lcb, dwx, dwa, dbx, dba, dsp = _lru_bwd(s['xr'], s['gate'], dyl, s['lru_c'], s['lru'])
        g['lru_conv_w'] = jnp.concatenate([lw0, lw1, lw2, lw3], axis=0)
        g['lru_conv_b'], g['lru_bx'], g['lru_ba'] = lcb[0], dbx[0], dba[0]
        g['lru_wx'] = jnp.einsum('hihj->hij', dwx.reshape(4, 64, 4, 64))
        g['lru_wa'] = jnp.einsum('hihj->hij', dwa.reshape(4, 64, 4, 64))
        g['lru_a_param'] = -dsp[0] * jax.nn.sigmoid(-p['lru_a_param'][l])

        g['s5_c_re'] = jnp.einsum('gpgc->gcp', dcr.reshape(16, 64, 16, 16))
        g['s5_c_im'] = jnp.einsum('gpgc->gcp', dci.reshape(16, 64, 16, 16))
        g['s5_d'], g['s5_glu_b'] = dd[0], dgb[0]
        d_disc[l] = (dar.reshape(16, 64), dai.reshape(16, 64), jnp.einsum('gcgp->gpc', dbr.reshape(16, 16, 16, 64)),
                     jnp.einsum('gcgp->gpc', dbi.reshape(16, 16, 16, 64)))

        dq, dk, dv, dsink = _attn_bwd(s['q'], s['k'], s['v'], s['sinks'], s['ya'], dya)
        g['attn_sinks'] = dsink[:, 0]
        dh, g['w_in'], dbin = _inproj_bwd(dq, dk[ATTN_BLOCK:], dv[ATTN_BLOCK:], du, dxr, dgate, cos_t, sin_t, s['x0'], dr1,
                                          s['bw']['w_in_t'])
        g['b_in'] = dbin[0]
        placed = emit_grads(l, 'mix', [g[name] for name in PARTS['mix']])
        for name in g:
            grads[name][l] = g[name]
    big = dict(BIG)
    out = {name: grads[name] if name in big else jnp.stack(grads[name]) for name in WEIGHTS if name not in S5_DISC}
    out.update(zip(S5_DISC, s5_vjp(tuple(jnp.stack([d_disc[l][i] for l in range(DEPTH)]) for i in range(4)))))
    return loss, dh, out


def kernel(x, w_in, b_in, attn_sinks, s5_a_re, s5_a_im, s5_b_re, s5_b_im, s5_c_re, s5_c_im, s5_d, s5_log_dt, s5_glu_w, s5_glu_b, lru_conv_w, lru_conv_b, lru_wx, lru_bx, lru_wa, lru_ba, lru_a_param, mix_norm_g, w_out, b_out, ln1_g, ln1_b, ffn_w_gate, ffn_w_up, ffn_conv_w, ffn_conv_b, ffn_w_down, ln2_g, ln2_b, loss_target, m_w_in, m_b_in, m_attn_sinks, m_s5_a_re, m_s5_a_im, m_s5_b_re, m_s5_b_im, m_s5_c_re, m_s5_c_im, m_s5_d, m_s5_log_dt, m_s5_glu_w, m_s5_glu_b, m_lru_conv_w, m_lru_conv_b, m_lru_wx, m_lru_bx, m_lru_wa, m_lru_ba, m_lru_a_param, m_mix_norm_g, m_w_out, m_b_out, m_ln1_g, m_ln1_b, m_ffn_w_gate, m_ffn_w_up, m_ffn_conv_w, m_ffn_conv_b, m_ffn_w_down, m_ln2_g, m_ln2_b, v_w_in, v_b_in, v_attn_sinks, v_s5_a_re, v_s5_a_im, v_s5_b_re, v_s5_b_im, v_s5_c_re, v_s5_c_im, v_s5_d, v_s5_log_dt, v_s5_glu_w, v_s5_glu_b, v_lru_conv_w, v_lru_conv_b, v_lru_wx, v_lru_bx, v_lru_wa, v_lru_ba, v_lru_a_param, v_mix_norm_g, v_w_out, v_b_out, v_ln1_g, v_ln1_b, v_ffn_w_gate, v_ffn_w_up, v_ffn_conv_w, v_ffn_conv_b, v_ffn_w_down, v_ln2_g, v_ln2_b):
    given = dict(locals())
    whole = {name: given[name] for name in WEIGHTS if name not in dict(BIG)}
    small_whole, big_weights, placed = _gather_weights({name: given[name] for name in SHARDED})
    whole.update(small_whole)
    whole['b_in'] = whole['b_in'] + placed
    in_flight = {}

    def emit_grads(l, part, grads):
        in_flight[(l, part)] = _scatter_start(f"grads_start_{part}{l}", grads)
        return in_flight[(l, part)][-1][0, 0]

    loss, grad_x, grads = _local_step(x[0], loss_target[0], whole, big_weights, emit_grads)
    total = lax.psum(loss[0, 0], ("x", "y", "c"))
    return (total, grad_x[None], *_update(given, grads, in_flight, grad_x))


def _update(given, grads, in_flight, after):
    local_w = {name: given[name] for name in WEIGHTS}
    me = _me()
    outs = {}

    shard_rows = sum(_packed_rows(local_w[name].shape) for name in SMALL_SHARDED)
    rep_pad = -sum(_packed_rows(local_w[name].shape) for name in REPLICATED) % PACK_TILE

    def packed_rep(arrays):
        return jnp.concatenate([_pack_rows(arrays[name]) for name in REPLICATED] + [jnp.zeros((rep_pad, LANES), F32)], axis=0)

    rep_grads = packed_rep(grads)
    chunk = rep_grads.shape[0] // N_DEV
    small = jnp.concatenate(
        [_pack_rows(_to_blocks(grads[name], SHARD_AXIS[name]), lead=1) for name in SMALL_SHARDED]
        + [rep_grads.reshape(N_DEV, chunk, LANES)], axis=1)
    small_rows = shard_rows + chunk
    small_flight = _scatter_start("grads_start_small", [small.reshape(N_DEV * small_rows, LANES)])

    def summed(name, flight, rows_of, packed, after):
        send_sems, recv_sems, srcs, landing, _ = flight
        _, landing = _scatter_wait(name, send_sems, recv_sems, srcs, landing, after)
        return _reduce_adamw(landing, *packed, tile_rows=sum(rows_of) // 4)

    forms = {part: [(name, dict(BIG)[name], _row_form(local_w[name], dict(BIG)[name]).shape[1]) for name in names]
             for part, names in PARTS.items()}

    def big_part(key, after):
        l, part = key
        packed = [jnp.concatenate([_row_form(given[prefix + name], t)[l] for name, t, _ in forms[part]], axis=0)
                  for prefix in ('', 'm_', 'v_')]
        outs[key] = summed(f"grads_wait_{part}{l}", in_flight[key], [r for _, _, r in forms[part]], packed, after)
        return outs[key][0]

    *earlier, last = in_flight
    for key in earlier:
        after = big_part(key, after)
    rep_state = [packed_rep({name: given[prefix + name] for name in REPLICATED}) for prefix in ('', 'm_', 'v_')]
    small_state = [jnp.concatenate([_pack_rows(given[prefix + name]) for name in SMALL_SHARDED]
                                   + [lax.dynamic_slice_in_dim(rep, me * chunk, chunk, axis=0)], axis=0)
                   for prefix, rep in zip(('', 'm_', 'v_'), rep_state)]
    small_outs = summed("grads_wait_small", small_flight, [small_rows], small_state, after)
    rep_sum = _all_gather("gather_small_grads", [], small_outs[0][shard_rows:])[0].reshape(N_DEV * chunk, LANES)
    rep_outs = _reduce_adamw(rep_sum[None], *rep_state, tile_rows=N_DEV * chunk // 4)
    big_part(last, rep_outs[0])

    def unpack(i):
        res = {}
        for part in PARTS:
            lo = 0
            for name, t, r in forms[part]:
                res[name] = _row_form(jnp.stack([outs[(l, part)][i][lo:lo + r] for l in range(DEPTH)]), t)
                lo += r
        for names, packed in ((SMALL_SHARDED, small_outs[i]), (REPLICATED, rep_outs[i])):
            lo = 0
            for name in names:
                shape = local_w[name].shape
                res[name] = packed[lo:lo + _packed_rows(shape)].reshape(-1)[:math.prod(shape)].reshape(shape)
                lo += _packed_rows(shape)
        return [res[name] for name in WEIGHTS]

    return (*unpack(0), *unpack(1), *unpack(2), *unpack(3))
```
